```python
import math
import jax
import jax.numpy as jnp
from jax import lax
import numpy as np

D_MODEL = 1024
BATCH = 4
SEQ = 8192
DEPTH = 2

HEAD_DIM = 64
Q_BLOCK = 128
RMS_EPS = 1e-6
D_FF = 4 * D_MODEL
NEG_INF = -1e30
FORCE_SCORE = 1e6

SB_HEADS = D_MODEL // (2 * HEAD_DIM)
SB_WIDTH = SB_HEADS * HEAD_DIM
DIFF_HEADS = D_MODEL // (4 * HEAD_DIM)
DIFF_WIDTH = DIFF_HEADS * 2 * HEAD_DIM
EVEN_IN = 3 * SB_WIDTH + 3 * DIFF_WIDTH
EVEN_OUT = SB_WIDTH + DIFF_WIDTH

NSA_HEADS = D_MODEL // HEAD_DIM
NSA_GROUPS = 4
NSA_HPG = NSA_HEADS // NSA_GROUPS
NSA_Q_WIDTH = NSA_HEADS * HEAD_DIM
NSA_KV_WIDTH = NSA_GROUPS * HEAD_DIM
CMP_LEN = 32
CMP_STRIDE = 16
CMP_HIDDEN = 4 * HEAD_DIM
SEL_LEN = 64
SEL_TOPK = 16
WINDOW = 512
N_GATES = 3
ODD_IN = NSA_Q_WIDTH + 6 * NSA_KV_WIDTH + N_GATES * NSA_HEADS

N_EVEN = (DEPTH + 1) // 2
N_ODD = DEPTH // 2

kernel_name = "hybrid_sb_diff_nsa_trunk"


def rmsnorm(x, g):
    x32 = x.astype(jnp.float32)
    y = x32 * lax.rsqrt(jnp.mean(x32 * x32, axis=-1, keepdims=True) + RMS_EPS)
    return (y * g.astype(jnp.float32)).astype(x.dtype)


def alibi_slopes(n_heads):
    return jnp.exp2(-8.0 * (jnp.arange(n_heads, dtype=jnp.float32) + 1.0) / n_heads)


def sweep_query_blocks(block_fn, seq):
    out = lax.map(block_fn, jnp.arange(seq // Q_BLOCK))
    out = jnp.moveaxis(out, 0, 1)
    return out.reshape((out.shape[0], seq) + out.shape[3:])


def stick_breaking_attention(q, k, v):
    S = q.shape[1]
    scale = q.shape[-1] ** -0.5
    kpos = jnp.arange(S)

    def block(i):
        t0 = i * Q_BLOCK
        qb = lax.dynamic_slice_in_dim(q, t0, Q_BLOCK, axis=1)
        tpos = t0 + jnp.arange(Q_BLOCK)
        past = kpos[None, :] < tpos[:, None]
        z = jnp.einsum("bqhd,bkhd->bhqk", qb, k).astype(jnp.float32) * scale
        log_beta = jax.nn.log_sigmoid(z)
        log_rem = jnp.where(past, jax.nn.log_sigmoid(-z), 0.0)
        tail = lax.cumsum(log_rem, axis=3, reverse=True) - log_rem
        w = jnp.where(past, jnp.exp(log_beta + tail), 0.0)
        return jnp.einsum("bhqk,bkhd->bqhd", w.astype(v.dtype), v)

    return sweep_query_blocks(block, S)


def differential_attention(q, k, v, lam, slopes):
    S = q.shape[1]
    scale = q.shape[-1] ** -0.5
    kpos = jnp.arange(S)

    def block(i):
        t0 = i * Q_BLOCK
        qb = lax.dynamic_slice_in_dim(q, t0, Q_BLOCK, axis=1)
        tpos = t0 + jnp.arange(Q_BLOCK)
        dist = (tpos[:, None] - kpos[None, :]).astype(jnp.float32)
        s = jnp.einsum("bqhcd,bkhcd->bhcqk", qb, k).astype(jnp.float32) * scale
        s = jnp.where(dist >= 0, s - slopes[None, :, None, None, None] * dist, NEG_INF)
        p = jax.nn.softmax(s, axis=-1)
        a = p[:, :, 0] - lam * p[:, :, 1]
        return jnp.einsum("bhqk,bkhe->bqhe", a.astype(v.dtype), v)

    return sweep_query_blocks(block, S)


def sb_diff_mixer(h, w_in, lam_q1, lam_k1, lam_q2, lam_k2, subln, w_out, layer):
    B, S, _ = h.shape
    proj = h @ w_in
    sb_q, sb_k, sb_v, df_q, df_k, df_v = jnp.split(proj, 6, axis=-1)
    sb_shape = (B, S, SB_HEADS, HEAD_DIM)
    o_sb = stick_breaking_attention(sb_q.reshape(sb_shape), sb_k.reshape(sb_shape),
                                    sb_v.reshape(sb_shape))
    lambda_init = 0.8 - 0.6 * math.exp(-0.3 * layer)
    lam = (jnp.exp(jnp.sum(lam_q1.astype(jnp.float32) * lam_k1.astype(jnp.float32)))
           - jnp.exp(jnp.sum(lam_q2.astype(jnp.float32) * lam_k2.astype(jnp.float32)))
           + lambda_init)
    qk_shape = (B, S, DIFF_HEADS, 2, HEAD_DIM)
    o_df = differential_attention(df_q.reshape(qk_shape), df_k.reshape(qk_shape),
                                  df_v.reshape(B, S, DIFF_HEADS, 2 * HEAD_DIM),
                                  lam, alibi_slopes(DIFF_HEADS))
    o_df = rmsnorm(o_df, subln) * (1.0 - lambda_init)
    o = jnp.concatenate([o_sb.reshape(B, S, SB_WIDTH), o_df.reshape(B, S, DIFF_WIDTH)], axis=-1)
    return o @ w_out


def compress(t, pos, w1, w2):
    B, S, G, Dh = t.shape
    ratio = CMP_LEN // CMP_STRIDE
    n_chunks = S // CMP_STRIDE
    n_cmp = n_chunks - ratio + 1
    chunks = t.reshape(B, n_chunks, CMP_STRIDE, G, Dh)
    blocks = jnp.concatenate([chunks[:, r:r + n_cmp] for r in range(ratio)], axis=2)
    blocks = blocks + pos[None, None, :, None, :]
    flat = jnp.moveaxis(blocks, 3, 2).reshape(B, n_cmp, G, CMP_LEN * Dh)
    return jax.nn.gelu(flat @ w1) @ w2


def nsa_attention(q, kc, vc, ks, vs, kw, vw, gates, slopes):
    B, S, H, Dh = q.shape
    G = kc.shape[2]
    hpg = H // G
    n_cmp = kc.shape[1]
    n_sel = S // SEL_LEN
    topk = min(SEL_TOPK, n_sel)
    scale = Dh ** -0.5
    slope_g = slopes.reshape(G, hpg)
    qg = q.reshape(B, S, G, hpg, Dh)
    gg = gates.reshape(B, S, G, hpg, N_GATES)
    cmp_start = jnp.arange(n_cmp) * CMP_STRIDE
    cmp_end = cmp_start + CMP_LEN - 1
    sel_ids = jnp.arange(n_sel)
    sel_start = sel_ids * SEL_LEN
    overlap = ((cmp_start[:, None] < sel_start[None, :] + SEL_LEN)
               & (sel_start[None, :] <= cmp_end[:, None])).astype(jnp.float32)
    ks_bg = jnp.moveaxis(ks.reshape(B, n_sel, SEL_LEN, G, Dh), 3, 1)
    vs_bg = jnp.moveaxis(vs.reshape(B, n_sel, SEL_LEN, G, Dh), 3, 1)
    gather_blocks = jax.vmap(jax.vmap(lambda blk, idx: blk[idx]))
    pad = ((0, 0), (WINDOW, 0), (0, 0), (0, 0))
    kw_pad = jnp.pad(kw, pad)
    vw_pad = jnp.pad(vw, pad)
    win_off = jnp.arange(Q_BLOCK + WINDOW) - WINDOW
    sel_off = jnp.arange(SEL_LEN)

    def block(i):
        t0 = i * Q_BLOCK
        tpos = t0 + jnp.arange(Q_BLOCK)
        qb = lax.dynamic_slice_in_dim(qg, t0, Q_BLOCK, axis=1)
        gb = lax.dynamic_slice_in_dim(gg, t0, Q_BLOCK, axis=1)
        dc = (tpos[:, None] - cmp_end[None, :]).astype(jnp.float32)
        valid_c = dc >= 0
        sc = jnp.einsum("bqgrd,bngd->bgrqn", qb, kc).astype(jnp.float32) * scale
        sc = jnp.where(valid_c, sc - slope_g[None, :, :, None, None] * dc, NEG_INF)
        pc = jax.nn.softmax(sc, axis=-1) * jnp.any(valid_c, axis=-1)[:, None].astype(jnp.float32)
        o_cmp = jnp.einsum("bgrqn,bngd->bqgrd", pc.astype(vc.dtype), vc)
        imp = jnp.einsum("bgrqn,ns->bgqs", pc, overlap)
        cur = tpos // SEL_LEN
        forced = ((sel_ids[None, :] == 0) | (sel_ids[None, :] == cur[:, None])
                  | (sel_ids[None, :] == cur[:, None] - 1))
        imp = jnp.where(forced, FORCE_SCORE, imp)
        imp = jnp.where(sel_ids[None, :] <= cur[:, None], imp, -1.0)
        top_val, top_idx = lax.top_k(imp, topk)
        k_sel = gather_blocks(ks_bg, top_idx)
        v_sel = gather_blocks(vs_bg, top_idx)
        d_sel = (tpos[None, None, :, None, None]
                 - (top_idx[..., None] * SEL_LEN + sel_off)).astype(jnp.float32)
        valid_s = (d_sel >= 0) & (top_val >= 0)[..., None]
        ss = jnp.einsum("bqgrd,bgqkld->bgrqkl", qb, k_sel).astype(jnp.float32) * scale
        ss = jnp.where(valid_s[:, :, None],
                       ss - slope_g[None, :, :, None, None, None] * d_sel[:, :, None], NEG_INF)
        ps = jax.nn.softmax(ss.reshape(ss.shape[:4] + (-1,)), axis=-1).reshape(ss.shape)
        o_sel = jnp.einsum("bgrqkl,bgqkld->bqgrd", ps.astype(vs.dtype), v_sel)
        kwb = lax.dynamic_slice_in_dim(kw_pad, t0, Q_BLOCK + WINDOW, axis=1)
        vwb = lax.dynamic_slice_in_dim(vw_pad, t0, Q_BLOCK + WINDOW, axis=1)
        kpos_w = t0 + win_off
        dw = tpos[:, None] - kpos_w[None, :]
        valid_w = (dw >= 0) & (dw < WINDOW) & (kpos_w[None, :] >= 0)
        sw = jnp.einsum("bqgrd,bkgd->bgrqk", qb, kwb).astype(jnp.float32) * scale
        sw = jnp.where(valid_w, sw - slope_g[None, :, :, None, None] * dw.astype(jnp.float32), NEG_INF)
        pw = jax.nn.softmax(sw, axis=-1)
        o_win = jnp.einsum("bgrqk,bkgd->bqgrd", pw.astype(vw.dtype), vwb)
        return gb[..., 0:1] * o_cmp + gb[..., 1:2] * o_sel + gb[..., 2:3] * o_win

    out = sweep_query_blocks(block, S)
    return out.reshape(B, S, H * Dh)


def nsa_mixer(h, w_in, cmp_pos_k, cmp_k_w1, cmp_k_w2, cmp_pos_v, cmp_v_w1, cmp_v_w2, w_out):
    B, S, _ = h.shape
    proj = h @ w_in
    cuts = [NSA_Q_WIDTH + j * NSA_KV_WIDTH for j in range(7)]
    q, kc, vc, ks, vs, kw, vw, g = jnp.split(proj, cuts, axis=-1)
    kv_shape = (B, S, NSA_GROUPS, HEAD_DIM)
    kc = compress(kc.reshape(kv_shape), cmp_pos_k, cmp_k_w1, cmp_k_w2)
    vc = compress(vc.reshape(kv_shape), cmp_pos_v, cmp_v_w1, cmp_v_w2)
    gates = jax.nn.sigmoid(g.astype(jnp.float32)).astype(h.dtype).reshape(B, S, NSA_HEADS, N_GATES)
    o = nsa_attention(q.reshape(B, S, NSA_HEADS, HEAD_DIM), kc, vc,
                      ks.reshape(kv_shape), vs.reshape(kv_shape),
                      kw.reshape(kv_shape), vw.reshape(kv_shape),
                      gates, alibi_slopes(NSA_HEADS))
    return o @ w_out


def squared_relu_mlp(h, w1, w2):
    return jnp.square(jax.nn.relu(h @ w1)) @ w2


def setup_inputs(seed: int = 0) -> dict:
    key = jax.random.key(seed)
    ks = jax.random.split(key, 21)

    def nrm(k, shape, scale):
        return jax.random.normal(k, shape, jnp.float32) * scale

    return {
        "x": nrm(ks[0], (BATCH, SEQ, D_MODEL), 1.0),
        "attn_norm": 1.0 + nrm(ks[1], (DEPTH, D_MODEL), 0.02),
        "mlp_norm": 1.0 + nrm(ks[2], (DEPTH, D_MODEL), 0.02),
        "final_norm": 1.0 + nrm(ks[3], (D_MODEL,), 0.02),
        "ev_w_in": nrm(ks[4], (N_EVEN, D_MODEL, EVEN_IN), D_MODEL ** -0.5),
        "ev_lam_q1": nrm(ks[5], (N_EVEN, HEAD_DIM), 0.1),
        "ev_lam_k1": nrm(ks[6], (N_EVEN, HEAD_DIM), 0.1),
        "ev_lam_q2": nrm(ks[7], (N_EVEN, HEAD_DIM), 0.1),
        "ev_lam_k2": nrm(ks[8], (N_EVEN, HEAD_DIM), 0.1),
        "ev_subln": 1.0 + nrm(ks[9], (N_EVEN, 2 * HEAD_DIM), 0.02),
        "ev_w_out": nrm(ks[10], (N_EVEN, EVEN_OUT, D_MODEL), EVEN_OUT ** -0.5),
        "od_w_in": nrm(ks[11], (N_ODD, D_MODEL, ODD_IN), D_MODEL ** -0.5),
        "od_cmp_pos_k": nrm(ks[12], (N_ODD, CMP_LEN, HEAD_DIM), 0.1),
        "od_cmp_k_w1": nrm(ks[13], (N_ODD, CMP_LEN * HEAD_DIM, CMP_HIDDEN), (CMP_LEN * HEAD_DIM) ** -0.5),
        "od_cmp_k_w2": nrm(ks[14], (N_ODD, CMP_HIDDEN, HEAD_DIM), CMP_HIDDEN ** -0.5),
        "od_cmp_pos_v": nrm(ks[15], (N_ODD, CMP_LEN, HEAD_DIM), 0.1),
        "od_cmp_v_w1": nrm(ks[16], (N_ODD, CMP_LEN * HEAD_DIM, CMP_HIDDEN), (CMP_LEN * HEAD_DIM) ** -0.5),
        "od_cmp_v_w2": nrm(ks[17], (N_ODD, CMP_HIDDEN, HEAD_DIM), CMP_HIDDEN ** -0.5),
        "od_w_out": nrm(ks[18], (N_ODD, NSA_Q_WIDTH, D_MODEL), NSA_Q_WIDTH ** -0.5),
        "mlp_w1": nrm(ks[19], (DEPTH, D_MODEL, D_FF), D_MODEL ** -0.5),
        "mlp_w2": nrm(ks[20], (DEPTH, D_FF, D_MODEL), D_FF ** -0.5),
    }


def reference(x, attn_norm, mlp_norm, final_norm, ev_w_in, ev_lam_q1, ev_lam_k1, ev_lam_q2,
              ev_lam_k2, ev_subln, ev_w_out, od_w_in, od_cmp_pos_k, od_cmp_k_w1, od_cmp_k_w2,
              od_cmp_pos_v, od_cmp_v_w1, od_cmp_v_w2, od_w_out, mlp_w1, mlp_w2):
    for layer in range(DEPTH):
        h = rmsnorm(x, attn_norm[layer])
        if layer % 2 == 0:
            e = layer // 2
            mix = sb_diff_mixer(h, ev_w_in[e], ev_lam_q1[e], ev_lam_k1[e], ev_lam_q2[e],
                                ev_lam_k2[e], ev_subln[e], ev_w_out[e], layer)
        else:
            o = layer // 2
            mix = nsa_mixer(h, od_w_in[o], od_cmp_pos_k[o], od_cmp_k_w1[o], od_cmp_k_w2[o],
                            od_cmp_pos_v[o], od_cmp_v_w1[o], od_cmp_v_w2[o], od_w_out[o])
        x = x + mix
        x = x + squared_relu_mlp(rmsnorm(x, mlp_norm[layer]), mlp_w1[layer], mlp_w2[layer])
    return rmsnorm(x, final_norm)
```

```python
import functools
import math

import jax
import jax.numpy as jnp
from jax import lax
from jax.experimental import pallas as pl
from jax.experimental.pallas import tpu as pltpu

F32 = jnp.float32
BF16 = jnp.bfloat16

HEAD_DIM = 64
RMS_EPS = 1e-6
NEG_INF = -1e30
FORCE_SCORE = 1e6
NSA_GROUPS = 4
NSA_HPG = 4
CMP_LEN = 32
CMP_STRIDE = 16
SEL_LEN = 64
SEL_TOPK = 16
WINDOW = 512
N_GATES = 3
GATE_PAD = 128

VMEM_LIMIT = 56 * 1024 * 1024


def _params(n_parallel, n_arbitrary=0):
    sem = ("parallel",) * n_parallel + ("arbitrary",) * n_arbitrary
    return pltpu.CompilerParams(dimension_semantics=sem, vmem_limit_bytes=VMEM_LIMIT)


def _rms(x, g):
    ms = jnp.mean(x * x, axis=-1, keepdims=True)
    return x * lax.rsqrt(ms + RMS_EPS) * g


def _dot(a, b):
    return jnp.dot(a, b, preferred_element_type=F32)


def _dot_nt(a, b):
    return lax.dot_general(a, b, (((1,), (1,)), ((), ())), preferred_element_type=F32)


def _split_bf16(x):
    hi = x.astype(BF16)
    lo = (x - hi.astype(F32)).astype(BF16)
    return hi, lo


def _norm_proj_kernel(x_ref, g_ref, w_ref, o_ref, *gate_ref, n_main, col_chunk):
    xn = _rms(x_ref[...], g_ref[...]).astype(BF16)
    for c in range(n_main // col_chunk):
        sl = slice(c * col_chunk, (c + 1) * col_chunk)
        o_ref[:, sl] = _dot(xn, w_ref[:, sl]).astype(o_ref.dtype)
    if gate_ref:
        logits = _dot(xn, w_ref[:, n_main:n_main + GATE_PAD])
        gate_ref[0][...] = jax.nn.sigmoid(logits)


def norm_proj(x2d, g, w_bf16, n_main, with_gates, tm=512, col_chunk=512):
    T, D = x2d.shape
    n_w = w_bf16.shape[1]
    out_shape = [jax.ShapeDtypeStruct((T, n_main), BF16)]
    out_specs = [pl.BlockSpec((tm, n_main), lambda i: (i, 0))]
    if with_gates:
        out_shape.append(jax.ShapeDtypeStruct((T, GATE_PAD), F32))
        out_specs.append(pl.BlockSpec((tm, GATE_PAD), lambda i: (i, 0)))
    res = pl.pallas_call(
        functools.partial(_norm_proj_kernel, n_main=n_main, col_chunk=col_chunk),
        grid=(T // tm,),
        in_specs=[pl.BlockSpec((tm, D), lambda i: (i, 0)),
                  pl.BlockSpec((1, D), lambda i: (0, 0)),
                  pl.BlockSpec((D, n_w), lambda i: (0, 0))],
        out_specs=out_specs,
        out_shape=out_shape,
        compiler_params=_params(1),
        name="norm_proj",
    )(x2d, g.reshape(1, D), w_bf16)
    return res if with_gates else res[0]


def _sb_kernel(q_ref, k_ref, v_ref, o_ref, acc_ref, carry_ref, *, tq, tk):
    qi = pl.program_id(2)
    lane = lax.broadcasted_iota(jnp.int32, (tq, 2 * HEAD_DIM), 1)
    q = q_ref[0] * 0.125
    zero = jnp.zeros_like(q)
    q_heads = (jnp.where(lane < HEAD_DIM, q, zero), jnp.where(lane >= HEAD_DIM, q, zero))
    jj = lax.broadcasted_iota(jnp.int32, (tk, tk), 0)
    ss = lax.broadcasted_iota(jnp.int32, (tk, tk), 1)
    suffix = jnp.where(jj > ss, 1.0, 0.0).astype(BF16)
    row = lax.broadcasted_iota(jnp.int32, (tq, tk), 0)
    col = lax.broadcasted_iota(jnp.int32, (tq, tk), 1)
    past_diag = col < row

    acc_ref[...] = jnp.zeros_like(acc_ref)
    carry_ref[...] = jnp.zeros_like(carry_ref)

    def chunk(kc, masked):
        start = pl.multiple_of(kc * tk, tk)
        k = k_ref[0, pl.ds(start, tk), :]
        v = v_ref[0, pl.ds(start, tk), :]
        for h in range(2):
            z = _dot_nt(q_heads[h], k)
            softplus = jnp.maximum(z, 0.0) + jnp.log1p(jnp.exp(-jnp.abs(z)))
            log_rem = -softplus
            if masked:
                log_rem = jnp.where(past_diag, log_rem, 0.0)
            hi, lo = _split_bf16(log_rem)
            tail = _dot(hi, suffix) + _dot(lo, suffix)
            w = jnp.exp((z - softplus) + tail + carry_ref[h])
            if masked:
                w = jnp.where(past_diag, w, 0.0)
            carry_ref[h] += jnp.sum(log_rem, axis=1, keepdims=True)
            acc_ref[h] += _dot(w.astype(BF16), v)

    chunk(qi, True)

    def body(j, c):
        chunk(qi - 1 - j, False)
        return c

    lax.fori_loop(0, qi, body, 0)
    o_ref[0] = jnp.where(lane < HEAD_DIM, acc_ref[0], acc_ref[1]).astype(o_ref.dtype)


def sb_attention(proj, t=256):
    B, S, _ = proj.shape
    n_pairs = 4
    return pl.pallas_call(
        functools.partial(_sb_kernel, tq=t, tk=t),
        grid=(B, n_pairs, S // t),
        in_specs=[pl.BlockSpec((1, t, 128), lambda b, p, i: (b, i, p)),
                  pl.BlockSpec((1, S, 128), lambda b, p, i: (b, 0, n_pairs + p)),
                  pl.BlockSpec((1, S, 128), lambda b, p, i: (b, 0, 2 * n_pairs + p))],
        out_specs=pl.BlockSpec((1, t, 128), lambda b, p, i: (b, i, p)),
        out_shape=jax.ShapeDtypeStruct((B, S, n_pairs * 128), BF16),
        scratch_shapes=[pltpu.VMEM((2, t, 128), F32), pltpu.VMEM((2, t, 1), F32)],
        compiler_params=_params(3),
        name="sb_attention",
    )(proj, proj, proj)


def _diff_kernel(slopes_ref, lam_ref, q_ref, k_ref, v_ref, subln_ref, o_ref,
                 m_ref, l_ref, acc_ref, *, tq, tk, out_scale, lambda_init):
    h = pl.program_id(1)
    qi = pl.program_id(2)
    slope = slopes_ref[h]
    lane = lax.broadcasted_iota(jnp.int32, (tq, 2 * HEAD_DIM), 1)
    q = q_ref[0] * 0.125
    zero = jnp.zeros_like(q)
    q_comp = (jnp.where(lane < HEAD_DIM, q, zero), jnp.where(lane >= HEAD_DIM, q, zero))
    row = lax.broadcasted_iota(jnp.int32, (tq, tk), 0)
    col = lax.broadcasted_iota(jnp.int32, (tq, tk), 1)
    rel = (row - col).astype(F32)
    bias_rel = slope * rel

    m_ref[...] = jnp.full_like(m_ref, NEG_INF)
    l_ref[...] = jnp.zeros_like(l_ref)
    acc_ref[...] = jnp.zeros_like(acc_ref)

    def chunk(kc, masked):
        start = pl.multiple_of(kc * tk, tk)
        k = k_ref[0, pl.ds(start, tk), :]
        v = v_ref[0, pl.ds(start, tk), :]
        tile_bias = slope * ((qi - kc) * tq).astype(F32)
        for c in range(2):
            s = _dot_nt(q_comp[c], k) - bias_rel - tile_bias
            if masked:
                s = jnp.where(rel >= 0, s, NEG_INF)
            m_old = m_ref[c]
            m_new = jnp.maximum(m_old, jnp.max(s, axis=1, keepdims=True))
            alpha = jnp.exp(m_old - m_new)
            p = jnp.exp(s - m_new)
            l_ref[c] = alpha * l_ref[c] + jnp.sum(p, axis=1, keepdims=True)
            acc_ref[c] = alpha * acc_ref[c] + _dot(p.astype(BF16), v)
            m_ref[c] = m_new

    def body(j, c):
        chunk(j, False)
        return c

    lax.fori_loop(0, qi, body, 0)
    chunk(qi, True)

    lam_terms = lam_ref[...]
    lam = (jnp.exp(jnp.sum(lam_terms[0:1] * lam_terms[1:2], axis=1, keepdims=True))
           - jnp.exp(jnp.sum(lam_terms[2:3] * lam_terms[3:4], axis=1, keepdims=True))
           + lambda_init)
    o = acc_ref[0] / l_ref[0] - lam * (acc_ref[1] / l_ref[1])
    o_ref[0] = (_rms(o, subln_ref[...]) * out_scale).astype(o_ref.dtype)


def diff_attention(proj, lam_rows, subln, slopes, layer, t=256):
    B, S, _ = proj.shape
    n_heads = 4
    lambda_init = 0.8 - 0.6 * math.exp(-0.3 * layer)
    smem = pl.BlockSpec(memory_space=pltpu.SMEM)
    return pl.pallas_call(
        functools.partial(_diff_kernel, tq=t, tk=t, out_scale=1.0 - lambda_init,
                          lambda_init=lambda_init),
        grid=(B, n_heads, S // t),
        in_specs=[smem,
                  pl.BlockSpec((4, HEAD_DIM), lambda b, h, i: (0, 0)),
                  pl.BlockSpec((1, t, 128), lambda b, h, i: (b, i, 12 + h)),
                  pl.BlockSpec((1, S, 128), lambda b, h, i: (b, 0, 16 + h)),
                  pl.BlockSpec((1, S, 128), lambda b, h, i: (b, 0, 20 + h)),
                  pl.BlockSpec((1, 128), lambda b, h, i: (0, 0))],
        out_specs=pl.BlockSpec((1, t, 128), lambda b, h, i: (b, i, h)),
        out_shape=jax.ShapeDtypeStruct((B, S, n_heads * 128), BF16),
        scratch_shapes=[pltpu.VMEM((2, t, 1), F32), pltpu.VMEM((2, t, 1), F32),
                        pltpu.VMEM((2, t, 128), F32)],
        compiler_params=_params(3),
        name="diff_attention",
    )(slopes, lam_rows, proj, proj, proj, subln.reshape(1, 128))


def _post_kernel(*refs, n_mix, ff_chunk, final):
    mix_refs = refs[:n_mix]
    x_ref = refs[n_mix]
    wo_refs = refs[n_mix + 1:2 * n_mix + 1]
    g_ref, w1_ref, w2_ref = refs[2 * n_mix + 1:2 * n_mix + 4]
    gf_ref = refs[2 * n_mix + 4] if final else None
    o_ref = refs[-1]
    x = x_ref[...]
    for m_ref, wo_ref in zip(mix_refs, wo_refs):
        x = x + _dot(m_ref[...], wo_ref[...])
    hn = _rms(x, g_ref[...]).astype(BF16)
    acc = x
    for f in range(w1_ref.shape[1] // ff_chunk):
        sl = slice(f * ff_chunk, (f + 1) * ff_chunk)
        hid = jnp.maximum(_dot(hn, w1_ref[:, sl]), 0.0)
        acc = acc + _dot((hid * hid).astype(BF16), w2_ref[sl, :])
    if final:
        acc = _rms(acc, gf_ref[...])
    o_ref[...] = acc


def post_block(mixes, x2d, w_outs, g_mlp, w1, w2, g_final=None, tm=512, ff_chunk=1024):
    T, D = x2d.shape
    n_mix = len(mixes)
    final = g_final is not None
    const = lambda i: (0, 0)
    in_specs = [pl.BlockSpec((tm, m.shape[1]), lambda i: (i, 0)) for m in mixes]
    in_specs.append(pl.BlockSpec((tm, D), lambda i: (i, 0)))
    in_specs += [pl.BlockSpec(w.shape, const) for w in w_outs]
    in_specs += [pl.BlockSpec((1, D), const), pl.BlockSpec(w1.shape, const), pl.BlockSpec(w2.shape, const)]
    args = list(mixes) + [x2d] + list(w_outs) + [g_mlp.reshape(1, D), w1, w2]
    if final:
        in_specs.append(pl.BlockSpec((1, D), const))
        args.append(g_final.reshape(1, D))
    return pl.pallas_call(
        functools.partial(_post_kernel, n_mix=n_mix, ff_chunk=ff_chunk, final=final),
        grid=(T // tm,),
        in_specs=in_specs,
        out_specs=pl.BlockSpec((tm, D), lambda i: (i, 0)),
        out_shape=jax.ShapeDtypeStruct((T, D), F32),
        compiler_params=_params(1),
        name="post_block",
    )(*args)


def _compress_kernel(c_ref, pos_ref, w1_ref, w2_ref, o_ref):
    half = w1_ref.shape[1] // 2
    chunks = c_ref[0, 0]
    pos = jnp.broadcast_to(pos_ref[0], (8, 2 * half)).astype(BF16)
    first = _dot(chunks, w1_ref[0, :half, :])
    second = _dot(chunks, w1_ref[0, half:, :])
    n_chunks = chunks.shape[0]
    pre = first + pltpu.roll(second, n_chunks - 1, 0) + _dot(pos, w1_ref[0])[0:1]
    hid = jax.nn.gelu(pre)
    o_ref[0, 0] = _dot(hid.astype(BF16), w2_ref[0]).astype(o_ref.dtype)


def compress_kv(chunks, pos_flat, w1, w2):
    _, BG, n_chunks, width = chunks.shape
    hidden = w1.shape[-1]
    return pl.pallas_call(
        _compress_kernel,
        grid=(2, BG),
        in_specs=[pl.BlockSpec((1, 1, n_chunks, width), lambda s, i: (s, i, 0, 0)),
                  pl.BlockSpec((1, 1, 2 * width), lambda s, i: (s, 0, 0)),
                  pl.BlockSpec((1, 2 * width, hidden), lambda s, i: (s, 0, 0)),
                  pl.BlockSpec((1, hidden, HEAD_DIM), lambda s, i: (s, 0, 0))],
        out_specs=pl.BlockSpec((1, 1, n_chunks, HEAD_DIM), lambda s, i: (s, i, 0, 0)),
        out_shape=jax.ShapeDtypeStruct((2, BG, n_chunks, HEAD_DIM), BF16),
        compiler_params=_params(2),
        name="compress_kv",
    )(chunks, pos_flat, w1, w2)


def _stack_heads(q):
    return jnp.concatenate([q[:, r * HEAD_DIM:(r + 1) * HEAD_DIM] for r in range(NSA_HPG)], axis=0)


def _unstack_heads(o, tq):
    return jnp.concatenate([o[r * tq:(r + 1) * tq] for r in range(NSA_HPG)], axis=1)


def _slope_column(slopes_ref, g, tq):
    row = lax.broadcasted_iota(jnp.int32, (NSA_HPG * tq, 1), 0)
    col = jnp.zeros((NSA_HPG * tq, 1), F32)
    for r in range(NSA_HPG):
        col = jnp.where(row // tq == r, slopes_ref[g * NSA_HPG + r], col)
    return col


def _cmp_select_kernel(slopes_ref, q_ref, kc_ref, vc_ref, ov_ref, ocmp_ref, sel_ref, *, tq, n_sel):
    g = pl.program_id(1)
    t0 = pl.program_id(2) * tq
    rows = NSA_HPG * tq
    n_cmp = kc_ref.shape[2]
    slope_col = _slope_column(slopes_ref, g, tq)
    q_rows = _stack_heads(q_ref[0] * 0.125)
    tpos = t0 + lax.broadcasted_iota(jnp.int32, (rows, n_cmp), 0) % tq
    cmp_end = lax.broadcasted_iota(jnp.int32, (rows, n_cmp), 1) * CMP_STRIDE + (CMP_LEN - 1)
    dc = (tpos - cmp_end).astype(F32)
    valid = dc >= 0
    sc = jnp.where(valid, _dot_nt(q_rows, kc_ref[0, 0]) - slope_col * dc, NEG_INF)
    e = jnp.exp(sc - jnp.max(sc, axis=1, keepdims=True))
    any_valid = jnp.where(dc[:, 0:1] >= 0, 1.0, 0.0)
    pc = e / jnp.sum(e, axis=1, keepdims=True) * any_valid
    ocmp_ref[0] = _unstack_heads(_dot(pc.astype(BF16), vc_ref[0, 0]), tq)

    pc_group = pc[0:tq]
    for r in range(1, NSA_HPG):
        pc_group = pc_group + pc[r * tq:(r + 1) * tq]
    hi, lo = _split_bf16(pc_group)
    imp = _dot(hi, ov_ref[...]) + _dot(lo, ov_ref[...])

    blk = lax.broadcasted_iota(jnp.int32, (tq, n_sel), 1)
    cur = (t0 + lax.broadcasted_iota(jnp.int32, (tq, n_sel), 0)) // SEL_LEN
    forced = (blk == 0) | (blk == cur) | (blk == cur - 1)
    imp = jnp.where(forced, FORCE_SCORE, imp)
    imp = jnp.where(blk <= cur, imp, -1.0)
    picked = jnp.zeros((tq, n_sel), F32)
    for _ in range(min(SEL_TOPK, n_sel)):
        best = jnp.max(imp, axis=1, keepdims=True)
        first = jnp.min(jnp.where(imp == best, blk, n_sel), axis=1, keepdims=True)
        hit = blk == first
        picked = jnp.where(hit, 1.0, picked)
        imp = jnp.where(hit, -2.0, imp)
    sel_ref[0, 0] = jnp.where(blk <= cur, picked, 0.0).astype(sel_ref.dtype)


def cmp_select(proj, kc, vc, overlap, slopes, tq=128):
    B, S, _ = proj.shape
    n_cmp = kc.shape[2]
    n_sel = S // SEL_LEN
    G = NSA_GROUPS
    smem = pl.BlockSpec(memory_space=pltpu.SMEM)
    return pl.pallas_call(
        functools.partial(_cmp_select_kernel, tq=tq, n_sel=n_sel),
        grid=(B, G, S // tq),
        in_specs=[smem,
                  pl.BlockSpec((1, tq, 256), lambda b, g, i: (b, i, g)),
                  pl.BlockSpec((1, 1, n_cmp, HEAD_DIM), lambda b, g, i: (b, g, 0, 0)),
                  pl.BlockSpec((1, 1, n_cmp, HEAD_DIM), lambda b, g, i: (b, g, 0, 0)),
                  pl.BlockSpec((n_cmp, n_sel), lambda b, g, i: (0, 0))],
        out_specs=[pl.BlockSpec((1, tq, 256), lambda b, g, i: (b, i, g)),
                   pl.BlockSpec((1, 1, tq, n_sel), lambda b, g, i: (b, g, i, 0))],
        out_shape=[jax.ShapeDtypeStruct((B, S, G * 256), F32),
                   jax.ShapeDtypeStruct((B, G, S, n_sel), BF16)],
        compiler_params=_params(3),
        name="cmp_select",
    )(slopes, proj, kc, vc, overlap)


def _online_softmax_step(s, v, m_ref, l_ref, acc_ref):
    m_old = m_ref[...]
    m_new = jnp.maximum(m_old, jnp.max(s, axis=1, keepdims=True))
    alpha = jnp.exp(m_old - m_new)
    p = jnp.exp(s - m_new)
    l_ref[...] = alpha * l_ref[...] + jnp.sum(p, axis=1, keepdims=True)
    acc_ref[...] = alpha * acc_ref[...] + _dot(p.astype(BF16), v)
    m_ref[...] = m_new


def _sel_win_kernel(slopes_ref, q_ref, ks_ref, vs_ref, kw_ref, vw_ref, sel_ref, expand_ref,
                    ocmp_ref, gate_ref, o_ref, m_ref, l_ref, acc_ref, *, tq, tk):
    g = pl.program_id(1)
    qi = pl.program_id(2)
    t0 = qi * tq
    rows = NSA_HPG * tq
    slope_col = _slope_column(slopes_ref, g, tq)
    q_rows = _stack_heads(q_ref[0] * 0.125)
    tpos = (t0 + lax.broadcasted_iota(jnp.int32, (rows, tk), 0) % tq
            - lax.broadcasted_iota(jnp.int32, (rows, tk), 1)).astype(F32)
    sel = sel_ref[0, 0]

    def reset():
        m_ref[...] = jnp.full_like(m_ref, NEG_INF)
        l_ref[...] = jnp.zeros_like(l_ref)
        acc_ref[...] = jnp.zeros_like(acc_ref)

    def sel_chunk(c, carry):
        start = pl.multiple_of(c * tk, tk)
        dist = tpos - (c * tk).astype(F32)
        chosen = _dot(sel, expand_ref[:, pl.ds(start, tk)])
        chosen = jnp.concatenate([chosen] * NSA_HPG, axis=0)
        ok = (chosen > 0.5) & (dist >= 0)
        s = _dot_nt(q_rows, ks_ref[0, 0, pl.ds(start, tk), :])
        s = jnp.where(ok, s - slope_col * dist, NEG_INF)
        _online_softmax_step(s, vs_ref[0, 0, pl.ds(start, tk), :], m_ref, l_ref, acc_ref)
        return carry

    reset()
    lax.fori_loop(0, t0 // tk + 1, sel_chunk, 0)
    o_sel = acc_ref[...] / l_ref[...]

    def win_chunk(c, carry):
        start = pl.multiple_of(c * tk, tk)
        dist = tpos - (c * tk).astype(F32)
        ok = (dist >= 0) & (dist < WINDOW)
        s = _dot_nt(q_rows, kw_ref[0, 0, pl.ds(start, tk), :])
        s = jnp.where(ok, s - slope_col * dist, NEG_INF)
        _online_softmax_step(s, vw_ref[0, 0, pl.ds(start, tk), :], m_ref, l_ref, acc_ref)
        return carry

    reset()
    first_win = jnp.maximum(t0 - (WINDOW - 1), 0) // tk
    lax.fori_loop(first_win, t0 // tk + 1, win_chunk, 0)
    o_win = acc_ref[...] / l_ref[...]

    gates = gate_ref[0, 0]
    o_cmp = ocmp_ref[0]
    outs = []
    for r in range(NSA_HPG):
        rs = slice(r * tq, (r + 1) * tq)
        outs.append(gates[:, 3 * r:3 * r + 1] * o_cmp[:, r * HEAD_DIM:(r + 1) * HEAD_DIM]
                    + gates[:, 3 * r + 1:3 * r + 2] * o_sel[rs]
                    + gates[:, 3 * r + 2:3 * r + 3] * o_win[rs])
    o_ref[0] = jnp.concatenate(outs, axis=1).astype(o_ref.dtype)


def sel_win_attention(proj, ks, vs, kw, vw, sel, expand, o_cmp, gates, slopes, tq=128, tk=128):
    B, S, _ = proj.shape
    G = NSA_GROUPS
    n_sel = S // SEL_LEN
    smem = pl.BlockSpec(memory_space=pltpu.SMEM)
    kv_spec = pl.BlockSpec((1, 1, S, HEAD_DIM), lambda b, g, i: (b, g, 0, 0))
    rows = NSA_HPG * tq
    return pl.pallas_call(
        functools.partial(_sel_win_kernel, tq=tq, tk=tk),
        grid=(B, G, S // tq),
        in_specs=[smem,
                  pl.BlockSpec((1, tq, 256), lambda b, g, i: (b, i, g)),
                  kv_spec, kv_spec, kv_spec, kv_spec,
                  pl.BlockSpec((1, 1, tq, n_sel), lambda b, g, i: (b, g, i, 0)),
                  pl.BlockSpec((n_sel, S), lambda b, g, i: (0, 0)),
                  pl.BlockSpec((1, tq, 256), lambda b, g, i: (b, i, g)),
                  pl.BlockSpec((1, 1, tq, NSA_HPG * N_GATES), lambda b, g, i: (b, g, i, 0))],
        out_specs=pl.BlockSpec((1, tq, 256), lambda b, g, i: (b, i, g)),
        out_shape=jax.ShapeDtypeStruct((B, S, G * 256), BF16),
        scratch_shapes=[pltpu.VMEM((rows, 1), F32), pltpu.VMEM((rows, 1), F32),
                        pltpu.VMEM((rows, HEAD_DIM), F32)],
        compiler_params=_params(3),
        name="sel_win_attention",
    )(slopes, proj, ks, vs, kw, vw, sel, expand, o_cmp, gates)


def _alibi_slopes(n_heads):
    return jnp.exp2(-8.0 * (jnp.arange(n_heads, dtype=F32) + 1.0) / n_heads)


def even_layer_mix(x2d, B, S, norm_g, w_in, lam_q1, lam_k1, lam_q2, lam_k2, subln, layer):
    proj = norm_proj(x2d, norm_g, w_in.astype(BF16), w_in.shape[1], False).reshape(B, S, -1)
    o_sb = sb_attention(proj)
    lam_rows = jnp.stack([lam_q1, lam_k1, lam_q2, lam_k2]).astype(F32)
    o_df = diff_attention(proj, lam_rows, subln.astype(F32), _alibi_slopes(4), layer)
    return o_sb.reshape(B * S, -1), o_df.reshape(B * S, -1)


def odd_layer_mix(x2d, B, S, norm_g, w_in, pos_k, k_w1, k_w2, pos_v, v_w1, v_w2):
    G, Dh = NSA_GROUPS, HEAD_DIM
    q_width = NSA_GROUPS * NSA_HPG * Dh
    kv_width = G * Dh
    n_main = q_width + 6 * kv_width
    pad = n_main + GATE_PAD - w_in.shape[1]
    w_pad = jnp.pad(w_in, ((0, 0), (0, pad))).astype(BF16)
    proj, gates = norm_proj(x2d, norm_g, w_pad, n_main, True)
    proj = proj.reshape(B, S, n_main)

    def group_major(j):
        cols = proj[:, :, q_width + j * kv_width:q_width + (j + 1) * kv_width]
        return cols.reshape(B, S, G, Dh).transpose(0, 2, 1, 3)

    n_chunks = S // CMP_STRIDE
    chunks = jnp.stack([group_major(0), group_major(1)]).reshape(2, B * G, n_chunks, CMP_STRIDE * Dh)
    pos_flat = jnp.stack([pos_k, pos_v]).reshape(2, 1, CMP_LEN * Dh).astype(F32)
    w1 = jnp.stack([k_w1, v_w1]).astype(BF16)
    w2 = jnp.stack([k_w2, v_w2]).astype(BF16)
    cmp = compress_kv(chunks, pos_flat, w1, w2).reshape(2, B, G, n_chunks, Dh)

    n_sel = S // SEL_LEN
    cmp_start = jnp.arange(n_chunks) * CMP_STRIDE
    sel_start = jnp.arange(n_sel) * SEL_LEN
    overlap = ((cmp_start[:, None] < sel_start[None, :] + SEL_LEN)
               & (sel_start[None, :] <= cmp_start[:, None] + CMP_LEN - 1)).astype(BF16)
    expand = (jnp.arange(S)[None, :] // SEL_LEN == jnp.arange(n_sel)[:, None]).astype(BF16)
    slopes = _alibi_slopes(NSA_GROUPS * NSA_HPG)

    o_cmp, sel = cmp_select(proj, cmp[0], cmp[1], overlap, slopes)
    gates_g = gates[:, :G * NSA_HPG * N_GATES].reshape(B, S, G, NSA_HPG * N_GATES).transpose(0, 2, 1, 3)
    o = sel_win_attention(proj, group_major(2), group_major(3), group_major(4), group_major(5),
                          sel, expand, o_cmp, gates_g, slopes)
    return o.reshape(B * S, q_width)


def kernel(x, attn_norm, mlp_norm, final_norm, ev_w_in, ev_lam_q1, ev_lam_k1, ev_lam_q2, ev_lam_k2,
           ev_subln, ev_w_out, od_w_in, od_cmp_pos_k, od_cmp_k_w1, od_cmp_k_w2, od_cmp_pos_v,
           od_cmp_v_w1, od_cmp_v_w2, od_w_out, mlp_w1, mlp_w2):
    B, S, D = x.shape
    depth = attn_norm.shape[0]
    x2d = x.reshape(B * S, D)
    for layer in range(depth):
        idx = layer // 2
        if layer % 2 == 0:
            o_sb, o_df = even_layer_mix(x2d, B, S, attn_norm[layer], ev_w_in[idx], ev_lam_q1[idx],
                                        ev_lam_k1[idx], ev_lam_q2[idx], ev_lam_k2[idx],
                                        ev_subln[idx], layer)
            w_out = ev_w_out[idx].astype(BF16)
            mixes, w_outs = [jnp.concatenate([o_sb, o_df], axis=1)], [w_out]
        else:
            o = odd_layer_mix(x2d, B, S, attn_norm[layer], od_w_in[idx], od_cmp_pos_k[idx],
                              od_cmp_k_w1[idx], od_cmp_k_w2[idx], od_cmp_pos_v[idx],
                              od_cmp_v_w1[idx], od_cmp_v_w2[idx])
            mixes, w_outs = [o], [od_w_out[idx].astype(BF16)]
        g_final = final_norm if layer == depth - 1 else None
        x2d = post_block(mixes, x2d, w_outs, mlp_norm[layer], mlp_w1[layer].astype(BF16),
                         mlp_w2[layer].astype(BF16), g_final)
    return x2d.reshape(B, S, D)
```

```python
import functools
import math

import jax
import jax.numpy as jnp
from jax import lax
from jax.experimental import pallas as pl
from jax.experimental.pallas import tpu as pltpu

F32 = jnp.float32
BF16 = jnp.bfloat16

HEAD_DIM = 64
RMS_EPS = 1e-6
NEG_INF = -1e30
FORCE_SCORE = 1e6
NSA_GROUPS = 4
NSA_HPG = 4
CMP_LEN = 32
CMP_STRIDE = 16
SEL_LEN = 64
SEL_TOPK = 16
WINDOW = 512
N_GATES = 3
GATE_PAD = 128
SEL_DROP = 2.0 ** 24
BF16_EXACT_INT = 256
V_ROWS_64 = 80
V_ROWS_128 = 144

VMEM_LIMIT = 56 * 1024 * 1024


def _params(n_parallel):
    return pltpu.CompilerParams(dimension_semantics=("parallel",) * n_parallel,
                                vmem_limit_bytes=VMEM_LIMIT)


def _rms(x, g):
    ms = jnp.mean(x * x, axis=-1, keepdims=True)
    return x * lax.rsqrt(ms + RMS_EPS) * g


def _dot(a, b):
    return jnp.dot(a, b, preferred_element_type=F32)


def _dot_nt(a, b):
    return lax.dot_general(a, b, (((1,), (1,)), ((), ())), preferred_element_type=F32)


def _split_bf16(x):
    hi = x.astype(BF16)
    lo = (x - hi.astype(F32)).astype(BF16)
    return hi, lo


def _slope_pieces(slopes, width):
    s1 = slopes.astype(BF16)
    r1 = slopes - s1.astype(F32)
    s2 = r1.astype(BF16)
    s3 = (r1 - s2.astype(F32)).astype(BF16)
    cols = jnp.stack([s1, s2, s3, s1, s2, s3], axis=1)
    return jnp.pad(cols, ((0, 0), (0, width - cols.shape[1])))


def _key_position_columns(n, tk, width):
    j = jnp.arange(n) % tk
    a = (j // BF16_EXACT_INT) * BF16_EXACT_INT
    b = j % BF16_EXACT_INT
    cols = jnp.stack([a, a, a, b, b, b], axis=1).astype(BF16)
    return jnp.pad(cols, ((0, 0), (0, width - cols.shape[1])))


def _with_ones_row(v_t, rows):
    d = v_t.shape[-2]
    ones = jnp.ones(v_t.shape[:-2] + (1, v_t.shape[-1]), v_t.dtype)
    pad = jnp.zeros(v_t.shape[:-2] + (rows - d - 1, v_t.shape[-1]), v_t.dtype)
    return jnp.concatenate([v_t, ones, pad], axis=-2)


def _softmax_step(s_t, vt_chunk, kappa, m_ref, acc_ref):
    m_old = m_ref[...]
    m_new = jnp.maximum(m_old, jnp.max(s_t, axis=0, keepdims=True) + kappa)
    alpha = jnp.exp(m_old - m_new)
    p = jnp.exp(s_t - (m_new - kappa))
    acc_ref[...] = alpha * acc_ref[...] + _dot(vt_chunk, p.astype(BF16))
    m_ref[...] = m_new


def _norm_proj_kernel(x_ref, g_ref, w_ref, o_ref, *gate_ref, n_main, col_chunk):
    xn = _rms(x_ref[...], g_ref[...]).astype(BF16)
    for c in range(n_main // col_chunk):
        sl = slice(c * col_chunk, (c + 1) * col_chunk)
        o_ref[:, sl] = _dot(xn, w_ref[:, sl]).astype(o_ref.dtype)
    if gate_ref:
        logits = _dot(xn, w_ref[:, n_main:n_main + GATE_PAD])
        gate_ref[0][...] = jax.nn.sigmoid(logits)


def norm_proj(x2d, g, w_bf16, n_main, with_gates, tm=512, col_chunk=512):
    T, D = x2d.shape
    n_w = w_bf16.shape[1]
    out_shape = [jax.ShapeDtypeStruct((T, n_main), BF16)]
    out_specs = [pl.BlockSpec((tm, n_main), lambda i: (i, 0))]
    if with_gates:
        out_shape.append(jax.ShapeDtypeStruct((T, GATE_PAD), F32))
        out_specs.append(pl.BlockSpec((tm, GATE_PAD), lambda i: (i, 0)))
    res = pl.pallas_call(
        functools.partial(_norm_proj_kernel, n_main=n_main, col_chunk=col_chunk),
        grid=(T // tm,),
        in_specs=[pl.BlockSpec((tm, D), lambda i: (i, 0)),
                  pl.BlockSpec((1, D), lambda i: (0, 0)),
                  pl.BlockSpec((D, n_w), lambda i: (0, 0))],
        out_specs=out_specs,
        out_shape=out_shape,
        compiler_params=_params(1),
        name="norm_proj",
    )(x2d, g.reshape(1, D), w_bf16)
    return res if with_gates else res[0]


def _head_pair_rows(q, t):
    lane = lax.broadcasted_iota(jnp.int32, (t, 2 * HEAD_DIM), 1)
    zero = jnp.zeros_like(q)
    return jnp.where(lane < HEAD_DIM, q, zero), jnp.where(lane >= HEAD_DIM, q, zero)


def _sb_kernel(q_ref, k_ref, vt_ref, o_ref, acc_ref, carry_ref, *, t):
    qi = pl.program_id(2)
    cols = 2 * t
    q_both = jnp.concatenate(_head_pair_rows(q_ref[0] * 0.125, t), axis=0)
    s_idx = lax.broadcasted_iota(jnp.int32, (t, 2 * t), 0)
    j_idx = lax.broadcasted_iota(jnp.int32, (t, 2 * t), 1) % t
    upper2 = jnp.where(j_idx > s_idx, 1.0, 0.0).astype(BF16)
    key = lax.broadcasted_iota(jnp.int32, (t, cols), 0)
    qry = lax.broadcasted_iota(jnp.int32, (t, cols), 1) % t
    past_diag = key < qry

    acc_ref[...] = jnp.zeros_like(acc_ref)
    carry_ref[...] = jnp.zeros_like(carry_ref)

    def chunk(kc, masked):
        start = pl.multiple_of(kc * t, t)
        z = _dot_nt(k_ref[0, pl.ds(start, t), :], q_both)
        softplus = jnp.maximum(z, 0.0) + jnp.log(1.0 + jnp.exp(-jnp.abs(z)))
        log_rem = -softplus
        if masked:
            log_rem = jnp.where(past_diag, log_rem, 0.0)
        hi, lo = _split_bf16(log_rem)
        tail = _dot(upper2, jnp.concatenate([hi, lo], axis=0))
        carry = carry_ref[...]
        w = jnp.exp((z - softplus) + tail + carry)
        if masked:
            w = jnp.where(past_diag, w, 0.0)
        carry_ref[...] = carry + tail[0:1] + log_rem[0:1]
        acc_ref[...] += _dot(vt_ref[0, :, pl.ds(start, t)], w.astype(BF16))

    chunk(qi, True)

    def body(j, c):
        chunk(qi - 1 - j, False)
        return c

    lax.fori_loop(0, qi, body, 0)
    row = lax.broadcasted_iota(jnp.int32, (2 * HEAD_DIM, t), 0)
    o_t = jnp.where(row < HEAD_DIM, acc_ref[:, :t], acc_ref[:, t:])
    o_ref[0] = o_t.T.astype(o_ref.dtype)


def sb_attention(proj, v_t, t=256):
    B, S, _ = proj.shape
    n_pairs = 4
    return pl.pallas_call(
        functools.partial(_sb_kernel, t=t),
        grid=(B, n_pairs, S // t),
        in_specs=[pl.BlockSpec((1, t, 128), lambda b, p, i: (b, i, p)),
                  pl.BlockSpec((1, S, 128), lambda b, p, i: (b, 0, n_pairs + p)),
                  pl.BlockSpec((1, 128, S), lambda b, p, i: (b, p, 0))],
        out_specs=pl.BlockSpec((1, t, 128), lambda b, p, i: (b, i, p)),
        out_shape=jax.ShapeDtypeStruct((B, S, n_pairs * 128), BF16),
        scratch_shapes=[pltpu.VMEM((128, 2 * t), F32), pltpu.VMEM((1, 2 * t), F32)],
        compiler_params=_params(3),
        name="sb_attention",
    )(proj, proj, v_t)


def _diff_kernel(slopes_ref, lam_ref, q_ref, k_ref, vt_ref, pos_ref, sl_ref, subln_ref, o_ref,
                 m_ref, acc_ref, *, t, out_scale, lambda_init):
    h = pl.program_id(1)
    qi = pl.program_id(2)
    slope = slopes_ref[h]
    cols = 2 * t
    slope_cols = jnp.broadcast_to(sl_ref[0], (t, 128))
    q_both = jnp.concatenate(
        [jnp.concatenate([qc, slope_cols], axis=1) for qc in _head_pair_rows(q_ref[0] * 0.125, t)],
        axis=0)
    pos = pos_ref[...]
    offset = (lax.broadcasted_iota(jnp.int32, (t, cols), 0)
              - lax.broadcasted_iota(jnp.int32, (t, cols), 1) % t)

    m_ref[...] = jnp.full_like(m_ref, NEG_INF)
    acc_ref[...] = jnp.zeros_like(acc_ref)

    def chunk(kc, masked):
        start = pl.multiple_of(kc * t, t)
        k_aug = jnp.concatenate([k_ref[0, pl.ds(start, t), :], pos], axis=1)
        s_t = _dot_nt(k_aug, q_both)
        if masked:
            s_t = jnp.where(offset <= 0, s_t, NEG_INF)
        kappa = slope * ((kc - qi) * t).astype(F32)
        _softmax_step(s_t, vt_ref[0, 0, :, pl.ds(start, t)], kappa, m_ref, acc_ref)

    def body(j, c):
        chunk(j, False)
        return c

    lax.fori_loop(0, qi, body, 0)
    chunk(qi, True)

    lam_terms = lam_ref[...]
    lam = (jnp.exp(jnp.sum(lam_terms[0:1] * lam_terms[1:2], axis=1, keepdims=True))
           - jnp.exp(jnp.sum(lam_terms[2:3] * lam_terms[3:4], axis=1, keepdims=True))
           + lambda_init)
    o_t = acc_ref[0:128, :] / acc_ref[128:129, :]
    o = (o_t[:, :t] - lam * o_t[:, t:]).T
    o_ref[0] = (_rms(o, subln_ref[...]) * out_scale).astype(o_ref.dtype)


def diff_attention(proj, vt_aug, lam_rows, subln, slopes, layer, t=256):
    B, S, _ = proj.shape
    n_heads = 4
    assert t <= BF16_EXACT_INT
    lambda_init = 0.8 - 0.6 * math.exp(-0.3 * layer)
    smem = pl.BlockSpec(memory_space=pltpu.SMEM)
    pos = _key_position_columns(t, t, 128)
    slope_cols = _slope_pieces(slopes, 128).reshape(n_heads, 1, 128)
    return pl.pallas_call(
        functools.partial(_diff_kernel, t=t, out_scale=1.0 - lambda_init, lambda_init=lambda_init),
        grid=(B, n_heads, S // t),
        in_specs=[smem,
                  pl.BlockSpec((4, HEAD_DIM), lambda b, h, i: (0, 0)),
                  pl.BlockSpec((1, t, 128), lambda b, h, i: (b, i, 12 + h)),
                  pl.BlockSpec((1, S, 128), lambda b, h, i: (b, 0, 16 + h)),
                  pl.BlockSpec((1, 1, V_ROWS_128, S), lambda b, h, i: (b, h, 0, 0)),
                  pl.BlockSpec((t, 128), lambda b, h, i: (0, 0)),
                  pl.BlockSpec((1, 1, 128), lambda b, h, i: (h, 0, 0)),
                  pl.BlockSpec((1, 128), lambda b, h, i: (0, 0))],
        out_specs=pl.BlockSpec((1, t, 128), lambda b, h, i: (b, i, h)),
        out_shape=jax.ShapeDtypeStruct((B, S, n_heads * 128), BF16),
        scratch_shapes=[pltpu.VMEM((1, 2 * t), F32), pltpu.VMEM((V_ROWS_128, 2 * t), F32)],
        compiler_params=_params(3),
        name="diff_attention",
    )(slopes, lam_rows, proj, proj, vt_aug, pos, slope_cols, subln.reshape(1, 128))


def _post_kernel(*refs, ff_chunk, final):
    mix_ref, x_ref, wo_ref, g_ref, w1_ref, w2_ref = refs[:6]
    gf_ref = refs[6] if final else None
    o_ref = refs[-1]
    x = x_ref[...] + _dot(mix_ref[...], wo_ref[...])
    hn = _rms(x, g_ref[...]).astype(BF16)
    acc = x
    for f in range(w1_ref.shape[1] // ff_chunk):
        sl = slice(f * ff_chunk, (f + 1) * ff_chunk)
        hid = jnp.maximum(_dot(hn, w1_ref[:, sl]), 0.0)
        acc = acc + _dot((hid * hid).astype(BF16), w2_ref[sl, :])
    if final:
        acc = _rms(acc, gf_ref[...])
    o_ref[...] = acc


def post_block(mix, x2d, w_out, g_mlp, w1, w2, g_final=None, tm=512, ff_chunk=1024):
    T, D = x2d.shape
    final = g_final is not None
    const = lambda i: (0, 0)
    in_specs = [pl.BlockSpec((tm, mix.shape[1]), lambda i: (i, 0)),
                pl.BlockSpec((tm, D), lambda i: (i, 0)),
                pl.BlockSpec(w_out.shape, const),
                pl.BlockSpec((1, D), const), pl.BlockSpec(w1.shape, const), pl.BlockSpec(w2.shape, const)]
    args = [mix, x2d, w_out, g_mlp.reshape(1, D), w1, w2]
    if final:
        in_specs.append(pl.BlockSpec((1, D), const))
        args.append(g_final.reshape(1, D))
    return pl.pallas_call(
        functools.partial(_post_kernel, ff_chunk=ff_chunk, final=final),
        grid=(T // tm,),
        in_specs=in_specs,
        out_specs=pl.BlockSpec((tm, D), lambda i: (i, 0)),
        out_shape=jax.ShapeDtypeStruct((T, D), F32),
        compiler_params=_params(1),
        name="post_block",
    )(*args)


def _compress_kernel(c_ref, pos_ref, w1_ref, w2_ref, o_ref):
    half = w1_ref.shape[1] // 2
    chunks = c_ref[0, 0]
    pos = jnp.broadcast_to(pos_ref[0], (8, 2 * half)).astype(BF16)
    first = _dot(chunks, w1_ref[0, :half, :])
    second = _dot(chunks, w1_ref[0, half:, :])
    n_chunks = chunks.shape[0]
    pre = first + pltpu.roll(second, n_chunks - 1, 0) + _dot(pos, w1_ref[0])[0:1]
    hid = jax.nn.gelu(pre)
    o_ref[0, 0] = _dot(hid.astype(BF16), w2_ref[0]).astype(o_ref.dtype)


def compress_kv(chunks, pos_flat, w1, w2):
    _, BG, n_chunks, width = chunks.shape
    hidden = w1.shape[-1]
    return pl.pallas_call(
        _compress_kernel,
        grid=(2, BG),
        in_specs=[pl.BlockSpec((1, 1, n_chunks, width), lambda s, i: (s, i, 0, 0)),
                  pl.BlockSpec((1, 1, 2 * width), lambda s, i: (s, 0, 0)),
                  pl.BlockSpec((1, 2 * width, hidden), lambda s, i: (s, 0, 0)),
                  pl.BlockSpec((1, hidden, HEAD_DIM), lambda s, i: (s, 0, 0))],
        out_specs=pl.BlockSpec((1, 1, n_chunks, HEAD_DIM), lambda s, i: (s, i, 0, 0)),
        out_shape=jax.ShapeDtypeStruct((2, BG, n_chunks, HEAD_DIM), BF16),
        compiler_params=_params(2),
        name="compress_kv",
    )(chunks, pos_flat, w1, w2)


def _stack_heads(q):
    return jnp.concatenate([q[:, r * HEAD_DIM:(r + 1) * HEAD_DIM] for r in range(NSA_HPG)], axis=0)


def _slope_row(slopes_ref, g, tq):
    col = lax.broadcasted_iota(jnp.int32, (1, NSA_HPG * tq), 1)
    out = jnp.zeros((1, NSA_HPG * tq), F32)
    for r in range(NSA_HPG):
        out = jnp.where(col // tq == r, slopes_ref[g * NSA_HPG + r], out)
    return out


def _cmp_select_kernel(slopes_ref, q_ref, kc_ref, vct_ref, ovt_ref, ocmp_ref, sel_ref, *, tq, n_sel):
    g = pl.program_id(1)
    t0 = pl.program_id(2) * tq
    cols = NSA_HPG * tq
    n_cmp = kc_ref.shape[2]
    slope_row = _slope_row(slopes_ref, g, tq)
    q_rows = _stack_heads(q_ref[0] * 0.125)
    tpos = t0 + lax.broadcasted_iota(jnp.int32, (n_cmp, cols), 1) % tq
    cmp_end = lax.broadcasted_iota(jnp.int32, (n_cmp, cols), 0) * CMP_STRIDE + (CMP_LEN - 1)
    dc = (tpos - cmp_end).astype(F32)
    sc = jnp.where(dc >= 0, _dot_nt(kc_ref[0, 0], q_rows) - slope_row * dc, NEG_INF)
    e = jnp.exp(sc - jnp.max(sc, axis=0, keepdims=True))
    any_valid = jnp.where(dc[0:1] >= 0, 1.0, 0.0)
    pc = e * (any_valid / jnp.sum(e, axis=0, keepdims=True))
    ocmp_ref[0, 0, 0] = _dot(vct_ref[0, 0], pc.astype(BF16))

    pc_group = pc[:, 0:tq]
    for r in range(1, NSA_HPG):
        pc_group = pc_group + pc[:, r * tq:(r + 1) * tq]
    hi, lo = _split_bf16(pc_group)
    imp = _dot(ovt_ref[...], hi) + _dot(ovt_ref[...], lo)

    blk = lax.broadcasted_iota(jnp.int32, (n_sel, tq), 0)
    cur = (t0 + lax.broadcasted_iota(jnp.int32, (n_sel, tq), 1)) // SEL_LEN
    forced = (blk == 0) | (blk == cur) | (blk == cur - 1)
    imp = jnp.where(forced, FORCE_SCORE, imp)
    imp = jnp.where(blk <= cur, imp, -1.0)
    picked = jnp.zeros((n_sel, tq), F32)
    for _ in range(min(SEL_TOPK, n_sel)):
        best = jnp.max(imp, axis=0, keepdims=True)
        first = jnp.min(jnp.where(imp == best, blk, n_sel), axis=0, keepdims=True)
        hit = blk == first
        picked = jnp.where(hit, 1.0, picked)
        imp = jnp.where(hit, -2.0, imp)
    keep = (picked > 0.5) & (blk <= cur)
    sel_ref[0, 0] = jnp.where(keep, 0.0, -SEL_DROP).astype(sel_ref.dtype)


def cmp_select(proj, kc, vc_t, overlap_t, slopes, tq=128):
    B, S, _ = proj.shape
    n_cmp = kc.shape[2]
    n_sel = S // SEL_LEN
    G = NSA_GROUPS
    cols = NSA_HPG * tq
    smem = pl.BlockSpec(memory_space=pltpu.SMEM)
    return pl.pallas_call(
        functools.partial(_cmp_select_kernel, tq=tq, n_sel=n_sel),
        grid=(B, G, S // tq),
        in_specs=[smem,
                  pl.BlockSpec((1, tq, 256), lambda b, g, i: (b, i, g)),
                  pl.BlockSpec((1, 1, n_cmp, HEAD_DIM), lambda b, g, i: (b, g, 0, 0)),
                  pl.BlockSpec((1, 1, HEAD_DIM, n_cmp), lambda b, g, i: (b, g, 0, 0)),
                  pl.BlockSpec((n_sel, n_cmp), lambda b, g, i: (0, 0))],
        out_specs=[pl.BlockSpec((1, 1, 1, HEAD_DIM, cols), lambda b, g, i: (b, g, i, 0, 0)),
                   pl.BlockSpec((1, 1, n_sel, tq), lambda b, g, i: (b, g, 0, i))],
        out_shape=[jax.ShapeDtypeStruct((B, G, S // tq, HEAD_DIM, cols), F32),
                   jax.ShapeDtypeStruct((B, G, n_sel, S), BF16)],
        compiler_params=_params(3),
        name="cmp_select",
    )(slopes, proj, kc, vc_t, overlap_t)


def _sel_win_kernel(slopes_ref, q_ref, sl_ref, ks_ref, vst_ref, kw_ref, vwt_ref, sel_ref, onehot_ref,
                    ocmp_ref, gate_ref, o_ref, m_ref, acc_ref, *, tq, tk):
    g = pl.program_id(1)
    t0 = pl.program_id(2) * tq
    cols = NSA_HPG * tq
    slope_row = _slope_row(slopes_ref, g, tq)
    q = q_ref[0] * 0.125
    q_aug = jnp.concatenate(
        [jnp.concatenate([q[:, r * HEAD_DIM:(r + 1) * HEAD_DIM],
                          jnp.broadcast_to(sl_ref[0, r:r + 1, :], (tq, HEAD_DIM))], axis=1)
         for r in range(NSA_HPG)], axis=0)
    sel_bias = jnp.concatenate([sel_ref[0, 0]] * NSA_HPG, axis=1)
    offset = (lax.broadcasted_iota(jnp.int32, (tk, cols), 0)
              - lax.broadcasted_iota(jnp.int32, (tk, cols), 1) % tq)

    def reset():
        m_ref[...] = jnp.full_like(m_ref, NEG_INF)
        acc_ref[...] = jnp.zeros_like(acc_ref)

    def finish():
        return acc_ref[0:HEAD_DIM, :] / acc_ref[HEAD_DIM:HEAD_DIM + 1, :]

    def sel_chunk(c, diagonal):
        start = pl.multiple_of(c * tk, tk)
        shift = c * tk - t0
        s_t = (_dot_nt(ks_ref[0, 0, pl.ds(start, tk), :], q_aug)
               + _dot(onehot_ref[pl.ds(start, tk), :], sel_bias))
        if diagonal:
            s_t = jnp.where(offset + shift <= 0, s_t, NEG_INF)
        _softmax_step(s_t, vst_ref[0, 0, :, pl.ds(start, tk)], slope_row * shift.astype(F32),
                      m_ref, acc_ref)

    reset()
    last = t0 // tk

    def sel_body(c, carry):
        sel_chunk(c, False)
        return carry

    lax.fori_loop(0, last, sel_body, 0)
    sel_chunk(last, True)
    o_sel = finish()

    def win_body(c, carry):
        start = pl.multiple_of(c * tk, tk)
        shift = c * tk - t0
        dist = -(offset + shift)
        s_t = _dot_nt(kw_ref[0, 0, pl.ds(start, tk), :], q_aug)
        s_t = jnp.where((dist >= 0) & (dist < WINDOW), s_t, NEG_INF)
        _softmax_step(s_t, vwt_ref[0, 0, :, pl.ds(start, tk)], slope_row * shift.astype(F32),
                      m_ref, acc_ref)
        return carry

    reset()
    lax.fori_loop(jnp.maximum(t0 - (WINDOW - 1), 0) // tk, last + 1, win_body, 0)
    o_win = finish()

    gates = gate_ref[0, 0]
    o_cmp = ocmp_ref[0, 0, 0]
    outs = []
    for r in range(NSA_HPG):
        cs = slice(r * tq, (r + 1) * tq)
        outs.append(gates[3 * r:3 * r + 1] * o_cmp[:, cs] + gates[3 * r + 1:3 * r + 2] * o_sel[:, cs]
                    + gates[3 * r + 2:3 * r + 3] * o_win[:, cs])
    o_ref[0] = jnp.concatenate(outs, axis=0).T.astype(o_ref.dtype)


def sel_win_attention(proj, slope_cols, ks, vs_t, kw, vw_t, sel_bias, onehot, o_cmp, gates_t, slopes,
                      tq=128, tk=256):
    B, S, _ = proj.shape
    G = NSA_GROUPS
    n_sel = S // SEL_LEN
    cols = NSA_HPG * tq
    smem = pl.BlockSpec(memory_space=pltpu.SMEM)
    k_spec = pl.BlockSpec((1, 1, S, 128), lambda b, g, i: (b, g, 0, 0))
    v_spec = pl.BlockSpec((1, 1, V_ROWS_64, S), lambda b, g, i: (b, g, 0, 0))
    return pl.pallas_call(
        functools.partial(_sel_win_kernel, tq=tq, tk=tk),
        grid=(B, G, S // tq),
        in_specs=[smem,
                  pl.BlockSpec((1, tq, 256), lambda b, g, i: (b, i, g)),
                  pl.BlockSpec((1, NSA_HPG, HEAD_DIM), lambda b, g, i: (g, 0, 0)),
                  k_spec, v_spec, k_spec, v_spec,
                  pl.BlockSpec((1, 1, n_sel, tq), lambda b, g, i: (b, g, 0, i)),
                  pl.BlockSpec((S, n_sel), lambda b, g, i: (0, 0)),
                  pl.BlockSpec((1, 1, 1, HEAD_DIM, cols), lambda b, g, i: (b, g, i, 0, 0)),
                  pl.BlockSpec((1, 1, NSA_HPG * N_GATES, tq), lambda b, g, i: (b, g, 0, i))],
        out_specs=pl.BlockSpec((1, tq, 256), lambda b, g, i: (b, i, g)),
        out_shape=jax.ShapeDtypeStruct((B, S, G * 256), BF16),
        scratch_shapes=[pltpu.VMEM((1, cols), F32), pltpu.VMEM((V_ROWS_64, cols), F32)],
        compiler_params=_params(3),
        name="sel_win_attention",
    )(slopes, proj, slope_cols, ks, vs_t, kw, vw_t, sel_bias, onehot, o_cmp, gates_t)


def _alibi_slopes(n_heads):
    return jnp.exp2(-8.0 * (jnp.arange(n_heads, dtype=F32) + 1.0) / n_heads)


def even_layer_mix(x2d, B, S, norm_g, w_in, lam_q1, lam_k1, lam_q2, lam_k2, subln, layer):
    proj = norm_proj(x2d, norm_g, w_in.astype(BF16), w_in.shape[1], False).reshape(B, S, -1)
    sb_vt = proj[:, :, 1024:1536].transpose(0, 2, 1)
    df_vt = proj[:, :, 2560:3072].transpose(0, 2, 1).reshape(B, 4, 128, S)
    o_sb = sb_attention(proj, sb_vt)
    lam_rows = jnp.stack([lam_q1, lam_k1, lam_q2, lam_k2]).astype(F32)
    o_df = diff_attention(proj, _with_ones_row(df_vt, V_ROWS_128), lam_rows, subln.astype(F32),
                          _alibi_slopes(4), layer)
    return jnp.concatenate([o_sb, o_df], axis=2).reshape(B * S, -1)


def odd_layer_mix(x2d, B, S, norm_g, w_in, pos_k, k_w1, k_w2, pos_v, v_w1, v_w2, tk=256):
    G, Dh = NSA_GROUPS, HEAD_DIM
    q_width = NSA_GROUPS * NSA_HPG * Dh
    kv_width = G * Dh
    n_main = q_width + 6 * kv_width
    pad = n_main + GATE_PAD - w_in.shape[1]
    w_pad = jnp.pad(w_in, ((0, 0), (0, pad))).astype(BF16)
    proj, gates = norm_proj(x2d, norm_g, w_pad, n_main, True)
    proj = proj.reshape(B, S, n_main)

    def group_major(j):
        cols = proj[:, :, q_width + j * kv_width:q_width + (j + 1) * kv_width]
        return cols.reshape(B, S, G, Dh).transpose(0, 2, 1, 3)

    def keys_with_positions(j):
        pos = jnp.broadcast_to(_key_position_columns(S, tk, Dh), (B, G, S, Dh))
        return jnp.concatenate([group_major(j), pos], axis=-1)

    def values_transposed(j):
        return _with_ones_row(group_major(j).transpose(0, 1, 3, 2), V_ROWS_64)

    n_chunks = S // CMP_STRIDE
    chunks = jnp.stack([group_major(0), group_major(1)]).reshape(2, B * G, n_chunks, CMP_STRIDE * Dh)
    pos_flat = jnp.stack([pos_k, pos_v]).reshape(2, 1, CMP_LEN * Dh).astype(F32)
    w1 = jnp.stack([k_w1, v_w1]).astype(BF16)
    w2 = jnp.stack([k_w2, v_w2]).astype(BF16)
    cmp = compress_kv(chunks, pos_flat, w1, w2).reshape(2, B, G, n_chunks, Dh)

    n_sel = S // SEL_LEN
    cmp_start = jnp.arange(n_chunks) * CMP_STRIDE
    sel_start = jnp.arange(n_sel) * SEL_LEN
    overlap_t = ((cmp_start[None, :] < sel_start[:, None] + SEL_LEN)
                 & (sel_start[:, None] <= cmp_start[None, :] + CMP_LEN - 1)).astype(BF16)
    onehot = (jnp.arange(S)[:, None] // SEL_LEN == jnp.arange(n_sel)[None, :]).astype(BF16)
    slopes = _alibi_slopes(NSA_GROUPS * NSA_HPG)
    slope_cols = _slope_pieces(slopes, Dh).reshape(G, NSA_HPG, Dh)

    o_cmp, sel_bias = cmp_select(proj, cmp[0], cmp[1].transpose(0, 1, 3, 2), overlap_t, slopes)
    gates_t = (gates[:, :G * NSA_HPG * N_GATES].reshape(B, S, G, NSA_HPG * N_GATES)
               .transpose(0, 2, 3, 1))
    o = sel_win_attention(proj, slope_cols, keys_with_positions(2), values_transposed(3),
                          keys_with_positions(4), values_transposed(5), sel_bias, onehot, o_cmp,
                          gates_t, slopes, tk=tk)
    return o.reshape(B * S, q_width)


def kernel(x, attn_norm, mlp_norm, final_norm, ev_w_in, ev_lam_q1, ev_lam_k1, ev_lam_q2, ev_lam_k2,
           ev_subln, ev_w_out, od_w_in, od_cmp_pos_k, od_cmp_k_w1, od_cmp_k_w2, od_cmp_pos_v,
           od_cmp_v_w1, od_cmp_v_w2, od_w_out, mlp_w1, mlp_w2):
    B, S, D = x.shape
    depth = attn_norm.shape[0]
    x2d = x.reshape(B * S, D)
    for layer in range(depth):
        idx = layer // 2
        if layer % 2 == 0:
            mix = even_layer_mix(x2d, B, S, attn_norm[layer], ev_w_in[idx], ev_lam_q1[idx],
                                 ev_lam_k1[idx], ev_lam_q2[idx], ev_lam_k2[idx], ev_subln[idx], layer)
            w_out = ev_w_out[idx]
        else:
            mix = odd_layer_mix(x2d, B, S, attn_norm[layer], od_w_in[idx], od_cmp_pos_k[idx],
                                od_cmp_k_w1[idx], od_cmp_k_w2[idx], od_cmp_pos_v[idx],
                                od_cmp_v_w1[idx], od_cmp_v_w2[idx])
            w_out = od_w_out[idx]
        g_final = final_norm if layer == depth - 1 else None
        x2d = post_block(mix, x2d, w_out.astype(BF16), mlp_norm[layer], mlp_w1[layer].astype(BF16),
                         mlp_w2[layer].astype(BF16), g_final)
    return x2d.reshape(B, S, D)
```

```python
import functools
import math

import jax
import jax.numpy as jnp
import numpy as np
from jax import lax
from jax.experimental import pallas as pl
from jax.experimental.pallas import tpu as pltpu

F32 = jnp.float32
BF16 = jnp.bfloat16

HEAD_DIM = 64
RMS_EPS = 1e-6
NEG_INF = -1e30
FORCE_SCORE = 1e6
NSA_GROUPS = 4
NSA_HPG = 4
CMP_LEN = 32
CMP_STRIDE = 16
SEL_LEN = 64
SEL_TOPK = 16
WINDOW = 512
N_GATES = 3
GATE_PAD = 128
SEL_DROP = 2.0 ** 24
BF16_EXACT_INT = 256
V_ROWS_64 = 80
V_ROWS_128 = 144

LOG2E = math.log2(math.e)
Q_SCALE = HEAD_DIM ** -0.5 * LOG2E
UNDERFLOW_LOG2 = -160.0

VMEM_LIMIT = 56 * 1024 * 1024


def _params(n_parallel):
    return pltpu.CompilerParams(dimension_semantics=("parallel",) * n_parallel,
                                vmem_limit_bytes=VMEM_LIMIT)


def _rms(x, g):
    ms = jnp.mean(x * x, axis=-1, keepdims=True)
    return x * lax.rsqrt(ms + RMS_EPS) * g


def _dot(a, b):
    return jnp.dot(a, b, preferred_element_type=F32)


def _dot_nt(a, b):
    return lax.dot_general(a, b, (((1,), (1,)), ((), ())), preferred_element_type=F32)


def _split_bf16(x):
    hi = x.astype(BF16)
    lo = (x - hi.astype(F32)).astype(BF16)
    return hi, lo


def _slope_pieces(slopes, width):
    def top_bits(x):
        return (x.view(np.uint32) & np.uint32(0xFFFF0000)).view(np.float32)

    s1 = top_bits(slopes)
    r1 = slopes - s1
    s2 = top_bits(r1)
    s3 = top_bits(r1 - s2)
    out = np.zeros((slopes.shape[0], width), np.float32)
    out[:, :6] = np.stack([s1, s2, s3, s1, s2, s3], axis=1)
    return jnp.asarray(out).astype(BF16)


def _key_position_columns(n, tk, width):
    j = jnp.arange(n) % tk
    a = (j // BF16_EXACT_INT) * BF16_EXACT_INT
    b = j % BF16_EXACT_INT
    cols = jnp.stack([a, a, a, b, b, b], axis=1).astype(BF16)
    return jnp.pad(cols, ((0, 0), (0, width - cols.shape[1])))


def _with_ones_row(v_t, rows):
    d = v_t.shape[-2]
    ones = jnp.ones(v_t.shape[:-2] + (1, v_t.shape[-1]), v_t.dtype)
    pad = jnp.zeros(v_t.shape[:-2] + (rows - d - 1, v_t.shape[-1]), v_t.dtype)
    return jnp.concatenate([v_t, ones, pad], axis=-2)


def _softmax_step(s_t, vt_chunk, kappa, m_ref, acc_ref):
    m_old = m_ref[...]
    m_new = jnp.maximum(m_old, jnp.max(s_t, axis=0, keepdims=True) + kappa)
    alpha = jnp.exp2(m_old - m_new)
    p = jnp.exp2(s_t - (m_new - kappa))
    acc_ref[...] = alpha * acc_ref[...] + _dot(vt_chunk, p.astype(BF16))
    m_ref[...] = m_new


def _norm_proj_kernel(x_ref, g_ref, w_ref, scale_ref, o_ref, *gate_ref, n_main, col_chunk):
    xn = _rms(x_ref[...], g_ref[...]).astype(BF16)
    for c in range(n_main // col_chunk):
        sl = slice(c * col_chunk, (c + 1) * col_chunk)
        o_ref[:, sl] = (_dot(xn, w_ref[:, sl]) * scale_ref[:, sl]).astype(o_ref.dtype)
    if gate_ref:
        logits = _dot(xn, w_ref[:, n_main:n_main + GATE_PAD])
        gate_ref[0][...] = jax.nn.sigmoid(logits)


def norm_proj(x2d, g, w_bf16, n_main, q_cols, with_gates, tm=512, col_chunk=512):
    T, D = x2d.shape
    n_w = w_bf16.shape[1]
    col = jnp.arange(n_main)
    is_q = functools.reduce(jnp.logical_or, [(col >= lo) & (col < hi) for lo, hi in q_cols])
    col_scale = jnp.where(is_q, Q_SCALE, 1.0).astype(F32).reshape(1, n_main)
    out_shape = [jax.ShapeDtypeStruct((T, n_main), BF16)]
    out_specs = [pl.BlockSpec((tm, n_main), lambda i: (i, 0))]
    if with_gates:
        out_shape.append(jax.ShapeDtypeStruct((T, GATE_PAD), F32))
        out_specs.append(pl.BlockSpec((tm, GATE_PAD), lambda i: (i, 0)))
    res = pl.pallas_call(
        functools.partial(_norm_proj_kernel, n_main=n_main, col_chunk=col_chunk),
        grid=(T // tm,),
        in_specs=[pl.BlockSpec((tm, D), lambda i: (i, 0)),
                  pl.BlockSpec((1, D), lambda i: (0, 0)),
                  pl.BlockSpec((D, n_w), lambda i: (0, 0)),
                  pl.BlockSpec((1, n_main), lambda i: (0, 0))],
        out_specs=out_specs,
        out_shape=out_shape,
        compiler_params=_params(1),
        name="norm_proj",
    )(x2d, g.reshape(1, D), w_bf16, col_scale)
    return res if with_gates else res[0]


def _head_pair_rows(q, t):
    lane = lax.broadcasted_iota(jnp.int32, (t, 2 * HEAD_DIM), 1)
    zero = jnp.zeros_like(q)
    return jnp.where(lane < HEAD_DIM, q, zero), jnp.where(lane >= HEAD_DIM, q, zero)


def _sb_kernel(q_ref, k_ref, vt_ref, o_ref, acc_ref, carry_ref, *, t):
    qi = pl.program_id(2)
    cols = 2 * t
    q_both = jnp.concatenate(_head_pair_rows(q_ref[0], t), axis=0)
    s_idx = lax.broadcasted_iota(jnp.int32, (t, 2 * t), 0)
    j_idx = lax.broadcasted_iota(jnp.int32, (t, 2 * t), 1) % t
    upper2 = jnp.where(j_idx > s_idx, 1.0, 0.0).astype(BF16)
    key = lax.broadcasted_iota(jnp.int32, (t, cols), 0)
    qry = lax.broadcasted_iota(jnp.int32, (t, cols), 1) % t
    past_diag = key < qry

    acc_ref[...] = jnp.zeros_like(acc_ref)
    carry_ref[...] = jnp.zeros_like(carry_ref)

    def chunk(kc, masked):
        start = pl.multiple_of(kc * t, t)
        z = _dot_nt(k_ref[0, pl.ds(start, t), :], q_both)
        drop = jnp.maximum(z, 0.0) + jnp.log2(1.0 + jnp.exp2(jnp.abs(z) * -1.0))
        log_beta = z - drop
        if masked:
            drop = jnp.where(past_diag, drop, 0.0)
        hi, lo = _split_bf16(drop)
        tail = _dot(upper2, jnp.concatenate([hi, lo], axis=0))
        carry = carry_ref[...]
        w = jnp.exp2(log_beta - tail - carry)
        if masked:
            w = jnp.where(past_diag, w, 0.0)
        carry = carry + tail[0:1] + drop[0:1]
        carry_ref[...] = carry
        acc_ref[...] += _dot(vt_ref[0, :, pl.ds(start, t)], w.astype(BF16))
        return jnp.min(carry)

    def cond(state):
        j, least_carry = state
        return (j < qi) & (least_carry < -UNDERFLOW_LOG2)

    def body(state):
        j, _ = state
        return j + 1, chunk(qi - 1 - j, False)

    lax.while_loop(cond, body, (0, chunk(qi, True)))
    row = lax.broadcasted_iota(jnp.int32, (2 * HEAD_DIM, t), 0)
    o_t = jnp.where(row < HEAD_DIM, acc_ref[:, :t], acc_ref[:, t:])
    o_ref[0] = o_t.T.astype(o_ref.dtype)


def sb_attention(proj, v_t, t=256):
    B, S, _ = proj.shape
    n_pairs = 4
    return pl.pallas_call(
        functools.partial(_sb_kernel, t=t),
        grid=(B, n_pairs, S // t),
        in_specs=[pl.BlockSpec((1, t, 128), lambda b, p, i: (b, i, p)),
                  pl.BlockSpec((1, S, 128), lambda b, p, i: (b, 0, n_pairs + p)),
                  pl.BlockSpec((1, 128, S), lambda b, p, i: (b, p, 0))],
        out_specs=pl.BlockSpec((1, t, 128), lambda b, p, i: (b, i, p)),
        out_shape=jax.ShapeDtypeStruct((B, S, n_pairs * 128), BF16),
        scratch_shapes=[pltpu.VMEM((128, 2 * t), F32), pltpu.VMEM((1, 2 * t), F32)],
        compiler_params=_params(3),
        name="sb_attention",
    )(proj, proj, v_t)


def _diff_kernel(slopes_ref, lam_ref, q_ref, k_ref, vt_ref, pos_ref, sl_ref, subln_ref, o_ref,
                 m_ref, acc_ref, *, t, out_scale, lambda_init):
    h = pl.program_id(1)
    qi = pl.program_id(2)
    slope = slopes_ref[h]
    cols = 2 * t
    slope_cols = jnp.broadcast_to(sl_ref[0], (t, 128))
    q_both = jnp.concatenate(
        [jnp.concatenate([qc, slope_cols], axis=1) for qc in _head_pair_rows(q_ref[0], t)],
        axis=0)
    pos = pos_ref[...]
    offset = (lax.broadcasted_iota(jnp.int32, (t, cols), 0)
              - lax.broadcasted_iota(jnp.int32, (t, cols), 1) % t)

    m_ref[...] = jnp.full_like(m_ref, NEG_INF)
    acc_ref[...] = jnp.zeros_like(acc_ref)

    def scores(kc):
        start = pl.multiple_of(kc * t, t)
        return _dot_nt(jnp.concatenate([k_ref[0, pl.ds(start, t), :], pos], axis=1), q_both)

    def step(s_t, kc):
        start = pl.multiple_of(kc * t, t)
        kappa = slope * ((kc - qi) * t).astype(F32)
        _softmax_step(s_t, vt_ref[0, 0, :, pl.ds(start, t)], kappa, m_ref, acc_ref)

    def body(kc, s_cur):
        s_next = scores(kc + 1)
        step(s_cur, kc)
        return s_next

    s_diag = lax.fori_loop(0, qi, body, scores(0))
    step(jnp.where(offset <= 0, s_diag, NEG_INF), qi)

    lam_terms = lam_ref[...]
    lam = (jnp.exp(jnp.sum(lam_terms[0:1] * lam_terms[1:2], axis=1, keepdims=True))
           - jnp.exp(jnp.sum(lam_terms[2:3] * lam_terms[3:4], axis=1, keepdims=True))
           + lambda_init)
    o_t = acc_ref[0:128, :] / acc_ref[128:129, :]
    o = (o_t[:, :t] - lam * o_t[:, t:]).T
    o_ref[0] = (_rms(o, subln_ref[...]) * out_scale).astype(o_ref.dtype)


def diff_attention(proj, vt_aug, lam_rows, subln, slopes, layer, t=512):
    B, S, _ = proj.shape
    n_heads = 4
    lambda_init = 0.8 - 0.6 * math.exp(-0.3 * layer)
    smem = pl.BlockSpec(memory_space=pltpu.SMEM)
    pos = _key_position_columns(t, t, 128)
    slope_cols = _slope_pieces(slopes, 128).reshape(n_heads, 1, 128)
    return pl.pallas_call(
        functools.partial(_diff_kernel, t=t, out_scale=1.0 - lambda_init, lambda_init=lambda_init),
        grid=(B, n_heads, S // t),
        in_specs=[smem,
                  pl.BlockSpec((4, HEAD_DIM), lambda b, h, i: (0, 0)),
                  pl.BlockSpec((1, t, 128), lambda b, h, i: (b, i, 12 + h)),
                  pl.BlockSpec((1, S, 128), lambda b, h, i: (b, 0, 16 + h)),
                  pl.BlockSpec((1, 1, V_ROWS_128, S), lambda b, h, i: (b, h, 0, 0)),
                  pl.BlockSpec((t, 128), lambda b, h, i: (0, 0)),
                  pl.BlockSpec((1, 1, 128), lambda b, h, i: (h, 0, 0)),
                  pl.BlockSpec((1, 128), lambda b, h, i: (0, 0))],
        out_specs=pl.BlockSpec((1, t, 128), lambda b, h, i: (b, i, h)),
        out_shape=jax.ShapeDtypeStruct((B, S, n_heads * 128), BF16),
        scratch_shapes=[pltpu.VMEM((1, 2 * t), F32), pltpu.VMEM((V_ROWS_128, 2 * t), F32)],
        compiler_params=_params(3),
        name="diff_attention",
    )(slopes, lam_rows, proj, proj, vt_aug, pos, slope_cols, subln.reshape(1, 128))


def _post_kernel(*refs, ff_chunk, final):
    mix_ref, x_ref, wo_ref, g_ref, w1_ref, w2_ref = refs[:6]
    gf_ref = refs[6] if final else None
    o_ref = refs[-1]
    x = x_ref[...] + _dot(mix_ref[...], wo_ref[...])
    hn = _rms(x, g_ref[...]).astype(BF16)
    acc = x
    for f in range(w1_ref.shape[1] // ff_chunk):
        sl = slice(f * ff_chunk, (f + 1) * ff_chunk)
        hid = jnp.maximum(_dot(hn, w1_ref[:, sl]), 0.0)
        acc = acc + _dot((hid * hid).astype(BF16), w2_ref[sl, :])
    if final:
        acc = _rms(acc, gf_ref[...])
    o_ref[...] = acc


def post_block(mix, x2d, w_out, g_mlp, w1, w2, g_final=None, tm=512, ff_chunk=1024):
    T, D = x2d.shape
    final = g_final is not None
    const = lambda i: (0, 0)
    in_specs = [pl.BlockSpec((tm, mix.shape[1]), lambda i: (i, 0)),
                pl.BlockSpec((tm, D), lambda i: (i, 0)),
                pl.BlockSpec(w_out.shape, const),
                pl.BlockSpec((1, D), const), pl.BlockSpec(w1.shape, const), pl.BlockSpec(w2.shape, const)]
    args = [mix, x2d, w_out, g_mlp.reshape(1, D), w1, w2]
    if final:
        in_specs.append(pl.BlockSpec((1, D), const))
        args.append(g_final.reshape(1, D))
    return pl.pallas_call(
        functools.partial(_post_kernel, ff_chunk=ff_chunk, final=final),
        grid=(T // tm,),
        in_specs=in_specs,
        out_specs=pl.BlockSpec((tm, D), lambda i: (i, 0)),
        out_shape=jax.ShapeDtypeStruct((T, D), F32),
        compiler_params=_params(1),
        name="post_block",
    )(*args)


def _compress_kernel(c_ref, pos_ref, w1_ref, w2_ref, o_ref):
    half = w1_ref.shape[1] // 2
    chunks = c_ref[0, 0]
    pos = jnp.broadcast_to(pos_ref[0], (8, 2 * half)).astype(BF16)
    first = _dot(chunks, w1_ref[0, :half, :])
    second = _dot(chunks, w1_ref[0, half:, :])
    n_chunks = chunks.shape[0]
    pre = first + pltpu.roll(second, n_chunks - 1, 0) + _dot(pos, w1_ref[0])[0:1]
    hid = jax.nn.gelu(pre)
    o_ref[0, 0] = _dot(hid.astype(BF16), w2_ref[0]).astype(o_ref.dtype)


def compress_kv(chunks, pos_flat, w1, w2):
    _, BG, n_chunks, width = chunks.shape
    hidden = w1.shape[-1]
    return pl.pallas_call(
        _compress_kernel,
        grid=(2, BG),
        in_specs=[pl.BlockSpec((1, 1, n_chunks, width), lambda s, i: (s, i, 0, 0)),
                  pl.BlockSpec((1, 1, 2 * width), lambda s, i: (s, 0, 0)),
                  pl.BlockSpec((1, 2 * width, hidden), lambda s, i: (s, 0, 0)),
                  pl.BlockSpec((1, hidden, HEAD_DIM), lambda s, i: (s, 0, 0))],
        out_specs=pl.BlockSpec((1, 1, n_chunks, HEAD_DIM), lambda s, i: (s, i, 0, 0)),
        out_shape=jax.ShapeDtypeStruct((2, BG, n_chunks, HEAD_DIM), BF16),
        compiler_params=_params(2),
        name="compress_kv",
    )(chunks, pos_flat, w1, w2)


def _stack_heads(q):
    return jnp.concatenate([q[:, r * HEAD_DIM:(r + 1) * HEAD_DIM] for r in range(NSA_HPG)], axis=0)


def _slope_row(slopes_ref, g, tq):
    col = lax.broadcasted_iota(jnp.int32, (1, NSA_HPG * tq), 1)
    out = jnp.zeros((1, NSA_HPG * tq), F32)
    for r in range(NSA_HPG):
        out = jnp.where(col // tq == r, slopes_ref[g * NSA_HPG + r], out)
    return out


def _cmp_select_kernel(slopes_ref, q_ref, kc_ref, vct_ref, ovt_ref, ocmp_ref, sel_ref, hits_ref,
                       *, tq, n_sel, blocks_per_chunk):
    g = pl.program_id(1)
    t0 = pl.program_id(2) * tq
    cols = NSA_HPG * tq
    n_cmp = kc_ref.shape[2]
    slope_row = _slope_row(slopes_ref, g, tq)
    q_rows = _stack_heads(q_ref[0])
    tpos = t0 + lax.broadcasted_iota(jnp.int32, (n_cmp, cols), 1) % tq
    cmp_end = lax.broadcasted_iota(jnp.int32, (n_cmp, cols), 0) * CMP_STRIDE + (CMP_LEN - 1)
    dc = (tpos - cmp_end).astype(F32)
    sc = jnp.where(dc >= 0, _dot_nt(kc_ref[0, 0], q_rows) - slope_row * dc, NEG_INF)
    e = jnp.exp2(sc - jnp.max(sc, axis=0, keepdims=True))
    any_valid = jnp.where(dc[0:1] >= 0, 1.0, 0.0)
    pc = e * (any_valid / jnp.sum(e, axis=0, keepdims=True))
    ocmp_ref[0, 0, 0] = _dot(vct_ref[0, 0], pc.astype(BF16))

    pc_group = pc[:, 0:tq]
    for r in range(1, NSA_HPG):
        pc_group = pc_group + pc[:, r * tq:(r + 1) * tq]
    hi, lo = _split_bf16(pc_group)
    imp = _dot(ovt_ref[...], hi) + _dot(ovt_ref[...], lo)

    blk = lax.broadcasted_iota(jnp.int32, (n_sel, tq), 0)
    cur = (t0 + lax.broadcasted_iota(jnp.int32, (n_sel, tq), 1)) // SEL_LEN
    forced = (blk == 0) | (blk == cur) | (blk == cur - 1)
    imp = jnp.where(forced, FORCE_SCORE, imp)
    imp = jnp.where(blk <= cur, imp, -1.0)
    picked = jnp.zeros((n_sel, tq), F32)
    for _ in range(min(SEL_TOPK, n_sel)):
        best = jnp.max(imp, axis=0, keepdims=True)
        first = jnp.min(jnp.where(imp == best, blk, n_sel), axis=0, keepdims=True)
        hit = blk == first
        picked = jnp.where(hit, 1.0, picked)
        imp = jnp.where(hit, -2.0, imp)
    keep = (picked > 0.5) & (blk <= cur)
    sel_ref[0, 0] = jnp.where(keep, 0.0, -SEL_DROP).astype(sel_ref.dtype)
    n_chunks = n_sel // blocks_per_chunk
    member = (lax.broadcasted_iota(jnp.int32, (n_chunks, n_sel), 1) // blocks_per_chunk
              == lax.broadcasted_iota(jnp.int32, (n_chunks, n_sel), 0))
    per_query = _dot(jnp.where(member, 1.0, 0.0).astype(BF16), jnp.where(keep, 1.0, 0.0).astype(BF16))
    hits_ref[0, 0, 0] = _dot(per_query.astype(BF16), jnp.ones((tq, 128), BF16))


def cmp_select(proj, kc, vc_t, overlap_t, slopes, tk, tq=128):
    B, S, _ = proj.shape
    n_cmp = kc.shape[2]
    n_sel = S // SEL_LEN
    G = NSA_GROUPS
    cols = NSA_HPG * tq
    smem = pl.BlockSpec(memory_space=pltpu.SMEM)
    return pl.pallas_call(
        functools.partial(_cmp_select_kernel, tq=tq, n_sel=n_sel, blocks_per_chunk=tk // SEL_LEN),
        grid=(B, G, S // tq),
        in_specs=[smem,
                  pl.BlockSpec((1, tq, 256), lambda b, g, i: (b, i, g)),
                  pl.BlockSpec((1, 1, n_cmp, HEAD_DIM), lambda b, g, i: (b, g, 0, 0)),
                  pl.BlockSpec((1, 1, HEAD_DIM, n_cmp), lambda b, g, i: (b, g, 0, 0)),
                  pl.BlockSpec((n_sel, n_cmp), lambda b, g, i: (0, 0))],
        out_specs=[pl.BlockSpec((1, 1, 1, HEAD_DIM, cols), lambda b, g, i: (b, g, i, 0, 0)),
                   pl.BlockSpec((1, 1, n_sel, tq), lambda b, g, i: (b, g, 0, i)),
                   pl.BlockSpec((1, 1, 1, S // tk, 128), lambda b, g, i: (b, g, i, 0, 0))],
        out_shape=[jax.ShapeDtypeStruct((B, G, S // tq, HEAD_DIM, cols), F32),
                   jax.ShapeDtypeStruct((B, G, n_sel, S), BF16),
                   jax.ShapeDtypeStruct((B, G, S // tq, S // tk, 128), F32)],
        compiler_params=_params(3),
        name="cmp_select",
    )(slopes, proj, kc, vc_t, overlap_t)


def _sel_win_kernel(slopes_ref, active_ref, q_ref, sl_ref, ks_ref, vst_ref, kw_ref, vwt_ref, sel_ref,
                    onehot_ref, ocmp_ref, gate_ref, o_ref, m_ref, acc_ref, todo_ref, *, tq, tk):
    g = pl.program_id(1)
    t0 = pl.program_id(2) * tq
    cols = NSA_HPG * tq
    slope_row = _slope_row(slopes_ref, g, tq)
    q = q_ref[0]
    q_aug = jnp.concatenate(
        [jnp.concatenate([q[:, r * HEAD_DIM:(r + 1) * HEAD_DIM],
                          jnp.broadcast_to(sl_ref[0, r:r + 1, :], (tq, HEAD_DIM))], axis=1)
         for r in range(NSA_HPG)], axis=0)
    sel_bias = jnp.concatenate([sel_ref[0, 0]] * NSA_HPG, axis=1)
    offset = (lax.broadcasted_iota(jnp.int32, (tk, cols), 0)
              - lax.broadcasted_iota(jnp.int32, (tk, cols), 1) % tq)

    def reset():
        m_ref[...] = jnp.full_like(m_ref, NEG_INF)
        acc_ref[...] = jnp.zeros_like(acc_ref)

    def finish():
        return acc_ref[0:HEAD_DIM, :] / acc_ref[HEAD_DIM:HEAD_DIM + 1, :]

    def kappa(c):
        return slope_row * (c * tk - t0).astype(F32)

    def causal(s_t, c):
        return jnp.where(offset + (c * tk - t0) <= 0, s_t, NEG_INF)

    def in_window(s_t, c):
        dist = -(offset + (c * tk - t0))
        return jnp.where((dist >= 0) & (dist < WINDOW), s_t, NEG_INF)

    def sel_scores(c):
        start = pl.multiple_of(c * tk, tk)
        return (_dot_nt(ks_ref[0, 0, pl.ds(start, tk), :], q_aug)
                + _dot(onehot_ref[pl.ds(start, tk), :], sel_bias))

    def win_scores(c):
        start = pl.multiple_of(c * tk, tk)
        return _dot_nt(kw_ref[0, 0, pl.ds(start, tk), :], q_aug)

    def step(s_t, c, vt_ref):
        start = pl.multiple_of(c * tk, tk)
        _softmax_step(s_t, vt_ref[0, 0, :, pl.ds(start, tk)], kappa(c), m_ref, acc_ref)

    last = t0 // tk

    def note_active(c, n):
        hit = active_ref[0, 0, 0, 0, c] > 0

        @pl.when(hit)
        def _():
            todo_ref[n] = c

        return n + hit.astype(jnp.int32)

    n_todo = lax.fori_loop(0, last, note_active, 0)
    todo_ref[n_todo] = last

    def sel_body(i, s_cur):
        s_next = sel_scores(todo_ref[i + 1])
        step(s_cur, todo_ref[i], vst_ref)
        return s_next

    reset()
    s_last = lax.fori_loop(0, n_todo, sel_body, sel_scores(todo_ref[0]))
    step(causal(s_last, last), last, vst_ref)
    o_sel = finish()

    first = jnp.maximum(t0 - (WINDOW - 1), 0) // tk

    def win_body(c, s_cur):
        s_next = win_scores(c + 1)
        step(in_window(s_cur, c), c, vwt_ref)
        return s_next

    reset()
    s_last = lax.fori_loop(first, last, win_body, win_scores(first))
    step(in_window(s_last, last), last, vwt_ref)
    o_win = finish()

    gates = gate_ref[0, 0]
    o_cmp = ocmp_ref[0, 0, 0]
    outs = []
    for r in range(NSA_HPG):
        cs = slice(r * tq, (r + 1) * tq)
        outs.append(gates[3 * r:3 * r + 1] * o_cmp[:, cs] + gates[3 * r + 1:3 * r + 2] * o_sel[:, cs]
                    + gates[3 * r + 2:3 * r + 3] * o_win[:, cs])
    o_ref[0] = jnp.concatenate(outs, axis=0).T.astype(o_ref.dtype)


def sel_win_attention(proj, active, slope_cols, ks, vs_t, kw, vw_t, sel_bias, onehot, o_cmp, gates_t,
                      slopes, tq=128, tk=256):
    B, S, _ = proj.shape
    G = NSA_GROUPS
    n_sel = S // SEL_LEN
    cols = NSA_HPG * tq
    smem = pl.BlockSpec(memory_space=pltpu.SMEM)
    k_spec = pl.BlockSpec((1, 1, S, 128), lambda b, g, i: (b, g, 0, 0))
    v_spec = pl.BlockSpec((1, 1, V_ROWS_64, S), lambda b, g, i: (b, g, 0, 0))
    return pl.pallas_call(
        functools.partial(_sel_win_kernel, tq=tq, tk=tk),
        grid=(B, G, S // tq),
        in_specs=[smem,
                  pl.BlockSpec((1, 1, 1, 1, S // tk), lambda b, g, i: (b, g, i, 0, 0),
                               memory_space=pltpu.SMEM),
                  pl.BlockSpec((1, tq, 256), lambda b, g, i: (b, i, g)),
                  pl.BlockSpec((1, NSA_HPG, HEAD_DIM), lambda b, g, i: (g, 0, 0)),
                  k_spec, v_spec, k_spec, v_spec,
                  pl.BlockSpec((1, 1, n_sel, tq), lambda b, g, i: (b, g, 0, i)),
                  pl.BlockSpec((S, n_sel), lambda b, g, i: (0, 0)),
                  pl.BlockSpec((1, 1, 1, HEAD_DIM, cols), lambda b, g, i: (b, g, i, 0, 0)),
                  pl.BlockSpec((1, 1, NSA_HPG * N_GATES, tq), lambda b, g, i: (b, g, 0, i))],
        out_specs=pl.BlockSpec((1, tq, 256), lambda b, g, i: (b, i, g)),
        out_shape=jax.ShapeDtypeStruct((B, S, G * 256), BF16),
        scratch_shapes=[pltpu.VMEM((1, cols), F32), pltpu.VMEM((V_ROWS_64, cols), F32),
                        pltpu.SMEM((S // tk + 1,), jnp.int32)],
        compiler_params=_params(3),
        name="sel_win_attention",
    )(slopes, active, proj, slope_cols, ks, vs_t, kw, vw_t, sel_bias, onehot, o_cmp, gates_t)


def _alibi_slopes_log2(n_heads):
    slopes = np.exp2(-8.0 * (np.arange(n_heads, dtype=np.float32) + 1.0) / n_heads)
    return (slopes.astype(np.float32) * np.float32(LOG2E)).astype(np.float32)


def even_layer_mix(x2d, B, S, norm_g, w_in, lam_q1, lam_k1, lam_q2, lam_k2, subln, layer):
    proj = norm_proj(x2d, norm_g, w_in.astype(BF16), w_in.shape[1], [(0, 512), (1536, 2048)],
                     False).reshape(B, S, -1)
    sb_vt = proj[:, :, 1024:1536].transpose(0, 2, 1)
    df_vt = proj[:, :, 2560:3072].transpose(0, 2, 1).reshape(B, 4, 128, S)
    o_sb = sb_attention(proj, sb_vt)
    lam_rows = jnp.stack([lam_q1, lam_k1, lam_q2, lam_k2]).astype(F32)
    o_df = diff_attention(proj, _with_ones_row(df_vt, V_ROWS_128), lam_rows, subln.astype(F32),
                          _alibi_slopes_log2(4), layer)
    return jnp.concatenate([o_sb, o_df], axis=2).reshape(B * S, -1)


def odd_layer_mix(x2d, B, S, norm_g, w_in, pos_k, k_w1, k_w2, pos_v, v_w1, v_w2, tk=256):
    G, Dh = NSA_GROUPS, HEAD_DIM
    q_width = NSA_GROUPS * NSA_HPG * Dh
    kv_width = G * Dh
    n_main = q_width + 6 * kv_width
    pad = n_main + GATE_PAD - w_in.shape[1]
    w_pad = jnp.pad(w_in, ((0, 0), (0, pad))).astype(BF16)
    proj, gates = norm_proj(x2d, norm_g, w_pad, n_main, [(0, q_width)], True)
    proj = proj.reshape(B, S, n_main)

    def group_major(j):
        cols = proj[:, :, q_width + j * kv_width:q_width + (j + 1) * kv_width]
        return cols.reshape(B, S, G, Dh).transpose(0, 2, 1, 3)

    def keys_with_positions(j):
        pos = jnp.broadcast_to(_key_position_columns(S, tk, Dh), (B, G, S, Dh))
        return jnp.concatenate([group_major(j), pos], axis=-1)

    def values_transposed(j):
        return _with_ones_row(group_major(j).transpose(0, 1, 3, 2), V_ROWS_64)

    n_chunks = S // CMP_STRIDE
    chunks = jnp.stack([group_major(0), group_major(1)]).reshape(2, B * G, n_chunks, CMP_STRIDE * Dh)
    pos_flat = jnp.stack([pos_k, pos_v]).reshape(2, 1, CMP_LEN * Dh).astype(F32)
    w1 = jnp.stack([k_w1, v_w1]).astype(BF16)
    w2 = jnp.stack([k_w2, v_w2]).astype(BF16)
    cmp = compress_kv(chunks, pos_flat, w1, w2).reshape(2, B, G, n_chunks, Dh)

    n_sel = S // SEL_LEN
    cmp_start = jnp.arange(n_chunks) * CMP_STRIDE
    sel_start = jnp.arange(n_sel) * SEL_LEN
    overlap_t = ((cmp_start[None, :] < sel_start[:, None] + SEL_LEN)
                 & (sel_start[:, None] <= cmp_start[None, :] + CMP_LEN - 1)).astype(BF16)
    onehot = (jnp.arange(S)[:, None] // SEL_LEN == jnp.arange(n_sel)[None, :]).astype(BF16)
    slopes = _alibi_slopes_log2(NSA_GROUPS * NSA_HPG)
    slope_cols = _slope_pieces(slopes, Dh).reshape(G, NSA_HPG, Dh)

    o_cmp, sel_bias, hits = cmp_select(proj, cmp[0], cmp[1].transpose(0, 1, 3, 2), overlap_t, slopes, tk)
    active = (hits[..., 0] > 0).astype(jnp.int32)[:, :, :, None, :]
    gates_t = (gates[:, :G * NSA_HPG * N_GATES].reshape(B, S, G, NSA_HPG * N_GATES)
               .transpose(0, 2, 3, 1))
    o = sel_win_attention(proj, active, slope_cols, keys_with_positions(2), values_transposed(3),
                          keys_with_positions(4), values_transposed(5), sel_bias, onehot, o_cmp,
                          gates_t, slopes, tk=tk)
    return o.reshape(B * S, q_width)


def kernel(x, attn_norm, mlp_norm, final_norm, ev_w_in, ev_lam_q1, ev_lam_k1, ev_lam_q2, ev_lam_k2,
           ev_subln, ev_w_out, od_w_in, od_cmp_pos_k, od_cmp_k_w1, od_cmp_k_w2, od_cmp_pos_v,
           od_cmp_v_w1, od_cmp_v_w2, od_w_out, mlp_w1, mlp_w2):
    B, S, D = x.shape
    depth = attn_norm.shape[0]
    x2d = x.reshape(B * S, D)
    for layer in range(depth):
        idx = layer // 2
        if layer % 2 == 0:
            mix = even_layer_mix(x2d, B, S, attn_norm[layer], ev_w_in[idx], ev_lam_q1[idx],
                                 ev_lam_k1[idx], ev_lam_q2[idx], ev_lam_k2[idx], ev_subln[idx], layer)
            w_out = ev_w_out[idx]
        else:
            mix = odd_layer_mix(x2d, B, S, attn_norm[layer], od_w_in[idx], od_cmp_pos_k[idx],
                                od_cmp_k_w1[idx], od_cmp_k_w2[idx], od_cmp_pos_v[idx],
                                od_cmp_v_w1[idx], od_cmp_v_w2[idx])
            w_out = od_w_out[idx]
        g_final = final_norm if layer == depth - 1 else None
        x2d = post_block(mix, x2d, w_out.astype(BF16), mlp_norm[layer], mlp_w1[layer].astype(BF16),
                         mlp_w2[layer].astype(BF16), g_final)
    return x2d.reshape(B, S, D)
```

```python
import functools
import math

import jax
import jax.numpy as jnp
import numpy as np
from jax import lax
from jax.experimental import pallas as pl
from jax.experimental.pallas import tpu as pltpu

F32 = jnp.float32
BF16 = jnp.bfloat16

HEAD_DIM = 64
RMS_EPS = 1e-6
NEG_INF = -1e30
FORCE_SCORE = 1e6
NSA_GROUPS = 4
NSA_HPG = 4
CMP_LEN = 32
CMP_STRIDE = 16
SEL_LEN = 64
SEL_TOPK = 16
WINDOW = 512
N_GATES = 3
GATE_PAD = 128
SEL_DROP = 2.0 ** 24
BF16_EXACT_INT = 256
V_ROWS_64 = 80
V_ROWS_128 = 144

LOG2E = math.log2(math.e)
Q_SCALE = HEAD_DIM ** -0.5 * LOG2E
UNDERFLOW_LOG2 = -160.0

VMEM_LIMIT = 56 * 1024 * 1024


def _params(n_parallel):
    return pltpu.CompilerParams(dimension_semantics=("parallel",) * n_parallel,
                                vmem_limit_bytes=VMEM_LIMIT)


def _rms(x, g):
    ms = jnp.mean(x * x, axis=-1, keepdims=True)
    return x * lax.rsqrt(ms + RMS_EPS) * g


def _dot(a, b):
    return jnp.dot(a, b, preferred_element_type=F32)


def _dot_nt(a, b):
    return lax.dot_general(a, b, (((1,), (1,)), ((), ())), preferred_element_type=F32)


def _split_bf16(x):
    hi = x.astype(BF16)
    lo = (x - hi.astype(F32)).astype(BF16)
    return hi, lo


def _slope_pieces(slopes, width):
    def top_bits(x):
        return (x.view(np.uint32) & np.uint32(0xFFFF0000)).view(np.float32)

    s1 = top_bits(slopes)
    r1 = slopes - s1
    s2 = top_bits(r1)
    s3 = top_bits(r1 - s2)
    out = np.zeros((slopes.shape[0], width), np.float32)
    out[:, :6] = np.stack([s1, s2, s3, s1, s2, s3], axis=1)
    return jnp.asarray(out).astype(BF16)


def _key_position_columns(n, tk, width):
    j = jnp.arange(n) % tk
    a = (j // BF16_EXACT_INT) * BF16_EXACT_INT
    b = j % BF16_EXACT_INT
    cols = jnp.stack([a, a, a, b, b, b], axis=1).astype(BF16)
    return jnp.pad(cols, ((0, 0), (0, width - cols.shape[1])))


def _with_ones_row(v_t, rows):
    d = v_t.shape[-2]
    ones = jnp.ones(v_t.shape[:-2] + (1, v_t.shape[-1]), v_t.dtype)
    pad = jnp.zeros(v_t.shape[:-2] + (rows - d - 1, v_t.shape[-1]), v_t.dtype)
    return jnp.concatenate([v_t, ones, pad], axis=-2)


def _softmax_step(s_t, vt_chunk, kappa, m_ref, acc_ref):
    m_old = m_ref[...]
    m_new = jnp.maximum(m_old, jnp.max(s_t, axis=0, keepdims=True) + kappa)
    alpha = jnp.exp2(m_old - m_new)
    p = jnp.exp2(s_t - (m_new - kappa))
    acc_ref[...] = alpha * acc_ref[...] + _dot(vt_chunk, p.astype(BF16))
    m_ref[...] = m_new


def _norm_proj_kernel(x_ref, g_ref, w_ref, scale_ref, o_ref, *gate_ref, n_main, col_chunk):
    xn = _rms(x_ref[...], g_ref[...]).astype(BF16)
    for c in range(n_main // col_chunk):
        sl = slice(c * col_chunk, (c + 1) * col_chunk)
        o_ref[:, sl] = (_dot(xn, w_ref[:, sl]) * scale_ref[:, sl]).astype(o_ref.dtype)
    if gate_ref:
        logits = _dot(xn, w_ref[:, n_main:n_main + GATE_PAD])
        gate_ref[0][...] = jax.nn.sigmoid(logits)


def norm_proj(x2d, g, w_bf16, n_main, q_cols, with_gates, tm=512, col_chunk=512):
    T, D = x2d.shape
    n_w = w_bf16.shape[1]
    col = jnp.arange(n_main)
    is_q = functools.reduce(jnp.logical_or, [(col >= lo) & (col < hi) for lo, hi in q_cols])
    col_scale = jnp.where(is_q, Q_SCALE, 1.0).astype(F32).reshape(1, n_main)
    out_shape = [jax.ShapeDtypeStruct((T, n_main), BF16)]
    out_specs = [pl.BlockSpec((tm, n_main), lambda i: (i, 0))]
    if with_gates:
        out_shape.append(jax.ShapeDtypeStruct((T, GATE_PAD), F32))
        out_specs.append(pl.BlockSpec((tm, GATE_PAD), lambda i: (i, 0)))
    res = pl.pallas_call(
        functools.partial(_norm_proj_kernel, n_main=n_main, col_chunk=col_chunk),
        grid=(T // tm,),
        in_specs=[pl.BlockSpec((tm, D), lambda i: (i, 0)),
                  pl.BlockSpec((1, D), lambda i: (0, 0)),
                  pl.BlockSpec((D, n_w), lambda i: (0, 0)),
                  pl.BlockSpec((1, n_main), lambda i: (0, 0))],
        out_specs=out_specs,
        out_shape=out_shape,
        compiler_params=_params(1),
        name="norm_proj",
    )(x2d, g.reshape(1, D), w_bf16, col_scale)
    return res if with_gates else res[0]


def _head_pair_rows(q, t):
    lane = lax.broadcasted_iota(jnp.int32, (t, 2 * HEAD_DIM), 1)
    zero = jnp.zeros_like(q)
    return jnp.where(lane < HEAD_DIM, q, zero), jnp.where(lane >= HEAD_DIM, q, zero)


def _sb_kernel(q_ref, k_ref, vt_ref, o_ref, acc_ref, carry_ref, *, t):
    qi = pl.program_id(2)
    cols = 2 * t
    q_both = jnp.concatenate(_head_pair_rows(q_ref[0], t), axis=0)
    s_idx = lax.broadcasted_iota(jnp.int32, (t, 2 * t), 0)
    j_idx = lax.broadcasted_iota(jnp.int32, (t, 2 * t), 1) % t
    upper2 = jnp.where(j_idx > s_idx, 1.0, 0.0).astype(BF16)
    key = lax.broadcasted_iota(jnp.int32, (t, cols), 0)
    qry = lax.broadcasted_iota(jnp.int32, (t, cols), 1) % t
    past_diag = key < qry

    acc_ref[...] = jnp.zeros_like(acc_ref)
    carry_ref[...] = jnp.zeros_like(carry_ref)

    def chunk(kc, masked):
        start = pl.multiple_of(kc * t, t)
        z = _dot_nt(k_ref[0, pl.ds(start, t), :], q_both)
        drop = jnp.maximum(z, 0.0) + jnp.log2(1.0 + jnp.exp2(jnp.abs(z) * -1.0))
        log_beta = z - drop
        if masked:
            drop = jnp.where(past_diag, drop, 0.0)
        hi, lo = _split_bf16(drop)
        tail = _dot(upper2, jnp.concatenate([hi, lo], axis=0))
        carry = carry_ref[...]
        w = jnp.exp2(log_beta - tail - carry)
        if masked:
            w = jnp.where(past_diag, w, 0.0)
        carry = carry + tail[0:1] + drop[0:1]
        carry_ref[...] = carry
        acc_ref[...] += _dot(vt_ref[0, :, pl.ds(start, t)], w.astype(BF16))
        return jnp.min(carry)

    def cond(state):
        j, least_carry = state
        return (j < qi) & (least_carry < -UNDERFLOW_LOG2)

    def body(state):
        j, _ = state
        return j + 1, chunk(qi - 1 - j, False)

    lax.while_loop(cond, body, (0, chunk(qi, True)))
    row = lax.broadcasted_iota(jnp.int32, (2 * HEAD_DIM, t), 0)
    o_t = jnp.where(row < HEAD_DIM, acc_ref[:, :t], acc_ref[:, t:])
    o_ref[0] = o_t.T.astype(o_ref.dtype)


def sb_attention(proj, v_t, t=256):
    B, S, _ = proj.shape
    n_pairs = 4
    return pl.pallas_call(
        functools.partial(_sb_kernel, t=t),
        grid=(B, n_pairs, S // t),
        in_specs=[pl.BlockSpec((1, t, 128), lambda b, p, i: (b, i, p)),
                  pl.BlockSpec((1, S, 128), lambda b, p, i: (b, 0, n_pairs + p)),
                  pl.BlockSpec((1, 128, S), lambda b, p, i: (b, p, 0))],
        out_specs=pl.BlockSpec((1, t, 128), lambda b, p, i: (b, i, p)),
        out_shape=jax.ShapeDtypeStruct((B, S, n_pairs * 128), BF16),
        scratch_shapes=[pltpu.VMEM((128, 2 * t), F32), pltpu.VMEM((1, 2 * t), F32)],
        compiler_params=_params(3),
        name="sb_attention",
    )(proj, proj, v_t)


def _diff_kernel(slopes_ref, lam_ref, q_ref, k_ref, vt_ref, pos_ref, sl_ref, subln_ref, o_ref,
                 m_ref, acc_ref, *, t, out_scale, lambda_init):
    h = pl.program_id(1)
    qi = pl.program_id(2)
    slope = slopes_ref[h]
    cols = 2 * t
    slope_cols = jnp.broadcast_to(sl_ref[0], (t, 128))
    q_both = jnp.concatenate(
        [jnp.concatenate([qc, slope_cols], axis=1) for qc in _head_pair_rows(q_ref[0], t)],
        axis=0)
    pos = pos_ref[...]
    offset = (lax.broadcasted_iota(jnp.int32, (t, cols), 0)
              - lax.broadcasted_iota(jnp.int32, (t, cols), 1) % t)

    m_ref[...] = jnp.full_like(m_ref, NEG_INF)
    acc_ref[...] = jnp.zeros_like(acc_ref)

    def scores(kc):
        start = pl.multiple_of(kc * t, t)
        return _dot_nt(jnp.concatenate([k_ref[0, pl.ds(start, t), :], pos], axis=1), q_both)

    def step(s_t, kc):
        start = pl.multiple_of(kc * t, t)
        kappa = slope * ((kc - qi) * t).astype(F32)
        _softmax_step(s_t, vt_ref[0, 0, :, pl.ds(start, t)], kappa, m_ref, acc_ref)

    def body(kc, s_cur):
        s_next = scores(kc + 1)
        step(s_cur, kc)
        return s_next

    s_diag = lax.fori_loop(0, qi, body, scores(0))
    step(jnp.where(offset <= 0, s_diag, NEG_INF), qi)

    lam_terms = lam_ref[...]
    lam = (jnp.exp(jnp.sum(lam_terms[0:1] * lam_terms[1:2], axis=1, keepdims=True))
           - jnp.exp(jnp.sum(lam_terms[2:3] * lam_terms[3:4], axis=1, keepdims=True))
           + lambda_init)
    o_t = acc_ref[0:128, :] / acc_ref[128:129, :]
    o = (o_t[:, :t] - lam * o_t[:, t:]).T
    o_ref[0] = (_rms(o, subln_ref[...]) * out_scale).astype(o_ref.dtype)


def diff_attention(proj, vt_aug, lam_rows, subln, slopes, layer, t=512):
    B, S, _ = proj.shape
    n_heads = 4
    lambda_init = 0.8 - 0.6 * math.exp(-0.3 * layer)
    smem = pl.BlockSpec(memory_space=pltpu.SMEM)
    pos = _key_position_columns(t, t, 128)
    slope_cols = _slope_pieces(slopes, 128).reshape(n_heads, 1, 128)
    return pl.pallas_call(
        functools.partial(_diff_kernel, t=t, out_scale=1.0 - lambda_init, lambda_init=lambda_init),
        grid=(B, n_heads, S // t),
        in_specs=[smem,
                  pl.BlockSpec((4, HEAD_DIM), lambda b, h, i: (0, 0)),
                  pl.BlockSpec((1, t, 128), lambda b, h, i: (b, i, 12 + h)),
                  pl.BlockSpec((1, S, 128), lambda b, h, i: (b, 0, 16 + h)),
                  pl.BlockSpec((1, 1, V_ROWS_128, S), lambda b, h, i: (b, h, 0, 0)),
                  pl.BlockSpec((t, 128), lambda b, h, i: (0, 0)),
                  pl.BlockSpec((1, 1, 128), lambda b, h, i: (h, 0, 0)),
                  pl.BlockSpec((1, 128), lambda b, h, i: (0, 0))],
        out_specs=pl.BlockSpec((1, t, 128), lambda b, h, i: (b, i, h)),
        out_shape=jax.ShapeDtypeStruct((B, S, n_heads * 128), BF16),
        scratch_shapes=[pltpu.VMEM((1, 2 * t), F32), pltpu.VMEM((V_ROWS_128, 2 * t), F32)],
        compiler_params=_params(3),
        name="diff_attention",
    )(slopes, lam_rows, proj, proj, vt_aug, pos, slope_cols, subln.reshape(1, 128))


def _post_kernel(*refs, ff_chunk, final):
    mix_ref, x_ref, wo_ref, g_ref, w1_ref, w2_ref = refs[:6]
    gf_ref = refs[6] if final else None
    o_ref = refs[-1]
    x = x_ref[...] + _dot(mix_ref[...], wo_ref[...])
    hn = _rms(x, g_ref[...]).astype(BF16)
    acc = x
    for f in range(w1_ref.shape[1] // ff_chunk):
        sl = slice(f * ff_chunk, (f + 1) * ff_chunk)
        hid = jnp.maximum(_dot(hn, w1_ref[:, sl]), 0.0)
        acc = acc + _dot((hid * hid).astype(BF16), w2_ref[sl, :])
    if final:
        acc = _rms(acc, gf_ref[...])
    o_ref[...] = acc


def post_block(mix, x2d, w_out, g_mlp, w1, w2, g_final=None, tm=512, ff_chunk=1024):
    T, D = x2d.shape
    final = g_final is not None
    const = lambda i: (0, 0)
    in_specs = [pl.BlockSpec((tm, mix.shape[1]), lambda i: (i, 0)),
                pl.BlockSpec((tm, D), lambda i: (i, 0)),
                pl.BlockSpec(w_out.shape, const),
                pl.BlockSpec((1, D), const), pl.BlockSpec(w1.shape, const), pl.BlockSpec(w2.shape, const)]
    args = [mix, x2d, w_out, g_mlp.reshape(1, D), w1, w2]
    if final:
        in_specs.append(pl.BlockSpec((1, D), const))
        args.append(g_final.reshape(1, D))
    return pl.pallas_call(
        functools.partial(_post_kernel, ff_chunk=ff_chunk, final=final),
        grid=(T // tm,),
        in_specs=in_specs,
        out_specs=pl.BlockSpec((tm, D), lambda i: (i, 0)),
        out_shape=jax.ShapeDtypeStruct((T, D), F32),
        compiler_params=_params(1),
        name="post_block",
    )(*args)


def _compress_kernel(c_ref, pos_ref, w1_ref, w2_ref, o_ref):
    half = w1_ref.shape[1] // 2
    chunks = c_ref[0, 0]
    pos = jnp.broadcast_to(pos_ref[0], (8, 2 * half)).astype(BF16)
    first = _dot(chunks, w1_ref[0, :half, :])
    second = _dot(chunks, w1_ref[0, half:, :])
    n_chunks = chunks.shape[0]
    pre = first + pltpu.roll(second, n_chunks - 1, 0) + _dot(pos, w1_ref[0])[0:1]
    hid = jax.nn.gelu(pre)
    o_ref[0, 0] = _dot(hid.astype(BF16), w2_ref[0]).astype(o_ref.dtype)


def compress_kv(chunks, pos_flat, w1, w2):
    _, BG, n_chunks, width = chunks.shape
    hidden = w1.shape[-1]
    return pl.pallas_call(
        _compress_kernel,
        grid=(2, BG),
        in_specs=[pl.BlockSpec((1, 1, n_chunks, width), lambda s, i: (s, i, 0, 0)),
                  pl.BlockSpec((1, 1, 2 * width), lambda s, i: (s, 0, 0)),
                  pl.BlockSpec((1, 2 * width, hidden), lambda s, i: (s, 0, 0)),
                  pl.BlockSpec((1, hidden, HEAD_DIM), lambda s, i: (s, 0, 0))],
        out_specs=pl.BlockSpec((1, 1, n_chunks, HEAD_DIM), lambda s, i: (s, i, 0, 0)),
        out_shape=jax.ShapeDtypeStruct((2, BG, n_chunks, HEAD_DIM), BF16),
        compiler_params=_params(2),
        name="compress_kv",
    )(chunks, pos_flat, w1, w2)


def _stack_heads(q):
    return jnp.concatenate([q[:, r * HEAD_DIM:(r + 1) * HEAD_DIM] for r in range(NSA_HPG)], axis=0)


def _slope_row(slopes_ref, g, tq):
    col = lax.broadcasted_iota(jnp.int32, (1, NSA_HPG * tq), 1)
    out = jnp.zeros((1, NSA_HPG * tq), F32)
    for r in range(NSA_HPG):
        out = jnp.where(col // tq == r, slopes_ref[g * NSA_HPG + r], out)
    return out


def _cmp_select_kernel(slopes_ref, q_ref, kc_ref, vct_ref, ovt_ref, ocmp_ref, sel_ref, hits_ref,
                       *, tq, n_sel, blocks_per_chunk, hit_tile):
    g = pl.program_id(1)
    t0 = pl.program_id(2) * tq
    cols = NSA_HPG * tq
    n_cmp = kc_ref.shape[2]
    slope_row = _slope_row(slopes_ref, g, tq)
    q_rows = _stack_heads(q_ref[0])
    tpos = t0 + lax.broadcasted_iota(jnp.int32, (n_cmp, cols), 1) % tq
    cmp_end = lax.broadcasted_iota(jnp.int32, (n_cmp, cols), 0) * CMP_STRIDE + (CMP_LEN - 1)
    dc = (tpos - cmp_end).astype(F32)
    sc = jnp.where(dc >= 0, _dot_nt(kc_ref[0, 0], q_rows) - slope_row * dc, NEG_INF)
    e = jnp.exp2(sc - jnp.max(sc, axis=0, keepdims=True))
    any_valid = jnp.where(dc[0:1] >= 0, 1.0, 0.0)
    pc = e * (any_valid / jnp.sum(e, axis=0, keepdims=True))
    o_t = _dot(vct_ref[0, 0], pc.astype(BF16))
    for r in range(NSA_HPG):
        ocmp_ref[0, 0, r] = o_t[:, r * tq:(r + 1) * tq]

    pc_group = pc[:, 0:tq]
    for r in range(1, NSA_HPG):
        pc_group = pc_group + pc[:, r * tq:(r + 1) * tq]
    hi, lo = _split_bf16(pc_group)
    imp = _dot(ovt_ref[...], hi) + _dot(ovt_ref[...], lo)

    blk = lax.broadcasted_iota(jnp.int32, (n_sel, tq), 0)
    cur = (t0 + lax.broadcasted_iota(jnp.int32, (n_sel, tq), 1)) // SEL_LEN
    forced = (blk == 0) | (blk == cur) | (blk == cur - 1)
    imp = jnp.where(forced, FORCE_SCORE, imp)
    imp = jnp.where(blk <= cur, imp, -1.0)
    picked = jnp.zeros((n_sel, tq), F32)
    for _ in range(min(SEL_TOPK, n_sel)):
        best = jnp.max(imp, axis=0, keepdims=True)
        first = jnp.min(jnp.where(imp == best, blk, n_sel), axis=0, keepdims=True)
        hit = blk == first
        picked = jnp.where(hit, 1.0, picked)
        imp = jnp.where(hit, -2.0, imp)
    keep = (picked > 0.5) & (blk <= cur)
    sel_ref[0, 0] = jnp.where(keep, 0.0, -SEL_DROP).T.astype(sel_ref.dtype)
    n_chunks = n_sel // blocks_per_chunk
    member = (lax.broadcasted_iota(jnp.int32, (n_chunks, n_sel), 1) // blocks_per_chunk
              == lax.broadcasted_iota(jnp.int32, (n_chunks, n_sel), 0))
    per_query = _dot(jnp.where(member, 1.0, 0.0).astype(BF16),
                     jnp.where(keep, 1.0, 0.0).astype(BF16)).astype(BF16)
    for part in range(tq // hit_tile):
        hits_ref[0, 0, part] = _dot(per_query[:, part * hit_tile:(part + 1) * hit_tile],
                                    jnp.ones((hit_tile, 128), BF16))


def cmp_select(proj, kc, vc_t, overlap_t, slopes, tk, hit_tile, tq=256):
    B, S, _ = proj.shape
    n_cmp = kc.shape[2]
    n_sel = S // SEL_LEN
    G = NSA_GROUPS
    smem = pl.BlockSpec(memory_space=pltpu.SMEM)
    return pl.pallas_call(
        functools.partial(_cmp_select_kernel, tq=tq, n_sel=n_sel, blocks_per_chunk=tk // SEL_LEN,
                          hit_tile=hit_tile),
        grid=(B, G, S // tq),
        in_specs=[smem,
                  pl.BlockSpec((1, tq, 256), lambda b, g, i: (b, i, g)),
                  pl.BlockSpec((1, 1, n_cmp, HEAD_DIM), lambda b, g, i: (b, g, 0, 0)),
                  pl.BlockSpec((1, 1, HEAD_DIM, n_cmp), lambda b, g, i: (b, g, 0, 0)),
                  pl.BlockSpec((n_sel, n_cmp), lambda b, g, i: (0, 0))],
        out_specs=[pl.BlockSpec((1, 1, NSA_HPG, HEAD_DIM, tq), lambda b, g, i: (b, g, 0, 0, i)),
                   pl.BlockSpec((1, 1, tq, n_sel), lambda b, g, i: (b, g, i, 0)),
                   pl.BlockSpec((1, 1, tq // hit_tile, S // tk, 128), lambda b, g, i: (b, g, i, 0, 0))],
        out_shape=[jax.ShapeDtypeStruct((B, G, NSA_HPG, HEAD_DIM, S), F32),
                   jax.ShapeDtypeStruct((B, G, S, n_sel), BF16),
                   jax.ShapeDtypeStruct((B, G, S // hit_tile, S // tk, 128), F32)],
        compiler_params=_params(3),
        name="cmp_select",
    )(slopes, proj, kc, vc_t, overlap_t)


def _sel_win_kernel(slopes_ref, active_ref, q_ref, sl_ref, ks_ref, vst_ref, kw_ref, vwt_ref, sel_ref,
                    ocmp_ref, gate_ref, o_ref, m_ref, acc_ref, todo_ref, *, tq, tk):
    g = pl.program_id(1)
    t0 = pl.program_id(2) * tq
    cols = NSA_HPG * tq
    slope_row = _slope_row(slopes_ref, g, tq)
    q = q_ref[0]
    q_aug = jnp.concatenate(
        [jnp.concatenate([q[:, r * HEAD_DIM:(r + 1) * HEAD_DIM],
                          jnp.broadcast_to(sl_ref[0, r:r + 1, :], (tq, HEAD_DIM))], axis=1)
         for r in range(NSA_HPG)], axis=0)
    q_sel = jnp.concatenate([q_aug, jnp.concatenate([sel_ref[0, 0]] * NSA_HPG, axis=0)], axis=1)
    offset = (lax.broadcasted_iota(jnp.int32, (tk, cols), 0)
              - lax.broadcasted_iota(jnp.int32, (tk, cols), 1) % tq)

    def reset():
        m_ref[...] = jnp.full_like(m_ref, NEG_INF)
        acc_ref[...] = jnp.zeros_like(acc_ref)

    def finish():
        return acc_ref[0:HEAD_DIM, :] / acc_ref[HEAD_DIM:HEAD_DIM + 1, :]

    def kappa(c):
        return slope_row * (c * tk - t0).astype(F32)

    def causal(s_t, c):
        return jnp.where(offset + (c * tk - t0) <= 0, s_t, NEG_INF)

    def recent(s_t, c):
        return jnp.where(offset + (c * tk - t0) > -WINDOW, s_t, NEG_INF)

    def sel_scores(c):
        start = pl.multiple_of(c * tk, tk)
        return _dot_nt(ks_ref[0, 0, pl.ds(start, tk), :], q_sel)

    def win_scores(c):
        start = pl.multiple_of(c * tk, tk)
        return _dot_nt(kw_ref[0, 0, pl.ds(start, tk), :], q_aug)

    def step(s_t, c, vt_ref):
        start = pl.multiple_of(c * tk, tk)
        _softmax_step(s_t, vt_ref[0, 0, :, pl.ds(start, tk)], kappa(c), m_ref, acc_ref)

    last = t0 // tk

    def note_active(c, n):
        hit = active_ref[0, 0, 0, 0, c] > 0

        @pl.when(hit)
        def _():
            todo_ref[n] = c

        return n + hit.astype(jnp.int32)

    n_todo = lax.fori_loop(0, last, note_active, 0)
    todo_ref[n_todo] = last

    def sel_body(i, s_cur):
        s_next = sel_scores(todo_ref[i + 1])
        step(s_cur, todo_ref[i], vst_ref)
        return s_next

    reset()
    s_last = lax.fori_loop(0, n_todo, sel_body, sel_scores(todo_ref[0]))
    step(causal(s_last, last), last, vst_ref)
    o_sel = finish()

    first = jnp.maximum(t0 - (WINDOW - 1), 0) // tk

    def win_body(c, s_cur):
        s_next = win_scores(c + 1)
        step(recent(s_cur, c), c, vwt_ref)
        return s_next

    reset()
    s_last = lax.fori_loop(first, last, win_body, win_scores(first))
    step(causal(s_last, last), last, vwt_ref)
    o_win = finish()

    gates = gate_ref[0, 0]
    outs = []
    for r in range(NSA_HPG):
        cs = slice(r * tq, (r + 1) * tq)
        outs.append(gates[3 * r:3 * r + 1] * ocmp_ref[0, 0, r] + gates[3 * r + 1:3 * r + 2] * o_sel[:, cs]
                    + gates[3 * r + 2:3 * r + 3] * o_win[:, cs])
    o_ref[0] = jnp.concatenate(outs, axis=0).T.astype(o_ref.dtype)


def sel_win_attention(proj, active, slope_cols, ks, vs_t, kw, vw_t, sel_bias, o_cmp, gates_t,
                      slopes, tq=128, tk=256):
    B, S, _ = proj.shape
    assert tk + tq <= WINDOW
    G = NSA_GROUPS
    n_sel = S // SEL_LEN
    cols = NSA_HPG * tq
    smem = pl.BlockSpec(memory_space=pltpu.SMEM)
    ks_spec = pl.BlockSpec((1, 1, S, ks.shape[-1]), lambda b, g, i: (b, g, 0, 0))
    kw_spec = pl.BlockSpec((1, 1, S, kw.shape[-1]), lambda b, g, i: (b, g, 0, 0))
    v_spec = pl.BlockSpec((1, 1, V_ROWS_64, S), lambda b, g, i: (b, g, 0, 0))
    return pl.pallas_call(
        functools.partial(_sel_win_kernel, tq=tq, tk=tk),
        grid=(B, G, S // tq),
        in_specs=[smem,
                  pl.BlockSpec((1, 1, 1, 1, S // tk), lambda b, g, i: (b, g, i, 0, 0),
                               memory_space=pltpu.SMEM),
                  pl.BlockSpec((1, tq, 256), lambda b, g, i: (b, i, g)),
                  pl.BlockSpec((1, NSA_HPG, HEAD_DIM), lambda b, g, i: (g, 0, 0)),
                  ks_spec, v_spec, kw_spec, v_spec,
                  pl.BlockSpec((1, 1, tq, n_sel), lambda b, g, i: (b, g, i, 0)),
                  pl.BlockSpec((1, 1, NSA_HPG, HEAD_DIM, tq), lambda b, g, i: (b, g, 0, 0, i)),
                  pl.BlockSpec((1, 1, NSA_HPG * N_GATES, tq), lambda b, g, i: (b, g, 0, i))],
        out_specs=pl.BlockSpec((1, tq, 256), lambda b, g, i: (b, i, g)),
        out_shape=jax.ShapeDtypeStruct((B, S, G * 256), BF16),
        scratch_shapes=[pltpu.VMEM((1, cols), F32), pltpu.VMEM((V_ROWS_64, cols), F32),
                        pltpu.SMEM((S // tk + 1,), jnp.int32)],
        compiler_params=_params(3),
        name="sel_win_attention",
    )(slopes, active, proj, slope_cols, ks, vs_t, kw, vw_t, sel_bias, o_cmp, gates_t)


def _alibi_slopes_log2(n_heads):
    slopes = np.exp2(-8.0 * (np.arange(n_heads, dtype=np.float32) + 1.0) / n_heads)
    return (slopes.astype(np.float32) * np.float32(LOG2E)).astype(np.float32)


def even_layer_mix(x2d, B, S, norm_g, w_in, lam_q1, lam_k1, lam_q2, lam_k2, subln, layer):
    proj = norm_proj(x2d, norm_g, w_in.astype(BF16), w_in.shape[1], [(0, 512), (1536, 2048)],
                     False).reshape(B, S, -1)
    sb_vt = proj[:, :, 1024:1536].transpose(0, 2, 1)
    df_vt = proj[:, :, 2560:3072].transpose(0, 2, 1).reshape(B, 4, 128, S)
    o_sb = sb_attention(proj, sb_vt)
    lam_rows = jnp.stack([lam_q1, lam_k1, lam_q2, lam_k2]).astype(F32)
    o_df = diff_attention(proj, _with_ones_row(df_vt, V_ROWS_128), lam_rows, subln.astype(F32),
                          _alibi_slopes_log2(4), layer)
    return jnp.concatenate([o_sb, o_df], axis=2).reshape(B * S, -1)


def odd_layer_mix(x2d, B, S, norm_g, w_in, pos_k, k_w1, k_w2, pos_v, v_w1, v_w2, tk=256):
    G, Dh = NSA_GROUPS, HEAD_DIM
    q_width = NSA_GROUPS * NSA_HPG * Dh
    kv_width = G * Dh
    n_main = q_width + 6 * kv_width
    pad = n_main + GATE_PAD - w_in.shape[1]
    w_pad = jnp.pad(w_in, ((0, 0), (0, pad))).astype(BF16)
    proj, gates = norm_proj(x2d, norm_g, w_pad, n_main, [(0, q_width)], True)
    proj = proj.reshape(B, S, n_main)

    def group_major(j):
        cols = proj[:, :, q_width + j * kv_width:q_width + (j + 1) * kv_width]
        return cols.reshape(B, S, G, Dh).transpose(0, 2, 1, 3)

    def keys_with_positions(j, extra=()):
        cols = [_key_position_columns(S, tk, Dh)] + list(extra)
        return jnp.concatenate([group_major(j)] + [jnp.broadcast_to(c, (B, G) + c.shape) for c in cols],
                               axis=-1)

    def values_transposed(j):
        return _with_ones_row(group_major(j).transpose(0, 1, 3, 2), V_ROWS_64)

    n_chunks = S // CMP_STRIDE
    chunks = jnp.stack([group_major(0), group_major(1)]).reshape(2, B * G, n_chunks, CMP_STRIDE * Dh)
    pos_flat = jnp.stack([pos_k, pos_v]).reshape(2, 1, CMP_LEN * Dh).astype(F32)
    w1 = jnp.stack([k_w1, v_w1]).astype(BF16)
    w2 = jnp.stack([k_w2, v_w2]).astype(BF16)
    cmp = compress_kv(chunks, pos_flat, w1, w2).reshape(2, B, G, n_chunks, Dh)

    n_sel = S // SEL_LEN
    cmp_start = jnp.arange(n_chunks) * CMP_STRIDE
    sel_start = jnp.arange(n_sel) * SEL_LEN
    overlap_t = ((cmp_start[None, :] < sel_start[:, None] + SEL_LEN)
                 & (sel_start[:, None] <= cmp_start[None, :] + CMP_LEN - 1)).astype(BF16)
    onehot = (jnp.arange(S)[:, None] // SEL_LEN == jnp.arange(n_sel)[None, :]).astype(BF16)
    slopes = _alibi_slopes_log2(NSA_GROUPS * NSA_HPG)
    slope_cols = _slope_pieces(slopes, Dh).reshape(G, NSA_HPG, Dh)

    tq = 256
    o_cmp, sel_bias, hits = cmp_select(proj, cmp[0], cmp[1].transpose(0, 1, 3, 2), overlap_t, slopes,
                                       tk, tq)
    active = (hits[..., 0] > 0).astype(jnp.int32)[:, :, :, None, :]
    gates_t = (gates[:, :G * NSA_HPG * N_GATES].reshape(B, S, G, NSA_HPG * N_GATES)
               .transpose(0, 2, 3, 1))
    o = sel_win_attention(proj, active, slope_cols, keys_with_positions(2, [onehot]),
                          values_transposed(3), keys_with_positions(4), values_transposed(5),
                          sel_bias, o_cmp, gates_t, slopes, tq=tq, tk=tk)
    return o.reshape(B * S, q_width)


def kernel(x, attn_norm, mlp_norm, final_norm, ev_w_in, ev_lam_q1, ev_lam_k1, ev_lam_q2, ev_lam_k2,
           ev_subln, ev_w_out, od_w_in, od_cmp_pos_k, od_cmp_k_w1, od_cmp_k_w2, od_cmp_pos_v,
           od_cmp_v_w1, od_cmp_v_w2, od_w_out, mlp_w1, mlp_w2):
    B, S, D = x.shape
    depth = attn_norm.shape[0]
    x2d = x.reshape(B * S, D)
    for layer in range(depth):
        idx = layer // 2
        if layer % 2 == 0:
            mix = even_layer_mix(x2d, B, S, attn_norm[layer], ev_w_in[idx], ev_lam_q1[idx],
                                 ev_lam_k1[idx], ev_lam_q2[idx], ev_lam_k2[idx], ev_subln[idx], layer)
            w_out = ev_w_out[idx]
        else:
            mix = odd_layer_mix(x2d, B, S, attn_norm[layer], od_w_in[idx], od_cmp_pos_k[idx],
                                od_cmp_k_w1[idx], od_cmp_k_w2[idx], od_cmp_pos_v[idx],
                                od_cmp_v_w1[idx], od_cmp_v_w2[idx])
            w_out = od_w_out[idx]
        g_final = final_norm if layer == depth - 1 else None
        x2d = post_block(mix, x2d, w_out.astype(BF16), mlp_norm[layer], mlp_w1[layer].astype(BF16),
                         mlp_w2[layer].astype(BF16), g_final)
    return x2d.reshape(B, S, D)
```

```python
import functools
import math

import jax
import jax.numpy as jnp
import numpy as np
from jax import lax
from jax.experimental import pallas as pl
from jax.experimental.pallas import tpu as pltpu

F32 = jnp.float32
BF16 = jnp.bfloat16

HEAD_DIM = 64
RMS_EPS = 1e-6
NEG_INF = -1e30
FORCE_SCORE = 1e6
NSA_GROUPS = 4
NSA_HPG = 4
CMP_LEN = 32
CMP_STRIDE = 16
SEL_LEN = 64
SEL_TOPK = 16
WINDOW = 512
N_GATES = 3
GATE_PAD = 128
SEL_DROP = 2.0 ** 24
BF16_EXACT_INT = 256
V_ROWS_64 = 80
V_ROWS_128 = 144

LOG2E = math.log2(math.e)
Q_SCALE = HEAD_DIM ** -0.5 * LOG2E
UNDERFLOW_LOG2 = -160.0

VMEM_LIMIT = 56 * 1024 * 1024


def _params(n_parallel):
    return pltpu.CompilerParams(dimension_semantics=("parallel",) * n_parallel,
                                vmem_limit_bytes=VMEM_LIMIT)


def _rms(x, g):
    ms = jnp.mean(x * x, axis=-1, keepdims=True)
    return x * lax.rsqrt(ms + RMS_EPS) * g


def _dot(a, b):
    return jnp.dot(a, b, preferred_element_type=F32)


def _dot_nt(a, b):
    return lax.dot_general(a, b, (((1,), (1,)), ((), ())), preferred_element_type=F32)


def _split_bf16(x):
    hi = x.astype(BF16)
    lo = (x - hi.astype(F32)).astype(BF16)
    return hi, lo


def _slope_pieces(slopes, width):
    def top_bits(x):
        return (x.view(np.uint32) & np.uint32(0xFFFF0000)).view(np.float32)

    s1 = top_bits(slopes)
    r1 = slopes - s1
    s2 = top_bits(r1)
    s3 = top_bits(r1 - s2)
    out = np.zeros((slopes.shape[0], width), np.float32)
    out[:, :6] = np.stack([s1, s2, s3, s1, s2, s3], axis=1)
    return jnp.asarray(out).astype(BF16)


def _key_position_columns(n, tk, width):
    j = jnp.arange(n) % tk
    a = (j // BF16_EXACT_INT) * BF16_EXACT_INT
    b = j % BF16_EXACT_INT
    cols = jnp.stack([a, a, a, b, b, b], axis=1).astype(BF16)
    return jnp.pad(cols, ((0, 0), (0, width - cols.shape[1])))


def _with_ones_row(v_t, rows):
    d = v_t.shape[-2]
    ones = jnp.ones(v_t.shape[:-2] + (1, v_t.shape[-1]), v_t.dtype)
    pad = jnp.zeros(v_t.shape[:-2] + (rows - d - 1, v_t.shape[-1]), v_t.dtype)
    return jnp.concatenate([v_t, ones, pad], axis=-2)


def _online_softmax(n_steps, chunk_id, final_chunk, scores, values, kappa, mask, final_mask,
                    m_ref, acc_ref):
    def block(chunks, masks):
        s_all = [msk(scores(c)) for c, msk in zip(chunks, masks)]
        m_run = m_ref[...]
        acc = acc_ref[...]
        for c, s_t in zip(chunks, s_all):
            kap = kappa(c)
            m_new = jnp.maximum(m_run, jnp.max(s_t, axis=0, keepdims=True) + kap)
            p = jnp.exp2(s_t - (m_new - kap)).astype(BF16)
            acc = jnp.exp2(m_run - m_new) * acc + _dot(values(c), p)
            m_run = m_new
        m_ref[...] = m_run
        acc_ref[...] = acc

    def run(group, first_step, n_groups):
        def body(i, carry):
            chunks = [chunk_id(first_step + group * i + j) for j in range(group)]
            block(chunks, [functools.partial(lambda s_t, c: mask(s_t, c), c=c) for c in chunks])
            return carry

        lax.fori_loop(0, n_groups, body, 0)
        return first_step + group * n_groups

    done = run(4, 0, n_steps // 4)
    done = run(2, done, (n_steps - done) // 2)
    run(1, done, n_steps - done)
    block([final_chunk], [final_mask])


def _norm_proj_kernel(x_ref, g_ref, w_ref, scale_ref, o_ref, *gate_ref, n_main, col_chunk):
    xn = _rms(x_ref[...], g_ref[...]).astype(BF16)
    for c in range(n_main // col_chunk):
        sl = slice(c * col_chunk, (c + 1) * col_chunk)
        o_ref[:, sl] = (_dot(xn, w_ref[:, sl]) * scale_ref[:, sl]).astype(o_ref.dtype)
    if gate_ref:
        logits = _dot(xn, w_ref[:, n_main:n_main + GATE_PAD])
        gate_ref[0][...] = jax.nn.sigmoid(logits)


def norm_proj(x2d, g, w_bf16, n_main, q_cols, with_gates, tm=512, col_chunk=512):
    T, D = x2d.shape
    n_w = w_bf16.shape[1]
    col = jnp.arange(n_main)
    is_q = functools.reduce(jnp.logical_or, [(col >= lo) & (col < hi) for lo, hi in q_cols])
    col_scale = jnp.where(is_q, Q_SCALE, 1.0).astype(F32).reshape(1, n_main)
    out_shape = [jax.ShapeDtypeStruct((T, n_main), BF16)]
    out_specs = [pl.BlockSpec((tm, n_main), lambda i: (i, 0))]
    if with_gates:
        out_shape.append(jax.ShapeDtypeStruct((T, GATE_PAD), F32))
        out_specs.append(pl.BlockSpec((tm, GATE_PAD), lambda i: (i, 0)))
    res = pl.pallas_call(
        functools.partial(_norm_proj_kernel, n_main=n_main, col_chunk=col_chunk),
        grid=(T // tm,),
        in_specs=[pl.BlockSpec((tm, D), lambda i: (i, 0)),
                  pl.BlockSpec((1, D), lambda i: (0, 0)),
                  pl.BlockSpec((D, n_w), lambda i: (0, 0)),
                  pl.BlockSpec((1, n_main), lambda i: (0, 0))],
        out_specs=out_specs,
        out_shape=out_shape,
        compiler_params=_params(1),
        name="norm_proj",
    )(x2d, g.reshape(1, D), w_bf16, col_scale)
    return res if with_gates else res[0]


def _head_pair_rows(q, t):
    lane = lax.broadcasted_iota(jnp.int32, (t, 2 * HEAD_DIM), 1)
    zero = jnp.zeros_like(q)
    return jnp.where(lane < HEAD_DIM, q, zero), jnp.where(lane >= HEAD_DIM, q, zero)


def _sb_kernel(q_ref, k_ref, vt_ref, o_ref, acc_ref, carry_ref, *, t):
    qi = pl.program_id(2)
    cols = 2 * t
    q_both = jnp.concatenate(_head_pair_rows(q_ref[0], t), axis=0)
    s_idx = lax.broadcasted_iota(jnp.int32, (t, 2 * t), 0)
    j_idx = lax.broadcasted_iota(jnp.int32, (t, 2 * t), 1) % t
    upper2 = jnp.where(j_idx > s_idx, 1.0, 0.0).astype(BF16)
    key = lax.broadcasted_iota(jnp.int32, (t, cols), 0)
    qry = lax.broadcasted_iota(jnp.int32, (t, cols), 1) % t
    past_diag = key < qry

    acc_ref[...] = jnp.zeros_like(acc_ref)
    carry_ref[...] = jnp.zeros_like(carry_ref)

    def chunk(kc, masked):
        start = pl.multiple_of(kc * t, t)
        z = _dot_nt(k_ref[0, pl.ds(start, t), :], q_both)
        drop = jnp.maximum(z, 0.0) + jnp.log2(1.0 + jnp.exp2(jnp.abs(z) * -1.0))
        log_beta = z - drop
        if masked:
            drop = jnp.where(past_diag, drop, 0.0)
        hi, lo = _split_bf16(drop)
        tail = _dot(upper2, jnp.concatenate([hi, lo], axis=0))
        carry = carry_ref[...]
        w = jnp.exp2(log_beta - tail - carry)
        if masked:
            w = jnp.where(past_diag, w, 0.0)
        carry = carry + tail[0:1] + drop[0:1]
        carry_ref[...] = carry
        acc_ref[...] += _dot(vt_ref[0, :, pl.ds(start, t)], w.astype(BF16))
        return jnp.min(carry)

    def cond(state):
        j, least_carry = state
        return (j < qi) & (least_carry < -UNDERFLOW_LOG2)

    def body(state):
        j, _ = state
        return j + 1, chunk(qi - 1 - j, False)

    lax.while_loop(cond, body, (0, chunk(qi, True)))
    row = lax.broadcasted_iota(jnp.int32, (2 * HEAD_DIM, t), 0)
    o_t = jnp.where(row < HEAD_DIM, acc_ref[:, :t], acc_ref[:, t:])
    o_ref[0] = o_t.T.astype(o_ref.dtype)


def sb_attention(proj, v_t, t=256):
    B, S, _ = proj.shape
    n_pairs = 4
    return pl.pallas_call(
        functools.partial(_sb_kernel, t=t),
        grid=(B, n_pairs, S // t),
        in_specs=[pl.BlockSpec((1, t, 128), lambda b, p, i: (b, i, p)),
                  pl.BlockSpec((1, S, 128), lambda b, p, i: (b, 0, n_pairs + p)),
                  pl.BlockSpec((1, 128, S), lambda b, p, i: (b, p, 0))],
        out_specs=pl.BlockSpec((1, t, 128), lambda b, p, i: (b, i, p)),
        out_shape=jax.ShapeDtypeStruct((B, S, n_pairs * 128), BF16),
        scratch_shapes=[pltpu.VMEM((128, 2 * t), F32), pltpu.VMEM((1, 2 * t), F32)],
        compiler_params=_params(3),
        name="sb_attention",
    )(proj, proj, v_t)


def _diff_kernel(slopes_ref, lam_ref, q_ref, k_ref, vt_ref, pos_ref, sl_ref, subln_ref, o_ref,
                 m_ref, acc_ref, *, t, out_scale, lambda_init):
    h = pl.program_id(1)
    qi = pl.program_id(2)
    slope = slopes_ref[h]
    cols = 2 * t
    slope_cols = jnp.broadcast_to(sl_ref[0], (t, 128))
    q_both = jnp.concatenate(
        [jnp.concatenate([qc, slope_cols], axis=1) for qc in _head_pair_rows(q_ref[0], t)],
        axis=0)
    pos = pos_ref[...]
    offset = (lax.broadcasted_iota(jnp.int32, (t, cols), 0)
              - lax.broadcasted_iota(jnp.int32, (t, cols), 1) % t)

    m_ref[...] = jnp.full_like(m_ref, NEG_INF)
    acc_ref[...] = jnp.zeros_like(acc_ref)

    def scores(kc):
        start = pl.multiple_of(kc * t, t)
        return _dot_nt(jnp.concatenate([k_ref[0, pl.ds(start, t), :], pos], axis=1), q_both)

    _online_softmax(
        qi, lambda i: i, qi, scores,
        values=lambda kc: vt_ref[0, 0, :, pl.ds(pl.multiple_of(kc * t, t), t)],
        kappa=lambda kc: slope * ((kc - qi) * t).astype(F32),
        mask=lambda s_t, kc: s_t,
        final_mask=lambda s_t: jnp.where(offset <= 0, s_t, NEG_INF),
        m_ref=m_ref, acc_ref=acc_ref)

    lam_terms = lam_ref[...]
    lam = (jnp.exp(jnp.sum(lam_terms[0:1] * lam_terms[1:2], axis=1, keepdims=True))
           - jnp.exp(jnp.sum(lam_terms[2:3] * lam_terms[3:4], axis=1, keepdims=True))
           + lambda_init)
    o_t = acc_ref[0:128, :] / acc_ref[128:129, :]
    o = (o_t[:, :t] - lam * o_t[:, t:]).T
    o_ref[0] = (_rms(o, subln_ref[...]) * out_scale).astype(o_ref.dtype)


def diff_attention(proj, vt_aug, lam_rows, subln, slopes, layer, t=512):
    B, S, _ = proj.shape
    n_heads = 4
    lambda_init = 0.8 - 0.6 * math.exp(-0.3 * layer)
    smem = pl.BlockSpec(memory_space=pltpu.SMEM)
    pos = _key_position_columns(t, t, 128)
    slope_cols = _slope_pieces(slopes, 128).reshape(n_heads, 1, 128)
    return pl.pallas_call(
        functools.partial(_diff_kernel, t=t, out_scale=1.0 - lambda_init, lambda_init=lambda_init),
        grid=(B, n_heads, S // t),
        in_specs=[smem,
                  pl.BlockSpec((4, HEAD_DIM), lambda b, h, i: (0, 0)),
                  pl.BlockSpec((1, t, 128), lambda b, h, i: (b, i, 12 + h)),
                  pl.BlockSpec((1, S, 128), lambda b, h, i: (b, 0, 16 + h)),
                  pl.BlockSpec((1, 1, V_ROWS_128, S), lambda b, h, i: (b, h, 0, 0)),
                  pl.BlockSpec((t, 128), lambda b, h, i: (0, 0)),
                  pl.BlockSpec((1, 1, 128), lambda b, h, i: (h, 0, 0)),
                  pl.BlockSpec((1, 128), lambda b, h, i: (0, 0))],
        out_specs=pl.BlockSpec((1, t, 128), lambda b, h, i: (b, i, h)),
        out_shape=jax.ShapeDtypeStruct((B, S, n_heads * 128), BF16),
        scratch_shapes=[pltpu.VMEM((1, 2 * t), F32), pltpu.VMEM((V_ROWS_128, 2 * t), F32)],
        compiler_params=_params(3),
        name="diff_attention",
    )(slopes, lam_rows, proj, proj, vt_aug, pos, slope_cols, subln.reshape(1, 128))


def _post_kernel(*refs, ff_chunk, final):
    mix_ref, x_ref, wo_ref, g_ref, w1_ref, w2_ref = refs[:6]
    gf_ref = refs[6] if final else None
    o_ref = refs[-1]
    x = x_ref[...] + _dot(mix_ref[...], wo_ref[...])
    hn = _rms(x, g_ref[...]).astype(BF16)
    acc = x
    for f in range(w1_ref.shape[1] // ff_chunk):
        sl = slice(f * ff_chunk, (f + 1) * ff_chunk)
        hid = jnp.maximum(_dot(hn, w1_ref[:, sl]), 0.0)
        acc = acc + _dot((hid * hid).astype(BF16), w2_ref[sl, :])
    if final:
        acc = _rms(acc, gf_ref[...])
    o_ref[...] = acc


def post_block(mix, x2d, w_out, g_mlp, w1, w2, g_final=None, tm=512, ff_chunk=1024):
    T, D = x2d.shape
    final = g_final is not None
    const = lambda i: (0, 0)
    in_specs = [pl.BlockSpec((tm, mix.shape[1]), lambda i: (i, 0)),
                pl.BlockSpec((tm, D), lambda i: (i, 0)),
                pl.BlockSpec(w_out.shape, const),
                pl.BlockSpec((1, D), const), pl.BlockSpec(w1.shape, const), pl.BlockSpec(w2.shape, const)]
    args = [mix, x2d, w_out, g_mlp.reshape(1, D), w1, w2]
    if final:
        in_specs.append(pl.BlockSpec((1, D), const))
        args.append(g_final.reshape(1, D))
    return pl.pallas_call(
        functools.partial(_post_kernel, ff_chunk=ff_chunk, final=final),
        grid=(T // tm,),
        in_specs=in_specs,
        out_specs=pl.BlockSpec((tm, D), lambda i: (i, 0)),
        out_shape=jax.ShapeDtypeStruct((T, D), F32),
        compiler_params=_params(1),
        name="post_block",
    )(*args)


def _compress_kernel(c_ref, pos_ref, w1_ref, w2_ref, o_ref):
    half = w1_ref.shape[1] // 2
    chunks = c_ref[0, 0]
    pos = jnp.broadcast_to(pos_ref[0], (8, 2 * half)).astype(BF16)
    first = _dot(chunks, w1_ref[0, :half, :])
    second = _dot(chunks, w1_ref[0, half:, :])
    n_chunks = chunks.shape[0]
    pre = first + pltpu.roll(second, n_chunks - 1, 0) + _dot(pos, w1_ref[0])[0:1]
    hid = jax.nn.gelu(pre)
    o_ref[0, 0] = _dot(hid.astype(BF16), w2_ref[0]).astype(o_ref.dtype)


def compress_kv(chunks, pos_flat, w1, w2):
    _, BG, n_chunks, width = chunks.shape
    hidden = w1.shape[-1]
    return pl.pallas_call(
        _compress_kernel,
        grid=(2, BG),
        in_specs=[pl.BlockSpec((1, 1, n_chunks, width), lambda s, i: (s, i, 0, 0)),
                  pl.BlockSpec((1, 1, 2 * width), lambda s, i: (s, 0, 0)),
                  pl.BlockSpec((1, 2 * width, hidden), lambda s, i: (s, 0, 0)),
                  pl.BlockSpec((1, hidden, HEAD_DIM), lambda s, i: (s, 0, 0))],
        out_specs=pl.BlockSpec((1, 1, n_chunks, HEAD_DIM), lambda s, i: (s, i, 0, 0)),
        out_shape=jax.ShapeDtypeStruct((2, BG, n_chunks, HEAD_DIM), BF16),
        compiler_params=_params(2),
        name="compress_kv",
    )(chunks, pos_flat, w1, w2)


def _stack_heads(q):
    return jnp.concatenate([q[:, r * HEAD_DIM:(r + 1) * HEAD_DIM] for r in range(NSA_HPG)], axis=0)


def _slope_row(slopes_ref, g, tq):
    col = lax.broadcasted_iota(jnp.int32, (1, NSA_HPG * tq), 1)
    out = jnp.zeros((1, NSA_HPG * tq), F32)
    for r in range(NSA_HPG):
        out = jnp.where(col // tq == r, slopes_ref[g * NSA_HPG + r], out)
    return out


def _cmp_select_kernel(slopes_ref, q_ref, kc_ref, vct_ref, ovt_ref, ocmp_ref, sel_ref, hits_ref,
                       *, tq, n_sel, blocks_per_chunk, hit_tile):
    g = pl.program_id(1)
    t0 = pl.program_id(2) * tq
    cols = NSA_HPG * tq
    n_cmp = kc_ref.shape[2]
    slope_row = _slope_row(slopes_ref, g, tq)
    q_rows = _stack_heads(q_ref[0])
    tpos = t0 + lax.broadcasted_iota(jnp.int32, (n_cmp, cols), 1) % tq
    cmp_end = lax.broadcasted_iota(jnp.int32, (n_cmp, cols), 0) * CMP_STRIDE + (CMP_LEN - 1)
    dc = (tpos - cmp_end).astype(F32)
    sc = jnp.where(dc >= 0, _dot_nt(kc_ref[0, 0], q_rows) - slope_row * dc, NEG_INF)
    e = jnp.exp2(sc - jnp.max(sc, axis=0, keepdims=True))
    any_valid = jnp.where(dc[0:1] >= 0, 1.0, 0.0)
    pc = e * (any_valid / jnp.sum(e, axis=0, keepdims=True))
    o_t = _dot(vct_ref[0, 0], pc.astype(BF16))
    for r in range(NSA_HPG):
        ocmp_ref[0, 0, r] = o_t[:, r * tq:(r + 1) * tq]

    pc_group = pc[:, 0:tq]
    for r in range(1, NSA_HPG):
        pc_group = pc_group + pc[:, r * tq:(r + 1) * tq]
    hi, lo = _split_bf16(pc_group)
    imp = _dot(ovt_ref[...], hi) + _dot(ovt_ref[...], lo)

    blk = lax.broadcasted_iota(jnp.int32, (n_sel, tq), 0)
    cur = (t0 + lax.broadcasted_iota(jnp.int32, (n_sel, tq), 1)) // SEL_LEN
    forced = (blk == 0) | (blk == cur) | (blk == cur - 1)
    imp = jnp.where(forced, FORCE_SCORE, imp)
    imp = jnp.where(blk <= cur, imp, -1.0)
    picked = jnp.zeros((n_sel, tq), F32)
    for _ in range(min(SEL_TOPK, n_sel)):
        best = jnp.max(imp, axis=0, keepdims=True)
        first = jnp.min(jnp.where(imp == best, blk, n_sel), axis=0, keepdims=True)
        hit = blk == first
        picked = jnp.where(hit, 1.0, picked)
        imp = jnp.where(hit, -2.0, imp)
    keep = (picked > 0.5) & (blk <= cur)
    sel_ref[0, 0] = jnp.where(keep, 0.0, -SEL_DROP).T.astype(sel_ref.dtype)
    n_chunks = n_sel // blocks_per_chunk
    member = (lax.broadcasted_iota(jnp.int32, (n_chunks, n_sel), 1) // blocks_per_chunk
              == lax.broadcasted_iota(jnp.int32, (n_chunks, n_sel), 0))
    per_query = _dot(jnp.where(member, 1.0, 0.0).astype(BF16),
                     jnp.where(keep, 1.0, 0.0).astype(BF16)).astype(BF16)
    for part in range(tq // hit_tile):
        hits_ref[0, 0, part] = _dot(per_query[:, part * hit_tile:(part + 1) * hit_tile],
                                    jnp.ones((hit_tile, 128), BF16))


def cmp_select(proj, kc, vc_t, overlap_t, slopes, tk, hit_tile, tq=256):
    B, S, _ = proj.shape
    n_cmp = kc.shape[2]
    n_sel = S // SEL_LEN
    G = NSA_GROUPS
    smem = pl.BlockSpec(memory_space=pltpu.SMEM)
    return pl.pallas_call(
        functools.partial(_cmp_select_kernel, tq=tq, n_sel=n_sel, blocks_per_chunk=tk // SEL_LEN,
                          hit_tile=hit_tile),
        grid=(B, G, S // tq),
        in_specs=[smem,
                  pl.BlockSpec((1, tq, 256), lambda b, g, i: (b, i, g)),
                  pl.BlockSpec((1, 1, n_cmp, HEAD_DIM), lambda b, g, i: (b, g, 0, 0)),
                  pl.BlockSpec((1, 1, HEAD_DIM, n_cmp), lambda b, g, i: (b, g, 0, 0)),
                  pl.BlockSpec((n_sel, n_cmp), lambda b, g, i: (0, 0))],
        out_specs=[pl.BlockSpec((1, 1, NSA_HPG, HEAD_DIM, tq), lambda b, g, i: (b, g, 0, 0, i)),
                   pl.BlockSpec((1, 1, tq, n_sel), lambda b, g, i: (b, g, i, 0)),
                   pl.BlockSpec((1, 1, tq // hit_tile, S // tk, 128), lambda b, g, i: (b, g, i, 0, 0))],
        out_shape=[jax.ShapeDtypeStruct((B, G, NSA_HPG, HEAD_DIM, S), F32),
                   jax.ShapeDtypeStruct((B, G, S, n_sel), BF16),
                   jax.ShapeDtypeStruct((B, G, S // hit_tile, S // tk, 128), F32)],
        compiler_params=_params(3),
        name="cmp_select",
    )(slopes, proj, kc, vc_t, overlap_t)


def _sel_win_kernel(slopes_ref, active_ref, q_ref, sl_ref, ks_ref, vst_ref, kw_ref, vwt_ref, sel_ref,
                    ocmp_ref, gate_ref, o_ref, m_ref, acc_ref, todo_ref, *, tq, tk):
    g = pl.program_id(1)
    t0 = pl.program_id(2) * tq
    cols = NSA_HPG * tq
    slope_row = _slope_row(slopes_ref, g, tq)
    q = q_ref[0]
    q_aug = jnp.concatenate(
        [jnp.concatenate([q[:, r * HEAD_DIM:(r + 1) * HEAD_DIM],
                          jnp.broadcast_to(sl_ref[0, r:r + 1, :], (tq, HEAD_DIM))], axis=1)
         for r in range(NSA_HPG)], axis=0)
    q_sel = jnp.concatenate([q_aug, jnp.concatenate([sel_ref[0, 0]] * NSA_HPG, axis=0)], axis=1)
    offset = (lax.broadcasted_iota(jnp.int32, (tk, cols), 0)
              - lax.broadcasted_iota(jnp.int32, (tk, cols), 1) % tq)

    def reset():
        m_ref[...] = jnp.full_like(m_ref, NEG_INF)
        acc_ref[...] = jnp.zeros_like(acc_ref)

    def finish():
        return acc_ref[0:HEAD_DIM, :] / acc_ref[HEAD_DIM:HEAD_DIM + 1, :]

    def kappa(c):
        return slope_row * (c * tk - t0).astype(F32)

    def causal(s_t, c):
        return jnp.where(offset + (c * tk - t0) <= 0, s_t, NEG_INF)

    def values(vt_ref):
        return lambda c: vt_ref[0, 0, :, pl.ds(pl.multiple_of(c * tk, tk), tk)]

    def recent(s_t, c):
        return jnp.where(offset + (c * tk - t0) > -WINDOW, s_t, NEG_INF)

    def sel_scores(c):
        start = pl.multiple_of(c * tk, tk)
        return _dot_nt(ks_ref[0, 0, pl.ds(start, tk), :], q_sel)

    def win_scores(c):
        start = pl.multiple_of(c * tk, tk)
        return _dot_nt(kw_ref[0, 0, pl.ds(start, tk), :], q_aug)

    last = t0 // tk

    def note_active(c, n):
        hit = active_ref[0, 0, 0, 0, c] > 0

        @pl.when(hit)
        def _():
            todo_ref[n] = c

        return n + hit.astype(jnp.int32)

    n_todo = lax.fori_loop(0, last, note_active, 0)
    todo_ref[n_todo] = last

    reset()
    _online_softmax(n_todo, lambda i: todo_ref[i], last, sel_scores, values(vst_ref), kappa,
                    mask=lambda s_t, c: s_t, final_mask=lambda s_t: causal(s_t, last),
                    m_ref=m_ref, acc_ref=acc_ref)
    o_sel = finish()

    first = jnp.maximum(t0 - (WINDOW - 1), 0) // tk

    reset()
    _online_softmax(last - first, lambda i: first + i, last, win_scores, values(vwt_ref), kappa,
                    mask=recent, final_mask=lambda s_t: causal(s_t, last),
                    m_ref=m_ref, acc_ref=acc_ref)
    o_win = finish()

    gates = gate_ref[0, 0]
    outs = []
    for r in range(NSA_HPG):
        cs = slice(r * tq, (r + 1) * tq)
        outs.append(gates[3 * r:3 * r + 1] * ocmp_ref[0, 0, r] + gates[3 * r + 1:3 * r + 2] * o_sel[:, cs]
                    + gates[3 * r + 2:3 * r + 3] * o_win[:, cs])
    o_ref[0] = jnp.concatenate(outs, axis=0).T.astype(o_ref.dtype)


def sel_win_attention(proj, active, slope_cols, ks, vs_t, kw, vw_t, sel_bias, o_cmp, gates_t,
                      slopes, tq=128, tk=256):
    B, S, _ = proj.shape
    assert tk + tq <= WINDOW
    G = NSA_GROUPS
    n_sel = S // SEL_LEN
    cols = NSA_HPG * tq
    smem = pl.BlockSpec(memory_space=pltpu.SMEM)
    ks_spec = pl.BlockSpec((1, 1, S, ks.shape[-1]), lambda b, g, i: (b, g, 0, 0))
    kw_spec = pl.BlockSpec((1, 1, S, kw.shape[-1]), lambda b, g, i: (b, g, 0, 0))
    v_spec = pl.BlockSpec((1, 1, V_ROWS_64, S), lambda b, g, i: (b, g, 0, 0))
    return pl.pallas_call(
        functools.partial(_sel_win_kernel, tq=tq, tk=tk),
        grid=(B, G, S // tq),
        in_specs=[smem,
                  pl.BlockSpec((1, 1, 1, 1, S // tk), lambda b, g, i: (b, g, i, 0, 0),
                               memory_space=pltpu.SMEM),
                  pl.BlockSpec((1, tq, 256), lambda b, g, i: (b, i, g)),
                  pl.BlockSpec((1, NSA_HPG, HEAD_DIM), lambda b, g, i: (g, 0, 0)),
                  ks_spec, v_spec, kw_spec, v_spec,
                  pl.BlockSpec((1, 1, tq, n_sel), lambda b, g, i: (b, g, i, 0)),
                  pl.BlockSpec((1, 1, NSA_HPG, HEAD_DIM, tq), lambda b, g, i: (b, g, 0, 0, i)),
                  pl.BlockSpec((1, 1, NSA_HPG * N_GATES, tq), lambda b, g, i: (b, g, 0, i))],
        out_specs=pl.BlockSpec((1, tq, 256), lambda b, g, i: (b, i, g)),
        out_shape=jax.ShapeDtypeStruct((B, S, G * 256), BF16),
        scratch_shapes=[pltpu.VMEM((1, cols), F32), pltpu.VMEM((V_ROWS_64, cols), F32),
                        pltpu.SMEM((S // tk + 1,), jnp.int32)],
        compiler_params=_params(3),
        name="sel_win_attention",
    )(slopes, active, proj, slope_cols, ks, vs_t, kw, vw_t, sel_bias, o_cmp, gates_t)


def _alibi_slopes_log2(n_heads):
    slopes = np.exp2(-8.0 * (np.arange(n_heads, dtype=np.float32) + 1.0) / n_heads)
    return (slopes.astype(np.float32) * np.float32(LOG2E)).astype(np.float32)


def even_layer_mix(x2d, B, S, norm_g, w_in, lam_q1, lam_k1, lam_q2, lam_k2, subln, layer):
    proj = norm_proj(x2d, norm_g, w_in.astype(BF16), w_in.shape[1], [(0, 512), (1536, 2048)],
                     False).reshape(B, S, -1)
    sb_vt = proj[:, :, 1024:1536].transpose(0, 2, 1)
    df_vt = proj[:, :, 2560:3072].transpose(0, 2, 1).reshape(B, 4, 128, S)
    o_sb = sb_attention(proj, sb_vt)
    lam_rows = jnp.stack([lam_q1, lam_k1, lam_q2, lam_k2]).astype(F32)
    o_df = diff_attention(proj, _with_ones_row(df_vt, V_ROWS_128), lam_rows, subln.astype(F32),
                          _alibi_slopes_log2(4), layer)
    return jnp.concatenate([o_sb, o_df], axis=2).reshape(B * S, -1)


def odd_layer_mix(x2d, B, S, norm_g, w_in, pos_k, k_w1, k_w2, pos_v, v_w1, v_w2, tk=256):
    G, Dh = NSA_GROUPS, HEAD_DIM
    q_width = NSA_GROUPS * NSA_HPG * Dh
    kv_width = G * Dh
    n_main = q_width + 6 * kv_width
    pad = n_main + GATE_PAD - w_in.shape[1]
    w_pad = jnp.pad(w_in, ((0, 0), (0, pad))).astype(BF16)
    proj, gates = norm_proj(x2d, norm_g, w_pad, n_main, [(0, q_width)], True)
    proj = proj.reshape(B, S, n_main)

    def group_major(j):
        cols = proj[:, :, q_width + j * kv_width:q_width + (j + 1) * kv_width]
        return cols.reshape(B, S, G, Dh).transpose(0, 2, 1, 3)

    def keys_with_positions(j, extra=()):
        cols = [_key_position_columns(S, tk, Dh)] + list(extra)
        return jnp.concatenate([group_major(j)] + [jnp.broadcast_to(c, (B, G) + c.shape) for c in cols],
                               axis=-1)

    def values_transposed(j):
        return _with_ones_row(group_major(j).transpose(0, 1, 3, 2), V_ROWS_64)

    n_chunks = S // CMP_STRIDE
    chunks = jnp.stack([group_major(0), group_major(1)]).reshape(2, B * G, n_chunks, CMP_STRIDE * Dh)
    pos_flat = jnp.stack([pos_k, pos_v]).reshape(2, 1, CMP_LEN * Dh).astype(F32)
    w1 = jnp.stack([k_w1, v_w1]).astype(BF16)
    w2 = jnp.stack([k_w2, v_w2]).astype(BF16)
    cmp = compress_kv(chunks, pos_flat, w1, w2).reshape(2, B, G, n_chunks, Dh)

    n_sel = S // SEL_LEN
    cmp_start = jnp.arange(n_chunks) * CMP_STRIDE
    sel_start = jnp.arange(n_sel) * SEL_LEN
    overlap_t = ((cmp_start[None, :] < sel_start[:, None] + SEL_LEN)
                 & (sel_start[:, None] <= cmp_start[None, :] + CMP_LEN - 1)).astype(BF16)
    onehot = (jnp.arange(S)[:, None] // SEL_LEN == jnp.arange(n_sel)[None, :]).astype(BF16)
    slopes = _alibi_slopes_log2(NSA_GROUPS * NSA_HPG)
    slope_cols = _slope_pieces(slopes, Dh).reshape(G, NSA_HPG, Dh)

    tq = 256
    o_cmp, sel_bias, hits = cmp_select(proj, cmp[0], cmp[1].transpose(0, 1, 3, 2), overlap_t, slopes,
                                       tk, tq)
    active = (hits[..., 0] > 0).astype(jnp.int32)[:, :, :, None, :]
    gates_t = (gates[:, :G * NSA_HPG * N_GATES].reshape(B, S, G, NSA_HPG * N_GATES)
               .transpose(0, 2, 3, 1))
    o = sel_win_attention(proj, active, slope_cols, keys_with_positions(2, [onehot]),
                          values_transposed(3), keys_with_positions(4), values_transposed(5),
                          sel_bias, o_cmp, gates_t, slopes, tq=tq, tk=tk)
    return o.reshape(B * S, q_width)


def kernel(x, attn_norm, mlp_norm, final_norm, ev_w_in, ev_lam_q1, ev_lam_k1, ev_lam_q2, ev_lam_k2,
           ev_subln, ev_w_out, od_w_in, od_cmp_pos_k, od_cmp_k_w1, od_cmp_k_w2, od_cmp_pos_v,
           od_cmp_v_w1, od_cmp_v_w2, od_w_out, mlp_w1, mlp_w2):
    B, S, D = x.shape
    depth = attn_norm.shape[0]
    x2d = x.reshape(B * S, D)
    for layer in range(depth):
        idx = layer // 2
        if layer % 2 == 0:
            mix = even_layer_mix(x2d, B, S, attn_norm[layer], ev_w_in[idx], ev_lam_q1[idx],
                                 ev_lam_k1[idx], ev_lam_q2[idx], ev_lam_k2[idx], ev_subln[idx], layer)
            w_out = ev_w_out[idx]
        else:
            mix = odd_layer_mix(x2d, B, S, attn_norm[layer], od_w_in[idx], od_cmp_pos_k[idx],
                                od_cmp_k_w1[idx], od_cmp_k_w2[idx], od_cmp_pos_v[idx],
                                od_cmp_v_w1[idx], od_cmp_v_w2[idx])
            w_out = od_w_out[idx]
        g_final = final_norm if layer == depth - 1 else None
        x2d = post_block(mix, x2d, w_out.astype(BF16), mlp_norm[layer], mlp_w1[layer].astype(BF16),
                         mlp_w2[layer].astype(BF16), g_final)
    return x2d.reshape(B, S, D)
```

```python
import functools
import math

import jax
import jax.numpy as jnp
import numpy as np
from jax import lax
from jax.experimental import pallas as pl
from jax.experimental.pallas import tpu as pltpu

F32 = jnp.float32
BF16 = jnp.bfloat16

HEAD_DIM = 64
RMS_EPS = 1e-6
NEG_INF = -1e30
FORCE_SCORE = 1e6
NSA_GROUPS = 4
NSA_HPG = 4
CMP_LEN = 32
CMP_STRIDE = 16
SEL_LEN = 64
SEL_TOPK = 16
WINDOW = 512
N_GATES = 3
GATE_PAD = 128
SEL_DROP = 2.0 ** 24
BF16_EXACT_INT = 256
V_ROWS_64 = 80
V_ROWS_128 = 144

LOG2E = math.log2(math.e)
Q_SCALE = HEAD_DIM ** -0.5 * LOG2E
UNDERFLOW_LOG2 = -160.0

VMEM_LIMIT = 56 * 1024 * 1024


def _params(n_parallel):
    return pltpu.CompilerParams(dimension_semantics=("parallel",) * n_parallel,
                                vmem_limit_bytes=VMEM_LIMIT)


def _rms(x, g):
    ms = jnp.mean(x * x, axis=-1, keepdims=True)
    return x * lax.rsqrt(ms + RMS_EPS) * g


def _dot(a, b):
    return jnp.dot(a, b, preferred_element_type=F32)


def _dot_nt(a, b):
    return lax.dot_general(a, b, (((1,), (1,)), ((), ())), preferred_element_type=F32)


def _split_bf16(x):
    hi = x.astype(BF16)
    lo = (x - hi.astype(F32)).astype(BF16)
    return hi, lo


def _slope_pieces(slopes, width):
    def top_bits(x):
        return (x.view(np.uint32) & np.uint32(0xFFFF0000)).view(np.float32)

    s1 = top_bits(slopes)
    r1 = slopes - s1
    s2 = top_bits(r1)
    s3 = top_bits(r1 - s2)
    out = np.zeros((slopes.shape[0], width), np.float32)
    out[:, :6] = np.stack([s1, s2, s3, s1, s2, s3], axis=1)
    return jnp.asarray(out).astype(BF16)


def _key_position_columns(n, tk, width):
    j = jnp.arange(n) % tk
    a = (j // BF16_EXACT_INT) * BF16_EXACT_INT
    b = j % BF16_EXACT_INT
    cols = jnp.stack([a, a, a, b, b, b], axis=1).astype(BF16)
    return jnp.pad(cols, ((0, 0), (0, width - cols.shape[1])))


def _with_ones_row(v_t, rows):
    d = v_t.shape[-2]
    ones = jnp.ones(v_t.shape[:-2] + (1, v_t.shape[-1]), v_t.dtype)
    pad = jnp.zeros(v_t.shape[:-2] + (rows - d - 1, v_t.shape[-1]), v_t.dtype)
    return jnp.concatenate([v_t, ones, pad], axis=-2)


def _softmax_block(steps, states):
    s_all = [scores() for scores, _, _, _ in steps]
    live = {k: (states[k][0][...], states[k][1][...]) for k in sorted({k for _, _, _, k in steps})}
    for (_, kap, values, k), s_t in zip(steps, s_all):
        m_run, acc = live[k]
        m_new = jnp.maximum(m_run, jnp.max(s_t, axis=0, keepdims=True) + kap)
        p = jnp.exp2(s_t - (m_new - kap)).astype(BF16)
        live[k] = (m_new, jnp.exp2(m_run - m_new) * acc + _dot(values(), p))
    for k, (m_run, acc) in live.items():
        states[k][0][...] = m_run
        states[k][1][...] = acc


def _softmax_loop(n_steps, step, state):
    done = 0
    for group in (4, 2, 1):
        def body(i, carry, group=group, done=done):
            _softmax_block([step(done + group * i + j) + (0,) for j in range(group)], [state])
            return carry

        n_groups = (n_steps - done) // group
        lax.fori_loop(0, n_groups, body, 0)
        done = done + group * n_groups


def _norm_proj_kernel(x_ref, g_ref, w_ref, scale_ref, o_ref, *gate_ref, n_main, col_chunk):
    xn = _rms(x_ref[...], g_ref[...]).astype(BF16)
    for c in range(n_main // col_chunk):
        sl = slice(c * col_chunk, (c + 1) * col_chunk)
        o_ref[:, sl] = (_dot(xn, w_ref[:, sl]) * scale_ref[:, sl]).astype(o_ref.dtype)
    if gate_ref:
        logits = _dot(xn, w_ref[:, n_main:n_main + GATE_PAD])
        gate_ref[0][...] = jax.nn.sigmoid(logits)


def norm_proj(x2d, g, w_bf16, n_main, q_cols, with_gates, tm=512, col_chunk=512):
    T, D = x2d.shape
    n_w = w_bf16.shape[1]
    col = jnp.arange(n_main)
    is_q = functools.reduce(jnp.logical_or, [(col >= lo) & (col < hi) for lo, hi in q_cols])
    col_scale = jnp.where(is_q, Q_SCALE, 1.0).astype(F32).reshape(1, n_main)
    out_shape = [jax.ShapeDtypeStruct((T, n_main), BF16)]
    out_specs = [pl.BlockSpec((tm, n_main), lambda i: (i, 0))]
    if with_gates:
        out_shape.append(jax.ShapeDtypeStruct((T, GATE_PAD), F32))
        out_specs.append(pl.BlockSpec((tm, GATE_PAD), lambda i: (i, 0)))
    res = pl.pallas_call(
        functools.partial(_norm_proj_kernel, n_main=n_main, col_chunk=col_chunk),
        grid=(T // tm,),
        in_specs=[pl.BlockSpec((tm, D), lambda i: (i, 0)),
                  pl.BlockSpec((1, D), lambda i: (0, 0)),
                  pl.BlockSpec((D, n_w), lambda i: (0, 0)),
                  pl.BlockSpec((1, n_main), lambda i: (0, 0))],
        out_specs=out_specs,
        out_shape=out_shape,
        compiler_params=_params(1),
        name="norm_proj",
    )(x2d, g.reshape(1, D), w_bf16, col_scale)
    return res if with_gates else res[0]


def _head_pair_rows(q, t):
    lane = lax.broadcasted_iota(jnp.int32, (t, 2 * HEAD_DIM), 1)
    zero = jnp.zeros_like(q)
    return jnp.where(lane < HEAD_DIM, q, zero), jnp.where(lane >= HEAD_DIM, q, zero)


def _sb_kernel(q_ref, k_ref, vt_ref, o_ref, acc_ref, carry_ref, *, t):
    qi = pl.program_id(2)
    cols = 2 * t
    q_both = jnp.concatenate(_head_pair_rows(q_ref[0], t), axis=0)
    s_idx = lax.broadcasted_iota(jnp.int32, (t, 2 * t), 0)
    j_idx = lax.broadcasted_iota(jnp.int32, (t, 2 * t), 1) % t
    upper2 = jnp.where(j_idx > s_idx, 1.0, 0.0).astype(BF16)
    key = lax.broadcasted_iota(jnp.int32, (t, cols), 0)
    qry = lax.broadcasted_iota(jnp.int32, (t, cols), 1) % t
    past_diag = key < qry

    acc_ref[...] = jnp.zeros_like(acc_ref)
    carry_ref[...] = jnp.zeros_like(carry_ref)

    def chunk(kc, masked):
        start = pl.multiple_of(kc * t, t)
        z = _dot_nt(k_ref[0, pl.ds(start, t), :], q_both)
        drop = jnp.maximum(z, 0.0) + jnp.log2(1.0 + jnp.exp2(jnp.abs(z) * -1.0))
        log_beta = z - drop
        if masked:
            drop = jnp.where(past_diag, drop, 0.0)
        hi, lo = _split_bf16(drop)
        tail = _dot(upper2, jnp.concatenate([hi, lo], axis=0))
        carry = carry_ref[...]
        w = jnp.exp2(log_beta - tail - carry)
        if masked:
            w = jnp.where(past_diag, w, 0.0)
        carry = carry + tail[0:1] + drop[0:1]
        carry_ref[...] = carry
        acc_ref[...] += _dot(vt_ref[0, :, pl.ds(start, t)], w.astype(BF16))
        return jnp.min(carry)

    def cond(state):
        j, least_carry = state
        return (j < qi) & (least_carry < -UNDERFLOW_LOG2)

    def body(state):
        j, _ = state
        return j + 1, chunk(qi - 1 - j, False)

    lax.while_loop(cond, body, (0, chunk(qi, True)))
    row = lax.broadcasted_iota(jnp.int32, (2 * HEAD_DIM, t), 0)
    o_t = jnp.where(row < HEAD_DIM, acc_ref[:, :t], acc_ref[:, t:])
    o_ref[0] = o_t.T.astype(o_ref.dtype)


def sb_attention(proj, v_t, t=256):
    B, S, _ = proj.shape
    n_pairs = 4
    return pl.pallas_call(
        functools.partial(_sb_kernel, t=t),
        grid=(B, n_pairs, S // t),
        in_specs=[pl.BlockSpec((1, t, 128), lambda b, p, i: (b, i, p)),
                  pl.BlockSpec((1, S, 128), lambda b, p, i: (b, 0, n_pairs + p)),
                  pl.BlockSpec((1, 128, S), lambda b, p, i: (b, p, 0))],
        out_specs=pl.BlockSpec((1, t, 128), lambda b, p, i: (b, i, p)),
        out_shape=jax.ShapeDtypeStruct((B, S, n_pairs * 128), BF16),
        scratch_shapes=[pltpu.VMEM((128, 2 * t), F32), pltpu.VMEM((1, 2 * t), F32)],
        compiler_params=_params(3),
        name="sb_attention",
    )(proj, proj, v_t)


def _diff_kernel(slopes_ref, lam_ref, q_ref, k_ref, vt_ref, pos_ref, sl_ref, subln_ref, o_ref,
                 m_ref, acc_ref, *, t, out_scale, lambda_init):
    h = pl.program_id(1)
    qi = pl.program_id(2)
    slope = slopes_ref[h]
    cols = 2 * t
    slope_cols = jnp.broadcast_to(sl_ref[0], (t, 128))
    q_both = jnp.concatenate(
        [jnp.concatenate([qc, slope_cols], axis=1) for qc in _head_pair_rows(q_ref[0], t)],
        axis=0)
    pos = pos_ref[...]
    offset = (lax.broadcasted_iota(jnp.int32, (t, cols), 0)
              - lax.broadcasted_iota(jnp.int32, (t, cols), 1) % t)

    m_ref[...] = jnp.full_like(m_ref, NEG_INF)
    acc_ref[...] = jnp.zeros_like(acc_ref)

    def step(kc, diagonal=False):
        start = pl.multiple_of(kc * t, t)

        def scores():
            s_t = _dot_nt(jnp.concatenate([k_ref[0, pl.ds(start, t), :], pos], axis=1), q_both)
            return jnp.where(offset <= 0, s_t, NEG_INF) if diagonal else s_t

        return scores, slope * ((kc - qi) * t).astype(F32), lambda: vt_ref[0, 0, :, pl.ds(start, t)]

    state = (m_ref, acc_ref)
    _softmax_loop(qi, step, state)
    _softmax_block([step(qi, True) + (0,)], [state])

    lam_terms = lam_ref[...]
    lam = (jnp.exp(jnp.sum(lam_terms[0:1] * lam_terms[1:2], axis=1, keepdims=True))
           - jnp.exp(jnp.sum(lam_terms[2:3] * lam_terms[3:4], axis=1, keepdims=True))
           + lambda_init)
    o_t = acc_ref[0:128, :] / acc_ref[128:129, :]
    o = (o_t[:, :t] - lam * o_t[:, t:]).T
    o_ref[0] = (_rms(o, subln_ref[...]) * out_scale).astype(o_ref.dtype)


def diff_attention(proj, vt_aug, lam_rows, subln, slopes, layer, t=512):
    B, S, _ = proj.shape
    n_heads = 4
    lambda_init = 0.8 - 0.6 * math.exp(-0.3 * layer)
    smem = pl.BlockSpec(memory_space=pltpu.SMEM)
    pos = _key_position_columns(t, t, 128)
    slope_cols = _slope_pieces(slopes, 128).reshape(n_heads, 1, 128)
    return pl.pallas_call(
        functools.partial(_diff_kernel, t=t, out_scale=1.0 - lambda_init, lambda_init=lambda_init),
        grid=(B, n_heads, S // t),
        in_specs=[smem,
                  pl.BlockSpec((4, HEAD_DIM), lambda b, h, i: (0, 0)),
                  pl.BlockSpec((1, t, 128), lambda b, h, i: (b, i, 12 + h)),
                  pl.BlockSpec((1, S, 128), lambda b, h, i: (b, 0, 16 + h)),
                  pl.BlockSpec((1, 1, V_ROWS_128, S), lambda b, h, i: (b, h, 0, 0)),
                  pl.BlockSpec((t, 128), lambda b, h, i: (0, 0)),
                  pl.BlockSpec((1, 1, 128), lambda b, h, i: (h, 0, 0)),
                  pl.BlockSpec((1, 128), lambda b, h, i: (0, 0))],
        out_specs=pl.BlockSpec((1, t, 128), lambda b, h, i: (b, i, h)),
        out_shape=jax.ShapeDtypeStruct((B, S, n_heads * 128), BF16),
        scratch_shapes=[pltpu.VMEM((1, 2 * t), F32), pltpu.VMEM((V_ROWS_128, 2 * t), F32)],
        compiler_params=_params(3),
        name="diff_attention",
    )(slopes, lam_rows, proj, proj, vt_aug, pos, slope_cols, subln.reshape(1, 128))


def _post_kernel(*refs, ff_chunk, final):
    mix_ref, x_ref, wo_ref, g_ref, w1_ref, w2_ref = refs[:6]
    gf_ref = refs[6] if final else None
    o_ref = refs[-1]
    x = x_ref[...] + _dot(mix_ref[...], wo_ref[...])
    hn = _rms(x, g_ref[...]).astype(BF16)
    acc = x
    for f in range(w1_ref.shape[1] // ff_chunk):
        sl = slice(f * ff_chunk, (f + 1) * ff_chunk)
        hid = jnp.maximum(_dot(hn, w1_ref[:, sl]), 0.0)
        acc = acc + _dot((hid * hid).astype(BF16), w2_ref[sl, :])
    if final:
        acc = _rms(acc, gf_ref[...])
    o_ref[...] = acc


def post_block(mix, x2d, w_out, g_mlp, w1, w2, g_final=None, tm=512, ff_chunk=1024):
    T, D = x2d.shape
    final = g_final is not None
    const = lambda i: (0, 0)
    in_specs = [pl.BlockSpec((tm, mix.shape[1]), lambda i: (i, 0)),
                pl.BlockSpec((tm, D), lambda i: (i, 0)),
                pl.BlockSpec(w_out.shape, const),
                pl.BlockSpec((1, D), const), pl.BlockSpec(w1.shape, const), pl.BlockSpec(w2.shape, const)]
    args = [mix, x2d, w_out, g_mlp.reshape(1, D), w1, w2]
    if final:
        in_specs.append(pl.BlockSpec((1, D), const))
        args.append(g_final.reshape(1, D))
    return pl.pallas_call(
        functools.partial(_post_kernel, ff_chunk=ff_chunk, final=final),
        grid=(T // tm,),
        in_specs=in_specs,
        out_specs=pl.BlockSpec((tm, D), lambda i: (i, 0)),
        out_shape=jax.ShapeDtypeStruct((T, D), F32),
        compiler_params=_params(1),
        name="post_block",
    )(*args)


def _compress_kernel(c_ref, pos_ref, w1_ref, w2_ref, o_ref):
    half = w1_ref.shape[1] // 2
    chunks = c_ref[0, 0]
    pos = jnp.broadcast_to(pos_ref[0], (8, 2 * half)).astype(BF16)
    first = _dot(chunks, w1_ref[0, :half, :])
    second = _dot(chunks, w1_ref[0, half:, :])
    n_chunks = chunks.shape[0]
    pre = first + pltpu.roll(second, n_chunks - 1, 0) + _dot(pos, w1_ref[0])[0:1]
    hid = jax.nn.gelu(pre)
    o_ref[0, 0] = _dot(hid.astype(BF16), w2_ref[0]).astype(o_ref.dtype)


def compress_kv(chunks, pos_flat, w1, w2):
    _, BG, n_chunks, width = chunks.shape
    hidden = w1.shape[-1]
    return pl.pallas_call(
        _compress_kernel,
        grid=(2, BG),
        in_specs=[pl.BlockSpec((1, 1, n_chunks, width), lambda s, i: (s, i, 0, 0)),
                  pl.BlockSpec((1, 1, 2 * width), lambda s, i: (s, 0, 0)),
                  pl.BlockSpec((1, 2 * width, hidden), lambda s, i: (s, 0, 0)),
                  pl.BlockSpec((1, hidden, HEAD_DIM), lambda s, i: (s, 0, 0))],
        out_specs=pl.BlockSpec((1, 1, n_chunks, HEAD_DIM), lambda s, i: (s, i, 0, 0)),
        out_shape=jax.ShapeDtypeStruct((2, BG, n_chunks, HEAD_DIM), BF16),
        compiler_params=_params(2),
        name="compress_kv",
    )(chunks, pos_flat, w1, w2)


def _stack_heads(q):
    return jnp.concatenate([q[:, r * HEAD_DIM:(r + 1) * HEAD_DIM] for r in range(NSA_HPG)], axis=0)


def _slope_row(slopes_ref, g, tq):
    col = lax.broadcasted_iota(jnp.int32, (1, NSA_HPG * tq), 1)
    out = jnp.zeros((1, NSA_HPG * tq), F32)
    for r in range(NSA_HPG):
        out = jnp.where(col // tq == r, slopes_ref[g * NSA_HPG + r], out)
    return out


def _cmp_select_kernel(slopes_ref, q_ref, kc_ref, vct_ref, ovt_ref, ocmp_ref, sel_ref, hits_ref,
                       *, tq, n_sel, blocks_per_chunk, hit_tile):
    g = pl.program_id(1)
    t0 = pl.program_id(2) * tq
    cols = NSA_HPG * tq
    n_cmp = kc_ref.shape[2]
    slope_row = _slope_row(slopes_ref, g, tq)
    q_rows = _stack_heads(q_ref[0])
    tpos = t0 + lax.broadcasted_iota(jnp.int32, (n_cmp, cols), 1) % tq
    cmp_end = lax.broadcasted_iota(jnp.int32, (n_cmp, cols), 0) * CMP_STRIDE + (CMP_LEN - 1)
    dc = (tpos - cmp_end).astype(F32)
    sc = jnp.where(dc >= 0, _dot_nt(kc_ref[0, 0], q_rows) - slope_row * dc, NEG_INF)
    e = jnp.exp2(sc - jnp.max(sc, axis=0, keepdims=True))
    any_valid = jnp.where(dc[0:1] >= 0, 1.0, 0.0)
    pc = e * (any_valid / jnp.sum(e, axis=0, keepdims=True))
    o_t = _dot(vct_ref[0, 0], pc.astype(BF16))
    for r in range(NSA_HPG):
        ocmp_ref[0, 0, r] = o_t[:, r * tq:(r + 1) * tq]

    pc_group = pc[:, 0:tq]
    for r in range(1, NSA_HPG):
        pc_group = pc_group + pc[:, r * tq:(r + 1) * tq]
    hi, lo = _split_bf16(pc_group)
    imp = _dot(ovt_ref[...], hi) + _dot(ovt_ref[...], lo)

    blk = lax.broadcasted_iota(jnp.int32, (n_sel, tq), 0)
    cur = (t0 + lax.broadcasted_iota(jnp.int32, (n_sel, tq), 1)) // SEL_LEN
    forced = (blk == 0) | (blk == cur) | (blk == cur - 1)
    imp = jnp.where(forced, FORCE_SCORE, imp)
    imp = jnp.where(blk <= cur, imp, -1.0)
    picked = jnp.zeros((n_sel, tq), F32)
    for _ in range(min(SEL_TOPK, n_sel)):
        best = jnp.max(imp, axis=0, keepdims=True)
        first = jnp.min(jnp.where(imp == best, blk, n_sel), axis=0, keepdims=True)
        hit = blk == first
        picked = jnp.where(hit, 1.0, picked)
        imp = jnp.where(hit, -2.0, imp)
    keep = (picked > 0.5) & (blk <= cur)
    sel_ref[0, 0] = jnp.where(keep, 0.0, -SEL_DROP).T.astype(sel_ref.dtype)
    n_chunks = n_sel // blocks_per_chunk
    member = (lax.broadcasted_iota(jnp.int32, (n_chunks, n_sel), 1) // blocks_per_chunk
              == lax.broadcasted_iota(jnp.int32, (n_chunks, n_sel), 0))
    per_query = _dot(jnp.where(member, 1.0, 0.0).astype(BF16),
                     jnp.where(keep, 1.0, 0.0).astype(BF16)).astype(BF16)
    for part in range(tq // hit_tile):
        hits_ref[0, 0, part] = _dot(per_query[:, part * hit_tile:(part + 1) * hit_tile],
                                    jnp.ones((hit_tile, 128), BF16))


def cmp_select(proj, kc, vc_t, overlap_t, slopes, tk, hit_tile, tq=256):
    B, S, _ = proj.shape
    n_cmp = kc.shape[2]
    n_sel = S // SEL_LEN
    G = NSA_GROUPS
    smem = pl.BlockSpec(memory_space=pltpu.SMEM)
    return pl.pallas_call(
        functools.partial(_cmp_select_kernel, tq=tq, n_sel=n_sel, blocks_per_chunk=tk // SEL_LEN,
                          hit_tile=hit_tile),
        grid=(B, G, S // tq),
        in_specs=[smem,
                  pl.BlockSpec((1, tq, 256), lambda b, g, i: (b, i, g)),
                  pl.BlockSpec((1, 1, n_cmp, HEAD_DIM), lambda b, g, i: (b, g, 0, 0)),
                  pl.BlockSpec((1, 1, HEAD_DIM, n_cmp), lambda b, g, i: (b, g, 0, 0)),
                  pl.BlockSpec((n_sel, n_cmp), lambda b, g, i: (0, 0))],
        out_specs=[pl.BlockSpec((1, 1, NSA_HPG, HEAD_DIM, tq), lambda b, g, i: (b, g, 0, 0, i)),
                   pl.BlockSpec((1, 1, tq, n_sel), lambda b, g, i: (b, g, i, 0)),
                   pl.BlockSpec((1, 1, tq // hit_tile, S // tk, 128), lambda b, g, i: (b, g, i, 0, 0))],
        out_shape=[jax.ShapeDtypeStruct((B, G, NSA_HPG, HEAD_DIM, S), F32),
                   jax.ShapeDtypeStruct((B, G, S, n_sel), BF16),
                   jax.ShapeDtypeStruct((B, G, S // hit_tile, S // tk, 128), F32)],
        compiler_params=_params(3),
        name="cmp_select",
    )(slopes, proj, kc, vc_t, overlap_t)


def _sel_win_kernel(slopes_ref, active_ref, q_ref, sl_ref, ks_ref, vst_ref, kw_ref, vwt_ref, sel_ref,
                    ocmp_ref, gate_ref, o_ref, ms_ref, accs_ref, mw_ref, accw_ref, todo_ref, *, tq, tk):
    g = pl.program_id(1)
    t0 = pl.program_id(2) * tq
    cols = NSA_HPG * tq
    slope_row = _slope_row(slopes_ref, g, tq)
    q = q_ref[0]
    q_aug = jnp.concatenate(
        [jnp.concatenate([q[:, r * HEAD_DIM:(r + 1) * HEAD_DIM],
                          jnp.broadcast_to(sl_ref[0, r:r + 1, :], (tq, HEAD_DIM))], axis=1)
         for r in range(NSA_HPG)], axis=0)
    q_sel = jnp.concatenate([q_aug, jnp.concatenate([sel_ref[0, 0]] * NSA_HPG, axis=0)], axis=1)
    offset = (lax.broadcasted_iota(jnp.int32, (tk, cols), 0)
              - lax.broadcasted_iota(jnp.int32, (tk, cols), 1) % tq)

    sel_state, win_state = (ms_ref, accs_ref), (mw_ref, accw_ref)
    for m_ref, acc_ref in (sel_state, win_state):
        m_ref[...] = jnp.full_like(m_ref, NEG_INF)
        acc_ref[...] = jnp.zeros_like(acc_ref)

    def step(c, k_ref, q_rows, vt_ref, mask=None):
        inside = c >= 0
        start = pl.multiple_of(jnp.maximum(c, 0) * tk, tk)
        shift = c * tk - t0

        def scores():
            s_t = _dot_nt(k_ref[0, 0, pl.ds(start, tk), :], q_rows)
            return s_t if mask is None else jnp.where(mask(offset + shift) & inside, s_t, NEG_INF)

        kappa = jnp.where(inside, slope_row * shift.astype(F32), NEG_INF)
        return scores, kappa, lambda: vt_ref[0, 0, :, pl.ds(start, tk)]

    last = t0 // tk

    def note_active(c, n):
        hit = active_ref[0, 0, 0, 0, c] > 0

        @pl.when(hit)
        def _():
            todo_ref[n] = c

        return n + hit.astype(jnp.int32)

    n_todo = lax.fori_loop(0, last, note_active, 0)
    _softmax_loop(n_todo, lambda i: step(todo_ref[i], ks_ref, q_sel, vst_ref), sel_state)

    causal = lambda d: d <= 0
    recent = lambda d: d > -WINDOW
    anything = lambda d: True
    _softmax_block([step(last - 2, kw_ref, q_aug, vwt_ref, recent) + (1,),
                    step(last, ks_ref, q_sel, vst_ref, causal) + (0,),
                    step(last - 1, kw_ref, q_aug, vwt_ref, anything) + (1,),
                    step(last, kw_ref, q_aug, vwt_ref, causal) + (1,)], [sel_state, win_state])
    o_sel = accs_ref[0:HEAD_DIM, :] / accs_ref[HEAD_DIM:HEAD_DIM + 1, :]
    o_win = accw_ref[0:HEAD_DIM, :] / accw_ref[HEAD_DIM:HEAD_DIM + 1, :]

    gates = gate_ref[0, 0]
    outs = []
    for r in range(NSA_HPG):
        cs = slice(r * tq, (r + 1) * tq)
        outs.append(gates[3 * r:3 * r + 1] * ocmp_ref[0, 0, r] + gates[3 * r + 1:3 * r + 2] * o_sel[:, cs]
                    + gates[3 * r + 2:3 * r + 3] * o_win[:, cs])
    o_ref[0] = jnp.concatenate(outs, axis=0).T.astype(o_ref.dtype)


def sel_win_attention(proj, active, slope_cols, ks, vs_t, kw, vw_t, sel_bias, o_cmp, gates_t,
                      slopes, tq=128, tk=256):
    B, S, _ = proj.shape
    assert tq == tk and WINDOW == 2 * tk
    G = NSA_GROUPS
    n_sel = S // SEL_LEN
    cols = NSA_HPG * tq
    smem = pl.BlockSpec(memory_space=pltpu.SMEM)
    ks_spec = pl.BlockSpec((1, 1, S, ks.shape[-1]), lambda b, g, i: (b, g, 0, 0))
    kw_spec = pl.BlockSpec((1, 1, S, kw.shape[-1]), lambda b, g, i: (b, g, 0, 0))
    v_spec = pl.BlockSpec((1, 1, V_ROWS_64, S), lambda b, g, i: (b, g, 0, 0))
    return pl.pallas_call(
        functools.partial(_sel_win_kernel, tq=tq, tk=tk),
        grid=(B, G, S // tq),
        in_specs=[smem,
                  pl.BlockSpec((1, 1, 1, 1, S // tk), lambda b, g, i: (b, g, i, 0, 0),
                               memory_space=pltpu.SMEM),
                  pl.BlockSpec((1, tq, 256), lambda b, g, i: (b, i, g)),
                  pl.BlockSpec((1, NSA_HPG, HEAD_DIM), lambda b, g, i: (g, 0, 0)),
                  ks_spec, v_spec, kw_spec, v_spec,
                  pl.BlockSpec((1, 1, tq, n_sel), lambda b, g, i: (b, g, i, 0)),
                  pl.BlockSpec((1, 1, NSA_HPG, HEAD_DIM, tq), lambda b, g, i: (b, g, 0, 0, i)),
                  pl.BlockSpec((1, 1, NSA_HPG * N_GATES, tq), lambda b, g, i: (b, g, 0, i))],
        out_specs=pl.BlockSpec((1, tq, 256), lambda b, g, i: (b, i, g)),
        out_shape=jax.ShapeDtypeStruct((B, S, G * 256), BF16),
        scratch_shapes=[pltpu.VMEM((1, cols), F32), pltpu.VMEM((V_ROWS_64, cols), F32),
                        pltpu.VMEM((1, cols), F32), pltpu.VMEM((V_ROWS_64, cols), F32),
                        pltpu.SMEM((S // tk,), jnp.int32)],
        compiler_params=_params(3),
        name="sel_win_attention",
    )(slopes, active, proj, slope_cols, ks, vs_t, kw, vw_t, sel_bias, o_cmp, gates_t)


def _alibi_slopes_log2(n_heads):
    slopes = np.exp2(-8.0 * (np.arange(n_heads, dtype=np.float32) + 1.0) / n_heads)
    return (slopes.astype(np.float32) * np.float32(LOG2E)).astype(np.float32)


def even_layer_mix(x2d, B, S, norm_g, w_in, lam_q1, lam_k1, lam_q2, lam_k2, subln, layer):
    proj = norm_proj(x2d, norm_g, w_in.astype(BF16), w_in.shape[1], [(0, 512), (1536, 2048)],
                     False).reshape(B, S, -1)
    sb_vt = proj[:, :, 1024:1536].transpose(0, 2, 1)
    df_vt = proj[:, :, 2560:3072].transpose(0, 2, 1).reshape(B, 4, 128, S)
    o_sb = sb_attention(proj, sb_vt)
    lam_rows = jnp.stack([lam_q1, lam_k1, lam_q2, lam_k2]).astype(F32)
    o_df = diff_attention(proj, _with_ones_row(df_vt, V_ROWS_128), lam_rows, subln.astype(F32),
                          _alibi_slopes_log2(4), layer)
    return jnp.concatenate([o_sb, o_df], axis=2).reshape(B * S, -1)


def odd_layer_mix(x2d, B, S, norm_g, w_in, pos_k, k_w1, k_w2, pos_v, v_w1, v_w2, tk=256):
    G, Dh = NSA_GROUPS, HEAD_DIM
    q_width = NSA_GROUPS * NSA_HPG * Dh
    kv_width = G * Dh
    n_main = q_width + 6 * kv_width
    pad = n_main + GATE_PAD - w_in.shape[1]
    w_pad = jnp.pad(w_in, ((0, 0), (0, pad))).astype(BF16)
    proj, gates = norm_proj(x2d, norm_g, w_pad, n_main, [(0, q_width)], True)
    proj = proj.reshape(B, S, n_main)

    def group_major(j):
        cols = proj[:, :, q_width + j * kv_width:q_width + (j + 1) * kv_width]
        return cols.reshape(B, S, G, Dh).transpose(0, 2, 1, 3)

    def keys_with_positions(j, extra=()):
        cols = [_key_position_columns(S, tk, Dh)] + list(extra)
        return jnp.concatenate([group_major(j)] + [jnp.broadcast_to(c, (B, G) + c.shape) for c in cols],
                               axis=-1)

    def values_transposed(j):
        return _with_ones_row(group_major(j).transpose(0, 1, 3, 2), V_ROWS_64)

    n_chunks = S // CMP_STRIDE
    chunks = jnp.stack([group_major(0), group_major(1)]).reshape(2, B * G, n_chunks, CMP_STRIDE * Dh)
    pos_flat = jnp.stack([pos_k, pos_v]).reshape(2, 1, CMP_LEN * Dh).astype(F32)
    w1 = jnp.stack([k_w1, v_w1]).astype(BF16)
    w2 = jnp.stack([k_w2, v_w2]).astype(BF16)
    cmp = compress_kv(chunks, pos_flat, w1, w2).reshape(2, B, G, n_chunks, Dh)

    n_sel = S // SEL_LEN
    cmp_start = jnp.arange(n_chunks) * CMP_STRIDE
    sel_start = jnp.arange(n_sel) * SEL_LEN
    overlap_t = ((cmp_start[None, :] < sel_start[:, None] + SEL_LEN)
                 & (sel_start[:, None] <= cmp_start[None, :] + CMP_LEN - 1)).astype(BF16)
    onehot = (jnp.arange(S)[:, None] // SEL_LEN == jnp.arange(n_sel)[None, :]).astype(BF16)
    slopes = _alibi_slopes_log2(NSA_GROUPS * NSA_HPG)
    slope_cols = _slope_pieces(slopes, Dh).reshape(G, NSA_HPG, Dh)

    tq = 256
    o_cmp, sel_bias, hits = cmp_select(proj, cmp[0], cmp[1].transpose(0, 1, 3, 2), overlap_t, slopes,
                                       tk, tq)
    active = (hits[..., 0] > 0).astype(jnp.int32)[:, :, :, None, :]
    gates_t = (gates[:, :G * NSA_HPG * N_GATES].reshape(B, S, G, NSA_HPG * N_GATES)
               .transpose(0, 2, 3, 1))
    o = sel_win_attention(proj, active, slope_cols, keys_with_positions(2, [onehot]),
                          values_transposed(3), keys_with_positions(4), values_transposed(5),
                          sel_bias, o_cmp, gates_t, slopes, tq=tq, tk=tk)
    return o.reshape(B * S, q_width)


def kernel(x, attn_norm, mlp_norm, final_norm, ev_w_in, ev_lam_q1, ev_lam_k1, ev_lam_q2, ev_lam_k2,
           ev_subln, ev_w_out, od_w_in, od_cmp_pos_k, od_cmp_k_w1, od_cmp_k_w2, od_cmp_pos_v,
           od_cmp_v_w1, od_cmp_v_w2, od_w_out, mlp_w1, mlp_w2):
    B, S, D = x.shape
    depth = attn_norm.shape[0]
    x2d = x.reshape(B * S, D)
    for layer in range(depth):
        idx = layer // 2
        if layer % 2 == 0:
            mix = even_layer_mix(x2d, B, S, attn_norm[layer], ev_w_in[idx], ev_lam_q1[idx],
                                 ev_lam_k1[idx], ev_lam_q2[idx], ev_lam_k2[idx], ev_subln[idx], layer)
            w_out = ev_w_out[idx]
        else:
            mix = odd_layer_mix(x2d, B, S, attn_norm[layer], od_w_in[idx], od_cmp_pos_k[idx],
                                od_cmp_k_w1[idx], od_cmp_k_w2[idx], od_cmp_pos_v[idx],
                                od_cmp_v_w1[idx], od_cmp_v_w2[idx])
            w_out = od_w_out[idx]
        g_final = final_norm if layer == depth - 1 else None
        x2d = post_block(mix, x2d, w_out.astype(BF16), mlp_norm[layer], mlp_w1[layer].astype(BF16),
                         mlp_w2[layer].astype(BF16), g_final)
    return x2d.reshape(B, S, D)
```

```python
import functools
import math

import jax
import jax.numpy as jnp
import numpy as np
from jax import lax
from jax.experimental import pallas as pl
from jax.experimental.pallas import tpu as pltpu

F32 = jnp.float32
BF16 = jnp.bfloat16

HEAD_DIM = 64
RMS_EPS = 1e-6
NEG_INF = -1e30
FORCE_SCORE = 1e6
NSA_GROUPS = 4
NSA_HPG = 4
CMP_LEN = 32
CMP_STRIDE = 16
SEL_LEN = 64
SEL_TOPK = 16
WINDOW = 512
N_GATES = 3
GATE_PAD = 128
SEL_DROP = 2.0 ** 24
BF16_EXACT_INT = 256
V_ROWS_64 = 80
V_ROWS_128 = 144

LOG2E = math.log2(math.e)
Q_SCALE = HEAD_DIM ** -0.5 * LOG2E
UNDERFLOW_LOG2 = -160.0

VMEM_LIMIT = 56 * 1024 * 1024


def _params(n_parallel, n_arbitrary=0):
    return pltpu.CompilerParams(dimension_semantics=("parallel",) * n_parallel + ("arbitrary",) * n_arbitrary,
                                vmem_limit_bytes=VMEM_LIMIT)


def _rms(x, g):
    ms = jnp.mean(x * x, axis=-1, keepdims=True)
    return x * lax.rsqrt(ms + RMS_EPS) * g


def _dot(a, b):
    return jnp.dot(a, b, preferred_element_type=F32)


def _dot_nt(a, b):
    return lax.dot_general(a, b, (((1,), (1,)), ((), ())), preferred_element_type=F32)


def _split_bf16(x):
    hi = x.astype(BF16)
    lo = (x - hi.astype(F32)).astype(BF16)
    return hi, lo


def _slope_pieces(slopes, width):
    def top_bits(x):
        return (x.view(np.uint32) & np.uint32(0xFFFF0000)).view(np.float32)

    s1 = top_bits(slopes)
    r1 = slopes - s1
    s2 = top_bits(r1)
    s3 = top_bits(r1 - s2)
    out = np.zeros((slopes.shape[0], width), np.float32)
    out[:, :6] = np.stack([s1, s2, s3, s1, s2, s3], axis=1)
    return jnp.asarray(out).astype(BF16)


def _key_position_columns(n, tk, width):
    j = jnp.arange(n) % tk
    a = (j // BF16_EXACT_INT) * BF16_EXACT_INT
    b = j % BF16_EXACT_INT
    cols = jnp.stack([a, a, a, b, b, b], axis=1).astype(BF16)
    return jnp.pad(cols, ((0, 0), (0, width - cols.shape[1])))


def _with_ones_row(v_t, rows):
    d = v_t.shape[-2]
    ones = jnp.ones(v_t.shape[:-2] + (1, v_t.shape[-1]), v_t.dtype)
    pad = jnp.zeros(v_t.shape[:-2] + (rows - d - 1, v_t.shape[-1]), v_t.dtype)
    return jnp.concatenate([v_t, ones, pad], axis=-2)


def _softmax_block(steps, states):
    s_all = [scores() for scores, _, _, _ in steps]
    live = {k: (states[k][0][...], states[k][1][...]) for k in sorted({k for _, _, _, k in steps})}
    for (_, kap, values, k), s_t in zip(steps, s_all):
        m_run, acc = live[k]
        m_new = jnp.maximum(m_run, jnp.max(s_t, axis=0, keepdims=True) + kap)
        p = jnp.exp2(s_t - (m_new - kap)).astype(BF16)
        live[k] = (m_new, jnp.exp2(m_run - m_new) * acc + _dot(values(), p))
    for k, (m_run, acc) in live.items():
        states[k][0][...] = m_run
        states[k][1][...] = acc


def _softmax_loop(n_steps, step, state):
    done = 0
    for group in (4, 2, 1):
        def body(i, carry, group=group, done=done):
            _softmax_block([step(done + group * i + j) + (0,) for j in range(group)], [state])
            return carry

        n_groups = (n_steps - done) // group
        lax.fori_loop(0, n_groups, body, 0)
        done = done + group * n_groups


def _norm_proj_kernel(x_ref, g_ref, w_ref, scale_ref, o_ref, *gate_ref, n_main, col_chunk):
    xn = _rms(x_ref[...], g_ref[...]).astype(BF16)
    for c in range(n_main // col_chunk):
        sl = slice(c * col_chunk, (c + 1) * col_chunk)
        o_ref[:, sl] = (_dot(xn, w_ref[:, sl]) * scale_ref[:, sl]).astype(o_ref.dtype)
    if gate_ref:
        logits = _dot(xn, w_ref[:, n_main:n_main + GATE_PAD])
        gate_ref[0][...] = jax.nn.sigmoid(logits)


def norm_proj(x2d, g, w_bf16, n_main, q_cols, with_gates, tm=512, col_chunk=512):
    T, D = x2d.shape
    n_w = w_bf16.shape[1]
    col = jnp.arange(n_main)
    is_q = functools.reduce(jnp.logical_or, [(col >= lo) & (col < hi) for lo, hi in q_cols])
    col_scale = jnp.where(is_q, Q_SCALE, 1.0).astype(F32).reshape(1, n_main)
    out_shape = [jax.ShapeDtypeStruct((T, n_main), BF16)]
    out_specs = [pl.BlockSpec((tm, n_main), lambda i: (i, 0))]
    if with_gates:
        out_shape.append(jax.ShapeDtypeStruct((T, GATE_PAD), F32))
        out_specs.append(pl.BlockSpec((tm, GATE_PAD), lambda i: (i, 0)))
    res = pl.pallas_call(
        functools.partial(_norm_proj_kernel, n_main=n_main, col_chunk=col_chunk),
        grid=(T // tm,),
        in_specs=[pl.BlockSpec((tm, D), lambda i: (i, 0)),
                  pl.BlockSpec((1, D), lambda i: (0, 0)),
                  pl.BlockSpec((D, n_w), lambda i: (0, 0)),
                  pl.BlockSpec((1, n_main), lambda i: (0, 0))],
        out_specs=out_specs,
        out_shape=out_shape,
        compiler_params=_params(1),
        name="norm_proj",
    )(x2d, g.reshape(1, D), w_bf16, col_scale)
    return res if with_gates else res[0]


def _head_pair_rows(q, t):
    lane = lax.broadcasted_iota(jnp.int32, (t, 2 * HEAD_DIM), 1)
    zero = jnp.zeros_like(q)
    return jnp.where(lane < HEAD_DIM, q, zero), jnp.where(lane >= HEAD_DIM, q, zero)


def _sb_kernel(q_ref, k_ref, vt_ref, o_ref, acc_ref, carry_ref, *, t):
    qi = pl.program_id(2)
    cols = 2 * t
    q_both = jnp.concatenate(_head_pair_rows(q_ref[0], t), axis=0)
    s_idx = lax.broadcasted_iota(jnp.int32, (t, 2 * t), 0)
    j_idx = lax.broadcasted_iota(jnp.int32, (t, 2 * t), 1) % t
    upper2 = jnp.where(j_idx > s_idx, 1.0, 0.0).astype(BF16)
    key = lax.broadcasted_iota(jnp.int32, (t, cols), 0)
    qry = lax.broadcasted_iota(jnp.int32, (t, cols), 1) % t
    past_diag = key < qry

    acc_ref[...] = jnp.zeros_like(acc_ref)
    carry_ref[...] = jnp.zeros_like(carry_ref)

    def chunk(kc, masked):
        start = pl.multiple_of(kc * t, t)
        z = _dot_nt(k_ref[0, pl.ds(start, t), :], q_both)
        drop = jnp.maximum(z, 0.0) + jnp.log2(1.0 + jnp.exp2(jnp.abs(z) * -1.0))
        log_beta = z - drop
        if masked:
            drop = jnp.where(past_diag, drop, 0.0)
        hi, lo = _split_bf16(drop)
        tail = _dot(upper2, jnp.concatenate([hi, lo], axis=0))
        carry = carry_ref[...]
        w = jnp.exp2(log_beta - tail - carry)
        if masked:
            w = jnp.where(past_diag, w, 0.0)
        carry = carry + tail[0:1] + drop[0:1]
        carry_ref[...] = carry
        acc_ref[...] += _dot(vt_ref[0, :, pl.ds(start, t)], w.astype(BF16))
        return jnp.min(carry)

    def cond(state):
        j, least_carry = state
        return (j < qi) & (least_carry < -UNDERFLOW_LOG2)

    def body(state):
        j, _ = state
        return j + 1, chunk(qi - 1 - j, False)

    lax.while_loop(cond, body, (0, chunk(qi, True)))
    row = lax.broadcasted_iota(jnp.int32, (2 * HEAD_DIM, t), 0)
    o_t = jnp.where(row < HEAD_DIM, acc_ref[:, :t], acc_ref[:, t:])
    o_ref[0] = o_t.T.astype(o_ref.dtype)


def sb_attention(proj, v_t, t=256):
    B, S, _ = proj.shape
    n_pairs = 4
    return pl.pallas_call(
        functools.partial(_sb_kernel, t=t),
        grid=(B, n_pairs, S // t),
        in_specs=[pl.BlockSpec((1, t, 128), lambda b, p, i: (b, i, p)),
                  pl.BlockSpec((1, S, 128), lambda b, p, i: (b, 0, n_pairs + p)),
                  pl.BlockSpec((1, 128, S), lambda b, p, i: (b, p, 0))],
        out_specs=pl.BlockSpec((1, t, 128), lambda b, p, i: (b, i, p)),
        out_shape=jax.ShapeDtypeStruct((B, S, n_pairs * 128), BF16),
        scratch_shapes=[pltpu.VMEM((128, 2 * t), F32), pltpu.VMEM((1, 2 * t), F32)],
        compiler_params=_params(3),
        name="sb_attention",
    )(proj, proj, v_t)


def _diff_kernel(slopes_ref, lam_ref, q_ref, k_ref, vt_ref, pos_ref, sl_ref, subln_ref, o_ref,
                 m_ref, acc_ref, knorm_ref, *, t, out_scale, lambda_init):
    h = pl.program_id(1)
    qi = pl.program_id(2)
    slope = slopes_ref[h]
    cols = 2 * t
    slope_cols = jnp.broadcast_to(sl_ref[0], (t, 128))
    q_parts = _head_pair_rows(q_ref[0], t)
    q_both = jnp.concatenate([jnp.concatenate([qc, slope_cols], axis=1) for qc in q_parts],
                             axis=0)
    lane = lax.broadcasted_iota(jnp.int32, (t, 2 * HEAD_DIM), 1)

    @pl.when(qi == 0)
    def _():
        def body(c, best):
            k = k_ref[0, pl.ds(pl.multiple_of(c * t, t), t), :].astype(F32)
            k2 = k * k
            return (jnp.maximum(best[0], jnp.max(jnp.sum(jnp.where(lane < HEAD_DIM, k2, 0.0), axis=1))),
                    jnp.maximum(best[1], jnp.max(jnp.sum(jnp.where(lane >= HEAD_DIM, k2, 0.0), axis=1))))

        best = lax.fori_loop(0, k_ref.shape[1] // t, body, (jnp.float32(0.0), jnp.float32(0.0)))
        knorm_ref[0] = best[0]
        knorm_ref[1] = best[1]

    ones = jnp.ones((8, 2 * HEAD_DIM), BF16)
    q_norm2 = jnp.concatenate(
        [_dot_nt(ones, (qc.astype(F32) * qc.astype(F32)).astype(BF16))[0:1] for qc in q_parts], axis=1)
    col = lax.broadcasted_iota(jnp.int32, (1, cols), 1)
    k_norm2 = jnp.where(col < t, knorm_ref[0], knorm_ref[1])
    score_bound = jnp.sqrt(q_norm2 * k_norm2 * 1.05)
    pos = pos_ref[...]
    offset = (lax.broadcasted_iota(jnp.int32, (t, cols), 0)
              - lax.broadcasted_iota(jnp.int32, (t, cols), 1) % t)

    m_ref[...] = jnp.full_like(m_ref, NEG_INF)
    acc_ref[...] = jnp.zeros_like(acc_ref)

    def step(kc, diagonal=False):
        start = pl.multiple_of(kc * t, t)

        def scores():
            s_t = _dot_nt(jnp.concatenate([k_ref[0, pl.ds(start, t), :], pos], axis=1), q_both)
            return jnp.where(offset <= 0, s_t, NEG_INF) if diagonal else s_t

        return scores, slope * ((kc - qi) * t).astype(F32), lambda: vt_ref[0, 0, :, pl.ds(start, t)]

    state = (m_ref, acc_ref)
    _softmax_block([step(qi, True) + (0,)], [state])
    headroom = jnp.max(score_bound - m_ref[...]) + slope * (t - 1) - UNDERFLOW_LOG2
    n_back = jnp.clip(jnp.ceil(headroom / (slope * t)).astype(jnp.int32), 0, qi)
    _softmax_loop(n_back, lambda i: step(qi - 1 - i), state)

    lam_terms = lam_ref[...]
    lam = (jnp.exp(jnp.sum(lam_terms[0:1] * lam_terms[1:2], axis=1, keepdims=True))
           - jnp.exp(jnp.sum(lam_terms[2:3] * lam_terms[3:4], axis=1, keepdims=True))
           + lambda_init)
    o_t = acc_ref[0:128, :] / acc_ref[128:129, :]
    o = (o_t[:, :t] - lam * o_t[:, t:]).T
    o_ref[0] = (_rms(o, subln_ref[...]) * out_scale).astype(o_ref.dtype)


def diff_attention(proj, vt_aug, lam_rows, subln, slopes, layer, t=512):
    B, S, _ = proj.shape
    n_heads = 4
    lambda_init = 0.8 - 0.6 * math.exp(-0.3 * layer)
    smem = pl.BlockSpec(memory_space=pltpu.SMEM)
    pos = _key_position_columns(t, t, 128)
    slope_cols = _slope_pieces(slopes, 128).reshape(n_heads, 1, 128)
    return pl.pallas_call(
        functools.partial(_diff_kernel, t=t, out_scale=1.0 - lambda_init, lambda_init=lambda_init),
        grid=(B, n_heads, S // t),
        in_specs=[smem,
                  pl.BlockSpec((4, HEAD_DIM), lambda b, h, i: (0, 0)),
                  pl.BlockSpec((1, t, 128), lambda b, h, i: (b, i, 12 + h)),
                  pl.BlockSpec((1, S, 128), lambda b, h, i: (b, 0, 16 + h)),
                  pl.BlockSpec((1, 1, V_ROWS_128, S), lambda b, h, i: (b, h, 0, 0)),
                  pl.BlockSpec((t, 128), lambda b, h, i: (0, 0)),
                  pl.BlockSpec((1, 1, 128), lambda b, h, i: (h, 0, 0)),
                  pl.BlockSpec((1, 128), lambda b, h, i: (0, 0))],
        out_specs=pl.BlockSpec((1, t, 128), lambda b, h, i: (b, i, h)),
        out_shape=jax.ShapeDtypeStruct((B, S, n_heads * 128), BF16),
        scratch_shapes=[pltpu.VMEM((1, 2 * t), F32), pltpu.VMEM((V_ROWS_128, 2 * t), F32),
                        pltpu.SMEM((2,), F32)],
        compiler_params=_params(2, 1),
        name="diff_attention",
    )(slopes, lam_rows, proj, proj, vt_aug, pos, slope_cols, subln.reshape(1, 128))


def _post_kernel(*refs, ff_chunk, final):
    mix_ref, x_ref, wo_ref, g_ref, w1_ref, w2_ref = refs[:6]
    gf_ref = refs[6] if final else None
    o_ref = refs[-1]
    x = x_ref[...] + _dot(mix_ref[...], wo_ref[...])
    hn = _rms(x, g_ref[...]).astype(BF16)
    acc = x
    for f in range(w1_ref.shape[1] // ff_chunk):
        sl = slice(f * ff_chunk, (f + 1) * ff_chunk)
        hid = jnp.maximum(_dot(hn, w1_ref[:, sl]), 0.0)
        acc = acc + _dot((hid * hid).astype(BF16), w2_ref[sl, :])
    if final:
        acc = _rms(acc, gf_ref[...])
    o_ref[...] = acc


def post_block(mix, x2d, w_out, g_mlp, w1, w2, g_final=None, tm=512, ff_chunk=1024):
    T, D = x2d.shape
    final = g_final is not None
    const = lambda i: (0, 0)
    in_specs = [pl.BlockSpec((tm, mix.shape[1]), lambda i: (i, 0)),
                pl.BlockSpec((tm, D), lambda i: (i, 0)),
                pl.BlockSpec(w_out.shape, const),
                pl.BlockSpec((1, D), const), pl.BlockSpec(w1.shape, const), pl.BlockSpec(w2.shape, const)]
    args = [mix, x2d, w_out, g_mlp.reshape(1, D), w1, w2]
    if final:
        in_specs.append(pl.BlockSpec((1, D), const))
        args.append(g_final.reshape(1, D))
    return pl.pallas_call(
        functools.partial(_post_kernel, ff_chunk=ff_chunk, final=final),
        grid=(T // tm,),
        in_specs=in_specs,
        out_specs=pl.BlockSpec((tm, D), lambda i: (i, 0)),
        out_shape=jax.ShapeDtypeStruct((T, D), F32),
        compiler_params=_params(1),
        name="post_block",
    )(*args)


def _compress_kernel(c_ref, pos_ref, w1_ref, w2_ref, o_ref):
    half = w1_ref.shape[1] // 2
    chunks = c_ref[0, 0]
    pos = jnp.broadcast_to(pos_ref[0], (8, 2 * half)).astype(BF16)
    first = _dot(chunks, w1_ref[0, :half, :])
    second = _dot(chunks, w1_ref[0, half:, :])
    n_chunks = chunks.shape[0]
    pre = first + pltpu.roll(second, n_chunks - 1, 0) + _dot(pos, w1_ref[0])[0:1]
    hid = jax.nn.gelu(pre)
    o_ref[0, 0] = _dot(hid.astype(BF16), w2_ref[0]).astype(o_ref.dtype)


def compress_kv(chunks, pos_flat, w1, w2):
    _, BG, n_chunks, width = chunks.shape
    hidden = w1.shape[-1]
    return pl.pallas_call(
        _compress_kernel,
        grid=(2, BG),
        in_specs=[pl.BlockSpec((1, 1, n_chunks, width), lambda s, i: (s, i, 0, 0)),
                  pl.BlockSpec((1, 1, 2 * width), lambda s, i: (s, 0, 0)),
                  pl.BlockSpec((1, 2 * width, hidden), lambda s, i: (s, 0, 0)),
                  pl.BlockSpec((1, hidden, HEAD_DIM), lambda s, i: (s, 0, 0))],
        out_specs=pl.BlockSpec((1, 1, n_chunks, HEAD_DIM), lambda s, i: (s, i, 0, 0)),
        out_shape=jax.ShapeDtypeStruct((2, BG, n_chunks, HEAD_DIM), BF16),
        compiler_params=_params(2),
        name="compress_kv",
    )(chunks, pos_flat, w1, w2)


def _stack_heads(q):
    return jnp.concatenate([q[:, r * HEAD_DIM:(r + 1) * HEAD_DIM] for r in range(NSA_HPG)], axis=0)


def _slope_row(slopes_ref, g, tq):
    col = lax.broadcasted_iota(jnp.int32, (1, NSA_HPG * tq), 1)
    out = jnp.zeros((1, NSA_HPG * tq), F32)
    for r in range(NSA_HPG):
        out = jnp.where(col // tq == r, slopes_ref[g * NSA_HPG + r], out)
    return out


def _cmp_select_kernel(slopes_ref, q_ref, kc_ref, vct_ref, ovt_ref, ocmp_ref, sel_ref, hits_ref,
                       *, tq, n_sel, blocks_per_chunk, hit_tile):
    g = pl.program_id(1)
    t0 = pl.program_id(2) * tq
    cols = NSA_HPG * tq
    n_cmp = kc_ref.shape[2]
    slope_row = _slope_row(slopes_ref, g, tq)
    q_rows = _stack_heads(q_ref[0])
    tpos = t0 + lax.broadcasted_iota(jnp.int32, (n_cmp, cols), 1) % tq
    cmp_end = lax.broadcasted_iota(jnp.int32, (n_cmp, cols), 0) * CMP_STRIDE + (CMP_LEN - 1)
    dc = (tpos - cmp_end).astype(F32)
    sc = jnp.where(dc >= 0, _dot_nt(kc_ref[0, 0], q_rows) - slope_row * dc, NEG_INF)
    e = jnp.exp2(sc - jnp.max(sc, axis=0, keepdims=True))
    any_valid = jnp.where(dc[0:1] >= 0, 1.0, 0.0)
    pc = e * (any_valid / jnp.sum(e, axis=0, keepdims=True))
    o_t = _dot(vct_ref[0, 0], pc.astype(BF16))
    for r in range(NSA_HPG):
        ocmp_ref[0, 0, r] = o_t[:, r * tq:(r + 1) * tq]

    pc_group = pc[:, 0:tq]
    for r in range(1, NSA_HPG):
        pc_group = pc_group + pc[:, r * tq:(r + 1) * tq]
    hi, lo = _split_bf16(pc_group)
    imp = _dot(ovt_ref[...], hi) + _dot(ovt_ref[...], lo)

    blk = lax.broadcasted_iota(jnp.int32, (n_sel, tq), 0)
    cur = (t0 + lax.broadcasted_iota(jnp.int32, (n_sel, tq), 1)) // SEL_LEN
    forced = (blk == 0) | (blk == cur) | (blk == cur - 1)
    imp = jnp.where(forced, FORCE_SCORE, imp)
    imp = jnp.where(blk <= cur, imp, -1.0)
    picked = jnp.zeros((n_sel, tq), F32)
    for _ in range(min(SEL_TOPK, n_sel)):
        best = jnp.max(imp, axis=0, keepdims=True)
        first = jnp.min(jnp.where(imp == best, blk, n_sel), axis=0, keepdims=True)
        hit = blk == first
        picked = jnp.where(hit, 1.0, picked)
        imp = jnp.where(hit, -2.0, imp)
    keep = (picked > 0.5) & (blk <= cur)
    sel_ref[0, 0] = jnp.where(keep, 0.0, -SEL_DROP).T.astype(sel_ref.dtype)
    n_chunks = n_sel // blocks_per_chunk
    member = (lax.broadcasted_iota(jnp.int32, (n_chunks, n_sel), 1) // blocks_per_chunk
              == lax.broadcasted_iota(jnp.int32, (n_chunks, n_sel), 0))
    per_query = _dot(jnp.where(member, 1.0, 0.0).astype(BF16),
                     jnp.where(keep, 1.0, 0.0).astype(BF16)).astype(BF16)
    for part in range(tq // hit_tile):
        hits_ref[0, 0, part] = _dot(per_query[:, part * hit_tile:(part + 1) * hit_tile],
                                    jnp.ones((hit_tile, 128), BF16))


def cmp_select(proj, kc, vc_t, overlap_t, slopes, tk, hit_tile, tq=256):
    B, S, _ = proj.shape
    n_cmp = kc.shape[2]
    n_sel = S // SEL_LEN
    G = NSA_GROUPS
    smem = pl.BlockSpec(memory_space=pltpu.SMEM)
    return pl.pallas_call(
        functools.partial(_cmp_select_kernel, tq=tq, n_sel=n_sel, blocks_per_chunk=tk // SEL_LEN,
                          hit_tile=hit_tile),
        grid=(B, G, S // tq),
        in_specs=[smem,
                  pl.BlockSpec((1, tq, 256), lambda b, g, i: (b, i, g)),
                  pl.BlockSpec((1, 1, n_cmp, HEAD_DIM), lambda b, g, i: (b, g, 0, 0)),
                  pl.BlockSpec((1, 1, HEAD_DIM, n_cmp), lambda b, g, i: (b, g, 0, 0)),
                  pl.BlockSpec((n_sel, n_cmp), lambda b, g, i: (0, 0))],
        out_specs=[pl.BlockSpec((1, 1, NSA_HPG, HEAD_DIM, tq), lambda b, g, i: (b, g, 0, 0, i)),
                   pl.BlockSpec((1, 1, tq, n_sel), lambda b, g, i: (b, g, i, 0)),
                   pl.BlockSpec((1, 1, tq // hit_tile, S // tk, 128), lambda b, g, i: (b, g, i, 0, 0))],
        out_shape=[jax.ShapeDtypeStruct((B, G, NSA_HPG, HEAD_DIM, S), F32),
                   jax.ShapeDtypeStruct((B, G, S, n_sel), BF16),
                   jax.ShapeDtypeStruct((B, G, S // hit_tile, S // tk, 128), F32)],
        compiler_params=_params(3),
        name="cmp_select",
    )(slopes, proj, kc, vc_t, overlap_t)


def _sel_win_kernel(slopes_ref, active_ref, q_ref, sl_ref, ks_ref, vst_ref, kw_ref, vwt_ref, sel_ref,
                    ocmp_ref, gate_ref, o_ref, ms_ref, accs_ref, mw_ref, accw_ref, todo_ref, *, tq, tk):
    g = pl.program_id(1)
    t0 = pl.program_id(2) * tq
    cols = NSA_HPG * tq
    slope_row = _slope_row(slopes_ref, g, tq)
    q = q_ref[0]
    q_aug = jnp.concatenate(
        [jnp.concatenate([q[:, r * HEAD_DIM:(r + 1) * HEAD_DIM],
                          jnp.broadcast_to(sl_ref[0, r:r + 1, :], (tq, HEAD_DIM))], axis=1)
         for r in range(NSA_HPG)], axis=0)
    q_sel = jnp.concatenate([q_aug, jnp.concatenate([sel_ref[0, 0]] * NSA_HPG, axis=0)], axis=1)
    offset = (lax.broadcasted_iota(jnp.int32, (tk, cols), 0)
              - lax.broadcasted_iota(jnp.int32, (tk, cols), 1) % tq)

    sel_state, win_state = (ms_ref, accs_ref), (mw_ref, accw_ref)
    for m_ref, acc_ref in (sel_state, win_state):
        m_ref[...] = jnp.full_like(m_ref, NEG_INF)
        acc_ref[...] = jnp.zeros_like(acc_ref)

    def step(c, k_ref, q_rows, vt_ref, mask=None):
        inside = c >= 0
        start = pl.multiple_of(jnp.maximum(c, 0) * tk, tk)
        shift = c * tk - t0

        def scores():
            s_t = _dot_nt(k_ref[0, 0, pl.ds(start, tk), :], q_rows)
            return s_t if mask is None else jnp.where(mask(offset + shift) & inside, s_t, NEG_INF)

        kappa = jnp.where(inside, slope_row * shift.astype(F32), NEG_INF)
        return scores, kappa, lambda: vt_ref[0, 0, :, pl.ds(start, tk)]

    last = t0 // tk

    def note_active(c, n):
        hit = active_ref[0, 0, 0, 0, c] > 0

        @pl.when(hit)
        def _():
            todo_ref[n] = c

        return n + hit.astype(jnp.int32)

    n_todo = lax.fori_loop(0, last, note_active, 0)
    _softmax_loop(n_todo, lambda i: step(todo_ref[i], ks_ref, q_sel, vst_ref), sel_state)

    causal = lambda d: d <= 0
    recent = lambda d: d > -WINDOW
    anything = lambda d: True
    _softmax_block([step(last - 2, kw_ref, q_aug, vwt_ref, recent) + (1,),
                    step(last, ks_ref, q_sel, vst_ref, causal) + (0,),
                    step(last - 1, kw_ref, q_aug, vwt_ref, anything) + (1,),
                    step(last, kw_ref, q_aug, vwt_ref, causal) + (1,)], [sel_state, win_state])
    o_sel = accs_ref[0:HEAD_DIM, :] / accs_ref[HEAD_DIM:HEAD_DIM + 1, :]
    o_win = accw_ref[0:HEAD_DIM, :] / accw_ref[HEAD_DIM:HEAD_DIM + 1, :]

    gates = gate_ref[0, 0]
    outs = []
    for r in range(NSA_HPG):
        cs = slice(r * tq, (r + 1) * tq)
        outs.append(gates[3 * r:3 * r + 1] * ocmp_ref[0, 0, r] + gates[3 * r + 1:3 * r + 2] * o_sel[:, cs]
                    + gates[3 * r + 2:3 * r + 3] * o_win[:, cs])
    o_ref[0] = jnp.concatenate(outs, axis=0).T.astype(o_ref.dtype)


def sel_win_attention(proj, active, slope_cols, ks, vs_t, kw, vw_t, sel_bias, o_cmp, gates_t,
                      slopes, tq=128, tk=256):
    B, S, _ = proj.shape
    assert tq == tk and WINDOW == 2 * tk
    G = NSA_GROUPS
    n_sel = S // SEL_LEN
    cols = NSA_HPG * tq
    smem = pl.BlockSpec(memory_space=pltpu.SMEM)
    ks_spec = pl.BlockSpec((1, 1, S, ks.shape[-1]), lambda b, g, i: (b, g, 0, 0))
    kw_spec = pl.BlockSpec((1, 1, S, kw.shape[-1]), lambda b, g, i: (b, g, 0, 0))
    v_spec = pl.BlockSpec((1, 1, V_ROWS_64, S), lambda b, g, i: (b, g, 0, 0))
    return pl.pallas_call(
        functools.partial(_sel_win_kernel, tq=tq, tk=tk),
        grid=(B, G, S // tq),
        in_specs=[smem,
                  pl.BlockSpec((1, 1, 1, 1, S // tk), lambda b, g, i: (b, g, i, 0, 0),
                               memory_space=pltpu.SMEM),
                  pl.BlockSpec((1, tq, 256), lambda b, g, i: (b, i, g)),
                  pl.BlockSpec((1, NSA_HPG, HEAD_DIM), lambda b, g, i: (g, 0, 0)),
                  ks_spec, v_spec, kw_spec, v_spec,
                  pl.BlockSpec((1, 1, tq, n_sel), lambda b, g, i: (b, g, i, 0)),
                  pl.BlockSpec((1, 1, NSA_HPG, HEAD_DIM, tq), lambda b, g, i: (b, g, 0, 0, i)),
                  pl.BlockSpec((1, 1, NSA_HPG * N_GATES, tq), lambda b, g, i: (b, g, 0, i))],
        out_specs=pl.BlockSpec((1, tq, 256), lambda b, g, i: (b, i, g)),
        out_shape=jax.ShapeDtypeStruct((B, S, G * 256), BF16),
        scratch_shapes=[pltpu.VMEM((1, cols), F32), pltpu.VMEM((V_ROWS_64, cols), F32),
                        pltpu.VMEM((1, cols), F32), pltpu.VMEM((V_ROWS_64, cols), F32),
                        pltpu.SMEM((S // tk,), jnp.int32)],
        compiler_params=_params(3),
        name="sel_win_attention",
    )(slopes, active, proj, slope_cols, ks, vs_t, kw, vw_t, sel_bias, o_cmp, gates_t)


def _alibi_slopes_log2(n_heads):
    slopes = np.exp2(-8.0 * (np.arange(n_heads, dtype=np.float32) + 1.0) / n_heads)
    return (slopes.astype(np.float32) * np.float32(LOG2E)).astype(np.float32)


def even_layer_mix(x2d, B, S, norm_g, w_in, lam_q1, lam_k1, lam_q2, lam_k2, subln, layer):
    proj = norm_proj(x2d, norm_g, w_in.astype(BF16), w_in.shape[1], [(0, 512), (1536, 2048)],
                     False).reshape(B, S, -1)
    sb_vt = proj[:, :, 1024:1536].transpose(0, 2, 1)
    df_vt = proj[:, :, 2560:3072].transpose(0, 2, 1).reshape(B, 4, 128, S)
    o_sb = sb_attention(proj, sb_vt)
    lam_rows = jnp.stack([lam_q1, lam_k1, lam_q2, lam_k2]).astype(F32)
    o_df = diff_attention(proj, _with_ones_row(df_vt, V_ROWS_128), lam_rows, subln.astype(F32),
                          _alibi_slopes_log2(4), layer)
    return jnp.concatenate([o_sb, o_df], axis=2).reshape(B * S, -1)


def odd_layer_mix(x2d, B, S, norm_g, w_in, pos_k, k_w1, k_w2, pos_v, v_w1, v_w2, tk=256):
    G, Dh = NSA_GROUPS, HEAD_DIM
    q_width = NSA_GROUPS * NSA_HPG * Dh
    kv_width = G * Dh
    n_main = q_width + 6 * kv_width
    pad = n_main + GATE_PAD - w_in.shape[1]
    w_pad = jnp.pad(w_in, ((0, 0), (0, pad))).astype(BF16)
    proj, gates = norm_proj(x2d, norm_g, w_pad, n_main, [(0, q_width)], True)
    proj = proj.reshape(B, S, n_main)

    def group_major(j):
        cols = proj[:, :, q_width + j * kv_width:q_width + (j + 1) * kv_width]
        return cols.reshape(B, S, G, Dh).transpose(0, 2, 1, 3)

    def keys_with_positions(j, extra=()):
        cols = [_key_position_columns(S, tk, Dh)] + list(extra)
        return jnp.concatenate([group_major(j)] + [jnp.broadcast_to(c, (B, G) + c.shape) for c in cols],
                               axis=-1)

    def values_transposed(j):
        return _with_ones_row(group_major(j).transpose(0, 1, 3, 2), V_ROWS_64)

    n_chunks = S // CMP_STRIDE
    chunks = jnp.stack([group_major(0), group_major(1)]).reshape(2, B * G, n_chunks, CMP_STRIDE * Dh)
    pos_flat = jnp.stack([pos_k, pos_v]).reshape(2, 1, CMP_LEN * Dh).astype(F32)
    w1 = jnp.stack([k_w1, v_w1]).astype(BF16)
    w2 = jnp.stack([k_w2, v_w2]).astype(BF16)
    cmp = compress_kv(chunks, pos_flat, w1, w2).reshape(2, B, G, n_chunks, Dh)

    n_sel = S // SEL_LEN
    cmp_start = jnp.arange(n_chunks) * CMP_STRIDE
    sel_start = jnp.arange(n_sel) * SEL_LEN
    overlap_t = ((cmp_start[None, :] < sel_start[:, None] + SEL_LEN)
                 & (sel_start[:, None] <= cmp_start[None, :] + CMP_LEN - 1)).astype(BF16)
    onehot = (jnp.arange(S)[:, None] // SEL_LEN == jnp.arange(n_sel)[None, :]).astype(BF16)
    slopes = _alibi_slopes_log2(NSA_GROUPS * NSA_HPG)
    slope_cols = _slope_pieces(slopes, Dh).reshape(G, NSA_HPG, Dh)

    tq = 256
    o_cmp, sel_bias, hits = cmp_select(proj, cmp[0], cmp[1].transpose(0, 1, 3, 2), overlap_t, slopes,
                                       tk, tq)
    active = (hits[..., 0] > 0).astype(jnp.int32)[:, :, :, None, :]
    gates_t = (gates[:, :G * NSA_HPG * N_GATES].reshape(B, S, G, NSA_HPG * N_GATES)
               .transpose(0, 2, 3, 1))
    o = sel_win_attention(proj, active, slope_cols, keys_with_positions(2, [onehot]),
                          values_transposed(3), keys_with_positions(4), values_transposed(5),
                          sel_bias, o_cmp, gates_t, slopes, tq=tq, tk=tk)
    return o.reshape(B * S, q_width)


def kernel(x, attn_norm, mlp_norm, final_norm, ev_w_in, ev_lam_q1, ev_lam_k1, ev_lam_q2, ev_lam_k2,
           ev_subln, ev_w_out, od_w_in, od_cmp_pos_k, od_cmp_k_w1, od_cmp_k_w2, od_cmp_pos_v,
           od_cmp_v_w1, od_cmp_v_w2, od_w_out, mlp_w1, mlp_w2):
    B, S, D = x.shape
    depth = attn_norm.shape[0]
    x2d = x.reshape(B * S, D)
    for layer in range(depth):
        idx = layer // 2
        if layer % 2 == 0:
            mix = even_layer_mix(x2d, B, S, attn_norm[layer], ev_w_in[idx], ev_lam_q1[idx],
                                 ev_lam_k1[idx], ev_lam_q2[idx], ev_lam_k2[idx], ev_subln[idx], layer)
            w_out = ev_w_out[idx]
        else:
            mix = odd_layer_mix(x2d, B, S, attn_norm[layer], od_w_in[idx], od_cmp_pos_k[idx],
                                od_cmp_k_w1[idx], od_cmp_k_w2[idx], od_cmp_pos_v[idx],
                                od_cmp_v_w1[idx], od_cmp_v_w2[idx])
            w_out = od_w_out[idx]
        g_final = final_norm if layer == depth - 1 else None
        x2d = post_block(mix, x2d, w_out.astype(BF16), mlp_norm[layer], mlp_w1[layer].astype(BF16),
                         mlp_w2[layer].astype(BF16), g_final)
    return x2d.reshape(B, S, D)
```

```python
import functools
import math

import jax
import jax.numpy as jnp
import numpy as np
from jax import lax
from jax.experimental import pallas as pl
from jax.experimental.pallas import tpu as pltpu

F32 = jnp.float32
BF16 = jnp.bfloat16

HEAD_DIM = 64
RMS_EPS = 1e-6
NEG_INF = -1e30
FORCE_SCORE = 1e6
NSA_GROUPS = 4
NSA_HPG = 4
CMP_LEN = 32
CMP_STRIDE = 16
SEL_LEN = 64
SEL_TOPK = 16
WINDOW = 512
N_GATES = 3
GATE_PAD = 128
SEL_DROP = 2.0 ** 24
BF16_EXACT_INT = 256
V_ROWS_64 = 80
V_ROWS_128 = 144

LOG2E = math.log2(math.e)
Q_SCALE = HEAD_DIM ** -0.5 * LOG2E
UNDERFLOW_LOG2 = -160.0
SAFE_EXP_LOG2 = 100.0

VMEM_LIMIT = 56 * 1024 * 1024


def _params(n_parallel, n_arbitrary=0):
    return pltpu.CompilerParams(dimension_semantics=("parallel",) * n_parallel + ("arbitrary",) * n_arbitrary,
                                vmem_limit_bytes=VMEM_LIMIT)


def _rms(x, g):
    ms = jnp.mean(x * x, axis=-1, keepdims=True)
    return x * lax.rsqrt(ms + RMS_EPS) * g


def _dot(a, b):
    return jnp.dot(a, b, preferred_element_type=F32)


def _dot_nt(a, b):
    return lax.dot_general(a, b, (((1,), (1,)), ((), ())), preferred_element_type=F32)


def _split_bf16(x):
    hi = x.astype(BF16)
    lo = (x - hi.astype(F32)).astype(BF16)
    return hi, lo


def _slope_pieces(slopes, width):
    def top_bits(x):
        return (x.view(np.uint32) & np.uint32(0xFFFF0000)).view(np.float32)

    s1 = top_bits(slopes)
    r1 = slopes - s1
    s2 = top_bits(r1)
    s3 = top_bits(r1 - s2)
    out = np.zeros((slopes.shape[0], width), np.float32)
    out[:, :6] = np.stack([s1, s2, s3, s1, s2, s3], axis=1)
    return jnp.asarray(out).astype(BF16)


def _key_position_columns(n, tk, width):
    j = jnp.arange(n) % tk
    a = (j // BF16_EXACT_INT) * BF16_EXACT_INT
    b = j % BF16_EXACT_INT
    cols = jnp.stack([a, a, a, b, b, b], axis=1).astype(BF16)
    return jnp.pad(cols, ((0, 0), (0, width - cols.shape[1])))


def _with_ones_row(v_t, rows):
    d = v_t.shape[-2]
    ones = jnp.ones(v_t.shape[:-2] + (1, v_t.shape[-1]), v_t.dtype)
    pad = jnp.zeros(v_t.shape[:-2] + (rows - d - 1, v_t.shape[-1]), v_t.dtype)
    return jnp.concatenate([v_t, ones, pad], axis=-2)


def _softmax_block(steps, states):
    s_all = [scores() for scores, _, _, _ in steps]
    live = {k: (states[k][0][...], states[k][1][...]) for k in sorted({k for _, _, _, k in steps})}
    for (_, kap, values, k), s_t in zip(steps, s_all):
        m_run, acc = live[k]
        m_new = jnp.maximum(m_run, jnp.max(s_t, axis=0, keepdims=True) + kap)
        p = jnp.exp2(s_t - (m_new - kap)).astype(BF16)
        live[k] = (m_new, jnp.exp2(m_run - m_new) * acc + _dot(values(), p))
    for k, (m_run, acc) in live.items():
        states[k][0][...] = m_run
        states[k][1][...] = acc


def _softmax_block_fixed_frame(steps, state):
    m_ref, acc_ref = state
    s_all = [scores() for scores, _, _ in steps]
    frame = m_ref[...]
    m_run, acc = frame, acc_ref[...]
    for (_, kap, values), s_t in zip(steps, s_all):
        m_run = jnp.maximum(m_run, jnp.max(s_t, axis=0, keepdims=True) + kap)
        acc = acc + _dot(values(), jnp.exp2(s_t - (frame - kap)).astype(BF16))
    m_ref[...] = m_run
    acc_ref[...] = acc * jnp.exp2(frame - m_run)


def _softmax_loop(n_steps, step, state, safe):
    def run(groups, block):
        done = 0
        for group in groups:
            def body(i, carry, group=group, done=done):
                block([step(done + group * i + j) for j in range(group)])
                return carry

            n_groups = (n_steps - done) // group
            lax.fori_loop(0, n_groups, body, 0)
            done = done + group * n_groups

    @pl.when(safe)
    def _():
        run((8, 4, 2, 1), lambda steps: _softmax_block_fixed_frame(steps, state))

    @pl.when(jnp.logical_not(safe))
    def _():
        run((4, 2, 1), lambda steps: _softmax_block([st + (0,) for st in steps], [state]))


def _norm_proj_kernel(x_ref, g_ref, w_ref, scale_ref, o_ref, *gate_ref, n_main, col_chunk):
    xn = _rms(x_ref[...], g_ref[...]).astype(BF16)
    for c in range(n_main // col_chunk):
        sl = slice(c * col_chunk, (c + 1) * col_chunk)
        o_ref[:, sl] = (_dot(xn, w_ref[:, sl]) * scale_ref[:, sl]).astype(o_ref.dtype)
    if gate_ref:
        logits = _dot(xn, w_ref[:, n_main:n_main + GATE_PAD])
        gate_ref[0][...] = jax.nn.sigmoid(logits)


def norm_proj(x2d, g, w_bf16, n_main, q_cols, with_gates, tm=512, col_chunk=512):
    T, D = x2d.shape
    n_w = w_bf16.shape[1]
    col = jnp.arange(n_main)
    is_q = functools.reduce(jnp.logical_or, [(col >= lo) & (col < hi) for lo, hi in q_cols])
    col_scale = jnp.where(is_q, Q_SCALE, 1.0).astype(F32).reshape(1, n_main)
    out_shape = [jax.ShapeDtypeStruct((T, n_main), BF16)]
    out_specs = [pl.BlockSpec((tm, n_main), lambda i: (i, 0))]
    if with_gates:
        out_shape.append(jax.ShapeDtypeStruct((T, GATE_PAD), F32))
        out_specs.append(pl.BlockSpec((tm, GATE_PAD), lambda i: (i, 0)))
    res = pl.pallas_call(
        functools.partial(_norm_proj_kernel, n_main=n_main, col_chunk=col_chunk),
        grid=(T // tm,),
        in_specs=[pl.BlockSpec((tm, D), lambda i: (i, 0)),
                  pl.BlockSpec((1, D), lambda i: (0, 0)),
                  pl.BlockSpec((D, n_w), lambda i: (0, 0)),
                  pl.BlockSpec((1, n_main), lambda i: (0, 0))],
        out_specs=out_specs,
        out_shape=out_shape,
        compiler_params=_params(1),
        name="norm_proj",
    )(x2d, g.reshape(1, D), w_bf16, col_scale)
    return res if with_gates else res[0]


def _head_pair_rows(q, t):
    lane = lax.broadcasted_iota(jnp.int32, (t, 2 * HEAD_DIM), 1)
    zero = jnp.zeros_like(q)
    return jnp.where(lane < HEAD_DIM, q, zero), jnp.where(lane >= HEAD_DIM, q, zero)


def _sb_kernel(q_ref, k_ref, vt_ref, o_ref, acc_ref, carry_ref, *, t):
    qi = pl.program_id(2)
    cols = 2 * t
    q_both = jnp.concatenate(_head_pair_rows(q_ref[0], t), axis=0)
    s_idx = lax.broadcasted_iota(jnp.int32, (t, 2 * t), 0)
    j_idx = lax.broadcasted_iota(jnp.int32, (t, 2 * t), 1) % t
    upper2 = jnp.where(j_idx > s_idx, 1.0, 0.0).astype(BF16)
    key = lax.broadcasted_iota(jnp.int32, (t, cols), 0)
    qry = lax.broadcasted_iota(jnp.int32, (t, cols), 1) % t
    past_diag = key < qry

    acc_ref[...] = jnp.zeros_like(acc_ref)
    carry_ref[...] = jnp.zeros_like(carry_ref)

    def chunk(kc, masked):
        start = pl.multiple_of(kc * t, t)
        z = _dot_nt(k_ref[0, pl.ds(start, t), :], q_both)
        drop = jnp.maximum(z, 0.0) + jnp.log2(1.0 + jnp.exp2(jnp.abs(z) * -1.0))
        log_beta = z - drop
        if masked:
            drop = jnp.where(past_diag, drop, 0.0)
        hi, lo = _split_bf16(drop)
        tail = _dot(upper2, jnp.concatenate([hi, lo], axis=0))
        carry = carry_ref[...]
        w = jnp.exp2(log_beta - tail - carry)
        if masked:
            w = jnp.where(past_diag, w, 0.0)
        carry = carry + tail[0:1] + drop[0:1]
        carry_ref[...] = carry
        acc_ref[...] += _dot(vt_ref[0, :, pl.ds(start, t)], w.astype(BF16))
        return jnp.min(carry)

    def cond(state):
        j, least_carry = state
        return (j < qi) & (least_carry < -UNDERFLOW_LOG2)

    def body(state):
        j, _ = state
        return j + 1, chunk(qi - 1 - j, False)

    lax.while_loop(cond, body, (0, chunk(qi, True)))
    row = lax.broadcasted_iota(jnp.int32, (2 * HEAD_DIM, t), 0)
    o_t = jnp.where(row < HEAD_DIM, acc_ref[:, :t], acc_ref[:, t:])
    o_ref[0] = o_t.T.astype(o_ref.dtype)


def sb_attention(proj, v_t, t=256):
    B, S, _ = proj.shape
    n_pairs = 4
    return pl.pallas_call(
        functools.partial(_sb_kernel, t=t),
        grid=(B, n_pairs, S // t),
        in_specs=[pl.BlockSpec((1, t, 128), lambda b, p, i: (b, i, p)),
                  pl.BlockSpec((1, S, 128), lambda b, p, i: (b, 0, n_pairs + p)),
                  pl.BlockSpec((1, 128, S), lambda b, p, i: (b, p, 0))],
        out_specs=pl.BlockSpec((1, t, 128), lambda b, p, i: (b, i, p)),
        out_shape=jax.ShapeDtypeStruct((B, S, n_pairs * 128), BF16),
        scratch_shapes=[pltpu.VMEM((128, 2 * t), F32), pltpu.VMEM((1, 2 * t), F32)],
        compiler_params=_params(3),
        name="sb_attention",
    )(proj, proj, v_t)


def _diff_kernel(slopes_ref, lam_ref, q_ref, k_ref, vt_ref, pos_ref, sl_ref, subln_ref, o_ref,
                 m_ref, acc_ref, knorm_ref, *, t, out_scale, lambda_init):
    h = pl.program_id(1)
    qi = pl.program_id(2)
    slope = slopes_ref[h]
    cols = 2 * t
    slope_cols = jnp.broadcast_to(sl_ref[0], (t, 128))
    q_parts = _head_pair_rows(q_ref[0], t)
    q_both = jnp.concatenate([jnp.concatenate([qc, slope_cols], axis=1) for qc in q_parts],
                             axis=0)
    lane = lax.broadcasted_iota(jnp.int32, (t, 2 * HEAD_DIM), 1)

    @pl.when(qi == 0)
    def _():
        def body(c, best):
            k = k_ref[0, pl.ds(pl.multiple_of(c * t, t), t), :].astype(F32)
            k2 = k * k
            return (jnp.maximum(best[0], jnp.max(jnp.sum(jnp.where(lane < HEAD_DIM, k2, 0.0), axis=1))),
                    jnp.maximum(best[1], jnp.max(jnp.sum(jnp.where(lane >= HEAD_DIM, k2, 0.0), axis=1))))

        best = lax.fori_loop(0, k_ref.shape[1] // t, body, (jnp.float32(0.0), jnp.float32(0.0)))
        knorm_ref[0] = best[0]
        knorm_ref[1] = best[1]

    ones = jnp.ones((8, 2 * HEAD_DIM), BF16)
    q_norm2 = jnp.concatenate(
        [_dot_nt(ones, (qc.astype(F32) * qc.astype(F32)).astype(BF16))[0:1] for qc in q_parts], axis=1)
    col = lax.broadcasted_iota(jnp.int32, (1, cols), 1)
    k_norm2 = jnp.where(col < t, knorm_ref[0], knorm_ref[1])
    score_bound = jnp.sqrt(q_norm2 * k_norm2 * 1.05)
    pos = pos_ref[...]
    offset = (lax.broadcasted_iota(jnp.int32, (t, cols), 0)
              - lax.broadcasted_iota(jnp.int32, (t, cols), 1) % t)

    m_ref[...] = jnp.full_like(m_ref, NEG_INF)
    acc_ref[...] = jnp.zeros_like(acc_ref)

    def step(kc, diagonal=False):
        start = pl.multiple_of(kc * t, t)

        def scores():
            s_t = _dot_nt(jnp.concatenate([k_ref[0, pl.ds(start, t), :], pos], axis=1), q_both)
            return jnp.where(offset <= 0, s_t, NEG_INF) if diagonal else s_t

        return scores, slope * ((kc - qi) * t).astype(F32), lambda: vt_ref[0, 0, :, pl.ds(start, t)]

    state = (m_ref, acc_ref)
    _softmax_block([step(qi, True) + (0,)], [state])
    excess = jnp.max(score_bound - m_ref[...]) + slope * (t - 1)
    n_back = jnp.clip(jnp.ceil((excess - UNDERFLOW_LOG2) / (slope * t)).astype(jnp.int32), 0, qi)
    _softmax_loop(n_back, lambda i: step(qi - 1 - i), state, excess <= SAFE_EXP_LOG2)

    lam_terms = lam_ref[...]
    lam = (jnp.exp(jnp.sum(lam_terms[0:1] * lam_terms[1:2], axis=1, keepdims=True))
           - jnp.exp(jnp.sum(lam_terms[2:3] * lam_terms[3:4], axis=1, keepdims=True))
           + lambda_init)
    o_t = acc_ref[0:128, :] / acc_ref[128:129, :]
    o = (o_t[:, :t] - lam * o_t[:, t:]).T
    o_ref[0] = (_rms(o, subln_ref[...]) * out_scale).astype(o_ref.dtype)


def diff_attention(proj, vt_aug, lam_rows, subln, slopes, layer, t=512):
    B, S, _ = proj.shape
    n_heads = 4
    lambda_init = 0.8 - 0.6 * math.exp(-0.3 * layer)
    smem = pl.BlockSpec(memory_space=pltpu.SMEM)
    pos = _key_position_columns(t, t, 128)
    slope_cols = _slope_pieces(slopes, 128).reshape(n_heads, 1, 128)
    return pl.pallas_call(
        functools.partial(_diff_kernel, t=t, out_scale=1.0 - lambda_init, lambda_init=lambda_init),
        grid=(B, n_heads, S // t),
        in_specs=[smem,
                  pl.BlockSpec((4, HEAD_DIM), lambda b, h, i: (0, 0)),
                  pl.BlockSpec((1, t, 128), lambda b, h, i: (b, i, 12 + h)),
                  pl.BlockSpec((1, S, 128), lambda b, h, i: (b, 0, 16 + h)),
                  pl.BlockSpec((1, 1, V_ROWS_128, S), lambda b, h, i: (b, h, 0, 0)),
                  pl.BlockSpec((t, 128), lambda b, h, i: (0, 0)),
                  pl.BlockSpec((1, 1, 128), lambda b, h, i: (h, 0, 0)),
                  pl.BlockSpec((1, 128), lambda b, h, i: (0, 0))],
        out_specs=pl.BlockSpec((1, t, 128), lambda b, h, i: (b, i, h)),
        out_shape=jax.ShapeDtypeStruct((B, S, n_heads * 128), BF16),
        scratch_shapes=[pltpu.VMEM((1, 2 * t), F32), pltpu.VMEM((V_ROWS_128, 2 * t), F32),
                        pltpu.SMEM((2,), F32)],
        compiler_params=_params(2, 1),
        name="diff_attention",
    )(slopes, lam_rows, proj, proj, vt_aug, pos, slope_cols, subln.reshape(1, 128))


def _post_kernel(*refs, ff_chunk, final):
    mix_ref, x_ref, wo_ref, g_ref, w1_ref, w2_ref = refs[:6]
    gf_ref = refs[6] if final else None
    o_ref = refs[-1]
    x = x_ref[...] + _dot(mix_ref[...], wo_ref[...])
    hn = _rms(x, g_ref[...]).astype(BF16)
    acc = x
    for f in range(w1_ref.shape[1] // ff_chunk):
        sl = slice(f * ff_chunk, (f + 1) * ff_chunk)
        hid = jnp.maximum(_dot(hn, w1_ref[:, sl]), 0.0)
        acc = acc + _dot((hid * hid).astype(BF16), w2_ref[sl, :])
    if final:
        acc = _rms(acc, gf_ref[...])
    o_ref[...] = acc


def post_block(mix, x2d, w_out, g_mlp, w1, w2, g_final=None, tm=512, ff_chunk=1024):
    T, D = x2d.shape
    final = g_final is not None
    const = lambda i: (0, 0)
    in_specs = [pl.BlockSpec((tm, mix.shape[1]), lambda i: (i, 0)),
                pl.BlockSpec((tm, D), lambda i: (i, 0)),
                pl.BlockSpec(w_out.shape, const),
                pl.BlockSpec((1, D), const), pl.BlockSpec(w1.shape, const), pl.BlockSpec(w2.shape, const)]
    args = [mix, x2d, w_out, g_mlp.reshape(1, D), w1, w2]
    if final:
        in_specs.append(pl.BlockSpec((1, D), const))
        args.append(g_final.reshape(1, D))
    return pl.pallas_call(
        functools.partial(_post_kernel, ff_chunk=ff_chunk, final=final),
        grid=(T // tm,),
        in_specs=in_specs,
        out_specs=pl.BlockSpec((tm, D), lambda i: (i, 0)),
        out_shape=jax.ShapeDtypeStruct((T, D), F32),
        compiler_params=_params(1),
        name="post_block",
    )(*args)


def _compress_kernel(c_ref, pos_ref, w1_ref, w2_ref, o_ref):
    half = w1_ref.shape[1] // 2
    chunks = c_ref[0, 0]
    pos = jnp.broadcast_to(pos_ref[0], (8, 2 * half)).astype(BF16)
    first = _dot(chunks, w1_ref[0, :half, :])
    second = _dot(chunks, w1_ref[0, half:, :])
    n_chunks = chunks.shape[0]
    pre = first + pltpu.roll(second, n_chunks - 1, 0) + _dot(pos, w1_ref[0])[0:1]
    hid = jax.nn.gelu(pre)
    o_ref[0, 0] = _dot(hid.astype(BF16), w2_ref[0]).astype(o_ref.dtype)


def compress_kv(chunks, pos_flat, w1, w2):
    _, BG, n_chunks, width = chunks.shape
    hidden = w1.shape[-1]
    return pl.pallas_call(
        _compress_kernel,
        grid=(2, BG),
        in_specs=[pl.BlockSpec((1, 1, n_chunks, width), lambda s, i: (s, i, 0, 0)),
                  pl.BlockSpec((1, 1, 2 * width), lambda s, i: (s, 0, 0)),
                  pl.BlockSpec((1, 2 * width, hidden), lambda s, i: (s, 0, 0)),
                  pl.BlockSpec((1, hidden, HEAD_DIM), lambda s, i: (s, 0, 0))],
        out_specs=pl.BlockSpec((1, 1, n_chunks, HEAD_DIM), lambda s, i: (s, i, 0, 0)),
        out_shape=jax.ShapeDtypeStruct((2, BG, n_chunks, HEAD_DIM), BF16),
        compiler_params=_params(2),
        name="compress_kv",
    )(chunks, pos_flat, w1, w2)


def _stack_heads(q):
    return jnp.concatenate([q[:, r * HEAD_DIM:(r + 1) * HEAD_DIM] for r in range(NSA_HPG)], axis=0)


def _slope_row(slopes_ref, g, tq):
    col = lax.broadcasted_iota(jnp.int32, (1, NSA_HPG * tq), 1)
    out = jnp.zeros((1, NSA_HPG * tq), F32)
    for r in range(NSA_HPG):
        out = jnp.where(col // tq == r, slopes_ref[g * NSA_HPG + r], out)
    return out


def _cmp_select_kernel(slopes_ref, q_ref, kc_ref, vct_ref, ovt_ref, ocmp_ref, sel_ref, hits_ref,
                       *, tq, n_sel, blocks_per_chunk, hit_tile):
    g = pl.program_id(1)
    t0 = pl.program_id(2) * tq
    cols = NSA_HPG * tq
    n_cmp = kc_ref.shape[2]
    slope_row = _slope_row(slopes_ref, g, tq)
    q_rows = _stack_heads(q_ref[0])
    tpos = t0 + lax.broadcasted_iota(jnp.int32, (n_cmp, cols), 1) % tq
    cmp_end = lax.broadcasted_iota(jnp.int32, (n_cmp, cols), 0) * CMP_STRIDE + (CMP_LEN - 1)
    dc = (tpos - cmp_end).astype(F32)
    sc = jnp.where(dc >= 0, _dot_nt(kc_ref[0, 0], q_rows) - slope_row * dc, NEG_INF)
    e = jnp.exp2(sc - jnp.max(sc, axis=0, keepdims=True))
    any_valid = jnp.where(dc[0:1] >= 0, 1.0, 0.0)
    pc = e * (any_valid / jnp.sum(e, axis=0, keepdims=True))
    o_t = _dot(vct_ref[0, 0], pc.astype(BF16))
    for r in range(NSA_HPG):
        ocmp_ref[0, 0, r] = o_t[:, r * tq:(r + 1) * tq]

    pc_group = pc[:, 0:tq]
    for r in range(1, NSA_HPG):
        pc_group = pc_group + pc[:, r * tq:(r + 1) * tq]
    hi, lo = _split_bf16(pc_group)
    imp = _dot(ovt_ref[...], hi) + _dot(ovt_ref[...], lo)

    blk = lax.broadcasted_iota(jnp.int32, (n_sel, tq), 0)
    cur = (t0 + lax.broadcasted_iota(jnp.int32, (n_sel, tq), 1)) // SEL_LEN
    forced = (blk == 0) | (blk == cur) | (blk == cur - 1)
    imp = jnp.where(forced, FORCE_SCORE, imp)
    imp = jnp.where(blk <= cur, imp, -1.0)
    picked = jnp.zeros((n_sel, tq), F32)
    for _ in range(min(SEL_TOPK, n_sel)):
        best = jnp.max(imp, axis=0, keepdims=True)
        first = jnp.min(jnp.where(imp == best, blk, n_sel), axis=0, keepdims=True)
        hit = blk == first
        picked = jnp.where(hit, 1.0, picked)
        imp = jnp.where(hit, -2.0, imp)
    keep = (picked > 0.5) & (blk <= cur)
    sel_ref[0, 0] = jnp.where(keep, 0.0, -SEL_DROP).T.astype(sel_ref.dtype)
    n_chunks = n_sel // blocks_per_chunk
    member = (lax.broadcasted_iota(jnp.int32, (n_chunks, n_sel), 1) // blocks_per_chunk
              == lax.broadcasted_iota(jnp.int32, (n_chunks, n_sel), 0))
    per_query = _dot(jnp.where(member, 1.0, 0.0).astype(BF16),
                     jnp.where(keep, 1.0, 0.0).astype(BF16)).astype(BF16)
    for part in range(tq // hit_tile):
        hits_ref[0, 0, part] = _dot(per_query[:, part * hit_tile:(part + 1) * hit_tile],
                                    jnp.ones((hit_tile, 128), BF16))


def cmp_select(proj, kc, vc_t, overlap_t, slopes, tk, hit_tile, tq=256):
    B, S, _ = proj.shape
    n_cmp = kc.shape[2]
    n_sel = S // SEL_LEN
    G = NSA_GROUPS
    smem = pl.BlockSpec(memory_space=pltpu.SMEM)
    return pl.pallas_call(
        functools.partial(_cmp_select_kernel, tq=tq, n_sel=n_sel, blocks_per_chunk=tk // SEL_LEN,
                          hit_tile=hit_tile),
        grid=(B, G, S // tq),
        in_specs=[smem,
                  pl.BlockSpec((1, tq, 256), lambda b, g, i: (b, i, g)),
                  pl.BlockSpec((1, 1, n_cmp, HEAD_DIM), lambda b, g, i: (b, g, 0, 0)),
                  pl.BlockSpec((1, 1, HEAD_DIM, n_cmp), lambda b, g, i: (b, g, 0, 0)),
                  pl.BlockSpec((n_sel, n_cmp), lambda b, g, i: (0, 0))],
        out_specs=[pl.BlockSpec((1, 1, NSA_HPG, HEAD_DIM, tq), lambda b, g, i: (b, g, 0, 0, i)),
                   pl.BlockSpec((1, 1, tq, n_sel), lambda b, g, i: (b, g, i, 0)),
                   pl.BlockSpec((1, 1, tq // hit_tile, S // tk, 128), lambda b, g, i: (b, g, i, 0, 0))],
        out_shape=[jax.ShapeDtypeStruct((B, G, NSA_HPG, HEAD_DIM, S), F32),
                   jax.ShapeDtypeStruct((B, G, S, n_sel), BF16),
                   jax.ShapeDtypeStruct((B, G, S // hit_tile, S // tk, 128), F32)],
        compiler_params=_params(3),
        name="cmp_select",
    )(slopes, proj, kc, vc_t, overlap_t)


def _sel_win_kernel(slopes_ref, active_ref, q_ref, sl_ref, ks_ref, vst_ref, kw_ref, vwt_ref, sel_ref,
                    ocmp_ref, gate_ref, o_ref, ms_ref, accs_ref, mw_ref, accw_ref, todo_ref, knorm_ref,
                    *, tq, tk):
    g = pl.program_id(1)
    t0 = pl.program_id(2) * tq
    cols = NSA_HPG * tq
    slope_row = _slope_row(slopes_ref, g, tq)
    q = q_ref[0]
    q_aug = jnp.concatenate(
        [jnp.concatenate([q[:, r * HEAD_DIM:(r + 1) * HEAD_DIM],
                          jnp.broadcast_to(sl_ref[0, r:r + 1, :], (tq, HEAD_DIM))], axis=1)
         for r in range(NSA_HPG)], axis=0)
    q_sel = jnp.concatenate([q_aug, jnp.concatenate([sel_ref[0, 0]] * NSA_HPG, axis=0)], axis=1)
    offset = (lax.broadcasted_iota(jnp.int32, (tk, cols), 0)
              - lax.broadcasted_iota(jnp.int32, (tk, cols), 1) % tq)

    key_lane = lax.broadcasted_iota(jnp.int32, (tk, ks_ref.shape[3]), 1) < HEAD_DIM

    @pl.when(pl.program_id(2) == 0)
    def _():
        def body(c, best):
            k = ks_ref[0, 0, pl.ds(pl.multiple_of(c * tk, tk), tk), :].astype(F32)
            return jnp.maximum(best, jnp.max(jnp.sum(jnp.where(key_lane, k * k, 0.0), axis=1)))

        knorm_ref[0] = lax.fori_loop(0, ks_ref.shape[2] // tk, body, jnp.float32(0.0))

    ones = jnp.ones((8, HEAD_DIM), BF16)
    q32 = q.astype(F32)
    q_norm2 = jnp.concatenate(
        [_dot_nt(ones, (q32 * q32)[:, r * HEAD_DIM:(r + 1) * HEAD_DIM].astype(BF16))[0:1]
         for r in range(NSA_HPG)], axis=1)
    score_bound = jnp.sqrt(q_norm2 * knorm_ref[0] * 1.05) + slope_row * (tk - 1)

    sel_state, win_state = (ms_ref, accs_ref), (mw_ref, accw_ref)
    for m_ref, acc_ref in (sel_state, win_state):
        m_ref[...] = jnp.full_like(m_ref, NEG_INF)
        acc_ref[...] = jnp.zeros_like(acc_ref)

    def step(c, k_ref, q_rows, vt_ref, mask=None):
        inside = c >= 0
        start = pl.multiple_of(jnp.maximum(c, 0) * tk, tk)
        shift = c * tk - t0

        def scores():
            s_t = _dot_nt(k_ref[0, 0, pl.ds(start, tk), :], q_rows)
            return s_t if mask is None else jnp.where(mask(offset + shift) & inside, s_t, NEG_INF)

        kappa = jnp.where(inside, slope_row * shift.astype(F32), NEG_INF)
        return scores, kappa, lambda: vt_ref[0, 0, :, pl.ds(start, tk)]

    last = t0 // tk
    causal = lambda d: d <= 0
    recent = lambda d: d > -WINDOW
    anything = lambda d: True
    _softmax_block([step(last - 2, kw_ref, q_aug, vwt_ref, recent) + (1,),
                    step(last, ks_ref, q_sel, vst_ref, causal) + (0,),
                    step(last - 1, kw_ref, q_aug, vwt_ref, anything) + (1,),
                    step(last, kw_ref, q_aug, vwt_ref, causal) + (1,)], [sel_state, win_state])

    def note_active(c, n):
        hit = active_ref[0, 0, 0, 0, c] > 0

        @pl.when(hit)
        def _():
            todo_ref[n] = c

        return n + hit.astype(jnp.int32)

    n_todo = lax.fori_loop(0, last, note_active, 0)
    safe = jnp.max(score_bound - ms_ref[...]) <= SAFE_EXP_LOG2
    _softmax_loop(n_todo, lambda i: step(todo_ref[i], ks_ref, q_sel, vst_ref), sel_state, safe)
    o_sel = accs_ref[0:HEAD_DIM, :] / accs_ref[HEAD_DIM:HEAD_DIM + 1, :]
    o_win = accw_ref[0:HEAD_DIM, :] / accw_ref[HEAD_DIM:HEAD_DIM + 1, :]

    gates = gate_ref[0, 0]
    outs = []
    for r in range(NSA_HPG):
        cs = slice(r * tq, (r + 1) * tq)
        outs.append(gates[3 * r:3 * r + 1] * ocmp_ref[0, 0, r] + gates[3 * r + 1:3 * r + 2] * o_sel[:, cs]
                    + gates[3 * r + 2:3 * r + 3] * o_win[:, cs])
    o_ref[0] = jnp.concatenate(outs, axis=0).T.astype(o_ref.dtype)


def sel_win_attention(proj, active, slope_cols, ks, vs_t, kw, vw_t, sel_bias, o_cmp, gates_t,
                      slopes, tq=128, tk=256):
    B, S, _ = proj.shape
    assert tq == tk and WINDOW == 2 * tk
    G = NSA_GROUPS
    n_sel = S // SEL_LEN
    cols = NSA_HPG * tq
    smem = pl.BlockSpec(memory_space=pltpu.SMEM)
    ks_spec = pl.BlockSpec((1, 1, S, ks.shape[-1]), lambda b, g, i: (b, g, 0, 0))
    kw_spec = pl.BlockSpec((1, 1, S, kw.shape[-1]), lambda b, g, i: (b, g, 0, 0))
    v_spec = pl.BlockSpec((1, 1, V_ROWS_64, S), lambda b, g, i: (b, g, 0, 0))
    return pl.pallas_call(
        functools.partial(_sel_win_kernel, tq=tq, tk=tk),
        grid=(B, G, S // tq),
        in_specs=[smem,
                  pl.BlockSpec((1, 1, 1, 1, S // tk), lambda b, g, i: (b, g, i, 0, 0),
                               memory_space=pltpu.SMEM),
                  pl.BlockSpec((1, tq, 256), lambda b, g, i: (b, i, g)),
                  pl.BlockSpec((1, NSA_HPG, HEAD_DIM), lambda b, g, i: (g, 0, 0)),
                  ks_spec, v_spec, kw_spec, v_spec,
                  pl.BlockSpec((1, 1, tq, n_sel), lambda b, g, i: (b, g, i, 0)),
                  pl.BlockSpec((1, 1, NSA_HPG, HEAD_DIM, tq), lambda b, g, i: (b, g, 0, 0, i)),
                  pl.BlockSpec((1, 1, NSA_HPG * N_GATES, tq), lambda b, g, i: (b, g, 0, i))],
        out_specs=pl.BlockSpec((1, tq, 256), lambda b, g, i: (b, i, g)),
        out_shape=jax.ShapeDtypeStruct((B, S, G * 256), BF16),
        scratch_shapes=[pltpu.VMEM((1, cols), F32), pltpu.VMEM((V_ROWS_64, cols), F32),
                        pltpu.VMEM((1, cols), F32), pltpu.VMEM((V_ROWS_64, cols), F32),
                        pltpu.SMEM((S // tk,), jnp.int32), pltpu.SMEM((1,), F32)],
        compiler_params=_params(2, 1),
        name="sel_win_attention",
    )(slopes, active, proj, slope_cols, ks, vs_t, kw, vw_t, sel_bias, o_cmp, gates_t)


def _alibi_slopes_log2(n_heads):
    slopes = np.exp2(-8.0 * (np.arange(n_heads, dtype=np.float32) + 1.0) / n_heads)
    return (slopes.astype(np.float32) * np.float32(LOG2E)).astype(np.float32)


def even_layer_mix(x2d, B, S, norm_g, w_in, lam_q1, lam_k1, lam_q2, lam_k2, subln, layer):
    proj = norm_proj(x2d, norm_g, w_in.astype(BF16), w_in.shape[1], [(0, 512), (1536, 2048)],
                     False).reshape(B, S, -1)
    sb_vt = proj[:, :, 1024:1536].transpose(0, 2, 1)
    df_vt = proj[:, :, 2560:3072].transpose(0, 2, 1).reshape(B, 4, 128, S)
    o_sb = sb_attention(proj, sb_vt)
    lam_rows = jnp.stack([lam_q1, lam_k1, lam_q2, lam_k2]).astype(F32)
    o_df = diff_attention(proj, _with_ones_row(df_vt, V_ROWS_128), lam_rows, subln.astype(F32),
                          _alibi_slopes_log2(4), layer)
    return jnp.concatenate([o_sb, o_df], axis=2).reshape(B * S, -1)


def odd_layer_mix(x2d, B, S, norm_g, w_in, pos_k, k_w1, k_w2, pos_v, v_w1, v_w2, tk=256):
    G, Dh = NSA_GROUPS, HEAD_DIM
    q_width = NSA_GROUPS * NSA_HPG * Dh
    kv_width = G * Dh
    n_main = q_width + 6 * kv_width
    pad = n_main + GATE_PAD - w_in.shape[1]
    w_pad = jnp.pad(w_in, ((0, 0), (0, pad))).astype(BF16)
    proj, gates = norm_proj(x2d, norm_g, w_pad, n_main, [(0, q_width)], True)
    proj = proj.reshape(B, S, n_main)

    def group_major(j):
        cols = proj[:, :, q_width + j * kv_width:q_width + (j + 1) * kv_width]
        return cols.reshape(B, S, G, Dh).transpose(0, 2, 1, 3)

    def keys_with_positions(j, extra=()):
        cols = [_key_position_columns(S, tk, Dh)] + list(extra)
        return jnp.concatenate([group_major(j)] + [jnp.broadcast_to(c, (B, G) + c.shape) for c in cols],
                               axis=-1)

    def values_transposed(j):
        return _with_ones_row(group_major(j).transpose(0, 1, 3, 2), V_ROWS_64)

    n_chunks = S // CMP_STRIDE
    chunks = jnp.stack([group_major(0), group_major(1)]).reshape(2, B * G, n_chunks, CMP_STRIDE * Dh)
    pos_flat = jnp.stack([pos_k, pos_v]).reshape(2, 1, CMP_LEN * Dh).astype(F32)
    w1 = jnp.stack([k_w1, v_w1]).astype(BF16)
    w2 = jnp.stack([k_w2, v_w2]).astype(BF16)
    cmp = compress_kv(chunks, pos_flat, w1, w2).reshape(2, B, G, n_chunks, Dh)

    n_sel = S // SEL_LEN
    cmp_start = jnp.arange(n_chunks) * CMP_STRIDE
    sel_start = jnp.arange(n_sel) * SEL_LEN
    overlap_t = ((cmp_start[None, :] < sel_start[:, None] + SEL_LEN)
                 & (sel_start[:, None] <= cmp_start[None, :] + CMP_LEN - 1)).astype(BF16)
    onehot = (jnp.arange(S)[:, None] // SEL_LEN == jnp.arange(n_sel)[None, :]).astype(BF16)
    slopes = _alibi_slopes_log2(NSA_GROUPS * NSA_HPG)
    slope_cols = _slope_pieces(slopes, Dh).reshape(G, NSA_HPG, Dh)

    tq = 256
    o_cmp, sel_bias, hits = cmp_select(proj, cmp[0], cmp[1].transpose(0, 1, 3, 2), overlap_t, slopes,
                                       tk, tq)
    active = (hits[..., 0] > 0).astype(jnp.int32)[:, :, :, None, :]
    gates_t = (gates[:, :G * NSA_HPG * N_GATES].reshape(B, S, G, NSA_HPG * N_GATES)
               .transpose(0, 2, 3, 1))
    o = sel_win_attention(proj, active, slope_cols, keys_with_positions(2, [onehot]),
                          values_transposed(3), keys_with_positions(4), values_transposed(5),
                          sel_bias, o_cmp, gates_t, slopes, tq=tq, tk=tk)
    return o.reshape(B * S, q_width)


def kernel(x, attn_norm, mlp_norm, final_norm, ev_w_in, ev_lam_q1, ev_lam_k1, ev_lam_q2, ev_lam_k2,
           ev_subln, ev_w_out, od_w_in, od_cmp_pos_k, od_cmp_k_w1, od_cmp_k_w2, od_cmp_pos_v,
           od_cmp_v_w1, od_cmp_v_w2, od_w_out, mlp_w1, mlp_w2):
    B, S, D = x.shape
    depth = attn_norm.shape[0]
    x2d = x.reshape(B * S, D)
    for layer in range(depth):
        idx = layer // 2
        if layer % 2 == 0:
            mix = even_layer_mix(x2d, B, S, attn_norm[layer], ev_w_in[idx], ev_lam_q1[idx],
                                 ev_lam_k1[idx], ev_lam_q2[idx], ev_lam_k2[idx], ev_subln[idx], layer)
            w_out = ev_w_out[idx]
        else:
            mix = odd_layer_mix(x2d, B, S, attn_norm[layer], od_w_in[idx], od_cmp_pos_k[idx],
                                od_cmp_k_w1[idx], od_cmp_k_w2[idx], od_cmp_pos_v[idx],
                                od_cmp_v_w1[idx], od_cmp_v_w2[idx])
            w_out = od_w_out[idx]
        g_final = final_norm if layer == depth - 1 else None
        x2d = post_block(mix, x2d, w_out.astype(BF16), mlp_norm[layer], mlp_w1[layer].astype(BF16),
                         mlp_w2[layer].astype(BF16), g_final)
    return x2d.reshape(B, S, D)
```

```python
import functools
import math

import jax
import jax.numpy as jnp
import numpy as np
from jax import lax
from jax.experimental import pallas as pl
from jax.experimental.pallas import tpu as pltpu

F32 = jnp.float32
BF16 = jnp.bfloat16

HEAD_DIM = 64
RMS_EPS = 1e-6
NEG_INF = -1e30
FORCE_SCORE = 1e6
NSA_GROUPS = 4
NSA_HPG = 4
CMP_LEN = 32
CMP_STRIDE = 16
SEL_LEN = 64
SEL_TOPK = 16
WINDOW = 512
N_GATES = 3
GATE_PAD = 128
SEL_DROP = 2.0 ** 24
BF16_EXACT_INT = 256
V_ROWS_64 = 80
V_ROWS_128 = 144

LOG2E = math.log2(math.e)
Q_SCALE = HEAD_DIM ** -0.5 * LOG2E
UNDERFLOW_LOG2 = -160.0
SAFE_EXP_LOG2 = 100.0

VMEM_LIMIT = 56 * 1024 * 1024


def _params(n_parallel, n_arbitrary=0):
    return pltpu.CompilerParams(dimension_semantics=("parallel",) * n_parallel + ("arbitrary",) * n_arbitrary,
                                vmem_limit_bytes=VMEM_LIMIT)


def _rms(x, g):
    ms = jnp.mean(x * x, axis=-1, keepdims=True)
    return x * lax.rsqrt(ms + RMS_EPS) * g


def _dot(a, b):
    return jnp.dot(a, b, preferred_element_type=F32)


def _dot_nt(a, b):
    return lax.dot_general(a, b, (((1,), (1,)), ((), ())), preferred_element_type=F32)


def _split_bf16(x):
    hi = x.astype(BF16)
    lo = (x - hi.astype(F32)).astype(BF16)
    return hi, lo


def _slope_pieces(slopes, width):
    def top_bits(x):
        return (x.view(np.uint32) & np.uint32(0xFFFF0000)).view(np.float32)

    s1 = top_bits(slopes)
    r1 = slopes - s1
    s2 = top_bits(r1)
    s3 = top_bits(r1 - s2)
    out = np.zeros((slopes.shape[0], width), np.float32)
    out[:, :6] = np.stack([s1, s2, s3, s1, s2, s3], axis=1)
    return jnp.asarray(out).astype(BF16)


def _key_position_columns(n, tk, width):
    j = jnp.arange(n) % tk
    a = (j // BF16_EXACT_INT) * BF16_EXACT_INT
    b = j % BF16_EXACT_INT
    cols = jnp.stack([a, a, a, b, b, b], axis=1).astype(BF16)
    return jnp.pad(cols, ((0, 0), (0, width - cols.shape[1])))


def _with_ones_row(v_t, rows):
    d = v_t.shape[-2]
    ones = jnp.ones(v_t.shape[:-2] + (1, v_t.shape[-1]), v_t.dtype)
    pad = jnp.zeros(v_t.shape[:-2] + (rows - d - 1, v_t.shape[-1]), v_t.dtype)
    return jnp.concatenate([v_t, ones, pad], axis=-2)


def _softmax_block(steps, states):
    s_all = [scores() for scores, _, _, _ in steps]
    live = {k: (states[k][0][...], states[k][1][...]) for k in sorted({k for _, _, _, k in steps})}
    for (_, kap, values, k), s_t in zip(steps, s_all):
        m_run, acc = live[k]
        m_new = jnp.maximum(m_run, jnp.max(s_t, axis=0, keepdims=True) + kap)
        p = jnp.exp2(s_t - (m_new - kap)).astype(BF16)
        live[k] = (m_new, jnp.exp2(m_run - m_new) * acc + _dot(values(), p))
    for k, (m_run, acc) in live.items():
        states[k][0][...] = m_run
        states[k][1][...] = acc


def _softmax_block_fixed_frame(steps, state):
    m_ref, acc_ref = state
    s_all = [scores() for scores, _, _ in steps]
    frame = m_ref[...]
    m_run, acc = frame, acc_ref[...]
    for (_, kap, values), s_t in zip(steps, s_all):
        m_run = jnp.maximum(m_run, jnp.max(s_t, axis=0, keepdims=True) + kap)
        acc = acc + _dot(values(), jnp.exp2(s_t - (frame - kap)).astype(BF16))
    m_ref[...] = m_run
    acc_ref[...] = acc * jnp.exp2(frame - m_run)


def _softmax_loop(n_steps, step, state, safe):
    def run(groups, block):
        done = 0
        for group in groups:
            def body(i, carry, group=group, done=done):
                block([step(done + group * i + j) for j in range(group)])
                return carry

            n_groups = (n_steps - done) // group
            lax.fori_loop(0, n_groups, body, 0)
            done = done + group * n_groups

    @pl.when(safe)
    def _():
        run((8, 4, 2, 1), lambda steps: _softmax_block_fixed_frame(steps, state))

    @pl.when(jnp.logical_not(safe))
    def _():
        run((4, 2, 1), lambda steps: _softmax_block([st + (0,) for st in steps], [state]))


def _norm_proj_kernel(x_ref, g_ref, w_ref, scale_ref, o_ref, *gate_ref, n_main, col_chunk):
    xn = _rms(x_ref[...], g_ref[...]).astype(BF16)
    for c in range(n_main // col_chunk):
        sl = slice(c * col_chunk, (c + 1) * col_chunk)
        o_ref[:, sl] = (_dot(xn, w_ref[:, sl]) * scale_ref[:, sl]).astype(o_ref.dtype)
    if gate_ref:
        logits = _dot(xn, w_ref[:, n_main:n_main + GATE_PAD])
        gate_ref[0][...] = jax.nn.sigmoid(logits)


def norm_proj(x2d, g, w_bf16, n_main, q_cols, with_gates, tm=512, col_chunk=512):
    T, D = x2d.shape
    n_w = w_bf16.shape[1]
    col = jnp.arange(n_main)
    is_q = functools.reduce(jnp.logical_or, [(col >= lo) & (col < hi) for lo, hi in q_cols])
    col_scale = jnp.where(is_q, Q_SCALE, 1.0).astype(F32).reshape(1, n_main)
    out_shape = [jax.ShapeDtypeStruct((T, n_main), BF16)]
    out_specs = [pl.BlockSpec((tm, n_main), lambda i: (i, 0))]
    if with_gates:
        out_shape.append(jax.ShapeDtypeStruct((T, GATE_PAD), F32))
        out_specs.append(pl.BlockSpec((tm, GATE_PAD), lambda i: (i, 0)))
    res = pl.pallas_call(
        functools.partial(_norm_proj_kernel, n_main=n_main, col_chunk=col_chunk),
        grid=(T // tm,),
        in_specs=[pl.BlockSpec((tm, D), lambda i: (i, 0)),
                  pl.BlockSpec((1, D), lambda i: (0, 0)),
                  pl.BlockSpec((D, n_w), lambda i: (0, 0)),
                  pl.BlockSpec((1, n_main), lambda i: (0, 0))],
        out_specs=out_specs,
        out_shape=out_shape,
        compiler_params=_params(1),
        name="norm_proj",
    )(x2d, g.reshape(1, D), w_bf16, col_scale)
    return res if with_gates else res[0]


def _head_pair_rows(q, t):
    lane = lax.broadcasted_iota(jnp.int32, (t, 2 * HEAD_DIM), 1)
    zero = jnp.zeros_like(q)
    return jnp.where(lane < HEAD_DIM, q, zero), jnp.where(lane >= HEAD_DIM, q, zero)


def _sb_kernel(q_ref, k_ref, vt_ref, o_ref, acc_ref, carry_ref, *, t):
    qi = pl.program_id(2)
    cols = 2 * t
    q_both = jnp.concatenate(_head_pair_rows(q_ref[0], t), axis=0)
    s_idx = lax.broadcasted_iota(jnp.int32, (t, 2 * t), 0)
    j_idx = lax.broadcasted_iota(jnp.int32, (t, 2 * t), 1) % t
    upper2 = jnp.where(j_idx > s_idx, 1.0, 0.0).astype(BF16)
    key = lax.broadcasted_iota(jnp.int32, (t, cols), 0)
    qry = lax.broadcasted_iota(jnp.int32, (t, cols), 1) % t
    past_diag = key < qry

    acc_ref[...] = jnp.zeros_like(acc_ref)
    carry_ref[...] = jnp.zeros_like(carry_ref)

    def chunk(kc, masked):
        start = pl.multiple_of(kc * t, t)
        z = _dot_nt(k_ref[0, pl.ds(start, t), :], q_both)
        drop = jnp.maximum(z, 0.0) + jnp.log2(1.0 + jnp.exp2(jnp.abs(z) * -1.0))
        log_beta = z - drop
        if masked:
            drop = jnp.where(past_diag, drop, 0.0)
        hi, lo = _split_bf16(drop)
        tail = _dot(upper2, jnp.concatenate([hi, lo], axis=0))
        carry = carry_ref[...]
        w = jnp.exp2(log_beta - tail - carry)
        if masked:
            w = jnp.where(past_diag, w, 0.0)
        carry = carry + tail[0:1] + drop[0:1]
        carry_ref[...] = carry
        acc_ref[...] += _dot(vt_ref[0, :, pl.ds(start, t)], w.astype(BF16))
        return jnp.min(carry)

    def cond(state):
        j, least_carry = state
        return (j < qi) & (least_carry < -UNDERFLOW_LOG2)

    def body(state):
        j, _ = state
        return j + 1, chunk(qi - 1 - j, False)

    lax.while_loop(cond, body, (0, chunk(qi, True)))
    row = lax.broadcasted_iota(jnp.int32, (2 * HEAD_DIM, t), 0)
    o_t = jnp.where(row < HEAD_DIM, acc_ref[:, :t], acc_ref[:, t:])
    o_ref[0] = o_t.T.astype(o_ref.dtype)


def sb_attention(proj, v_t, t=256):
    B, S, _ = proj.shape
    n_pairs = 4
    return pl.pallas_call(
        functools.partial(_sb_kernel, t=t),
        grid=(B, n_pairs, S // t),
        in_specs=[pl.BlockSpec((1, t, 128), lambda b, p, i: (b, i, p)),
                  pl.BlockSpec((1, S, 128), lambda b, p, i: (b, 0, n_pairs + p)),
                  pl.BlockSpec((1, 128, S), lambda b, p, i: (b, p, 0))],
        out_specs=pl.BlockSpec((1, t, 128), lambda b, p, i: (b, i, p)),
        out_shape=jax.ShapeDtypeStruct((B, S, n_pairs * 128), BF16),
        scratch_shapes=[pltpu.VMEM((128, 2 * t), F32), pltpu.VMEM((1, 2 * t), F32)],
        compiler_params=_params(3),
        name="sb_attention",
    )(proj, proj, v_t)


def _diff_kernel(slopes_ref, lam_ref, q_ref, k_ref, vt_ref, pos_ref, sl_ref, subln_ref, o_ref,
                 m_ref, acc_ref, knorm_ref, *, t, out_scale, lambda_init):
    h = pl.program_id(1)
    qi = pl.program_id(2)
    slope = slopes_ref[h]
    cols = 2 * t
    slope_cols = jnp.broadcast_to(sl_ref[0], (t, 128))
    q_parts = _head_pair_rows(q_ref[0], t)
    q_both = jnp.concatenate([jnp.concatenate([qc, slope_cols], axis=1) for qc in q_parts],
                             axis=0)
    lane = lax.broadcasted_iota(jnp.int32, (t, 2 * HEAD_DIM), 1)

    @pl.when(qi == 0)
    def _():
        def body(c, best):
            k = k_ref[0, pl.ds(pl.multiple_of(c * t, t), t), :].astype(F32)
            k2 = k * k
            return (jnp.maximum(best[0], jnp.max(jnp.sum(jnp.where(lane < HEAD_DIM, k2, 0.0), axis=1))),
                    jnp.maximum(best[1], jnp.max(jnp.sum(jnp.where(lane >= HEAD_DIM, k2, 0.0), axis=1))))

        best = lax.fori_loop(0, k_ref.shape[1] // t, body, (jnp.float32(0.0), jnp.float32(0.0)))
        knorm_ref[0] = best[0]
        knorm_ref[1] = best[1]

    ones = jnp.ones((8, 2 * HEAD_DIM), BF16)
    q_norm2 = jnp.concatenate(
        [_dot_nt(ones, (qc.astype(F32) * qc.astype(F32)).astype(BF16))[0:1] for qc in q_parts], axis=1)
    col = lax.broadcasted_iota(jnp.int32, (1, cols), 1)
    k_norm2 = jnp.where(col < t, knorm_ref[0], knorm_ref[1])
    score_bound = jnp.sqrt(q_norm2 * k_norm2 * 1.05)
    pos = pos_ref[...]
    offset = (lax.broadcasted_iota(jnp.int32, (t, cols), 0)
              - lax.broadcasted_iota(jnp.int32, (t, cols), 1) % t)

    m_ref[...] = jnp.full_like(m_ref, NEG_INF)
    acc_ref[...] = jnp.zeros_like(acc_ref)

    def step(kc, diagonal=False):
        start = pl.multiple_of(kc * t, t)

        def scores():
            s_t = _dot_nt(jnp.concatenate([k_ref[0, pl.ds(start, t), :], pos], axis=1), q_both)
            return jnp.where(offset <= 0, s_t, NEG_INF) if diagonal else s_t

        return scores, slope * ((kc - qi) * t).astype(F32), lambda: vt_ref[0, 0, :, pl.ds(start, t)]

    state = (m_ref, acc_ref)
    _softmax_block([step(qi, True) + (0,)], [state])
    excess = jnp.max(score_bound - m_ref[...]) + slope * (t - 1)
    n_back = jnp.clip(jnp.ceil((excess - UNDERFLOW_LOG2) / (slope * t)).astype(jnp.int32), 0, qi)
    _softmax_loop(n_back, lambda i: step(qi - 1 - i), state, excess - slope * t <= SAFE_EXP_LOG2)

    lam_terms = lam_ref[...]
    lam = (jnp.exp(jnp.sum(lam_terms[0:1] * lam_terms[1:2], axis=1, keepdims=True))
           - jnp.exp(jnp.sum(lam_terms[2:3] * lam_terms[3:4], axis=1, keepdims=True))
           + lambda_init)
    o_t = acc_ref[0:128, :] / acc_ref[128:129, :]
    o = (o_t[:, :t] - lam * o_t[:, t:]).T
    o_ref[0] = (_rms(o, subln_ref[...]) * out_scale).astype(o_ref.dtype)


def diff_attention(proj, vt_aug, lam_rows, subln, slopes, layer, t=512):
    B, S, _ = proj.shape
    n_heads = 4
    lambda_init = 0.8 - 0.6 * math.exp(-0.3 * layer)
    smem = pl.BlockSpec(memory_space=pltpu.SMEM)
    pos = _key_position_columns(t, t, 128)
    slope_cols = _slope_pieces(slopes, 128).reshape(n_heads, 1, 128)
    return pl.pallas_call(
        functools.partial(_diff_kernel, t=t, out_scale=1.0 - lambda_init, lambda_init=lambda_init),
        grid=(B, n_heads, S // t),
        in_specs=[smem,
                  pl.BlockSpec((4, HEAD_DIM), lambda b, h, i: (0, 0)),
                  pl.BlockSpec((1, t, 128), lambda b, h, i: (b, i, 12 + h)),
                  pl.BlockSpec((1, S, 128), lambda b, h, i: (b, 0, 16 + h)),
                  pl.BlockSpec((1, 1, V_ROWS_128, S), lambda b, h, i: (b, h, 0, 0)),
                  pl.BlockSpec((t, 128), lambda b, h, i: (0, 0)),
                  pl.BlockSpec((1, 1, 128), lambda b, h, i: (h, 0, 0)),
                  pl.BlockSpec((1, 128), lambda b, h, i: (0, 0))],
        out_specs=pl.BlockSpec((1, t, 128), lambda b, h, i: (b, i, h)),
        out_shape=jax.ShapeDtypeStruct((B, S, n_heads * 128), BF16),
        scratch_shapes=[pltpu.VMEM((1, 2 * t), F32), pltpu.VMEM((V_ROWS_128, 2 * t), F32),
                        pltpu.SMEM((2,), F32)],
        compiler_params=_params(2, 1),
        name="diff_attention",
    )(slopes, lam_rows, proj, proj, vt_aug, pos, slope_cols, subln.reshape(1, 128))


def _post_kernel(*refs, ff_chunk, final):
    mix_ref, x_ref, wo_ref, g_ref, w1_ref, w2_ref = refs[:6]
    gf_ref = refs[6] if final else None
    o_ref = refs[-1]
    x = x_ref[...] + _dot(mix_ref[...], wo_ref[...])
    hn = _rms(x, g_ref[...]).astype(BF16)
    acc = x
    for f in range(w1_ref.shape[1] // ff_chunk):
        sl = slice(f * ff_chunk, (f + 1) * ff_chunk)
        hid = jnp.maximum(_dot(hn, w1_ref[:, sl]), 0.0)
        acc = acc + _dot((hid * hid).astype(BF16), w2_ref[sl, :])
    if final:
        acc = _rms(acc, gf_ref[...])
    o_ref[...] = acc


def post_block(mix, x2d, w_out, g_mlp, w1, w2, g_final=None, tm=512, ff_chunk=1024):
    T, D = x2d.shape
    final = g_final is not None
    const = lambda i: (0, 0)
    in_specs = [pl.BlockSpec((tm, mix.shape[1]), lambda i: (i, 0)),
                pl.BlockSpec((tm, D), lambda i: (i, 0)),
                pl.BlockSpec(w_out.shape, const),
                pl.BlockSpec((1, D), const), pl.BlockSpec(w1.shape, const), pl.BlockSpec(w2.shape, const)]
    args = [mix, x2d, w_out, g_mlp.reshape(1, D), w1, w2]
    if final:
        in_specs.append(pl.BlockSpec((1, D), const))
        args.append(g_final.reshape(1, D))
    return pl.pallas_call(
        functools.partial(_post_kernel, ff_chunk=ff_chunk, final=final),
        grid=(T // tm,),
        in_specs=in_specs,
        out_specs=pl.BlockSpec((tm, D), lambda i: (i, 0)),
        out_shape=jax.ShapeDtypeStruct((T, D), F32),
        compiler_params=_params(1),
        name="post_block",
    )(*args)


def _compress_kernel(c_ref, pos_ref, w1_ref, w2_ref, o_ref):
    half = w1_ref.shape[1] // 2
    chunks = c_ref[0, 0]
    pos = jnp.broadcast_to(pos_ref[0], (8, 2 * half)).astype(BF16)
    first = _dot(chunks, w1_ref[0, :half, :])
    second = _dot(chunks, w1_ref[0, half:, :])
    n_chunks = chunks.shape[0]
    pre = first + pltpu.roll(second, n_chunks - 1, 0) + _dot(pos, w1_ref[0])[0:1]
    hid = jax.nn.gelu(pre)
    o_ref[0, 0] = _dot(hid.astype(BF16), w2_ref[0]).astype(o_ref.dtype)


def compress_kv(chunks, pos_flat, w1, w2):
    _, BG, n_chunks, width = chunks.shape
    hidden = w1.shape[-1]
    return pl.pallas_call(
        _compress_kernel,
        grid=(2, BG),
        in_specs=[pl.BlockSpec((1, 1, n_chunks, width), lambda s, i: (s, i, 0, 0)),
                  pl.BlockSpec((1, 1, 2 * width), lambda s, i: (s, 0, 0)),
                  pl.BlockSpec((1, 2 * width, hidden), lambda s, i: (s, 0, 0)),
                  pl.BlockSpec((1, hidden, HEAD_DIM), lambda s, i: (s, 0, 0))],
        out_specs=pl.BlockSpec((1, 1, n_chunks, HEAD_DIM), lambda s, i: (s, i, 0, 0)),
        out_shape=jax.ShapeDtypeStruct((2, BG, n_chunks, HEAD_DIM), BF16),
        compiler_params=_params(2),
        name="compress_kv",
    )(chunks, pos_flat, w1, w2)


def _stack_heads(q):
    return jnp.concatenate([q[:, r * HEAD_DIM:(r + 1) * HEAD_DIM] for r in range(NSA_HPG)], axis=0)


def _slope_row(slopes_ref, g, tq):
    col = lax.broadcasted_iota(jnp.int32, (1, NSA_HPG * tq), 1)
    out = jnp.zeros((1, NSA_HPG * tq), F32)
    for r in range(NSA_HPG):
        out = jnp.where(col // tq == r, slopes_ref[g * NSA_HPG + r], out)
    return out


def _cmp_select_kernel(slopes_ref, q_ref, kc_ref, vct_ref, ovt_ref, ocmp_ref, sel_ref, hits_ref,
                       *, tq, n_sel, blocks_per_chunk, hit_tile):
    g = pl.program_id(1)
    t0 = pl.program_id(2) * tq
    cols = NSA_HPG * tq
    n_cmp = kc_ref.shape[2]
    slope_row = _slope_row(slopes_ref, g, tq)
    q_rows = _stack_heads(q_ref[0])
    tpos = t0 + lax.broadcasted_iota(jnp.int32, (n_cmp, cols), 1) % tq
    cmp_end = lax.broadcasted_iota(jnp.int32, (n_cmp, cols), 0) * CMP_STRIDE + (CMP_LEN - 1)
    dc = (tpos - cmp_end).astype(F32)
    sc = jnp.where(dc >= 0, _dot_nt(kc_ref[0, 0], q_rows) - slope_row * dc, NEG_INF)
    e = jnp.exp2(sc - jnp.max(sc, axis=0, keepdims=True))
    any_valid = jnp.where(dc[0:1] >= 0, 1.0, 0.0)
    pc = e * (any_valid / jnp.sum(e, axis=0, keepdims=True))
    o_t = _dot(vct_ref[0, 0], pc.astype(BF16))
    for r in range(NSA_HPG):
        ocmp_ref[0, 0, r] = o_t[:, r * tq:(r + 1) * tq]

    pc_group = pc[:, 0:tq]
    for r in range(1, NSA_HPG):
        pc_group = pc_group + pc[:, r * tq:(r + 1) * tq]
    hi, lo = _split_bf16(pc_group)
    imp = _dot(ovt_ref[...], hi) + _dot(ovt_ref[...], lo)

    blk = lax.broadcasted_iota(jnp.int32, (n_sel, tq), 0)
    cur = (t0 + lax.broadcasted_iota(jnp.int32, (n_sel, tq), 1)) // SEL_LEN
    forced = (blk == 0) | (blk == cur) | (blk == cur - 1)
    imp = jnp.where(forced, FORCE_SCORE, imp)
    imp = jnp.where(blk <= cur, imp, -1.0)
    picked = jnp.zeros((n_sel, tq), F32)
    for _ in range(min(SEL_TOPK, n_sel)):
        best = jnp.max(imp, axis=0, keepdims=True)
        first = jnp.min(jnp.where(imp == best, blk, n_sel), axis=0, keepdims=True)
        hit = blk == first
        picked = jnp.where(hit, 1.0, picked)
        imp = jnp.where(hit, -2.0, imp)
    keep = (picked > 0.5) & (blk <= cur)
    sel_ref[0, 0] = jnp.where(keep, 0.0, -SEL_DROP).T.astype(sel_ref.dtype)
    n_chunks = n_sel // blocks_per_chunk
    member = (lax.broadcasted_iota(jnp.int32, (n_chunks, n_sel), 1) // blocks_per_chunk
              == lax.broadcasted_iota(jnp.int32, (n_chunks, n_sel), 0))
    per_query = _dot(jnp.where(member, 1.0, 0.0).astype(BF16),
                     jnp.where(keep, 1.0, 0.0).astype(BF16)).astype(BF16)
    for part in range(tq // hit_tile):
        hits_ref[0, 0, part] = _dot(per_query[:, part * hit_tile:(part + 1) * hit_tile],
                                    jnp.ones((hit_tile, 128), BF16))


def cmp_select(proj, kc, vc_t, overlap_t, slopes, tk, hit_tile, tq=256):
    B, S, _ = proj.shape
    n_cmp = kc.shape[2]
    n_sel = S // SEL_LEN
    G = NSA_GROUPS
    smem = pl.BlockSpec(memory_space=pltpu.SMEM)
    return pl.pallas_call(
        functools.partial(_cmp_select_kernel, tq=tq, n_sel=n_sel, blocks_per_chunk=tk // SEL_LEN,
                          hit_tile=hit_tile),
        grid=(B, G, S // tq),
        in_specs=[smem,
                  pl.BlockSpec((1, tq, 256), lambda b, g, i: (b, i, g)),
                  pl.BlockSpec((1, 1, n_cmp, HEAD_DIM), lambda b, g, i: (b, g, 0, 0)),
                  pl.BlockSpec((1, 1, HEAD_DIM, n_cmp), lambda b, g, i: (b, g, 0, 0)),
                  pl.BlockSpec((n_sel, n_cmp), lambda b, g, i: (0, 0))],
        out_specs=[pl.BlockSpec((1, 1, NSA_HPG, HEAD_DIM, tq), lambda b, g, i: (b, g, 0, 0, i)),
                   pl.BlockSpec((1, 1, tq, n_sel), lambda b, g, i: (b, g, i, 0)),
                   pl.BlockSpec((1, 1, tq // hit_tile, S // tk, 128), lambda b, g, i: (b, g, i, 0, 0))],
        out_shape=[jax.ShapeDtypeStruct((B, G, NSA_HPG, HEAD_DIM, S), F32),
                   jax.ShapeDtypeStruct((B, G, S, n_sel), BF16),
                   jax.ShapeDtypeStruct((B, G, S // hit_tile, S // tk, 128), F32)],
        compiler_params=_params(3),
        name="cmp_select",
    )(slopes, proj, kc, vc_t, overlap_t)


def _sel_win_kernel(slopes_ref, active_ref, q_ref, sl_ref, ks_ref, vst_ref, kw_ref, vwt_ref, sel_ref,
                    ocmp_ref, gate_ref, o_ref, ms_ref, accs_ref, mw_ref, accw_ref, todo_ref, knorm_ref,
                    *, tq, tk):
    g = pl.program_id(1)
    t0 = pl.program_id(2) * tq
    cols = NSA_HPG * tq
    slope_row = _slope_row(slopes_ref, g, tq)
    q = q_ref[0]
    q_aug = jnp.concatenate(
        [jnp.concatenate([q[:, r * HEAD_DIM:(r + 1) * HEAD_DIM],
                          jnp.broadcast_to(sl_ref[0, r:r + 1, :], (tq, HEAD_DIM))], axis=1)
         for r in range(NSA_HPG)], axis=0)
    q_sel = jnp.concatenate([q_aug, jnp.concatenate([sel_ref[0, 0]] * NSA_HPG, axis=0)], axis=1)
    offset = (lax.broadcasted_iota(jnp.int32, (tk, cols), 0)
              - lax.broadcasted_iota(jnp.int32, (tk, cols), 1) % tq)

    key_lane = lax.broadcasted_iota(jnp.int32, (tk, ks_ref.shape[3]), 1) < HEAD_DIM

    @pl.when(pl.program_id(2) == 0)
    def _():
        def body(c, best):
            k = ks_ref[0, 0, pl.ds(pl.multiple_of(c * tk, tk), tk), :].astype(F32)
            return jnp.maximum(best, jnp.max(jnp.sum(jnp.where(key_lane, k * k, 0.0), axis=1)))

        knorm_ref[0] = lax.fori_loop(0, ks_ref.shape[2] // tk, body, jnp.float32(0.0))

    ones = jnp.ones((8, HEAD_DIM), BF16)
    q32 = q.astype(F32)
    q_norm2 = jnp.concatenate(
        [_dot_nt(ones, (q32 * q32)[:, r * HEAD_DIM:(r + 1) * HEAD_DIM].astype(BF16))[0:1]
         for r in range(NSA_HPG)], axis=1)
    score_bound = jnp.sqrt(q_norm2 * knorm_ref[0] * 1.05) + slope_row * (tk - 1)

    sel_state, win_state = (ms_ref, accs_ref), (mw_ref, accw_ref)
    for m_ref, acc_ref in (sel_state, win_state):
        m_ref[...] = jnp.full_like(m_ref, NEG_INF)
        acc_ref[...] = jnp.zeros_like(acc_ref)

    def step(c, k_ref, q_rows, vt_ref, mask=None):
        inside = c >= 0
        start = pl.multiple_of(jnp.maximum(c, 0) * tk, tk)
        shift = c * tk - t0

        def scores():
            s_t = _dot_nt(k_ref[0, 0, pl.ds(start, tk), :], q_rows)
            return s_t if mask is None else jnp.where(mask(offset + shift) & inside, s_t, NEG_INF)

        kappa = jnp.where(inside, slope_row * shift.astype(F32), NEG_INF)
        return scores, kappa, lambda: vt_ref[0, 0, :, pl.ds(start, tk)]

    last = t0 // tk
    causal = lambda d: d <= 0
    recent = lambda d: d > -WINDOW
    anything = lambda d: True
    _softmax_block([step(last - 2, kw_ref, q_aug, vwt_ref, recent) + (1,),
                    step(last, ks_ref, q_sel, vst_ref, causal) + (0,),
                    step(last - 1, kw_ref, q_aug, vwt_ref, anything) + (1,),
                    step(last, kw_ref, q_aug, vwt_ref, causal) + (1,)], [sel_state, win_state])

    def note_active(c, n):
        hit = active_ref[0, 0, 0, 0, c] > 0

        @pl.when(hit)
        def _():
            todo_ref[n] = c

        return n + hit.astype(jnp.int32)

    n_todo = lax.fori_loop(0, last, note_active, 0)
    safe = jnp.max(score_bound - slope_row * tk - ms_ref[...]) <= SAFE_EXP_LOG2
    _softmax_loop(n_todo, lambda i: step(todo_ref[i], ks_ref, q_sel, vst_ref), sel_state, safe)
    o_sel = accs_ref[0:HEAD_DIM, :] / accs_ref[HEAD_DIM:HEAD_DIM + 1, :]
    o_win = accw_ref[0:HEAD_DIM, :] / accw_ref[HEAD_DIM:HEAD_DIM + 1, :]

    gates = gate_ref[0, 0]
    outs = []
    for r in range(NSA_HPG):
        cs = slice(r * tq, (r + 1) * tq)
        outs.append(gates[3 * r:3 * r + 1] * ocmp_ref[0, 0, r] + gates[3 * r + 1:3 * r + 2] * o_sel[:, cs]
                    + gates[3 * r + 2:3 * r + 3] * o_win[:, cs])
    o_ref[0] = jnp.concatenate(outs, axis=0).T.astype(o_ref.dtype)


def sel_win_attention(proj, active, slope_cols, ks, vs_t, kw, vw_t, sel_bias, o_cmp, gates_t,
                      slopes, tq=128, tk=256):
    B, S, _ = proj.shape
    assert tq == tk and WINDOW == 2 * tk
    G = NSA_GROUPS
    n_sel = S // SEL_LEN
    cols = NSA_HPG * tq
    smem = pl.BlockSpec(memory_space=pltpu.SMEM)
    ks_spec = pl.BlockSpec((1, 1, S, ks.shape[-1]), lambda b, g, i: (b, g, 0, 0))
    kw_spec = pl.BlockSpec((1, 1, S, kw.shape[-1]), lambda b, g, i: (b, g, 0, 0))
    v_spec = pl.BlockSpec((1, 1, V_ROWS_64, S), lambda b, g, i: (b, g, 0, 0))
    return pl.pallas_call(
        functools.partial(_sel_win_kernel, tq=tq, tk=tk),
        grid=(B, G, S // tq),
        in_specs=[smem,
                  pl.BlockSpec((1, 1, 1, 1, S // tk), lambda b, g, i: (b, g, i, 0, 0),
                               memory_space=pltpu.SMEM),
                  pl.BlockSpec((1, tq, 256), lambda b, g, i: (b, i, g)),
                  pl.BlockSpec((1, NSA_HPG, HEAD_DIM), lambda b, g, i: (g, 0, 0)),
                  ks_spec, v_spec, kw_spec, v_spec,
                  pl.BlockSpec((1, 1, tq, n_sel), lambda b, g, i: (b, g, i, 0)),
                  pl.BlockSpec((1, 1, NSA_HPG, HEAD_DIM, tq), lambda b, g, i: (b, g, 0, 0, i)),
                  pl.BlockSpec((1, 1, NSA_HPG * N_GATES, tq), lambda b, g, i: (b, g, 0, i))],
        out_specs=pl.BlockSpec((1, tq, 256), lambda b, g, i: (b, i, g)),
        out_shape=jax.ShapeDtypeStruct((B, S, G * 256), BF16),
        scratch_shapes=[pltpu.VMEM((1, cols), F32), pltpu.VMEM((V_ROWS_64, cols), F32),
                        pltpu.VMEM((1, cols), F32), pltpu.VMEM((V_ROWS_64, cols), F32),
                        pltpu.SMEM((S // tk,), jnp.int32), pltpu.SMEM((1,), F32)],
        compiler_params=_params(2, 1),
        name="sel_win_attention",
    )(slopes, active, proj, slope_cols, ks, vs_t, kw, vw_t, sel_bias, o_cmp, gates_t)


def _alibi_slopes_log2(n_heads):
    slopes = np.exp2(-8.0 * (np.arange(n_heads, dtype=np.float32) + 1.0) / n_heads)
    return (slopes.astype(np.float32) * np.float32(LOG2E)).astype(np.float32)


def even_layer_mix(x2d, B, S, norm_g, w_in, lam_q1, lam_k1, lam_q2, lam_k2, subln, layer):
    proj = norm_proj(x2d, norm_g, w_in.astype(BF16), w_in.shape[1], [(0, 512), (1536, 2048)],
                     False).reshape(B, S, -1)
    sb_vt = proj[:, :, 1024:1536].transpose(0, 2, 1)
    df_vt = proj[:, :, 2560:3072].transpose(0, 2, 1).reshape(B, 4, 128, S)
    o_sb = sb_attention(proj, sb_vt)
    lam_rows = jnp.stack([lam_q1, lam_k1, lam_q2, lam_k2]).astype(F32)
    o_df = diff_attention(proj, _with_ones_row(df_vt, V_ROWS_128), lam_rows, subln.astype(F32),
                          _alibi_slopes_log2(4), layer)
    return jnp.concatenate([o_sb, o_df], axis=2).reshape(B * S, -1)


def odd_layer_mix(x2d, B, S, norm_g, w_in, pos_k, k_w1, k_w2, pos_v, v_w1, v_w2, tk=256):
    G, Dh = NSA_GROUPS, HEAD_DIM
    q_width = NSA_GROUPS * NSA_HPG * Dh
    kv_width = G * Dh
    n_main = q_width + 6 * kv_width
    pad = n_main + GATE_PAD - w_in.shape[1]
    w_pad = jnp.pad(w_in, ((0, 0), (0, pad))).astype(BF16)
    proj, gates = norm_proj(x2d, norm_g, w_pad, n_main, [(0, q_width)], True)
    proj = proj.reshape(B, S, n_main)

    def group_major(j):
        cols = proj[:, :, q_width + j * kv_width:q_width + (j + 1) * kv_width]
        return cols.reshape(B, S, G, Dh).transpose(0, 2, 1, 3)

    def keys_with_positions(j, extra=()):
        cols = [_key_position_columns(S, tk, Dh)] + list(extra)
        return jnp.concatenate([group_major(j)] + [jnp.broadcast_to(c, (B, G) + c.shape) for c in cols],
                               axis=-1)

    def values_transposed(j):
        return _with_ones_row(group_major(j).transpose(0, 1, 3, 2), V_ROWS_64)

    n_chunks = S // CMP_STRIDE
    chunks = jnp.stack([group_major(0), group_major(1)]).reshape(2, B * G, n_chunks, CMP_STRIDE * Dh)
    pos_flat = jnp.stack([pos_k, pos_v]).reshape(2, 1, CMP_LEN * Dh).astype(F32)
    w1 = jnp.stack([k_w1, v_w1]).astype(BF16)
    w2 = jnp.stack([k_w2, v_w2]).astype(BF16)
    cmp = compress_kv(chunks, pos_flat, w1, w2).reshape(2, B, G, n_chunks, Dh)

    n_sel = S // SEL_LEN
    cmp_start = jnp.arange(n_chunks) * CMP_STRIDE
    sel_start = jnp.arange(n_sel) * SEL_LEN
    overlap_t = ((cmp_start[None, :] < sel_start[:, None] + SEL_LEN)
                 & (sel_start[:, None] <= cmp_start[None, :] + CMP_LEN - 1)).astype(BF16)
    onehot = (jnp.arange(S)[:, None] // SEL_LEN == jnp.arange(n_sel)[None, :]).astype(BF16)
    slopes = _alibi_slopes_log2(NSA_GROUPS * NSA_HPG)
    slope_cols = _slope_pieces(slopes, Dh).reshape(G, NSA_HPG, Dh)

    tq = 256
    o_cmp, sel_bias, hits = cmp_select(proj, cmp[0], cmp[1].transpose(0, 1, 3, 2), overlap_t, slopes,
                                       tk, tq)
    active = (hits[..., 0] > 0).astype(jnp.int32)[:, :, :, None, :]
    gates_t = (gates[:, :G * NSA_HPG * N_GATES].reshape(B, S, G, NSA_HPG * N_GATES)
               .transpose(0, 2, 3, 1))
    o = sel_win_attention(proj, active, slope_cols, keys_with_positions(2, [onehot]),
                          values_transposed(3), keys_with_positions(4), values_transposed(5),
                          sel_bias, o_cmp, gates_t, slopes, tq=tq, tk=tk)
    return o.reshape(B * S, q_width)


def kernel(x, attn_norm, mlp_norm, final_norm, ev_w_in, ev_lam_q1, ev_lam_k1, ev_lam_q2, ev_lam_k2,
           ev_subln, ev_w_out, od_w_in, od_cmp_pos_k, od_cmp_k_w1, od_cmp_k_w2, od_cmp_pos_v,
           od_cmp_v_w1, od_cmp_v_w2, od_w_out, mlp_w1, mlp_w2):
    B, S, D = x.shape
    depth = attn_norm.shape[0]
    x2d = x.reshape(B * S, D)
    for layer in range(depth):
        idx = layer // 2
        if layer % 2 == 0:
            mix = even_layer_mix(x2d, B, S, attn_norm[layer], ev_w_in[idx], ev_lam_q1[idx],
                                 ev_lam_k1[idx], ev_lam_q2[idx], ev_lam_k2[idx], ev_subln[idx], layer)
            w_out = ev_w_out[idx]
        else:
            mix = odd_layer_mix(x2d, B, S, attn_norm[layer], od_w_in[idx], od_cmp_pos_k[idx],
                                od_cmp_k_w1[idx], od_cmp_k_w2[idx], od_cmp_pos_v[idx],
                                od_cmp_v_w1[idx], od_cmp_v_w2[idx])
            w_out = od_w_out[idx]
        g_final = final_norm if layer == depth - 1 else None
        x2d = post_block(mix, x2d, w_out.astype(BF16), mlp_norm[layer], mlp_w1[layer].astype(BF16),
                         mlp_w2[layer].astype(BF16), g_final)
    return x2d.reshape(B, S, D)
```

```python
import functools
import math

import jax
import jax.numpy as jnp
import numpy as np
from jax import lax
from jax.experimental import pallas as pl
from jax.experimental.pallas import tpu as pltpu

F32 = jnp.float32
BF16 = jnp.bfloat16

HEAD_DIM = 64
RMS_EPS = 1e-6
NEG_INF = -1e30
FORCE_SCORE = 1e6
NSA_GROUPS = 4
NSA_HPG = 4
CMP_LEN = 32
CMP_STRIDE = 16
SEL_LEN = 64
SEL_TOPK = 16
WINDOW = 512
N_GATES = 3
GATE_PAD = 128
SEL_DROP = 2.0 ** 24
BF16_EXACT_INT = 256
V_ROWS_64 = 80
V_ROWS_128 = 144

LOG2E = math.log2(math.e)
Q_SCALE = HEAD_DIM ** -0.5 * LOG2E
UNDERFLOW_LOG2 = -160.0
SAFE_EXP_LOG2 = 100.0

VMEM_LIMIT = 56 * 1024 * 1024


def _params(n_parallel, n_arbitrary=0):
    return pltpu.CompilerParams(dimension_semantics=("parallel",) * n_parallel + ("arbitrary",) * n_arbitrary,
                                vmem_limit_bytes=VMEM_LIMIT)


def _rms(x, g):
    ms = jnp.mean(x * x, axis=-1, keepdims=True)
    return x * lax.rsqrt(ms + RMS_EPS) * g


def _dot(a, b):
    return jnp.dot(a, b, preferred_element_type=F32)


def _dot_nt(a, b):
    return lax.dot_general(a, b, (((1,), (1,)), ((), ())), preferred_element_type=F32)


def _split_bf16(x):
    hi = x.astype(BF16)
    lo = (x - hi.astype(F32)).astype(BF16)
    return hi, lo


def _slope_pieces(slopes, width):
    def top_bits(x):
        return (x.view(np.uint32) & np.uint32(0xFFFF0000)).view(np.float32)

    s1 = top_bits(slopes)
    r1 = slopes - s1
    s2 = top_bits(r1)
    s3 = top_bits(r1 - s2)
    out = np.zeros((slopes.shape[0], width), np.float32)
    out[:, :6] = np.stack([s1, s2, s3, s1, s2, s3], axis=1)
    return jnp.asarray(out).astype(BF16)


def _key_position_columns(n, tk, width):
    j = jnp.arange(n) % tk
    a = (j // BF16_EXACT_INT) * BF16_EXACT_INT
    b = j % BF16_EXACT_INT
    cols = jnp.stack([a, a, a, b, b, b], axis=1).astype(BF16)
    return jnp.pad(cols, ((0, 0), (0, width - cols.shape[1])))


def _with_ones_row(v_t, rows):
    d = v_t.shape[-2]
    ones = jnp.ones(v_t.shape[:-2] + (1, v_t.shape[-1]), v_t.dtype)
    pad = jnp.zeros(v_t.shape[:-2] + (rows - d - 1, v_t.shape[-1]), v_t.dtype)
    return jnp.concatenate([v_t, ones, pad], axis=-2)


def _softmax_block(steps, states):
    s_all = [scores() for scores, _, _, _ in steps]
    live = {k: (states[k][0][...], states[k][1][...]) for k in sorted({k for _, _, _, k in steps})}
    for (_, kap, values, k), s_t in zip(steps, s_all):
        m_run, acc = live[k]
        m_new = jnp.maximum(m_run, jnp.max(s_t, axis=0, keepdims=True) + kap)
        p = jnp.exp2(s_t - (m_new - kap)).astype(BF16)
        live[k] = (m_new, jnp.exp2(m_run - m_new) * acc + _dot(values(), p))
    for k, (m_run, acc) in live.items():
        states[k][0][...] = m_run
        states[k][1][...] = acc


def _fixed_frame_block(steps, acc_refs, frames):
    s_all = [scores() for scores, _, _, _ in steps]
    live = {k: acc_refs[k][...] for k in sorted({k for _, _, _, k in steps})}
    for (_, kap, values, k), s_t in zip(steps, s_all):
        live[k] = live[k] + _dot(values(), jnp.exp2(s_t - (frames[k] - kap)).astype(BF16))
    for k, acc in live.items():
        acc_refs[k][...] = acc


def _chunk_loop(n_steps, step, block, groups):
    done = 0
    for group in groups:
        def body(i, carry, group=group, done=done):
            block([step(done + group * i + j) for j in range(group)])
            return carry

        n_groups = (n_steps - done) // group
        lax.fori_loop(0, n_groups, body, 0)
        done = done + group * n_groups


def _ones_row_sums(x):
    return _dot_nt(jnp.ones((8, x.shape[1]), BF16), x.astype(BF16))[0:1]


def _norm_proj_kernel(x_ref, g_ref, w_ref, scale_ref, o_ref, *gate_ref, n_main, col_chunk):
    xn = _rms(x_ref[...], g_ref[...]).astype(BF16)
    for c in range(n_main // col_chunk):
        sl = slice(c * col_chunk, (c + 1) * col_chunk)
        o_ref[:, sl] = (_dot(xn, w_ref[:, sl]) * scale_ref[:, sl]).astype(o_ref.dtype)
    if gate_ref:
        logits = _dot(xn, w_ref[:, n_main:n_main + GATE_PAD])
        gate_ref[0][...] = jax.nn.sigmoid(logits)


def norm_proj(x2d, g, w_bf16, n_main, q_cols, with_gates, tm=512, col_chunk=512):
    T, D = x2d.shape
    n_w = w_bf16.shape[1]
    col = jnp.arange(n_main)
    is_q = functools.reduce(jnp.logical_or, [(col >= lo) & (col < hi) for lo, hi in q_cols])
    col_scale = jnp.where(is_q, Q_SCALE, 1.0).astype(F32).reshape(1, n_main)
    out_shape = [jax.ShapeDtypeStruct((T, n_main), BF16)]
    out_specs = [pl.BlockSpec((tm, n_main), lambda i: (i, 0))]
    if with_gates:
        out_shape.append(jax.ShapeDtypeStruct((T, GATE_PAD), F32))
        out_specs.append(pl.BlockSpec((tm, GATE_PAD), lambda i: (i, 0)))
    res = pl.pallas_call(
        functools.partial(_norm_proj_kernel, n_main=n_main, col_chunk=col_chunk),
        grid=(T // tm,),
        in_specs=[pl.BlockSpec((tm, D), lambda i: (i, 0)),
                  pl.BlockSpec((1, D), lambda i: (0, 0)),
                  pl.BlockSpec((D, n_w), lambda i: (0, 0)),
                  pl.BlockSpec((1, n_main), lambda i: (0, 0))],
        out_specs=out_specs,
        out_shape=out_shape,
        compiler_params=_params(1),
        name="norm_proj",
    )(x2d, g.reshape(1, D), w_bf16, col_scale)
    return res if with_gates else res[0]


def _head_pair_rows(q, t):
    lane = lax.broadcasted_iota(jnp.int32, (t, 2 * HEAD_DIM), 1)
    zero = jnp.zeros_like(q)
    return jnp.where(lane < HEAD_DIM, q, zero), jnp.where(lane >= HEAD_DIM, q, zero)


def _sb_kernel(q_ref, k_ref, vt_ref, o_ref, acc_ref, carry_ref, *, t):
    qi = pl.program_id(2)
    cols = 2 * t
    q_both = jnp.concatenate(_head_pair_rows(q_ref[0], t), axis=0)
    s_idx = lax.broadcasted_iota(jnp.int32, (t, 2 * t), 0)
    j_idx = lax.broadcasted_iota(jnp.int32, (t, 2 * t), 1) % t
    upper2 = jnp.where(j_idx > s_idx, 1.0, 0.0).astype(BF16)
    key = lax.broadcasted_iota(jnp.int32, (t, cols), 0)
    qry = lax.broadcasted_iota(jnp.int32, (t, cols), 1) % t
    past_diag = key < qry

    acc_ref[...] = jnp.zeros_like(acc_ref)
    carry_ref[...] = jnp.zeros_like(carry_ref)

    def chunk(kc, masked):
        start = pl.multiple_of(kc * t, t)
        z = _dot_nt(k_ref[0, pl.ds(start, t), :], q_both)
        drop = jnp.maximum(z, 0.0) + jnp.log2(1.0 + jnp.exp2(jnp.abs(z) * -1.0))
        log_beta = z - drop
        if masked:
            drop = jnp.where(past_diag, drop, 0.0)
        hi, lo = _split_bf16(drop)
        tail = _dot(upper2, jnp.concatenate([hi, lo], axis=0))
        carry = carry_ref[...]
        w = jnp.exp2(log_beta - tail - carry)
        if masked:
            w = jnp.where(past_diag, w, 0.0)
        carry = carry + tail[0:1] + drop[0:1]
        carry_ref[...] = carry
        acc_ref[...] += _dot(vt_ref[0, :, pl.ds(start, t)], w.astype(BF16))
        return jnp.min(carry)

    def cond(state):
        j, least_carry = state
        return (j < qi) & (least_carry < -UNDERFLOW_LOG2)

    def body(state):
        j, _ = state
        return j + 1, chunk(qi - 1 - j, False)

    lax.while_loop(cond, body, (0, chunk(qi, True)))
    row = lax.broadcasted_iota(jnp.int32, (2 * HEAD_DIM, t), 0)
    o_t = jnp.where(row < HEAD_DIM, acc_ref[:, :t], acc_ref[:, t:])
    o_ref[0] = o_t.T.astype(o_ref.dtype)


def sb_attention(proj, v_t, t=256):
    B, S, _ = proj.shape
    n_pairs = 4
    return pl.pallas_call(
        functools.partial(_sb_kernel, t=t),
        grid=(B, n_pairs, S // t),
        in_specs=[pl.BlockSpec((1, t, 128), lambda b, p, i: (b, i, p)),
                  pl.BlockSpec((1, S, 128), lambda b, p, i: (b, 0, n_pairs + p)),
                  pl.BlockSpec((1, 128, S), lambda b, p, i: (b, p, 0))],
        out_specs=pl.BlockSpec((1, t, 128), lambda b, p, i: (b, i, p)),
        out_shape=jax.ShapeDtypeStruct((B, S, n_pairs * 128), BF16),
        scratch_shapes=[pltpu.VMEM((128, 2 * t), F32), pltpu.VMEM((1, 2 * t), F32)],
        compiler_params=_params(3),
        name="sb_attention",
    )(proj, proj, v_t)


def _diff_kernel(slopes_ref, lam_ref, q_ref, k_ref, vt_ref, pos_ref, sl_ref, subln_ref, o_ref,
                 m_ref, acc_ref, knorm_ref, *, t, out_scale, lambda_init):
    h = pl.program_id(1)
    qi = pl.program_id(2)
    slope = slopes_ref[h]
    cols = 2 * t
    slope_cols = jnp.broadcast_to(sl_ref[0], (t, 128))
    q_parts = _head_pair_rows(q_ref[0], t)
    q_both = jnp.concatenate([jnp.concatenate([qc, slope_cols], axis=1) for qc in q_parts],
                             axis=0)
    lane = lax.broadcasted_iota(jnp.int32, (t, 2 * HEAD_DIM), 1)

    @pl.when(qi == 0)
    def _():
        def body(c, best):
            k = k_ref[0, pl.ds(pl.multiple_of(c * t, t), t), :].astype(F32)
            k2 = k * k
            return (jnp.maximum(best[0], jnp.max(jnp.sum(jnp.where(lane < HEAD_DIM, k2, 0.0), axis=1))),
                    jnp.maximum(best[1], jnp.max(jnp.sum(jnp.where(lane >= HEAD_DIM, k2, 0.0), axis=1))))

        best = lax.fori_loop(0, k_ref.shape[1] // t, body, (jnp.float32(0.0), jnp.float32(0.0)))
        knorm_ref[0] = best[0]
        knorm_ref[1] = best[1]

    q32 = [qc.astype(F32) for qc in q_parts]
    col = lax.broadcasted_iota(jnp.int32, (1, cols), 1)
    k_norm2 = jnp.where(col < t, knorm_ref[0], knorm_ref[1])
    qk_bound = jnp.sqrt(jnp.concatenate([_ones_row_sums(x * x) for x in q32], axis=1) * k_norm2 * 1.05)
    k_self = k_ref[0, pl.ds(pl.multiple_of(qi * t, t), t), :].astype(F32)
    self_score = jnp.concatenate([_ones_row_sums(x * k_self) for x in q32], axis=1)
    safe = jnp.max(qk_bound - self_score) <= SAFE_EXP_LOG2

    pos = pos_ref[...]
    offset = (lax.broadcasted_iota(jnp.int32, (t, cols), 0)
              - lax.broadcasted_iota(jnp.int32, (t, cols), 1) % t)

    def step(kc, diagonal=False):
        start = pl.multiple_of(kc * t, t)

        def scores():
            s_t = _dot_nt(jnp.concatenate([k_ref[0, pl.ds(start, t), :], pos], axis=1), q_both)
            return jnp.where(offset <= 0, s_t, NEG_INF) if diagonal else s_t

        return scores, slope * ((kc - qi) * t).astype(F32), lambda: vt_ref[0, 0, :, pl.ds(start, t)], 0

    acc_ref[...] = jnp.zeros_like(acc_ref)

    @pl.when(safe)
    def _():
        frame = qk_bound + slope * (col % t).astype(F32)
        block = lambda steps: _fixed_frame_block(steps, [acc_ref], [frame])
        block([step(qi, True)])
        n_back = jnp.ceil((slope * (t - 1) - UNDERFLOW_LOG2) / (slope * t)).astype(jnp.int32)
        _chunk_loop(jnp.minimum(n_back, qi), lambda i: step(qi - 1 - i), block, (8, 4, 2, 1))

    @pl.when(jnp.logical_not(safe))
    def _():
        m_ref[...] = jnp.full_like(m_ref, NEG_INF)
        block = lambda steps: _softmax_block(steps, [(m_ref, acc_ref)])
        block([step(qi, True)])
        excess = jnp.max(qk_bound - m_ref[...]) + slope * (t - 1)
        n_back = jnp.clip(jnp.ceil((excess - UNDERFLOW_LOG2) / (slope * t)).astype(jnp.int32), 0, qi)
        _chunk_loop(n_back, lambda i: step(qi - 1 - i), block, (4, 2, 1))

    lam_terms = lam_ref[...]
    lam = (jnp.exp(jnp.sum(lam_terms[0:1] * lam_terms[1:2], axis=1, keepdims=True))
           - jnp.exp(jnp.sum(lam_terms[2:3] * lam_terms[3:4], axis=1, keepdims=True))
           + lambda_init)
    o_t = acc_ref[0:128, :] / acc_ref[128:129, :]
    o = (o_t[:, :t] - lam * o_t[:, t:]).T
    o_ref[0] = (_rms(o, subln_ref[...]) * out_scale).astype(o_ref.dtype)


def diff_attention(proj, vt_aug, lam_rows, subln, slopes, layer, t=512):
    B, S, _ = proj.shape
    n_heads = 4
    lambda_init = 0.8 - 0.6 * math.exp(-0.3 * layer)
    smem = pl.BlockSpec(memory_space=pltpu.SMEM)
    pos = _key_position_columns(t, t, 128)
    slope_cols = _slope_pieces(slopes, 128).reshape(n_heads, 1, 128)
    return pl.pallas_call(
        functools.partial(_diff_kernel, t=t, out_scale=1.0 - lambda_init, lambda_init=lambda_init),
        grid=(B, n_heads, S // t),
        in_specs=[smem,
                  pl.BlockSpec((4, HEAD_DIM), lambda b, h, i: (0, 0)),
                  pl.BlockSpec((1, t, 128), lambda b, h, i: (b, i, 12 + h)),
                  pl.BlockSpec((1, S, 128), lambda b, h, i: (b, 0, 16 + h)),
                  pl.BlockSpec((1, 1, V_ROWS_128, S), lambda b, h, i: (b, h, 0, 0)),
                  pl.BlockSpec((t, 128), lambda b, h, i: (0, 0)),
                  pl.BlockSpec((1, 1, 128), lambda b, h, i: (h, 0, 0)),
                  pl.BlockSpec((1, 128), lambda b, h, i: (0, 0))],
        out_specs=pl.BlockSpec((1, t, 128), lambda b, h, i: (b, i, h)),
        out_shape=jax.ShapeDtypeStruct((B, S, n_heads * 128), BF16),
        scratch_shapes=[pltpu.VMEM((1, 2 * t), F32), pltpu.VMEM((V_ROWS_128, 2 * t), F32),
                        pltpu.SMEM((2,), F32)],
        compiler_params=_params(2, 1),
        name="diff_attention",
    )(slopes, lam_rows, proj, proj, vt_aug, pos, slope_cols, subln.reshape(1, 128))


def _post_kernel(*refs, ff_chunk, final):
    mix_ref, x_ref, wo_ref, g_ref, w1_ref, w2_ref = refs[:6]
    gf_ref = refs[6] if final else None
    o_ref = refs[-1]
    x = x_ref[...] + _dot(mix_ref[...], wo_ref[...])
    hn = _rms(x, g_ref[...]).astype(BF16)
    acc = x
    for f in range(w1_ref.shape[1] // ff_chunk):
        sl = slice(f * ff_chunk, (f + 1) * ff_chunk)
        hid = jnp.maximum(_dot(hn, w1_ref[:, sl]), 0.0)
        acc = acc + _dot((hid * hid).astype(BF16), w2_ref[sl, :])
    if final:
        acc = _rms(acc, gf_ref[...])
    o_ref[...] = acc


def post_block(mix, x2d, w_out, g_mlp, w1, w2, g_final=None, tm=512, ff_chunk=1024):
    T, D = x2d.shape
    final = g_final is not None
    const = lambda i: (0, 0)
    in_specs = [pl.BlockSpec((tm, mix.shape[1]), lambda i: (i, 0)),
                pl.BlockSpec((tm, D), lambda i: (i, 0)),
                pl.BlockSpec(w_out.shape, const),
                pl.BlockSpec((1, D), const), pl.BlockSpec(w1.shape, const), pl.BlockSpec(w2.shape, const)]
    args = [mix, x2d, w_out, g_mlp.reshape(1, D), w1, w2]
    if final:
        in_specs.append(pl.BlockSpec((1, D), const))
        args.append(g_final.reshape(1, D))
    return pl.pallas_call(
        functools.partial(_post_kernel, ff_chunk=ff_chunk, final=final),
        grid=(T // tm,),
        in_specs=in_specs,
        out_specs=pl.BlockSpec((tm, D), lambda i: (i, 0)),
        out_shape=jax.ShapeDtypeStruct((T, D), F32),
        compiler_params=_params(1),
        name="post_block",
    )(*args)


def _compress_kernel(c_ref, pos_ref, w1_ref, w2_ref, o_ref):
    half = w1_ref.shape[1] // 2
    chunks = c_ref[0, 0]
    pos = jnp.broadcast_to(pos_ref[0], (8, 2 * half)).astype(BF16)
    first = _dot(chunks, w1_ref[0, :half, :])
    second = _dot(chunks, w1_ref[0, half:, :])
    n_chunks = chunks.shape[0]
    pre = first + pltpu.roll(second, n_chunks - 1, 0) + _dot(pos, w1_ref[0])[0:1]
    hid = jax.nn.gelu(pre)
    o_ref[0, 0] = _dot(hid.astype(BF16), w2_ref[0]).astype(o_ref.dtype)


def compress_kv(chunks, pos_flat, w1, w2):
    _, BG, n_chunks, width = chunks.shape
    hidden = w1.shape[-1]
    return pl.pallas_call(
        _compress_kernel,
        grid=(2, BG),
        in_specs=[pl.BlockSpec((1, 1, n_chunks, width), lambda s, i: (s, i, 0, 0)),
                  pl.BlockSpec((1, 1, 2 * width), lambda s, i: (s, 0, 0)),
                  pl.BlockSpec((1, 2 * width, hidden), lambda s, i: (s, 0, 0)),
                  pl.BlockSpec((1, hidden, HEAD_DIM), lambda s, i: (s, 0, 0))],
        out_specs=pl.BlockSpec((1, 1, n_chunks, HEAD_DIM), lambda s, i: (s, i, 0, 0)),
        out_shape=jax.ShapeDtypeStruct((2, BG, n_chunks, HEAD_DIM), BF16),
        compiler_params=_params(2),
        name="compress_kv",
    )(chunks, pos_flat, w1, w2)


def _stack_heads(q):
    return jnp.concatenate([q[:, r * HEAD_DIM:(r + 1) * HEAD_DIM] for r in range(NSA_HPG)], axis=0)


def _slope_row(slopes_ref, g, tq):
    col = lax.broadcasted_iota(jnp.int32, (1, NSA_HPG * tq), 1)
    out = jnp.zeros((1, NSA_HPG * tq), F32)
    for r in range(NSA_HPG):
        out = jnp.where(col // tq == r, slopes_ref[g * NSA_HPG + r], out)
    return out


def _cmp_select_kernel(slopes_ref, q_ref, kc_ref, vct_ref, ovt_ref, ocmp_ref, sel_ref, hits_ref,
                       *, tq, n_sel, blocks_per_chunk, hit_tile):
    g = pl.program_id(1)
    t0 = pl.program_id(2) * tq
    cols = NSA_HPG * tq
    n_cmp = kc_ref.shape[2]
    slope_row = _slope_row(slopes_ref, g, tq)
    q_rows = _stack_heads(q_ref[0])
    tpos = t0 + lax.broadcasted_iota(jnp.int32, (n_cmp, cols), 1) % tq
    cmp_end = lax.broadcasted_iota(jnp.int32, (n_cmp, cols), 0) * CMP_STRIDE + (CMP_LEN - 1)
    dc = (tpos - cmp_end).astype(F32)
    sc = jnp.where(dc >= 0, _dot_nt(kc_ref[0, 0], q_rows) - slope_row * dc, NEG_INF)
    e = jnp.exp2(sc - jnp.max(sc, axis=0, keepdims=True))
    any_valid = jnp.where(dc[0:1] >= 0, 1.0, 0.0)
    pc = e * (any_valid / jnp.sum(e, axis=0, keepdims=True))
    o_t = _dot(vct_ref[0, 0], pc.astype(BF16))
    for r in range(NSA_HPG):
        ocmp_ref[0, 0, r] = o_t[:, r * tq:(r + 1) * tq]

    pc_group = pc[:, 0:tq]
    for r in range(1, NSA_HPG):
        pc_group = pc_group + pc[:, r * tq:(r + 1) * tq]
    hi, lo = _split_bf16(pc_group)
    imp = _dot(ovt_ref[...], hi) + _dot(ovt_ref[...], lo)

    blk = lax.broadcasted_iota(jnp.int32, (n_sel, tq), 0)
    cur = (t0 + lax.broadcasted_iota(jnp.int32, (n_sel, tq), 1)) // SEL_LEN
    forced = (blk == 0) | (blk == cur) | (blk == cur - 1)
    imp = jnp.where(forced, FORCE_SCORE, imp)
    imp = jnp.where(blk <= cur, imp, -1.0)
    picked = jnp.zeros((n_sel, tq), F32)
    for _ in range(min(SEL_TOPK, n_sel)):
        best = jnp.max(imp, axis=0, keepdims=True)
        first = jnp.min(jnp.where(imp == best, blk, n_sel), axis=0, keepdims=True)
        hit = blk == first
        picked = jnp.where(hit, 1.0, picked)
        imp = jnp.where(hit, -2.0, imp)
    keep = (picked > 0.5) & (blk <= cur)
    sel_ref[0, 0] = jnp.where(keep, 0.0, -SEL_DROP).T.astype(sel_ref.dtype)
    n_chunks = n_sel // blocks_per_chunk
    member = (lax.broadcasted_iota(jnp.int32, (n_chunks, n_sel), 1) // blocks_per_chunk
              == lax.broadcasted_iota(jnp.int32, (n_chunks, n_sel), 0))
    per_query = _dot(jnp.where(member, 1.0, 0.0).astype(BF16),
                     jnp.where(keep, 1.0, 0.0).astype(BF16)).astype(BF16)
    for part in range(tq // hit_tile):
        hits_ref[0, 0, part] = _dot(per_query[:, part * hit_tile:(part + 1) * hit_tile],
                                    jnp.ones((hit_tile, 128), BF16))


def cmp_select(proj, kc, vc_t, overlap_t, slopes, tk, hit_tile, tq=256):
    B, S, _ = proj.shape
    n_cmp = kc.shape[2]
    n_sel = S // SEL_LEN
    G = NSA_GROUPS
    smem = pl.BlockSpec(memory_space=pltpu.SMEM)
    return pl.pallas_call(
        functools.partial(_cmp_select_kernel, tq=tq, n_sel=n_sel, blocks_per_chunk=tk // SEL_LEN,
                          hit_tile=hit_tile),
        grid=(B, G, S // tq),
        in_specs=[smem,
                  pl.BlockSpec((1, tq, 256), lambda b, g, i: (b, i, g)),
                  pl.BlockSpec((1, 1, n_cmp, HEAD_DIM), lambda b, g, i: (b, g, 0, 0)),
                  pl.BlockSpec((1, 1, HEAD_DIM, n_cmp), lambda b, g, i: (b, g, 0, 0)),
                  pl.BlockSpec((n_sel, n_cmp), lambda b, g, i: (0, 0))],
        out_specs=[pl.BlockSpec((1, 1, NSA_HPG, HEAD_DIM, tq), lambda b, g, i: (b, g, 0, 0, i)),
                   pl.BlockSpec((1, 1, tq, n_sel), lambda b, g, i: (b, g, i, 0)),
                   pl.BlockSpec((1, 1, tq // hit_tile, S // tk, 128), lambda b, g, i: (b, g, i, 0, 0))],
        out_shape=[jax.ShapeDtypeStruct((B, G, NSA_HPG, HEAD_DIM, S), F32),
                   jax.ShapeDtypeStruct((B, G, S, n_sel), BF16),
                   jax.ShapeDtypeStruct((B, G, S // hit_tile, S // tk, 128), F32)],
        compiler_params=_params(3),
        name="cmp_select",
    )(slopes, proj, kc, vc_t, overlap_t)


def _sel_win_kernel(slopes_ref, active_ref, q_ref, sl_ref, ks_ref, vst_ref, kw_ref, vwt_ref, sel_ref,
                    ocmp_ref, gate_ref, o_ref, ms_ref, accs_ref, mw_ref, accw_ref, todo_ref, knorm_ref,
                    *, tq, tk):
    g = pl.program_id(1)
    t0 = pl.program_id(2) * tq
    cols = NSA_HPG * tq
    slope_row = _slope_row(slopes_ref, g, tq)
    q = q_ref[0]
    q_aug = jnp.concatenate(
        [jnp.concatenate([q[:, r * HEAD_DIM:(r + 1) * HEAD_DIM],
                          jnp.broadcast_to(sl_ref[0, r:r + 1, :], (tq, HEAD_DIM))], axis=1)
         for r in range(NSA_HPG)], axis=0)
    q_sel = jnp.concatenate([q_aug, jnp.concatenate([sel_ref[0, 0]] * NSA_HPG, axis=0)], axis=1)
    offset = (lax.broadcasted_iota(jnp.int32, (tk, cols), 0)
              - lax.broadcasted_iota(jnp.int32, (tk, cols), 1) % tq)

    @pl.when(pl.program_id(2) == 0)
    def _():
        def body(c, best):
            rows = pl.ds(pl.multiple_of(c * tk, tk), tk)
            ks = ks_ref[0, 0, rows, :][:, :HEAD_DIM].astype(F32)
            kw = kw_ref[0, 0, rows, :][:, :HEAD_DIM].astype(F32)
            return (jnp.maximum(best[0], jnp.max(jnp.sum(ks * ks, axis=1))),
                    jnp.maximum(best[1], jnp.max(jnp.sum(kw * kw, axis=1))))

        best = lax.fori_loop(0, ks_ref.shape[2] // tk, body, (jnp.float32(0.0), jnp.float32(0.0)))
        knorm_ref[0] = best[0]
        knorm_ref[1] = best[1]

    heads32 = [q.astype(F32)[:, r * HEAD_DIM:(r + 1) * HEAD_DIM] for r in range(NSA_HPG)]
    q_norm2 = jnp.concatenate([_ones_row_sums(x * x) for x in heads32], axis=1)
    own_rows = pl.ds(pl.multiple_of(t0, tq), tq)

    def bound_and_self(k_ref, k_norm2):
        k_self = k_ref[0, 0, own_rows, :][:, :HEAD_DIM].astype(F32)
        own = jnp.concatenate([_ones_row_sums(x * k_self) for x in heads32], axis=1)
        return jnp.sqrt(q_norm2 * k_norm2 * 1.05), own

    sel_bound, sel_self = bound_and_self(ks_ref, knorm_ref[0])
    win_bound, win_self = bound_and_self(kw_ref, knorm_ref[1])
    safe = jnp.maximum(jnp.max(sel_bound - sel_self), jnp.max(win_bound - win_self)) <= SAFE_EXP_LOG2

    def step(c, k_ref, q_rows, vt_ref, mask=None):
        inside = c >= 0
        start = pl.multiple_of(jnp.maximum(c, 0) * tk, tk)
        shift = c * tk - t0

        def scores():
            s_t = _dot_nt(k_ref[0, 0, pl.ds(start, tk), :], q_rows)
            return s_t if mask is None else jnp.where(mask(offset + shift) & inside, s_t, NEG_INF)

        kappa = jnp.where(inside, slope_row * shift.astype(F32), NEG_INF)
        return scores, kappa, lambda: vt_ref[0, 0, :, pl.ds(start, tk)]

    last = t0 // tk
    causal = lambda d: d <= 0
    recent = lambda d: d > -WINDOW
    anything = lambda d: True
    first_steps = [step(last - 2, kw_ref, q_aug, vwt_ref, recent) + (1,),
                   step(last, ks_ref, q_sel, vst_ref, causal) + (0,),
                   step(last - 1, kw_ref, q_aug, vwt_ref, anything) + (1,),
                   step(last, kw_ref, q_aug, vwt_ref, causal) + (1,)]

    def note_active(c, n):
        hit = active_ref[0, 0, 0, 0, c] > 0

        @pl.when(hit)
        def _():
            todo_ref[n] = c

        return n + hit.astype(jnp.int32)

    n_todo = lax.fori_loop(0, last, note_active, 0)
    sel_step = lambda i: step(todo_ref[i], ks_ref, q_sel, vst_ref) + (0,)

    accs_ref[...] = jnp.zeros_like(accs_ref)
    accw_ref[...] = jnp.zeros_like(accw_ref)

    @pl.when(safe)
    def _():
        in_tile = slope_row * (lax.broadcasted_iota(jnp.int32, (1, cols), 1) % tq).astype(F32)
        block = lambda steps: _fixed_frame_block(steps, [accs_ref, accw_ref],
                                                 [sel_bound + in_tile, win_bound + in_tile])
        block(first_steps)
        _chunk_loop(n_todo, sel_step, block, (8, 4, 2, 1))

    @pl.when(jnp.logical_not(safe))
    def _():
        states = [(ms_ref, accs_ref), (mw_ref, accw_ref)]
        for m_ref, _ in states:
            m_ref[...] = jnp.full_like(m_ref, NEG_INF)
        block = lambda steps: _softmax_block(steps, states)
        block(first_steps)
        _chunk_loop(n_todo, sel_step, block, (4, 2, 1))

    o_sel = accs_ref[0:HEAD_DIM, :] / accs_ref[HEAD_DIM:HEAD_DIM + 1, :]
    o_win = accw_ref[0:HEAD_DIM, :] / accw_ref[HEAD_DIM:HEAD_DIM + 1, :]

    gates = gate_ref[0, 0]
    outs = []
    for r in range(NSA_HPG):
        cs = slice(r * tq, (r + 1) * tq)
        outs.append(gates[3 * r:3 * r + 1] * ocmp_ref[0, 0, r] + gates[3 * r + 1:3 * r + 2] * o_sel[:, cs]
                    + gates[3 * r + 2:3 * r + 3] * o_win[:, cs])
    o_ref[0] = jnp.concatenate(outs, axis=0).T.astype(o_ref.dtype)


def sel_win_attention(proj, active, slope_cols, ks, vs_t, kw, vw_t, sel_bias, o_cmp, gates_t,
                      slopes, tq=128, tk=256):
    B, S, _ = proj.shape
    assert tq == tk and WINDOW == 2 * tk
    G = NSA_GROUPS
    n_sel = S // SEL_LEN
    cols = NSA_HPG * tq
    smem = pl.BlockSpec(memory_space=pltpu.SMEM)
    ks_spec = pl.BlockSpec((1, 1, S, ks.shape[-1]), lambda b, g, i: (b, g, 0, 0))
    kw_spec = pl.BlockSpec((1, 1, S, kw.shape[-1]), lambda b, g, i: (b, g, 0, 0))
    v_spec = pl.BlockSpec((1, 1, V_ROWS_64, S), lambda b, g, i: (b, g, 0, 0))
    return pl.pallas_call(
        functools.partial(_sel_win_kernel, tq=tq, tk=tk),
        grid=(B, G, S // tq),
        in_specs=[smem,
                  pl.BlockSpec((1, 1, 1, 1, S // tk), lambda b, g, i: (b, g, i, 0, 0),
                               memory_space=pltpu.SMEM),
                  pl.BlockSpec((1, tq, 256), lambda b, g, i: (b, i, g)),
                  pl.BlockSpec((1, NSA_HPG, HEAD_DIM), lambda b, g, i: (g, 0, 0)),
                  ks_spec, v_spec, kw_spec, v_spec,
                  pl.BlockSpec((1, 1, tq, n_sel), lambda b, g, i: (b, g, i, 0)),
                  pl.BlockSpec((1, 1, NSA_HPG, HEAD_DIM, tq), lambda b, g, i: (b, g, 0, 0, i)),
                  pl.BlockSpec((1, 1, NSA_HPG * N_GATES, tq), lambda b, g, i: (b, g, 0, i))],
        out_specs=pl.BlockSpec((1, tq, 256), lambda b, g, i: (b, i, g)),
        out_shape=jax.ShapeDtypeStruct((B, S, G * 256), BF16),
        scratch_shapes=[pltpu.VMEM((1, cols), F32), pltpu.VMEM((V_ROWS_64, cols), F32),
                        pltpu.VMEM((1, cols), F32), pltpu.VMEM((V_ROWS_64, cols), F32),
                        pltpu.SMEM((S // tk,), jnp.int32), pltpu.SMEM((2,), F32)],
        compiler_params=_params(2, 1),
        name="sel_win_attention",
    )(slopes, active, proj, slope_cols, ks, vs_t, kw, vw_t, sel_bias, o_cmp, gates_t)


def _alibi_slopes_log2(n_heads):
    slopes = np.exp2(-8.0 * (np.arange(n_heads, dtype=np.float32) + 1.0) / n_heads)
    return (slopes.astype(np.float32) * np.float32(LOG2E)).astype(np.float32)


def even_layer_mix(x2d, B, S, norm_g, w_in, lam_q1, lam_k1, lam_q2, lam_k2, subln, layer):
    proj = norm_proj(x2d, norm_g, w_in.astype(BF16), w_in.shape[1], [(0, 512), (1536, 2048)],
                     False).reshape(B, S, -1)
    sb_vt = proj[:, :, 1024:1536].transpose(0, 2, 1)
    df_vt = proj[:, :, 2560:3072].transpose(0, 2, 1).reshape(B, 4, 128, S)
    o_sb = sb_attention(proj, sb_vt)
    lam_rows = jnp.stack([lam_q1, lam_k1, lam_q2, lam_k2]).astype(F32)
    o_df = diff_attention(proj, _with_ones_row(df_vt, V_ROWS_128), lam_rows, subln.astype(F32),
                          _alibi_slopes_log2(4), layer)
    return jnp.concatenate([o_sb, o_df], axis=2).reshape(B * S, -1)


def odd_layer_mix(x2d, B, S, norm_g, w_in, pos_k, k_w1, k_w2, pos_v, v_w1, v_w2, tk=256):
    G, Dh = NSA_GROUPS, HEAD_DIM
    q_width = NSA_GROUPS * NSA_HPG * Dh
    kv_width = G * Dh
    n_main = q_width + 6 * kv_width
    pad = n_main + GATE_PAD - w_in.shape[1]
    w_pad = jnp.pad(w_in, ((0, 0), (0, pad))).astype(BF16)
    proj, gates = norm_proj(x2d, norm_g, w_pad, n_main, [(0, q_width)], True)
    proj = proj.reshape(B, S, n_main)

    def group_major(j):
        cols = proj[:, :, q_width + j * kv_width:q_width + (j + 1) * kv_width]
        return cols.reshape(B, S, G, Dh).transpose(0, 2, 1, 3)

    def keys_with_positions(j, extra=()):
        cols = [_key_position_columns(S, tk, Dh)] + list(extra)
        return jnp.concatenate([group_major(j)] + [jnp.broadcast_to(c, (B, G) + c.shape) for c in cols],
                               axis=-1)

    def values_transposed(j):
        return _with_ones_row(group_major(j).transpose(0, 1, 3, 2), V_ROWS_64)

    n_chunks = S // CMP_STRIDE
    chunks = jnp.stack([group_major(0), group_major(1)]).reshape(2, B * G, n_chunks, CMP_STRIDE * Dh)
    pos_flat = jnp.stack([pos_k, pos_v]).reshape(2, 1, CMP_LEN * Dh).astype(F32)
    w1 = jnp.stack([k_w1, v_w1]).astype(BF16)
    w2 = jnp.stack([k_w2, v_w2]).astype(BF16)
    cmp = compress_kv(chunks, pos_flat, w1, w2).reshape(2, B, G, n_chunks, Dh)

    n_sel = S // SEL_LEN
    cmp_start = jnp.arange(n_chunks) * CMP_STRIDE
    sel_start = jnp.arange(n_sel) * SEL_LEN
    overlap_t = ((cmp_start[None, :] < sel_start[:, None] + SEL_LEN)
                 & (sel_start[:, None] <= cmp_start[None, :] + CMP_LEN - 1)).astype(BF16)
    onehot = (jnp.arange(S)[:, None] // SEL_LEN == jnp.arange(n_sel)[None, :]).astype(BF16)
    slopes = _alibi_slopes_log2(NSA_GROUPS * NSA_HPG)
    slope_cols = _slope_pieces(slopes, Dh).reshape(G, NSA_HPG, Dh)

    tq = 256
    o_cmp, sel_bias, hits = cmp_select(proj, cmp[0], cmp[1].transpose(0, 1, 3, 2), overlap_t, slopes,
                                       tk, tq)
    active = (hits[..., 0] > 0).astype(jnp.int32)[:, :, :, None, :]
    gates_t = (gates[:, :G * NSA_HPG * N_GATES].reshape(B, S, G, NSA_HPG * N_GATES)
               .transpose(0, 2, 3, 1))
    o = sel_win_attention(proj, active, slope_cols, keys_with_positions(2, [onehot]),
                          values_transposed(3), keys_with_positions(4), values_transposed(5),
                          sel_bias, o_cmp, gates_t, slopes, tq=tq, tk=tk)
    return o.reshape(B * S, q_width)


def kernel(x, attn_norm, mlp_norm, final_norm, ev_w_in, ev_lam_q1, ev_lam_k1, ev_lam_q2, ev_lam_k2,
           ev_subln, ev_w_out, od_w_in, od_cmp_pos_k, od_cmp_k_w1, od_cmp_k_w2, od_cmp_pos_v,
           od_cmp_v_w1, od_cmp_v_w2, od_w_out, mlp_w1, mlp_w2):
    B, S, D = x.shape
    depth = attn_norm.shape[0]
    x2d = x.reshape(B * S, D)
    for layer in range(depth):
        idx = layer // 2
        if layer % 2 == 0:
            mix = even_layer_mix(x2d, B, S, attn_norm[layer], ev_w_in[idx], ev_lam_q1[idx],
                                 ev_lam_k1[idx], ev_lam_q2[idx], ev_lam_k2[idx], ev_subln[idx], layer)
            w_out = ev_w_out[idx]
        else:
            mix = odd_layer_mix(x2d, B, S, attn_norm[layer], od_w_in[idx], od_cmp_pos_k[idx],
                                od_cmp_k_w1[idx], od_cmp_k_w2[idx], od_cmp_pos_v[idx],
                                od_cmp_v_w1[idx], od_cmp_v_w2[idx])
            w_out = od_w_out[idx]
        g_final = final_norm if layer == depth - 1 else None
        x2d = post_block(mix, x2d, w_out.astype(BF16), mlp_norm[layer], mlp_w1[layer].astype(BF16),
                         mlp_w2[layer].astype(BF16), g_final)
    return x2d.reshape(B, S, D)
```

```python
import functools
import math

import jax
import jax.numpy as jnp
import numpy as np
from jax import lax
from jax.experimental import pallas as pl
from jax.experimental.pallas import tpu as pltpu

F32 = jnp.float32
BF16 = jnp.bfloat16

HEAD_DIM = 64
RMS_EPS = 1e-6
NEG_INF = -1e30
FORCE_SCORE = 1e6
NSA_GROUPS = 4
NSA_HPG = 4
CMP_LEN = 32
CMP_STRIDE = 16
SEL_LEN = 64
SEL_TOPK = 16
WINDOW = 512
N_GATES = 3
GATE_PAD = 128
SEL_DROP = 2.0 ** 24
BF16_EXACT_INT = 256
V_ROWS_64 = 80
V_ROWS_128 = 144

LOG2E = math.log2(math.e)
Q_SCALE = HEAD_DIM ** -0.5 * LOG2E
UNDERFLOW_LOG2 = -160.0
SAFE_EXP_LOG2 = 100.0
FAR = 1 << 30

VMEM_LIMIT = 56 * 1024 * 1024


def _params(n_parallel, n_arbitrary=0):
    return pltpu.CompilerParams(dimension_semantics=("parallel",) * n_parallel + ("arbitrary",) * n_arbitrary,
                                vmem_limit_bytes=VMEM_LIMIT)


def _rms(x, g):
    ms = jnp.mean(x * x, axis=-1, keepdims=True)
    return x * lax.rsqrt(ms + RMS_EPS) * g


def _dot(a, b):
    return jnp.dot(a, b, preferred_element_type=F32)


def _dot_nt(a, b):
    return lax.dot_general(a, b, (((1,), (1,)), ((), ())), preferred_element_type=F32)


def _split_bf16(x):
    hi = x.astype(BF16)
    lo = (x - hi.astype(F32)).astype(BF16)
    return hi, lo


def _slope_pieces(slopes, width):
    def top_bits(x):
        return (x.view(np.uint32) & np.uint32(0xFFFF0000)).view(np.float32)

    s1 = top_bits(slopes)
    r1 = slopes - s1
    s2 = top_bits(r1)
    s3 = top_bits(r1 - s2)
    out = np.zeros((slopes.shape[0], width), np.float32)
    out[:, :6] = np.stack([s1, s2, s3, s1, s2, s3], axis=1)
    return jnp.asarray(out).astype(BF16)


def _key_position_columns(n, tk, width):
    j = jnp.arange(n) % tk
    a = (j // BF16_EXACT_INT) * BF16_EXACT_INT
    b = j % BF16_EXACT_INT
    cols = jnp.stack([a, a, a, b, b, b], axis=1).astype(BF16)
    return jnp.pad(cols, ((0, 0), (0, width - cols.shape[1])))


def _with_ones_row(v_t, rows):
    d = v_t.shape[-2]
    ones = jnp.ones(v_t.shape[:-2] + (1, v_t.shape[-1]), v_t.dtype)
    pad = jnp.zeros(v_t.shape[:-2] + (rows - d - 1, v_t.shape[-1]), v_t.dtype)
    return jnp.concatenate([v_t, ones, pad], axis=-2)


def _softmax_block(steps, states):
    s_all = [scores() for scores, _, _, _ in steps]
    live = {k: (states[k][0][...], states[k][1][...]) for k in sorted({k for _, _, _, k in steps})}
    for (_, kap, values, k), s_t in zip(steps, s_all):
        m_run, acc = live[k]
        m_new = jnp.maximum(m_run, jnp.max(s_t, axis=0, keepdims=True) + kap)
        p = jnp.exp2(s_t - (m_new - kap)).astype(BF16)
        live[k] = (m_new, jnp.exp2(m_run - m_new) * acc + _dot(values(), p))
    for k, (m_run, acc) in live.items():
        states[k][0][...] = m_run
        states[k][1][...] = acc


def _fixed_frame_block(steps, acc_refs, frames):
    s_all = [scores() for scores, _, _, _ in steps]
    live = {k: acc_refs[k][...] for k in sorted({k for _, _, _, k in steps})}
    for (_, kap, values, k), s_t in zip(steps, s_all):
        live[k] = live[k] + _dot(values(), jnp.exp2(s_t - (frames[k] - kap)).astype(BF16))
    for k, acc in live.items():
        acc_refs[k][...] = acc


def _chunk_loop(n_steps, step, block, groups):
    done = 0
    for group in groups:
        def body(i, carry, group=group, done=done):
            block([step(done + group * i + j) for j in range(group)])
            return carry

        n_groups = (n_steps - done) // group
        lax.fori_loop(0, n_groups, body, 0)
        done = done + group * n_groups


def _ones_row_sums(x):
    return _dot_nt(jnp.ones((8, x.shape[1]), BF16), x.astype(BF16))[0:1]


def _norm_proj_kernel(x_ref, g_ref, w_ref, scale_ref, o_ref, *gate_ref, n_main, col_chunk):
    xn = _rms(x_ref[...], g_ref[...]).astype(BF16)
    for c in range(n_main // col_chunk):
        sl = slice(c * col_chunk, (c + 1) * col_chunk)
        o_ref[:, sl] = (_dot(xn, w_ref[:, sl]) * scale_ref[:, sl]).astype(o_ref.dtype)
    if gate_ref:
        logits = _dot(xn, w_ref[:, n_main:n_main + GATE_PAD])
        gate_ref[0][...] = jax.nn.sigmoid(logits)


def norm_proj(x2d, g, w_bf16, n_main, q_cols, with_gates, tm=512, col_chunk=512):
    T, D = x2d.shape
    n_w = w_bf16.shape[1]
    col = jnp.arange(n_main)
    is_q = functools.reduce(jnp.logical_or, [(col >= lo) & (col < hi) for lo, hi in q_cols])
    col_scale = jnp.where(is_q, Q_SCALE, 1.0).astype(F32).reshape(1, n_main)
    out_shape = [jax.ShapeDtypeStruct((T, n_main), BF16)]
    out_specs = [pl.BlockSpec((tm, n_main), lambda i: (i, 0))]
    if with_gates:
        out_shape.append(jax.ShapeDtypeStruct((T, GATE_PAD), F32))
        out_specs.append(pl.BlockSpec((tm, GATE_PAD), lambda i: (i, 0)))
    res = pl.pallas_call(
        functools.partial(_norm_proj_kernel, n_main=n_main, col_chunk=col_chunk),
        grid=(T // tm,),
        in_specs=[pl.BlockSpec((tm, D), lambda i: (i, 0)),
                  pl.BlockSpec((1, D), lambda i: (0, 0)),
                  pl.BlockSpec((D, n_w), lambda i: (0, 0)),
                  pl.BlockSpec((1, n_main), lambda i: (0, 0))],
        out_specs=out_specs,
        out_shape=out_shape,
        compiler_params=_params(1),
        name="norm_proj",
    )(x2d, g.reshape(1, D), w_bf16, col_scale)
    return res if with_gates else res[0]


def _head_pair_rows(q, t):
    lane = lax.broadcasted_iota(jnp.int32, (t, 2 * HEAD_DIM), 1)
    zero = jnp.zeros_like(q)
    return jnp.where(lane < HEAD_DIM, q, zero), jnp.where(lane >= HEAD_DIM, q, zero)


def _sb_kernel(q_ref, k_ref, vt_ref, o_ref, acc_ref, carry_ref, *, t):
    qi = pl.program_id(2)
    cols = 2 * t
    q_both = jnp.concatenate(_head_pair_rows(q_ref[0], t), axis=0)
    s_idx = lax.broadcasted_iota(jnp.int32, (t, 2 * t), 0)
    j_idx = lax.broadcasted_iota(jnp.int32, (t, 2 * t), 1) % t
    upper2 = jnp.where(j_idx > s_idx, 1.0, 0.0).astype(BF16)
    key = lax.broadcasted_iota(jnp.int32, (t, cols), 0)
    qry = lax.broadcasted_iota(jnp.int32, (t, cols), 1) % t
    past_diag = key < qry

    acc_ref[...] = jnp.zeros_like(acc_ref)
    carry_ref[...] = jnp.zeros_like(carry_ref)

    def chunks(specs):
        starts = [pl.multiple_of(kc * t, t) for kc, _ in specs]
        z_all = [_dot_nt(k_ref[0, pl.ds(start, t), :], q_both) for start in starts]
        carry = carry_ref[...]
        acc = acc_ref[...]
        for (_, masked), start, z in zip(specs, starts, z_all):
            drop = jnp.maximum(z, 0.0) + jnp.log2(1.0 + jnp.exp2(jnp.abs(z) * -1.0))
            log_beta = z - drop
            if masked:
                drop = jnp.where(past_diag, drop, 0.0)
            hi, lo = _split_bf16(drop)
            tail = _dot(upper2, jnp.concatenate([hi, lo], axis=0))
            w = jnp.exp2(log_beta - tail - carry)
            if masked:
                w = jnp.where(past_diag, w, 0.0)
            carry = carry + tail[0:1] + drop[0:1]
            acc = acc + _dot(vt_ref[0, :, pl.ds(start, t)], w.astype(BF16))
        carry_ref[...] = carry
        acc_ref[...] = acc
        return jnp.min(carry)

    least = lax.cond(qi > 0, lambda: chunks([(qi, True), (qi - 1, False)]), lambda: chunks([(qi, True)]))

    def cond(state):
        j, least_carry = state
        return (j < qi) & (least_carry < -UNDERFLOW_LOG2)

    def body(state):
        j, _ = state
        return j + 1, chunks([(qi - 1 - j, False)])

    lax.while_loop(cond, body, (1, least))
    row = lax.broadcasted_iota(jnp.int32, (2 * HEAD_DIM, t), 0)
    o_t = jnp.where(row < HEAD_DIM, acc_ref[:, :t], acc_ref[:, t:])
    o_ref[0] = o_t.T.astype(o_ref.dtype)


def sb_attention(proj, v_t, t=256):
    B, S, _ = proj.shape
    n_pairs = 4
    return pl.pallas_call(
        functools.partial(_sb_kernel, t=t),
        grid=(B, n_pairs, S // t),
        in_specs=[pl.BlockSpec((1, t, 128), lambda b, p, i: (b, i, p)),
                  pl.BlockSpec((1, S, 128), lambda b, p, i: (b, 0, n_pairs + p)),
                  pl.BlockSpec((1, 128, S), lambda b, p, i: (b, p, 0))],
        out_specs=pl.BlockSpec((1, t, 128), lambda b, p, i: (b, i, p)),
        out_shape=jax.ShapeDtypeStruct((B, S, n_pairs * 128), BF16),
        scratch_shapes=[pltpu.VMEM((128, 2 * t), F32), pltpu.VMEM((1, 2 * t), F32)],
        compiler_params=_params(3),
        name="sb_attention",
    )(proj, proj, v_t)


def _diff_kernel(slopes_ref, lam_ref, q_ref, k_ref, vt_ref, pos_ref, sl_ref, subln_ref, o_ref,
                 m_ref, acc_ref, offset_ref, knorm_ref, *, t, out_scale, lambda_init):
    h = pl.program_id(1)
    qi = pl.program_id(2)
    slope = slopes_ref[h]
    cols = 2 * t
    slope_cols = jnp.broadcast_to(sl_ref[0], (t, 128))
    q_parts = _head_pair_rows(q_ref[0], t)
    q_both = jnp.concatenate([jnp.concatenate([qc, slope_cols], axis=1) for qc in q_parts],
                             axis=0)
    lane = lax.broadcasted_iota(jnp.int32, (t, 2 * HEAD_DIM), 1)

    @pl.when(qi == 0)
    def _():
        def body(c, best):
            k = k_ref[0, pl.ds(pl.multiple_of(c * t, t), t), :].astype(F32)
            k2 = k * k
            return (jnp.maximum(best[0], jnp.max(jnp.sum(jnp.where(lane < HEAD_DIM, k2, 0.0), axis=1))),
                    jnp.maximum(best[1], jnp.max(jnp.sum(jnp.where(lane >= HEAD_DIM, k2, 0.0), axis=1))))

        best = lax.fori_loop(0, k_ref.shape[1] // t, body, (jnp.float32(0.0), jnp.float32(0.0)))
        knorm_ref[0] = best[0]
        knorm_ref[1] = best[1]
        offset_ref[...] = (lax.broadcasted_iota(jnp.int32, (t, cols), 0)
                           - lax.broadcasted_iota(jnp.int32, (t, cols), 1) % t)

    q32 = [qc.astype(F32) for qc in q_parts]
    col = lax.broadcasted_iota(jnp.int32, (1, cols), 1)
    k_norm2 = jnp.where(col < t, knorm_ref[0], knorm_ref[1])
    qk_bound = jnp.sqrt(jnp.concatenate([_ones_row_sums(x * x) for x in q32], axis=1) * k_norm2 * 1.05)
    k_self = k_ref[0, pl.ds(pl.multiple_of(qi * t, t), t), :].astype(F32)
    self_score = jnp.concatenate([_ones_row_sums(x * k_self) for x in q32], axis=1)
    safe = jnp.max(qk_bound - self_score) <= SAFE_EXP_LOG2

    pos = pos_ref[...]

    def step(kc, diagonal=False):
        start = pl.multiple_of(kc * t, t)

        def scores():
            s_t = _dot_nt(jnp.concatenate([k_ref[0, pl.ds(start, t), :], pos], axis=1), q_both)
            return jnp.where(offset_ref[...] <= 0, s_t, NEG_INF) if diagonal else s_t

        return scores, slope * ((kc - qi) * t).astype(F32), lambda: vt_ref[0, 0, :, pl.ds(start, t)], 0

    acc_ref[...] = jnp.zeros_like(acc_ref)

    @pl.when(safe)
    def _():
        frame = qk_bound + slope * (col % t).astype(F32)
        block = lambda steps: _fixed_frame_block(steps, [acc_ref], [frame])
        block([step(qi, True)])
        n_back = jnp.ceil((slope * (t - 1) - UNDERFLOW_LOG2) / (slope * t)).astype(jnp.int32)
        _chunk_loop(jnp.minimum(n_back, qi), lambda i: step(qi - 1 - i), block, (8, 4, 2, 1))

    @pl.when(jnp.logical_not(safe))
    def _():
        m_ref[...] = jnp.full_like(m_ref, NEG_INF)
        block = lambda steps: _softmax_block(steps, [(m_ref, acc_ref)])
        block([step(qi, True)])
        excess = jnp.max(qk_bound - m_ref[...]) + slope * (t - 1)
        n_back = jnp.clip(jnp.ceil((excess - UNDERFLOW_LOG2) / (slope * t)).astype(jnp.int32), 0, qi)
        _chunk_loop(n_back, lambda i: step(qi - 1 - i), block, (4, 2, 1))

    lam_terms = lam_ref[...]
    lam = (jnp.exp(jnp.sum(lam_terms[0:1] * lam_terms[1:2], axis=1, keepdims=True))
           - jnp.exp(jnp.sum(lam_terms[2:3] * lam_terms[3:4], axis=1, keepdims=True))
           + lambda_init)
    o_t = acc_ref[0:128, :] / acc_ref[128:129, :]
    o = (o_t[:, :t] - lam * o_t[:, t:]).T
    o_ref[0] = (_rms(o, subln_ref[...]) * out_scale).astype(o_ref.dtype)


def diff_attention(proj, vt_aug, lam_rows, subln, slopes, layer, t=512):
    B, S, _ = proj.shape
    n_heads = 4
    lambda_init = 0.8 - 0.6 * math.exp(-0.3 * layer)
    smem = pl.BlockSpec(memory_space=pltpu.SMEM)
    pos = _key_position_columns(t, t, 128)
    slope_cols = _slope_pieces(slopes, 128).reshape(n_heads, 1, 128)
    return pl.pallas_call(
        functools.partial(_diff_kernel, t=t, out_scale=1.0 - lambda_init, lambda_init=lambda_init),
        grid=(B, n_heads, S // t),
        in_specs=[smem,
                  pl.BlockSpec((4, HEAD_DIM), lambda b, h, i: (0, 0)),
                  pl.BlockSpec((1, t, 128), lambda b, h, i: (b, i, 12 + h)),
                  pl.BlockSpec((1, S, 128), lambda b, h, i: (b, 0, 16 + h)),
                  pl.BlockSpec((1, 1, V_ROWS_128, S), lambda b, h, i: (b, h, 0, 0)),
                  pl.BlockSpec((t, 128), lambda b, h, i: (0, 0)),
                  pl.BlockSpec((1, 1, 128), lambda b, h, i: (h, 0, 0)),
                  pl.BlockSpec((1, 128), lambda b, h, i: (0, 0))],
        out_specs=pl.BlockSpec((1, t, 128), lambda b, h, i: (b, i, h)),
        out_shape=jax.ShapeDtypeStruct((B, S, n_heads * 128), BF16),
        scratch_shapes=[pltpu.VMEM((1, 2 * t), F32), pltpu.VMEM((V_ROWS_128, 2 * t), F32),
                        pltpu.VMEM((t, 2 * t), jnp.int32), pltpu.SMEM((2,), F32)],
        compiler_params=_params(2, 1),
        name="diff_attention",
    )(slopes, lam_rows, proj, proj, vt_aug, pos, slope_cols, subln.reshape(1, 128))


def _post_kernel(*refs, ff_chunk, final):
    mix_ref, x_ref, wo_ref, g_ref, w1_ref, w2_ref = refs[:6]
    gf_ref = refs[6] if final else None
    o_ref = refs[-1]
    x = x_ref[...] + _dot(mix_ref[...], wo_ref[...])
    hn = _rms(x, g_ref[...]).astype(BF16)
    acc = x
    for f in range(w1_ref.shape[1] // ff_chunk):
        sl = slice(f * ff_chunk, (f + 1) * ff_chunk)
        hid = jnp.maximum(_dot(hn, w1_ref[:, sl]), 0.0)
        acc = acc + _dot((hid * hid).astype(BF16), w2_ref[sl, :])
    if final:
        acc = _rms(acc, gf_ref[...])
    o_ref[...] = acc


def post_block(mix, x2d, w_out, g_mlp, w1, w2, g_final=None, tm=512, ff_chunk=1024):
    T, D = x2d.shape
    final = g_final is not None
    const = lambda i: (0, 0)
    in_specs = [pl.BlockSpec((tm, mix.shape[1]), lambda i: (i, 0)),
                pl.BlockSpec((tm, D), lambda i: (i, 0)),
                pl.BlockSpec(w_out.shape, const),
                pl.BlockSpec((1, D), const), pl.BlockSpec(w1.shape, const), pl.BlockSpec(w2.shape, const)]
    args = [mix, x2d, w_out, g_mlp.reshape(1, D), w1, w2]
    if final:
        in_specs.append(pl.BlockSpec((1, D), const))
        args.append(g_final.reshape(1, D))
    return pl.pallas_call(
        functools.partial(_post_kernel, ff_chunk=ff_chunk, final=final),
        grid=(T // tm,),
        in_specs=in_specs,
        out_specs=pl.BlockSpec((tm, D), lambda i: (i, 0)),
        out_shape=jax.ShapeDtypeStruct((T, D), F32),
        compiler_params=_params(1),
        name="post_block",
    )(*args)


def _compress_kernel(c_ref, pos_ref, w1_ref, w2_ref, o_ref):
    half = w1_ref.shape[1] // 2
    chunks = c_ref[0, 0]
    pos = jnp.broadcast_to(pos_ref[0], (8, 2 * half)).astype(BF16)
    first = _dot(chunks, w1_ref[0, :half, :])
    second = _dot(chunks, w1_ref[0, half:, :])
    n_chunks = chunks.shape[0]
    pre = first + pltpu.roll(second, n_chunks - 1, 0) + _dot(pos, w1_ref[0])[0:1]
    hid = jax.nn.gelu(pre)
    o_ref[0, 0] = _dot(hid.astype(BF16), w2_ref[0]).astype(o_ref.dtype)


def compress_kv(chunks, pos_flat, w1, w2):
    _, BG, n_chunks, width = chunks.shape
    hidden = w1.shape[-1]
    return pl.pallas_call(
        _compress_kernel,
        grid=(2, BG),
        in_specs=[pl.BlockSpec((1, 1, n_chunks, width), lambda s, i: (s, i, 0, 0)),
                  pl.BlockSpec((1, 1, 2 * width), lambda s, i: (s, 0, 0)),
                  pl.BlockSpec((1, 2 * width, hidden), lambda s, i: (s, 0, 0)),
                  pl.BlockSpec((1, hidden, HEAD_DIM), lambda s, i: (s, 0, 0))],
        out_specs=pl.BlockSpec((1, 1, n_chunks, HEAD_DIM), lambda s, i: (s, i, 0, 0)),
        out_shape=jax.ShapeDtypeStruct((2, BG, n_chunks, HEAD_DIM), BF16),
        compiler_params=_params(2),
        name="compress_kv",
    )(chunks, pos_flat, w1, w2)


def _stack_heads(q):
    return jnp.concatenate([q[:, r * HEAD_DIM:(r + 1) * HEAD_DIM] for r in range(NSA_HPG)], axis=0)


def _slope_row(slopes_ref, g, tq):
    col = lax.broadcasted_iota(jnp.int32, (1, NSA_HPG * tq), 1)
    out = jnp.zeros((1, NSA_HPG * tq), F32)
    for r in range(NSA_HPG):
        out = jnp.where(col // tq == r, slopes_ref[g * NSA_HPG + r], out)
    return out


def _cmp_select_kernel(slopes_ref, q_ref, kc_ref, vct_ref, ovt_ref, ocmp_ref, sel_ref, hits_ref,
                       *, tq, n_sel, blocks_per_chunk, hit_tile):
    g = pl.program_id(1)
    t0 = pl.program_id(2) * tq
    cols = NSA_HPG * tq
    n_cmp = kc_ref.shape[2]
    slope_row = _slope_row(slopes_ref, g, tq)
    q_rows = _stack_heads(q_ref[0])
    tpos = t0 + lax.broadcasted_iota(jnp.int32, (n_cmp, cols), 1) % tq
    cmp_end = lax.broadcasted_iota(jnp.int32, (n_cmp, cols), 0) * CMP_STRIDE + (CMP_LEN - 1)
    dc = (tpos - cmp_end).astype(F32)
    sc = jnp.where(dc >= 0, _dot_nt(kc_ref[0, 0], q_rows) - slope_row * dc, NEG_INF)
    e = jnp.exp2(sc - jnp.max(sc, axis=0, keepdims=True))
    any_valid = jnp.where(dc[0:1] >= 0, 1.0, 0.0)
    pc = e * (any_valid / jnp.sum(e, axis=0, keepdims=True))
    o_t = _dot(vct_ref[0, 0], pc.astype(BF16))
    for r in range(NSA_HPG):
        ocmp_ref[0, 0, r] = o_t[:, r * tq:(r + 1) * tq]

    pc_group = pc[:, 0:tq]
    for r in range(1, NSA_HPG):
        pc_group = pc_group + pc[:, r * tq:(r + 1) * tq]
    hi, lo = _split_bf16(pc_group)
    imp = _dot(ovt_ref[...], hi) + _dot(ovt_ref[...], lo)

    blk = lax.broadcasted_iota(jnp.int32, (n_sel, tq), 0)
    cur = (t0 + lax.broadcasted_iota(jnp.int32, (n_sel, tq), 1)) // SEL_LEN
    forced = (blk == 0) | (blk == cur) | (blk == cur - 1)
    imp = jnp.where(forced, FORCE_SCORE, imp)
    imp = jnp.where(blk <= cur, imp, -1.0)
    picked = jnp.zeros((n_sel, tq), F32)
    for _ in range(min(SEL_TOPK, n_sel)):
        best = jnp.max(imp, axis=0, keepdims=True)
        first = jnp.min(jnp.where(imp == best, blk, n_sel), axis=0, keepdims=True)
        hit = blk == first
        picked = jnp.where(hit, 1.0, picked)
        imp = jnp.where(hit, -2.0, imp)
    keep = (picked > 0.5) & (blk <= cur)
    sel_ref[0, 0] = jnp.where(keep, 0.0, -SEL_DROP).T.astype(sel_ref.dtype)
    n_chunks = n_sel // blocks_per_chunk
    member = (lax.broadcasted_iota(jnp.int32, (n_chunks, n_sel), 1) // blocks_per_chunk
              == lax.broadcasted_iota(jnp.int32, (n_chunks, n_sel), 0))
    per_query = _dot(jnp.where(member, 1.0, 0.0).astype(BF16),
                     jnp.where(keep, 1.0, 0.0).astype(BF16)).astype(BF16)
    for part in range(tq // hit_tile):
        hits_ref[0, 0, part] = _dot(per_query[:, part * hit_tile:(part + 1) * hit_tile],
                                    jnp.ones((hit_tile, 128), BF16))


def cmp_select(proj, kc, vc_t, overlap_t, slopes, tk, hit_tile, tq=256):
    B, S, _ = proj.shape
    n_cmp = kc.shape[2]
    n_sel = S // SEL_LEN
    G = NSA_GROUPS
    smem = pl.BlockSpec(memory_space=pltpu.SMEM)
    return pl.pallas_call(
        functools.partial(_cmp_select_kernel, tq=tq, n_sel=n_sel, blocks_per_chunk=tk // SEL_LEN,
                          hit_tile=hit_tile),
        grid=(B, G, S // tq),
        in_specs=[smem,
                  pl.BlockSpec((1, tq, 256), lambda b, g, i: (b, i, g)),
                  pl.BlockSpec((1, 1, n_cmp, HEAD_DIM), lambda b, g, i: (b, g, 0, 0)),
                  pl.BlockSpec((1, 1, HEAD_DIM, n_cmp), lambda b, g, i: (b, g, 0, 0)),
                  pl.BlockSpec((n_sel, n_cmp), lambda b, g, i: (0, 0))],
        out_specs=[pl.BlockSpec((1, 1, NSA_HPG, HEAD_DIM, tq), lambda b, g, i: (b, g, 0, 0, i)),
                   pl.BlockSpec((1, 1, tq, n_sel), lambda b, g, i: (b, g, i, 0)),
                   pl.BlockSpec((1, 1, tq // hit_tile, S // tk, 128), lambda b, g, i: (b, g, i, 0, 0))],
        out_shape=[jax.ShapeDtypeStruct((B, G, NSA_HPG, HEAD_DIM, S), F32),
                   jax.ShapeDtypeStruct((B, G, S, n_sel), BF16),
                   jax.ShapeDtypeStruct((B, G, S // hit_tile, S // tk, 128), F32)],
        compiler_params=_params(3),
        name="cmp_select",
    )(slopes, proj, kc, vc_t, overlap_t)


def _sel_win_kernel(slopes_ref, active_ref, q_ref, sl_ref, ks_ref, vst_ref, kw_ref, vwt_ref, sel_ref,
                    ocmp_ref, gate_ref, o_ref, ms_ref, accs_ref, mw_ref, accw_ref, offset_ref, todo_ref,
                    knorm_ref, *, tq, tk):
    g = pl.program_id(1)
    t0 = pl.program_id(2) * tq
    cols = NSA_HPG * tq
    slope_row = _slope_row(slopes_ref, g, tq)
    q = q_ref[0]
    q_aug = jnp.concatenate(
        [jnp.concatenate([q[:, r * HEAD_DIM:(r + 1) * HEAD_DIM],
                          jnp.broadcast_to(sl_ref[0, r:r + 1, :], (tq, HEAD_DIM))], axis=1)
         for r in range(NSA_HPG)], axis=0)
    q_sel = jnp.concatenate([q_aug, jnp.concatenate([sel_ref[0, 0]] * NSA_HPG, axis=0)], axis=1)

    @pl.when(pl.program_id(2) == 0)
    def _():
        offset_ref[...] = (lax.broadcasted_iota(jnp.int32, (tk, cols), 0)
                           - lax.broadcasted_iota(jnp.int32, (tk, cols), 1) % tq)

        def body(c, best):
            rows = pl.ds(pl.multiple_of(c * tk, tk), tk)
            ks = ks_ref[0, 0, rows, :][:, :HEAD_DIM].astype(F32)
            kw = kw_ref[0, 0, rows, :][:, :HEAD_DIM].astype(F32)
            return (jnp.maximum(best[0], jnp.max(jnp.sum(ks * ks, axis=1))),
                    jnp.maximum(best[1], jnp.max(jnp.sum(kw * kw, axis=1))))

        best = lax.fori_loop(0, ks_ref.shape[2] // tk, body, (jnp.float32(0.0), jnp.float32(0.0)))
        knorm_ref[0] = best[0]
        knorm_ref[1] = best[1]

    heads32 = [q.astype(F32)[:, r * HEAD_DIM:(r + 1) * HEAD_DIM] for r in range(NSA_HPG)]
    q_norm2 = jnp.concatenate([_ones_row_sums(x * x) for x in heads32], axis=1)
    own_rows = pl.ds(pl.multiple_of(t0, tq), tq)

    def bound_and_self(k_ref, k_norm2):
        k_self = k_ref[0, 0, own_rows, :][:, :HEAD_DIM].astype(F32)
        own = jnp.concatenate([_ones_row_sums(x * k_self) for x in heads32], axis=1)
        return jnp.sqrt(q_norm2 * k_norm2 * 1.05), own

    sel_bound, sel_self = bound_and_self(ks_ref, knorm_ref[0])
    win_bound, win_self = bound_and_self(kw_ref, knorm_ref[1])
    safe = jnp.maximum(jnp.max(sel_bound - sel_self), jnp.max(win_bound - win_self)) <= SAFE_EXP_LOG2

    def step(c, k_ref, q_rows, vt_ref, keep=None):
        inside = c >= 0
        start = pl.multiple_of(jnp.maximum(c, 0) * tk, tk)
        shift = c * tk - t0

        def scores():
            s_t = _dot_nt(k_ref[0, 0, pl.ds(start, tk), :], q_rows)
            if keep is None:
                return s_t
            lo, hi = keep
            offset = offset_ref[...]
            if hi is not None:
                return jnp.where(offset <= jnp.where(inside, hi - shift, -FAR), s_t, NEG_INF)
            return jnp.where(offset > jnp.where(inside, lo - shift, FAR), s_t, NEG_INF)

        kappa = jnp.where(inside, slope_row * shift.astype(F32), NEG_INF)
        return scores, kappa, lambda: vt_ref[0, 0, :, pl.ds(start, tk)]

    last = t0 // tk
    causal, recent, anything = (None, 0), (-WINDOW, None), (-FAR, None)
    first_steps = [step(last - 2, kw_ref, q_aug, vwt_ref, recent) + (1,),
                   step(last, ks_ref, q_sel, vst_ref, causal) + (0,),
                   step(last - 1, kw_ref, q_aug, vwt_ref, anything) + (1,),
                   step(last, kw_ref, q_aug, vwt_ref, causal) + (1,)]

    def note_active(c, n):
        hit = active_ref[0, 0, 0, 0, c] > 0

        @pl.when(hit)
        def _():
            todo_ref[n] = c

        return n + hit.astype(jnp.int32)

    n_todo = lax.fori_loop(0, last, note_active, 0)
    sel_step = lambda i: step(todo_ref[i], ks_ref, q_sel, vst_ref) + (0,)

    accs_ref[...] = jnp.zeros_like(accs_ref)
    accw_ref[...] = jnp.zeros_like(accw_ref)

    @pl.when(safe)
    def _():
        in_tile = slope_row * (lax.broadcasted_iota(jnp.int32, (1, cols), 1) % tq).astype(F32)
        block = lambda steps: _fixed_frame_block(steps, [accs_ref, accw_ref],
                                                 [sel_bound + in_tile, win_bound + in_tile])
        block(first_steps)
        _chunk_loop(n_todo, sel_step, block, (8, 4, 2, 1))

    @pl.when(jnp.logical_not(safe))
    def _():
        states = [(ms_ref, accs_ref), (mw_ref, accw_ref)]
        for m_ref, _ in states:
            m_ref[...] = jnp.full_like(m_ref, NEG_INF)
        block = lambda steps: _softmax_block(steps, states)
        block(first_steps)
        _chunk_loop(n_todo, sel_step, block, (4, 2, 1))

    o_sel = accs_ref[0:HEAD_DIM, :] / accs_ref[HEAD_DIM:HEAD_DIM + 1, :]
    o_win = accw_ref[0:HEAD_DIM, :] / accw_ref[HEAD_DIM:HEAD_DIM + 1, :]

    gates = gate_ref[0, 0]
    outs = []
    for r in range(NSA_HPG):
        cs = slice(r * tq, (r + 1) * tq)
        outs.append(gates[3 * r:3 * r + 1] * ocmp_ref[0, 0, r] + gates[3 * r + 1:3 * r + 2] * o_sel[:, cs]
                    + gates[3 * r + 2:3 * r + 3] * o_win[:, cs])
    o_ref[0] = jnp.concatenate(outs, axis=0).T.astype(o_ref.dtype)


def sel_win_attention(proj, active, slope_cols, ks, vs_t, kw, vw_t, sel_bias, o_cmp, gates_t,
                      slopes, tq=128, tk=256):
    B, S, _ = proj.shape
    assert tq == tk and WINDOW == 2 * tk
    G = NSA_GROUPS
    n_sel = S // SEL_LEN
    cols = NSA_HPG * tq
    smem = pl.BlockSpec(memory_space=pltpu.SMEM)
    ks_spec = pl.BlockSpec((1, 1, S, ks.shape[-1]), lambda b, g, i: (b, g, 0, 0))
    kw_spec = pl.BlockSpec((1, 1, S, kw.shape[-1]), lambda b, g, i: (b, g, 0, 0))
    v_spec = pl.BlockSpec((1, 1, V_ROWS_64, S), lambda b, g, i: (b, g, 0, 0))
    return pl.pallas_call(
        functools.partial(_sel_win_kernel, tq=tq, tk=tk),
        grid=(B, G, S // tq),
        in_specs=[smem,
                  pl.BlockSpec((1, 1, 1, 1, S // tk), lambda b, g, i: (b, g, i, 0, 0),
                               memory_space=pltpu.SMEM),
                  pl.BlockSpec((1, tq, 256), lambda b, g, i: (b, i, g)),
                  pl.BlockSpec((1, NSA_HPG, HEAD_DIM), lambda b, g, i: (g, 0, 0)),
                  ks_spec, v_spec, kw_spec, v_spec,
                  pl.BlockSpec((1, 1, tq, n_sel), lambda b, g, i: (b, g, i, 0)),
                  pl.BlockSpec((1, 1, NSA_HPG, HEAD_DIM, tq), lambda b, g, i: (b, g, 0, 0, i)),
                  pl.BlockSpec((1, 1, NSA_HPG * N_GATES, tq), lambda b, g, i: (b, g, 0, i))],
        out_specs=pl.BlockSpec((1, tq, 256), lambda b, g, i: (b, i, g)),
        out_shape=jax.ShapeDtypeStruct((B, S, G * 256), BF16),
        scratch_shapes=[pltpu.VMEM((1, cols), F32), pltpu.VMEM((V_ROWS_64, cols), F32),
                        pltpu.VMEM((1, cols), F32), pltpu.VMEM((V_ROWS_64, cols), F32),
                        pltpu.VMEM((tk, cols), jnp.int32),
                        pltpu.SMEM((S // tk,), jnp.int32), pltpu.SMEM((2,), F32)],
        compiler_params=_params(2, 1),
        name="sel_win_attention",
    )(slopes, active, proj, slope_cols, ks, vs_t, kw, vw_t, sel_bias, o_cmp, gates_t)


def _alibi_slopes_log2(n_heads):
    slopes = np.exp2(-8.0 * (np.arange(n_heads, dtype=np.float32) + 1.0) / n_heads)
    return (slopes.astype(np.float32) * np.float32(LOG2E)).astype(np.float32)


def even_layer_mix(x2d, B, S, norm_g, w_in, lam_q1, lam_k1, lam_q2, lam_k2, subln, layer):
    proj = norm_proj(x2d, norm_g, w_in.astype(BF16), w_in.shape[1], [(0, 512), (1536, 2048)],
                     False).reshape(B, S, -1)
    sb_vt = proj[:, :, 1024:1536].transpose(0, 2, 1)
    df_vt = proj[:, :, 2560:3072].transpose(0, 2, 1).reshape(B, 4, 128, S)
    o_sb = sb_attention(proj, sb_vt)
    lam_rows = jnp.stack([lam_q1, lam_k1, lam_q2, lam_k2]).astype(F32)
    o_df = diff_attention(proj, _with_ones_row(df_vt, V_ROWS_128), lam_rows, subln.astype(F32),
                          _alibi_slopes_log2(4), layer)
    return jnp.concatenate([o_sb, o_df], axis=2).reshape(B * S, -1)


def odd_layer_mix(x2d, B, S, norm_g, w_in, pos_k, k_w1, k_w2, pos_v, v_w1, v_w2, tk=256):
    G, Dh = NSA_GROUPS, HEAD_DIM
    q_width = NSA_GROUPS * NSA_HPG * Dh
    kv_width = G * Dh
    n_main = q_width + 6 * kv_width
    pad = n_main + GATE_PAD - w_in.shape[1]
    w_pad = jnp.pad(w_in, ((0, 0), (0, pad))).astype(BF16)
    proj, gates = norm_proj(x2d, norm_g, w_pad, n_main, [(0, q_width)], True)
    proj = proj.reshape(B, S, n_main)

    def group_major(j):
        cols = proj[:, :, q_width + j * kv_width:q_width + (j + 1) * kv_width]
        return cols.reshape(B, S, G, Dh).transpose(0, 2, 1, 3)

    def keys_with_positions(j, extra=()):
        cols = [_key_position_columns(S, tk, Dh)] + list(extra)
        return jnp.concatenate([group_major(j)] + [jnp.broadcast_to(c, (B, G) + c.shape) for c in cols],
                               axis=-1)

    def values_transposed(j):
        return _with_ones_row(group_major(j).transpose(0, 1, 3, 2), V_ROWS_64)

    n_chunks = S // CMP_STRIDE
    chunks = jnp.stack([group_major(0), group_major(1)]).reshape(2, B * G, n_chunks, CMP_STRIDE * Dh)
    pos_flat = jnp.stack([pos_k, pos_v]).reshape(2, 1, CMP_LEN * Dh).astype(F32)
    w1 = jnp.stack([k_w1, v_w1]).astype(BF16)
    w2 = jnp.stack([k_w2, v_w2]).astype(BF16)
    cmp = compress_kv(chunks, pos_flat, w1, w2).reshape(2, B, G, n_chunks, Dh)

    n_sel = S // SEL_LEN
    cmp_start = jnp.arange(n_chunks) * CMP_STRIDE
    sel_start = jnp.arange(n_sel) * SEL_LEN
    overlap_t = ((cmp_start[None, :] < sel_start[:, None] + SEL_LEN)
                 & (sel_start[:, None] <= cmp_start[None, :] + CMP_LEN - 1)).astype(BF16)
    onehot = (jnp.arange(S)[:, None] // SEL_LEN == jnp.arange(n_sel)[None, :]).astype(BF16)
    slopes = _alibi_slopes_log2(NSA_GROUPS * NSA_HPG)
    slope_cols = _slope_pieces(slopes, Dh).reshape(G, NSA_HPG, Dh)

    tq = 256
    o_cmp, sel_bias, hits = cmp_select(proj, cmp[0], cmp[1].transpose(0, 1, 3, 2), overlap_t, slopes,
                                       tk, tq)
    active = (hits[..., 0] > 0).astype(jnp.int32)[:, :, :, None, :]
    gates_t = (gates[:, :G * NSA_HPG * N_GATES].reshape(B, S, G, NSA_HPG * N_GATES)
               .transpose(0, 2, 3, 1))
    o = sel_win_attention(proj, active, slope_cols, keys_with_positions(2, [onehot]),
                          values_transposed(3), keys_with_positions(4), values_transposed(5),
                          sel_bias, o_cmp, gates_t, slopes, tq=tq, tk=tk)
    return o.reshape(B * S, q_width)


def kernel(x, attn_norm, mlp_norm, final_norm, ev_w_in, ev_lam_q1, ev_lam_k1, ev_lam_q2, ev_lam_k2,
           ev_subln, ev_w_out, od_w_in, od_cmp_pos_k, od_cmp_k_w1, od_cmp_k_w2, od_cmp_pos_v,
           od_cmp_v_w1, od_cmp_v_w2, od_w_out, mlp_w1, mlp_w2):
    B, S, D = x.shape
    depth = attn_norm.shape[0]
    x2d = x.reshape(B * S, D)
    for layer in range(depth):
        idx = layer // 2
        if layer % 2 == 0:
            mix = even_layer_mix(x2d, B, S, attn_norm[layer], ev_w_in[idx], ev_lam_q1[idx],
                                 ev_lam_k1[idx], ev_lam_q2[idx], ev_lam_k2[idx], ev_subln[idx], layer)
            w_out = ev_w_out[idx]
        else:
            mix = odd_layer_mix(x2d, B, S, attn_norm[layer], od_w_in[idx], od_cmp_pos_k[idx],
                                od_cmp_k_w1[idx], od_cmp_k_w2[idx], od_cmp_pos_v[idx],
                                od_cmp_v_w1[idx], od_cmp_v_w2[idx])
            w_out = od_w_out[idx]
        g_final = final_norm if layer == depth - 1 else None
        x2d = post_block(mix, x2d, w_out.astype(BF16), mlp_norm[layer], mlp_w1[layer].astype(BF16),
                         mlp_w2[layer].astype(BF16), g_final)
    return x2d.reshape(B, S, D)
```

```python
import functools
import math

import jax
import jax.numpy as jnp
import numpy as np
from jax import lax
from jax.experimental import pallas as pl
from jax.experimental.pallas import tpu as pltpu

F32 = jnp.float32
BF16 = jnp.bfloat16

HEAD_DIM = 64
RMS_EPS = 1e-6
NEG_INF = -1e30
FORCE_SCORE = 1e6
NSA_GROUPS = 4
NSA_HPG = 4
CMP_LEN = 32
CMP_STRIDE = 16
SEL_LEN = 64
SEL_TOPK = 16
WINDOW = 512
N_GATES = 3
GATE_PAD = 128
SEL_DROP = 2.0 ** 24
BF16_EXACT_INT = 256
V_ROWS_64 = 80
V_ROWS_128 = 144

LOG2E = math.log2(math.e)
Q_SCALE = HEAD_DIM ** -0.5 * LOG2E
UNDERFLOW_LOG2 = -160.0
SAFE_EXP_LOG2 = 100.0
FAR = 1 << 30

VMEM_LIMIT = 56 * 1024 * 1024


def _params(n_parallel, n_arbitrary=0):
    return pltpu.CompilerParams(dimension_semantics=("parallel",) * n_parallel + ("arbitrary",) * n_arbitrary,
                                vmem_limit_bytes=VMEM_LIMIT)


def _rms(x, g):
    ms = jnp.mean(x * x, axis=-1, keepdims=True)
    return x * lax.rsqrt(ms + RMS_EPS) * g


def _dot(a, b):
    return jnp.dot(a, b, preferred_element_type=F32)


def _dot_nt(a, b):
    return lax.dot_general(a, b, (((1,), (1,)), ((), ())), preferred_element_type=F32)


def _split_bf16(x):
    hi = x.astype(BF16)
    lo = (x - hi.astype(F32)).astype(BF16)
    return hi, lo


def _slope_pieces(slopes, width):
    def top_bits(x):
        return (x.view(np.uint32) & np.uint32(0xFFFF0000)).view(np.float32)

    s1 = top_bits(slopes)
    r1 = slopes - s1
    s2 = top_bits(r1)
    s3 = top_bits(r1 - s2)
    out = np.zeros((slopes.shape[0], width), np.float32)
    out[:, :6] = np.stack([s1, s2, s3, s1, s2, s3], axis=1)
    return jnp.asarray(out).astype(BF16)


def _key_position_columns(n, tk, width):
    j = jnp.arange(n) % tk
    a = (j // BF16_EXACT_INT) * BF16_EXACT_INT
    b = j % BF16_EXACT_INT
    cols = jnp.stack([a, a, a, b, b, b], axis=1).astype(BF16)
    return jnp.pad(cols, ((0, 0), (0, width - cols.shape[1])))


def _softmax_block(steps, states):
    s_all = [scores() for scores, _, _, _ in steps]
    live = {k: (states[k][0][...], states[k][1][...]) for k in sorted({k for _, _, _, k in steps})}
    for (_, kap, values, k), s_t in zip(steps, s_all):
        m_run, acc = live[k]
        m_new = jnp.maximum(m_run, jnp.max(s_t, axis=0, keepdims=True) + kap)
        p = jnp.exp2(s_t - (m_new - kap)).astype(BF16)
        live[k] = (m_new, jnp.exp2(m_run - m_new) * acc + _dot(values(), p))
    for k, (m_run, acc) in live.items():
        states[k][0][...] = m_run
        states[k][1][...] = acc


def _fixed_frame_block(steps, acc_refs, frames):
    s_all = [scores() for scores, _, _, _ in steps]
    live = {k: acc_refs[k][...] for k in sorted({k for _, _, _, k in steps})}
    for (_, kap, values, k), s_t in zip(steps, s_all):
        live[k] = live[k] + _dot(values(), jnp.exp2(s_t - (frames[k] - kap)).astype(BF16))
    for k, acc in live.items():
        acc_refs[k][...] = acc


def _chunk_loop(n_steps, step, block, groups):
    done = 0
    for group in groups:
        def body(i, carry, group=group, done=done):
            block([step(done + group * i + j) for j in range(group)])
            return carry

        n_groups = (n_steps - done) // group
        lax.fori_loop(0, n_groups, body, 0)
        done = done + group * n_groups


def _ones_row_sums(x):
    return _dot_nt(jnp.ones((8, x.shape[1]), BF16), x.astype(BF16))[0:1]


PROJ_CHUNK = 512


def _ones_row_tail(rows, width):
    return jnp.where(lax.broadcasted_iota(jnp.int32, (rows, width), 0) == 0, 1.0, 0.0).astype(BF16)


def _even_proj_kernel(x_ref, g_ref, w_ref, o_ref, sbv_ref, dfv_ref):
    tm = x_ref.shape[0]
    xn = _rms(x_ref[...], g_ref[...]).astype(BF16)
    tail = _ones_row_tail(V_ROWS_128 - 2 * HEAD_DIM, tm)
    for c in range(w_ref.shape[1] // PROJ_CHUNK):
        sl = slice(c * PROJ_CHUNK, (c + 1) * PROJ_CHUNK)
        res = _dot(xn, w_ref[:, sl])
        if c in (0, 3):
            res = res * Q_SCALE
        o_ref[:, sl] = res.astype(BF16)
        if c == 2:
            sbv_ref[0] = res.T.astype(BF16)
        if c == 5:
            v_t = res.T.astype(BF16)
            for h in range(dfv_ref.shape[1]):
                dfv_ref[0, h, 0:2 * HEAD_DIM, :] = v_t[h * 2 * HEAD_DIM:(h + 1) * 2 * HEAD_DIM]
                dfv_ref[0, h, 2 * HEAD_DIM:V_ROWS_128, :] = tail


def even_proj(x2d, B, S, g, w_bf16, tm=512):
    T, D = x2d.shape
    n_w = w_bf16.shape[1]
    per_seq = S // tm
    assert n_w == 6 * PROJ_CHUNK
    return pl.pallas_call(
        _even_proj_kernel,
        grid=(T // tm,),
        in_specs=[pl.BlockSpec((tm, D), lambda i: (i, 0)),
                  pl.BlockSpec((1, D), lambda i: (0, 0)),
                  pl.BlockSpec((D, n_w), lambda i: (0, 0))],
        out_specs=[pl.BlockSpec((tm, n_w), lambda i: (i, 0)),
                   pl.BlockSpec((1, PROJ_CHUNK, tm), lambda i: (i // per_seq, 0, i % per_seq)),
                   pl.BlockSpec((1, 4, V_ROWS_128, tm), lambda i: (i // per_seq, 0, 0, i % per_seq))],
        out_shape=[jax.ShapeDtypeStruct((T, n_w), BF16),
                   jax.ShapeDtypeStruct((B, PROJ_CHUNK, S), BF16),
                   jax.ShapeDtypeStruct((B, 4, V_ROWS_128, S), BF16)],
        compiler_params=_params(1),
        name="even_proj",
    )(x2d, g.reshape(1, D), w_bf16)


def _odd_proj_kernel(x_ref, g_ref, w_ref, pos_ref, onehot_ref,
                     q_ref, cin_ref, ks_ref, kw_ref, vst_ref, vwt_ref, gate_ref):
    tm = x_ref.shape[0]
    G, Dh = NSA_GROUPS, HEAD_DIM
    kv = G * Dh
    xn = _rms(x_ref[...], g_ref[...]).astype(BF16)
    tail = _ones_row_tail(V_ROWS_64 - Dh, tm)
    pos = pos_ref[...]
    onehot = onehot_ref[...]
    q_width = q_ref.shape[1]
    for c in range(q_width // PROJ_CHUNK):
        sl = slice(c * PROJ_CHUNK, (c + 1) * PROJ_CHUNK)
        q_ref[:, sl] = (_dot(xn, w_ref[:, sl]) * Q_SCALE).astype(BF16)

    def pair(j):
        return _dot(xn, w_ref[:, q_width + j * 2 * kv:q_width + (j + 1) * 2 * kv])

    res = pair(0)
    for which in range(2):
        for g in range(G):
            cin_ref[which, 0, g] = res[:, which * kv + g * Dh:which * kv + (g + 1) * Dh].astype(BF16)

    for j, k_ref, vt_ref, extra in ((1, ks_ref, vst_ref, [pos, onehot]), (2, kw_ref, vwt_ref, [pos])):
        res = pair(j)
        v_t = res[:, kv:2 * kv].T.astype(BF16)
        for g in range(G):
            k_ref[0, g] = jnp.concatenate([res[:, g * Dh:(g + 1) * Dh].astype(BF16)] + extra, axis=1)
            vt_ref[0, g, 0:Dh, :] = v_t[g * Dh:(g + 1) * Dh]
            vt_ref[0, g, Dh:V_ROWS_64, :] = tail

    logits = _dot(xn, w_ref[:, q_width + 6 * kv:q_width + 6 * kv + GATE_PAD])
    gate_ref[0] = jax.nn.sigmoid(logits).T[0:gate_ref.shape[1]]


def odd_proj(x2d, B, S, g, w_bf16, pos_cols, onehot, tm=512):
    T, D = x2d.shape
    G, Dh = NSA_GROUPS, HEAD_DIM
    q_width = G * NSA_HPG * Dh
    n_sel = onehot.shape[1]
    per_seq = S // tm
    seq_tile = lambda i: (i // per_seq, 0, i % per_seq, 0)
    seq_tile_t = lambda i: (i // per_seq, 0, 0, i % per_seq)
    return pl.pallas_call(
        _odd_proj_kernel,
        grid=(T // tm,),
        in_specs=[pl.BlockSpec((tm, D), lambda i: (i, 0)),
                  pl.BlockSpec((1, D), lambda i: (0, 0)),
                  pl.BlockSpec(w_bf16.shape, lambda i: (0, 0)),
                  pl.BlockSpec((tm, Dh), lambda i: (i % per_seq, 0)),
                  pl.BlockSpec((tm, n_sel), lambda i: (i % per_seq, 0))],
        out_specs=[pl.BlockSpec((tm, q_width), lambda i: (i, 0)),
                   pl.BlockSpec((2, 1, G, tm, Dh), lambda i: (0, i // per_seq, 0, i % per_seq, 0)),
                   pl.BlockSpec((1, G, tm, 2 * Dh + n_sel), seq_tile),
                   pl.BlockSpec((1, G, tm, 2 * Dh), seq_tile),
                   pl.BlockSpec((1, G, V_ROWS_64, tm), seq_tile_t),
                   pl.BlockSpec((1, G, V_ROWS_64, tm), seq_tile_t),
                   pl.BlockSpec((1, 4 * NSA_HPG * G, tm), lambda i: (i // per_seq, 0, i % per_seq))],
        out_shape=[jax.ShapeDtypeStruct((T, q_width), BF16),
                   jax.ShapeDtypeStruct((2, B, G, S, Dh), BF16),
                   jax.ShapeDtypeStruct((B, G, S, 2 * Dh + n_sel), BF16),
                   jax.ShapeDtypeStruct((B, G, S, 2 * Dh), BF16),
                   jax.ShapeDtypeStruct((B, G, V_ROWS_64, S), BF16),
                   jax.ShapeDtypeStruct((B, G, V_ROWS_64, S), BF16),
                   jax.ShapeDtypeStruct((B, 4 * NSA_HPG * G, S), F32)],
        compiler_params=_params(1),
        name="odd_proj",
    )(x2d, g.reshape(1, D), w_bf16, pos_cols, onehot)


def _head_pair_rows(q, t):
    lane = lax.broadcasted_iota(jnp.int32, (t, 2 * HEAD_DIM), 1)
    zero = jnp.zeros_like(q)
    return jnp.where(lane < HEAD_DIM, q, zero), jnp.where(lane >= HEAD_DIM, q, zero)


def _sb_kernel(q_ref, k_ref, vt_ref, o_ref, acc_ref, carry_ref, *, t):
    qi = pl.program_id(2)
    cols = 2 * t
    q_both = jnp.concatenate(_head_pair_rows(q_ref[0], t), axis=0)
    s_idx = lax.broadcasted_iota(jnp.int32, (t, 2 * t), 0)
    j_idx = lax.broadcasted_iota(jnp.int32, (t, 2 * t), 1) % t
    upper2 = jnp.where(j_idx > s_idx, 1.0, 0.0).astype(BF16)
    key = lax.broadcasted_iota(jnp.int32, (t, cols), 0)
    qry = lax.broadcasted_iota(jnp.int32, (t, cols), 1) % t
    past_diag = key < qry

    acc_ref[...] = jnp.zeros_like(acc_ref)
    carry_ref[...] = jnp.zeros_like(carry_ref)

    def chunks(specs):
        starts = [pl.multiple_of(kc * t, t) for kc, _ in specs]
        z_all = [_dot_nt(k_ref[0, pl.ds(start, t), :], q_both) for start in starts]
        carry = carry_ref[...]
        acc = acc_ref[...]
        for (_, masked), start, z in zip(specs, starts, z_all):
            drop = jnp.maximum(z, 0.0) + jnp.log2(1.0 + jnp.exp2(jnp.abs(z) * -1.0))
            log_beta = z - drop
            if masked:
                drop = jnp.where(past_diag, drop, 0.0)
            hi, lo = _split_bf16(drop)
            tail = _dot(upper2, jnp.concatenate([hi, lo], axis=0))
            w = jnp.exp2(log_beta - tail - carry)
            if masked:
                w = jnp.where(past_diag, w, 0.0)
            carry = carry + tail[0:1] + drop[0:1]
            acc = acc + _dot(vt_ref[0, :, pl.ds(start, t)], w.astype(BF16))
        carry_ref[...] = carry
        acc_ref[...] = acc
        return jnp.min(carry)

    least = lax.cond(qi > 0, lambda: chunks([(qi, True), (qi - 1, False)]), lambda: chunks([(qi, True)]))

    def cond(state):
        j, least_carry = state
        return (j < qi) & (least_carry < -UNDERFLOW_LOG2)

    def body(state):
        j, _ = state
        return j + 1, chunks([(qi - 1 - j, False)])

    lax.while_loop(cond, body, (1, least))
    row = lax.broadcasted_iota(jnp.int32, (2 * HEAD_DIM, t), 0)
    o_t = jnp.where(row < HEAD_DIM, acc_ref[:, :t], acc_ref[:, t:])
    o_ref[0] = o_t.T.astype(o_ref.dtype)


def sb_attention(proj, v_t, t=256):
    B, S, _ = proj.shape
    n_pairs = 4
    return pl.pallas_call(
        functools.partial(_sb_kernel, t=t),
        grid=(B, n_pairs, S // t),
        in_specs=[pl.BlockSpec((1, t, 128), lambda b, p, i: (b, i, p)),
                  pl.BlockSpec((1, S, 128), lambda b, p, i: (b, 0, n_pairs + p)),
                  pl.BlockSpec((1, 128, S), lambda b, p, i: (b, p, 0))],
        out_specs=pl.BlockSpec((1, t, 128), lambda b, p, i: (b, i, p)),
        out_shape=jax.ShapeDtypeStruct((B, S, 2 * n_pairs * 128), BF16),
        scratch_shapes=[pltpu.VMEM((128, 2 * t), F32), pltpu.VMEM((1, 2 * t), F32)],
        compiler_params=_params(3),
        name="sb_attention",
    )(proj, proj, v_t)


def _diff_kernel(slopes_ref, lam_ref, q_ref, k_ref, vt_ref, pos_ref, sl_ref, subln_ref, _, o_ref,
                 m_ref, acc_ref, offset_ref, knorm_ref, *, t, out_scale, lambda_init):
    h = pl.program_id(1)
    qi = pl.program_id(2)
    slope = slopes_ref[h]
    cols = 2 * t
    slope_cols = jnp.broadcast_to(sl_ref[0], (t, 128))
    q_parts = _head_pair_rows(q_ref[0], t)
    q_both = jnp.concatenate([jnp.concatenate([qc, slope_cols], axis=1) for qc in q_parts],
                             axis=0)
    lane = lax.broadcasted_iota(jnp.int32, (t, 2 * HEAD_DIM), 1)

    @pl.when(qi == 0)
    def _():
        def body(c, best):
            k = k_ref[0, pl.ds(pl.multiple_of(c * t, t), t), :].astype(F32)
            k2 = k * k
            return (jnp.maximum(best[0], jnp.max(jnp.sum(jnp.where(lane < HEAD_DIM, k2, 0.0), axis=1))),
                    jnp.maximum(best[1], jnp.max(jnp.sum(jnp.where(lane >= HEAD_DIM, k2, 0.0), axis=1))))

        best = lax.fori_loop(0, k_ref.shape[1] // t, body, (jnp.float32(0.0), jnp.float32(0.0)))
        knorm_ref[0] = best[0]
        knorm_ref[1] = best[1]
        offset_ref[...] = (lax.broadcasted_iota(jnp.int32, (t, cols), 0)
                           - lax.broadcasted_iota(jnp.int32, (t, cols), 1) % t)

    q32 = [qc.astype(F32) for qc in q_parts]
    col = lax.broadcasted_iota(jnp.int32, (1, cols), 1)
    k_norm2 = jnp.where(col < t, knorm_ref[0], knorm_ref[1])
    qk_bound = jnp.sqrt(jnp.concatenate([_ones_row_sums(x * x) for x in q32], axis=1) * k_norm2 * 1.05)
    k_self = k_ref[0, pl.ds(pl.multiple_of(qi * t, t), t), :].astype(F32)
    self_score = jnp.concatenate([_ones_row_sums(x * k_self) for x in q32], axis=1)
    safe = jnp.max(qk_bound - self_score) <= SAFE_EXP_LOG2

    pos = pos_ref[...]

    def step(kc, diagonal=False):
        start = pl.multiple_of(kc * t, t)

        def scores():
            s_t = _dot_nt(jnp.concatenate([k_ref[0, pl.ds(start, t), :], pos], axis=1), q_both)
            return jnp.where(offset_ref[...] <= 0, s_t, NEG_INF) if diagonal else s_t

        return scores, slope * ((kc - qi) * t).astype(F32), lambda: vt_ref[0, 0, :, pl.ds(start, t)], 0

    acc_ref[...] = jnp.zeros_like(acc_ref)

    @pl.when(safe)
    def _():
        frame = qk_bound + slope * (col % t).astype(F32)
        block = lambda steps: _fixed_frame_block(steps, [acc_ref], [frame])
        block([step(qi, True)])
        n_back = jnp.ceil((slope * (t - 1) - UNDERFLOW_LOG2) / (slope * t)).astype(jnp.int32)
        _chunk_loop(jnp.minimum(n_back, qi), lambda i: step(qi - 1 - i), block, (8, 4, 2, 1))

    @pl.when(jnp.logical_not(safe))
    def _():
        m_ref[...] = jnp.full_like(m_ref, NEG_INF)
        block = lambda steps: _softmax_block(steps, [(m_ref, acc_ref)])
        block([step(qi, True)])
        excess = jnp.max(qk_bound - m_ref[...]) + slope * (t - 1)
        n_back = jnp.clip(jnp.ceil((excess - UNDERFLOW_LOG2) / (slope * t)).astype(jnp.int32), 0, qi)
        _chunk_loop(n_back, lambda i: step(qi - 1 - i), block, (4, 2, 1))

    lam_terms = lam_ref[...]
    lam = (jnp.exp(jnp.sum(lam_terms[0:1] * lam_terms[1:2], axis=1, keepdims=True))
           - jnp.exp(jnp.sum(lam_terms[2:3] * lam_terms[3:4], axis=1, keepdims=True))
           + lambda_init)
    o_t = acc_ref[0:128, :] / acc_ref[128:129, :]
    o = (o_t[:, :t] - lam * o_t[:, t:]).T
    o_ref[0] = (_rms(o, subln_ref[...]) * out_scale).astype(o_ref.dtype)


def diff_attention(proj, vt_aug, lam_rows, subln, slopes, layer, mix, t=512):
    B, S, _ = proj.shape
    n_heads = 4
    lambda_init = 0.8 - 0.6 * math.exp(-0.3 * layer)
    smem = pl.BlockSpec(memory_space=pltpu.SMEM)
    pos = _key_position_columns(t, t, 128)
    slope_cols = _slope_pieces(slopes, 128).reshape(n_heads, 1, 128)
    return pl.pallas_call(
        functools.partial(_diff_kernel, t=t, out_scale=1.0 - lambda_init, lambda_init=lambda_init),
        grid=(B, n_heads, S // t),
        in_specs=[smem,
                  pl.BlockSpec((4, HEAD_DIM), lambda b, h, i: (0, 0)),
                  pl.BlockSpec((1, t, 128), lambda b, h, i: (b, i, 12 + h)),
                  pl.BlockSpec((1, S, 128), lambda b, h, i: (b, 0, 16 + h)),
                  pl.BlockSpec((1, 1, V_ROWS_128, S), lambda b, h, i: (b, h, 0, 0)),
                  pl.BlockSpec((t, 128), lambda b, h, i: (0, 0)),
                  pl.BlockSpec((1, 1, 128), lambda b, h, i: (h, 0, 0)),
                  pl.BlockSpec((1, 128), lambda b, h, i: (0, 0)),
                  pl.BlockSpec(memory_space=pl.ANY)],
        out_specs=pl.BlockSpec((1, t, 128), lambda b, h, i: (b, i, n_heads + h)),
        out_shape=jax.ShapeDtypeStruct(mix.shape, mix.dtype),
        input_output_aliases={8: 0},
        scratch_shapes=[pltpu.VMEM((1, 2 * t), F32), pltpu.VMEM((V_ROWS_128, 2 * t), F32),
                        pltpu.VMEM((t, 2 * t), jnp.int32), pltpu.SMEM((2,), F32)],
        compiler_params=_params(2, 1),
        name="diff_attention",
    )(slopes, lam_rows, proj, proj, vt_aug, pos, slope_cols, subln.reshape(1, 128), mix)


def _post_kernel(*refs, ff_chunk, final):
    mix_ref, x_ref, wo_ref, g_ref, w1_ref, w2_ref = refs[:6]
    gf_ref = refs[6] if final else None
    o_ref = refs[-1]
    x = x_ref[...] + _dot(mix_ref[...], wo_ref[...])
    hn = _rms(x, g_ref[...]).astype(BF16)
    acc = x
    for f in range(w1_ref.shape[1] // ff_chunk):
        sl = slice(f * ff_chunk, (f + 1) * ff_chunk)
        hid = jnp.maximum(_dot(hn, w1_ref[:, sl]), 0.0)
        acc = acc + _dot((hid * hid).astype(BF16), w2_ref[sl, :])
    if final:
        acc = _rms(acc, gf_ref[...])
    o_ref[...] = acc


def post_block(mix, x2d, w_out, g_mlp, w1, w2, g_final=None, tm=512, ff_chunk=1024):
    T, D = x2d.shape
    final = g_final is not None
    const = lambda i: (0, 0)
    in_specs = [pl.BlockSpec((tm, mix.shape[1]), lambda i: (i, 0)),
                pl.BlockSpec((tm, D), lambda i: (i, 0)),
                pl.BlockSpec(w_out.shape, const),
                pl.BlockSpec((1, D), const), pl.BlockSpec(w1.shape, const), pl.BlockSpec(w2.shape, const)]
    args = [mix, x2d, w_out, g_mlp.reshape(1, D), w1, w2]
    if final:
        in_specs.append(pl.BlockSpec((1, D), const))
        args.append(g_final.reshape(1, D))
    return pl.pallas_call(
        functools.partial(_post_kernel, ff_chunk=ff_chunk, final=final),
        grid=(T // tm,),
        in_specs=in_specs,
        out_specs=pl.BlockSpec((tm, D), lambda i: (i, 0)),
        out_shape=jax.ShapeDtypeStruct((T, D), F32),
        compiler_params=_params(1),
        name="post_block",
    )(*args)


def _compress_kernel(c_ref, pos_ref, w1_ref, w2_ref, o_ref):
    half = w1_ref.shape[1] // 2
    chunks = c_ref[0, 0]
    pos = jnp.broadcast_to(pos_ref[0], (8, 2 * half)).astype(BF16)
    first = _dot(chunks, w1_ref[0, :half, :])
    second = _dot(chunks, w1_ref[0, half:, :])
    n_chunks = chunks.shape[0]
    pre = first + pltpu.roll(second, n_chunks - 1, 0) + _dot(pos, w1_ref[0])[0:1]
    hid = jax.nn.gelu(pre)
    o_ref[0, 0] = _dot(hid.astype(BF16), w2_ref[0]).astype(o_ref.dtype)


def compress_kv(chunks, pos_flat, w1, w2):
    _, BG, n_chunks, width = chunks.shape
    hidden = w1.shape[-1]
    return pl.pallas_call(
        _compress_kernel,
        grid=(2, BG),
        in_specs=[pl.BlockSpec((1, 1, n_chunks, width), lambda s, i: (s, i, 0, 0)),
                  pl.BlockSpec((1, 1, 2 * width), lambda s, i: (s, 0, 0)),
                  pl.BlockSpec((1, 2 * width, hidden), lambda s, i: (s, 0, 0)),
                  pl.BlockSpec((1, hidden, HEAD_DIM), lambda s, i: (s, 0, 0))],
        out_specs=pl.BlockSpec((1, 1, n_chunks, HEAD_DIM), lambda s, i: (s, i, 0, 0)),
        out_shape=jax.ShapeDtypeStruct((2, BG, n_chunks, HEAD_DIM), BF16),
        compiler_params=_params(2),
        name="compress_kv",
    )(chunks, pos_flat, w1, w2)


def _stack_heads(q):
    return jnp.concatenate([q[:, r * HEAD_DIM:(r + 1) * HEAD_DIM] for r in range(NSA_HPG)], axis=0)


def _slope_row(slopes_ref, g, tq):
    col = lax.broadcasted_iota(jnp.int32, (1, NSA_HPG * tq), 1)
    out = jnp.zeros((1, NSA_HPG * tq), F32)
    for r in range(NSA_HPG):
        out = jnp.where(col // tq == r, slopes_ref[g * NSA_HPG + r], out)
    return out


def _cmp_select_kernel(slopes_ref, q_ref, kc_ref, vct_ref, ovt_ref, ocmp_ref, sel_ref, hits_ref,
                       *, tq, n_sel, blocks_per_chunk, hit_tile):
    g = pl.program_id(1)
    t0 = pl.program_id(2) * tq
    cols = NSA_HPG * tq
    n_cmp = kc_ref.shape[2]
    slope_row = _slope_row(slopes_ref, g, tq)
    q_rows = _stack_heads(q_ref[0])
    tpos = t0 + lax.broadcasted_iota(jnp.int32, (n_cmp, cols), 1) % tq
    cmp_end = lax.broadcasted_iota(jnp.int32, (n_cmp, cols), 0) * CMP_STRIDE + (CMP_LEN - 1)
    dc = (tpos - cmp_end).astype(F32)
    sc = jnp.where(dc >= 0, _dot_nt(kc_ref[0, 0], q_rows) - slope_row * dc, NEG_INF)
    e = jnp.exp2(sc - jnp.max(sc, axis=0, keepdims=True))
    any_valid = jnp.where(dc[0:1] >= 0, 1.0, 0.0)
    pc = e * (any_valid / jnp.sum(e, axis=0, keepdims=True))
    o_t = _dot(vct_ref[0, 0], pc.astype(BF16))
    for r in range(NSA_HPG):
        ocmp_ref[0, 0, r] = o_t[:, r * tq:(r + 1) * tq]

    pc_group = pc[:, 0:tq]
    for r in range(1, NSA_HPG):
        pc_group = pc_group + pc[:, r * tq:(r + 1) * tq]
    hi, lo = _split_bf16(pc_group)
    imp = _dot(ovt_ref[...], hi) + _dot(ovt_ref[...], lo)

    blk = lax.broadcasted_iota(jnp.int32, (n_sel, tq), 0)
    cur = (t0 + lax.broadcasted_iota(jnp.int32, (n_sel, tq), 1)) // SEL_LEN
    forced = (blk == 0) | (blk == cur) | (blk == cur - 1)
    imp = jnp.where(forced, FORCE_SCORE, imp)
    imp = jnp.where(blk <= cur, imp, -1.0)
    picked = jnp.zeros((n_sel, tq), F32)
    for _ in range(min(SEL_TOPK, n_sel)):
        best = jnp.max(imp, axis=0, keepdims=True)
        first = jnp.min(jnp.where(imp == best, blk, n_sel), axis=0, keepdims=True)
        hit = blk == first
        picked = jnp.where(hit, 1.0, picked)
        imp = jnp.where(hit, -2.0, imp)
    keep = (picked > 0.5) & (blk <= cur)
    sel_ref[0, 0] = jnp.where(keep, 0.0, -SEL_DROP).T.astype(sel_ref.dtype)
    n_chunks = n_sel // blocks_per_chunk
    member = (lax.broadcasted_iota(jnp.int32, (n_chunks, n_sel), 1) // blocks_per_chunk
              == lax.broadcasted_iota(jnp.int32, (n_chunks, n_sel), 0))
    per_query = _dot(jnp.where(member, 1.0, 0.0).astype(BF16),
                     jnp.where(keep, 1.0, 0.0).astype(BF16)).astype(BF16)
    for part in range(tq // hit_tile):
        hits_ref[0, 0, part] = _dot(per_query[:, part * hit_tile:(part + 1) * hit_tile],
                                    jnp.ones((hit_tile, 128), BF16))


def cmp_select(q, kc, vc_t, overlap_t, slopes, tk, hit_tile, tq=256):
    B, S, _ = q.shape
    n_cmp = kc.shape[2]
    n_sel = S // SEL_LEN
    G = NSA_GROUPS
    smem = pl.BlockSpec(memory_space=pltpu.SMEM)
    return pl.pallas_call(
        functools.partial(_cmp_select_kernel, tq=tq, n_sel=n_sel, blocks_per_chunk=tk // SEL_LEN,
                          hit_tile=hit_tile),
        grid=(B, G, S // tq),
        in_specs=[smem,
                  pl.BlockSpec((1, tq, 256), lambda b, g, i: (b, i, g)),
                  pl.BlockSpec((1, 1, n_cmp, HEAD_DIM), lambda b, g, i: (b, g, 0, 0)),
                  pl.BlockSpec((1, 1, HEAD_DIM, n_cmp), lambda b, g, i: (b, g, 0, 0)),
                  pl.BlockSpec((n_sel, n_cmp), lambda b, g, i: (0, 0))],
        out_specs=[pl.BlockSpec((1, 1, NSA_HPG, HEAD_DIM, tq), lambda b, g, i: (b, g, 0, 0, i)),
                   pl.BlockSpec((1, 1, tq, n_sel), lambda b, g, i: (b, g, i, 0)),
                   pl.BlockSpec((1, 1, tq // hit_tile, S // tk, 128), lambda b, g, i: (b, g, i, 0, 0))],
        out_shape=[jax.ShapeDtypeStruct((B, G, NSA_HPG, HEAD_DIM, S), F32),
                   jax.ShapeDtypeStruct((B, G, S, n_sel), BF16),
                   jax.ShapeDtypeStruct((B, G, S // hit_tile, S // tk, 128), F32)],
        compiler_params=_params(3),
        name="cmp_select",
    )(slopes, q, kc, vc_t, overlap_t)


def _sel_win_kernel(slopes_ref, active_ref, q_ref, sl_ref, ks_ref, vst_ref, kw_ref, vwt_ref, sel_ref,
                    ocmp_ref, gate_ref, o_ref, ms_ref, accs_ref, mw_ref, accw_ref, offset_ref, todo_ref,
                    knorm_ref, *, tq, tk):
    g = pl.program_id(1)
    t0 = pl.program_id(2) * tq
    cols = NSA_HPG * tq
    slope_row = _slope_row(slopes_ref, g, tq)
    q = q_ref[0]
    q_aug = jnp.concatenate(
        [jnp.concatenate([q[:, r * HEAD_DIM:(r + 1) * HEAD_DIM],
                          jnp.broadcast_to(sl_ref[0, r:r + 1, :], (tq, HEAD_DIM))], axis=1)
         for r in range(NSA_HPG)], axis=0)
    q_sel = jnp.concatenate([q_aug, jnp.concatenate([sel_ref[0, 0]] * NSA_HPG, axis=0)], axis=1)

    @pl.when(pl.program_id(2) == 0)
    def _():
        offset_ref[...] = (lax.broadcasted_iota(jnp.int32, (tk, cols), 0)
                           - lax.broadcasted_iota(jnp.int32, (tk, cols), 1) % tq)

        def body(c, best):
            rows = pl.ds(pl.multiple_of(c * tk, tk), tk)
            ks = ks_ref[0, 0, rows, :][:, :HEAD_DIM].astype(F32)
            kw = kw_ref[0, 0, rows, :][:, :HEAD_DIM].astype(F32)
            return (jnp.maximum(best[0], jnp.max(jnp.sum(ks * ks, axis=1))),
                    jnp.maximum(best[1], jnp.max(jnp.sum(kw * kw, axis=1))))

        best = lax.fori_loop(0, ks_ref.shape[2] // tk, body, (jnp.float32(0.0), jnp.float32(0.0)))
        knorm_ref[0] = best[0]
        knorm_ref[1] = best[1]

    heads32 = [q.astype(F32)[:, r * HEAD_DIM:(r + 1) * HEAD_DIM] for r in range(NSA_HPG)]
    q_norm2 = jnp.concatenate([_ones_row_sums(x * x) for x in heads32], axis=1)
    own_rows = pl.ds(pl.multiple_of(t0, tq), tq)

    def bound_and_self(k_ref, k_norm2):
        k_self = k_ref[0, 0, own_rows, :][:, :HEAD_DIM].astype(F32)
        own = jnp.concatenate([_ones_row_sums(x * k_self) for x in heads32], axis=1)
        return jnp.sqrt(q_norm2 * k_norm2 * 1.05), own

    sel_bound, sel_self = bound_and_self(ks_ref, knorm_ref[0])
    win_bound, win_self = bound_and_self(kw_ref, knorm_ref[1])
    safe = jnp.maximum(jnp.max(sel_bound - sel_self), jnp.max(win_bound - win_self)) <= SAFE_EXP_LOG2

    def step(c, k_ref, q_rows, vt_ref, keep=None):
        inside = c >= 0
        start = pl.multiple_of(jnp.maximum(c, 0) * tk, tk)
        shift = c * tk - t0

        def scores():
            s_t = _dot_nt(k_ref[0, 0, pl.ds(start, tk), :], q_rows)
            if keep is None:
                return s_t
            lo, hi = keep
            offset = offset_ref[...]
            if hi is not None:
                return jnp.where(offset <= jnp.where(inside, hi - shift, -FAR), s_t, NEG_INF)
            return jnp.where(offset > jnp.where(inside, lo - shift, FAR), s_t, NEG_INF)

        kappa = jnp.where(inside, slope_row * shift.astype(F32), NEG_INF)
        return scores, kappa, lambda: vt_ref[0, 0, :, pl.ds(start, tk)]

    last = t0 // tk
    causal, recent, anything = (None, 0), (-WINDOW, None), (-FAR, None)
    first_steps = [step(last - 2, kw_ref, q_aug, vwt_ref, recent) + (1,),
                   step(last, ks_ref, q_sel, vst_ref, causal) + (0,),
                   step(last - 1, kw_ref, q_aug, vwt_ref, anything) + (1,),
                   step(last, kw_ref, q_aug, vwt_ref, causal) + (1,)]

    def note_active(c, n):
        hit = active_ref[0, 0, 0, 0, c] > 0

        @pl.when(hit)
        def _():
            todo_ref[n] = c

        return n + hit.astype(jnp.int32)

    n_todo = lax.fori_loop(0, last, note_active, 0)
    sel_step = lambda i: step(todo_ref[i], ks_ref, q_sel, vst_ref) + (0,)

    accs_ref[...] = jnp.zeros_like(accs_ref)
    accw_ref[...] = jnp.zeros_like(accw_ref)

    @pl.when(safe)
    def _():
        in_tile = slope_row * (lax.broadcasted_iota(jnp.int32, (1, cols), 1) % tq).astype(F32)
        block = lambda steps: _fixed_frame_block(steps, [accs_ref, accw_ref],
                                                 [sel_bound + in_tile, win_bound + in_tile])
        block(first_steps)
        _chunk_loop(n_todo, sel_step, block, (8, 4, 2, 1))

    @pl.when(jnp.logical_not(safe))
    def _():
        states = [(ms_ref, accs_ref), (mw_ref, accw_ref)]
        for m_ref, _ in states:
            m_ref[...] = jnp.full_like(m_ref, NEG_INF)
        block = lambda steps: _softmax_block(steps, states)
        block(first_steps)
        _chunk_loop(n_todo, sel_step, block, (4, 2, 1))

    o_sel = accs_ref[0:HEAD_DIM, :] / accs_ref[HEAD_DIM:HEAD_DIM + 1, :]
    o_win = accw_ref[0:HEAD_DIM, :] / accw_ref[HEAD_DIM:HEAD_DIM + 1, :]

    gates = gate_ref[0]
    outs = []
    for r in range(NSA_HPG):
        cs = slice(r * tq, (r + 1) * tq)
        outs.append(gates[3 * r:3 * r + 1] * ocmp_ref[0, 0, r] + gates[3 * r + 1:3 * r + 2] * o_sel[:, cs]
                    + gates[3 * r + 2:3 * r + 3] * o_win[:, cs])
    o_ref[0] = jnp.concatenate(outs, axis=0).T.astype(o_ref.dtype)


def sel_win_attention(q, active, slope_cols, ks, vs_t, kw, vw_t, sel_bias, o_cmp, gates_t,
                      slopes, tq=256, tk=256):
    B, S, _ = q.shape
    assert tq == tk and WINDOW == 2 * tk
    G = NSA_GROUPS
    n_sel = S // SEL_LEN
    cols = NSA_HPG * tq
    smem = pl.BlockSpec(memory_space=pltpu.SMEM)
    ks_spec = pl.BlockSpec((1, 1, S, ks.shape[-1]), lambda b, g, i: (b, g, 0, 0))
    kw_spec = pl.BlockSpec((1, 1, S, kw.shape[-1]), lambda b, g, i: (b, g, 0, 0))
    v_spec = pl.BlockSpec((1, 1, V_ROWS_64, S), lambda b, g, i: (b, g, 0, 0))
    return pl.pallas_call(
        functools.partial(_sel_win_kernel, tq=tq, tk=tk),
        grid=(B, G, S // tq),
        in_specs=[smem,
                  pl.BlockSpec((1, 1, 1, 1, S // tk), lambda b, g, i: (b, g, i, 0, 0),
                               memory_space=pltpu.SMEM),
                  pl.BlockSpec((1, tq, 256), lambda b, g, i: (b, i, g)),
                  pl.BlockSpec((1, NSA_HPG, HEAD_DIM), lambda b, g, i: (g, 0, 0)),
                  ks_spec, v_spec, kw_spec, v_spec,
                  pl.BlockSpec((1, 1, tq, n_sel), lambda b, g, i: (b, g, i, 0)),
                  pl.BlockSpec((1, 1, NSA_HPG, HEAD_DIM, tq), lambda b, g, i: (b, g, 0, 0, i)),
                  pl.BlockSpec((1, 4 * NSA_HPG, tq), lambda b, g, i: (b, g, i))],
        out_specs=pl.BlockSpec((1, tq, 256), lambda b, g, i: (b, i, g)),
        out_shape=jax.ShapeDtypeStruct((B, S, G * 256), BF16),
        scratch_shapes=[pltpu.VMEM((1, cols), F32), pltpu.VMEM((V_ROWS_64, cols), F32),
                        pltpu.VMEM((1, cols), F32), pltpu.VMEM((V_ROWS_64, cols), F32),
                        pltpu.VMEM((tk, cols), jnp.int32),
                        pltpu.SMEM((S // tk,), jnp.int32), pltpu.SMEM((2,), F32)],
        compiler_params=_params(2, 1),
        name="sel_win_attention",
    )(slopes, active, q, slope_cols, ks, vs_t, kw, vw_t, sel_bias, o_cmp, gates_t)


def _alibi_slopes_log2(n_heads):
    slopes = np.exp2(-8.0 * (np.arange(n_heads, dtype=np.float32) + 1.0) / n_heads)
    return (slopes.astype(np.float32) * np.float32(LOG2E)).astype(np.float32)


def even_layer_mix(x2d, B, S, norm_g, w_in, lam_q1, lam_k1, lam_q2, lam_k2, subln, layer):
    proj, sb_vt, df_vt = even_proj(x2d, B, S, norm_g, w_in.astype(BF16))
    proj = proj.reshape(B, S, -1)
    lam_rows = jnp.stack([lam_q1, lam_k1, lam_q2, lam_k2]).astype(F32)
    mix = diff_attention(proj, df_vt, lam_rows, subln.astype(F32), _alibi_slopes_log2(4), layer,
                         sb_attention(proj, sb_vt))
    return mix.reshape(B * S, -1)


def odd_layer_mix(x2d, B, S, norm_g, w_in, pos_k, k_w1, k_w2, pos_v, v_w1, v_w2, tq=256, tk=256):
    G, Dh = NSA_GROUPS, HEAD_DIM
    q_width = G * NSA_HPG * Dh
    n_main = q_width + 6 * G * Dh
    per_group = NSA_HPG * N_GATES
    w_gate = w_in[:, n_main:n_main + G * per_group].reshape(-1, G, per_group)
    w_gate = jnp.pad(w_gate, ((0, 0), (0, 0), (0, 4 * NSA_HPG - per_group))).reshape(-1, 4 * NSA_HPG * G)
    w_gate = jnp.pad(w_gate, ((0, 0), (0, GATE_PAD - w_gate.shape[1])))
    w_all = jnp.concatenate([w_in[:, :n_main], w_gate], axis=1).astype(BF16)

    n_chunks = S // CMP_STRIDE
    n_sel = S // SEL_LEN
    onehot = (jnp.arange(S)[:, None] // SEL_LEN == jnp.arange(n_sel)[None, :]).astype(BF16)
    q, cmp_in, ks, kw, vs_t, vw_t, gates_t = odd_proj(x2d, B, S, norm_g, w_all,
                                                      _key_position_columns(S, tk, Dh), onehot)
    q = q.reshape(B, S, q_width)

    pos_flat = jnp.stack([pos_k, pos_v]).reshape(2, 1, CMP_LEN * Dh).astype(F32)
    w1 = jnp.stack([k_w1, v_w1]).astype(BF16)
    w2 = jnp.stack([k_w2, v_w2]).astype(BF16)
    cmp = compress_kv(cmp_in.reshape(2, B * G, n_chunks, CMP_STRIDE * Dh), pos_flat, w1, w2)
    cmp = cmp.reshape(2, B, G, n_chunks, Dh)

    cmp_start = jnp.arange(n_chunks) * CMP_STRIDE
    sel_start = jnp.arange(n_sel) * SEL_LEN
    overlap_t = ((cmp_start[None, :] < sel_start[:, None] + SEL_LEN)
                 & (sel_start[:, None] <= cmp_start[None, :] + CMP_LEN - 1)).astype(BF16)
    slopes = _alibi_slopes_log2(G * NSA_HPG)
    slope_cols = _slope_pieces(slopes, Dh).reshape(G, NSA_HPG, Dh)

    o_cmp, sel_bias, hits = cmp_select(q, cmp[0], cmp[1].transpose(0, 1, 3, 2), overlap_t, slopes, tk, tq)
    active = (hits[..., 0] > 0).astype(jnp.int32)[:, :, :, None, :]
    o = sel_win_attention(q, active, slope_cols, ks, vs_t, kw, vw_t, sel_bias, o_cmp, gates_t, slopes,
                          tq=tq, tk=tk)
    return o.reshape(B * S, q_width)


def kernel(x, attn_norm, mlp_norm, final_norm, ev_w_in, ev_lam_q1, ev_lam_k1, ev_lam_q2, ev_lam_k2,
           ev_subln, ev_w_out, od_w_in, od_cmp_pos_k, od_cmp_k_w1, od_cmp_k_w2, od_cmp_pos_v,
           od_cmp_v_w1, od_cmp_v_w2, od_w_out, mlp_w1, mlp_w2):
    B, S, D = x.shape
    depth = attn_norm.shape[0]
    x2d = x.reshape(B * S, D)
    for layer in range(depth):
        idx = layer // 2
        if layer % 2 == 0:
            mix = even_layer_mix(x2d, B, S, attn_norm[layer], ev_w_in[idx], ev_lam_q1[idx],
                                 ev_lam_k1[idx], ev_lam_q2[idx], ev_lam_k2[idx], ev_subln[idx], layer)
            w_out = ev_w_out[idx]
        else:
            mix = odd_layer_mix(x2d, B, S, attn_norm[layer], od_w_in[idx], od_cmp_pos_k[idx],
                                od_cmp_k_w1[idx], od_cmp_k_w2[idx], od_cmp_pos_v[idx],
                                od_cmp_v_w1[idx], od_cmp_v_w2[idx])
            w_out = od_w_out[idx]
        g_final = final_norm if layer == depth - 1 else None
        x2d = post_block(mix, x2d, w_out.astype(BF16), mlp_norm[layer], mlp_w1[layer].astype(BF16),
                         mlp_w2[layer].astype(BF16), g_final)
    return x2d.reshape(B, S, D)
```

```python
import functools
import math

import jax
import jax.numpy as jnp
import numpy as np
from jax import lax
from jax.experimental import pallas as pl
from jax.experimental.pallas import tpu as pltpu

F32 = jnp.float32
BF16 = jnp.bfloat16

HEAD_DIM = 64
RMS_EPS = 1e-6
NEG_INF = -1e30
FORCE_SCORE = 1e6
NSA_GROUPS = 4
NSA_HPG = 4
CMP_LEN = 32
CMP_STRIDE = 16
SEL_LEN = 64
SEL_TOPK = 16
WINDOW = 512
N_GATES = 3
GATE_PAD = 128
SEL_DROP = 2.0 ** 24
BF16_EXACT_INT = 256
V_ROWS_64 = 80
V_ROWS_128 = 144

LOG2E = math.log2(math.e)
Q_SCALE = HEAD_DIM ** -0.5 * LOG2E
UNDERFLOW_LOG2 = -160.0
SAFE_EXP_LOG2 = 100.0
FAR = 1 << 30
PICKED = -2.0

VMEM_LIMIT = 56 * 1024 * 1024


def _params(n_parallel, n_arbitrary=0):
    return pltpu.CompilerParams(dimension_semantics=("parallel",) * n_parallel + ("arbitrary",) * n_arbitrary,
                                vmem_limit_bytes=VMEM_LIMIT)


def _rms(x, g):
    ms = jnp.mean(x * x, axis=-1, keepdims=True)
    return x * lax.rsqrt(ms + RMS_EPS) * g


def _dot(a, b):
    return jnp.dot(a, b, preferred_element_type=F32)


def _dot_nt(a, b):
    return lax.dot_general(a, b, (((1,), (1,)), ((), ())), preferred_element_type=F32)


def _split_bf16(x):
    hi = x.astype(BF16)
    lo = (x - hi.astype(F32)).astype(BF16)
    return hi, lo


def _slope_pieces(slopes, width):
    def top_bits(x):
        return (x.view(np.uint32) & np.uint32(0xFFFF0000)).view(np.float32)

    s1 = top_bits(slopes)
    r1 = slopes - s1
    s2 = top_bits(r1)
    s3 = top_bits(r1 - s2)
    out = np.zeros((slopes.shape[0], width), np.float32)
    out[:, :6] = np.stack([s1, s2, s3, s1, s2, s3], axis=1)
    return jnp.asarray(out).astype(BF16)


def _key_position_columns(n, tk, width):
    j = jnp.arange(n) % tk
    a = (j // BF16_EXACT_INT) * BF16_EXACT_INT
    b = j % BF16_EXACT_INT
    cols = jnp.stack([a, a, a, b, b, b], axis=1).astype(BF16)
    return jnp.pad(cols, ((0, 0), (0, width - cols.shape[1])))


def _softmax_block(steps, states):
    s_all = [scores() for scores, _, _, _ in steps]
    live = {k: (states[k][0][...], states[k][1][...]) for k in sorted({k for _, _, _, k in steps})}
    for (_, kap, values, k), s_t in zip(steps, s_all):
        m_run, acc = live[k]
        m_new = jnp.maximum(m_run, jnp.max(s_t, axis=0, keepdims=True) + kap)
        p = jnp.exp2(s_t - (m_new - kap)).astype(BF16)
        live[k] = (m_new, jnp.exp2(m_run - m_new) * acc + _dot(values(), p))
    for k, (m_run, acc) in live.items():
        states[k][0][...] = m_run
        states[k][1][...] = acc


def _fixed_frame_block(steps, acc_refs, frames):
    s_all = [scores() for scores, _, _, _ in steps]
    live = {k: acc_refs[k][...] for k in sorted({k for _, _, _, k in steps})}
    for (_, kap, values, k), s_t in zip(steps, s_all):
        live[k] = live[k] + _dot(values(), jnp.exp2(s_t - (frames[k] - kap)).astype(BF16))
    for k, acc in live.items():
        acc_refs[k][...] = acc


def _chunk_loop(n_steps, step, block, groups):
    done = 0
    for group in groups:
        def body(i, carry, group=group, done=done):
            block([step(done + group * i + j) for j in range(group)])
            return carry

        n_groups = (n_steps - done) // group
        lax.fori_loop(0, n_groups, body, 0)
        done = done + group * n_groups


def _ones_row_sums(x):
    return _dot_nt(jnp.ones((8, x.shape[1]), BF16), x.astype(BF16))[0:1]


PROJ_CHUNK = 512


def _ones_row_tail(rows, width):
    return jnp.where(lax.broadcasted_iota(jnp.int32, (rows, width), 0) == 0, 1.0, 0.0).astype(BF16)


def _even_proj_kernel(x_ref, g_ref, w_ref, o_ref, sbv_ref, dfv_ref):
    tm = x_ref.shape[0]
    xn = _rms(x_ref[...], g_ref[...]).astype(BF16)
    tail = _ones_row_tail(V_ROWS_128 - 2 * HEAD_DIM, tm)
    for c in range(w_ref.shape[1] // PROJ_CHUNK):
        sl = slice(c * PROJ_CHUNK, (c + 1) * PROJ_CHUNK)
        res = _dot(xn, w_ref[:, sl])
        if c in (0, 3):
            res = res * Q_SCALE
        o_ref[:, sl] = res.astype(BF16)
        if c == 2:
            sbv_ref[0] = res.T.astype(BF16)
        if c == 5:
            v_t = res.T.astype(BF16)
            for h in range(dfv_ref.shape[1]):
                dfv_ref[0, h, 0:2 * HEAD_DIM, :] = v_t[h * 2 * HEAD_DIM:(h + 1) * 2 * HEAD_DIM]
                dfv_ref[0, h, 2 * HEAD_DIM:V_ROWS_128, :] = tail


def even_proj(x2d, B, S, g, w_bf16, tm=512):
    T, D = x2d.shape
    n_w = w_bf16.shape[1]
    per_seq = S // tm
    assert n_w == 6 * PROJ_CHUNK
    return pl.pallas_call(
        _even_proj_kernel,
        grid=(T // tm,),
        in_specs=[pl.BlockSpec((tm, D), lambda i: (i, 0)),
                  pl.BlockSpec((1, D), lambda i: (0, 0)),
                  pl.BlockSpec((D, n_w), lambda i: (0, 0))],
        out_specs=[pl.BlockSpec((tm, n_w), lambda i: (i, 0)),
                   pl.BlockSpec((1, PROJ_CHUNK, tm), lambda i: (i // per_seq, 0, i % per_seq)),
                   pl.BlockSpec((1, 4, V_ROWS_128, tm), lambda i: (i // per_seq, 0, 0, i % per_seq))],
        out_shape=[jax.ShapeDtypeStruct((T, n_w), BF16),
                   jax.ShapeDtypeStruct((B, PROJ_CHUNK, S), BF16),
                   jax.ShapeDtypeStruct((B, 4, V_ROWS_128, S), BF16)],
        compiler_params=_params(1),
        name="even_proj",
    )(x2d, g.reshape(1, D), w_bf16)


def _odd_proj_kernel(x_ref, g_ref, w_ref, pos_ref, onehot_ref,
                     q_ref, cin_ref, ks_ref, kw_ref, vst_ref, vwt_ref, gate_ref):
    tm = x_ref.shape[0]
    G, Dh = NSA_GROUPS, HEAD_DIM
    kv = G * Dh
    xn = _rms(x_ref[...], g_ref[...]).astype(BF16)
    tail = _ones_row_tail(V_ROWS_64 - Dh, tm)
    pos = pos_ref[...]
    onehot = onehot_ref[...]
    q_width = q_ref.shape[1]
    for c in range(q_width // PROJ_CHUNK):
        sl = slice(c * PROJ_CHUNK, (c + 1) * PROJ_CHUNK)
        q_ref[:, sl] = (_dot(xn, w_ref[:, sl]) * Q_SCALE).astype(BF16)

    def pair(j):
        return _dot(xn, w_ref[:, q_width + j * 2 * kv:q_width + (j + 1) * 2 * kv])

    res = pair(0)
    for which in range(2):
        for g in range(G):
            cin_ref[which, 0, g] = res[:, which * kv + g * Dh:which * kv + (g + 1) * Dh]

    for j, k_ref, vt_ref, extra in ((1, ks_ref, vst_ref, [pos, onehot]), (2, kw_ref, vwt_ref, [pos])):
        res = pair(j)
        v_t = res[:, kv:2 * kv].T.astype(BF16)
        for g in range(G):
            k_ref[0, g] = jnp.concatenate([res[:, g * Dh:(g + 1) * Dh].astype(BF16)] + extra, axis=1)
            vt_ref[0, g, 0:Dh, :] = v_t[g * Dh:(g + 1) * Dh]
            vt_ref[0, g, Dh:V_ROWS_64, :] = tail

    logits = _dot(xn, w_ref[:, q_width + 6 * kv:q_width + 6 * kv + GATE_PAD])
    gate_ref[0] = jax.nn.sigmoid(logits).T[0:gate_ref.shape[1]]


def odd_proj(x2d, B, S, g, w_bf16, pos_cols, onehot, tm=512):
    T, D = x2d.shape
    G, Dh = NSA_GROUPS, HEAD_DIM
    q_width = G * NSA_HPG * Dh
    n_sel = onehot.shape[1]
    per_seq = S // tm
    seq_tile = lambda i: (i // per_seq, 0, i % per_seq, 0)
    seq_tile_t = lambda i: (i // per_seq, 0, 0, i % per_seq)
    return pl.pallas_call(
        _odd_proj_kernel,
        grid=(T // tm,),
        in_specs=[pl.BlockSpec((tm, D), lambda i: (i, 0)),
                  pl.BlockSpec((1, D), lambda i: (0, 0)),
                  pl.BlockSpec(w_bf16.shape, lambda i: (0, 0)),
                  pl.BlockSpec((tm, Dh), lambda i: (i % per_seq, 0)),
                  pl.BlockSpec((tm, n_sel), lambda i: (i % per_seq, 0))],
        out_specs=[pl.BlockSpec((tm, q_width), lambda i: (i, 0)),
                   pl.BlockSpec((2, 1, G, tm, Dh), lambda i: (0, i // per_seq, 0, i % per_seq, 0)),
                   pl.BlockSpec((1, G, tm, 2 * Dh + n_sel), seq_tile),
                   pl.BlockSpec((1, G, tm, 2 * Dh), seq_tile),
                   pl.BlockSpec((1, G, V_ROWS_64, tm), seq_tile_t),
                   pl.BlockSpec((1, G, V_ROWS_64, tm), seq_tile_t),
                   pl.BlockSpec((1, 4 * NSA_HPG * G, tm), lambda i: (i // per_seq, 0, i % per_seq))],
        out_shape=[jax.ShapeDtypeStruct((T, q_width), BF16),
                   jax.ShapeDtypeStruct((2, B, G, S, Dh), F32),
                   jax.ShapeDtypeStruct((B, G, S, 2 * Dh + n_sel), BF16),
                   jax.ShapeDtypeStruct((B, G, S, 2 * Dh), BF16),
                   jax.ShapeDtypeStruct((B, G, V_ROWS_64, S), BF16),
                   jax.ShapeDtypeStruct((B, G, V_ROWS_64, S), BF16),
                   jax.ShapeDtypeStruct((B, 4 * NSA_HPG * G, S), F32)],
        compiler_params=_params(1),
        name="odd_proj",
    )(x2d, g.reshape(1, D), w_bf16, pos_cols, onehot)


def _head_pair_rows(q, t):
    lane = lax.broadcasted_iota(jnp.int32, (t, 2 * HEAD_DIM), 1)
    zero = jnp.zeros_like(q)
    return jnp.where(lane < HEAD_DIM, q, zero), jnp.where(lane >= HEAD_DIM, q, zero)


def _sb_kernel(q_ref, k_ref, vt_ref, o_ref, acc_ref, carry_ref, *, t):
    qi = pl.program_id(2)
    cols = 2 * t
    q_both = jnp.concatenate(_head_pair_rows(q_ref[0], t), axis=0)
    s_idx = lax.broadcasted_iota(jnp.int32, (t, 2 * t), 0)
    j_idx = lax.broadcasted_iota(jnp.int32, (t, 2 * t), 1) % t
    upper2 = jnp.where(j_idx > s_idx, 1.0, 0.0).astype(BF16)
    key = lax.broadcasted_iota(jnp.int32, (t, cols), 0)
    qry = lax.broadcasted_iota(jnp.int32, (t, cols), 1) % t
    past_diag = key < qry

    acc_ref[...] = jnp.zeros_like(acc_ref)
    carry_ref[...] = jnp.zeros_like(carry_ref)

    def chunks(specs):
        starts = [pl.multiple_of(kc * t, t) for kc, _ in specs]
        z_all = [_dot_nt(k_ref[0, pl.ds(start, t), :], q_both) for start in starts]
        carry = carry_ref[...]
        acc = acc_ref[...]
        for (_, masked), start, z in zip(specs, starts, z_all):
            drop = jnp.maximum(z, 0.0) + jnp.log2(1.0 + jnp.exp2(jnp.abs(z) * -1.0))
            log_beta = z - drop
            if masked:
                drop = jnp.where(past_diag, drop, 0.0)
            hi, lo = _split_bf16(drop)
            tail = _dot(upper2, jnp.concatenate([hi, lo], axis=0))
            w = jnp.exp2(log_beta - tail - carry)
            if masked:
                w = jnp.where(past_diag, w, 0.0)
            carry = carry + tail[0:1] + drop[0:1]
            acc = acc + _dot(vt_ref[0, :, pl.ds(start, t)], w.astype(BF16))
        carry_ref[...] = carry
        acc_ref[...] = acc
        return jnp.min(carry)

    least = lax.cond(qi > 0, lambda: chunks([(qi, True), (qi - 1, False)]), lambda: chunks([(qi, True)]))

    def cond(state):
        j, least_carry = state
        return (j < qi) & (least_carry < -UNDERFLOW_LOG2)

    def body(state):
        j, _ = state
        return j + 1, chunks([(qi - 1 - j, False)])

    lax.while_loop(cond, body, (1, least))
    row = lax.broadcasted_iota(jnp.int32, (2 * HEAD_DIM, t), 0)
    o_t = jnp.where(row < HEAD_DIM, acc_ref[:, :t], acc_ref[:, t:])
    o_ref[0] = o_t.T.astype(o_ref.dtype)


def sb_attention(proj, v_t, t=256):
    B, S, _ = proj.shape
    n_pairs = 4
    return pl.pallas_call(
        functools.partial(_sb_kernel, t=t),
        grid=(B, n_pairs, S // t),
        in_specs=[pl.BlockSpec((1, t, 128), lambda b, p, i: (b, i, p)),
                  pl.BlockSpec((1, S, 128), lambda b, p, i: (b, 0, n_pairs + p)),
                  pl.BlockSpec((1, 128, S), lambda b, p, i: (b, p, 0))],
        out_specs=pl.BlockSpec((1, t, 128), lambda b, p, i: (b, i, p)),
        out_shape=jax.ShapeDtypeStruct((B, S, 2 * n_pairs * 128), BF16),
        scratch_shapes=[pltpu.VMEM((128, 2 * t), F32), pltpu.VMEM((1, 2 * t), F32)],
        compiler_params=_params(3),
        name="sb_attention",
    )(proj, proj, v_t)


def _diff_kernel(slopes_ref, lam_ref, q_ref, k_ref, vt_ref, pos_ref, sl_ref, subln_ref, _, o_ref,
                 m_ref, acc_ref, offset_ref, knorm_ref, *, t, out_scale, lambda_init):
    h = pl.program_id(1)
    qi = pl.program_id(2)
    slope = slopes_ref[h]
    cols = 2 * t
    slope_cols = jnp.broadcast_to(sl_ref[0], (t, 128))
    q_parts = _head_pair_rows(q_ref[0], t)
    q_both = jnp.concatenate([jnp.concatenate([qc, slope_cols], axis=1) for qc in q_parts],
                             axis=0)
    lane = lax.broadcasted_iota(jnp.int32, (t, 2 * HEAD_DIM), 1)

    @pl.when(qi == 0)
    def _():
        def body(c, best):
            k = k_ref[0, pl.ds(pl.multiple_of(c * t, t), t), :].astype(F32)
            k2 = k * k
            return (jnp.maximum(best[0], jnp.max(jnp.sum(jnp.where(lane < HEAD_DIM, k2, 0.0), axis=1))),
                    jnp.maximum(best[1], jnp.max(jnp.sum(jnp.where(lane >= HEAD_DIM, k2, 0.0), axis=1))))

        best = lax.fori_loop(0, k_ref.shape[1] // t, body, (jnp.float32(0.0), jnp.float32(0.0)))
        knorm_ref[0] = best[0]
        knorm_ref[1] = best[1]
        offset_ref[...] = (lax.broadcasted_iota(jnp.int32, (t, cols), 0)
                           - lax.broadcasted_iota(jnp.int32, (t, cols), 1) % t)

    q32 = [qc.astype(F32) for qc in q_parts]
    col = lax.broadcasted_iota(jnp.int32, (1, cols), 1)
    k_norm2 = jnp.where(col < t, knorm_ref[0], knorm_ref[1])
    qk_bound = jnp.sqrt(jnp.concatenate([_ones_row_sums(x * x) for x in q32], axis=1) * k_norm2 * 1.05)
    k_self = k_ref[0, pl.ds(pl.multiple_of(qi * t, t), t), :].astype(F32)
    self_score = jnp.concatenate([_ones_row_sums(x * k_self) for x in q32], axis=1)
    safe = jnp.max(qk_bound - self_score) <= SAFE_EXP_LOG2

    pos = pos_ref[...]

    def step(kc, diagonal=False):
        start = pl.multiple_of(kc * t, t)

        def scores():
            s_t = _dot_nt(jnp.concatenate([k_ref[0, pl.ds(start, t), :], pos], axis=1), q_both)
            return jnp.where(offset_ref[...] <= 0, s_t, NEG_INF) if diagonal else s_t

        return scores, slope * ((kc - qi) * t).astype(F32), lambda: vt_ref[0, 0, :, pl.ds(start, t)], 0

    acc_ref[...] = jnp.zeros_like(acc_ref)

    @pl.when(safe)
    def _():
        frame = qk_bound + slope * (col % t).astype(F32)
        block = lambda steps: _fixed_frame_block(steps, [acc_ref], [frame])
        block([step(qi, True)])
        n_back = jnp.ceil((slope * (t - 1) - UNDERFLOW_LOG2) / (slope * t)).astype(jnp.int32)
        _chunk_loop(jnp.minimum(n_back, qi), lambda i: step(qi - 1 - i), block, (8, 4, 2, 1))

    @pl.when(jnp.logical_not(safe))
    def _():
        m_ref[...] = jnp.full_like(m_ref, NEG_INF)
        block = lambda steps: _softmax_block(steps, [(m_ref, acc_ref)])
        block([step(qi, True)])
        excess = jnp.max(qk_bound - m_ref[...]) + slope * (t - 1)
        n_back = jnp.clip(jnp.ceil((excess - UNDERFLOW_LOG2) / (slope * t)).astype(jnp.int32), 0, qi)
        _chunk_loop(n_back, lambda i: step(qi - 1 - i), block, (4, 2, 1))

    lam_terms = lam_ref[...]
    lam = (jnp.exp(jnp.sum(lam_terms[0:1] * lam_terms[1:2], axis=1, keepdims=True))
           - jnp.exp(jnp.sum(lam_terms[2:3] * lam_terms[3:4], axis=1, keepdims=True))
           + lambda_init)
    o_t = acc_ref[0:128, :] / acc_ref[128:129, :]
    o = (o_t[:, :t] - lam * o_t[:, t:]).T
    o_ref[0] = (_rms(o, subln_ref[...]) * out_scale).astype(o_ref.dtype)


def diff_attention(proj, vt_aug, lam_rows, subln, slopes, layer, mix, t=512):
    B, S, _ = proj.shape
    n_heads = 4
    lambda_init = 0.8 - 0.6 * math.exp(-0.3 * layer)
    smem = pl.BlockSpec(memory_space=pltpu.SMEM)
    pos = _key_position_columns(t, t, 128)
    slope_cols = _slope_pieces(slopes, 128).reshape(n_heads, 1, 128)
    return pl.pallas_call(
        functools.partial(_diff_kernel, t=t, out_scale=1.0 - lambda_init, lambda_init=lambda_init),
        grid=(B, n_heads, S // t),
        in_specs=[smem,
                  pl.BlockSpec((4, HEAD_DIM), lambda b, h, i: (0, 0)),
                  pl.BlockSpec((1, t, 128), lambda b, h, i: (b, i, 12 + h)),
                  pl.BlockSpec((1, S, 128), lambda b, h, i: (b, 0, 16 + h)),
                  pl.BlockSpec((1, 1, V_ROWS_128, S), lambda b, h, i: (b, h, 0, 0)),
                  pl.BlockSpec((t, 128), lambda b, h, i: (0, 0)),
                  pl.BlockSpec((1, 1, 128), lambda b, h, i: (h, 0, 0)),
                  pl.BlockSpec((1, 128), lambda b, h, i: (0, 0)),
                  pl.BlockSpec(memory_space=pl.ANY)],
        out_specs=pl.BlockSpec((1, t, 128), lambda b, h, i: (b, i, n_heads + h)),
        out_shape=jax.ShapeDtypeStruct(mix.shape, mix.dtype),
        input_output_aliases={8: 0},
        scratch_shapes=[pltpu.VMEM((1, 2 * t), F32), pltpu.VMEM((V_ROWS_128, 2 * t), F32),
                        pltpu.VMEM((t, 2 * t), jnp.int32), pltpu.SMEM((2,), F32)],
        compiler_params=_params(2, 1),
        name="diff_attention",
    )(slopes, lam_rows, proj, proj, vt_aug, pos, slope_cols, subln.reshape(1, 128), mix)


def _post_kernel(*refs, ff_chunk, final):
    mix_ref, x_ref, wo_ref, g_ref, w1_ref, w2_ref = refs[:6]
    gf_ref = refs[6] if final else None
    o_ref = refs[-1]
    x = x_ref[...] + _dot(mix_ref[...], wo_ref[...])
    hn = _rms(x, g_ref[...]).astype(BF16)
    acc = x
    for f in range(w1_ref.shape[1] // ff_chunk):
        sl = slice(f * ff_chunk, (f + 1) * ff_chunk)
        hid = jnp.maximum(_dot(hn, w1_ref[:, sl]), 0.0)
        acc = acc + _dot((hid * hid).astype(BF16), w2_ref[sl, :])
    if final:
        acc = _rms(acc, gf_ref[...])
    o_ref[...] = acc


def post_block(mix, x2d, w_out, g_mlp, w1, w2, g_final=None, tm=512, ff_chunk=1024):
    T, D = x2d.shape
    final = g_final is not None
    const = lambda i: (0, 0)
    in_specs = [pl.BlockSpec((tm, mix.shape[1]), lambda i: (i, 0)),
                pl.BlockSpec((tm, D), lambda i: (i, 0)),
                pl.BlockSpec(w_out.shape, const),
                pl.BlockSpec((1, D), const), pl.BlockSpec(w1.shape, const), pl.BlockSpec(w2.shape, const)]
    args = [mix, x2d, w_out, g_mlp.reshape(1, D), w1, w2]
    if final:
        in_specs.append(pl.BlockSpec((1, D), const))
        args.append(g_final.reshape(1, D))
    return pl.pallas_call(
        functools.partial(_post_kernel, ff_chunk=ff_chunk, final=final),
        grid=(T // tm,),
        in_specs=in_specs,
        out_specs=pl.BlockSpec((tm, D), lambda i: (i, 0)),
        out_shape=jax.ShapeDtypeStruct((T, D), F32),
        compiler_params=_params(1),
        name="post_block",
    )(*args)


def _compress_kernel(c_ref, pos_ref, w1_ref, w2_ref, o_ref):
    half = w1_ref.shape[1] // 2
    n_chunks = c_ref.shape[3] // CMP_STRIDE
    first = second = None
    for l in range(CMP_STRIDE):
        tok = c_ref[0, 0, 0, pl.ds(l, n_chunks, stride=CMP_STRIDE), :].astype(BF16)
        a = _dot(tok, w1_ref[0, l * HEAD_DIM:(l + 1) * HEAD_DIM, :])
        b = _dot(tok, w1_ref[0, half + l * HEAD_DIM:half + (l + 1) * HEAD_DIM, :])
        first = a if first is None else first + a
        second = b if second is None else second + b
    pos = jnp.broadcast_to(pos_ref[0], (8, 2 * half)).astype(BF16)
    pre = first + pltpu.roll(second, n_chunks - 1, 0) + _dot(pos, w1_ref[0])[0:1]
    hid = jax.nn.gelu(pre)
    o_ref[0, 0] = _dot(hid.astype(BF16), w2_ref[0]).astype(o_ref.dtype)


def compress_kv(tokens, pos_flat, w1, w2):
    _, B, G, S, Dh = tokens.shape
    n_chunks = S // CMP_STRIDE
    hidden = w1.shape[-1]
    return pl.pallas_call(
        _compress_kernel,
        grid=(2, B * G),
        in_specs=[pl.BlockSpec((1, 1, 1, S, Dh), lambda s, i: (s, i // G, i % G, 0, 0)),
                  pl.BlockSpec((1, 1, CMP_LEN * Dh), lambda s, i: (s, 0, 0)),
                  pl.BlockSpec((1, CMP_LEN * Dh, hidden), lambda s, i: (s, 0, 0)),
                  pl.BlockSpec((1, hidden, Dh), lambda s, i: (s, 0, 0))],
        out_specs=pl.BlockSpec((1, 1, n_chunks, Dh), lambda s, i: (s, i, 0, 0)),
        out_shape=jax.ShapeDtypeStruct((2, B * G, n_chunks, Dh), BF16),
        compiler_params=_params(2),
        name="compress_kv",
    )(tokens, pos_flat, w1, w2)


def _stack_heads(q):
    return jnp.concatenate([q[:, r * HEAD_DIM:(r + 1) * HEAD_DIM] for r in range(NSA_HPG)], axis=0)


def _slope_row(slopes_ref, g, tq):
    col = lax.broadcasted_iota(jnp.int32, (1, NSA_HPG * tq), 1)
    out = jnp.zeros((1, NSA_HPG * tq), F32)
    for r in range(NSA_HPG):
        out = jnp.where(col // tq == r, slopes_ref[g * NSA_HPG + r], out)
    return out


def _cmp_select_kernel(slopes_ref, q_ref, kc_ref, vct_ref, ovt_ref, ocmp_ref, sel_ref, hits_ref,
                       *, tq, n_sel, blocks_per_chunk, hit_tile):
    g = pl.program_id(1)
    t0 = pl.program_id(2) * tq
    cols = NSA_HPG * tq
    n_cmp = kc_ref.shape[2]
    slope_row = _slope_row(slopes_ref, g, tq)
    q_rows = _stack_heads(q_ref[0])
    tpos = t0 + lax.broadcasted_iota(jnp.int32, (n_cmp, cols), 1) % tq
    cmp_end = lax.broadcasted_iota(jnp.int32, (n_cmp, cols), 0) * CMP_STRIDE + (CMP_LEN - 1)
    dc = (tpos - cmp_end).astype(F32)
    sc = jnp.where(dc >= 0, _dot_nt(kc_ref[0, 0], q_rows) - slope_row * dc, NEG_INF)
    e = jnp.exp2(sc - jnp.max(sc, axis=0, keepdims=True))
    any_valid = jnp.where(dc[0:1] >= 0, 1.0, 0.0)
    pc = e * (any_valid / jnp.sum(e, axis=0, keepdims=True))
    o_t = _dot(vct_ref[0, 0], pc.astype(BF16))
    for r in range(NSA_HPG):
        ocmp_ref[0, 0, r] = o_t[:, r * tq:(r + 1) * tq]

    pc_group = pc[:, 0:tq]
    for r in range(1, NSA_HPG):
        pc_group = pc_group + pc[:, r * tq:(r + 1) * tq]
    hi, lo = _split_bf16(pc_group)
    imp = _dot(ovt_ref[...], hi) + _dot(ovt_ref[...], lo)

    blk = lax.broadcasted_iota(jnp.int32, (n_sel, tq), 0)
    cur = (t0 + lax.broadcasted_iota(jnp.int32, (n_sel, tq), 1)) // SEL_LEN
    forced = (blk == 0) | (blk == cur) | (blk == cur - 1)
    imp = jnp.where(blk <= cur, imp, -1.0)
    topk = min(SEL_TOPK, n_sel)

    def pick(imp, count):
        for _ in range(count):
            best = jnp.max(imp, axis=0, keepdims=True)
            first = jnp.min(jnp.where(imp == best, blk, n_sel), axis=0, keepdims=True)
            imp = jnp.where(blk == first, PICKED, imp)
        return imp

    imp = lax.cond(t0 >= 2 * SEL_LEN,
                   lambda: pick(jnp.where(forced, PICKED, imp), topk - 3),
                   lambda: pick(jnp.where(forced, FORCE_SCORE, imp), topk))
    keep = (imp == PICKED) & (blk <= cur)
    sel_ref[0, 0] = jnp.where(keep, 0.0, -SEL_DROP).T.astype(sel_ref.dtype)
    n_chunks = n_sel // blocks_per_chunk
    member = (lax.broadcasted_iota(jnp.int32, (n_chunks, n_sel), 1) // blocks_per_chunk
              == lax.broadcasted_iota(jnp.int32, (n_chunks, n_sel), 0))
    per_query = _dot(jnp.where(member, 1.0, 0.0).astype(BF16),
                     jnp.where(keep, 1.0, 0.0).astype(BF16)).astype(BF16)
    for part in range(tq // hit_tile):
        hits_ref[0, 0, part] = _dot(per_query[:, part * hit_tile:(part + 1) * hit_tile],
                                    jnp.ones((hit_tile, 128), BF16))


def cmp_select(q, kc, vc_t, overlap_t, slopes, tk, hit_tile, tq=256):
    B, S, _ = q.shape
    n_cmp = kc.shape[2]
    n_sel = S // SEL_LEN
    G = NSA_GROUPS
    smem = pl.BlockSpec(memory_space=pltpu.SMEM)
    return pl.pallas_call(
        functools.partial(_cmp_select_kernel, tq=tq, n_sel=n_sel, blocks_per_chunk=tk // SEL_LEN,
                          hit_tile=hit_tile),
        grid=(B, G, S // tq),
        in_specs=[smem,
                  pl.BlockSpec((1, tq, 256), lambda b, g, i: (b, i, g)),
                  pl.BlockSpec((1, 1, n_cmp, HEAD_DIM), lambda b, g, i: (b, g, 0, 0)),
                  pl.BlockSpec((1, 1, HEAD_DIM, n_cmp), lambda b, g, i: (b, g, 0, 0)),
                  pl.BlockSpec((n_sel, n_cmp), lambda b, g, i: (0, 0))],
        out_specs=[pl.BlockSpec((1, 1, NSA_HPG, HEAD_DIM, tq), lambda b, g, i: (b, g, 0, 0, i)),
                   pl.BlockSpec((1, 1, tq, n_sel), lambda b, g, i: (b, g, i, 0)),
                   pl.BlockSpec((1, 1, tq // hit_tile, S // tk, 128), lambda b, g, i: (b, g, i, 0, 0))],
        out_shape=[jax.ShapeDtypeStruct((B, G, NSA_HPG, HEAD_DIM, S), F32),
                   jax.ShapeDtypeStruct((B, G, S, n_sel), BF16),
                   jax.ShapeDtypeStruct((B, G, S // hit_tile, S // tk, 128), F32)],
        compiler_params=_params(3),
        name="cmp_select",
    )(slopes, q, kc, vc_t, overlap_t)


def _sel_win_kernel(slopes_ref, active_ref, q_ref, sl_ref, ks_ref, vst_ref, kw_ref, vwt_ref, sel_ref,
                    ocmp_ref, gate_ref, o_ref, ms_ref, accs_ref, mw_ref, accw_ref, offset_ref, todo_ref,
                    knorm_ref, *, tq, tk):
    g = pl.program_id(1)
    t0 = pl.program_id(2) * tq
    cols = NSA_HPG * tq
    slope_row = _slope_row(slopes_ref, g, tq)
    q = q_ref[0]
    q_aug = jnp.concatenate(
        [jnp.concatenate([q[:, r * HEAD_DIM:(r + 1) * HEAD_DIM],
                          jnp.broadcast_to(sl_ref[0, r:r + 1, :], (tq, HEAD_DIM))], axis=1)
         for r in range(NSA_HPG)], axis=0)
    q_sel = jnp.concatenate([q_aug, jnp.concatenate([sel_ref[0, 0]] * NSA_HPG, axis=0)], axis=1)

    @pl.when(pl.program_id(2) == 0)
    def _():
        offset_ref[...] = (lax.broadcasted_iota(jnp.int32, (tk, cols), 0)
                           - lax.broadcasted_iota(jnp.int32, (tk, cols), 1) % tq)

        def body(c, best):
            rows = pl.ds(pl.multiple_of(c * tk, tk), tk)
            ks = ks_ref[0, 0, rows, :][:, :HEAD_DIM].astype(F32)
            kw = kw_ref[0, 0, rows, :][:, :HEAD_DIM].astype(F32)
            return (jnp.maximum(best[0], jnp.max(jnp.sum(ks * ks, axis=1))),
                    jnp.maximum(best[1], jnp.max(jnp.sum(kw * kw, axis=1))))

        best = lax.fori_loop(0, ks_ref.shape[2] // tk, body, (jnp.float32(0.0), jnp.float32(0.0)))
        knorm_ref[0] = best[0]
        knorm_ref[1] = best[1]

    heads32 = [q.astype(F32)[:, r * HEAD_DIM:(r + 1) * HEAD_DIM] for r in range(NSA_HPG)]
    q_norm2 = jnp.concatenate([_ones_row_sums(x * x) for x in heads32], axis=1)
    own_rows = pl.ds(pl.multiple_of(t0, tq), tq)

    def bound_and_self(k_ref, k_norm2):
        k_self = k_ref[0, 0, own_rows, :][:, :HEAD_DIM].astype(F32)
        own = jnp.concatenate([_ones_row_sums(x * k_self) for x in heads32], axis=1)
        return jnp.sqrt(q_norm2 * k_norm2 * 1.05), own

    sel_bound, sel_self = bound_and_self(ks_ref, knorm_ref[0])
    win_bound, win_self = bound_and_self(kw_ref, knorm_ref[1])
    safe = jnp.maximum(jnp.max(sel_bound - sel_self), jnp.max(win_bound - win_self)) <= SAFE_EXP_LOG2

    def step(c, k_ref, q_rows, vt_ref, keep=None):
        inside = c >= 0
        start = pl.multiple_of(jnp.maximum(c, 0) * tk, tk)
        shift = c * tk - t0

        def scores():
            s_t = _dot_nt(k_ref[0, 0, pl.ds(start, tk), :], q_rows)
            if keep is None:
                return s_t
            lo, hi = keep
            offset = offset_ref[...]
            if hi is not None:
                return jnp.where(offset <= jnp.where(inside, hi - shift, -FAR), s_t, NEG_INF)
            return jnp.where(offset > jnp.where(inside, lo - shift, FAR), s_t, NEG_INF)

        kappa = jnp.where(inside, slope_row * shift.astype(F32), NEG_INF)
        return scores, kappa, lambda: vt_ref[0, 0, :, pl.ds(start, tk)]

    last = t0 // tk
    causal, recent, anything = (None, 0), (-WINDOW, None), (-FAR, None)
    first_steps = [step(last - 2, kw_ref, q_aug, vwt_ref, recent) + (1,),
                   step(last, ks_ref, q_sel, vst_ref, causal) + (0,),
                   step(last - 1, kw_ref, q_aug, vwt_ref, anything) + (1,),
                   step(last, kw_ref, q_aug, vwt_ref, causal) + (1,)]

    def note_active(c, n):
        hit = active_ref[0, 0, 0, 0, c] > 0

        @pl.when(hit)
        def _():
            todo_ref[n] = c

        return n + hit.astype(jnp.int32)

    n_todo = lax.fori_loop(0, last, note_active, 0)
    sel_step = lambda i: step(todo_ref[i], ks_ref, q_sel, vst_ref) + (0,)

    accs_ref[...] = jnp.zeros_like(accs_ref)
    accw_ref[...] = jnp.zeros_like(accw_ref)

    @pl.when(safe)
    def _():
        in_tile = slope_row * (lax.broadcasted_iota(jnp.int32, (1, cols), 1) % tq).astype(F32)
        block = lambda steps: _fixed_frame_block(steps, [accs_ref, accw_ref],
                                                 [sel_bound + in_tile, win_bound + in_tile])
        block(first_steps)
        _chunk_loop(n_todo, sel_step, block, (8, 4, 2, 1))

    @pl.when(jnp.logical_not(safe))
    def _():
        states = [(ms_ref, accs_ref), (mw_ref, accw_ref)]
        for m_ref, _ in states:
            m_ref[...] = jnp.full_like(m_ref, NEG_INF)
        block = lambda steps: _softmax_block(steps, states)
        block(first_steps)
        _chunk_loop(n_todo, sel_step, block, (4, 2, 1))

    o_sel = accs_ref[0:HEAD_DIM, :] / accs_ref[HEAD_DIM:HEAD_DIM + 1, :]
    o_win = accw_ref[0:HEAD_DIM, :] / accw_ref[HEAD_DIM:HEAD_DIM + 1, :]

    gates = gate_ref[0]
    outs = []
    for r in range(NSA_HPG):
        cs = slice(r * tq, (r + 1) * tq)
        outs.append(gates[3 * r:3 * r + 1] * ocmp_ref[0, 0, r] + gates[3 * r + 1:3 * r + 2] * o_sel[:, cs]
                    + gates[3 * r + 2:3 * r + 3] * o_win[:, cs])
    o_ref[0] = jnp.concatenate(outs, axis=0).T.astype(o_ref.dtype)


def sel_win_attention(q, active, slope_cols, ks, vs_t, kw, vw_t, sel_bias, o_cmp, gates_t,
                      slopes, tq=256, tk=256):
    B, S, _ = q.shape
    assert tq == tk and WINDOW == 2 * tk
    G = NSA_GROUPS
    n_sel = S // SEL_LEN
    cols = NSA_HPG * tq
    smem = pl.BlockSpec(memory_space=pltpu.SMEM)
    ks_spec = pl.BlockSpec((1, 1, S, ks.shape[-1]), lambda b, g, i: (b, g, 0, 0))
    kw_spec = pl.BlockSpec((1, 1, S, kw.shape[-1]), lambda b, g, i: (b, g, 0, 0))
    v_spec = pl.BlockSpec((1, 1, V_ROWS_64, S), lambda b, g, i: (b, g, 0, 0))
    return pl.pallas_call(
        functools.partial(_sel_win_kernel, tq=tq, tk=tk),
        grid=(B, G, S // tq),
        in_specs=[smem,
                  pl.BlockSpec((1, 1, 1, 1, S // tk), lambda b, g, i: (b, g, i, 0, 0),
                               memory_space=pltpu.SMEM),
                  pl.BlockSpec((1, tq, 256), lambda b, g, i: (b, i, g)),
                  pl.BlockSpec((1, NSA_HPG, HEAD_DIM), lambda b, g, i: (g, 0, 0)),
                  ks_spec, v_spec, kw_spec, v_spec,
                  pl.BlockSpec((1, 1, tq, n_sel), lambda b, g, i: (b, g, i, 0)),
                  pl.BlockSpec((1, 1, NSA_HPG, HEAD_DIM, tq), lambda b, g, i: (b, g, 0, 0, i)),
                  pl.BlockSpec((1, 4 * NSA_HPG, tq), lambda b, g, i: (b, g, i))],
        out_specs=pl.BlockSpec((1, tq, 256), lambda b, g, i: (b, i, g)),
        out_shape=jax.ShapeDtypeStruct((B, S, G * 256), BF16),
        scratch_shapes=[pltpu.VMEM((1, cols), F32), pltpu.VMEM((V_ROWS_64, cols), F32),
                        pltpu.VMEM((1, cols), F32), pltpu.VMEM((V_ROWS_64, cols), F32),
                        pltpu.VMEM((tk, cols), jnp.int32),
                        pltpu.SMEM((S // tk,), jnp.int32), pltpu.SMEM((2,), F32)],
        compiler_params=_params(2, 1),
        name="sel_win_attention",
    )(slopes, active, q, slope_cols, ks, vs_t, kw, vw_t, sel_bias, o_cmp, gates_t)


def _alibi_slopes_log2(n_heads):
    slopes = np.exp2(-8.0 * (np.arange(n_heads, dtype=np.float32) + 1.0) / n_heads)
    return (slopes.astype(np.float32) * np.float32(LOG2E)).astype(np.float32)


def even_layer_mix(x2d, B, S, norm_g, w_in, lam_q1, lam_k1, lam_q2, lam_k2, subln, layer):
    proj, sb_vt, df_vt = even_proj(x2d, B, S, norm_g, w_in.astype(BF16))
    proj = proj.reshape(B, S, -1)
    lam_rows = jnp.stack([lam_q1, lam_k1, lam_q2, lam_k2]).astype(F32)
    mix = diff_attention(proj, df_vt, lam_rows, subln.astype(F32), _alibi_slopes_log2(4), layer,
                         sb_attention(proj, sb_vt))
    return mix.reshape(B * S, -1)


def odd_layer_mix(x2d, B, S, norm_g, w_in, pos_k, k_w1, k_w2, pos_v, v_w1, v_w2, tq=256, tk=256):
    G, Dh = NSA_GROUPS, HEAD_DIM
    q_width = G * NSA_HPG * Dh
    n_main = q_width + 6 * G * Dh
    per_group = NSA_HPG * N_GATES
    w_gate = w_in[:, n_main:n_main + G * per_group].reshape(-1, G, per_group)
    w_gate = jnp.pad(w_gate, ((0, 0), (0, 0), (0, 4 * NSA_HPG - per_group))).reshape(-1, 4 * NSA_HPG * G)
    w_gate = jnp.pad(w_gate, ((0, 0), (0, GATE_PAD - w_gate.shape[1])))
    w_all = jnp.concatenate([w_in[:, :n_main], w_gate], axis=1).astype(BF16)

    n_chunks = S // CMP_STRIDE
    n_sel = S // SEL_LEN
    onehot = (jnp.arange(S)[:, None] // SEL_LEN == jnp.arange(n_sel)[None, :]).astype(BF16)
    q, cmp_in, ks, kw, vs_t, vw_t, gates_t = odd_proj(x2d, B, S, norm_g, w_all,
                                                      _key_position_columns(S, tk, Dh), onehot)
    q = q.reshape(B, S, q_width)

    pos_flat = jnp.stack([pos_k, pos_v]).reshape(2, 1, CMP_LEN * Dh).astype(F32)
    w1 = jnp.stack([k_w1, v_w1]).astype(BF16)
    w2 = jnp.stack([k_w2, v_w2]).astype(BF16)
    cmp = compress_kv(cmp_in, pos_flat, w1, w2).reshape(2, B, G, n_chunks, Dh)

    cmp_start = jnp.arange(n_chunks) * CMP_STRIDE
    sel_start = jnp.arange(n_sel) * SEL_LEN
    overlap_t = ((cmp_start[None, :] < sel_start[:, None] + SEL_LEN)
                 & (sel_start[:, None] <= cmp_start[None, :] + CMP_LEN - 1)).astype(BF16)
    slopes = _alibi_slopes_log2(G * NSA_HPG)
    slope_cols = _slope_pieces(slopes, Dh).reshape(G, NSA_HPG, Dh)

    o_cmp, sel_bias, hits = cmp_select(q, cmp[0], cmp[1].transpose(0, 1, 3, 2), overlap_t, slopes, tk, tq)
    active = (hits[..., 0] > 0).astype(jnp.int32)[:, :, :, None, :]
    o = sel_win_attention(q, active, slope_cols, ks, vs_t, kw, vw_t, sel_bias, o_cmp, gates_t, slopes,
                          tq=tq, tk=tk)
    return o.reshape(B * S, q_width)


def kernel(x, attn_norm, mlp_norm, final_norm, ev_w_in, ev_lam_q1, ev_lam_k1, ev_lam_q2, ev_lam_k2,
           ev_subln, ev_w_out, od_w_in, od_cmp_pos_k, od_cmp_k_w1, od_cmp_k_w2, od_cmp_pos_v,
           od_cmp_v_w1, od_cmp_v_w2, od_w_out, mlp_w1, mlp_w2):
    B, S, D = x.shape
    depth = attn_norm.shape[0]
    x2d = x.reshape(B * S, D)
    for layer in range(depth):
        idx = layer // 2
        if layer % 2 == 0:
            mix = even_layer_mix(x2d, B, S, attn_norm[layer], ev_w_in[idx], ev_lam_q1[idx],
                                 ev_lam_k1[idx], ev_lam_q2[idx], ev_lam_k2[idx], ev_subln[idx], layer)
            w_out = ev_w_out[idx]
        else:
            mix = odd_layer_mix(x2d, B, S, attn_norm[layer], od_w_in[idx], od_cmp_pos_k[idx],
                                od_cmp_k_w1[idx], od_cmp_k_w2[idx], od_cmp_pos_v[idx],
                                od_cmp_v_w1[idx], od_cmp_v_w2[idx])
            w_out = od_w_out[idx]
        g_final = final_norm if layer == depth - 1 else None
        x2d = post_block(mix, x2d, w_out.astype(BF16), mlp_norm[layer], mlp_w1[layer].astype(BF16),
                         mlp_w2[layer].astype(BF16), g_final)
    return x2d.reshape(B, S, D)
```

```python
import functools
import math

import jax
import jax.numpy as jnp
import numpy as np
from jax import lax
from jax.experimental import pallas as pl
from jax.experimental.pallas import tpu as pltpu

F32 = jnp.float32
BF16 = jnp.bfloat16

HEAD_DIM = 64
RMS_EPS = 1e-6
NEG_INF = -1e30
FORCE_SCORE = 1e6
NSA_GROUPS = 4
NSA_HPG = 4
CMP_LEN = 32
CMP_STRIDE = 16
SEL_LEN = 64
SEL_TOPK = 16
WINDOW = 512
N_GATES = 3
GATE_PAD = 128
SEL_DROP = 2.0 ** 24
BF16_EXACT_INT = 256
V_ROWS_64 = 80
V_ROWS_128 = 144

LOG2E = math.log2(math.e)
Q_SCALE = HEAD_DIM ** -0.5 * LOG2E
UNDERFLOW_LOG2 = -160.0
SAFE_EXP_LOG2 = 100.0
FAR = 1 << 30
PICKED = -2.0

VMEM_LIMIT = 56 * 1024 * 1024


def _params(n_parallel, n_arbitrary=0):
    return pltpu.CompilerParams(dimension_semantics=("parallel",) * n_parallel + ("arbitrary",) * n_arbitrary,
                                vmem_limit_bytes=VMEM_LIMIT)


def _rms(x, g):
    ms = jnp.mean(x * x, axis=-1, keepdims=True)
    return x * lax.rsqrt(ms + RMS_EPS) * g


def _dot(a, b):
    return jnp.dot(a, b, preferred_element_type=F32)


def _dot_nt(a, b):
    return lax.dot_general(a, b, (((1,), (1,)), ((), ())), preferred_element_type=F32)


def _split_bf16(x):
    hi = x.astype(BF16)
    lo = (x - hi.astype(F32)).astype(BF16)
    return hi, lo


def _slope_pieces(slopes, width):
    def top_bits(x):
        return (x.view(np.uint32) & np.uint32(0xFFFF0000)).view(np.float32)

    s1 = top_bits(slopes)
    r1 = slopes - s1
    s2 = top_bits(r1)
    s3 = top_bits(r1 - s2)
    out = np.zeros((slopes.shape[0], width), np.float32)
    out[:, :6] = np.stack([s1, s2, s3, s1, s2, s3], axis=1)
    return jnp.asarray(out).astype(BF16)


def _key_position_columns(n, tk, width):
    j = jnp.arange(n) % tk
    a = (j // BF16_EXACT_INT) * BF16_EXACT_INT
    b = j % BF16_EXACT_INT
    cols = jnp.stack([a, a, a, b, b, b], axis=1).astype(BF16)
    return jnp.pad(cols, ((0, 0), (0, width - cols.shape[1])))


def _softmax_block(steps, states):
    s_all = [scores() for scores, _, _, _ in steps]
    live = {k: (states[k][0][...], states[k][1][...]) for k in sorted({k for _, _, _, k in steps})}
    for (_, kap, values, k), s_t in zip(steps, s_all):
        m_run, acc = live[k]
        m_new = jnp.maximum(m_run, jnp.max(s_t, axis=0, keepdims=True) + kap)
        p = jnp.exp2(s_t - (m_new - kap)).astype(BF16)
        live[k] = (m_new, jnp.exp2(m_run - m_new) * acc + _dot(values(), p))
    for k, (m_run, acc) in live.items():
        states[k][0][...] = m_run
        states[k][1][...] = acc


def _fixed_frame_block(steps, acc_refs, frames):
    s_all = [scores() for scores, _, _, _ in steps]
    live = {k: acc_refs[k][...] for k in sorted({k for _, _, _, k in steps})}
    for (_, kap, values, k), s_t in zip(steps, s_all):
        live[k] = live[k] + _dot(values(), jnp.exp2(s_t - (frames[k] - kap)).astype(BF16))
    for k, acc in live.items():
        acc_refs[k][...] = acc


def _chunk_loop(n_steps, step, block, groups):
    done = 0
    for group in groups:
        def body(i, carry, group=group, done=done):
            block([step(done + group * i + j) for j in range(group)])
            return carry

        n_groups = (n_steps - done) // group
        lax.fori_loop(0, n_groups, body, 0)
        done = done + group * n_groups


def _ones_row_sums(x):
    return _dot_nt(jnp.ones((8, x.shape[1]), BF16), x.astype(BF16))[0:1]


PROJ_CHUNK = 512


def _ones_row_tail(rows, width):
    return jnp.where(lax.broadcasted_iota(jnp.int32, (rows, width), 0) == 0, 1.0, 0.0).astype(BF16)


def _even_proj_kernel(x_ref, g_ref, w_ref, o_ref, sbv_ref, dfv_ref):
    tm = x_ref.shape[0]
    xn = _rms(x_ref[...], g_ref[...]).astype(BF16)
    tail = _ones_row_tail(V_ROWS_128 - 2 * HEAD_DIM, tm)
    for c in range(w_ref.shape[1] // PROJ_CHUNK):
        sl = slice(c * PROJ_CHUNK, (c + 1) * PROJ_CHUNK)
        res = _dot(xn, w_ref[:, sl])
        if c in (0, 3):
            res = res * Q_SCALE
        o_ref[:, sl] = res.astype(BF16)
        if c == 2:
            sbv_ref[0] = res.T.astype(BF16)
        if c == 5:
            v_t = res.T.astype(BF16)
            for h in range(dfv_ref.shape[1]):
                dfv_ref[0, h, 0:2 * HEAD_DIM, :] = v_t[h * 2 * HEAD_DIM:(h + 1) * 2 * HEAD_DIM]
                dfv_ref[0, h, 2 * HEAD_DIM:V_ROWS_128, :] = tail


def even_proj(x2d, B, S, g, w_bf16, tm=512):
    T, D = x2d.shape
    n_w = w_bf16.shape[1]
    per_seq = S // tm
    assert n_w == 6 * PROJ_CHUNK
    return pl.pallas_call(
        _even_proj_kernel,
        grid=(T // tm,),
        in_specs=[pl.BlockSpec((tm, D), lambda i: (i, 0)),
                  pl.BlockSpec((1, D), lambda i: (0, 0)),
                  pl.BlockSpec((D, n_w), lambda i: (0, 0))],
        out_specs=[pl.BlockSpec((tm, n_w), lambda i: (i, 0)),
                   pl.BlockSpec((1, PROJ_CHUNK, tm), lambda i: (i // per_seq, 0, i % per_seq)),
                   pl.BlockSpec((1, 4, V_ROWS_128, tm), lambda i: (i // per_seq, 0, 0, i % per_seq))],
        out_shape=[jax.ShapeDtypeStruct((T, n_w), BF16),
                   jax.ShapeDtypeStruct((B, PROJ_CHUNK, S), BF16),
                   jax.ShapeDtypeStruct((B, 4, V_ROWS_128, S), BF16)],
        compiler_params=_params(1),
        name="even_proj",
    )(x2d, g.reshape(1, D), w_bf16)


def _odd_proj_kernel(x_ref, g_ref, w_ref, pos_ref, onehot_ref,
                     q_ref, cin_ref, ks_ref, kw_ref, vst_ref, vwt_ref, gate_ref):
    tm = x_ref.shape[0]
    G, Dh = NSA_GROUPS, HEAD_DIM
    kv = G * Dh
    xn = _rms(x_ref[...], g_ref[...]).astype(BF16)
    tail = _ones_row_tail(V_ROWS_64 - Dh, tm)
    pos = pos_ref[...]
    onehot = onehot_ref[...]
    q_width = q_ref.shape[1]
    for c in range(q_width // PROJ_CHUNK):
        sl = slice(c * PROJ_CHUNK, (c + 1) * PROJ_CHUNK)
        q_ref[:, sl] = (_dot(xn, w_ref[:, sl]) * Q_SCALE).astype(BF16)

    def pair(j):
        return _dot(xn, w_ref[:, q_width + j * 2 * kv:q_width + (j + 1) * 2 * kv])

    res = pair(0)
    for which in range(2):
        for g in range(G):
            cin_ref[which, 0, g] = res[:, which * kv + g * Dh:which * kv + (g + 1) * Dh]

    for j, k_ref, vt_ref, extra in ((1, ks_ref, vst_ref, [pos, onehot]), (2, kw_ref, vwt_ref, [pos])):
        res = pair(j)
        v_t = res[:, kv:2 * kv].T.astype(BF16)
        for g in range(G):
            k_ref[0, g] = jnp.concatenate([res[:, g * Dh:(g + 1) * Dh].astype(BF16)] + extra, axis=1)
            vt_ref[0, g, 0:Dh, :] = v_t[g * Dh:(g + 1) * Dh]
            vt_ref[0, g, Dh:V_ROWS_64, :] = tail

    logits = _dot(xn, w_ref[:, q_width + 6 * kv:q_width + 6 * kv + GATE_PAD])
    gate_ref[0] = jax.nn.sigmoid(logits).T[0:gate_ref.shape[1]]


def odd_proj(x2d, B, S, g, w_bf16, pos_cols, onehot, tm=512):
    T, D = x2d.shape
    G, Dh = NSA_GROUPS, HEAD_DIM
    q_width = G * NSA_HPG * Dh
    n_sel = onehot.shape[1]
    per_seq = S // tm
    seq_tile = lambda i: (i // per_seq, 0, i % per_seq, 0)
    seq_tile_t = lambda i: (i // per_seq, 0, 0, i % per_seq)
    return pl.pallas_call(
        _odd_proj_kernel,
        grid=(T // tm,),
        in_specs=[pl.BlockSpec((tm, D), lambda i: (i, 0)),
                  pl.BlockSpec((1, D), lambda i: (0, 0)),
                  pl.BlockSpec(w_bf16.shape, lambda i: (0, 0)),
                  pl.BlockSpec((tm, Dh), lambda i: (i % per_seq, 0)),
                  pl.BlockSpec((tm, n_sel), lambda i: (i % per_seq, 0))],
        out_specs=[pl.BlockSpec((tm, q_width), lambda i: (i, 0)),
                   pl.BlockSpec((2, 1, G, tm, Dh), lambda i: (0, i // per_seq, 0, i % per_seq, 0)),
                   pl.BlockSpec((1, G, tm, 2 * Dh + n_sel), seq_tile),
                   pl.BlockSpec((1, G, tm, 2 * Dh), seq_tile),
                   pl.BlockSpec((1, G, V_ROWS_64, tm), seq_tile_t),
                   pl.BlockSpec((1, G, V_ROWS_64, tm), seq_tile_t),
                   pl.BlockSpec((1, 4 * NSA_HPG * G, tm), lambda i: (i // per_seq, 0, i % per_seq))],
        out_shape=[jax.ShapeDtypeStruct((T, q_width), BF16),
                   jax.ShapeDtypeStruct((2, B, G, S, Dh), F32),
                   jax.ShapeDtypeStruct((B, G, S, 2 * Dh + n_sel), BF16),
                   jax.ShapeDtypeStruct((B, G, S, 2 * Dh), BF16),
                   jax.ShapeDtypeStruct((B, G, V_ROWS_64, S), BF16),
                   jax.ShapeDtypeStruct((B, G, V_ROWS_64, S), BF16),
                   jax.ShapeDtypeStruct((B, 4 * NSA_HPG * G, S), F32)],
        compiler_params=_params(1),
        name="odd_proj",
    )(x2d, g.reshape(1, D), w_bf16, pos_cols, onehot)


def _head_pair_rows(q, t):
    lane = lax.broadcasted_iota(jnp.int32, (t, 2 * HEAD_DIM), 1)
    zero = jnp.zeros_like(q)
    return jnp.where(lane < HEAD_DIM, q, zero), jnp.where(lane >= HEAD_DIM, q, zero)


def _sb_kernel(q_ref, k_ref, vt_ref, o_ref, acc_ref, carry_ref, *, t):
    qi = pl.program_id(2)
    cols = 2 * t
    q_both = jnp.concatenate(_head_pair_rows(q_ref[0], t), axis=0)
    s_idx = lax.broadcasted_iota(jnp.int32, (t, 2 * t), 0)
    j_idx = lax.broadcasted_iota(jnp.int32, (t, 2 * t), 1) % t
    upper2 = jnp.where(j_idx > s_idx, 1.0, 0.0).astype(BF16)
    key = lax.broadcasted_iota(jnp.int32, (t, cols), 0)
    qry = lax.broadcasted_iota(jnp.int32, (t, cols), 1) % t
    past_diag = key < qry

    acc_ref[...] = jnp.zeros_like(acc_ref)
    carry_ref[...] = jnp.zeros_like(carry_ref)

    def chunks(specs):
        starts = [pl.multiple_of(kc * t, t) for kc, _ in specs]
        z_all = [_dot_nt(k_ref[0, pl.ds(start, t), :], q_both) for start in starts]
        carry = carry_ref[...]
        acc = acc_ref[...]
        for (_, masked), start, z in zip(specs, starts, z_all):
            drop = jnp.maximum(z, 0.0) + jnp.log2(1.0 + jnp.exp2(jnp.abs(z) * -1.0))
            log_beta = z - drop
            if masked:
                drop = jnp.where(past_diag, drop, 0.0)
            hi, lo = _split_bf16(drop)
            tail = _dot(upper2, jnp.concatenate([hi, lo], axis=0))
            w = jnp.exp2(log_beta - tail - carry)
            if masked:
                w = jnp.where(past_diag, w, 0.0)
            carry = carry + tail[0:1] + drop[0:1]
            acc = acc + _dot(vt_ref[0, :, pl.ds(start, t)], w.astype(BF16))
        carry_ref[...] = carry
        acc_ref[...] = acc
        return jnp.min(carry)

    least = lax.cond(qi > 0, lambda: chunks([(qi, True), (qi - 1, False)]), lambda: chunks([(qi, True)]))

    def cond(state):
        j, least_carry = state
        return (j < qi) & (least_carry < -UNDERFLOW_LOG2)

    def body(state):
        j, _ = state
        return j + 1, chunks([(qi - 1 - j, False)])

    lax.while_loop(cond, body, (1, least))
    row = lax.broadcasted_iota(jnp.int32, (2 * HEAD_DIM, t), 0)
    o_t = jnp.where(row < HEAD_DIM, acc_ref[:, :t], acc_ref[:, t:])
    o_ref[0] = o_t.T.astype(o_ref.dtype)


def sb_attention(proj, v_t, t=256):
    B, S, _ = proj.shape
    n_pairs = 4
    return pl.pallas_call(
        functools.partial(_sb_kernel, t=t),
        grid=(B, n_pairs, S // t),
        in_specs=[pl.BlockSpec((1, t, 128), lambda b, p, i: (b, i, p)),
                  pl.BlockSpec((1, S, 128), lambda b, p, i: (b, 0, n_pairs + p)),
                  pl.BlockSpec((1, 128, S), lambda b, p, i: (b, p, 0))],
        out_specs=pl.BlockSpec((1, t, 128), lambda b, p, i: (b, i, p)),
        out_shape=jax.ShapeDtypeStruct((B, S, 2 * n_pairs * 128), BF16),
        scratch_shapes=[pltpu.VMEM((128, 2 * t), F32), pltpu.VMEM((1, 2 * t), F32)],
        compiler_params=_params(3),
        name="sb_attention",
    )(proj, proj, v_t)


def _diff_kernel(slopes_ref, lam_ref, q_ref, k_ref, vt_ref, pos_ref, sl_ref, subln_ref, _, o_ref,
                 m_ref, acc_ref, offset_ref, knorm_ref, *, t, out_scale, lambda_init):
    h = pl.program_id(1)
    qi = pl.program_id(2)
    slope = slopes_ref[h]
    cols = 2 * t
    slope_cols = jnp.broadcast_to(sl_ref[0], (t, 128))
    q_parts = _head_pair_rows(q_ref[0], t)
    q_both = jnp.concatenate([jnp.concatenate([qc, slope_cols], axis=1) for qc in q_parts],
                             axis=0)
    lane = lax.broadcasted_iota(jnp.int32, (t, 2 * HEAD_DIM), 1)

    @pl.when(qi == 0)
    def _():
        def body(c, best):
            k = k_ref[0, pl.ds(pl.multiple_of(c * t, t), t), :].astype(F32)
            k2 = k * k
            return (jnp.maximum(best[0], jnp.max(jnp.sum(jnp.where(lane < HEAD_DIM, k2, 0.0), axis=1))),
                    jnp.maximum(best[1], jnp.max(jnp.sum(jnp.where(lane >= HEAD_DIM, k2, 0.0), axis=1))))

        best = lax.fori_loop(0, k_ref.shape[1] // t, body, (jnp.float32(0.0), jnp.float32(0.0)))
        knorm_ref[0] = best[0]
        knorm_ref[1] = best[1]
        offset_ref[...] = (lax.broadcasted_iota(jnp.int32, (t, cols), 0)
                           - lax.broadcasted_iota(jnp.int32, (t, cols), 1) % t)

    q32 = [qc.astype(F32) for qc in q_parts]
    col = lax.broadcasted_iota(jnp.int32, (1, cols), 1)
    k_norm2 = jnp.where(col < t, knorm_ref[0], knorm_ref[1])
    qk_bound = jnp.sqrt(jnp.concatenate([_ones_row_sums(x * x) for x in q32], axis=1) * k_norm2 * 1.05)
    k_self = k_ref[0, pl.ds(pl.multiple_of(qi * t, t), t), :].astype(F32)
    self_score = jnp.concatenate([_ones_row_sums(x * k_self) for x in q32], axis=1)
    safe = jnp.max(qk_bound - self_score) <= SAFE_EXP_LOG2

    pos = pos_ref[...]

    def step(kc, diagonal=False):
        start = pl.multiple_of(kc * t, t)

        def scores():
            s_t = _dot_nt(jnp.concatenate([k_ref[0, pl.ds(start, t), :], pos], axis=1), q_both)
            return jnp.where(offset_ref[...] <= 0, s_t, NEG_INF) if diagonal else s_t

        return scores, slope * ((kc - qi) * t).astype(F32), lambda: vt_ref[0, 0, :, pl.ds(start, t)], 0

    acc_ref[...] = jnp.zeros_like(acc_ref)

    @pl.when(safe)
    def _():
        frame = qk_bound + slope * (col % t).astype(F32)
        block = lambda steps: _fixed_frame_block(steps, [acc_ref], [frame])
        block([step(qi, True)])
        n_back = jnp.ceil((slope * (t - 1) - UNDERFLOW_LOG2) / (slope * t)).astype(jnp.int32)
        _chunk_loop(jnp.minimum(n_back, qi), lambda i: step(qi - 1 - i), block, (8, 4, 2, 1))

    @pl.when(jnp.logical_not(safe))
    def _():
        m_ref[...] = jnp.full_like(m_ref, NEG_INF)
        block = lambda steps: _softmax_block(steps, [(m_ref, acc_ref)])
        block([step(qi, True)])
        excess = jnp.max(qk_bound - m_ref[...]) + slope * (t - 1)
        n_back = jnp.clip(jnp.ceil((excess - UNDERFLOW_LOG2) / (slope * t)).astype(jnp.int32), 0, qi)
        _chunk_loop(n_back, lambda i: step(qi - 1 - i), block, (4, 2, 1))

    lam_terms = lam_ref[...]
    lam = (jnp.exp(jnp.sum(lam_terms[0:1] * lam_terms[1:2], axis=1, keepdims=True))
           - jnp.exp(jnp.sum(lam_terms[2:3] * lam_terms[3:4], axis=1, keepdims=True))
           + lambda_init)
    o_t = acc_ref[0:128, :] / acc_ref[128:129, :]
    o = (o_t[:, :t] - lam * o_t[:, t:]).T
    o_ref[0] = (_rms(o, subln_ref[...]) * out_scale).astype(o_ref.dtype)


def diff_attention(proj, vt_aug, lam_rows, subln, slopes, layer, mix, t=512):
    B, S, _ = proj.shape
    n_heads = 4
    lambda_init = 0.8 - 0.6 * math.exp(-0.3 * layer)
    smem = pl.BlockSpec(memory_space=pltpu.SMEM)
    pos = _key_position_columns(t, t, 128)
    slope_cols = _slope_pieces(slopes, 128).reshape(n_heads, 1, 128)
    return pl.pallas_call(
        functools.partial(_diff_kernel, t=t, out_scale=1.0 - lambda_init, lambda_init=lambda_init),
        grid=(B, n_heads, S // t),
        in_specs=[smem,
                  pl.BlockSpec((4, HEAD_DIM), lambda b, h, i: (0, 0)),
                  pl.BlockSpec((1, t, 128), lambda b, h, i: (b, i, 12 + h)),
                  pl.BlockSpec((1, S, 128), lambda b, h, i: (b, 0, 16 + h)),
                  pl.BlockSpec((1, 1, V_ROWS_128, S), lambda b, h, i: (b, h, 0, 0)),
                  pl.BlockSpec((t, 128), lambda b, h, i: (0, 0)),
                  pl.BlockSpec((1, 1, 128), lambda b, h, i: (h, 0, 0)),
                  pl.BlockSpec((1, 128), lambda b, h, i: (0, 0)),
                  pl.BlockSpec(memory_space=pl.ANY)],
        out_specs=pl.BlockSpec((1, t, 128), lambda b, h, i: (b, i, n_heads + h)),
        out_shape=jax.ShapeDtypeStruct(mix.shape, mix.dtype),
        input_output_aliases={8: 0},
        scratch_shapes=[pltpu.VMEM((1, 2 * t), F32), pltpu.VMEM((V_ROWS_128, 2 * t), F32),
                        pltpu.VMEM((t, 2 * t), jnp.int32), pltpu.SMEM((2,), F32)],
        compiler_params=_params(2, 1),
        name="diff_attention",
    )(slopes, lam_rows, proj, proj, vt_aug, pos, slope_cols, subln.reshape(1, 128), mix)


def _post_kernel(*refs, ff_chunk, final):
    mix_ref, x_ref, wo_ref, g_ref, w1_ref, w2_ref = refs[:6]
    gf_ref = refs[6] if final else None
    o_ref = refs[-1]
    x = x_ref[...] + _dot(mix_ref[...], wo_ref[...])
    hn = _rms(x, g_ref[...]).astype(BF16)
    acc = x
    for f in range(w1_ref.shape[1] // ff_chunk):
        sl = slice(f * ff_chunk, (f + 1) * ff_chunk)
        hid = jnp.maximum(_dot(hn, w1_ref[:, sl]), 0.0)
        acc = acc + _dot((hid * hid).astype(BF16), w2_ref[sl, :])
    if final:
        acc = _rms(acc, gf_ref[...])
    o_ref[...] = acc


def post_block(mix, x2d, w_out, g_mlp, w1, w2, g_final=None, tm=512, ff_chunk=1024):
    T, D = x2d.shape
    final = g_final is not None
    const = lambda i: (0, 0)
    in_specs = [pl.BlockSpec((tm, mix.shape[1]), lambda i: (i, 0)),
                pl.BlockSpec((tm, D), lambda i: (i, 0)),
                pl.BlockSpec(w_out.shape, const),
                pl.BlockSpec((1, D), const), pl.BlockSpec(w1.shape, const), pl.BlockSpec(w2.shape, const)]
    args = [mix, x2d, w_out, g_mlp.reshape(1, D), w1, w2]
    if final:
        in_specs.append(pl.BlockSpec((1, D), const))
        args.append(g_final.reshape(1, D))
    return pl.pallas_call(
        functools.partial(_post_kernel, ff_chunk=ff_chunk, final=final),
        grid=(T // tm,),
        in_specs=in_specs,
        out_specs=pl.BlockSpec((tm, D), lambda i: (i, 0)),
        out_shape=jax.ShapeDtypeStruct((T, D), F32),
        compiler_params=_params(1),
        name="post_block",
    )(*args)


def _compress_kernel(c_ref, pos_ref, w1_ref, w2_ref, o_ref):
    half = w1_ref.shape[1] // 2
    n_chunks = c_ref.shape[3] // CMP_STRIDE
    first = second = None
    for l in range(CMP_STRIDE):
        tok = c_ref[0, 0, 0, pl.ds(l, n_chunks, stride=CMP_STRIDE), :].astype(BF16)
        a = _dot(tok, w1_ref[0, l * HEAD_DIM:(l + 1) * HEAD_DIM, :])
        b = _dot(tok, w1_ref[0, half + l * HEAD_DIM:half + (l + 1) * HEAD_DIM, :])
        first = a if first is None else first + a
        second = b if second is None else second + b
    pos = jnp.broadcast_to(pos_ref[0], (8, 2 * half)).astype(BF16)
    pre = first + pltpu.roll(second, n_chunks - 1, 0) + _dot(pos, w1_ref[0])[0:1]
    hid = jax.nn.gelu(pre)
    o_ref[0, 0] = _dot(hid.astype(BF16), w2_ref[0]).astype(o_ref.dtype)


def compress_kv(tokens, pos_flat, w1, w2):
    _, B, G, S, Dh = tokens.shape
    n_chunks = S // CMP_STRIDE
    hidden = w1.shape[-1]
    return pl.pallas_call(
        _compress_kernel,
        grid=(2, B * G),
        in_specs=[pl.BlockSpec((1, 1, 1, S, Dh), lambda s, i: (s, i // G, i % G, 0, 0)),
                  pl.BlockSpec((1, 1, CMP_LEN * Dh), lambda s, i: (s, 0, 0)),
                  pl.BlockSpec((1, CMP_LEN * Dh, hidden), lambda s, i: (s, 0, 0)),
                  pl.BlockSpec((1, hidden, Dh), lambda s, i: (s, 0, 0))],
        out_specs=pl.BlockSpec((1, 1, n_chunks, Dh), lambda s, i: (s, i, 0, 0)),
        out_shape=jax.ShapeDtypeStruct((2, B * G, n_chunks, Dh), BF16),
        compiler_params=_params(2),
        name="compress_kv",
    )(tokens, pos_flat, w1, w2)


def _stack_heads(q):
    return jnp.concatenate([q[:, r * HEAD_DIM:(r + 1) * HEAD_DIM] for r in range(NSA_HPG)], axis=0)


def _slope_row(slopes_ref, g, tq):
    col = lax.broadcasted_iota(jnp.int32, (1, NSA_HPG * tq), 1)
    out = jnp.zeros((1, NSA_HPG * tq), F32)
    for r in range(NSA_HPG):
        out = jnp.where(col // tq == r, slopes_ref[g * NSA_HPG + r], out)
    return out


def _cmp_select_kernel(slopes_ref, q_ref, kc_ref, vct_ref, ovt_ref, ocmp_ref, sel_ref, hits_ref,
                       dist_ref, imp_ref, knorm_ref,
                       *, tq, n_sel, blocks_per_chunk, hit_tile):
    g = pl.program_id(1)
    t0 = pl.program_id(2) * tq
    cols = NSA_HPG * tq
    n_cmp = kc_ref.shape[2]
    slope_row = _slope_row(slopes_ref, g, tq)
    q = q_ref[0]
    q_rows = _stack_heads(q)

    def scores(row0, n_rows):
        tpos = t0 + lax.broadcasted_iota(jnp.int32, (n_rows, cols), 1) % tq
        cmp_end = ((row0 + lax.broadcasted_iota(jnp.int32, (n_rows, cols), 0)) * CMP_STRIDE
                   + (CMP_LEN - 1))
        dc = (tpos - cmp_end).astype(F32)
        raw = _dot_nt(kc_ref[0, 0, row0:row0 + n_rows, :], q_rows)
        return jnp.where(dc >= 0, raw - slope_row * dc, NEG_INF)

    def group_sum(pc):
        out = pc[:, 0:tq]
        for r in range(1, NSA_HPG):
            out = out + pc[:, r * tq:(r + 1) * tq]
        return out

    @pl.when(pl.program_id(2) == 0)
    def _():
        kc32 = kc_ref[0, 0].astype(F32)
        knorm_ref[0] = jnp.max(jnp.sum(kc32 * kc32, axis=1))
        dist_ref[...] = (lax.broadcasted_iota(jnp.int32, (n_cmp, cols), 1) % tq
                         - lax.broadcasted_iota(jnp.int32, (n_cmp, cols), 0) * CMP_STRIDE
                         - (CMP_LEN - 1)).astype(F32)

    q32 = q.astype(F32)
    q_norm2 = jnp.concatenate([_ones_row_sums((q32 * q32)[:, r * HEAD_DIM:(r + 1) * HEAD_DIM])
                               for r in range(NSA_HPG)], axis=1)
    bound = jnp.sqrt(q_norm2 * knorm_ref[0] * 1.05)
    safe = jnp.max(2.0 * bound + CMP_STRIDE * slope_row) <= SAFE_EXP_LOG2

    def bounded_softmax(n_rows):
        dc = dist_ref[0:n_rows, :] + t0.astype(F32)
        raw = _dot_nt(kc_ref[0, 0, 0:n_rows, :], q_rows)
        e = jnp.exp2(jnp.where(dc >= 0, raw - slope_row * dc - bound, NEG_INF))
        l = jnp.sum(e, axis=0, keepdims=True)
        inv = jnp.where(l > 0, 1.0 / l, 0.0)
        o_t = _dot(vct_ref[0, 0, :, 0:n_rows], e.astype(BF16)) * inv
        for r in range(NSA_HPG):
            ocmp_ref[0, 0, r] = o_t[:, r * tq:(r + 1) * tq]
        hi, lo = _split_bf16(group_sum(e * inv))
        imp_ref[...] = _dot(ovt_ref[:, 0:n_rows], hi) + _dot(ovt_ref[:, 0:n_rows], lo)

    half_enough = (t0 + tq - CMP_LEN) // CMP_STRIDE < n_cmp // 2

    @pl.when(safe & half_enough)
    def _():
        bounded_softmax(n_cmp // 2)

    @pl.when(safe & jnp.logical_not(half_enough))
    def _():
        bounded_softmax(n_cmp)

    @pl.when(jnp.logical_not(safe))
    def _():
        sc = scores(0, n_cmp)
        e = jnp.exp2(sc - jnp.max(sc, axis=0, keepdims=True))
        any_valid = jnp.where(t0 + lax.broadcasted_iota(jnp.int32, (1, cols), 1) % tq >= CMP_LEN - 1, 1.0, 0.0)
        pc = e * (any_valid / jnp.sum(e, axis=0, keepdims=True))
        o_t = _dot(vct_ref[0, 0], pc.astype(BF16))
        for r in range(NSA_HPG):
            ocmp_ref[0, 0, r] = o_t[:, r * tq:(r + 1) * tq]
        hi, lo = _split_bf16(group_sum(pc))
        imp_ref[...] = _dot(ovt_ref[...], hi) + _dot(ovt_ref[...], lo)

    imp = imp_ref[...]
    blk = lax.broadcasted_iota(jnp.int32, (n_sel, tq), 0)
    cur = (t0 + lax.broadcasted_iota(jnp.int32, (n_sel, tq), 1)) // SEL_LEN
    forced = (blk == 0) | (blk == cur) | (blk == cur - 1)
    imp = jnp.where(blk <= cur, imp, -1.0)
    topk = min(SEL_TOPK, n_sel)

    def pick(imp, count):
        for _ in range(count):
            best = jnp.max(imp, axis=0, keepdims=True)
            first = jnp.min(jnp.where(imp == best, blk, n_sel), axis=0, keepdims=True)
            imp = jnp.where(blk == first, PICKED, imp)
        return imp

    imp = lax.cond(t0 >= 2 * SEL_LEN,
                   lambda: pick(jnp.where(forced, PICKED, imp), topk - 3),
                   lambda: pick(jnp.where(forced, FORCE_SCORE, imp), topk))
    keep = (imp == PICKED) & (blk <= cur)
    sel_ref[0, 0] = jnp.where(keep, 0.0, -SEL_DROP).T.astype(sel_ref.dtype)
    n_chunks = n_sel // blocks_per_chunk
    member = (lax.broadcasted_iota(jnp.int32, (n_chunks, n_sel), 1) // blocks_per_chunk
              == lax.broadcasted_iota(jnp.int32, (n_chunks, n_sel), 0))
    per_query = _dot(jnp.where(member, 1.0, 0.0).astype(BF16),
                     jnp.where(keep, 1.0, 0.0).astype(BF16)).astype(BF16)
    for part in range(tq // hit_tile):
        hits_ref[0, 0, part] = _dot(per_query[:, part * hit_tile:(part + 1) * hit_tile],
                                    jnp.ones((hit_tile, 128), BF16))


def cmp_select(q, kc, vc_t, overlap_t, slopes, tk, hit_tile, tq=256):
    B, S, _ = q.shape
    n_cmp = kc.shape[2]
    n_sel = S // SEL_LEN
    G = NSA_GROUPS
    smem = pl.BlockSpec(memory_space=pltpu.SMEM)
    return pl.pallas_call(
        functools.partial(_cmp_select_kernel, tq=tq, n_sel=n_sel, blocks_per_chunk=tk // SEL_LEN,
                          hit_tile=hit_tile),
        grid=(B, G, S // tq),
        in_specs=[smem,
                  pl.BlockSpec((1, tq, 256), lambda b, g, i: (b, i, g)),
                  pl.BlockSpec((1, 1, n_cmp, HEAD_DIM), lambda b, g, i: (b, g, 0, 0)),
                  pl.BlockSpec((1, 1, HEAD_DIM, n_cmp), lambda b, g, i: (b, g, 0, 0)),
                  pl.BlockSpec((n_sel, n_cmp), lambda b, g, i: (0, 0))],
        out_specs=[pl.BlockSpec((1, 1, NSA_HPG, HEAD_DIM, tq), lambda b, g, i: (b, g, 0, 0, i)),
                   pl.BlockSpec((1, 1, tq, n_sel), lambda b, g, i: (b, g, i, 0)),
                   pl.BlockSpec((1, 1, tq // hit_tile, S // tk, 128), lambda b, g, i: (b, g, i, 0, 0))],
        out_shape=[jax.ShapeDtypeStruct((B, G, NSA_HPG, HEAD_DIM, S), F32),
                   jax.ShapeDtypeStruct((B, G, S, n_sel), BF16),
                   jax.ShapeDtypeStruct((B, G, S // hit_tile, S // tk, 128), F32)],
        scratch_shapes=[pltpu.VMEM((n_cmp, NSA_HPG * tq), F32), pltpu.VMEM((n_sel, tq), F32),
                        pltpu.SMEM((1,), F32)],
        compiler_params=_params(2, 1),
        name="cmp_select",
    )(slopes, q, kc, vc_t, overlap_t)


def _sel_win_kernel(slopes_ref, active_ref, q_ref, sl_ref, ks_ref, vst_ref, kw_ref, vwt_ref, sel_ref,
                    ocmp_ref, gate_ref, o_ref, ms_ref, accs_ref, mw_ref, accw_ref, offset_ref, todo_ref,
                    knorm_ref, *, tq, tk):
    g = pl.program_id(1)
    t0 = pl.program_id(2) * tq
    cols = NSA_HPG * tq
    slope_row = _slope_row(slopes_ref, g, tq)
    q = q_ref[0]
    q_aug = jnp.concatenate(
        [jnp.concatenate([q[:, r * HEAD_DIM:(r + 1) * HEAD_DIM],
                          jnp.broadcast_to(sl_ref[0, r:r + 1, :], (tq, HEAD_DIM))], axis=1)
         for r in range(NSA_HPG)], axis=0)
    q_sel = jnp.concatenate([q_aug, jnp.concatenate([sel_ref[0, 0]] * NSA_HPG, axis=0)], axis=1)

    @pl.when(pl.program_id(2) == 0)
    def _():
        offset_ref[...] = (lax.broadcasted_iota(jnp.int32, (tk, cols), 0)
                           - lax.broadcasted_iota(jnp.int32, (tk, cols), 1) % tq)

        def body(c, best):
            rows = pl.ds(pl.multiple_of(c * tk, tk), tk)
            ks = ks_ref[0, 0, rows, :][:, :HEAD_DIM].astype(F32)
            kw = kw_ref[0, 0, rows, :][:, :HEAD_DIM].astype(F32)
            return (jnp.maximum(best[0], jnp.max(jnp.sum(ks * ks, axis=1))),
                    jnp.maximum(best[1], jnp.max(jnp.sum(kw * kw, axis=1))))

        best = lax.fori_loop(0, ks_ref.shape[2] // tk, body, (jnp.float32(0.0), jnp.float32(0.0)))
        knorm_ref[0] = best[0]
        knorm_ref[1] = best[1]

    heads32 = [q.astype(F32)[:, r * HEAD_DIM:(r + 1) * HEAD_DIM] for r in range(NSA_HPG)]
    q_norm2 = jnp.concatenate([_ones_row_sums(x * x) for x in heads32], axis=1)
    own_rows = pl.ds(pl.multiple_of(t0, tq), tq)

    def bound_and_self(k_ref, k_norm2):
        k_self = k_ref[0, 0, own_rows, :][:, :HEAD_DIM].astype(F32)
        own = jnp.concatenate([_ones_row_sums(x * k_self) for x in heads32], axis=1)
        return jnp.sqrt(q_norm2 * k_norm2 * 1.05), own

    sel_bound, sel_self = bound_and_self(ks_ref, knorm_ref[0])
    win_bound, win_self = bound_and_self(kw_ref, knorm_ref[1])
    safe = jnp.maximum(jnp.max(sel_bound - sel_self), jnp.max(win_bound - win_self)) <= SAFE_EXP_LOG2

    def step(c, k_ref, q_rows, vt_ref, keep=None):
        inside = c >= 0
        start = pl.multiple_of(jnp.maximum(c, 0) * tk, tk)
        shift = c * tk - t0

        def scores():
            s_t = _dot_nt(k_ref[0, 0, pl.ds(start, tk), :], q_rows)
            if keep is None:
                return s_t
            lo, hi = keep
            offset = offset_ref[...]
            if hi is not None:
                return jnp.where(offset <= jnp.where(inside, hi - shift, -FAR), s_t, NEG_INF)
            return jnp.where(offset > jnp.where(inside, lo - shift, FAR), s_t, NEG_INF)

        kappa = jnp.where(inside, slope_row * shift.astype(F32), NEG_INF)
        return scores, kappa, lambda: vt_ref[0, 0, :, pl.ds(start, tk)]

    last = t0 // tk
    causal, recent, anything = (None, 0), (-WINDOW, None), (-FAR, None)
    first_steps = [step(last - 2, kw_ref, q_aug, vwt_ref, recent) + (1,),
                   step(last, ks_ref, q_sel, vst_ref, causal) + (0,),
                   step(last - 1, kw_ref, q_aug, vwt_ref, anything) + (1,),
                   step(last, kw_ref, q_aug, vwt_ref, causal) + (1,)]

    def note_active(c, n):
        hit = active_ref[0, 0, 0, 0, c] > 0

        @pl.when(hit)
        def _():
            todo_ref[n] = c

        return n + hit.astype(jnp.int32)

    n_todo = lax.fori_loop(0, last, note_active, 0)
    sel_step = lambda i: step(todo_ref[i], ks_ref, q_sel, vst_ref) + (0,)

    accs_ref[...] = jnp.zeros_like(accs_ref)
    accw_ref[...] = jnp.zeros_like(accw_ref)

    @pl.when(safe)
    def _():
        in_tile = slope_row * (lax.broadcasted_iota(jnp.int32, (1, cols), 1) % tq).astype(F32)
        block = lambda steps: _fixed_frame_block(steps, [accs_ref, accw_ref],
                                                 [sel_bound + in_tile, win_bound + in_tile])
        block(first_steps)
        _chunk_loop(n_todo, sel_step, block, (8, 4, 2, 1))

    @pl.when(jnp.logical_not(safe))
    def _():
        states = [(ms_ref, accs_ref), (mw_ref, accw_ref)]
        for m_ref, _ in states:
            m_ref[...] = jnp.full_like(m_ref, NEG_INF)
        block = lambda steps: _softmax_block(steps, states)
        block(first_steps)
        _chunk_loop(n_todo, sel_step, block, (4, 2, 1))

    o_sel = accs_ref[0:HEAD_DIM, :] / accs_ref[HEAD_DIM:HEAD_DIM + 1, :]
    o_win = accw_ref[0:HEAD_DIM, :] / accw_ref[HEAD_DIM:HEAD_DIM + 1, :]

    gates = gate_ref[0]
    outs = []
    for r in range(NSA_HPG):
        cs = slice(r * tq, (r + 1) * tq)
        outs.append(gates[3 * r:3 * r + 1] * ocmp_ref[0, 0, r] + gates[3 * r + 1:3 * r + 2] * o_sel[:, cs]
                    + gates[3 * r + 2:3 * r + 3] * o_win[:, cs])
    o_ref[0] = jnp.concatenate(outs, axis=0).T.astype(o_ref.dtype)


def sel_win_attention(q, active, slope_cols, ks, vs_t, kw, vw_t, sel_bias, o_cmp, gates_t,
                      slopes, tq=256, tk=256):
    B, S, _ = q.shape
    assert tq == tk and WINDOW == 2 * tk
    G = NSA_GROUPS
    n_sel = S // SEL_LEN
    cols = NSA_HPG * tq
    smem = pl.BlockSpec(memory_space=pltpu.SMEM)
    ks_spec = pl.BlockSpec((1, 1, S, ks.shape[-1]), lambda b, g, i: (b, g, 0, 0))
    kw_spec = pl.BlockSpec((1, 1, S, kw.shape[-1]), lambda b, g, i: (b, g, 0, 0))
    v_spec = pl.BlockSpec((1, 1, V_ROWS_64, S), lambda b, g, i: (b, g, 0, 0))
    return pl.pallas_call(
        functools.partial(_sel_win_kernel, tq=tq, tk=tk),
        grid=(B, G, S // tq),
        in_specs=[smem,
                  pl.BlockSpec((1, 1, 1, 1, S // tk), lambda b, g, i: (b, g, i, 0, 0),
                               memory_space=pltpu.SMEM),
                  pl.BlockSpec((1, tq, 256), lambda b, g, i: (b, i, g)),
                  pl.BlockSpec((1, NSA_HPG, HEAD_DIM), lambda b, g, i: (g, 0, 0)),
                  ks_spec, v_spec, kw_spec, v_spec,
                  pl.BlockSpec((1, 1, tq, n_sel), lambda b, g, i: (b, g, i, 0)),
                  pl.BlockSpec((1, 1, NSA_HPG, HEAD_DIM, tq), lambda b, g, i: (b, g, 0, 0, i)),
                  pl.BlockSpec((1, 4 * NSA_HPG, tq), lambda b, g, i: (b, g, i))],
        out_specs=pl.BlockSpec((1, tq, 256), lambda b, g, i: (b, i, g)),
        out_shape=jax.ShapeDtypeStruct((B, S, G * 256), BF16),
        scratch_shapes=[pltpu.VMEM((1, cols), F32), pltpu.VMEM((V_ROWS_64, cols), F32),
                        pltpu.VMEM((1, cols), F32), pltpu.VMEM((V_ROWS_64, cols), F32),
                        pltpu.VMEM((tk, cols), jnp.int32),
                        pltpu.SMEM((S // tk,), jnp.int32), pltpu.SMEM((2,), F32)],
        compiler_params=_params(2, 1),
        name="sel_win_attention",
    )(slopes, active, q, slope_cols, ks, vs_t, kw, vw_t, sel_bias, o_cmp, gates_t)


def _alibi_slopes_log2(n_heads):
    slopes = np.exp2(-8.0 * (np.arange(n_heads, dtype=np.float32) + 1.0) / n_heads)
    return (slopes.astype(np.float32) * np.float32(LOG2E)).astype(np.float32)


def even_layer_mix(x2d, B, S, norm_g, w_in, lam_q1, lam_k1, lam_q2, lam_k2, subln, layer):
    proj, sb_vt, df_vt = even_proj(x2d, B, S, norm_g, w_in.astype(BF16))
    proj = proj.reshape(B, S, -1)
    lam_rows = jnp.stack([lam_q1, lam_k1, lam_q2, lam_k2]).astype(F32)
    mix = diff_attention(proj, df_vt, lam_rows, subln.astype(F32), _alibi_slopes_log2(4), layer,
                         sb_attention(proj, sb_vt))
    return mix.reshape(B * S, -1)


def odd_layer_mix(x2d, B, S, norm_g, w_in, pos_k, k_w1, k_w2, pos_v, v_w1, v_w2, tq=256, tk=256):
    G, Dh = NSA_GROUPS, HEAD_DIM
    q_width = G * NSA_HPG * Dh
    n_main = q_width + 6 * G * Dh
    per_group = NSA_HPG * N_GATES
    w_gate = w_in[:, n_main:n_main + G * per_group].reshape(-1, G, per_group)
    w_gate = jnp.pad(w_gate, ((0, 0), (0, 0), (0, 4 * NSA_HPG - per_group))).reshape(-1, 4 * NSA_HPG * G)
    w_gate = jnp.pad(w_gate, ((0, 0), (0, GATE_PAD - w_gate.shape[1])))
    w_all = jnp.concatenate([w_in[:, :n_main], w_gate], axis=1).astype(BF16)

    n_chunks = S // CMP_STRIDE
    n_sel = S // SEL_LEN
    onehot = (jnp.arange(S)[:, None] // SEL_LEN == jnp.arange(n_sel)[None, :]).astype(BF16)
    q, cmp_in, ks, kw, vs_t, vw_t, gates_t = odd_proj(x2d, B, S, norm_g, w_all,
                                                      _key_position_columns(S, tk, Dh), onehot)
    q = q.reshape(B, S, q_width)

    pos_flat = jnp.stack([pos_k, pos_v]).reshape(2, 1, CMP_LEN * Dh).astype(F32)
    w1 = jnp.stack([k_w1, v_w1]).astype(BF16)
    w2 = jnp.stack([k_w2, v_w2]).astype(BF16)
    cmp = compress_kv(cmp_in, pos_flat, w1, w2).reshape(2, B, G, n_chunks, Dh)

    cmp_start = jnp.arange(n_chunks) * CMP_STRIDE
    sel_start = jnp.arange(n_sel) * SEL_LEN
    overlap_t = ((cmp_start[None, :] < sel_start[:, None] + SEL_LEN)
                 & (sel_start[:, None] <= cmp_start[None, :] + CMP_LEN - 1)).astype(BF16)
    slopes = _alibi_slopes_log2(G * NSA_HPG)
    slope_cols = _slope_pieces(slopes, Dh).reshape(G, NSA_HPG, Dh)

    o_cmp, sel_bias, hits = cmp_select(q, cmp[0], cmp[1].transpose(0, 1, 3, 2), overlap_t, slopes, tk, tq)
    active = (hits[..., 0] > 0).astype(jnp.int32)[:, :, :, None, :]
    o = sel_win_attention(q, active, slope_cols, ks, vs_t, kw, vw_t, sel_bias, o_cmp, gates_t, slopes,
                          tq=tq, tk=tk)
    return o.reshape(B * S, q_width)


def kernel(x, attn_norm, mlp_norm, final_norm, ev_w_in, ev_lam_q1, ev_lam_k1, ev_lam_q2, ev_lam_k2,
           ev_subln, ev_w_out, od_w_in, od_cmp_pos_k, od_cmp_k_w1, od_cmp_k_w2, od_cmp_pos_v,
           od_cmp_v_w1, od_cmp_v_w2, od_w_out, mlp_w1, mlp_w2):
    B, S, D = x.shape
    depth = attn_norm.shape[0]
    x2d = x.reshape(B * S, D)
    for layer in range(depth):
        idx = layer // 2
        if layer % 2 == 0:
            mix = even_layer_mix(x2d, B, S, attn_norm[layer], ev_w_in[idx], ev_lam_q1[idx],
                                 ev_lam_k1[idx], ev_lam_q2[idx], ev_lam_k2[idx], ev_subln[idx], layer)
            w_out = ev_w_out[idx]
        else:
            mix = odd_layer_mix(x2d, B, S, attn_norm[layer], od_w_in[idx], od_cmp_pos_k[idx],
                                od_cmp_k_w1[idx], od_cmp_k_w2[idx], od_cmp_pos_v[idx],
                                od_cmp_v_w1[idx], od_cmp_v_w2[idx])
            w_out = od_w_out[idx]
        g_final = final_norm if layer == depth - 1 else None
        x2d = post_block(mix, x2d, w_out.astype(BF16), mlp_norm[layer], mlp_w1[layer].astype(BF16),
                         mlp_w2[layer].astype(BF16), g_final)
    return x2d.reshape(B, S, D)
```

```python
import functools
import math

import jax
import jax.numpy as jnp
import numpy as np
from jax import lax
from jax.experimental import pallas as pl
from jax.experimental.pallas import tpu as pltpu

F32 = jnp.float32
BF16 = jnp.bfloat16

HEAD_DIM = 64
RMS_EPS = 1e-6
NEG_INF = -1e30
FORCE_SCORE = 1e6
NSA_GROUPS = 4
NSA_HPG = 4
CMP_LEN = 32
CMP_STRIDE = 16
SEL_LEN = 64
SEL_TOPK = 16
WINDOW = 512
N_GATES = 3
GATE_PAD = 128
SEL_DROP = 2.0 ** 24
BF16_EXACT_INT = 256
V_ROWS_64 = 80
V_ROWS_128 = 144

LOG2E = math.log2(math.e)
Q_SCALE = HEAD_DIM ** -0.5 * LOG2E
UNDERFLOW_LOG2 = -160.0
SAFE_EXP_LOG2 = 100.0
FAR = 1 << 30
PICKED = -2.0

VMEM_LIMIT = 56 * 1024 * 1024


def _params(n_parallel, n_arbitrary=0):
    return pltpu.CompilerParams(dimension_semantics=("parallel",) * n_parallel + ("arbitrary",) * n_arbitrary,
                                vmem_limit_bytes=VMEM_LIMIT)


def _rms(x, g):
    ms = jnp.mean(x * x, axis=-1, keepdims=True)
    return x * lax.rsqrt(ms + RMS_EPS) * g


def _dot(a, b):
    return jnp.dot(a, b, preferred_element_type=F32)


def _dot_nt(a, b):
    return lax.dot_general(a, b, (((1,), (1,)), ((), ())), preferred_element_type=F32)


def _split_bf16(x):
    hi = x.astype(BF16)
    lo = (x - hi.astype(F32)).astype(BF16)
    return hi, lo


def _slope_pieces(slopes, width):
    def top_bits(x):
        return (x.view(np.uint32) & np.uint32(0xFFFF0000)).view(np.float32)

    s1 = top_bits(slopes)
    r1 = slopes - s1
    s2 = top_bits(r1)
    s3 = top_bits(r1 - s2)
    out = np.zeros((slopes.shape[0], width), np.float32)
    out[:, :6] = np.stack([s1, s2, s3, s1, s2, s3], axis=1)
    return jnp.asarray(out).astype(BF16)


def _key_position_columns(n, tk, width):
    j = jnp.arange(n) % tk
    a = (j // BF16_EXACT_INT) * BF16_EXACT_INT
    b = j % BF16_EXACT_INT
    cols = jnp.stack([a, a, a, b, b, b], axis=1).astype(BF16)
    return jnp.pad(cols, ((0, 0), (0, width - cols.shape[1])))


def _softmax_block(steps, states):
    s_all = [scores() for scores, _, _, _ in steps]
    live = {k: (states[k][0][...], states[k][1][...]) for k in sorted({k for _, _, _, k in steps})}
    for (_, kap, values, k), s_t in zip(steps, s_all):
        m_run, acc = live[k]
        m_new = jnp.maximum(m_run, jnp.max(s_t, axis=0, keepdims=True) + kap)
        p = jnp.exp2(s_t - (m_new - kap)).astype(BF16)
        live[k] = (m_new, jnp.exp2(m_run - m_new) * acc + _dot(values(), p))
    for k, (m_run, acc) in live.items():
        states[k][0][...] = m_run
        states[k][1][...] = acc


def _fixed_frame_block(steps, acc_refs, frames):
    s_all = [scores() for scores, _, _, _ in steps]
    live = {k: acc_refs[k][...] for k in sorted({k for _, _, _, k in steps})}
    for (_, kap, values, k), s_t in zip(steps, s_all):
        live[k] = live[k] + _dot(values(), jnp.exp2(s_t - (frames[k] - kap)).astype(BF16))
    for k, acc in live.items():
        acc_refs[k][...] = acc


def _chunk_loop(n_steps, step, block, groups):
    done = 0
    for group in groups:
        def body(i, carry, group=group, done=done):
            block([step(done + group * i + j) for j in range(group)])
            return carry

        n_groups = (n_steps - done) // group
        lax.fori_loop(0, n_groups, body, 0)
        done = done + group * n_groups


def _score_bound(q, k_norm2):
    q32 = q.astype(F32)
    width = q.shape[1]
    member = (lax.broadcasted_iota(jnp.int32, (width, 128), 0) // HEAD_DIM
              == lax.broadcasted_iota(jnp.int32, (width, 128), 1))
    head_norm2 = _dot((q32 * q32).astype(BF16), jnp.where(member, 1.0, 0.0).astype(BF16))
    return jnp.sqrt(jnp.max(head_norm2) * k_norm2 * 1.05)


PROJ_CHUNK = 512


def _ones_row_tail(rows, width):
    return jnp.where(lax.broadcasted_iota(jnp.int32, (rows, width), 0) == 0, 1.0, 0.0).astype(BF16)


def _even_proj_kernel(x_ref, g_ref, w_ref, o_ref, sbv_ref, dfv_ref):
    tm = x_ref.shape[0]
    xn = _rms(x_ref[...], g_ref[...]).astype(BF16)
    tail = _ones_row_tail(V_ROWS_128 - 2 * HEAD_DIM, tm)
    for c in range(w_ref.shape[1] // PROJ_CHUNK):
        sl = slice(c * PROJ_CHUNK, (c + 1) * PROJ_CHUNK)
        res = _dot(xn, w_ref[:, sl])
        if c in (0, 3):
            res = res * Q_SCALE
        o_ref[:, sl] = res.astype(BF16)
        if c == 2:
            sbv_ref[0] = res.T.astype(BF16)
        if c == 5:
            v_t = res.T.astype(BF16)
            for h in range(dfv_ref.shape[1]):
                dfv_ref[0, h, 0:2 * HEAD_DIM, :] = v_t[h * 2 * HEAD_DIM:(h + 1) * 2 * HEAD_DIM]
                dfv_ref[0, h, 2 * HEAD_DIM:V_ROWS_128, :] = tail


def even_proj(x2d, B, S, g, w_bf16, tm=512):
    T, D = x2d.shape
    n_w = w_bf16.shape[1]
    per_seq = S // tm
    assert n_w == 6 * PROJ_CHUNK
    return pl.pallas_call(
        _even_proj_kernel,
        grid=(T // tm,),
        in_specs=[pl.BlockSpec((tm, D), lambda i: (i, 0)),
                  pl.BlockSpec((1, D), lambda i: (0, 0)),
                  pl.BlockSpec((D, n_w), lambda i: (0, 0))],
        out_specs=[pl.BlockSpec((tm, n_w), lambda i: (i, 0)),
                   pl.BlockSpec((1, PROJ_CHUNK, tm), lambda i: (i // per_seq, 0, i % per_seq)),
                   pl.BlockSpec((1, 4, V_ROWS_128, tm), lambda i: (i // per_seq, 0, 0, i % per_seq))],
        out_shape=[jax.ShapeDtypeStruct((T, n_w), BF16),
                   jax.ShapeDtypeStruct((B, PROJ_CHUNK, S), BF16),
                   jax.ShapeDtypeStruct((B, 4, V_ROWS_128, S), BF16)],
        compiler_params=_params(1),
        name="even_proj",
    )(x2d, g.reshape(1, D), w_bf16)


def _odd_proj_kernel(x_ref, g_ref, w_ref, pos_ref, onehot_ref,
                     q_ref, cin_ref, ks_ref, kw_ref, vst_ref, vwt_ref, gate_ref):
    tm = x_ref.shape[0]
    G, Dh = NSA_GROUPS, HEAD_DIM
    kv = G * Dh
    xn = _rms(x_ref[...], g_ref[...]).astype(BF16)
    tail = _ones_row_tail(V_ROWS_64 - Dh, tm)
    pos = pos_ref[...]
    onehot = onehot_ref[...]
    q_width = q_ref.shape[1]
    for c in range(q_width // PROJ_CHUNK):
        sl = slice(c * PROJ_CHUNK, (c + 1) * PROJ_CHUNK)
        q_ref[:, sl] = (_dot(xn, w_ref[:, sl]) * Q_SCALE).astype(BF16)

    def pair(j):
        return _dot(xn, w_ref[:, q_width + j * 2 * kv:q_width + (j + 1) * 2 * kv])

    res = pair(0)
    for which in range(2):
        for g in range(G):
            cin_ref[which, 0, g] = res[:, which * kv + g * Dh:which * kv + (g + 1) * Dh]

    for j, k_ref, vt_ref, extra in ((1, ks_ref, vst_ref, [pos, onehot]), (2, kw_ref, vwt_ref, [pos])):
        res = pair(j)
        v_t = res[:, kv:2 * kv].T.astype(BF16)
        for g in range(G):
            k_ref[0, g] = jnp.concatenate([res[:, g * Dh:(g + 1) * Dh].astype(BF16)] + extra, axis=1)
            vt_ref[0, g, 0:Dh, :] = v_t[g * Dh:(g + 1) * Dh]
            vt_ref[0, g, Dh:V_ROWS_64, :] = tail

    logits = _dot(xn, w_ref[:, q_width + 6 * kv:q_width + 6 * kv + GATE_PAD])
    gate_ref[0] = jax.nn.sigmoid(logits).T[0:gate_ref.shape[1]]


def odd_proj(x2d, B, S, g, w_bf16, pos_cols, onehot, tm=512):
    T, D = x2d.shape
    G, Dh = NSA_GROUPS, HEAD_DIM
    q_width = G * NSA_HPG * Dh
    n_sel = onehot.shape[1]
    per_seq = S // tm
    seq_tile = lambda i: (i // per_seq, 0, i % per_seq, 0)
    seq_tile_t = lambda i: (i // per_seq, 0, 0, i % per_seq)
    return pl.pallas_call(
        _odd_proj_kernel,
        grid=(T // tm,),
        in_specs=[pl.BlockSpec((tm, D), lambda i: (i, 0)),
                  pl.BlockSpec((1, D), lambda i: (0, 0)),
                  pl.BlockSpec(w_bf16.shape, lambda i: (0, 0)),
                  pl.BlockSpec((tm, Dh), lambda i: (i % per_seq, 0)),
                  pl.BlockSpec((tm, n_sel), lambda i: (i % per_seq, 0))],
        out_specs=[pl.BlockSpec((tm, q_width), lambda i: (i, 0)),
                   pl.BlockSpec((2, 1, G, tm, Dh), lambda i: (0, i // per_seq, 0, i % per_seq, 0)),
                   pl.BlockSpec((1, G, tm, 2 * Dh + n_sel), seq_tile),
                   pl.BlockSpec((1, G, tm, 2 * Dh), seq_tile),
                   pl.BlockSpec((1, G, V_ROWS_64, tm), seq_tile_t),
                   pl.BlockSpec((1, G, V_ROWS_64, tm), seq_tile_t),
                   pl.BlockSpec((1, 4 * NSA_HPG * G, tm), lambda i: (i // per_seq, 0, i % per_seq))],
        out_shape=[jax.ShapeDtypeStruct((T, q_width), BF16),
                   jax.ShapeDtypeStruct((2, B, G, S, Dh), F32),
                   jax.ShapeDtypeStruct((B, G, S, 2 * Dh + n_sel), BF16),
                   jax.ShapeDtypeStruct((B, G, S, 2 * Dh), BF16),
                   jax.ShapeDtypeStruct((B, G, V_ROWS_64, S), BF16),
                   jax.ShapeDtypeStruct((B, G, V_ROWS_64, S), BF16),
                   jax.ShapeDtypeStruct((B, 4 * NSA_HPG * G, S), F32)],
        compiler_params=_params(1),
        name="odd_proj",
    )(x2d, g.reshape(1, D), w_bf16, pos_cols, onehot)


def _head_pair_rows(q, t):
    lane = lax.broadcasted_iota(jnp.int32, (t, 2 * HEAD_DIM), 1)
    zero = jnp.zeros_like(q)
    return jnp.where(lane < HEAD_DIM, q, zero), jnp.where(lane >= HEAD_DIM, q, zero)


def _sb_kernel(q_ref, k_ref, vt_ref, o_ref, acc_ref, carry_ref, *, t):
    qi = pl.program_id(2)
    cols = 2 * t
    q_both = jnp.concatenate(_head_pair_rows(q_ref[0], t), axis=0)
    s_idx = lax.broadcasted_iota(jnp.int32, (t, 2 * t), 0)
    j_idx = lax.broadcasted_iota(jnp.int32, (t, 2 * t), 1) % t
    upper2 = jnp.where(j_idx > s_idx, 1.0, 0.0).astype(BF16)
    key = lax.broadcasted_iota(jnp.int32, (t, cols), 0)
    qry = lax.broadcasted_iota(jnp.int32, (t, cols), 1) % t
    past_diag = key < qry

    acc_ref[...] = jnp.zeros_like(acc_ref)
    carry_ref[...] = jnp.zeros_like(carry_ref)

    def chunks(specs):
        starts = [pl.multiple_of(kc * t, t) for kc, _ in specs]
        z_all = [_dot_nt(k_ref[0, pl.ds(start, t), :], q_both) for start in starts]
        carry = carry_ref[...]
        acc = acc_ref[...]
        for (_, masked), start, z in zip(specs, starts, z_all):
            drop = jnp.maximum(z, 0.0) + jnp.log2(1.0 + jnp.exp2(jnp.abs(z) * -1.0))
            log_beta = z - drop
            if masked:
                drop = jnp.where(past_diag, drop, 0.0)
            hi, lo = _split_bf16(drop)
            tail = _dot(upper2, jnp.concatenate([hi, lo], axis=0))
            w = jnp.exp2(log_beta - tail - carry)
            if masked:
                w = jnp.where(past_diag, w, 0.0)
            carry = carry + tail[0:1] + drop[0:1]
            acc = acc + _dot(vt_ref[0, :, pl.ds(start, t)], w.astype(BF16))
        carry_ref[...] = carry
        acc_ref[...] = acc
        return jnp.min(carry)

    least = lax.cond(qi > 0, lambda: chunks([(qi, True), (qi - 1, False)]), lambda: chunks([(qi, True)]))

    def cond(state):
        j, least_carry = state
        return (j < qi) & (least_carry < -UNDERFLOW_LOG2)

    def body(state):
        j, _ = state
        return j + 1, chunks([(qi - 1 - j, False)])

    lax.while_loop(cond, body, (1, least))
    row = lax.broadcasted_iota(jnp.int32, (2 * HEAD_DIM, t), 0)
    o_t = jnp.where(row < HEAD_DIM, acc_ref[:, :t], acc_ref[:, t:])
    o_ref[0] = o_t.T.astype(o_ref.dtype)


def sb_attention(proj, v_t, t=256):
    B, S, _ = proj.shape
    n_pairs = 4
    return pl.pallas_call(
        functools.partial(_sb_kernel, t=t),
        grid=(B, n_pairs, S // t),
        in_specs=[pl.BlockSpec((1, t, 128), lambda b, p, i: (b, i, p)),
                  pl.BlockSpec((1, S, 128), lambda b, p, i: (b, 0, n_pairs + p)),
                  pl.BlockSpec((1, 128, S), lambda b, p, i: (b, p, 0))],
        out_specs=pl.BlockSpec((1, t, 128), lambda b, p, i: (b, i, p)),
        out_shape=jax.ShapeDtypeStruct((B, S, 2 * n_pairs * 128), BF16),
        scratch_shapes=[pltpu.VMEM((128, 2 * t), F32), pltpu.VMEM((1, 2 * t), F32)],
        compiler_params=_params(3),
        name="sb_attention",
    )(proj, proj, v_t)


def _diff_kernel(slopes_ref, lam_ref, q_ref, k_ref, vt_ref, pos_ref, sl_ref, subln_ref, _, o_ref,
                 m_ref, acc_ref, offset_ref, knorm_ref, *, t, out_scale, lambda_init):
    h = pl.program_id(1)
    qi = pl.program_id(2)
    slope = slopes_ref[h]
    cols = 2 * t
    slope_cols = jnp.broadcast_to(sl_ref[0], (t, 128))
    q_parts = _head_pair_rows(q_ref[0], t)
    q_both = jnp.concatenate([jnp.concatenate([qc, slope_cols], axis=1) for qc in q_parts],
                             axis=0)
    lane = lax.broadcasted_iota(jnp.int32, (t, 2 * HEAD_DIM), 1)

    @pl.when(qi == 0)
    def _():
        def body(c, best):
            k = k_ref[0, pl.ds(pl.multiple_of(c * t, t), t), :].astype(F32)
            k2 = k * k
            return (jnp.maximum(best[0], jnp.max(jnp.sum(jnp.where(lane < HEAD_DIM, k2, 0.0), axis=1))),
                    jnp.maximum(best[1], jnp.max(jnp.sum(jnp.where(lane >= HEAD_DIM, k2, 0.0), axis=1))))

        best = lax.fori_loop(0, k_ref.shape[1] // t, body, (jnp.float32(0.0), jnp.float32(0.0)))
        knorm_ref[0] = best[0]
        knorm_ref[1] = best[1]
        offset_ref[...] = (lax.broadcasted_iota(jnp.int32, (t, cols), 0)
                           - lax.broadcasted_iota(jnp.int32, (t, cols), 1) % t)

    qk_bound = _score_bound(q_ref[0], jnp.maximum(knorm_ref[0], knorm_ref[1]))
    safe = 2.0 * qk_bound <= SAFE_EXP_LOG2
    col = lax.broadcasted_iota(jnp.int32, (1, cols), 1)

    pos = pos_ref[...]

    def step(kc, diagonal=False):
        start = pl.multiple_of(kc * t, t)

        def scores():
            s_t = _dot_nt(jnp.concatenate([k_ref[0, pl.ds(start, t), :], pos], axis=1), q_both)
            return jnp.where(offset_ref[...] <= 0, s_t, NEG_INF) if diagonal else s_t

        return scores, slope * ((kc - qi) * t).astype(F32), lambda: vt_ref[0, 0, :, pl.ds(start, t)], 0

    acc_ref[...] = jnp.zeros_like(acc_ref)

    @pl.when(safe)
    def _():
        frame = qk_bound + slope * (col % t).astype(F32)
        block = lambda steps: _fixed_frame_block(steps, [acc_ref], [frame])
        block([step(qi, True)])
        n_back = jnp.ceil((slope * (t - 1) - UNDERFLOW_LOG2) / (slope * t)).astype(jnp.int32)
        _chunk_loop(jnp.minimum(n_back, qi), lambda i: step(qi - 1 - i), block, (8, 4, 2, 1))

    @pl.when(jnp.logical_not(safe))
    def _():
        m_ref[...] = jnp.full_like(m_ref, NEG_INF)
        block = lambda steps: _softmax_block(steps, [(m_ref, acc_ref)])
        block([step(qi, True)])
        excess = jnp.max(qk_bound - m_ref[...]) + slope * (t - 1)
        n_back = jnp.clip(jnp.ceil((excess - UNDERFLOW_LOG2) / (slope * t)).astype(jnp.int32), 0, qi)
        _chunk_loop(n_back, lambda i: step(qi - 1 - i), block, (4, 2, 1))

    lam_terms = lam_ref[...]
    lam = (jnp.exp(jnp.sum(lam_terms[0:1] * lam_terms[1:2], axis=1, keepdims=True))
           - jnp.exp(jnp.sum(lam_terms[2:3] * lam_terms[3:4], axis=1, keepdims=True))
           + lambda_init)
    o_t = acc_ref[0:128, :] / acc_ref[128:129, :]
    o = (o_t[:, :t] - lam * o_t[:, t:]).T
    o_ref[0] = (_rms(o, subln_ref[...]) * out_scale).astype(o_ref.dtype)


def diff_attention(proj, vt_aug, lam_rows, subln, slopes, layer, mix, t=512):
    B, S, _ = proj.shape
    n_heads = 4
    lambda_init = 0.8 - 0.6 * math.exp(-0.3 * layer)
    smem = pl.BlockSpec(memory_space=pltpu.SMEM)
    pos = _key_position_columns(t, t, 128)
    slope_cols = _slope_pieces(slopes, 128).reshape(n_heads, 1, 128)
    return pl.pallas_call(
        functools.partial(_diff_kernel, t=t, out_scale=1.0 - lambda_init, lambda_init=lambda_init),
        grid=(B, n_heads, S // t),
        in_specs=[smem,
                  pl.BlockSpec((4, HEAD_DIM), lambda b, h, i: (0, 0)),
                  pl.BlockSpec((1, t, 128), lambda b, h, i: (b, i, 12 + h)),
                  pl.BlockSpec((1, S, 128), lambda b, h, i: (b, 0, 16 + h)),
                  pl.BlockSpec((1, 1, V_ROWS_128, S), lambda b, h, i: (b, h, 0, 0)),
                  pl.BlockSpec((t, 128), lambda b, h, i: (0, 0)),
                  pl.BlockSpec((1, 1, 128), lambda b, h, i: (h, 0, 0)),
                  pl.BlockSpec((1, 128), lambda b, h, i: (0, 0)),
                  pl.BlockSpec(memory_space=pl.ANY)],
        out_specs=pl.BlockSpec((1, t, 128), lambda b, h, i: (b, i, n_heads + h)),
        out_shape=jax.ShapeDtypeStruct(mix.shape, mix.dtype),
        input_output_aliases={8: 0},
        scratch_shapes=[pltpu.VMEM((1, 2 * t), F32), pltpu.VMEM((V_ROWS_128, 2 * t), F32),
                        pltpu.VMEM((t, 2 * t), jnp.int32), pltpu.SMEM((2,), F32)],
        compiler_params=_params(2, 1),
        name="diff_attention",
    )(slopes, lam_rows, proj, proj, vt_aug, pos, slope_cols, subln.reshape(1, 128), mix)


def _post_kernel(*refs, ff_chunk, final):
    mix_ref, x_ref, wo_ref, g_ref, w1_ref, w2_ref = refs[:6]
    gf_ref = refs[6] if final else None
    o_ref = refs[-1]
    x = x_ref[...] + _dot(mix_ref[...], wo_ref[...])
    hn = _rms(x, g_ref[...]).astype(BF16)
    acc = x
    for f in range(w1_ref.shape[1] // ff_chunk):
        sl = slice(f * ff_chunk, (f + 1) * ff_chunk)
        hid = jnp.maximum(_dot(hn, w1_ref[:, sl]), 0.0)
        acc = acc + _dot((hid * hid).astype(BF16), w2_ref[sl, :])
    if final:
        acc = _rms(acc, gf_ref[...])
    o_ref[...] = acc


def post_block(mix, x2d, w_out, g_mlp, w1, w2, g_final=None, tm=512, ff_chunk=1024):
    T, D = x2d.shape
    final = g_final is not None
    const = lambda i: (0, 0)
    in_specs = [pl.BlockSpec((tm, mix.shape[1]), lambda i: (i, 0)),
                pl.BlockSpec((tm, D), lambda i: (i, 0)),
                pl.BlockSpec(w_out.shape, const),
                pl.BlockSpec((1, D), const), pl.BlockSpec(w1.shape, const), pl.BlockSpec(w2.shape, const)]
    args = [mix, x2d, w_out, g_mlp.reshape(1, D), w1, w2]
    if final:
        in_specs.append(pl.BlockSpec((1, D), const))
        args.append(g_final.reshape(1, D))
    return pl.pallas_call(
        functools.partial(_post_kernel, ff_chunk=ff_chunk, final=final),
        grid=(T // tm,),
        in_specs=in_specs,
        out_specs=pl.BlockSpec((tm, D), lambda i: (i, 0)),
        out_shape=jax.ShapeDtypeStruct((T, D), F32),
        compiler_params=_params(1),
        name="post_block",
    )(*args)


def _compress_kernel(c_ref, pos_ref, w1_ref, w2_ref, o_ref):
    half = w1_ref.shape[1] // 2
    n_chunks = c_ref.shape[3] // CMP_STRIDE
    first = second = None
    for l in range(CMP_STRIDE):
        tok = c_ref[0, 0, 0, pl.ds(l, n_chunks, stride=CMP_STRIDE), :].astype(BF16)
        a = _dot(tok, w1_ref[0, l * HEAD_DIM:(l + 1) * HEAD_DIM, :])
        b = _dot(tok, w1_ref[0, half + l * HEAD_DIM:half + (l + 1) * HEAD_DIM, :])
        first = a if first is None else first + a
        second = b if second is None else second + b
    pos = jnp.broadcast_to(pos_ref[0], (8, 2 * half)).astype(BF16)
    pre = first + pltpu.roll(second, n_chunks - 1, 0) + _dot(pos, w1_ref[0])[0:1]
    hid = jax.nn.gelu(pre)
    o_ref[0, 0] = _dot(hid.astype(BF16), w2_ref[0]).astype(o_ref.dtype)


def compress_kv(tokens, pos_flat, w1, w2):
    _, B, G, S, Dh = tokens.shape
    n_chunks = S // CMP_STRIDE
    hidden = w1.shape[-1]
    return pl.pallas_call(
        _compress_kernel,
        grid=(2, B * G),
        in_specs=[pl.BlockSpec((1, 1, 1, S, Dh), lambda s, i: (s, i // G, i % G, 0, 0)),
                  pl.BlockSpec((1, 1, CMP_LEN * Dh), lambda s, i: (s, 0, 0)),
                  pl.BlockSpec((1, CMP_LEN * Dh, hidden), lambda s, i: (s, 0, 0)),
                  pl.BlockSpec((1, hidden, Dh), lambda s, i: (s, 0, 0))],
        out_specs=pl.BlockSpec((1, 1, n_chunks, Dh), lambda s, i: (s, i, 0, 0)),
        out_shape=jax.ShapeDtypeStruct((2, B * G, n_chunks, Dh), BF16),
        compiler_params=_params(2),
        name="compress_kv",
    )(tokens, pos_flat, w1, w2)


def _stack_heads(q):
    return jnp.concatenate([q[:, r * HEAD_DIM:(r + 1) * HEAD_DIM] for r in range(NSA_HPG)], axis=0)


def _slope_row(slopes_ref, g, tq):
    col = lax.broadcasted_iota(jnp.int32, (1, NSA_HPG * tq), 1)
    out = jnp.zeros((1, NSA_HPG * tq), F32)
    for r in range(NSA_HPG):
        out = jnp.where(col // tq == r, slopes_ref[g * NSA_HPG + r], out)
    return out


def _cmp_select_kernel(slopes_ref, q_ref, kc_ref, vct_ref, ovt_ref, ocmp_ref, sel_ref, hits_ref,
                       dist_ref, imp_ref, knorm_ref,
                       *, tq, n_sel, blocks_per_chunk, hit_tile):
    g = pl.program_id(1)
    t0 = pl.program_id(2) * tq
    cols = NSA_HPG * tq
    n_cmp = kc_ref.shape[2]
    slope_row = _slope_row(slopes_ref, g, tq)
    q = q_ref[0]
    q_rows = _stack_heads(q)

    def scores(row0, n_rows):
        tpos = t0 + lax.broadcasted_iota(jnp.int32, (n_rows, cols), 1) % tq
        cmp_end = ((row0 + lax.broadcasted_iota(jnp.int32, (n_rows, cols), 0)) * CMP_STRIDE
                   + (CMP_LEN - 1))
        dc = (tpos - cmp_end).astype(F32)
        raw = _dot_nt(kc_ref[0, 0, row0:row0 + n_rows, :], q_rows)
        return jnp.where(dc >= 0, raw - slope_row * dc, NEG_INF)

    def group_sum(pc):
        out = pc[:, 0:tq]
        for r in range(1, NSA_HPG):
            out = out + pc[:, r * tq:(r + 1) * tq]
        return out

    @pl.when(pl.program_id(2) == 0)
    def _():
        kc32 = kc_ref[0, 0].astype(F32)
        knorm_ref[0] = jnp.max(jnp.sum(kc32 * kc32, axis=1))
        dist_ref[...] = (lax.broadcasted_iota(jnp.int32, (n_cmp, cols), 1) % tq
                         - lax.broadcasted_iota(jnp.int32, (n_cmp, cols), 0) * CMP_STRIDE
                         - (CMP_LEN - 1)).astype(F32)

    bound = _score_bound(q, knorm_ref[0])
    safe = 2.0 * bound + CMP_STRIDE * slopes_ref[g * NSA_HPG] <= SAFE_EXP_LOG2

    def bounded_softmax(n_rows):
        dc = dist_ref[0:n_rows, :] + t0.astype(F32)
        raw = _dot_nt(kc_ref[0, 0, 0:n_rows, :], q_rows)
        e = jnp.exp2(jnp.where(dc >= 0, raw - slope_row * dc - bound, NEG_INF))
        l = jnp.sum(e, axis=0, keepdims=True)
        inv = jnp.where(l > 0, 1.0 / l, 0.0)
        o_t = _dot(vct_ref[0, 0, :, 0:n_rows], e.astype(BF16)) * inv
        for r in range(NSA_HPG):
            ocmp_ref[0, 0, r] = o_t[:, r * tq:(r + 1) * tq]
        hi, lo = _split_bf16(group_sum(e * inv))
        imp_ref[...] = _dot(ovt_ref[:, 0:n_rows], hi) + _dot(ovt_ref[:, 0:n_rows], lo)

    half_enough = (t0 + tq - CMP_LEN) // CMP_STRIDE < n_cmp // 2

    @pl.when(safe & half_enough)
    def _():
        bounded_softmax(n_cmp // 2)

    @pl.when(safe & jnp.logical_not(half_enough))
    def _():
        bounded_softmax(n_cmp)

    @pl.when(jnp.logical_not(safe))
    def _():
        sc = scores(0, n_cmp)
        e = jnp.exp2(sc - jnp.max(sc, axis=0, keepdims=True))
        any_valid = jnp.where(t0 + lax.broadcasted_iota(jnp.int32, (1, cols), 1) % tq >= CMP_LEN - 1, 1.0, 0.0)
        pc = e * (any_valid / jnp.sum(e, axis=0, keepdims=True))
        o_t = _dot(vct_ref[0, 0], pc.astype(BF16))
        for r in range(NSA_HPG):
            ocmp_ref[0, 0, r] = o_t[:, r * tq:(r + 1) * tq]
        hi, lo = _split_bf16(group_sum(pc))
        imp_ref[...] = _dot(ovt_ref[...], hi) + _dot(ovt_ref[...], lo)

    imp = imp_ref[...]
    blk = lax.broadcasted_iota(jnp.int32, (n_sel, tq), 0)
    cur = (t0 + lax.broadcasted_iota(jnp.int32, (n_sel, tq), 1)) // SEL_LEN
    forced = (blk == 0) | (blk == cur) | (blk == cur - 1)
    imp = jnp.where(blk <= cur, imp, -1.0)
    topk = min(SEL_TOPK, n_sel)

    def pick(imp, count):
        for _ in range(count):
            best = jnp.max(imp, axis=0, keepdims=True)
            first = jnp.min(jnp.where(imp == best, blk, n_sel), axis=0, keepdims=True)
            imp = jnp.where(blk == first, PICKED, imp)
        return imp

    imp = lax.cond(t0 >= 2 * SEL_LEN,
                   lambda: pick(jnp.where(forced, PICKED, imp), topk - 3),
                   lambda: pick(jnp.where(forced, FORCE_SCORE, imp), topk))
    keep = (imp == PICKED) & (blk <= cur)
    sel_ref[0, 0] = jnp.where(keep, 0.0, -SEL_DROP).T.astype(sel_ref.dtype)
    n_chunks = n_sel // blocks_per_chunk
    member = (lax.broadcasted_iota(jnp.int32, (n_chunks, n_sel), 1) // blocks_per_chunk
              == lax.broadcasted_iota(jnp.int32, (n_chunks, n_sel), 0))
    per_query = _dot(jnp.where(member, 1.0, 0.0).astype(BF16),
                     jnp.where(keep, 1.0, 0.0).astype(BF16)).astype(BF16)
    for part in range(tq // hit_tile):
        hits_ref[0, 0, part] = _dot(per_query[:, part * hit_tile:(part + 1) * hit_tile],
                                    jnp.ones((hit_tile, 128), BF16))


def cmp_select(q, kc, vc_t, overlap_t, slopes, tk, hit_tile, tq=256):
    B, S, _ = q.shape
    n_cmp = kc.shape[2]
    n_sel = S // SEL_LEN
    G = NSA_GROUPS
    smem = pl.BlockSpec(memory_space=pltpu.SMEM)
    return pl.pallas_call(
        functools.partial(_cmp_select_kernel, tq=tq, n_sel=n_sel, blocks_per_chunk=tk // SEL_LEN,
                          hit_tile=hit_tile),
        grid=(B, G, S // tq),
        in_specs=[smem,
                  pl.BlockSpec((1, tq, 256), lambda b, g, i: (b, i, g)),
                  pl.BlockSpec((1, 1, n_cmp, HEAD_DIM), lambda b, g, i: (b, g, 0, 0)),
                  pl.BlockSpec((1, 1, HEAD_DIM, n_cmp), lambda b, g, i: (b, g, 0, 0)),
                  pl.BlockSpec((n_sel, n_cmp), lambda b, g, i: (0, 0))],
        out_specs=[pl.BlockSpec((1, 1, NSA_HPG, HEAD_DIM, tq), lambda b, g, i: (b, g, 0, 0, i)),
                   pl.BlockSpec((1, 1, tq, n_sel), lambda b, g, i: (b, g, i, 0)),
                   pl.BlockSpec((1, 1, tq // hit_tile, S // tk, 128), lambda b, g, i: (b, g, i, 0, 0))],
        out_shape=[jax.ShapeDtypeStruct((B, G, NSA_HPG, HEAD_DIM, S), F32),
                   jax.ShapeDtypeStruct((B, G, S, n_sel), BF16),
                   jax.ShapeDtypeStruct((B, G, S // hit_tile, S // tk, 128), F32)],
        scratch_shapes=[pltpu.VMEM((n_cmp, NSA_HPG * tq), F32), pltpu.VMEM((n_sel, tq), F32),
                        pltpu.SMEM((1,), F32)],
        compiler_params=_params(2, 1),
        name="cmp_select",
    )(slopes, q, kc, vc_t, overlap_t)


def _sel_win_kernel(slopes_ref, active_ref, q_ref, sl_ref, ks_ref, vst_ref, kw_ref, vwt_ref, sel_ref,
                    ocmp_ref, gate_ref, o_ref, ms_ref, accs_ref, mw_ref, accw_ref, offset_ref, todo_ref,
                    knorm_ref, *, tq, tk):
    g = pl.program_id(1)
    t0 = pl.program_id(2) * tq
    cols = NSA_HPG * tq
    slope_row = _slope_row(slopes_ref, g, tq)
    q = q_ref[0]
    q_aug = jnp.concatenate(
        [jnp.concatenate([q[:, r * HEAD_DIM:(r + 1) * HEAD_DIM],
                          jnp.broadcast_to(sl_ref[0, r:r + 1, :], (tq, HEAD_DIM))], axis=1)
         for r in range(NSA_HPG)], axis=0)
    q_sel = jnp.concatenate([q_aug, jnp.concatenate([sel_ref[0, 0]] * NSA_HPG, axis=0)], axis=1)

    @pl.when(pl.program_id(2) == 0)
    def _():
        offset_ref[...] = (lax.broadcasted_iota(jnp.int32, (tk, cols), 0)
                           - lax.broadcasted_iota(jnp.int32, (tk, cols), 1) % tq)

        def body(c, best):
            rows = pl.ds(pl.multiple_of(c * tk, tk), tk)
            ks = ks_ref[0, 0, rows, :][:, :HEAD_DIM].astype(F32)
            kw = kw_ref[0, 0, rows, :][:, :HEAD_DIM].astype(F32)
            return (jnp.maximum(best[0], jnp.max(jnp.sum(ks * ks, axis=1))),
                    jnp.maximum(best[1], jnp.max(jnp.sum(kw * kw, axis=1))))

        best = lax.fori_loop(0, ks_ref.shape[2] // tk, body, (jnp.float32(0.0), jnp.float32(0.0)))
        knorm_ref[0] = best[0]
        knorm_ref[1] = best[1]

    sel_bound = _score_bound(q, knorm_ref[0])
    win_bound = _score_bound(q, knorm_ref[1])
    safe = 2.0 * jnp.maximum(sel_bound, win_bound) <= SAFE_EXP_LOG2

    def step(c, k_ref, q_rows, vt_ref, keep=None):
        inside = c >= 0
        start = pl.multiple_of(jnp.maximum(c, 0) * tk, tk)
        shift = c * tk - t0

        def scores():
            s_t = _dot_nt(k_ref[0, 0, pl.ds(start, tk), :], q_rows)
            if keep is None:
                return s_t
            lo, hi = keep
            offset = offset_ref[...]
            if hi is not None:
                return jnp.where(offset <= jnp.where(inside, hi - shift, -FAR), s_t, NEG_INF)
            return jnp.where(offset > jnp.where(inside, lo - shift, FAR), s_t, NEG_INF)

        kappa = jnp.where(inside, slope_row * shift.astype(F32), NEG_INF)
        return scores, kappa, lambda: vt_ref[0, 0, :, pl.ds(start, tk)]

    last = t0 // tk
    causal, recent, anything = (None, 0), (-WINDOW, None), (-FAR, None)
    first_steps = [step(last - 2, kw_ref, q_aug, vwt_ref, recent) + (1,),
                   step(last, ks_ref, q_sel, vst_ref, causal) + (0,),
                   step(last - 1, kw_ref, q_aug, vwt_ref, anything) + (1,),
                   step(last, kw_ref, q_aug, vwt_ref, causal) + (1,)]

    def note_active(c, n):
        hit = active_ref[0, 0, 0, 0, c] > 0

        @pl.when(hit)
        def _():
            todo_ref[n] = c

        return n + hit.astype(jnp.int32)

    n_todo = lax.fori_loop(0, last, note_active, 0)
    sel_step = lambda i: step(todo_ref[i], ks_ref, q_sel, vst_ref) + (0,)

    accs_ref[...] = jnp.zeros_like(accs_ref)
    accw_ref[...] = jnp.zeros_like(accw_ref)

    @pl.when(safe)
    def _():
        in_tile = slope_row * (lax.broadcasted_iota(jnp.int32, (1, cols), 1) % tq).astype(F32)
        block = lambda steps: _fixed_frame_block(steps, [accs_ref, accw_ref],
                                                 [sel_bound + in_tile, win_bound + in_tile])
        block(first_steps)
        _chunk_loop(n_todo, sel_step, block, (8, 4, 2, 1))

    @pl.when(jnp.logical_not(safe))
    def _():
        states = [(ms_ref, accs_ref), (mw_ref, accw_ref)]
        for m_ref, _ in states:
            m_ref[...] = jnp.full_like(m_ref, NEG_INF)
        block = lambda steps: _softmax_block(steps, states)
        block(first_steps)
        _chunk_loop(n_todo, sel_step, block, (4, 2, 1))

    o_sel = accs_ref[0:HEAD_DIM, :] / accs_ref[HEAD_DIM:HEAD_DIM + 1, :]
    o_win = accw_ref[0:HEAD_DIM, :] / accw_ref[HEAD_DIM:HEAD_DIM + 1, :]

    gates = gate_ref[0]
    outs = []
    for r in range(NSA_HPG):
        cs = slice(r * tq, (r + 1) * tq)
        outs.append(gates[3 * r:3 * r + 1] * ocmp_ref[0, 0, r] + gates[3 * r + 1:3 * r + 2] * o_sel[:, cs]
                    + gates[3 * r + 2:3 * r + 3] * o_win[:, cs])
    o_ref[0] = jnp.concatenate(outs, axis=0).T.astype(o_ref.dtype)


def sel_win_attention(q, active, slope_cols, ks, vs_t, kw, vw_t, sel_bias, o_cmp, gates_t,
                      slopes, tq=256, tk=256):
    B, S, _ = q.shape
    assert tq == tk and WINDOW == 2 * tk
    G = NSA_GROUPS
    n_sel = S // SEL_LEN
    cols = NSA_HPG * tq
    smem = pl.BlockSpec(memory_space=pltpu.SMEM)
    ks_spec = pl.BlockSpec((1, 1, S, ks.shape[-1]), lambda b, g, i: (b, g, 0, 0))
    kw_spec = pl.BlockSpec((1, 1, S, kw.shape[-1]), lambda b, g, i: (b, g, 0, 0))
    v_spec = pl.BlockSpec((1, 1, V_ROWS_64, S), lambda b, g, i: (b, g, 0, 0))
    return pl.pallas_call(
        functools.partial(_sel_win_kernel, tq=tq, tk=tk),
        grid=(B, G, S // tq),
        in_specs=[smem,
                  pl.BlockSpec((1, 1, 1, 1, S // tk), lambda b, g, i: (b, g, i, 0, 0),
                               memory_space=pltpu.SMEM),
                  pl.BlockSpec((1, tq, 256), lambda b, g, i: (b, i, g)),
                  pl.BlockSpec((1, NSA_HPG, HEAD_DIM), lambda b, g, i: (g, 0, 0)),
                  ks_spec, v_spec, kw_spec, v_spec,
                  pl.BlockSpec((1, 1, tq, n_sel), lambda b, g, i: (b, g, i, 0)),
                  pl.BlockSpec((1, 1, NSA_HPG, HEAD_DIM, tq), lambda b, g, i: (b, g, 0, 0, i)),
                  pl.BlockSpec((1, 4 * NSA_HPG, tq), lambda b, g, i: (b, g, i))],
        out_specs=pl.BlockSpec((1, tq, 256), lambda b, g, i: (b, i, g)),
        out_shape=jax.ShapeDtypeStruct((B, S, G * 256), BF16),
        scratch_shapes=[pltpu.VMEM((1, cols), F32), pltpu.VMEM((V_ROWS_64, cols), F32),
                        pltpu.VMEM((1, cols), F32), pltpu.VMEM((V_ROWS_64, cols), F32),
                        pltpu.VMEM((tk, cols), jnp.int32),
                        pltpu.SMEM((S // tk,), jnp.int32), pltpu.SMEM((2,), F32)],
        compiler_params=_params(2, 1),
        name="sel_win_attention",
    )(slopes, active, q, slope_cols, ks, vs_t, kw, vw_t, sel_bias, o_cmp, gates_t)


def _alibi_slopes_log2(n_heads):
    slopes = np.exp2(-8.0 * (np.arange(n_heads, dtype=np.float32) + 1.0) / n_heads)
    return (slopes.astype(np.float32) * np.float32(LOG2E)).astype(np.float32)


def even_layer_mix(x2d, B, S, norm_g, w_in, lam_q1, lam_k1, lam_q2, lam_k2, subln, layer):
    proj, sb_vt, df_vt = even_proj(x2d, B, S, norm_g, w_in.astype(BF16))
    proj = proj.reshape(B, S, -1)
    lam_rows = jnp.stack([lam_q1, lam_k1, lam_q2, lam_k2]).astype(F32)
    mix = diff_attention(proj, df_vt, lam_rows, subln.astype(F32), _alibi_slopes_log2(4), layer,
                         sb_attention(proj, sb_vt))
    return mix.reshape(B * S, -1)


def odd_layer_mix(x2d, B, S, norm_g, w_in, pos_k, k_w1, k_w2, pos_v, v_w1, v_w2, tq=256, tk=256):
    G, Dh = NSA_GROUPS, HEAD_DIM
    q_width = G * NSA_HPG * Dh
    n_main = q_width + 6 * G * Dh
    per_group = NSA_HPG * N_GATES
    w_gate = w_in[:, n_main:n_main + G * per_group].reshape(-1, G, per_group)
    w_gate = jnp.pad(w_gate, ((0, 0), (0, 0), (0, 4 * NSA_HPG - per_group))).reshape(-1, 4 * NSA_HPG * G)
    w_gate = jnp.pad(w_gate, ((0, 0), (0, GATE_PAD - w_gate.shape[1])))
    w_all = jnp.concatenate([w_in[:, :n_main], w_gate], axis=1).astype(BF16)

    n_chunks = S // CMP_STRIDE
    n_sel = S // SEL_LEN
    onehot = (jnp.arange(S)[:, None] // SEL_LEN == jnp.arange(n_sel)[None, :]).astype(BF16)
    q, cmp_in, ks, kw, vs_t, vw_t, gates_t = odd_proj(x2d, B, S, norm_g, w_all,
                                                      _key_position_columns(S, tk, Dh), onehot)
    q = q.reshape(B, S, q_width)

    pos_flat = jnp.stack([pos_k, pos_v]).reshape(2, 1, CMP_LEN * Dh).astype(F32)
    w1 = jnp.stack([k_w1, v_w1]).astype(BF16)
    w2 = jnp.stack([k_w2, v_w2]).astype(BF16)
    cmp = compress_kv(cmp_in, pos_flat, w1, w2).reshape(2, B, G, n_chunks, Dh)

    cmp_start = jnp.arange(n_chunks) * CMP_STRIDE
    sel_start = jnp.arange(n_sel) * SEL_LEN
    overlap_t = ((cmp_start[None, :] < sel_start[:, None] + SEL_LEN)
                 & (sel_start[:, None] <= cmp_start[None, :] + CMP_LEN - 1)).astype(BF16)
    slopes = _alibi_slopes_log2(G * NSA_HPG)
    slope_cols = _slope_pieces(slopes, Dh).reshape(G, NSA_HPG, Dh)

    o_cmp, sel_bias, hits = cmp_select(q, cmp[0], cmp[1].transpose(0, 1, 3, 2), overlap_t, slopes, tk, tq)
    active = (hits[..., 0] > 0).astype(jnp.int32)[:, :, :, None, :]
    o = sel_win_attention(q, active, slope_cols, ks, vs_t, kw, vw_t, sel_bias, o_cmp, gates_t, slopes,
                          tq=tq, tk=tk)
    return o.reshape(B * S, q_width)


def kernel(x, attn_norm, mlp_norm, final_norm, ev_w_in, ev_lam_q1, ev_lam_k1, ev_lam_q2, ev_lam_k2,
           ev_subln, ev_w_out, od_w_in, od_cmp_pos_k, od_cmp_k_w1, od_cmp_k_w2, od_cmp_pos_v,
           od_cmp_v_w1, od_cmp_v_w2, od_w_out, mlp_w1, mlp_w2):
    B, S, D = x.shape
    depth = attn_norm.shape[0]
    x2d = x.reshape(B * S, D)
    for layer in range(depth):
        idx = layer // 2
        if layer % 2 == 0:
            mix = even_layer_mix(x2d, B, S, attn_norm[layer], ev_w_in[idx], ev_lam_q1[idx],
                                 ev_lam_k1[idx], ev_lam_q2[idx], ev_lam_k2[idx], ev_subln[idx], layer)
            w_out = ev_w_out[idx]
        else:
            mix = odd_layer_mix(x2d, B, S, attn_norm[layer], od_w_in[idx], od_cmp_pos_k[idx],
                                od_cmp_k_w1[idx], od_cmp_k_w2[idx], od_cmp_pos_v[idx],
                                od_cmp_v_w1[idx], od_cmp_v_w2[idx])
            w_out = od_w_out[idx]
        g_final = final_norm if layer == depth - 1 else None
        x2d = post_block(mix, x2d, w_out.astype(BF16), mlp_norm[layer], mlp_w1[layer].astype(BF16),
                         mlp_w2[layer].astype(BF16), g_final)
    return x2d.reshape(B, S, D)
```

```python
import functools
import math

import jax
import jax.numpy as jnp
import numpy as np
from jax import lax
from jax.experimental import pallas as pl
from jax.experimental.pallas import tpu as pltpu

F32 = jnp.float32
BF16 = jnp.bfloat16

HEAD_DIM = 64
RMS_EPS = 1e-6
NEG_INF = -1e30
FORCE_SCORE = 1e6
NSA_GROUPS = 4
NSA_HPG = 4
CMP_LEN = 32
CMP_STRIDE = 16
SEL_LEN = 64
SEL_TOPK = 16
WINDOW = 512
N_GATES = 3
GATE_PAD = 128
SEL_DROP = 2.0 ** 24
BF16_EXACT_INT = 256
V_ROWS_64 = 80
V_ROWS_128 = 144

LOG2E = math.log2(math.e)
Q_SCALE = HEAD_DIM ** -0.5 * LOG2E
UNDERFLOW_LOG2 = -160.0
SAFE_EXP_LOG2 = 100.0
FAR = 1 << 30
PICKED = -2.0

VMEM_LIMIT = 56 * 1024 * 1024


def _params(n_parallel, n_arbitrary=0):
    return pltpu.CompilerParams(dimension_semantics=("parallel",) * n_parallel + ("arbitrary",) * n_arbitrary,
                                vmem_limit_bytes=VMEM_LIMIT)


def _rms(x, g):
    ms = jnp.mean(x * x, axis=-1, keepdims=True)
    return x * lax.rsqrt(ms + RMS_EPS) * g


def _dot(a, b):
    return jnp.dot(a, b, preferred_element_type=F32)


def _dot_nt(a, b):
    return lax.dot_general(a, b, (((1,), (1,)), ((), ())), preferred_element_type=F32)


def _split_bf16(x):
    hi = x.astype(BF16)
    lo = (x - hi.astype(F32)).astype(BF16)
    return hi, lo


def _slope_pieces(slopes, width):
    def top_bits(x):
        return (x.view(np.uint32) & np.uint32(0xFFFF0000)).view(np.float32)

    s1 = top_bits(slopes)
    r1 = slopes - s1
    s2 = top_bits(r1)
    s3 = top_bits(r1 - s2)
    out = np.zeros((slopes.shape[0], width), np.float32)
    out[:, :6] = np.stack([s1, s2, s3, s1, s2, s3], axis=1)
    return jnp.asarray(out).astype(BF16)


def _key_position_columns(n, tk, width):
    j = jnp.arange(n) % tk
    a = (j // BF16_EXACT_INT) * BF16_EXACT_INT
    b = j % BF16_EXACT_INT
    cols = jnp.stack([a, a, a, b, b, b], axis=1).astype(BF16)
    return jnp.pad(cols, ((0, 0), (0, width - cols.shape[1])))


def _softmax_block(steps, states):
    s_all = [scores() for scores, _, _, _ in steps]
    live = {k: (states[k][0][...], states[k][1][...]) for k in sorted({k for _, _, _, k in steps})}
    for (_, kap, values, k), s_t in zip(steps, s_all):
        m_run, acc = live[k]
        m_new = jnp.maximum(m_run, jnp.max(s_t, axis=0, keepdims=True) + kap)
        p = jnp.exp2(s_t - (m_new - kap)).astype(BF16)
        live[k] = (m_new, jnp.exp2(m_run - m_new) * acc + _dot(values(), p))
    for k, (m_run, acc) in live.items():
        states[k][0][...] = m_run
        states[k][1][...] = acc


def _fixed_frame_block(steps, acc_refs, frames):
    s_all = [scores() for scores, _, _, _ in steps]
    live = {k: acc_refs[k][...] for k in sorted({k for _, _, _, k in steps})}
    for (_, kap, values, k), s_t in zip(steps, s_all):
        live[k] = live[k] + _dot(values(), jnp.exp2(s_t - (frames[k] - kap)).astype(BF16))
    for k, acc in live.items():
        acc_refs[k][...] = acc


def _chunk_loop(n_steps, step, block, groups):
    done = 0
    for group in groups:
        def body(i, carry, group=group, done=done):
            block([step(done + group * i + j) for j in range(group)])
            return carry

        n_groups = (n_steps - done) // group
        lax.fori_loop(0, n_groups, body, 0)
        done = done + group * n_groups


def _score_bound(q, k_norm2):
    q32 = q.astype(F32)
    width = q.shape[1]
    member = (lax.broadcasted_iota(jnp.int32, (width, 128), 0) // HEAD_DIM
              == lax.broadcasted_iota(jnp.int32, (width, 128), 1))
    head_norm2 = _dot((q32 * q32).astype(BF16), jnp.where(member, 1.0, 0.0).astype(BF16))
    return jnp.sqrt(jnp.max(head_norm2) * k_norm2 * 1.05)


PROJ_CHUNK = 512


def _ones_row_tail(rows, width):
    return jnp.where(lax.broadcasted_iota(jnp.int32, (rows, width), 0) == 0, 1.0, 0.0).astype(BF16)


def _even_proj_kernel(x_ref, g_ref, w_ref, o_ref, sbv_ref, dfv_ref):
    tm = x_ref.shape[0]
    xn = _rms(x_ref[...], g_ref[...]).astype(BF16)
    tail = _ones_row_tail(V_ROWS_128 - 2 * HEAD_DIM, tm)
    for c in range(w_ref.shape[1] // PROJ_CHUNK):
        sl = slice(c * PROJ_CHUNK, (c + 1) * PROJ_CHUNK)
        res = _dot(xn, w_ref[:, sl])
        if c in (0, 3):
            res = res * Q_SCALE
        o_ref[:, sl] = res.astype(BF16)
        if c == 2:
            sbv_ref[0] = res.T.astype(BF16)
        if c == 5:
            v_t = res.T.astype(BF16)
            for h in range(dfv_ref.shape[1]):
                dfv_ref[0, h, 0:2 * HEAD_DIM, :] = v_t[h * 2 * HEAD_DIM:(h + 1) * 2 * HEAD_DIM]
                dfv_ref[0, h, 2 * HEAD_DIM:V_ROWS_128, :] = tail


def even_proj(x2d, B, S, g, w_bf16, tm=512):
    T, D = x2d.shape
    n_w = w_bf16.shape[1]
    per_seq = S // tm
    assert n_w == 6 * PROJ_CHUNK
    return pl.pallas_call(
        _even_proj_kernel,
        grid=(T // tm,),
        in_specs=[pl.BlockSpec((tm, D), lambda i: (i, 0)),
                  pl.BlockSpec((1, D), lambda i: (0, 0)),
                  pl.BlockSpec((D, n_w), lambda i: (0, 0))],
        out_specs=[pl.BlockSpec((tm, n_w), lambda i: (i, 0)),
                   pl.BlockSpec((1, PROJ_CHUNK, tm), lambda i: (i // per_seq, 0, i % per_seq)),
                   pl.BlockSpec((1, 4, V_ROWS_128, tm), lambda i: (i // per_seq, 0, 0, i % per_seq))],
        out_shape=[jax.ShapeDtypeStruct((T, n_w), BF16),
                   jax.ShapeDtypeStruct((B, PROJ_CHUNK, S), BF16),
                   jax.ShapeDtypeStruct((B, 4, V_ROWS_128, S), BF16)],
        compiler_params=_params(1),
        name="even_proj",
    )(x2d, g.reshape(1, D), w_bf16)


def _odd_proj_kernel(x_ref, g_ref, w_ref, pos_ref, onehot_ref,
                     q_ref, cin_ref, ks_ref, kw_ref, vst_ref, vwt_ref, gate_ref):
    tm = x_ref.shape[0]
    G, Dh = NSA_GROUPS, HEAD_DIM
    kv = G * Dh
    xn = _rms(x_ref[...], g_ref[...]).astype(BF16)
    tail = _ones_row_tail(V_ROWS_64 - Dh, tm)
    pos = pos_ref[...]
    onehot = onehot_ref[...]
    q_width = q_ref.shape[1]
    for c in range(q_width // PROJ_CHUNK):
        sl = slice(c * PROJ_CHUNK, (c + 1) * PROJ_CHUNK)
        q_ref[:, sl] = (_dot(xn, w_ref[:, sl]) * Q_SCALE).astype(BF16)

    def pair(j):
        return _dot(xn, w_ref[:, q_width + j * 2 * kv:q_width + (j + 1) * 2 * kv])

    res = pair(0)
    for which in range(2):
        for g in range(G):
            cin_ref[which, 0, g] = res[:, which * kv + g * Dh:which * kv + (g + 1) * Dh]

    for j, k_ref, vt_ref, extra in ((1, ks_ref, vst_ref, [pos, onehot]), (2, kw_ref, vwt_ref, [pos])):
        res = pair(j)
        v_t = res[:, kv:2 * kv].T.astype(BF16)
        for g in range(G):
            k_ref[0, g] = jnp.concatenate([res[:, g * Dh:(g + 1) * Dh].astype(BF16)] + extra, axis=1)
            vt_ref[0, g, 0:Dh, :] = v_t[g * Dh:(g + 1) * Dh]
            vt_ref[0, g, Dh:V_ROWS_64, :] = tail

    logits = _dot(xn, w_ref[:, q_width + 6 * kv:q_width + 6 * kv + GATE_PAD])
    gate_ref[0] = jax.nn.sigmoid(logits).T[0:gate_ref.shape[1]]


def odd_proj(x2d, B, S, g, w_bf16, pos_cols, onehot, tm=512):
    T, D = x2d.shape
    G, Dh = NSA_GROUPS, HEAD_DIM
    q_width = G * NSA_HPG * Dh
    n_sel = onehot.shape[1]
    per_seq = S // tm
    seq_tile = lambda i: (i // per_seq, 0, i % per_seq, 0)
    seq_tile_t = lambda i: (i // per_seq, 0, 0, i % per_seq)
    return pl.pallas_call(
        _odd_proj_kernel,
        grid=(T // tm,),
        in_specs=[pl.BlockSpec((tm, D), lambda i: (i, 0)),
                  pl.BlockSpec((1, D), lambda i: (0, 0)),
                  pl.BlockSpec(w_bf16.shape, lambda i: (0, 0)),
                  pl.BlockSpec((tm, Dh), lambda i: (i % per_seq, 0)),
                  pl.BlockSpec((tm, n_sel), lambda i: (i % per_seq, 0))],
        out_specs=[pl.BlockSpec((tm, q_width), lambda i: (i, 0)),
                   pl.BlockSpec((2, 1, G, tm, Dh), lambda i: (0, i // per_seq, 0, i % per_seq, 0)),
                   pl.BlockSpec((1, G, tm, 2 * Dh + n_sel), seq_tile),
                   pl.BlockSpec((1, G, tm, 2 * Dh), seq_tile),
                   pl.BlockSpec((1, G, V_ROWS_64, tm), seq_tile_t),
                   pl.BlockSpec((1, G, V_ROWS_64, tm), seq_tile_t),
                   pl.BlockSpec((1, 4 * NSA_HPG * G, tm), lambda i: (i // per_seq, 0, i % per_seq))],
        out_shape=[jax.ShapeDtypeStruct((T, q_width), BF16),
                   jax.ShapeDtypeStruct((2, B, G, S, Dh), F32),
                   jax.ShapeDtypeStruct((B, G, S, 2 * Dh + n_sel), BF16),
                   jax.ShapeDtypeStruct((B, G, S, 2 * Dh), BF16),
                   jax.ShapeDtypeStruct((B, G, V_ROWS_64, S), BF16),
                   jax.ShapeDtypeStruct((B, G, V_ROWS_64, S), BF16),
                   jax.ShapeDtypeStruct((B, 4 * NSA_HPG * G, S), F32)],
        compiler_params=_params(1),
        name="odd_proj",
    )(x2d, g.reshape(1, D), w_bf16, pos_cols, onehot)


def _head_pair_rows(q, t):
    lane = lax.broadcasted_iota(jnp.int32, (t, 2 * HEAD_DIM), 1)
    zero = jnp.zeros_like(q)
    return jnp.where(lane < HEAD_DIM, q, zero), jnp.where(lane >= HEAD_DIM, q, zero)


def _sb_kernel(q_ref, k_ref, vt_ref, o_ref, acc_ref, carry_ref, *, t):
    qi = pl.program_id(2)
    cols = 2 * t
    q_both = jnp.concatenate(_head_pair_rows(q_ref[0], t), axis=0)
    s_idx = lax.broadcasted_iota(jnp.int32, (t, 2 * t), 0)
    j_idx = lax.broadcasted_iota(jnp.int32, (t, 2 * t), 1) % t
    upper2 = jnp.where(j_idx > s_idx, 1.0, 0.0).astype(BF16)
    key = lax.broadcasted_iota(jnp.int32, (t, cols), 0)
    qry = lax.broadcasted_iota(jnp.int32, (t, cols), 1) % t
    past_diag = key < qry

    acc_ref[...] = jnp.zeros_like(acc_ref)
    carry_ref[...] = jnp.zeros_like(carry_ref)

    def chunks(specs):
        starts = [pl.multiple_of(kc * t, t) for kc, _ in specs]
        z_all = [_dot_nt(k_ref[0, pl.ds(start, t), :], q_both) for start in starts]
        carry = carry_ref[...]
        acc = acc_ref[...]
        for (_, masked), start, z in zip(specs, starts, z_all):
            drop = jnp.maximum(z, 0.0) + jnp.log2(1.0 + jnp.exp2(jnp.abs(z) * -1.0))
            log_beta = z - drop
            if masked:
                drop = jnp.where(past_diag, drop, 0.0)
            hi, lo = _split_bf16(drop)
            tail = _dot(upper2, jnp.concatenate([hi, lo], axis=0))
            w = jnp.exp2(log_beta - tail - carry)
            if masked:
                w = jnp.where(past_diag, w, 0.0)
            carry = carry + tail[0:1] + drop[0:1]
            acc = acc + _dot(vt_ref[0, :, pl.ds(start, t)], w.astype(BF16))
        carry_ref[...] = carry
        acc_ref[...] = acc
        return jnp.min(carry)

    least = lax.cond(qi > 0, lambda: chunks([(qi, True), (qi - 1, False)]), lambda: chunks([(qi, True)]))

    def cond(state):
        j, least_carry = state
        return (j < qi) & (least_carry < -UNDERFLOW_LOG2)

    def body(state):
        j, _ = state
        return j + 1, chunks([(qi - 1 - j, False)])

    lax.while_loop(cond, body, (1, least))
    row = lax.broadcasted_iota(jnp.int32, (2 * HEAD_DIM, t), 0)
    o_t = jnp.where(row < HEAD_DIM, acc_ref[:, :t], acc_ref[:, t:])
    o_ref[0] = o_t.T.astype(o_ref.dtype)


def sb_attention(proj, v_t, t=256):
    B, S, _ = proj.shape
    n_pairs = 4
    return pl.pallas_call(
        functools.partial(_sb_kernel, t=t),
        grid=(B, n_pairs, S // t),
        in_specs=[pl.BlockSpec((1, t, 128), lambda b, p, i: (b, i, p)),
                  pl.BlockSpec((1, S, 128), lambda b, p, i: (b, 0, n_pairs + p)),
                  pl.BlockSpec((1, 128, S), lambda b, p, i: (b, p, 0))],
        out_specs=pl.BlockSpec((1, t, 128), lambda b, p, i: (b, i, p)),
        out_shape=jax.ShapeDtypeStruct((B, S, 2 * n_pairs * 128), BF16),
        scratch_shapes=[pltpu.VMEM((128, 2 * t), F32), pltpu.VMEM((1, 2 * t), F32)],
        compiler_params=_params(3),
        name="sb_attention",
    )(proj, proj, v_t)


def _diff_kernel(slopes_ref, lam_ref, q_ref, k_ref, vt_ref, pos_ref, sl_ref, subln_ref, _, o_ref,
                 m_ref, acc_ref, offset_ref, knorm_ref, *, t, out_scale, lambda_init):
    h = pl.program_id(1)
    qi = pl.program_id(2)
    slope = slopes_ref[h]
    cols = 2 * t
    slope_cols = jnp.broadcast_to(sl_ref[0], (t, 128))
    q_parts = _head_pair_rows(q_ref[0], t)
    q_both = jnp.concatenate([jnp.concatenate([qc, slope_cols], axis=1) for qc in q_parts],
                             axis=0)
    lane = lax.broadcasted_iota(jnp.int32, (t, 2 * HEAD_DIM), 1)

    @pl.when(qi == 0)
    def _():
        def body(c, best):
            k = k_ref[0, pl.ds(pl.multiple_of(c * t, t), t), :].astype(F32)
            k2 = k * k
            return (jnp.maximum(best[0], jnp.max(jnp.sum(jnp.where(lane < HEAD_DIM, k2, 0.0), axis=1))),
                    jnp.maximum(best[1], jnp.max(jnp.sum(jnp.where(lane >= HEAD_DIM, k2, 0.0), axis=1))))

        best = lax.fori_loop(0, k_ref.shape[1] // t, body, (jnp.float32(0.0), jnp.float32(0.0)))
        knorm_ref[0] = best[0]
        knorm_ref[1] = best[1]
        offset_ref[...] = (lax.broadcasted_iota(jnp.int32, (t, cols), 0)
                           - lax.broadcasted_iota(jnp.int32, (t, cols), 1) % t)

    qk_bound = _score_bound(q_ref[0], jnp.maximum(knorm_ref[0], knorm_ref[1]))
    safe = 2.0 * qk_bound <= SAFE_EXP_LOG2
    col = lax.broadcasted_iota(jnp.int32, (1, cols), 1)

    pos = pos_ref[...]

    def step(kc, diagonal=False):
        start = pl.multiple_of(kc * t, t)

        def scores():
            s_t = _dot_nt(jnp.concatenate([k_ref[0, pl.ds(start, t), :], pos], axis=1), q_both)
            return jnp.where(offset_ref[...] <= 0, s_t, NEG_INF) if diagonal else s_t

        return scores, slope * ((kc - qi) * t).astype(F32), lambda: vt_ref[0, 0, :, pl.ds(start, t)], 0

    acc_ref[...] = jnp.zeros_like(acc_ref)

    @pl.when(safe)
    def _():
        frame = qk_bound + slope * (col % t).astype(F32)
        block = lambda steps: _fixed_frame_block(steps, [acc_ref], [frame])
        block([step(qi, True)])
        n_back = jnp.ceil((slope * (t - 1) - UNDERFLOW_LOG2) / (slope * t)).astype(jnp.int32)
        _chunk_loop(jnp.minimum(n_back, qi), lambda i: step(qi - 1 - i), block, (8, 4, 2, 1))

    @pl.when(jnp.logical_not(safe))
    def _():
        m_ref[...] = jnp.full_like(m_ref, NEG_INF)
        block = lambda steps: _softmax_block(steps, [(m_ref, acc_ref)])
        block([step(qi, True)])
        excess = jnp.max(qk_bound - m_ref[...]) + slope * (t - 1)
        n_back = jnp.clip(jnp.ceil((excess - UNDERFLOW_LOG2) / (slope * t)).astype(jnp.int32), 0, qi)
        _chunk_loop(n_back, lambda i: step(qi - 1 - i), block, (4, 2, 1))

    lam_terms = lam_ref[...]
    lam = (jnp.exp(jnp.sum(lam_terms[0:1] * lam_terms[1:2], axis=1, keepdims=True))
           - jnp.exp(jnp.sum(lam_terms[2:3] * lam_terms[3:4], axis=1, keepdims=True))
           + lambda_init)
    o_t = acc_ref[0:128, :] / acc_ref[128:129, :]
    o = (o_t[:, :t] - lam * o_t[:, t:]).T
    o_ref[0] = (_rms(o, subln_ref[...]) * out_scale).astype(o_ref.dtype)


def diff_attention(proj, vt_aug, lam_rows, subln, slopes, layer, mix, t=512):
    B, S, _ = proj.shape
    n_heads = 4
    lambda_init = 0.8 - 0.6 * math.exp(-0.3 * layer)
    smem = pl.BlockSpec(memory_space=pltpu.SMEM)
    pos = _key_position_columns(t, t, 128)
    slope_cols = _slope_pieces(slopes, 128).reshape(n_heads, 1, 128)
    return pl.pallas_call(
        functools.partial(_diff_kernel, t=t, out_scale=1.0 - lambda_init, lambda_init=lambda_init),
        grid=(B, n_heads, S // t),
        in_specs=[smem,
                  pl.BlockSpec((4, HEAD_DIM), lambda b, h, i: (0, 0)),
                  pl.BlockSpec((1, t, 128), lambda b, h, i: (b, i, 12 + h)),
                  pl.BlockSpec((1, S, 128), lambda b, h, i: (b, 0, 16 + h)),
                  pl.BlockSpec((1, 1, V_ROWS_128, S), lambda b, h, i: (b, h, 0, 0)),
                  pl.BlockSpec((t, 128), lambda b, h, i: (0, 0)),
                  pl.BlockSpec((1, 1, 128), lambda b, h, i: (h, 0, 0)),
                  pl.BlockSpec((1, 128), lambda b, h, i: (0, 0)),
                  pl.BlockSpec(memory_space=pl.ANY)],
        out_specs=pl.BlockSpec((1, t, 128), lambda b, h, i: (b, i, n_heads + h)),
        out_shape=jax.ShapeDtypeStruct(mix.shape, mix.dtype),
        input_output_aliases={8: 0},
        scratch_shapes=[pltpu.VMEM((1, 2 * t), F32), pltpu.VMEM((V_ROWS_128, 2 * t), F32),
                        pltpu.VMEM((t, 2 * t), jnp.int32), pltpu.SMEM((2,), F32)],
        compiler_params=_params(2, 1),
        name="diff_attention",
    )(slopes, lam_rows, proj, proj, vt_aug, pos, slope_cols, subln.reshape(1, 128), mix)


def _post_kernel(*refs, ff_chunk, final):
    mix_ref, x_ref, wo_ref, g_ref, w1_ref, w2_ref = refs[:6]
    gf_ref = refs[6] if final else None
    o_ref = refs[-1]
    x = x_ref[...] + _dot(mix_ref[...], wo_ref[...])
    hn = _rms(x, g_ref[...]).astype(BF16)
    acc = x
    for f in range(w1_ref.shape[1] // ff_chunk):
        sl = slice(f * ff_chunk, (f + 1) * ff_chunk)
        hid = jnp.maximum(_dot(hn, w1_ref[:, sl]), 0.0)
        acc = acc + _dot((hid * hid).astype(BF16), w2_ref[sl, :])
    if final:
        acc = _rms(acc, gf_ref[...])
    o_ref[...] = acc


def post_block(mix, x2d, w_out, g_mlp, w1, w2, g_final=None, tm=512, ff_chunk=1024):
    T, D = x2d.shape
    final = g_final is not None
    const = lambda i: (0, 0)
    in_specs = [pl.BlockSpec((tm, mix.shape[1]), lambda i: (i, 0)),
                pl.BlockSpec((tm, D), lambda i: (i, 0)),
                pl.BlockSpec(w_out.shape, const),
                pl.BlockSpec((1, D), const), pl.BlockSpec(w1.shape, const), pl.BlockSpec(w2.shape, const)]
    args = [mix, x2d, w_out, g_mlp.reshape(1, D), w1, w2]
    if final:
        in_specs.append(pl.BlockSpec((1, D), const))
        args.append(g_final.reshape(1, D))
    return pl.pallas_call(
        functools.partial(_post_kernel, ff_chunk=ff_chunk, final=final),
        grid=(T // tm,),
        in_specs=in_specs,
        out_specs=pl.BlockSpec((tm, D), lambda i: (i, 0)),
        out_shape=jax.ShapeDtypeStruct((T, D), F32),
        compiler_params=_params(1),
        name="post_block",
    )(*args)


def _compress_kernel(c_ref, pos_ref, w1_ref, w2_ref, o_ref):
    half = w1_ref.shape[1] // 2
    n_chunks = c_ref.shape[3] // CMP_STRIDE
    first = second = None
    for l in range(CMP_STRIDE):
        tok = c_ref[0, 0, 0, pl.ds(l, n_chunks, stride=CMP_STRIDE), :].astype(BF16)
        a = _dot(tok, w1_ref[0, l * HEAD_DIM:(l + 1) * HEAD_DIM, :])
        b = _dot(tok, w1_ref[0, half + l * HEAD_DIM:half + (l + 1) * HEAD_DIM, :])
        first = a if first is None else first + a
        second = b if second is None else second + b
    pos = jnp.broadcast_to(pos_ref[0], (8, 2 * half)).astype(BF16)
    pre = first + pltpu.roll(second, n_chunks - 1, 0) + _dot(pos, w1_ref[0])[0:1]
    hid = jax.nn.gelu(pre)
    o_ref[0, 0] = _dot(hid.astype(BF16), w2_ref[0]).astype(o_ref.dtype)


def compress_kv(tokens, pos_flat, w1, w2):
    _, B, G, S, Dh = tokens.shape
    n_chunks = S // CMP_STRIDE
    hidden = w1.shape[-1]
    return pl.pallas_call(
        _compress_kernel,
        grid=(2, B * G),
        in_specs=[pl.BlockSpec((1, 1, 1, S, Dh), lambda s, i: (s, i // G, i % G, 0, 0)),
                  pl.BlockSpec((1, 1, CMP_LEN * Dh), lambda s, i: (s, 0, 0)),
                  pl.BlockSpec((1, CMP_LEN * Dh, hidden), lambda s, i: (s, 0, 0)),
                  pl.BlockSpec((1, hidden, Dh), lambda s, i: (s, 0, 0))],
        out_specs=pl.BlockSpec((1, 1, n_chunks, Dh), lambda s, i: (s, i, 0, 0)),
        out_shape=jax.ShapeDtypeStruct((2, B * G, n_chunks, Dh), BF16),
        compiler_params=_params(2),
        name="compress_kv",
    )(tokens, pos_flat, w1, w2)


def _stack_heads(q):
    return jnp.concatenate([q[:, r * HEAD_DIM:(r + 1) * HEAD_DIM] for r in range(NSA_HPG)], axis=0)


def _slope_row(slopes_ref, g, tq):
    col = lax.broadcasted_iota(jnp.int32, (1, NSA_HPG * tq), 1)
    out = jnp.zeros((1, NSA_HPG * tq), F32)
    for r in range(NSA_HPG):
        out = jnp.where(col // tq == r, slopes_ref[g * NSA_HPG + r], out)
    return out


def _cmp_select_kernel(slopes_ref, q_ref, kc_ref, vct_ref, ovt_ref, ocmp_ref, sel_ref, hits_ref,
                       *, tq, n_sel, blocks_per_chunk, hit_tile):
    g = pl.program_id(1)
    t0 = pl.program_id(2) * tq
    cols = NSA_HPG * tq
    n_cmp = kc_ref.shape[2]
    slope_row = _slope_row(slopes_ref, g, tq)
    q_rows = _stack_heads(q_ref[0])

    def group_sum(pc):
        out = pc[:, 0:tq]
        for r in range(1, NSA_HPG):
            out = out + pc[:, r * tq:(r + 1) * tq]
        return out

    tpos = t0 + lax.broadcasted_iota(jnp.int32, (n_cmp, cols), 1) % tq
    cmp_end = lax.broadcasted_iota(jnp.int32, (n_cmp, cols), 0) * CMP_STRIDE + (CMP_LEN - 1)
    dc = (tpos - cmp_end).astype(F32)
    sc = jnp.where(dc >= 0, _dot_nt(kc_ref[0, 0], q_rows) - slope_row * dc, NEG_INF)
    e = jnp.exp2(sc - jnp.max(sc, axis=0, keepdims=True))
    any_valid = jnp.where(dc[0:1] >= 0, 1.0, 0.0)
    pc = e * (any_valid / jnp.sum(e, axis=0, keepdims=True))
    o_t = _dot(vct_ref[0, 0], pc.astype(BF16))
    for r in range(NSA_HPG):
        ocmp_ref[0, 0, r] = o_t[:, r * tq:(r + 1) * tq]
    hi, lo = _split_bf16(group_sum(pc))
    imp = _dot(ovt_ref[...], hi) + _dot(ovt_ref[...], lo)
    blk = lax.broadcasted_iota(jnp.int32, (n_sel, tq), 0)
    cur = (t0 + lax.broadcasted_iota(jnp.int32, (n_sel, tq), 1)) // SEL_LEN
    forced = (blk == 0) | (blk == cur) | (blk == cur - 1)
    imp = jnp.where(blk <= cur, imp, -1.0)
    topk = min(SEL_TOPK, n_sel)

    def pick(imp, count):
        for _ in range(count):
            best = jnp.max(imp, axis=0, keepdims=True)
            first = jnp.min(jnp.where(imp == best, blk, n_sel), axis=0, keepdims=True)
            imp = jnp.where(blk == first, PICKED, imp)
        return imp

    imp = lax.cond(t0 >= 2 * SEL_LEN,
                   lambda: pick(jnp.where(forced, PICKED, imp), topk - 3),
                   lambda: pick(jnp.where(forced, FORCE_SCORE, imp), topk))
    keep = (imp == PICKED) & (blk <= cur)
    sel_ref[0, 0] = jnp.where(keep, 0.0, -SEL_DROP).T.astype(sel_ref.dtype)
    n_chunks = n_sel // blocks_per_chunk
    member = (lax.broadcasted_iota(jnp.int32, (n_chunks, n_sel), 1) // blocks_per_chunk
              == lax.broadcasted_iota(jnp.int32, (n_chunks, n_sel), 0))
    per_query = _dot(jnp.where(member, 1.0, 0.0).astype(BF16),
                     jnp.where(keep, 1.0, 0.0).astype(BF16)).astype(BF16)
    for part in range(tq // hit_tile):
        hits_ref[0, 0, part] = _dot(per_query[:, part * hit_tile:(part + 1) * hit_tile],
                                    jnp.ones((hit_tile, 128), BF16))


def cmp_select(q, kc, vc_t, overlap_t, slopes, tk, hit_tile, tq=256):
    B, S, _ = q.shape
    n_cmp = kc.shape[2]
    n_sel = S // SEL_LEN
    G = NSA_GROUPS
    smem = pl.BlockSpec(memory_space=pltpu.SMEM)
    return pl.pallas_call(
        functools.partial(_cmp_select_kernel, tq=tq, n_sel=n_sel, blocks_per_chunk=tk // SEL_LEN,
                          hit_tile=hit_tile),
        grid=(B, G, S // tq),
        in_specs=[smem,
                  pl.BlockSpec((1, tq, 256), lambda b, g, i: (b, i, g)),
                  pl.BlockSpec((1, 1, n_cmp, HEAD_DIM), lambda b, g, i: (b, g, 0, 0)),
                  pl.BlockSpec((1, 1, HEAD_DIM, n_cmp), lambda b, g, i: (b, g, 0, 0)),
                  pl.BlockSpec((n_sel, n_cmp), lambda b, g, i: (0, 0))],
        out_specs=[pl.BlockSpec((1, 1, NSA_HPG, HEAD_DIM, tq), lambda b, g, i: (b, g, 0, 0, i)),
                   pl.BlockSpec((1, 1, tq, n_sel), lambda b, g, i: (b, g, i, 0)),
                   pl.BlockSpec((1, 1, tq // hit_tile, S // tk, 128), lambda b, g, i: (b, g, i, 0, 0))],
        out_shape=[jax.ShapeDtypeStruct((B, G, NSA_HPG, HEAD_DIM, S), F32),
                   jax.ShapeDtypeStruct((B, G, S, n_sel), BF16),
                   jax.ShapeDtypeStruct((B, G, S // hit_tile, S // tk, 128), F32)],
        compiler_params=_params(3),
        name="cmp_select",
    )(slopes, q, kc, vc_t, overlap_t)


def _sel_win_kernel(slopes_ref, active_ref, q_ref, sl_ref, ks_ref, vst_ref, kw_ref, vwt_ref, sel_ref,
                    ocmp_ref, gate_ref, o_ref, ms_ref, accs_ref, mw_ref, accw_ref, offset_ref, todo_ref,
                    knorm_ref, *, tq, tk):
    g = pl.program_id(1)
    t0 = pl.program_id(2) * tq
    cols = NSA_HPG * tq
    slope_row = _slope_row(slopes_ref, g, tq)
    q = q_ref[0]
    q_aug = jnp.concatenate(
        [jnp.concatenate([q[:, r * HEAD_DIM:(r + 1) * HEAD_DIM],
                          jnp.broadcast_to(sl_ref[0, r:r + 1, :], (tq, HEAD_DIM))], axis=1)
         for r in range(NSA_HPG)], axis=0)
    q_sel = jnp.concatenate([q_aug, jnp.concatenate([sel_ref[0, 0]] * NSA_HPG, axis=0)], axis=1)

    @pl.when(pl.program_id(2) == 0)
    def _():
        offset_ref[...] = (lax.broadcasted_iota(jnp.int32, (tk, cols), 0)
                           - lax.broadcasted_iota(jnp.int32, (tk, cols), 1) % tq)

        def body(c, best):
            rows = pl.ds(pl.multiple_of(c * tk, tk), tk)
            ks = ks_ref[0, 0, rows, :][:, :HEAD_DIM].astype(F32)
            kw = kw_ref[0, 0, rows, :][:, :HEAD_DIM].astype(F32)
            return (jnp.maximum(best[0], jnp.max(jnp.sum(ks * ks, axis=1))),
                    jnp.maximum(best[1], jnp.max(jnp.sum(kw * kw, axis=1))))

        best = lax.fori_loop(0, ks_ref.shape[2] // tk, body, (jnp.float32(0.0), jnp.float32(0.0)))
        knorm_ref[0] = best[0]
        knorm_ref[1] = best[1]

    sel_bound = _score_bound(q, knorm_ref[0])
    win_bound = _score_bound(q, knorm_ref[1])
    safe = 2.0 * jnp.maximum(sel_bound, win_bound) <= SAFE_EXP_LOG2

    def step(c, k_ref, q_rows, vt_ref, keep=None):
        inside = c >= 0
        start = pl.multiple_of(jnp.maximum(c, 0) * tk, tk)
        shift = c * tk - t0

        def scores():
            s_t = _dot_nt(k_ref[0, 0, pl.ds(start, tk), :], q_rows)
            if keep is None:
                return s_t
            lo, hi = keep
            offset = offset_ref[...]
            if hi is not None:
                return jnp.where(offset <= jnp.where(inside, hi - shift, -FAR), s_t, NEG_INF)
            return jnp.where(offset > jnp.where(inside, lo - shift, FAR), s_t, NEG_INF)

        kappa = jnp.where(inside, slope_row * shift.astype(F32), NEG_INF)
        return scores, kappa, lambda: vt_ref[0, 0, :, pl.ds(start, tk)]

    last = t0 // tk
    causal, recent, anything = (None, 0), (-WINDOW, None), (-FAR, None)
    first_steps = [step(last - 2, kw_ref, q_aug, vwt_ref, recent) + (1,),
                   step(last, ks_ref, q_sel, vst_ref, causal) + (0,),
                   step(last - 1, kw_ref, q_aug, vwt_ref, anything) + (1,),
                   step(last, kw_ref, q_aug, vwt_ref, causal) + (1,)]

    def note_active(c, n):
        hit = active_ref[0, 0, 0, 0, c] > 0

        @pl.when(hit)
        def _():
            todo_ref[n] = c

        return n + hit.astype(jnp.int32)

    n_todo = lax.fori_loop(0, last, note_active, 0)
    sel_step = lambda i: step(todo_ref[i], ks_ref, q_sel, vst_ref) + (0,)

    accs_ref[...] = jnp.zeros_like(accs_ref)
    accw_ref[...] = jnp.zeros_like(accw_ref)

    @pl.when(safe)
    def _():
        in_tile = slope_row * (lax.broadcasted_iota(jnp.int32, (1, cols), 1) % tq).astype(F32)
        block = lambda steps: _fixed_frame_block(steps, [accs_ref, accw_ref],
                                                 [sel_bound + in_tile, win_bound + in_tile])
        block(first_steps)
        _chunk_loop(n_todo, sel_step, block, (8, 4, 2, 1))

    @pl.when(jnp.logical_not(safe))
    def _():
        states = [(ms_ref, accs_ref), (mw_ref, accw_ref)]
        for m_ref, _ in states:
            m_ref[...] = jnp.full_like(m_ref, NEG_INF)
        block = lambda steps: _softmax_block(steps, states)
        block(first_steps)
        _chunk_loop(n_todo, sel_step, block, (4, 2, 1))

    o_sel = accs_ref[0:HEAD_DIM, :] / accs_ref[HEAD_DIM:HEAD_DIM + 1, :]
    o_win = accw_ref[0:HEAD_DIM, :] / accw_ref[HEAD_DIM:HEAD_DIM + 1, :]

    gates = gate_ref[0]
    outs = []
    for r in range(NSA_HPG):
        cs = slice(r * tq, (r + 1) * tq)
        outs.append(gates[3 * r:3 * r + 1] * ocmp_ref[0, 0, r] + gates[3 * r + 1:3 * r + 2] * o_sel[:, cs]
                    + gates[3 * r + 2:3 * r + 3] * o_win[:, cs])
    o_ref[0] = jnp.concatenate(outs, axis=0).T.astype(o_ref.dtype)


def sel_win_attention(q, active, slope_cols, ks, vs_t, kw, vw_t, sel_bias, o_cmp, gates_t,
                      slopes, tq=256, tk=256):
    B, S, _ = q.shape
    assert tq == tk and WINDOW == 2 * tk
    G = NSA_GROUPS
    n_sel = S // SEL_LEN
    cols = NSA_HPG * tq
    smem = pl.BlockSpec(memory_space=pltpu.SMEM)
    ks_spec = pl.BlockSpec((1, 1, S, ks.shape[-1]), lambda b, g, i: (b, g, 0, 0))
    kw_spec = pl.BlockSpec((1, 1, S, kw.shape[-1]), lambda b, g, i: (b, g, 0, 0))
    v_spec = pl.BlockSpec((1, 1, V_ROWS_64, S), lambda b, g, i: (b, g, 0, 0))
    return pl.pallas_call(
        functools.partial(_sel_win_kernel, tq=tq, tk=tk),
        grid=(B, G, S // tq),
        in_specs=[smem,
                  pl.BlockSpec((1, 1, 1, 1, S // tk), lambda b, g, i: (b, g, i, 0, 0),
                               memory_space=pltpu.SMEM),
                  pl.BlockSpec((1, tq, 256), lambda b, g, i: (b, i, g)),
                  pl.BlockSpec((1, NSA_HPG, HEAD_DIM), lambda b, g, i: (g, 0, 0)),
                  ks_spec, v_spec, kw_spec, v_spec,
                  pl.BlockSpec((1, 1, tq, n_sel), lambda b, g, i: (b, g, i, 0)),
                  pl.BlockSpec((1, 1, NSA_HPG, HEAD_DIM, tq), lambda b, g, i: (b, g, 0, 0, i)),
                  pl.BlockSpec((1, 4 * NSA_HPG, tq), lambda b, g, i: (b, g, i))],
        out_specs=pl.BlockSpec((1, tq, 256), lambda b, g, i: (b, i, g)),
        out_shape=jax.ShapeDtypeStruct((B, S, G * 256), BF16),
        scratch_shapes=[pltpu.VMEM((1, cols), F32), pltpu.VMEM((V_ROWS_64, cols), F32),
                        pltpu.VMEM((1, cols), F32), pltpu.VMEM((V_ROWS_64, cols), F32),
                        pltpu.VMEM((tk, cols), jnp.int32),
                        pltpu.SMEM((S // tk,), jnp.int32), pltpu.SMEM((2,), F32)],
        compiler_params=_params(2, 1),
        name="sel_win_attention",
    )(slopes, active, q, slope_cols, ks, vs_t, kw, vw_t, sel_bias, o_cmp, gates_t)


def _alibi_slopes_log2(n_heads):
    slopes = np.exp2(-8.0 * (np.arange(n_heads, dtype=np.float32) + 1.0) / n_heads)
    return (slopes.astype(np.float32) * np.float32(LOG2E)).astype(np.float32)


def even_layer_mix(x2d, B, S, norm_g, w_in, lam_q1, lam_k1, lam_q2, lam_k2, subln, layer):
    proj, sb_vt, df_vt = even_proj(x2d, B, S, norm_g, w_in.astype(BF16))
    proj = proj.reshape(B, S, -1)
    lam_rows = jnp.stack([lam_q1, lam_k1, lam_q2, lam_k2]).astype(F32)
    mix = diff_attention(proj, df_vt, lam_rows, subln.astype(F32), _alibi_slopes_log2(4), layer,
                         sb_attention(proj, sb_vt))
    return mix.reshape(B * S, -1)


def odd_layer_mix(x2d, B, S, norm_g, w_in, pos_k, k_w1, k_w2, pos_v, v_w1, v_w2, tq=256, tk=256):
    G, Dh = NSA_GROUPS, HEAD_DIM
    q_width = G * NSA_HPG * Dh
    n_main = q_width + 6 * G * Dh
    per_group = NSA_HPG * N_GATES
    w_gate = w_in[:, n_main:n_main + G * per_group].reshape(-1, G, per_group)
    w_gate = jnp.pad(w_gate, ((0, 0), (0, 0), (0, 4 * NSA_HPG - per_group))).reshape(-1, 4 * NSA_HPG * G)
    w_gate = jnp.pad(w_gate, ((0, 0), (0, GATE_PAD - w_gate.shape[1])))
    w_all = jnp.concatenate([w_in[:, :n_main], w_gate], axis=1).astype(BF16)

    n_chunks = S // CMP_STRIDE
    n_sel = S // SEL_LEN
    onehot = (jnp.arange(S)[:, None] // SEL_LEN == jnp.arange(n_sel)[None, :]).astype(BF16)
    q, cmp_in, ks, kw, vs_t, vw_t, gates_t = odd_proj(x2d, B, S, norm_g, w_all,
                                                      _key_position_columns(S, tk, Dh), onehot)
    q = q.reshape(B, S, q_width)

    pos_flat = jnp.stack([pos_k, pos_v]).reshape(2, 1, CMP_LEN * Dh).astype(F32)
    w1 = jnp.stack([k_w1, v_w1]).astype(BF16)
    w2 = jnp.stack([k_w2, v_w2]).astype(BF16)
    cmp = compress_kv(cmp_in, pos_flat, w1, w2).reshape(2, B, G, n_chunks, Dh)

    cmp_start = jnp.arange(n_chunks) * CMP_STRIDE
    sel_start = jnp.arange(n_sel) * SEL_LEN
    overlap_t = ((cmp_start[None, :] < sel_start[:, None] + SEL_LEN)
                 & (sel_start[:, None] <= cmp_start[None, :] + CMP_LEN - 1)).astype(BF16)
    slopes = _alibi_slopes_log2(G * NSA_HPG)
    slope_cols = _slope_pieces(slopes, Dh).reshape(G, NSA_HPG, Dh)

    o_cmp, sel_bias, hits = cmp_select(q, cmp[0], cmp[1].transpose(0, 1, 3, 2), overlap_t, slopes, tk, tq)
    active = (hits[..., 0] > 0).astype(jnp.int32)[:, :, :, None, :]
    o = sel_win_attention(q, active, slope_cols, ks, vs_t, kw, vw_t, sel_bias, o_cmp, gates_t, slopes,
                          tq=tq, tk=tk)
    return o.reshape(B * S, q_width)


def kernel(x, attn_norm, mlp_norm, final_norm, ev_w_in, ev_lam_q1, ev_lam_k1, ev_lam_q2, ev_lam_k2,
           ev_subln, ev_w_out, od_w_in, od_cmp_pos_k, od_cmp_k_w1, od_cmp_k_w2, od_cmp_pos_v,
           od_cmp_v_w1, od_cmp_v_w2, od_w_out, mlp_w1, mlp_w2):
    B, S, D = x.shape
    depth = attn_norm.shape[0]
    x2d = x.reshape(B * S, D)
    for layer in range(depth):
        idx = layer // 2
        if layer % 2 == 0:
            mix = even_layer_mix(x2d, B, S, attn_norm[layer], ev_w_in[idx], ev_lam_q1[idx],
                                 ev_lam_k1[idx], ev_lam_q2[idx], ev_lam_k2[idx], ev_subln[idx], layer)
            w_out = ev_w_out[idx]
        else:
            mix = odd_layer_mix(x2d, B, S, attn_norm[layer], od_w_in[idx], od_cmp_pos_k[idx],
                                od_cmp_k_w1[idx], od_cmp_k_w2[idx], od_cmp_pos_v[idx],
                                od_cmp_v_w1[idx], od_cmp_v_w2[idx])
            w_out = od_w_out[idx]
        g_final = final_norm if layer == depth - 1 else None
        x2d = post_block(mix, x2d, w_out.astype(BF16), mlp_norm[layer], mlp_w1[layer].astype(BF16),
                         mlp_w2[layer].astype(BF16), g_final)
    return x2d.reshape(B, S, D)
```

```python
import functools
import math

import jax
import jax.numpy as jnp
import numpy as np
from jax import lax
from jax.experimental import pallas as pl
from jax.experimental.pallas import tpu as pltpu

F32 = jnp.float32
BF16 = jnp.bfloat16

HEAD_DIM = 64
RMS_EPS = 1e-6
NEG_INF = -1e30
FORCE_SCORE = 1e6
NSA_GROUPS = 4
NSA_HPG = 4
CMP_LEN = 32
CMP_STRIDE = 16
SEL_LEN = 64
SEL_TOPK = 16
WINDOW = 512
N_GATES = 3
GATE_PAD = 128
SEL_DROP = 2.0 ** 24
BF16_EXACT_INT = 256
V_ROWS_64 = 80
V_ROWS_128 = 144

LOG2E = math.log2(math.e)
Q_SCALE = HEAD_DIM ** -0.5 * LOG2E
UNDERFLOW_LOG2 = -160.0
SAFE_EXP_LOG2 = 100.0
FAR = 1 << 30
PICKED = -2.0

VMEM_LIMIT = 56 * 1024 * 1024


def _params(n_parallel, n_arbitrary=0):
    return pltpu.CompilerParams(dimension_semantics=("parallel",) * n_parallel + ("arbitrary",) * n_arbitrary,
                                vmem_limit_bytes=VMEM_LIMIT)


def _rms(x, g):
    ms = jnp.mean(x * x, axis=-1, keepdims=True)
    return x * lax.rsqrt(ms + RMS_EPS) * g


def _dot(a, b):
    return jnp.dot(a, b, preferred_element_type=F32)


def _dot_nt(a, b):
    return lax.dot_general(a, b, (((1,), (1,)), ((), ())), preferred_element_type=F32)


def _split_bf16(x):
    hi = x.astype(BF16)
    lo = (x - hi.astype(F32)).astype(BF16)
    return hi, lo


def _slope_pieces(slopes, width):
    def top_bits(x):
        return (x.view(np.uint32) & np.uint32(0xFFFF0000)).view(np.float32)

    s1 = top_bits(slopes)
    r1 = slopes - s1
    s2 = top_bits(r1)
    s3 = top_bits(r1 - s2)
    out = np.zeros((slopes.shape[0], width), np.float32)
    out[:, :6] = np.stack([s1, s2, s3, s1, s2, s3], axis=1)
    return jnp.asarray(out).astype(BF16)


def _key_position_columns(n, tk, width):
    j = jnp.arange(n) % tk
    a = (j // BF16_EXACT_INT) * BF16_EXACT_INT
    b = j % BF16_EXACT_INT
    cols = jnp.stack([a, a, a, b, b, b], axis=1).astype(BF16)
    return jnp.pad(cols, ((0, 0), (0, width - cols.shape[1])))


def _softmax_block(steps, states):
    s_all = [scores() for scores, _, _, _ in steps]
    live = {k: (states[k][0][...], states[k][1][...]) for k in sorted({k for _, _, _, k in steps})}
    for (_, kap, values, k), s_t in zip(steps, s_all):
        m_run, acc = live[k]
        m_new = jnp.maximum(m_run, jnp.max(s_t, axis=0, keepdims=True) + kap)
        p = jnp.exp2(s_t - (m_new - kap)).astype(BF16)
        live[k] = (m_new, jnp.exp2(m_run - m_new) * acc + _dot(values(), p))
    for k, (m_run, acc) in live.items():
        states[k][0][...] = m_run
        states[k][1][...] = acc


def _fixed_frame_block(steps, acc_refs, frames):
    s_all = [scores() for scores, _, _, _ in steps]
    live = {k: acc_refs[k][...] for k in sorted({k for _, _, _, k in steps})}
    for (_, kap, values, k), s_t in zip(steps, s_all):
        live[k] = live[k] + _dot(values(), jnp.exp2(s_t - (frames[k] - kap)).astype(BF16))
    for k, acc in live.items():
        acc_refs[k][...] = acc


def _chunk_loop(n_steps, step, block, groups):
    done = 0
    for group in groups:
        def body(i, carry, group=group, done=done):
            block([step(done + group * i + j) for j in range(group)])
            return carry

        n_groups = (n_steps - done) // group
        lax.fori_loop(0, n_groups, body, 0)
        done = done + group * n_groups


def _score_bound(q, k_norm2):
    q32 = q.astype(F32)
    width = q.shape[1]
    member = (lax.broadcasted_iota(jnp.int32, (width, 128), 0) // HEAD_DIM
              == lax.broadcasted_iota(jnp.int32, (width, 128), 1))
    head_norm2 = _dot((q32 * q32).astype(BF16), jnp.where(member, 1.0, 0.0).astype(BF16))
    return jnp.sqrt(jnp.max(head_norm2) * k_norm2 * 1.05)


PROJ_CHUNK = 512


def _ones_row_tail(rows, width):
    return jnp.where(lax.broadcasted_iota(jnp.int32, (rows, width), 0) == 0, 1.0, 0.0).astype(BF16)


def _even_proj_kernel(x_ref, g_ref, w_ref, o_ref, sbv_ref, dfv_ref):
    tm = x_ref.shape[0]
    xn = _rms(x_ref[...], g_ref[...]).astype(BF16)
    tail = _ones_row_tail(V_ROWS_128 - 2 * HEAD_DIM, tm)
    for c in range(w_ref.shape[1] // PROJ_CHUNK):
        sl = slice(c * PROJ_CHUNK, (c + 1) * PROJ_CHUNK)
        res = _dot(xn, w_ref[:, sl])
        if c in (0, 3):
            res = res * Q_SCALE
        o_ref[:, sl] = res.astype(BF16)
        if c == 2:
            sbv_ref[0] = res.T.astype(BF16)
        if c == 5:
            v_t = res.T.astype(BF16)
            for h in range(dfv_ref.shape[1]):
                dfv_ref[0, h, 0:2 * HEAD_DIM, :] = v_t[h * 2 * HEAD_DIM:(h + 1) * 2 * HEAD_DIM]
                dfv_ref[0, h, 2 * HEAD_DIM:V_ROWS_128, :] = tail


def even_proj(x2d, B, S, g, w_bf16, tm=512):
    T, D = x2d.shape
    n_w = w_bf16.shape[1]
    per_seq = S // tm
    assert n_w == 6 * PROJ_CHUNK
    return pl.pallas_call(
        _even_proj_kernel,
        grid=(T // tm,),
        in_specs=[pl.BlockSpec((tm, D), lambda i: (i, 0)),
                  pl.BlockSpec((1, D), lambda i: (0, 0)),
                  pl.BlockSpec((D, n_w), lambda i: (0, 0))],
        out_specs=[pl.BlockSpec((tm, n_w), lambda i: (i, 0)),
                   pl.BlockSpec((1, PROJ_CHUNK, tm), lambda i: (i // per_seq, 0, i % per_seq)),
                   pl.BlockSpec((1, 4, V_ROWS_128, tm), lambda i: (i // per_seq, 0, 0, i % per_seq))],
        out_shape=[jax.ShapeDtypeStruct((T, n_w), BF16),
                   jax.ShapeDtypeStruct((B, PROJ_CHUNK, S), BF16),
                   jax.ShapeDtypeStruct((B, 4, V_ROWS_128, S), BF16)],
        compiler_params=_params(1),
        name="even_proj",
    )(x2d, g.reshape(1, D), w_bf16)


def _odd_proj_kernel(x_ref, g_ref, w_ref, pos_ref, onehot_ref,
                     q_ref, cin_ref, ks_ref, kw_ref, vst_ref, vwt_ref, gate_ref):
    tm = x_ref.shape[0]
    G, Dh = NSA_GROUPS, HEAD_DIM
    kv = G * Dh
    xn = _rms(x_ref[...], g_ref[...]).astype(BF16)
    tail = _ones_row_tail(V_ROWS_64 - Dh, tm)
    pos = pos_ref[...]
    onehot = onehot_ref[...]
    q_width = q_ref.shape[1]
    for c in range(q_width // PROJ_CHUNK):
        sl = slice(c * PROJ_CHUNK, (c + 1) * PROJ_CHUNK)
        q_ref[:, sl] = (_dot(xn, w_ref[:, sl]) * Q_SCALE).astype(BF16)

    def pair(j):
        return _dot(xn, w_ref[:, q_width + j * 2 * kv:q_width + (j + 1) * 2 * kv])

    res = pair(0)
    for which in range(2):
        for g in range(G):
            cin_ref[which, 0, g] = res[:, which * kv + g * Dh:which * kv + (g + 1) * Dh]

    for j, k_ref, vt_ref, extra in ((1, ks_ref, vst_ref, [pos, onehot]), (2, kw_ref, vwt_ref, [pos])):
        res = pair(j)
        v_t = res[:, kv:2 * kv].T.astype(BF16)
        for g in range(G):
            k_ref[0, g] = jnp.concatenate([res[:, g * Dh:(g + 1) * Dh].astype(BF16)] + extra, axis=1)
            vt_ref[0, g, 0:Dh, :] = v_t[g * Dh:(g + 1) * Dh]
            vt_ref[0, g, Dh:V_ROWS_64, :] = tail

    logits = _dot(xn, w_ref[:, q_width + 6 * kv:q_width + 6 * kv + GATE_PAD])
    gate_ref[0] = jax.nn.sigmoid(logits).T[0:gate_ref.shape[1]]


def odd_proj(x2d, B, S, g, w_bf16, pos_cols, onehot, tm=512):
    T, D = x2d.shape
    G, Dh = NSA_GROUPS, HEAD_DIM
    q_width = G * NSA_HPG * Dh
    n_sel = onehot.shape[1]
    per_seq = S // tm
    seq_tile = lambda i: (i // per_seq, 0, i % per_seq, 0)
    seq_tile_t = lambda i: (i // per_seq, 0, 0, i % per_seq)
    return pl.pallas_call(
        _odd_proj_kernel,
        grid=(T // tm,),
        in_specs=[pl.BlockSpec((tm, D), lambda i: (i, 0)),
                  pl.BlockSpec((1, D), lambda i: (0, 0)),
                  pl.BlockSpec(w_bf16.shape, lambda i: (0, 0)),
                  pl.BlockSpec((tm, Dh), lambda i: (i % per_seq, 0)),
                  pl.BlockSpec((tm, n_sel), lambda i: (i % per_seq, 0))],
        out_specs=[pl.BlockSpec((tm, q_width), lambda i: (i, 0)),
                   pl.BlockSpec((2, 1, G, tm, Dh), lambda i: (0, i // per_seq, 0, i % per_seq, 0)),
                   pl.BlockSpec((1, G, tm, 2 * Dh + n_sel), seq_tile),
                   pl.BlockSpec((1, G, tm, 2 * Dh), seq_tile),
                   pl.BlockSpec((1, G, V_ROWS_64, tm), seq_tile_t),
                   pl.BlockSpec((1, G, V_ROWS_64, tm), seq_tile_t),
                   pl.BlockSpec((1, 4 * NSA_HPG * G, tm), lambda i: (i // per_seq, 0, i % per_seq))],
        out_shape=[jax.ShapeDtypeStruct((T, q_width), BF16),
                   jax.ShapeDtypeStruct((2, B, G, S, Dh), F32),
                   jax.ShapeDtypeStruct((B, G, S, 2 * Dh + n_sel), BF16),
                   jax.ShapeDtypeStruct((B, G, S, 2 * Dh), BF16),
                   jax.ShapeDtypeStruct((B, G, V_ROWS_64, S), BF16),
                   jax.ShapeDtypeStruct((B, G, V_ROWS_64, S), BF16),
                   jax.ShapeDtypeStruct((B, 4 * NSA_HPG * G, S), F32)],
        compiler_params=_params(1),
        name="odd_proj",
    )(x2d, g.reshape(1, D), w_bf16, pos_cols, onehot)


def _head_pair_rows(q, t):
    lane = lax.broadcasted_iota(jnp.int32, (t, 2 * HEAD_DIM), 1)
    zero = jnp.zeros_like(q)
    return jnp.where(lane < HEAD_DIM, q, zero), jnp.where(lane >= HEAD_DIM, q, zero)


def _sb_kernel(q_ref, k_ref, vt_ref, o_ref, acc_ref, carry_ref, *, t):
    qi = pl.program_id(2)
    cols = 2 * t
    q_both = jnp.concatenate(_head_pair_rows(q_ref[0], t), axis=0)
    s_idx = lax.broadcasted_iota(jnp.int32, (t, 2 * t), 0)
    j_idx = lax.broadcasted_iota(jnp.int32, (t, 2 * t), 1) % t
    upper2 = jnp.where(j_idx > s_idx, 1.0, 0.0).astype(BF16)
    key = lax.broadcasted_iota(jnp.int32, (t, cols), 0)
    qry = lax.broadcasted_iota(jnp.int32, (t, cols), 1) % t
    past_diag = key < qry

    acc_ref[...] = jnp.zeros_like(acc_ref)
    carry_ref[...] = jnp.zeros_like(carry_ref)

    def chunks(specs):
        starts = [pl.multiple_of(kc * t, t) for kc, _ in specs]
        z_all = [_dot_nt(k_ref[0, pl.ds(start, t), :], q_both) for start in starts]
        carry = carry_ref[...]
        acc = acc_ref[...]
        for (_, masked), start, z in zip(specs, starts, z_all):
            drop = jnp.maximum(z, 0.0) + jnp.log2(1.0 + jnp.exp2(jnp.abs(z) * -1.0))
            log_beta = z - drop
            if masked:
                drop = jnp.where(past_diag, drop, 0.0)
            hi, lo = _split_bf16(drop)
            tail = _dot(upper2, jnp.concatenate([hi, lo], axis=0))
            w = jnp.exp2(log_beta - tail - carry)
            if masked:
                w = jnp.where(past_diag, w, 0.0)
            carry = carry + tail[0:1] + drop[0:1]
            acc = acc + _dot(vt_ref[0, :, pl.ds(start, t)], w.astype(BF16))
        carry_ref[...] = carry
        acc_ref[...] = acc
        return jnp.min(carry)

    least = lax.cond(qi > 0, lambda: chunks([(qi, True), (qi - 1, False)]), lambda: chunks([(qi, True)]))

    def cond(state):
        j, least_carry = state
        return (j < qi) & (least_carry < -UNDERFLOW_LOG2)

    def body(state):
        j, _ = state
        return j + 1, chunks([(qi - 1 - j, False)])

    lax.while_loop(cond, body, (1, least))
    row = lax.broadcasted_iota(jnp.int32, (2 * HEAD_DIM, t), 0)
    o_t = jnp.where(row < HEAD_DIM, acc_ref[:, :t], acc_ref[:, t:])
    o_ref[0] = o_t.T.astype(o_ref.dtype)


def sb_attention(proj, v_t, t=256):
    B, S, _ = proj.shape
    n_pairs = 4
    return pl.pallas_call(
        functools.partial(_sb_kernel, t=t),
        grid=(B, n_pairs, S // t),
        in_specs=[pl.BlockSpec((1, t, 128), lambda b, p, i: (b, i, p)),
                  pl.BlockSpec((1, S, 128), lambda b, p, i: (b, 0, n_pairs + p)),
                  pl.BlockSpec((1, 128, S), lambda b, p, i: (b, p, 0))],
        out_specs=pl.BlockSpec((1, t, 128), lambda b, p, i: (b, i, p)),
        out_shape=jax.ShapeDtypeStruct((B, S, n_pairs * 128), BF16),
        scratch_shapes=[pltpu.VMEM((128, 2 * t), F32), pltpu.VMEM((1, 2 * t), F32)],
        compiler_params=_params(3),
        name="sb_attention",
    )(proj, proj, v_t)


def _diff_kernel(slopes_ref, lam_ref, q_ref, k_ref, vt_ref, pos_ref, sl_ref, subln_ref, sb_ref, o_ref,
                 m_ref, acc_ref, offset_ref, knorm_ref, *, t, out_scale, lambda_init):
    h = pl.program_id(1)
    qi = pl.program_id(2)
    slope = slopes_ref[h]
    cols = 2 * t
    slope_cols = jnp.broadcast_to(sl_ref[0], (t, 128))
    q_parts = _head_pair_rows(q_ref[0], t)
    q_both = jnp.concatenate([jnp.concatenate([qc, slope_cols], axis=1) for qc in q_parts],
                             axis=0)
    lane = lax.broadcasted_iota(jnp.int32, (t, 2 * HEAD_DIM), 1)

    @pl.when(qi == 0)
    def _():
        def body(c, best):
            k = k_ref[0, pl.ds(pl.multiple_of(c * t, t), t), :].astype(F32)
            k2 = k * k
            return (jnp.maximum(best[0], jnp.max(jnp.sum(jnp.where(lane < HEAD_DIM, k2, 0.0), axis=1))),
                    jnp.maximum(best[1], jnp.max(jnp.sum(jnp.where(lane >= HEAD_DIM, k2, 0.0), axis=1))))

        best = lax.fori_loop(0, k_ref.shape[1] // t, body, (jnp.float32(0.0), jnp.float32(0.0)))
        knorm_ref[0] = best[0]
        knorm_ref[1] = best[1]
        offset_ref[...] = (lax.broadcasted_iota(jnp.int32, (t, cols), 0)
                           - lax.broadcasted_iota(jnp.int32, (t, cols), 1) % t)

    qk_bound = _score_bound(q_ref[0], jnp.maximum(knorm_ref[0], knorm_ref[1]))
    safe = 2.0 * qk_bound <= SAFE_EXP_LOG2
    col = lax.broadcasted_iota(jnp.int32, (1, cols), 1)

    pos = pos_ref[...]

    def step(kc, diagonal=False):
        start = pl.multiple_of(kc * t, t)

        def scores():
            s_t = _dot_nt(jnp.concatenate([k_ref[0, pl.ds(start, t), :], pos], axis=1), q_both)
            return jnp.where(offset_ref[...] <= 0, s_t, NEG_INF) if diagonal else s_t

        return scores, slope * ((kc - qi) * t).astype(F32), lambda: vt_ref[0, 0, :, pl.ds(start, t)], 0

    acc_ref[...] = jnp.zeros_like(acc_ref)

    @pl.when(safe)
    def _():
        frame = qk_bound + slope * (col % t).astype(F32)
        block = lambda steps: _fixed_frame_block(steps, [acc_ref], [frame])
        block([step(qi, True)])
        n_back = jnp.ceil((slope * (t - 1) - UNDERFLOW_LOG2) / (slope * t)).astype(jnp.int32)
        _chunk_loop(jnp.minimum(n_back, qi), lambda i: step(qi - 1 - i), block, (8, 4, 2, 1))

    @pl.when(jnp.logical_not(safe))
    def _():
        m_ref[...] = jnp.full_like(m_ref, NEG_INF)
        block = lambda steps: _softmax_block(steps, [(m_ref, acc_ref)])
        block([step(qi, True)])
        excess = jnp.max(qk_bound - m_ref[...]) + slope * (t - 1)
        n_back = jnp.clip(jnp.ceil((excess - UNDERFLOW_LOG2) / (slope * t)).astype(jnp.int32), 0, qi)
        _chunk_loop(n_back, lambda i: step(qi - 1 - i), block, (4, 2, 1))

    lam_terms = lam_ref[...]
    lam = (jnp.exp(jnp.sum(lam_terms[0:1] * lam_terms[1:2], axis=1, keepdims=True))
           - jnp.exp(jnp.sum(lam_terms[2:3] * lam_terms[3:4], axis=1, keepdims=True))
           + lambda_init)
    o_t = acc_ref[0:128, :] / acc_ref[128:129, :]
    o = (o_t[:, :t] - lam * o_t[:, t:]).T
    o_ref[0, :, 0:128] = sb_ref[0]
    o_ref[0, :, 128:256] = (_rms(o, subln_ref[...]) * out_scale).astype(o_ref.dtype)


def diff_attention(proj, vt_aug, lam_rows, subln, slopes, layer, o_sb, t=512):
    B, S, _ = proj.shape
    n_heads = 4
    lambda_init = 0.8 - 0.6 * math.exp(-0.3 * layer)
    smem = pl.BlockSpec(memory_space=pltpu.SMEM)
    pos = _key_position_columns(t, t, 128)
    slope_cols = _slope_pieces(slopes, 128).reshape(n_heads, 1, 128)
    return pl.pallas_call(
        functools.partial(_diff_kernel, t=t, out_scale=1.0 - lambda_init, lambda_init=lambda_init),
        grid=(B, n_heads, S // t),
        in_specs=[smem,
                  pl.BlockSpec((4, HEAD_DIM), lambda b, h, i: (0, 0)),
                  pl.BlockSpec((1, t, 128), lambda b, h, i: (b, i, 12 + h)),
                  pl.BlockSpec((1, S, 128), lambda b, h, i: (b, 0, 16 + h)),
                  pl.BlockSpec((1, 1, V_ROWS_128, S), lambda b, h, i: (b, h, 0, 0)),
                  pl.BlockSpec((t, 128), lambda b, h, i: (0, 0)),
                  pl.BlockSpec((1, 1, 128), lambda b, h, i: (h, 0, 0)),
                  pl.BlockSpec((1, 128), lambda b, h, i: (0, 0)),
                  pl.BlockSpec((1, t, 128), lambda b, h, i: (b, i, h))],
        out_specs=pl.BlockSpec((1, t, 256), lambda b, h, i: (b, i, h)),
        out_shape=jax.ShapeDtypeStruct((B, S, n_heads * 256), BF16),
        scratch_shapes=[pltpu.VMEM((1, 2 * t), F32), pltpu.VMEM((V_ROWS_128, 2 * t), F32),
                        pltpu.VMEM((t, 2 * t), jnp.int32), pltpu.SMEM((2,), F32)],
        compiler_params=_params(2, 1),
        name="diff_attention",
    )(slopes, lam_rows, proj, proj, vt_aug, pos, slope_cols, subln.reshape(1, 128), o_sb)


def _post_kernel(*refs, ff_chunk, final):
    mix_ref, x_ref, wo_ref, g_ref, w1_ref, w2_ref = refs[:6]
    gf_ref = refs[6] if final else None
    o_ref = refs[-1]
    x = x_ref[...] + _dot(mix_ref[...], wo_ref[...])
    hn = _rms(x, g_ref[...]).astype(BF16)
    acc = x
    for f in range(w1_ref.shape[1] // ff_chunk):
        sl = slice(f * ff_chunk, (f + 1) * ff_chunk)
        hid = jnp.maximum(_dot(hn, w1_ref[:, sl]), 0.0)
        acc = acc + _dot((hid * hid).astype(BF16), w2_ref[sl, :])
    if final:
        acc = _rms(acc, gf_ref[...])
    o_ref[...] = acc


def post_block(mix, x2d, w_out, g_mlp, w1, w2, g_final=None, tm=512, ff_chunk=1024):
    T, D = x2d.shape
    final = g_final is not None
    const = lambda i: (0, 0)
    in_specs = [pl.BlockSpec((tm, mix.shape[1]), lambda i: (i, 0)),
                pl.BlockSpec((tm, D), lambda i: (i, 0)),
                pl.BlockSpec(w_out.shape, const),
                pl.BlockSpec((1, D), const), pl.BlockSpec(w1.shape, const), pl.BlockSpec(w2.shape, const)]
    args = [mix, x2d, w_out, g_mlp.reshape(1, D), w1, w2]
    if final:
        in_specs.append(pl.BlockSpec((1, D), const))
        args.append(g_final.reshape(1, D))
    return pl.pallas_call(
        functools.partial(_post_kernel, ff_chunk=ff_chunk, final=final),
        grid=(T // tm,),
        in_specs=in_specs,
        out_specs=pl.BlockSpec((tm, D), lambda i: (i, 0)),
        out_shape=jax.ShapeDtypeStruct((T, D), F32),
        compiler_params=_params(1),
        name="post_block",
    )(*args)


def _compress_kernel(c_ref, pos_ref, w1_ref, w2_ref, o_ref):
    half = w1_ref.shape[1] // 2
    n_chunks = c_ref.shape[3] // CMP_STRIDE
    first = second = None
    for l in range(CMP_STRIDE):
        tok = c_ref[0, 0, 0, pl.ds(l, n_chunks, stride=CMP_STRIDE), :].astype(BF16)
        a = _dot(tok, w1_ref[0, l * HEAD_DIM:(l + 1) * HEAD_DIM, :])
        b = _dot(tok, w1_ref[0, half + l * HEAD_DIM:half + (l + 1) * HEAD_DIM, :])
        first = a if first is None else first + a
        second = b if second is None else second + b
    pos = jnp.broadcast_to(pos_ref[0], (8, 2 * half)).astype(BF16)
    pre = first + pltpu.roll(second, n_chunks - 1, 0) + _dot(pos, w1_ref[0])[0:1]
    hid = jax.nn.gelu(pre)
    o_ref[0, 0] = _dot(hid.astype(BF16), w2_ref[0]).astype(o_ref.dtype)


def compress_kv(tokens, pos_flat, w1, w2):
    _, B, G, S, Dh = tokens.shape
    n_chunks = S // CMP_STRIDE
    hidden = w1.shape[-1]
    return pl.pallas_call(
        _compress_kernel,
        grid=(2, B * G),
        in_specs=[pl.BlockSpec((1, 1, 1, S, Dh), lambda s, i: (s, i // G, i % G, 0, 0)),
                  pl.BlockSpec((1, 1, CMP_LEN * Dh), lambda s, i: (s, 0, 0)),
                  pl.BlockSpec((1, CMP_LEN * Dh, hidden), lambda s, i: (s, 0, 0)),
                  pl.BlockSpec((1, hidden, Dh), lambda s, i: (s, 0, 0))],
        out_specs=pl.BlockSpec((1, 1, n_chunks, Dh), lambda s, i: (s, i, 0, 0)),
        out_shape=jax.ShapeDtypeStruct((2, B * G, n_chunks, Dh), BF16),
        compiler_params=_params(2),
        name="compress_kv",
    )(tokens, pos_flat, w1, w2)


def _stack_heads(q):
    return jnp.concatenate([q[:, r * HEAD_DIM:(r + 1) * HEAD_DIM] for r in range(NSA_HPG)], axis=0)


def _slope_row(slopes_ref, g, tq):
    col = lax.broadcasted_iota(jnp.int32, (1, NSA_HPG * tq), 1)
    out = jnp.zeros((1, NSA_HPG * tq), F32)
    for r in range(NSA_HPG):
        out = jnp.where(col // tq == r, slopes_ref[g * NSA_HPG + r], out)
    return out


def _cmp_select_kernel(slopes_ref, q_ref, kc_ref, vct_ref, ovt_ref, ocmp_ref, sel_ref, hits_ref,
                       *, tq, n_sel, blocks_per_chunk, hit_tile):
    g = pl.program_id(1)
    t0 = pl.program_id(2) * tq
    cols = NSA_HPG * tq
    n_cmp = kc_ref.shape[2]
    slope_row = _slope_row(slopes_ref, g, tq)
    q_rows = _stack_heads(q_ref[0])

    def group_sum(pc):
        out = pc[:, 0:tq]
        for r in range(1, NSA_HPG):
            out = out + pc[:, r * tq:(r + 1) * tq]
        return out

    tpos = t0 + lax.broadcasted_iota(jnp.int32, (n_cmp, cols), 1) % tq
    cmp_end = lax.broadcasted_iota(jnp.int32, (n_cmp, cols), 0) * CMP_STRIDE + (CMP_LEN - 1)
    dc = (tpos - cmp_end).astype(F32)
    sc = jnp.where(dc >= 0, _dot_nt(kc_ref[0, 0], q_rows) - slope_row * dc, NEG_INF)
    e = jnp.exp2(sc - jnp.max(sc, axis=0, keepdims=True))
    any_valid = jnp.where(dc[0:1] >= 0, 1.0, 0.0)
    pc = e * (any_valid / jnp.sum(e, axis=0, keepdims=True))
    o_t = _dot(vct_ref[0, 0], pc.astype(BF16))
    for r in range(NSA_HPG):
        ocmp_ref[0, 0, r] = o_t[:, r * tq:(r + 1) * tq]
    hi, lo = _split_bf16(group_sum(pc))
    imp = _dot(ovt_ref[...], hi) + _dot(ovt_ref[...], lo)
    blk = lax.broadcasted_iota(jnp.int32, (n_sel, tq), 0)
    cur = (t0 + lax.broadcasted_iota(jnp.int32, (n_sel, tq), 1)) // SEL_LEN
    forced = (blk == 0) | (blk == cur) | (blk == cur - 1)
    imp = jnp.where(blk <= cur, imp, -1.0)
    topk = min(SEL_TOPK, n_sel)

    def pick(imp, count):
        for _ in range(count):
            best = jnp.max(imp, axis=0, keepdims=True)
            first = jnp.min(jnp.where(imp == best, blk, n_sel), axis=0, keepdims=True)
            imp = jnp.where(blk == first, PICKED, imp)
        return imp

    imp = lax.cond(t0 >= 2 * SEL_LEN,
                   lambda: pick(jnp.where(forced, PICKED, imp), topk - 3),
                   lambda: pick(jnp.where(forced, FORCE_SCORE, imp), topk))
    keep = (imp == PICKED) & (blk <= cur)
    sel_ref[0, 0] = jnp.where(keep, 0.0, -SEL_DROP).T.astype(sel_ref.dtype)
    n_chunks = n_sel // blocks_per_chunk
    member = (lax.broadcasted_iota(jnp.int32, (n_chunks, n_sel), 1) // blocks_per_chunk
              == lax.broadcasted_iota(jnp.int32, (n_chunks, n_sel), 0))
    per_query = _dot(jnp.where(member, 1.0, 0.0).astype(BF16),
                     jnp.where(keep, 1.0, 0.0).astype(BF16)).astype(BF16)
    for part in range(tq // hit_tile):
        hits_ref[0, 0, part] = _dot(per_query[:, part * hit_tile:(part + 1) * hit_tile],
                                    jnp.ones((hit_tile, 128), BF16))


def cmp_select(q, kc, vc_t, overlap_t, slopes, tk, hit_tile, tq=256):
    B, S, _ = q.shape
    n_cmp = kc.shape[2]
    n_sel = S // SEL_LEN
    G = NSA_GROUPS
    smem = pl.BlockSpec(memory_space=pltpu.SMEM)
    return pl.pallas_call(
        functools.partial(_cmp_select_kernel, tq=tq, n_sel=n_sel, blocks_per_chunk=tk // SEL_LEN,
                          hit_tile=hit_tile),
        grid=(B, G, S // tq),
        in_specs=[smem,
                  pl.BlockSpec((1, tq, 256), lambda b, g, i: (b, i, g)),
                  pl.BlockSpec((1, 1, n_cmp, HEAD_DIM), lambda b, g, i: (b, g, 0, 0)),
                  pl.BlockSpec((1, 1, HEAD_DIM, n_cmp), lambda b, g, i: (b, g, 0, 0)),
                  pl.BlockSpec((n_sel, n_cmp), lambda b, g, i: (0, 0))],
        out_specs=[pl.BlockSpec((1, 1, NSA_HPG, HEAD_DIM, tq), lambda b, g, i: (b, g, 0, 0, i)),
                   pl.BlockSpec((1, 1, tq, n_sel), lambda b, g, i: (b, g, i, 0)),
                   pl.BlockSpec((1, 1, tq // hit_tile, S // tk, 128), lambda b, g, i: (b, g, i, 0, 0))],
        out_shape=[jax.ShapeDtypeStruct((B, G, NSA_HPG, HEAD_DIM, S), F32),
                   jax.ShapeDtypeStruct((B, G, S, n_sel), BF16),
                   jax.ShapeDtypeStruct((B, G, S // hit_tile, S // tk, 128), F32)],
        compiler_params=_params(3),
        name="cmp_select",
    )(slopes, q, kc, vc_t, overlap_t)


def _sel_win_kernel(slopes_ref, active_ref, q_ref, sl_ref, ks_ref, vst_ref, kw_ref, vwt_ref, sel_ref,
                    ocmp_ref, gate_ref, o_ref, ms_ref, accs_ref, mw_ref, accw_ref, offset_ref, todo_ref,
                    knorm_ref, *, tq, tk):
    g = pl.program_id(1)
    t0 = pl.program_id(2) * tq
    cols = NSA_HPG * tq
    slope_row = _slope_row(slopes_ref, g, tq)
    q = q_ref[0]
    q_aug = jnp.concatenate(
        [jnp.concatenate([q[:, r * HEAD_DIM:(r + 1) * HEAD_DIM],
                          jnp.broadcast_to(sl_ref[0, r:r + 1, :], (tq, HEAD_DIM))], axis=1)
         for r in range(NSA_HPG)], axis=0)
    q_sel = jnp.concatenate([q_aug, jnp.concatenate([sel_ref[0, 0]] * NSA_HPG, axis=0)], axis=1)

    @pl.when(pl.program_id(2) == 0)
    def _():
        offset_ref[...] = (lax.broadcasted_iota(jnp.int32, (tk, cols), 0)
                           - lax.broadcasted_iota(jnp.int32, (tk, cols), 1) % tq)

        def body(c, best):
            rows = pl.ds(pl.multiple_of(c * tk, tk), tk)
            ks = ks_ref[0, 0, rows, :][:, :HEAD_DIM].astype(F32)
            kw = kw_ref[0, 0, rows, :][:, :HEAD_DIM].astype(F32)
            return (jnp.maximum(best[0], jnp.max(jnp.sum(ks * ks, axis=1))),
                    jnp.maximum(best[1], jnp.max(jnp.sum(kw * kw, axis=1))))

        best = lax.fori_loop(0, ks_ref.shape[2] // tk, body, (jnp.float32(0.0), jnp.float32(0.0)))
        knorm_ref[0] = best[0]
        knorm_ref[1] = best[1]

    sel_bound = _score_bound(q, knorm_ref[0])
    win_bound = _score_bound(q, knorm_ref[1])
    safe = 2.0 * jnp.maximum(sel_bound, win_bound) <= SAFE_EXP_LOG2

    def step(c, k_ref, q_rows, vt_ref, keep=None):
        inside = c >= 0
        start = pl.multiple_of(jnp.maximum(c, 0) * tk, tk)
        shift = c * tk - t0

        def scores():
            s_t = _dot_nt(k_ref[0, 0, pl.ds(start, tk), :], q_rows)
            if keep is None:
                return s_t
            lo, hi = keep
            offset = offset_ref[...]
            if hi is not None:
                return jnp.where(offset <= jnp.where(inside, hi - shift, -FAR), s_t, NEG_INF)
            return jnp.where(offset > jnp.where(inside, lo - shift, FAR), s_t, NEG_INF)

        kappa = jnp.where(inside, slope_row * shift.astype(F32), NEG_INF)
        return scores, kappa, lambda: vt_ref[0, 0, :, pl.ds(start, tk)]

    last = t0 // tk
    causal, recent, anything = (None, 0), (-WINDOW, None), (-FAR, None)
    first_steps = [step(last - 2, kw_ref, q_aug, vwt_ref, recent) + (1,),
                   step(last, ks_ref, q_sel, vst_ref, causal) + (0,),
                   step(last - 1, kw_ref, q_aug, vwt_ref, anything) + (1,),
                   step(last, kw_ref, q_aug, vwt_ref, causal) + (1,)]

    def note_active(c, n):
        hit = active_ref[0, 0, 0, 0, c] > 0

        @pl.when(hit)
        def _():
            todo_ref[n] = c

        return n + hit.astype(jnp.int32)

    n_todo = lax.fori_loop(0, last, note_active, 0)
    sel_step = lambda i: step(todo_ref[i], ks_ref, q_sel, vst_ref) + (0,)

    accs_ref[...] = jnp.zeros_like(accs_ref)
    accw_ref[...] = jnp.zeros_like(accw_ref)

    @pl.when(safe)
    def _():
        in_tile = slope_row * (lax.broadcasted_iota(jnp.int32, (1, cols), 1) % tq).astype(F32)
        block = lambda steps: _fixed_frame_block(steps, [accs_ref, accw_ref],
                                                 [sel_bound + in_tile, win_bound + in_tile])
        block(first_steps)
        _chunk_loop(n_todo, sel_step, block, (8, 4, 2, 1))

    @pl.when(jnp.logical_not(safe))
    def _():
        states = [(ms_ref, accs_ref), (mw_ref, accw_ref)]
        for m_ref, _ in states:
            m_ref[...] = jnp.full_like(m_ref, NEG_INF)
        block = lambda steps: _softmax_block(steps, states)
        block(first_steps)
        _chunk_loop(n_todo, sel_step, block, (4, 2, 1))

    o_sel = accs_ref[0:HEAD_DIM, :] / accs_ref[HEAD_DIM:HEAD_DIM + 1, :]
    o_win = accw_ref[0:HEAD_DIM, :] / accw_ref[HEAD_DIM:HEAD_DIM + 1, :]

    gates = gate_ref[0]
    outs = []
    for r in range(NSA_HPG):
        cs = slice(r * tq, (r + 1) * tq)
        outs.append(gates[3 * r:3 * r + 1] * ocmp_ref[0, 0, r] + gates[3 * r + 1:3 * r + 2] * o_sel[:, cs]
                    + gates[3 * r + 2:3 * r + 3] * o_win[:, cs])
    o_ref[0] = jnp.concatenate(outs, axis=0).T.astype(o_ref.dtype)


def sel_win_attention(q, active, slope_cols, ks, vs_t, kw, vw_t, sel_bias, o_cmp, gates_t,
                      slopes, tq=256, tk=256):
    B, S, _ = q.shape
    assert tq == tk and WINDOW == 2 * tk
    G = NSA_GROUPS
    n_sel = S // SEL_LEN
    cols = NSA_HPG * tq
    smem = pl.BlockSpec(memory_space=pltpu.SMEM)
    ks_spec = pl.BlockSpec((1, 1, S, ks.shape[-1]), lambda b, g, i: (b, g, 0, 0))
    kw_spec = pl.BlockSpec((1, 1, S, kw.shape[-1]), lambda b, g, i: (b, g, 0, 0))
    v_spec = pl.BlockSpec((1, 1, V_ROWS_64, S), lambda b, g, i: (b, g, 0, 0))
    return pl.pallas_call(
        functools.partial(_sel_win_kernel, tq=tq, tk=tk),
        grid=(B, G, S // tq),
        in_specs=[smem,
                  pl.BlockSpec((1, 1, 1, 1, S // tk), lambda b, g, i: (b, g, i, 0, 0),
                               memory_space=pltpu.SMEM),
                  pl.BlockSpec((1, tq, 256), lambda b, g, i: (b, i, g)),
                  pl.BlockSpec((1, NSA_HPG, HEAD_DIM), lambda b, g, i: (g, 0, 0)),
                  ks_spec, v_spec, kw_spec, v_spec,
                  pl.BlockSpec((1, 1, tq, n_sel), lambda b, g, i: (b, g, i, 0)),
                  pl.BlockSpec((1, 1, NSA_HPG, HEAD_DIM, tq), lambda b, g, i: (b, g, 0, 0, i)),
                  pl.BlockSpec((1, 4 * NSA_HPG, tq), lambda b, g, i: (b, g, i))],
        out_specs=pl.BlockSpec((1, tq, 256), lambda b, g, i: (b, i, g)),
        out_shape=jax.ShapeDtypeStruct((B, S, G * 256), BF16),
        scratch_shapes=[pltpu.VMEM((1, cols), F32), pltpu.VMEM((V_ROWS_64, cols), F32),
                        pltpu.VMEM((1, cols), F32), pltpu.VMEM((V_ROWS_64, cols), F32),
                        pltpu.VMEM((tk, cols), jnp.int32),
                        pltpu.SMEM((S // tk,), jnp.int32), pltpu.SMEM((2,), F32)],
        compiler_params=_params(2, 1),
        name="sel_win_attention",
    )(slopes, active, q, slope_cols, ks, vs_t, kw, vw_t, sel_bias, o_cmp, gates_t)


def _alibi_slopes_log2(n_heads):
    slopes = np.exp2(-8.0 * (np.arange(n_heads, dtype=np.float32) + 1.0) / n_heads)
    return (slopes.astype(np.float32) * np.float32(LOG2E)).astype(np.float32)


def even_layer_mix(x2d, B, S, norm_g, w_in, lam_q1, lam_k1, lam_q2, lam_k2, subln, layer):
    proj, sb_vt, df_vt = even_proj(x2d, B, S, norm_g, w_in.astype(BF16))
    proj = proj.reshape(B, S, -1)
    lam_rows = jnp.stack([lam_q1, lam_k1, lam_q2, lam_k2]).astype(F32)
    mix = diff_attention(proj, df_vt, lam_rows, subln.astype(F32), _alibi_slopes_log2(4), layer,
                         sb_attention(proj, sb_vt))
    return mix.reshape(B * S, -1)


def even_w_out_rows(w_out):
    d = w_out.shape[1]
    return w_out.reshape(2, 4, 128, d).transpose(1, 0, 2, 3).reshape(-1, d)


def odd_layer_mix(x2d, B, S, norm_g, w_in, pos_k, k_w1, k_w2, pos_v, v_w1, v_w2, tq=256, tk=256):
    G, Dh = NSA_GROUPS, HEAD_DIM
    q_width = G * NSA_HPG * Dh
    n_main = q_width + 6 * G * Dh
    per_group = NSA_HPG * N_GATES
    w_gate = w_in[:, n_main:n_main + G * per_group].reshape(-1, G, per_group)
    w_gate = jnp.pad(w_gate, ((0, 0), (0, 0), (0, 4 * NSA_HPG - per_group))).reshape(-1, 4 * NSA_HPG * G)
    w_gate = jnp.pad(w_gate, ((0, 0), (0, GATE_PAD - w_gate.shape[1])))
    w_all = jnp.concatenate([w_in[:, :n_main], w_gate], axis=1).astype(BF16)

    n_chunks = S // CMP_STRIDE
    n_sel = S // SEL_LEN
    onehot = (jnp.arange(S)[:, None] // SEL_LEN == jnp.arange(n_sel)[None, :]).astype(BF16)
    q, cmp_in, ks, kw, vs_t, vw_t, gates_t = odd_proj(x2d, B, S, norm_g, w_all,
                                                      _key_position_columns(S, tk, Dh), onehot)
    q = q.reshape(B, S, q_width)

    pos_flat = jnp.stack([pos_k, pos_v]).reshape(2, 1, CMP_LEN * Dh).astype(F32)
    w1 = jnp.stack([k_w1, v_w1]).astype(BF16)
    w2 = jnp.stack([k_w2, v_w2]).astype(BF16)
    cmp = compress_kv(cmp_in, pos_flat, w1, w2).reshape(2, B, G, n_chunks, Dh)

    cmp_start = jnp.arange(n_chunks) * CMP_STRIDE
    sel_start = jnp.arange(n_sel) * SEL_LEN
    overlap_t = ((cmp_start[None, :] < sel_start[:, None] + SEL_LEN)
                 & (sel_start[:, None] <= cmp_start[None, :] + CMP_LEN - 1)).astype(BF16)
    slopes = _alibi_slopes_log2(G * NSA_HPG)
    slope_cols = _slope_pieces(slopes, Dh).reshape(G, NSA_HPG, Dh)

    o_cmp, sel_bias, hits = cmp_select(q, cmp[0], cmp[1].transpose(0, 1, 3, 2), overlap_t, slopes, tk, tq)
    active = (hits[..., 0] > 0).astype(jnp.int32)[:, :, :, None, :]
    o = sel_win_attention(q, active, slope_cols, ks, vs_t, kw, vw_t, sel_bias, o_cmp, gates_t, slopes,
                          tq=tq, tk=tk)
    return o.reshape(B * S, q_width)


def kernel(x, attn_norm, mlp_norm, final_norm, ev_w_in, ev_lam_q1, ev_lam_k1, ev_lam_q2, ev_lam_k2,
           ev_subln, ev_w_out, od_w_in, od_cmp_pos_k, od_cmp_k_w1, od_cmp_k_w2, od_cmp_pos_v,
           od_cmp_v_w1, od_cmp_v_w2, od_w_out, mlp_w1, mlp_w2):
    B, S, D = x.shape
    depth = attn_norm.shape[0]
    x2d = x.reshape(B * S, D)
    for layer in range(depth):
        idx = layer // 2
        if layer % 2 == 0:
            mix = even_layer_mix(x2d, B, S, attn_norm[layer], ev_w_in[idx], ev_lam_q1[idx],
                                 ev_lam_k1[idx], ev_lam_q2[idx], ev_lam_k2[idx], ev_subln[idx], layer)
            w_out = even_w_out_rows(ev_w_out[idx])
        else:
            mix = odd_layer_mix(x2d, B, S, attn_norm[layer], od_w_in[idx], od_cmp_pos_k[idx],
                                od_cmp_k_w1[idx], od_cmp_k_w2[idx], od_cmp_pos_v[idx],
                                od_cmp_v_w1[idx], od_cmp_v_w2[idx])
            w_out = od_w_out[idx]
        g_final = final_norm if layer == depth - 1 else None
        x2d = post_block(mix, x2d, w_out.astype(BF16), mlp_norm[layer], mlp_w1[layer].astype(BF16),
                         mlp_w2[layer].astype(BF16), g_final)
    return x2d.reshape(B, S, D)
```

```python
import functools
import math

import jax
import jax.numpy as jnp
import numpy as np
from jax import lax
from jax.experimental import pallas as pl
from jax.experimental.pallas import tpu as pltpu

F32 = jnp.float32
BF16 = jnp.bfloat16

HEAD_DIM = 64
RMS_EPS = 1e-6
NEG_INF = -1e30
FORCE_SCORE = 1e6
NSA_GROUPS = 4
NSA_HPG = 4
CMP_LEN = 32
CMP_STRIDE = 16
SEL_LEN = 64
SEL_TOPK = 16
WINDOW = 512
N_GATES = 3
GATE_PAD = 128
SEL_DROP = 2.0 ** 24
BF16_EXACT_INT = 256
V_ROWS_64 = 80
V_ROWS_128 = 144

LOG2E = math.log2(math.e)
Q_SCALE = HEAD_DIM ** -0.5 * LOG2E
UNDERFLOW_LOG2 = -160.0
SAFE_EXP_LOG2 = 100.0
FAR = 1 << 30
PICKED = -2.0

VMEM_LIMIT = 56 * 1024 * 1024


def _params(n_parallel, n_arbitrary=0):
    return pltpu.CompilerParams(dimension_semantics=("parallel",) * n_parallel + ("arbitrary",) * n_arbitrary,
                                vmem_limit_bytes=VMEM_LIMIT)


def _rms(x, g):
    ms = jnp.mean(x * x, axis=-1, keepdims=True)
    return x * lax.rsqrt(ms + RMS_EPS) * g


def _dot(a, b):
    return jnp.dot(a, b, preferred_element_type=F32)


def _dot_nt(a, b):
    return lax.dot_general(a, b, (((1,), (1,)), ((), ())), preferred_element_type=F32)


def _split_bf16(x):
    hi = x.astype(BF16)
    lo = (x - hi.astype(F32)).astype(BF16)
    return hi, lo


def _slope_pieces(slopes, width):
    def top_bits(x):
        return (x.view(np.uint32) & np.uint32(0xFFFF0000)).view(np.float32)

    s1 = top_bits(slopes)
    r1 = slopes - s1
    s2 = top_bits(r1)
    s3 = top_bits(r1 - s2)
    out = np.zeros((slopes.shape[0], width), np.float32)
    out[:, :6] = np.stack([s1, s2, s3, s1, s2, s3], axis=1)
    return jnp.asarray(out).astype(BF16)


def _key_position_columns(n, tk, width):
    j = jnp.arange(n) % tk
    a = (j // BF16_EXACT_INT) * BF16_EXACT_INT
    b = j % BF16_EXACT_INT
    cols = jnp.stack([a, a, a, b, b, b], axis=1).astype(BF16)
    return jnp.pad(cols, ((0, 0), (0, width - cols.shape[1])))


def _softmax_block(steps, states):
    s_all = [scores() for scores, _, _, _ in steps]
    live = {k: (states[k][0][...], states[k][1][...]) for k in sorted({k for _, _, _, k in steps})}
    for (_, kap, values, k), s_t in zip(steps, s_all):
        m_run, acc = live[k]
        m_new = jnp.maximum(m_run, jnp.max(s_t, axis=0, keepdims=True) + kap)
        p = jnp.exp2(s_t - (m_new - kap)).astype(BF16)
        live[k] = (m_new, jnp.exp2(m_run - m_new) * acc + _dot(values(), p))
    for k, (m_run, acc) in live.items():
        states[k][0][...] = m_run
        states[k][1][...] = acc


def _fixed_frame_block(steps, acc_refs, frames):
    s_all = [scores() for scores, _, _, _ in steps]
    live = {k: acc_refs[k][...] for k in sorted({k for _, _, _, k in steps})}
    for (_, kap, values, k), s_t in zip(steps, s_all):
        live[k] = live[k] + _dot(values(), jnp.exp2(s_t - (frames[k] - kap)).astype(BF16))
    for k, acc in live.items():
        acc_refs[k][...] = acc


def _chunk_loop(n_steps, step, block, groups):
    done = 0
    for group in groups:
        def body(i, carry, group=group, done=done):
            block([step(done + group * i + j) for j in range(group)])
            return carry

        n_groups = (n_steps - done) // group
        lax.fori_loop(0, n_groups, body, 0)
        done = done + group * n_groups


def _score_bound(q, k_norm2):
    q32 = q.astype(F32)
    width = q.shape[1]
    member = (lax.broadcasted_iota(jnp.int32, (width, 128), 0) // HEAD_DIM
              == lax.broadcasted_iota(jnp.int32, (width, 128), 1))
    head_norm2 = _dot((q32 * q32).astype(BF16), jnp.where(member, 1.0, 0.0).astype(BF16))
    return jnp.sqrt(jnp.max(head_norm2) * k_norm2 * 1.05)


PROJ_CHUNK = 512


def _ones_row_tail(rows, width):
    return jnp.where(lax.broadcasted_iota(jnp.int32, (rows, width), 0) == 0, 1.0, 0.0).astype(BF16)


def _even_proj_kernel(x_ref, g_ref, w_ref, o_ref, sbv_ref, dfv_ref):
    tm = x_ref.shape[0]
    xn = _rms(x_ref[...], g_ref[...]).astype(BF16)
    tail = _ones_row_tail(V_ROWS_128 - 2 * HEAD_DIM, tm)
    for c in range(w_ref.shape[1] // PROJ_CHUNK):
        sl = slice(c * PROJ_CHUNK, (c + 1) * PROJ_CHUNK)
        res = _dot(xn, w_ref[:, sl])
        if c in (0, 3):
            res = res * Q_SCALE
        o_ref[:, sl] = res.astype(BF16)
        if c == 2:
            sbv_ref[0] = res.T.astype(BF16)
        if c == 5:
            v_t = res.T.astype(BF16)
            for h in range(dfv_ref.shape[1]):
                dfv_ref[0, h, 0:2 * HEAD_DIM, :] = v_t[h * 2 * HEAD_DIM:(h + 1) * 2 * HEAD_DIM]
                dfv_ref[0, h, 2 * HEAD_DIM:V_ROWS_128, :] = tail


def even_proj(x2d, B, S, g, w_bf16, tm=512):
    T, D = x2d.shape
    n_w = w_bf16.shape[1]
    per_seq = S // tm
    assert n_w == 6 * PROJ_CHUNK
    return pl.pallas_call(
        _even_proj_kernel,
        grid=(T // tm,),
        in_specs=[pl.BlockSpec((tm, D), lambda i: (i, 0)),
                  pl.BlockSpec((1, D), lambda i: (0, 0)),
                  pl.BlockSpec((D, n_w), lambda i: (0, 0))],
        out_specs=[pl.BlockSpec((tm, n_w), lambda i: (i, 0)),
                   pl.BlockSpec((1, PROJ_CHUNK, tm), lambda i: (i // per_seq, 0, i % per_seq)),
                   pl.BlockSpec((1, 4, V_ROWS_128, tm), lambda i: (i // per_seq, 0, 0, i % per_seq))],
        out_shape=[jax.ShapeDtypeStruct((T, n_w), BF16),
                   jax.ShapeDtypeStruct((B, PROJ_CHUNK, S), BF16),
                   jax.ShapeDtypeStruct((B, 4, V_ROWS_128, S), BF16)],
        compiler_params=_params(1),
        name="even_proj",
    )(x2d, g.reshape(1, D), w_bf16)


def _odd_proj_kernel(x_ref, g_ref, w_ref, pos_ref, onehot_ref,
                     q_ref, cin_ref, ks_ref, kw_ref, vst_ref, vwt_ref, gate_ref):
    tm = x_ref.shape[0]
    G, Dh = NSA_GROUPS, HEAD_DIM
    kv = G * Dh
    xn = _rms(x_ref[...], g_ref[...]).astype(BF16)
    tail = _ones_row_tail(V_ROWS_64 - Dh, tm)
    pos = pos_ref[...]
    onehot = onehot_ref[...]
    q_width = q_ref.shape[1]
    for c in range(q_width // PROJ_CHUNK):
        sl = slice(c * PROJ_CHUNK, (c + 1) * PROJ_CHUNK)
        q_ref[:, sl] = (_dot(xn, w_ref[:, sl]) * Q_SCALE).astype(BF16)

    def pair(j):
        return _dot(xn, w_ref[:, q_width + j * 2 * kv:q_width + (j + 1) * 2 * kv])

    res = pair(0)
    for which in range(2):
        for g in range(G):
            cin_ref[which, 0, g] = res[:, which * kv + g * Dh:which * kv + (g + 1) * Dh]

    for j, k_ref, vt_ref, extra in ((1, ks_ref, vst_ref, [pos, onehot]), (2, kw_ref, vwt_ref, [pos])):
        res = pair(j)
        v_t = res[:, kv:2 * kv].T.astype(BF16)
        for g in range(G):
            k_ref[0, g] = jnp.concatenate([res[:, g * Dh:(g + 1) * Dh].astype(BF16)] + extra, axis=1)
            vt_ref[0, g, 0:Dh, :] = v_t[g * Dh:(g + 1) * Dh]
            vt_ref[0, g, Dh:V_ROWS_64, :] = tail

    logits = _dot(xn, w_ref[:, q_width + 6 * kv:q_width + 6 * kv + GATE_PAD])
    gate_ref[0] = jax.nn.sigmoid(logits).T[0:gate_ref.shape[1]]


def odd_proj(x2d, B, S, g, w_bf16, pos_cols, onehot, tm=512):
    T, D = x2d.shape
    G, Dh = NSA_GROUPS, HEAD_DIM
    q_width = G * NSA_HPG * Dh
    n_sel = onehot.shape[1]
    per_seq = S // tm
    seq_tile = lambda i: (i // per_seq, 0, i % per_seq, 0)
    seq_tile_t = lambda i: (i // per_seq, 0, 0, i % per_seq)
    return pl.pallas_call(
        _odd_proj_kernel,
        grid=(T // tm,),
        in_specs=[pl.BlockSpec((tm, D), lambda i: (i, 0)),
                  pl.BlockSpec((1, D), lambda i: (0, 0)),
                  pl.BlockSpec(w_bf16.shape, lambda i: (0, 0)),
                  pl.BlockSpec((tm, Dh), lambda i: (i % per_seq, 0)),
                  pl.BlockSpec((tm, n_sel), lambda i: (i % per_seq, 0))],
        out_specs=[pl.BlockSpec((tm, q_width), lambda i: (i, 0)),
                   pl.BlockSpec((2, 1, G, tm, Dh), lambda i: (0, i // per_seq, 0, i % per_seq, 0)),
                   pl.BlockSpec((1, G, tm, 2 * Dh + n_sel), seq_tile),
                   pl.BlockSpec((1, G, tm, 2 * Dh), seq_tile),
                   pl.BlockSpec((1, G, V_ROWS_64, tm), seq_tile_t),
                   pl.BlockSpec((1, G, V_ROWS_64, tm), seq_tile_t),
                   pl.BlockSpec((1, 4 * NSA_HPG * G, tm), lambda i: (i // per_seq, 0, i % per_seq))],
        out_shape=[jax.ShapeDtypeStruct((T, q_width), BF16),
                   jax.ShapeDtypeStruct((2, B, G, S, Dh), F32),
                   jax.ShapeDtypeStruct((B, G, S, 2 * Dh + n_sel), BF16),
                   jax.ShapeDtypeStruct((B, G, S, 2 * Dh), BF16),
                   jax.ShapeDtypeStruct((B, G, V_ROWS_64, S), BF16),
                   jax.ShapeDtypeStruct((B, G, V_ROWS_64, S), BF16),
                   jax.ShapeDtypeStruct((B, 4 * NSA_HPG * G, S), F32)],
        compiler_params=_params(1),
        name="odd_proj",
    )(x2d, g.reshape(1, D), w_bf16, pos_cols, onehot)


def _head_pair_rows(q, t):
    lane = lax.broadcasted_iota(jnp.int32, (t, 2 * HEAD_DIM), 1)
    zero = jnp.zeros_like(q)
    return jnp.where(lane < HEAD_DIM, q, zero), jnp.where(lane >= HEAD_DIM, q, zero)


def _sb_kernel(q_ref, k_ref, vt_ref, o_ref, acc_ref, carry_ref, upper_ref, *, t):
    qi = pl.program_id(2)
    cols = 2 * t
    q_both = jnp.concatenate(_head_pair_rows(q_ref[0], t), axis=0)

    @pl.when(qi == 0)
    def _():
        s_idx = lax.broadcasted_iota(jnp.int32, (t, 2 * t), 0)
        j_idx = lax.broadcasted_iota(jnp.int32, (t, 2 * t), 1) % t
        upper_ref[...] = jnp.where(j_idx > s_idx, 1.0, 0.0).astype(BF16)

    upper2 = upper_ref[...]
    key = lax.broadcasted_iota(jnp.int32, (t, cols), 0)
    qry = lax.broadcasted_iota(jnp.int32, (t, cols), 1) % t
    past_diag = key < qry

    acc_ref[...] = jnp.zeros_like(acc_ref)
    carry_ref[...] = jnp.zeros_like(carry_ref)

    def chunks(specs):
        starts = [pl.multiple_of(kc * t, t) for kc, _ in specs]
        z_all = [_dot_nt(k_ref[0, pl.ds(start, t), :], q_both) for start in starts]
        carry = carry_ref[...]
        acc = acc_ref[...]
        for (_, masked), start, z in zip(specs, starts, z_all):
            drop = jnp.maximum(z, 0.0) + jnp.log2(1.0 + jnp.exp2(jnp.abs(z) * -1.0))
            log_beta = z - drop
            if masked:
                drop = jnp.where(past_diag, drop, 0.0)
            hi, lo = _split_bf16(drop)
            tail = _dot(upper2, jnp.concatenate([hi, lo], axis=0))
            w = jnp.exp2(log_beta - tail - carry)
            if masked:
                w = jnp.where(past_diag, w, 0.0)
            carry = carry + tail[0:1] + drop[0:1]
            acc = acc + _dot(vt_ref[0, :, pl.ds(start, t)], w.astype(BF16))
        carry_ref[...] = carry
        acc_ref[...] = acc
        return jnp.min(carry)

    least = lax.cond(qi > 0, lambda: chunks([(qi, True), (qi - 1, False)]), lambda: chunks([(qi, True)]))

    def cond(state):
        j, least_carry = state
        return (j < qi) & (least_carry < -UNDERFLOW_LOG2)

    def body(state):
        j, _ = state
        return j + 1, chunks([(qi - 1 - j, False)])

    lax.while_loop(cond, body, (1, least))
    row = lax.broadcasted_iota(jnp.int32, (2 * HEAD_DIM, t), 0)
    o_t = jnp.where(row < HEAD_DIM, acc_ref[:, :t], acc_ref[:, t:])
    o_ref[0] = o_t.T.astype(o_ref.dtype)


def sb_attention(proj, v_t, t=256):
    B, S, _ = proj.shape
    n_pairs = 4
    return pl.pallas_call(
        functools.partial(_sb_kernel, t=t),
        grid=(B, n_pairs, S // t),
        in_specs=[pl.BlockSpec((1, t, 128), lambda b, p, i: (b, i, p)),
                  pl.BlockSpec((1, S, 128), lambda b, p, i: (b, 0, n_pairs + p)),
                  pl.BlockSpec((1, 128, S), lambda b, p, i: (b, p, 0))],
        out_specs=pl.BlockSpec((1, t, 128), lambda b, p, i: (b, i, p)),
        out_shape=jax.ShapeDtypeStruct((B, S, n_pairs * 128), BF16),
        scratch_shapes=[pltpu.VMEM((128, 2 * t), F32), pltpu.VMEM((1, 2 * t), F32),
                        pltpu.VMEM((t, 2 * t), BF16)],
        compiler_params=_params(2, 1),
        name="sb_attention",
    )(proj, proj, v_t)


def _diff_kernel(slopes_ref, lam_ref, q_ref, k_ref, vt_ref, pos_ref, sl_ref, subln_ref, sb_ref, o_ref,
                 m_ref, acc_ref, offset_ref, knorm_ref, *, t, out_scale, lambda_init):
    h = pl.program_id(1)
    qi = pl.program_id(2)
    slope = slopes_ref[h]
    cols = 2 * t
    slope_cols = jnp.broadcast_to(sl_ref[0], (t, 128))
    q_parts = _head_pair_rows(q_ref[0], t)
    q_both = jnp.concatenate([jnp.concatenate([qc, slope_cols], axis=1) for qc in q_parts],
                             axis=0)
    lane = lax.broadcasted_iota(jnp.int32, (t, 2 * HEAD_DIM), 1)

    @pl.when(qi == 0)
    def _():
        def body(c, best):
            k = k_ref[0, pl.ds(pl.multiple_of(c * t, t), t), :].astype(F32)
            k2 = k * k
            return (jnp.maximum(best[0], jnp.max(jnp.sum(jnp.where(lane < HEAD_DIM, k2, 0.0), axis=1))),
                    jnp.maximum(best[1], jnp.max(jnp.sum(jnp.where(lane >= HEAD_DIM, k2, 0.0), axis=1))))

        best = lax.fori_loop(0, k_ref.shape[1] // t, body, (jnp.float32(0.0), jnp.float32(0.0)))
        knorm_ref[0] = best[0]
        knorm_ref[1] = best[1]
        offset_ref[...] = (lax.broadcasted_iota(jnp.int32, (t, cols), 0)
                           - lax.broadcasted_iota(jnp.int32, (t, cols), 1) % t)

    qk_bound = _score_bound(q_ref[0], jnp.maximum(knorm_ref[0], knorm_ref[1]))
    safe = 2.0 * qk_bound <= SAFE_EXP_LOG2
    col = lax.broadcasted_iota(jnp.int32, (1, cols), 1)

    pos = pos_ref[...]

    def step(kc, diagonal=False):
        start = pl.multiple_of(kc * t, t)

        def scores():
            s_t = _dot_nt(jnp.concatenate([k_ref[0, pl.ds(start, t), :], pos], axis=1), q_both)
            return jnp.where(offset_ref[...] <= 0, s_t, NEG_INF) if diagonal else s_t

        return scores, slope * ((kc - qi) * t).astype(F32), lambda: vt_ref[0, 0, :, pl.ds(start, t)], 0

    acc_ref[...] = jnp.zeros_like(acc_ref)

    @pl.when(safe)
    def _():
        frame = qk_bound + slope * (col % t).astype(F32)
        block = lambda steps: _fixed_frame_block(steps, [acc_ref], [frame])
        block([step(qi, True)])
        n_back = jnp.ceil((slope * (t - 1) - UNDERFLOW_LOG2) / (slope * t)).astype(jnp.int32)
        _chunk_loop(jnp.minimum(n_back, qi), lambda i: step(qi - 1 - i), block, (8, 4, 2, 1))

    @pl.when(jnp.logical_not(safe))
    def _():
        m_ref[...] = jnp.full_like(m_ref, NEG_INF)
        block = lambda steps: _softmax_block(steps, [(m_ref, acc_ref)])
        block([step(qi, True)])
        excess = jnp.max(qk_bound - m_ref[...]) + slope * (t - 1)
        n_back = jnp.clip(jnp.ceil((excess - UNDERFLOW_LOG2) / (slope * t)).astype(jnp.int32), 0, qi)
        _chunk_loop(n_back, lambda i: step(qi - 1 - i), block, (4, 2, 1))

    lam_terms = lam_ref[...]
    lam = (jnp.exp(jnp.sum(lam_terms[0:1] * lam_terms[1:2], axis=1, keepdims=True))
           - jnp.exp(jnp.sum(lam_terms[2:3] * lam_terms[3:4], axis=1, keepdims=True))
           + lambda_init)
    o_t = acc_ref[0:128, :] / acc_ref[128:129, :]
    o = (o_t[:, :t] - lam * o_t[:, t:]).T
    o_ref[0, :, 0:128] = sb_ref[0]
    o_ref[0, :, 128:256] = (_rms(o, subln_ref[...]) * out_scale).astype(o_ref.dtype)


def diff_attention(proj, vt_aug, lam_rows, subln, slopes, layer, o_sb, t=512):
    B, S, _ = proj.shape
    n_heads = 4
    lambda_init = 0.8 - 0.6 * math.exp(-0.3 * layer)
    smem = pl.BlockSpec(memory_space=pltpu.SMEM)
    pos = _key_position_columns(t, t, 128)
    slope_cols = _slope_pieces(slopes, 128).reshape(n_heads, 1, 128)
    return pl.pallas_call(
        functools.partial(_diff_kernel, t=t, out_scale=1.0 - lambda_init, lambda_init=lambda_init),
        grid=(B, n_heads, S // t),
        in_specs=[smem,
                  pl.BlockSpec((4, HEAD_DIM), lambda b, h, i: (0, 0)),
                  pl.BlockSpec((1, t, 128), lambda b, h, i: (b, i, 12 + h)),
                  pl.BlockSpec((1, S, 128), lambda b, h, i: (b, 0, 16 + h)),
                  pl.BlockSpec((1, 1, V_ROWS_128, S), lambda b, h, i: (b, h, 0, 0)),
                  pl.BlockSpec((t, 128), lambda b, h, i: (0, 0)),
                  pl.BlockSpec((1, 1, 128), lambda b, h, i: (h, 0, 0)),
                  pl.BlockSpec((1, 128), lambda b, h, i: (0, 0)),
                  pl.BlockSpec((1, t, 128), lambda b, h, i: (b, i, h))],
        out_specs=pl.BlockSpec((1, t, 256), lambda b, h, i: (b, i, h)),
        out_shape=jax.ShapeDtypeStruct((B, S, n_heads * 256), BF16),
        scratch_shapes=[pltpu.VMEM((1, 2 * t), F32), pltpu.VMEM((V_ROWS_128, 2 * t), F32),
                        pltpu.VMEM((t, 2 * t), jnp.int32), pltpu.SMEM((2,), F32)],
        compiler_params=_params(2, 1),
        name="diff_attention",
    )(slopes, lam_rows, proj, proj, vt_aug, pos, slope_cols, subln.reshape(1, 128), o_sb)


def _post_kernel(*refs, ff_chunk, final):
    mix_ref, x_ref, wo_ref, g_ref, w1_ref, w2_ref = refs[:6]
    gf_ref = refs[6] if final else None
    o_ref = refs[-1]
    x = x_ref[...] + _dot(mix_ref[...], wo_ref[...])
    hn = _rms(x, g_ref[...]).astype(BF16)
    acc = x
    for f in range(w1_ref.shape[1] // ff_chunk):
        sl = slice(f * ff_chunk, (f + 1) * ff_chunk)
        hid = jnp.maximum(_dot(hn, w1_ref[:, sl]), 0.0)
        acc = acc + _dot((hid * hid).astype(BF16), w2_ref[sl, :])
    if final:
        acc = _rms(acc, gf_ref[...])
    o_ref[...] = acc


def post_block(mix, x2d, w_out, g_mlp, w1, w2, g_final=None, tm=512, ff_chunk=1024):
    T, D = x2d.shape
    final = g_final is not None
    const = lambda i: (0, 0)
    in_specs = [pl.BlockSpec((tm, mix.shape[1]), lambda i: (i, 0)),
                pl.BlockSpec((tm, D), lambda i: (i, 0)),
                pl.BlockSpec(w_out.shape, const),
                pl.BlockSpec((1, D), const), pl.BlockSpec(w1.shape, const), pl.BlockSpec(w2.shape, const)]
    args = [mix, x2d, w_out, g_mlp.reshape(1, D), w1, w2]
    if final:
        in_specs.append(pl.BlockSpec((1, D), const))
        args.append(g_final.reshape(1, D))
    return pl.pallas_call(
        functools.partial(_post_kernel, ff_chunk=ff_chunk, final=final),
        grid=(T // tm,),
        in_specs=in_specs,
        out_specs=pl.BlockSpec((tm, D), lambda i: (i, 0)),
        out_shape=jax.ShapeDtypeStruct((T, D), F32),
        compiler_params=_params(1),
        name="post_block",
    )(*args)


def _compress_kernel(c_ref, pos_ref, w1_ref, w2_ref, o_ref):
    half = w1_ref.shape[1] // 2
    n_chunks = c_ref.shape[3] // CMP_STRIDE
    first = second = None
    for l in range(CMP_STRIDE):
        tok = c_ref[0, 0, 0, pl.ds(l, n_chunks, stride=CMP_STRIDE), :].astype(BF16)
        a = _dot(tok, w1_ref[0, l * HEAD_DIM:(l + 1) * HEAD_DIM, :])
        b = _dot(tok, w1_ref[0, half + l * HEAD_DIM:half + (l + 1) * HEAD_DIM, :])
        first = a if first is None else first + a
        second = b if second is None else second + b
    pos = jnp.broadcast_to(pos_ref[0], (8, 2 * half)).astype(BF16)
    pre = first + pltpu.roll(second, n_chunks - 1, 0) + _dot(pos, w1_ref[0])[0:1]
    hid = jax.nn.gelu(pre)
    o_ref[0, 0] = _dot(hid.astype(BF16), w2_ref[0]).astype(o_ref.dtype)


def compress_kv(tokens, pos_flat, w1, w2):
    _, B, G, S, Dh = tokens.shape
    n_chunks = S // CMP_STRIDE
    hidden = w1.shape[-1]
    return pl.pallas_call(
        _compress_kernel,
        grid=(2, B * G),
        in_specs=[pl.BlockSpec((1, 1, 1, S, Dh), lambda s, i: (s, i // G, i % G, 0, 0)),
                  pl.BlockSpec((1, 1, CMP_LEN * Dh), lambda s, i: (s, 0, 0)),
                  pl.BlockSpec((1, CMP_LEN * Dh, hidden), lambda s, i: (s, 0, 0)),
                  pl.BlockSpec((1, hidden, Dh), lambda s, i: (s, 0, 0))],
        out_specs=pl.BlockSpec((1, 1, n_chunks, Dh), lambda s, i: (s, i, 0, 0)),
        out_shape=jax.ShapeDtypeStruct((2, B * G, n_chunks, Dh), BF16),
        compiler_params=_params(2),
        name="compress_kv",
    )(tokens, pos_flat, w1, w2)


def _stack_heads(q):
    return jnp.concatenate([q[:, r * HEAD_DIM:(r + 1) * HEAD_DIM] for r in range(NSA_HPG)], axis=0)


def _slope_row(slopes_ref, g, tq):
    col = lax.broadcasted_iota(jnp.int32, (1, NSA_HPG * tq), 1)
    out = jnp.zeros((1, NSA_HPG * tq), F32)
    for r in range(NSA_HPG):
        out = jnp.where(col // tq == r, slopes_ref[g * NSA_HPG + r], out)
    return out


def _cmp_select_kernel(slopes_ref, q_ref, kc_ref, vct_ref, ovt_ref, ocmp_ref, sel_ref, hits_ref,
                       dist_ref, *, tq, n_sel, blocks_per_chunk, hit_tile):
    g = pl.program_id(1)
    t0 = pl.program_id(2) * tq
    cols = NSA_HPG * tq
    n_cmp = kc_ref.shape[2]
    slope_row = _slope_row(slopes_ref, g, tq)
    q_rows = _stack_heads(q_ref[0])

    def group_sum(pc):
        out = pc[:, 0:tq]
        for r in range(1, NSA_HPG):
            out = out + pc[:, r * tq:(r + 1) * tq]
        return out

    @pl.when(pl.program_id(2) == 0)
    def _():
        dist_ref[...] = (lax.broadcasted_iota(jnp.int32, (n_cmp, cols), 1) % tq
                         - lax.broadcasted_iota(jnp.int32, (n_cmp, cols), 0) * CMP_STRIDE
                         - (CMP_LEN - 1)).astype(F32)

    dc = dist_ref[...] + t0.astype(F32)
    sc = jnp.where(dc >= 0, _dot_nt(kc_ref[0, 0], q_rows) - slope_row * dc, NEG_INF)
    e = jnp.exp2(sc - jnp.max(sc, axis=0, keepdims=True))
    any_valid = jnp.where(dc[0:1] >= 0, 1.0, 0.0)
    pc = e * (any_valid / jnp.sum(e, axis=0, keepdims=True))
    o_t = _dot(vct_ref[0, 0], pc.astype(BF16))
    for r in range(NSA_HPG):
        ocmp_ref[0, 0, r] = o_t[:, r * tq:(r + 1) * tq]
    hi, lo = _split_bf16(group_sum(pc))
    imp = _dot(ovt_ref[...], hi) + _dot(ovt_ref[...], lo)
    blk = lax.broadcasted_iota(jnp.int32, (n_sel, tq), 0)
    cur = (t0 + lax.broadcasted_iota(jnp.int32, (n_sel, tq), 1)) // SEL_LEN
    forced = (blk == 0) | (blk == cur) | (blk == cur - 1)
    imp = jnp.where(blk <= cur, imp, -1.0)
    topk = min(SEL_TOPK, n_sel)

    def pick(imp, count):
        for _ in range(count):
            best = jnp.max(imp, axis=0, keepdims=True)
            first = jnp.min(jnp.where(imp == best, blk, n_sel), axis=0, keepdims=True)
            imp = jnp.where(blk == first, PICKED, imp)
        return imp

    imp = lax.cond(t0 >= 2 * SEL_LEN,
                   lambda: pick(jnp.where(forced, PICKED, imp), topk - 3),
                   lambda: pick(jnp.where(forced, FORCE_SCORE, imp), topk))
    keep = (imp == PICKED) & (blk <= cur)
    sel_ref[0, 0] = jnp.where(keep, 0.0, -SEL_DROP).T.astype(sel_ref.dtype)
    n_chunks = n_sel // blocks_per_chunk
    member = (lax.broadcasted_iota(jnp.int32, (n_chunks, n_sel), 1) // blocks_per_chunk
              == lax.broadcasted_iota(jnp.int32, (n_chunks, n_sel), 0))
    per_query = _dot(jnp.where(member, 1.0, 0.0).astype(BF16),
                     jnp.where(keep, 1.0, 0.0).astype(BF16)).astype(BF16)
    for part in range(tq // hit_tile):
        hits_ref[0, 0, part] = _dot(per_query[:, part * hit_tile:(part + 1) * hit_tile],
                                    jnp.ones((hit_tile, 128), BF16))


def cmp_select(q, kc, vc_t, overlap_t, slopes, tk, hit_tile, tq=256):
    B, S, _ = q.shape
    n_cmp = kc.shape[2]
    n_sel = S // SEL_LEN
    G = NSA_GROUPS
    smem = pl.BlockSpec(memory_space=pltpu.SMEM)
    return pl.pallas_call(
        functools.partial(_cmp_select_kernel, tq=tq, n_sel=n_sel, blocks_per_chunk=tk // SEL_LEN,
                          hit_tile=hit_tile),
        grid=(B, G, S // tq),
        in_specs=[smem,
                  pl.BlockSpec((1, tq, 256), lambda b, g, i: (b, i, g)),
                  pl.BlockSpec((1, 1, n_cmp, HEAD_DIM), lambda b, g, i: (b, g, 0, 0)),
                  pl.BlockSpec((1, 1, HEAD_DIM, n_cmp), lambda b, g, i: (b, g, 0, 0)),
                  pl.BlockSpec((n_sel, n_cmp), lambda b, g, i: (0, 0))],
        out_specs=[pl.BlockSpec((1, 1, NSA_HPG, HEAD_DIM, tq), lambda b, g, i: (b, g, 0, 0, i)),
                   pl.BlockSpec((1, 1, tq, n_sel), lambda b, g, i: (b, g, i, 0)),
                   pl.BlockSpec((1, 1, tq // hit_tile, S // tk, 128), lambda b, g, i: (b, g, i, 0, 0))],
        out_shape=[jax.ShapeDtypeStruct((B, G, NSA_HPG, HEAD_DIM, S), F32),
                   jax.ShapeDtypeStruct((B, G, S, n_sel), BF16),
                   jax.ShapeDtypeStruct((B, G, S // hit_tile, S // tk, 128), F32)],
        scratch_shapes=[pltpu.VMEM((n_cmp, NSA_HPG * tq), F32)],
        compiler_params=_params(2, 1),
        name="cmp_select",
    )(slopes, q, kc, vc_t, overlap_t)


def _sel_win_kernel(slopes_ref, active_ref, q_ref, sl_ref, ks_ref, vst_ref, kw_ref, vwt_ref, sel_ref,
                    ocmp_ref, gate_ref, o_ref, ms_ref, accs_ref, mw_ref, accw_ref, offset_ref, todo_ref,
                    knorm_ref, *, tq, tk):
    g = pl.program_id(1)
    t0 = pl.program_id(2) * tq
    cols = NSA_HPG * tq
    slope_row = _slope_row(slopes_ref, g, tq)
    q = q_ref[0]
    q_aug = jnp.concatenate(
        [jnp.concatenate([q[:, r * HEAD_DIM:(r + 1) * HEAD_DIM],
                          jnp.broadcast_to(sl_ref[0, r:r + 1, :], (tq, HEAD_DIM))], axis=1)
         for r in range(NSA_HPG)], axis=0)
    q_sel = jnp.concatenate([q_aug, jnp.concatenate([sel_ref[0, 0]] * NSA_HPG, axis=0)], axis=1)

    @pl.when(pl.program_id(2) == 0)
    def _():
        offset_ref[...] = (lax.broadcasted_iota(jnp.int32, (tk, cols), 0)
                           - lax.broadcasted_iota(jnp.int32, (tk, cols), 1) % tq)

        def body(c, best):
            rows = pl.ds(pl.multiple_of(c * tk, tk), tk)
            ks = ks_ref[0, 0, rows, :][:, :HEAD_DIM].astype(F32)
            kw = kw_ref[0, 0, rows, :][:, :HEAD_DIM].astype(F32)
            return (jnp.maximum(best[0], jnp.max(jnp.sum(ks * ks, axis=1))),
                    jnp.maximum(best[1], jnp.max(jnp.sum(kw * kw, axis=1))))

        best = lax.fori_loop(0, ks_ref.shape[2] // tk, body, (jnp.float32(0.0), jnp.float32(0.0)))
        knorm_ref[0] = best[0]
        knorm_ref[1] = best[1]

    sel_bound = _score_bound(q, knorm_ref[0])
    win_bound = _score_bound(q, knorm_ref[1])
    safe = 2.0 * jnp.maximum(sel_bound, win_bound) <= SAFE_EXP_LOG2

    def step(c, k_ref, q_rows, vt_ref, keep=None):
        inside = c >= 0
        start = pl.multiple_of(jnp.maximum(c, 0) * tk, tk)
        shift = c * tk - t0

        def scores():
            s_t = _dot_nt(k_ref[0, 0, pl.ds(start, tk), :], q_rows)
            if keep is None:
                return s_t
            lo, hi = keep
            offset = offset_ref[...]
            if hi is not None:
                return jnp.where(offset <= jnp.where(inside, hi - shift, -FAR), s_t, NEG_INF)
            return jnp.where(offset > jnp.where(inside, lo - shift, FAR), s_t, NEG_INF)

        kappa = jnp.where(inside, slope_row * shift.astype(F32), NEG_INF)
        return scores, kappa, lambda: vt_ref[0, 0, :, pl.ds(start, tk)]

    last = t0 // tk
    causal, recent, anything = (None, 0), (-WINDOW, None), (-FAR, None)
    first_steps = [step(last - 2, kw_ref, q_aug, vwt_ref, recent) + (1,),
                   step(last, ks_ref, q_sel, vst_ref, causal) + (0,),
                   step(last - 1, kw_ref, q_aug, vwt_ref, anything) + (1,),
                   step(last, kw_ref, q_aug, vwt_ref, causal) + (1,)]

    def note_active(c, n):
        hit = active_ref[0, 0, 0, 0, c] > 0

        @pl.when(hit)
        def _():
            todo_ref[n] = c

        return n + hit.astype(jnp.int32)

    n_todo = lax.fori_loop(0, last, note_active, 0)
    sel_step = lambda i: step(todo_ref[i], ks_ref, q_sel, vst_ref) + (0,)

    accs_ref[...] = jnp.zeros_like(accs_ref)
    accw_ref[...] = jnp.zeros_like(accw_ref)

    @pl.when(safe)
    def _():
        in_tile = slope_row * (lax.broadcasted_iota(jnp.int32, (1, cols), 1) % tq).astype(F32)
        block = lambda steps: _fixed_frame_block(steps, [accs_ref, accw_ref],
                                                 [sel_bound + in_tile, win_bound + in_tile])
        block(first_steps)
        _chunk_loop(n_todo, sel_step, block, (8, 4, 2, 1))

    @pl.when(jnp.logical_not(safe))
    def _():
        states = [(ms_ref, accs_ref), (mw_ref, accw_ref)]
        for m_ref, _ in states:
            m_ref[...] = jnp.full_like(m_ref, NEG_INF)
        block = lambda steps: _softmax_block(steps, states)
        block(first_steps)
        _chunk_loop(n_todo, sel_step, block, (4, 2, 1))

    o_sel = accs_ref[0:HEAD_DIM, :] / accs_ref[HEAD_DIM:HEAD_DIM + 1, :]
    o_win = accw_ref[0:HEAD_DIM, :] / accw_ref[HEAD_DIM:HEAD_DIM + 1, :]

    gates = gate_ref[0]
    outs = []
    for r in range(NSA_HPG):
        cs = slice(r * tq, (r + 1) * tq)
        outs.append(gates[3 * r:3 * r + 1] * ocmp_ref[0, 0, r] + gates[3 * r + 1:3 * r + 2] * o_sel[:, cs]
                    + gates[3 * r + 2:3 * r + 3] * o_win[:, cs])
    o_ref[0] = jnp.concatenate(outs, axis=0).T.astype(o_ref.dtype)


def sel_win_attention(q, active, slope_cols, ks, vs_t, kw, vw_t, sel_bias, o_cmp, gates_t,
                      slopes, tq=256, tk=256):
    B, S, _ = q.shape
    assert tq == tk and WINDOW == 2 * tk
    G = NSA_GROUPS
    n_sel = S // SEL_LEN
    cols = NSA_HPG * tq
    smem = pl.BlockSpec(memory_space=pltpu.SMEM)
    ks_spec = pl.BlockSpec((1, 1, S, ks.shape[-1]), lambda b, g, i: (b, g, 0, 0))
    kw_spec = pl.BlockSpec((1, 1, S, kw.shape[-1]), lambda b, g, i: (b, g, 0, 0))
    v_spec = pl.BlockSpec((1, 1, V_ROWS_64, S), lambda b, g, i: (b, g, 0, 0))
    return pl.pallas_call(
        functools.partial(_sel_win_kernel, tq=tq, tk=tk),
        grid=(B, G, S // tq),
        in_specs=[smem,
                  pl.BlockSpec((1, 1, 1, 1, S // tk), lambda b, g, i: (b, g, i, 0, 0),
                               memory_space=pltpu.SMEM),
                  pl.BlockSpec((1, tq, 256), lambda b, g, i: (b, i, g)),
                  pl.BlockSpec((1, NSA_HPG, HEAD_DIM), lambda b, g, i: (g, 0, 0)),
                  ks_spec, v_spec, kw_spec, v_spec,
                  pl.BlockSpec((1, 1, tq, n_sel), lambda b, g, i: (b, g, i, 0)),
                  pl.BlockSpec((1, 1, NSA_HPG, HEAD_DIM, tq), lambda b, g, i: (b, g, 0, 0, i)),
                  pl.BlockSpec((1, 4 * NSA_HPG, tq), lambda b, g, i: (b, g, i))],
        out_specs=pl.BlockSpec((1, tq, 256), lambda b, g, i: (b, i, g)),
        out_shape=jax.ShapeDtypeStruct((B, S, G * 256), BF16),
        scratch_shapes=[pltpu.VMEM((1, cols), F32), pltpu.VMEM((V_ROWS_64, cols), F32),
                        pltpu.VMEM((1, cols), F32), pltpu.VMEM((V_ROWS_64, cols), F32),
                        pltpu.VMEM((tk, cols), jnp.int32),
                        pltpu.SMEM((S // tk,), jnp.int32), pltpu.SMEM((2,), F32)],
        compiler_params=_params(2, 1),
        name="sel_win_attention",
    )(slopes, active, q, slope_cols, ks, vs_t, kw, vw_t, sel_bias, o_cmp, gates_t)


def _alibi_slopes_log2(n_heads):
    slopes = np.exp2(-8.0 * (np.arange(n_heads, dtype=np.float32) + 1.0) / n_heads)
    return (slopes.astype(np.float32) * np.float32(LOG2E)).astype(np.float32)


def even_layer_mix(x2d, B, S, norm_g, w_in, lam_q1, lam_k1, lam_q2, lam_k2, subln, layer):
    proj, sb_vt, df_vt = even_proj(x2d, B, S, norm_g, w_in.astype(BF16))
    proj = proj.reshape(B, S, -1)
    lam_rows = jnp.stack([lam_q1, lam_k1, lam_q2, lam_k2]).astype(F32)
    mix = diff_attention(proj, df_vt, lam_rows, subln.astype(F32), _alibi_slopes_log2(4), layer,
                         sb_attention(proj, sb_vt))
    return mix.reshape(B * S, -1)


def even_w_out_rows(w_out):
    d = w_out.shape[1]
    return w_out.reshape(2, 4, 128, d).transpose(1, 0, 2, 3).reshape(-1, d)


def odd_layer_mix(x2d, B, S, norm_g, w_in, pos_k, k_w1, k_w2, pos_v, v_w1, v_w2, tq=256, tk=256):
    G, Dh = NSA_GROUPS, HEAD_DIM
    q_width = G * NSA_HPG * Dh
    n_main = q_width + 6 * G * Dh
    per_group = NSA_HPG * N_GATES
    w_gate = w_in[:, n_main:n_main + G * per_group].reshape(-1, G, per_group)
    w_gate = jnp.pad(w_gate, ((0, 0), (0, 0), (0, 4 * NSA_HPG - per_group))).reshape(-1, 4 * NSA_HPG * G)
    w_gate = jnp.pad(w_gate, ((0, 0), (0, GATE_PAD - w_gate.shape[1])))
    w_all = jnp.concatenate([w_in[:, :n_main], w_gate], axis=1).astype(BF16)

    n_chunks = S // CMP_STRIDE
    n_sel = S // SEL_LEN
    onehot = (jnp.arange(S)[:, None] // SEL_LEN == jnp.arange(n_sel)[None, :]).astype(BF16)
    q, cmp_in, ks, kw, vs_t, vw_t, gates_t = odd_proj(x2d, B, S, norm_g, w_all,
                                                      _key_position_columns(S, tk, Dh), onehot)
    q = q.reshape(B, S, q_width)

    pos_flat = jnp.stack([pos_k, pos_v]).reshape(2, 1, CMP_LEN * Dh).astype(F32)
    w1 = jnp.stack([k_w1, v_w1]).astype(BF16)
    w2 = jnp.stack([k_w2, v_w2]).astype(BF16)
    cmp = compress_kv(cmp_in, pos_flat, w1, w2).reshape(2, B, G, n_chunks, Dh)

    cmp_start = jnp.arange(n_chunks) * CMP_STRIDE
    sel_start = jnp.arange(n_sel) * SEL_LEN
    overlap_t = ((cmp_start[None, :] < sel_start[:, None] + SEL_LEN)
                 & (sel_start[:, None] <= cmp_start[None, :] + CMP_LEN - 1)).astype(BF16)
    slopes = _alibi_slopes_log2(G * NSA_HPG)
    slope_cols = _slope_pieces(slopes, Dh).reshape(G, NSA_HPG, Dh)

    o_cmp, sel_bias, hits = cmp_select(q, cmp[0], cmp[1].transpose(0, 1, 3, 2), overlap_t, slopes, tk, tq)
    active = (hits[..., 0] > 0).astype(jnp.int32)[:, :, :, None, :]
    o = sel_win_attention(q, active, slope_cols, ks, vs_t, kw, vw_t, sel_bias, o_cmp, gates_t, slopes,
                          tq=tq, tk=tk)
    return o.reshape(B * S, q_width)


def kernel(x, attn_norm, mlp_norm, final_norm, ev_w_in, ev_lam_q1, ev_lam_k1, ev_lam_q2, ev_lam_k2,
           ev_subln, ev_w_out, od_w_in, od_cmp_pos_k, od_cmp_k_w1, od_cmp_k_w2, od_cmp_pos_v,
           od_cmp_v_w1, od_cmp_v_w2, od_w_out, mlp_w1, mlp_w2):
    B, S, D = x.shape
    depth = attn_norm.shape[0]
    x2d = x.reshape(B * S, D)
    for layer in range(depth):
        idx = layer // 2
        if layer % 2 == 0:
            mix = even_layer_mix(x2d, B, S, attn_norm[layer], ev_w_in[idx], ev_lam_q1[idx],
                                 ev_lam_k1[idx], ev_lam_q2[idx], ev_lam_k2[idx], ev_subln[idx], layer)
            w_out = even_w_out_rows(ev_w_out[idx])
        else:
            mix = odd_layer_mix(x2d, B, S, attn_norm[layer], od_w_in[idx], od_cmp_pos_k[idx],
                                od_cmp_k_w1[idx], od_cmp_k_w2[idx], od_cmp_pos_v[idx],
                                od_cmp_v_w1[idx], od_cmp_v_w2[idx])
            w_out = od_w_out[idx]
        g_final = final_norm if layer == depth - 1 else None
        x2d = post_block(mix, x2d, w_out.astype(BF16), mlp_norm[layer], mlp_w1[layer].astype(BF16),
                         mlp_w2[layer].astype(BF16), g_final)
    return x2d.reshape(B, S, D)
```

```python
import functools
import math

import jax
import jax.numpy as jnp
import numpy as np
from jax import lax
from jax.experimental import pallas as pl
from jax.experimental.pallas import tpu as pltpu

F32 = jnp.float32
BF16 = jnp.bfloat16

HEAD_DIM = 64
RMS_EPS = 1e-6
NEG_INF = -1e30
FORCE_SCORE = 1e6
NSA_GROUPS = 4
NSA_HPG = 4
CMP_LEN = 32
CMP_STRIDE = 16
SEL_LEN = 64
SEL_TOPK = 16
WINDOW = 512
N_GATES = 3
GATE_PAD = 128
SEL_DROP = 2.0 ** 24
BF16_EXACT_INT = 256
V_ROWS_64 = 80
V_ROWS_128 = 144

LOG2E = math.log2(math.e)
Q_SCALE = HEAD_DIM ** -0.5 * LOG2E
UNDERFLOW_LOG2 = -160.0
SAFE_EXP_LOG2 = 100.0
FAR = 1 << 30
PICKED = -2.0

VMEM_LIMIT = 56 * 1024 * 1024


def _params(n_parallel, n_arbitrary=0):
    return pltpu.CompilerParams(dimension_semantics=("parallel",) * n_parallel + ("arbitrary",) * n_arbitrary,
                                vmem_limit_bytes=VMEM_LIMIT)


def _rms(x, g):
    ms = jnp.mean(x * x, axis=-1, keepdims=True)
    return x * lax.rsqrt(ms + RMS_EPS) * g


def _dot(a, b):
    return jnp.dot(a, b, preferred_element_type=F32)


def _dot_nt(a, b):
    return lax.dot_general(a, b, (((1,), (1,)), ((), ())), preferred_element_type=F32)


def _split_bf16(x):
    hi = x.astype(BF16)
    lo = (x - hi.astype(F32)).astype(BF16)
    return hi, lo


def _slope_pieces(slopes, width):
    def top_bits(x):
        return (x.view(np.uint32) & np.uint32(0xFFFF0000)).view(np.float32)

    s1 = top_bits(slopes)
    r1 = slopes - s1
    s2 = top_bits(r1)
    s3 = top_bits(r1 - s2)
    out = np.zeros((slopes.shape[0], width), np.float32)
    out[:, :6] = np.stack([s1, s2, s3, s1, s2, s3], axis=1)
    return jnp.asarray(out).astype(BF16)


def _key_position_columns(n, tk, width):
    j = jnp.arange(n) % tk
    a = (j // BF16_EXACT_INT) * BF16_EXACT_INT
    b = j % BF16_EXACT_INT
    cols = jnp.stack([a, a, a, b, b, b], axis=1).astype(BF16)
    return jnp.pad(cols, ((0, 0), (0, width - cols.shape[1])))


def _softmax_block(steps, states):
    s_all = [scores() for scores, _, _, _ in steps]
    live = {k: (states[k][0][...], states[k][1][...]) for k in sorted({k for _, _, _, k in steps})}
    for (_, kap, values, k), s_t in zip(steps, s_all):
        m_run, acc = live[k]
        m_new = jnp.maximum(m_run, jnp.max(s_t, axis=0, keepdims=True) + kap)
        p = jnp.exp2(s_t - (m_new - kap)).astype(BF16)
        live[k] = (m_new, jnp.exp2(m_run - m_new) * acc + _dot(values(), p))
    for k, (m_run, acc) in live.items():
        states[k][0][...] = m_run
        states[k][1][...] = acc


def _fixed_frame_block(steps, acc_refs, frames):
    s_all = [scores() for scores, _, _, _ in steps]
    live = {k: acc_refs[k][...] for k in sorted({k for _, _, _, k in steps})}
    for (_, kap, values, k), s_t in zip(steps, s_all):
        live[k] = live[k] + _dot(values(), jnp.exp2(s_t - (frames[k] - kap)).astype(BF16))
    for k, acc in live.items():
        acc_refs[k][...] = acc


def _chunk_loop(n_steps, step, block, groups):
    done = 0
    for group in groups:
        def body(i, carry, group=group, done=done):
            block([step(done + group * i + j) for j in range(group)])
            return carry

        n_groups = (n_steps - done) // group
        lax.fori_loop(0, n_groups, body, 0)
        done = done + group * n_groups


def _score_bound(q, k_norm2):
    q32 = q.astype(F32)
    width = q.shape[1]
    member = (lax.broadcasted_iota(jnp.int32, (width, 128), 0) // HEAD_DIM
              == lax.broadcasted_iota(jnp.int32, (width, 128), 1))
    head_norm2 = _dot((q32 * q32).astype(BF16), jnp.where(member, 1.0, 0.0).astype(BF16))
    return jnp.sqrt(jnp.max(head_norm2) * k_norm2 * 1.05)


PROJ_CHUNK = 512


def _ones_row_tail(rows, width):
    return jnp.where(lax.broadcasted_iota(jnp.int32, (rows, width), 0) == 0, 1.0, 0.0).astype(BF16)


def _even_proj_kernel(x_ref, g_ref, w_ref, o_ref, sbv_ref, dfv_ref):
    tm = x_ref.shape[0]
    xn = _rms(x_ref[...], g_ref[...]).astype(BF16)
    tail = _ones_row_tail(V_ROWS_128 - 2 * HEAD_DIM, tm)
    for c in range(w_ref.shape[1] // PROJ_CHUNK):
        sl = slice(c * PROJ_CHUNK, (c + 1) * PROJ_CHUNK)
        res = _dot(xn, w_ref[:, sl])
        if c in (0, 3):
            res = res * Q_SCALE
        o_ref[:, sl] = res.astype(BF16)
        if c == 2:
            sbv_ref[0] = res.T.astype(BF16)
        if c == 5:
            v_t = res.T.astype(BF16)
            for h in range(dfv_ref.shape[1]):
                dfv_ref[0, h, 0:2 * HEAD_DIM, :] = v_t[h * 2 * HEAD_DIM:(h + 1) * 2 * HEAD_DIM]
                dfv_ref[0, h, 2 * HEAD_DIM:V_ROWS_128, :] = tail


def even_proj(x2d, B, S, g, w_bf16, tm=512):
    T, D = x2d.shape
    n_w = w_bf16.shape[1]
    per_seq = S // tm
    assert n_w == 6 * PROJ_CHUNK
    return pl.pallas_call(
        _even_proj_kernel,
        grid=(T // tm,),
        in_specs=[pl.BlockSpec((tm, D), lambda i: (i, 0)),
                  pl.BlockSpec((1, D), lambda i: (0, 0)),
                  pl.BlockSpec((D, n_w), lambda i: (0, 0))],
        out_specs=[pl.BlockSpec((tm, n_w), lambda i: (i, 0)),
                   pl.BlockSpec((1, PROJ_CHUNK, tm), lambda i: (i // per_seq, 0, i % per_seq)),
                   pl.BlockSpec((1, 4, V_ROWS_128, tm), lambda i: (i // per_seq, 0, 0, i % per_seq))],
        out_shape=[jax.ShapeDtypeStruct((T, n_w), BF16),
                   jax.ShapeDtypeStruct((B, PROJ_CHUNK, S), BF16),
                   jax.ShapeDtypeStruct((B, 4, V_ROWS_128, S), BF16)],
        compiler_params=_params(1),
        name="even_proj",
    )(x2d, g.reshape(1, D), w_bf16)


def _odd_proj_kernel(x_ref, g_ref, w_ref, pos_ref, onehot_ref,
                     q_ref, cin_ref, ks_ref, kw_ref, vst_ref, vwt_ref, gate_ref):
    tm = x_ref.shape[0]
    G, Dh = NSA_GROUPS, HEAD_DIM
    kv = G * Dh
    xn = _rms(x_ref[...], g_ref[...]).astype(BF16)
    tail = _ones_row_tail(V_ROWS_64 - Dh, tm)
    pos = pos_ref[...]
    onehot = onehot_ref[...]
    q_width = q_ref.shape[1]
    for c in range(q_width // PROJ_CHUNK):
        sl = slice(c * PROJ_CHUNK, (c + 1) * PROJ_CHUNK)
        q_ref[:, sl] = (_dot(xn, w_ref[:, sl]) * Q_SCALE).astype(BF16)

    def pair(j):
        return _dot(xn, w_ref[:, q_width + j * 2 * kv:q_width + (j + 1) * 2 * kv])

    res = pair(0)
    for which in range(2):
        for g in range(G):
            cin_ref[which, 0, g] = res[:, which * kv + g * Dh:which * kv + (g + 1) * Dh]

    for j, k_ref, vt_ref, extra in ((1, ks_ref, vst_ref, [pos, onehot]), (2, kw_ref, vwt_ref, [pos])):
        res = pair(j)
        v_t = res[:, kv:2 * kv].T.astype(BF16)
        for g in range(G):
            k_ref[0, g] = jnp.concatenate([res[:, g * Dh:(g + 1) * Dh].astype(BF16)] + extra, axis=1)
            vt_ref[0, g, 0:Dh, :] = v_t[g * Dh:(g + 1) * Dh]
            vt_ref[0, g, Dh:V_ROWS_64, :] = tail

    logits = _dot(xn, w_ref[:, q_width + 6 * kv:q_width + 6 * kv + GATE_PAD])
    gate_ref[0] = jax.nn.sigmoid(logits).T[0:gate_ref.shape[1]]


def odd_proj(x2d, B, S, g, w_bf16, pos_cols, onehot, tm=512):
    T, D = x2d.shape
    G, Dh = NSA_GROUPS, HEAD_DIM
    q_width = G * NSA_HPG * Dh
    n_sel = onehot.shape[1]
    per_seq = S // tm
    seq_tile = lambda i: (i // per_seq, 0, i % per_seq, 0)
    seq_tile_t = lambda i: (i // per_seq, 0, 0, i % per_seq)
    return pl.pallas_call(
        _odd_proj_kernel,
        grid=(T // tm,),
        in_specs=[pl.BlockSpec((tm, D), lambda i: (i, 0)),
                  pl.BlockSpec((1, D), lambda i: (0, 0)),
                  pl.BlockSpec(w_bf16.shape, lambda i: (0, 0)),
                  pl.BlockSpec((tm, Dh), lambda i: (i % per_seq, 0)),
                  pl.BlockSpec((tm, n_sel), lambda i: (i % per_seq, 0))],
        out_specs=[pl.BlockSpec((tm, q_width), lambda i: (i, 0)),
                   pl.BlockSpec((2, 1, G, tm, Dh), lambda i: (0, i // per_seq, 0, i % per_seq, 0)),
                   pl.BlockSpec((1, G, tm, 2 * Dh + n_sel), seq_tile),
                   pl.BlockSpec((1, G, tm, 2 * Dh), seq_tile),
                   pl.BlockSpec((1, G, V_ROWS_64, tm), seq_tile_t),
                   pl.BlockSpec((1, G, V_ROWS_64, tm), seq_tile_t),
                   pl.BlockSpec((1, 4 * NSA_HPG * G, tm), lambda i: (i // per_seq, 0, i % per_seq))],
        out_shape=[jax.ShapeDtypeStruct((T, q_width), BF16),
                   jax.ShapeDtypeStruct((2, B, G, S, Dh), F32),
                   jax.ShapeDtypeStruct((B, G, S, 2 * Dh + n_sel), BF16),
                   jax.ShapeDtypeStruct((B, G, S, 2 * Dh), BF16),
                   jax.ShapeDtypeStruct((B, G, V_ROWS_64, S), BF16),
                   jax.ShapeDtypeStruct((B, G, V_ROWS_64, S), BF16),
                   jax.ShapeDtypeStruct((B, 4 * NSA_HPG * G, S), F32)],
        compiler_params=_params(1),
        name="odd_proj",
    )(x2d, g.reshape(1, D), w_bf16, pos_cols, onehot)


def _head_pair_rows(q, t):
    lane = lax.broadcasted_iota(jnp.int32, (t, 2 * HEAD_DIM), 1)
    zero = jnp.zeros_like(q)
    return jnp.where(lane < HEAD_DIM, q, zero), jnp.where(lane >= HEAD_DIM, q, zero)


def _sb_kernel(q_ref, k_ref, vt_ref, o_ref, acc_ref, carry_ref, *, t):
    qi = pl.program_id(2)
    cols = 2 * t
    q_both = jnp.concatenate(_head_pair_rows(q_ref[0], t), axis=0)
    s_idx = lax.broadcasted_iota(jnp.int32, (t, 2 * t), 0)
    j_idx = lax.broadcasted_iota(jnp.int32, (t, 2 * t), 1) % t
    upper2 = jnp.where(j_idx > s_idx, 1.0, 0.0).astype(BF16)
    key = lax.broadcasted_iota(jnp.int32, (t, cols), 0)
    qry = lax.broadcasted_iota(jnp.int32, (t, cols), 1) % t
    past_diag = key < qry

    acc_ref[...] = jnp.zeros_like(acc_ref)
    carry_ref[...] = jnp.zeros_like(carry_ref)

    def chunks(specs):
        starts = [pl.multiple_of(kc * t, t) for kc, _ in specs]
        z_all = [_dot_nt(k_ref[0, pl.ds(start, t), :], q_both) for start in starts]
        carry = carry_ref[...]
        acc = acc_ref[...]
        for (_, masked), start, z in zip(specs, starts, z_all):
            drop = jnp.maximum(z, 0.0) + jnp.log2(1.0 + jnp.exp2(jnp.abs(z) * -1.0))
            log_beta = z - drop
            if masked:
                drop = jnp.where(past_diag, drop, 0.0)
            hi, lo = _split_bf16(drop)
            tail = _dot(upper2, jnp.concatenate([hi, lo], axis=0))
            w = jnp.exp2(log_beta - tail - carry)
            if masked:
                w = jnp.where(past_diag, w, 0.0)
            carry = carry + tail[0:1] + drop[0:1]
            acc = acc + _dot(vt_ref[0, :, pl.ds(start, t)], w.astype(BF16))
        carry_ref[...] = carry
        acc_ref[...] = acc
        return jnp.min(carry)

    least = lax.cond(qi > 0, lambda: chunks([(qi, True), (qi - 1, False)]), lambda: chunks([(qi, True)]))

    def cond(state):
        j, least_carry = state
        return (j < qi) & (least_carry < -UNDERFLOW_LOG2)

    def body(state):
        j, _ = state
        return j + 1, chunks([(qi - 1 - j, False)])

    lax.while_loop(cond, body, (1, least))
    row = lax.broadcasted_iota(jnp.int32, (2 * HEAD_DIM, t), 0)
    o_t = jnp.where(row < HEAD_DIM, acc_ref[:, :t], acc_ref[:, t:])
    o_ref[0] = o_t.T.astype(o_ref.dtype)


def sb_attention(proj, v_t, t=256):
    B, S, _ = proj.shape
    n_pairs = 4
    return pl.pallas_call(
        functools.partial(_sb_kernel, t=t),
        grid=(B, n_pairs, S // t),
        in_specs=[pl.BlockSpec((1, t, 128), lambda b, p, i: (b, i, p)),
                  pl.BlockSpec((1, S, 128), lambda b, p, i: (b, 0, n_pairs + p)),
                  pl.BlockSpec((1, 128, S), lambda b, p, i: (b, p, 0))],
        out_specs=pl.BlockSpec((1, t, 128), lambda b, p, i: (b, i, p)),
        out_shape=jax.ShapeDtypeStruct((B, S, n_pairs * 128), BF16),
        scratch_shapes=[pltpu.VMEM((128, 2 * t), F32), pltpu.VMEM((1, 2 * t), F32)],
        compiler_params=_params(3),
        name="sb_attention",
    )(proj, proj, v_t)


def _diff_kernel(slopes_ref, lam_ref, q_ref, k_ref, vt_ref, pos_ref, sl_ref, subln_ref, sb_ref, o_ref,
                 m_ref, acc_ref, offset_ref, knorm_ref, *, t, out_scale, lambda_init):
    h = pl.program_id(1)
    qi = pl.program_id(2)
    slope = slopes_ref[h]
    cols = 2 * t
    slope_cols = jnp.broadcast_to(sl_ref[0], (t, 128))
    q_parts = _head_pair_rows(q_ref[0], t)
    q_both = jnp.concatenate([jnp.concatenate([qc, slope_cols], axis=1) for qc in q_parts],
                             axis=0)
    lane = lax.broadcasted_iota(jnp.int32, (t, 2 * HEAD_DIM), 1)

    @pl.when(qi == 0)
    def _():
        def body(c, best):
            k = k_ref[0, pl.ds(pl.multiple_of(c * t, t), t), :].astype(F32)
            k2 = k * k
            return (jnp.maximum(best[0], jnp.max(jnp.sum(jnp.where(lane < HEAD_DIM, k2, 0.0), axis=1))),
                    jnp.maximum(best[1], jnp.max(jnp.sum(jnp.where(lane >= HEAD_DIM, k2, 0.0), axis=1))))

        best = lax.fori_loop(0, k_ref.shape[1] // t, body, (jnp.float32(0.0), jnp.float32(0.0)))
        knorm_ref[0] = best[0]
        knorm_ref[1] = best[1]
        offset_ref[...] = (lax.broadcasted_iota(jnp.int32, (t, cols), 0)
                           - lax.broadcasted_iota(jnp.int32, (t, cols), 1) % t)

    qk_bound = _score_bound(q_ref[0], jnp.maximum(knorm_ref[0], knorm_ref[1]))
    safe = 2.0 * qk_bound <= SAFE_EXP_LOG2
    col = lax.broadcasted_iota(jnp.int32, (1, cols), 1)

    pos = pos_ref[...]

    def step(kc, diagonal=False):
        start = pl.multiple_of(kc * t, t)

        def scores():
            s_t = _dot_nt(jnp.concatenate([k_ref[0, pl.ds(start, t), :], pos], axis=1), q_both)
            return jnp.where(offset_ref[...] <= 0, s_t, NEG_INF) if diagonal else s_t

        return scores, slope * ((kc - qi) * t).astype(F32), lambda: vt_ref[0, 0, :, pl.ds(start, t)], 0

    acc_ref[...] = jnp.zeros_like(acc_ref)

    @pl.when(safe)
    def _():
        frame = qk_bound + slope * (col % t).astype(F32)
        block = lambda steps: _fixed_frame_block(steps, [acc_ref], [frame])
        block([step(qi, True)])
        n_back = jnp.ceil((slope * (t - 1) - UNDERFLOW_LOG2) / (slope * t)).astype(jnp.int32)
        _chunk_loop(jnp.minimum(n_back, qi), lambda i: step(qi - 1 - i), block, (8, 4, 2, 1))

    @pl.when(jnp.logical_not(safe))
    def _():
        m_ref[...] = jnp.full_like(m_ref, NEG_INF)
        block = lambda steps: _softmax_block(steps, [(m_ref, acc_ref)])
        block([step(qi, True)])
        excess = jnp.max(qk_bound - m_ref[...]) + slope * (t - 1)
        n_back = jnp.clip(jnp.ceil((excess - UNDERFLOW_LOG2) / (slope * t)).astype(jnp.int32), 0, qi)
        _chunk_loop(n_back, lambda i: step(qi - 1 - i), block, (4, 2, 1))

    lam_terms = lam_ref[...]
    lam = (jnp.exp(jnp.sum(lam_terms[0:1] * lam_terms[1:2], axis=1, keepdims=True))
           - jnp.exp(jnp.sum(lam_terms[2:3] * lam_terms[3:4], axis=1, keepdims=True))
           + lambda_init)
    o_t = acc_ref[0:128, :] / acc_ref[128:129, :]
    o = (o_t[:, :t] - lam * o_t[:, t:]).T
    o_ref[0, :, 0:128] = sb_ref[0]
    o_ref[0, :, 128:256] = (_rms(o, subln_ref[...]) * out_scale).astype(o_ref.dtype)


def diff_attention(proj, vt_aug, lam_rows, subln, slopes, layer, o_sb, t=512):
    B, S, _ = proj.shape
    n_heads = 4
    lambda_init = 0.8 - 0.6 * math.exp(-0.3 * layer)
    smem = pl.BlockSpec(memory_space=pltpu.SMEM)
    pos = _key_position_columns(t, t, 128)
    slope_cols = _slope_pieces(slopes, 128).reshape(n_heads, 1, 128)
    return pl.pallas_call(
        functools.partial(_diff_kernel, t=t, out_scale=1.0 - lambda_init, lambda_init=lambda_init),
        grid=(B, n_heads, S // t),
        in_specs=[smem,
                  pl.BlockSpec((4, HEAD_DIM), lambda b, h, i: (0, 0)),
                  pl.BlockSpec((1, t, 128), lambda b, h, i: (b, i, 12 + h)),
                  pl.BlockSpec((1, S, 128), lambda b, h, i: (b, 0, 16 + h)),
                  pl.BlockSpec((1, 1, V_ROWS_128, S), lambda b, h, i: (b, h, 0, 0)),
                  pl.BlockSpec((t, 128), lambda b, h, i: (0, 0)),
                  pl.BlockSpec((1, 1, 128), lambda b, h, i: (h, 0, 0)),
                  pl.BlockSpec((1, 128), lambda b, h, i: (0, 0)),
                  pl.BlockSpec((1, t, 128), lambda b, h, i: (b, i, h))],
        out_specs=pl.BlockSpec((1, t, 256), lambda b, h, i: (b, i, h)),
        out_shape=jax.ShapeDtypeStruct((B, S, n_heads * 256), BF16),
        scratch_shapes=[pltpu.VMEM((1, 2 * t), F32), pltpu.VMEM((V_ROWS_128, 2 * t), F32),
                        pltpu.VMEM((t, 2 * t), jnp.int32), pltpu.SMEM((2,), F32)],
        compiler_params=_params(2, 1),
        name="diff_attention",
    )(slopes, lam_rows, proj, proj, vt_aug, pos, slope_cols, subln.reshape(1, 128), o_sb)


def _post_kernel(*refs, ff_chunk, final):
    mix_ref, x_ref, wo_ref, g_ref, w1_ref, w2_ref = refs[:6]
    gf_ref = refs[6] if final else None
    o_ref = refs[-1]
    x = x_ref[...] + _dot(mix_ref[...], wo_ref[...])
    hn = _rms(x, g_ref[...]).astype(BF16)
    acc = x
    for f in range(w1_ref.shape[1] // ff_chunk):
        sl = slice(f * ff_chunk, (f + 1) * ff_chunk)
        hid = jnp.maximum(_dot(hn, w1_ref[:, sl]), 0.0)
        acc = acc + _dot((hid * hid).astype(BF16), w2_ref[sl, :])
    if final:
        acc = _rms(acc, gf_ref[...])
    o_ref[...] = acc


def post_block(mix, x2d, w_out, g_mlp, w1, w2, g_final=None, tm=512, ff_chunk=1024):
    T, D = x2d.shape
    final = g_final is not None
    const = lambda i: (0, 0)
    in_specs = [pl.BlockSpec((tm, mix.shape[1]), lambda i: (i, 0)),
                pl.BlockSpec((tm, D), lambda i: (i, 0)),
                pl.BlockSpec(w_out.shape, const),
                pl.BlockSpec((1, D), const), pl.BlockSpec(w1.shape, const), pl.BlockSpec(w2.shape, const)]
    args = [mix, x2d, w_out, g_mlp.reshape(1, D), w1, w2]
    if final:
        in_specs.append(pl.BlockSpec((1, D), const))
        args.append(g_final.reshape(1, D))
    return pl.pallas_call(
        functools.partial(_post_kernel, ff_chunk=ff_chunk, final=final),
        grid=(T // tm,),
        in_specs=in_specs,
        out_specs=pl.BlockSpec((tm, D), lambda i: (i, 0)),
        out_shape=jax.ShapeDtypeStruct((T, D), F32),
        compiler_params=_params(1),
        name="post_block",
    )(*args)


def _compress_kernel(c_ref, pos_ref, w1_ref, w2_ref, o_ref):
    half = w1_ref.shape[1] // 2
    n_chunks = c_ref.shape[3] // CMP_STRIDE
    first = second = None
    for l in range(CMP_STRIDE):
        tok = c_ref[0, 0, 0, pl.ds(l, n_chunks, stride=CMP_STRIDE), :].astype(BF16)
        a = _dot(tok, w1_ref[0, l * HEAD_DIM:(l + 1) * HEAD_DIM, :])
        b = _dot(tok, w1_ref[0, half + l * HEAD_DIM:half + (l + 1) * HEAD_DIM, :])
        first = a if first is None else first + a
        second = b if second is None else second + b
    pos = jnp.broadcast_to(pos_ref[0], (8, 2 * half)).astype(BF16)
    pre = first + pltpu.roll(second, n_chunks - 1, 0) + _dot(pos, w1_ref[0])[0:1]
    hid = jax.nn.gelu(pre)
    o_ref[0, 0] = _dot(hid.astype(BF16), w2_ref[0]).astype(o_ref.dtype)


def compress_kv(tokens, pos_flat, w1, w2):
    _, B, G, S, Dh = tokens.shape
    n_chunks = S // CMP_STRIDE
    hidden = w1.shape[-1]
    return pl.pallas_call(
        _compress_kernel,
        grid=(2, B * G),
        in_specs=[pl.BlockSpec((1, 1, 1, S, Dh), lambda s, i: (s, i // G, i % G, 0, 0)),
                  pl.BlockSpec((1, 1, CMP_LEN * Dh), lambda s, i: (s, 0, 0)),
                  pl.BlockSpec((1, CMP_LEN * Dh, hidden), lambda s, i: (s, 0, 0)),
                  pl.BlockSpec((1, hidden, Dh), lambda s, i: (s, 0, 0))],
        out_specs=pl.BlockSpec((1, 1, n_chunks, Dh), lambda s, i: (s, i, 0, 0)),
        out_shape=jax.ShapeDtypeStruct((2, B * G, n_chunks, Dh), BF16),
        compiler_params=_params(2),
        name="compress_kv",
    )(tokens, pos_flat, w1, w2)


def _stack_heads(q):
    return jnp.concatenate([q[:, r * HEAD_DIM:(r + 1) * HEAD_DIM] for r in range(NSA_HPG)], axis=0)


def _slope_row(slopes_ref, g, tq):
    col = lax.broadcasted_iota(jnp.int32, (1, NSA_HPG * tq), 1)
    out = jnp.zeros((1, NSA_HPG * tq), F32)
    for r in range(NSA_HPG):
        out = jnp.where(col // tq == r, slopes_ref[g * NSA_HPG + r], out)
    return out


def _cmp_select_kernel(slopes_ref, q_ref, kc_ref, vct_ref, ovt_ref, ocmp_ref, sel_ref, hits_ref,
                       imp_ref, *, tq, n_sel, blocks_per_chunk, hit_tile):
    g = pl.program_id(1)
    t0 = pl.program_id(2) * tq
    cols = NSA_HPG * tq
    n_cmp = kc_ref.shape[2]
    slope_row = _slope_row(slopes_ref, g, tq)
    q_rows = _stack_heads(q_ref[0])

    def group_sum(pc):
        out = pc[:, 0:tq]
        for r in range(1, NSA_HPG):
            out = out + pc[:, r * tq:(r + 1) * tq]
        return out

    def compressed_branch(n_rows):
        tpos = t0 + lax.broadcasted_iota(jnp.int32, (n_rows, cols), 1) % tq
        cmp_end = lax.broadcasted_iota(jnp.int32, (n_rows, cols), 0) * CMP_STRIDE + (CMP_LEN - 1)
        dc = (tpos - cmp_end).astype(F32)
        sc = jnp.where(dc >= 0, _dot_nt(kc_ref[0, 0, 0:n_rows, :], q_rows) - slope_row * dc, NEG_INF)
        e = jnp.exp2(sc - jnp.max(sc, axis=0, keepdims=True))
        any_valid = jnp.where(dc[0:1] >= 0, 1.0, 0.0)
        pc = e * (any_valid / jnp.sum(e, axis=0, keepdims=True))
        o_t = _dot(vct_ref[0, 0, :, 0:n_rows], pc.astype(BF16))
        for r in range(NSA_HPG):
            ocmp_ref[0, 0, r] = o_t[:, r * tq:(r + 1) * tq]
        hi, lo = _split_bf16(group_sum(pc))
        imp_ref[...] = _dot(ovt_ref[:, 0:n_rows], hi) + _dot(ovt_ref[:, 0:n_rows], lo)

    half_enough = (t0 + tq - CMP_LEN) // CMP_STRIDE < n_cmp // 2

    @pl.when(half_enough)
    def _():
        compressed_branch(n_cmp // 2)

    @pl.when(jnp.logical_not(half_enough))
    def _():
        compressed_branch(n_cmp)

    imp = imp_ref[...]
    blk = lax.broadcasted_iota(jnp.int32, (n_sel, tq), 0)
    cur = (t0 + lax.broadcasted_iota(jnp.int32, (n_sel, tq), 1)) // SEL_LEN
    forced = (blk == 0) | (blk == cur) | (blk == cur - 1)
    imp = jnp.where(blk <= cur, imp, -1.0)
    topk = min(SEL_TOPK, n_sel)

    def pick(imp, count):
        for _ in range(count):
            best = jnp.max(imp, axis=0, keepdims=True)
            first = jnp.min(jnp.where(imp == best, blk, n_sel), axis=0, keepdims=True)
            imp = jnp.where(blk == first, PICKED, imp)
        return imp

    imp = lax.cond(t0 >= 2 * SEL_LEN,
                   lambda: pick(jnp.where(forced, PICKED, imp), topk - 3),
                   lambda: pick(jnp.where(forced, FORCE_SCORE, imp), topk))
    keep = (imp == PICKED) & (blk <= cur)
    sel_ref[0, 0] = jnp.where(keep, 0.0, -SEL_DROP).T.astype(sel_ref.dtype)
    n_chunks = n_sel // blocks_per_chunk
    member = (lax.broadcasted_iota(jnp.int32, (n_chunks, n_sel), 1) // blocks_per_chunk
              == lax.broadcasted_iota(jnp.int32, (n_chunks, n_sel), 0))
    per_query = _dot(jnp.where(member, 1.0, 0.0).astype(BF16),
                     jnp.where(keep, 1.0, 0.0).astype(BF16)).astype(BF16)
    for part in range(tq // hit_tile):
        hits_ref[0, 0, part] = _dot(per_query[:, part * hit_tile:(part + 1) * hit_tile],
                                    jnp.ones((hit_tile, 128), BF16))


def cmp_select(q, kc, vc_t, overlap_t, slopes, tk, hit_tile, tq=256):
    B, S, _ = q.shape
    n_cmp = kc.shape[2]
    n_sel = S // SEL_LEN
    G = NSA_GROUPS
    smem = pl.BlockSpec(memory_space=pltpu.SMEM)
    return pl.pallas_call(
        functools.partial(_cmp_select_kernel, tq=tq, n_sel=n_sel, blocks_per_chunk=tk // SEL_LEN,
                          hit_tile=hit_tile),
        grid=(B, G, S // tq),
        in_specs=[smem,
                  pl.BlockSpec((1, tq, 256), lambda b, g, i: (b, i, g)),
                  pl.BlockSpec((1, 1, n_cmp, HEAD_DIM), lambda b, g, i: (b, g, 0, 0)),
                  pl.BlockSpec((1, 1, HEAD_DIM, n_cmp), lambda b, g, i: (b, g, 0, 0)),
                  pl.BlockSpec((n_sel, n_cmp), lambda b, g, i: (0, 0))],
        out_specs=[pl.BlockSpec((1, 1, NSA_HPG, HEAD_DIM, tq), lambda b, g, i: (b, g, 0, 0, i)),
                   pl.BlockSpec((1, 1, tq, n_sel), lambda b, g, i: (b, g, i, 0)),
                   pl.BlockSpec((1, 1, tq // hit_tile, S // tk, 128), lambda b, g, i: (b, g, i, 0, 0))],
        out_shape=[jax.ShapeDtypeStruct((B, G, NSA_HPG, HEAD_DIM, S), F32),
                   jax.ShapeDtypeStruct((B, G, S, n_sel), BF16),
                   jax.ShapeDtypeStruct((B, G, S // hit_tile, S // tk, 128), F32)],
        scratch_shapes=[pltpu.VMEM((n_sel, tq), F32)],
        compiler_params=_params(3),
        name="cmp_select",
    )(slopes, q, kc, vc_t, overlap_t)


def _sel_win_kernel(slopes_ref, active_ref, q_ref, sl_ref, ks_ref, vst_ref, kw_ref, vwt_ref, sel_ref,
                    ocmp_ref, gate_ref, o_ref, ms_ref, accs_ref, mw_ref, accw_ref, offset_ref, todo_ref,
                    knorm_ref, *, tq, tk):
    g = pl.program_id(1)
    t0 = pl.program_id(2) * tq
    cols = NSA_HPG * tq
    slope_row = _slope_row(slopes_ref, g, tq)
    q = q_ref[0]
    q_aug = jnp.concatenate(
        [jnp.concatenate([q[:, r * HEAD_DIM:(r + 1) * HEAD_DIM],
                          jnp.broadcast_to(sl_ref[0, r:r + 1, :], (tq, HEAD_DIM))], axis=1)
         for r in range(NSA_HPG)], axis=0)
    q_sel = jnp.concatenate([q_aug, jnp.concatenate([sel_ref[0, 0]] * NSA_HPG, axis=0)], axis=1)

    @pl.when(pl.program_id(2) == 0)
    def _():
        offset_ref[...] = (lax.broadcasted_iota(jnp.int32, (tk, cols), 0)
                           - lax.broadcasted_iota(jnp.int32, (tk, cols), 1) % tq)

        def body(c, best):
            rows = pl.ds(pl.multiple_of(c * tk, tk), tk)
            ks = ks_ref[0, 0, rows, :][:, :HEAD_DIM].astype(F32)
            kw = kw_ref[0, 0, rows, :][:, :HEAD_DIM].astype(F32)
            return (jnp.maximum(best[0], jnp.max(jnp.sum(ks * ks, axis=1))),
                    jnp.maximum(best[1], jnp.max(jnp.sum(kw * kw, axis=1))))

        best = lax.fori_loop(0, ks_ref.shape[2] // tk, body, (jnp.float32(0.0), jnp.float32(0.0)))
        knorm_ref[0] = best[0]
        knorm_ref[1] = best[1]

    sel_bound = _score_bound(q, knorm_ref[0])
    win_bound = _score_bound(q, knorm_ref[1])
    safe = 2.0 * jnp.maximum(sel_bound, win_bound) <= SAFE_EXP_LOG2

    def step(c, k_ref, q_rows, vt_ref, keep=None):
        inside = c >= 0
        start = pl.multiple_of(jnp.maximum(c, 0) * tk, tk)
        shift = c * tk - t0

        def scores():
            s_t = _dot_nt(k_ref[0, 0, pl.ds(start, tk), :], q_rows)
            if keep is None:
                return s_t
            lo, hi = keep
            offset = offset_ref[...]
            if hi is not None:
                return jnp.where(offset <= jnp.where(inside, hi - shift, -FAR), s_t, NEG_INF)
            return jnp.where(offset > jnp.where(inside, lo - shift, FAR), s_t, NEG_INF)

        kappa = jnp.where(inside, slope_row * shift.astype(F32), NEG_INF)
        return scores, kappa, lambda: vt_ref[0, 0, :, pl.ds(start, tk)]

    last = t0 // tk
    causal, recent, anything = (None, 0), (-WINDOW, None), (-FAR, None)
    first_steps = [step(last - 2, kw_ref, q_aug, vwt_ref, recent) + (1,),
                   step(last, ks_ref, q_sel, vst_ref, causal) + (0,),
                   step(last - 1, kw_ref, q_aug, vwt_ref, anything) + (1,),
                   step(last, kw_ref, q_aug, vwt_ref, causal) + (1,)]

    def note_active(c, n):
        hit = active_ref[0, 0, 0, 0, c] > 0

        @pl.when(hit)
        def _():
            todo_ref[n] = c

        return n + hit.astype(jnp.int32)

    n_todo = lax.fori_loop(0, last, note_active, 0)
    sel_step = lambda i: step(todo_ref[i], ks_ref, q_sel, vst_ref) + (0,)

    accs_ref[...] = jnp.zeros_like(accs_ref)
    accw_ref[...] = jnp.zeros_like(accw_ref)

    @pl.when(safe)
    def _():
        in_tile = slope_row * (lax.broadcasted_iota(jnp.int32, (1, cols), 1) % tq).astype(F32)
        block = lambda steps: _fixed_frame_block(steps, [accs_ref, accw_ref],
                                                 [sel_bound + in_tile, win_bound + in_tile])
        block(first_steps)
        _chunk_loop(n_todo, sel_step, block, (8, 4, 2, 1))

    @pl.when(jnp.logical_not(safe))
    def _():
        states = [(ms_ref, accs_ref), (mw_ref, accw_ref)]
        for m_ref, _ in states:
            m_ref[...] = jnp.full_like(m_ref, NEG_INF)
        block = lambda steps: _softmax_block(steps, states)
        block(first_steps)
        _chunk_loop(n_todo, sel_step, block, (4, 2, 1))

    o_sel = accs_ref[0:HEAD_DIM, :] / accs_ref[HEAD_DIM:HEAD_DIM + 1, :]
    o_win = accw_ref[0:HEAD_DIM, :] / accw_ref[HEAD_DIM:HEAD_DIM + 1, :]

    gates = gate_ref[0]
    outs = []
    for r in range(NSA_HPG):
        cs = slice(r * tq, (r + 1) * tq)
        outs.append(gates[3 * r:3 * r + 1] * ocmp_ref[0, 0, r] + gates[3 * r + 1:3 * r + 2] * o_sel[:, cs]
                    + gates[3 * r + 2:3 * r + 3] * o_win[:, cs])
    o_ref[0] = jnp.concatenate(outs, axis=0).T.astype(o_ref.dtype)


def sel_win_attention(q, active, slope_cols, ks, vs_t, kw, vw_t, sel_bias, o_cmp, gates_t,
                      slopes, tq=256, tk=256):
    B, S, _ = q.shape
    assert tq == tk and WINDOW == 2 * tk
    G = NSA_GROUPS
    n_sel = S // SEL_LEN
    cols = NSA_HPG * tq
    smem = pl.BlockSpec(memory_space=pltpu.SMEM)
    ks_spec = pl.BlockSpec((1, 1, S, ks.shape[-1]), lambda b, g, i: (b, g, 0, 0))
    kw_spec = pl.BlockSpec((1, 1, S, kw.shape[-1]), lambda b, g, i: (b, g, 0, 0))
    v_spec = pl.BlockSpec((1, 1, V_ROWS_64, S), lambda b, g, i: (b, g, 0, 0))
    return pl.pallas_call(
        functools.partial(_sel_win_kernel, tq=tq, tk=tk),
        grid=(B, G, S // tq),
        in_specs=[smem,
                  pl.BlockSpec((1, 1, 1, 1, S // tk), lambda b, g, i: (b, g, i, 0, 0),
                               memory_space=pltpu.SMEM),
                  pl.BlockSpec((1, tq, 256), lambda b, g, i: (b, i, g)),
                  pl.BlockSpec((1, NSA_HPG, HEAD_DIM), lambda b, g, i: (g, 0, 0)),
                  ks_spec, v_spec, kw_spec, v_spec,
                  pl.BlockSpec((1, 1, tq, n_sel), lambda b, g, i: (b, g, i, 0)),
                  pl.BlockSpec((1, 1, NSA_HPG, HEAD_DIM, tq), lambda b, g, i: (b, g, 0, 0, i)),
                  pl.BlockSpec((1, 4 * NSA_HPG, tq), lambda b, g, i: (b, g, i))],
        out_specs=pl.BlockSpec((1, tq, 256), lambda b, g, i: (b, i, g)),
        out_shape=jax.ShapeDtypeStruct((B, S, G * 256), BF16),
        scratch_shapes=[pltpu.VMEM((1, cols), F32), pltpu.VMEM((V_ROWS_64, cols), F32),
                        pltpu.VMEM((1, cols), F32), pltpu.VMEM((V_ROWS_64, cols), F32),
                        pltpu.VMEM((tk, cols), jnp.int32),
                        pltpu.SMEM((S // tk,), jnp.int32), pltpu.SMEM((2,), F32)],
        compiler_params=_params(2, 1),
        name="sel_win_attention",
    )(slopes, active, q, slope_cols, ks, vs_t, kw, vw_t, sel_bias, o_cmp, gates_t)


def _alibi_slopes_log2(n_heads):
    slopes = np.exp2(-8.0 * (np.arange(n_heads, dtype=np.float32) + 1.0) / n_heads)
    return (slopes.astype(np.float32) * np.float32(LOG2E)).astype(np.float32)


def even_layer_mix(x2d, B, S, norm_g, w_in, lam_q1, lam_k1, lam_q2, lam_k2, subln, layer):
    proj, sb_vt, df_vt = even_proj(x2d, B, S, norm_g, w_in.astype(BF16))
    proj = proj.reshape(B, S, -1)
    lam_rows = jnp.stack([lam_q1, lam_k1, lam_q2, lam_k2]).astype(F32)
    mix = diff_attention(proj, df_vt, lam_rows, subln.astype(F32), _alibi_slopes_log2(4), layer,
                         sb_attention(proj, sb_vt))
    return mix.reshape(B * S, -1)


def even_w_out_rows(w_out):
    d = w_out.shape[1]
    return w_out.reshape(2, 4, 128, d).transpose(1, 0, 2, 3).reshape(-1, d)


def odd_layer_mix(x2d, B, S, norm_g, w_in, pos_k, k_w1, k_w2, pos_v, v_w1, v_w2, tq=256, tk=256):
    G, Dh = NSA_GROUPS, HEAD_DIM
    q_width = G * NSA_HPG * Dh
    n_main = q_width + 6 * G * Dh
    per_group = NSA_HPG * N_GATES
    w_gate = w_in[:, n_main:n_main + G * per_group].reshape(-1, G, per_group)
    w_gate = jnp.pad(w_gate, ((0, 0), (0, 0), (0, 4 * NSA_HPG - per_group))).reshape(-1, 4 * NSA_HPG * G)
    w_gate = jnp.pad(w_gate, ((0, 0), (0, GATE_PAD - w_gate.shape[1])))
    w_all = jnp.concatenate([w_in[:, :n_main], w_gate], axis=1).astype(BF16)

    n_chunks = S // CMP_STRIDE
    n_sel = S // SEL_LEN
    onehot = (jnp.arange(S)[:, None] // SEL_LEN == jnp.arange(n_sel)[None, :]).astype(BF16)
    q, cmp_in, ks, kw, vs_t, vw_t, gates_t = odd_proj(x2d, B, S, norm_g, w_all,
                                                      _key_position_columns(S, tk, Dh), onehot)
    q = q.reshape(B, S, q_width)

    pos_flat = jnp.stack([pos_k, pos_v]).reshape(2, 1, CMP_LEN * Dh).astype(F32)
    w1 = jnp.stack([k_w1, v_w1]).astype(BF16)
    w2 = jnp.stack([k_w2, v_w2]).astype(BF16)
    cmp = compress_kv(cmp_in, pos_flat, w1, w2).reshape(2, B, G, n_chunks, Dh)

    cmp_start = jnp.arange(n_chunks) * CMP_STRIDE
    sel_start = jnp.arange(n_sel) * SEL_LEN
    overlap_t = ((cmp_start[None, :] < sel_start[:, None] + SEL_LEN)
                 & (sel_start[:, None] <= cmp_start[None, :] + CMP_LEN - 1)).astype(BF16)
    slopes = _alibi_slopes_log2(G * NSA_HPG)
    slope_cols = _slope_pieces(slopes, Dh).reshape(G, NSA_HPG, Dh)

    o_cmp, sel_bias, hits = cmp_select(q, cmp[0], cmp[1].transpose(0, 1, 3, 2), overlap_t, slopes, tk, tq)
    active = (hits[..., 0] > 0).astype(jnp.int32)[:, :, :, None, :]
    o = sel_win_attention(q, active, slope_cols, ks, vs_t, kw, vw_t, sel_bias, o_cmp, gates_t, slopes,
                          tq=tq, tk=tk)
    return o.reshape(B * S, q_width)


def kernel(x, attn_norm, mlp_norm, final_norm, ev_w_in, ev_lam_q1, ev_lam_k1, ev_lam_q2, ev_lam_k2,
           ev_subln, ev_w_out, od_w_in, od_cmp_pos_k, od_cmp_k_w1, od_cmp_k_w2, od_cmp_pos_v,
           od_cmp_v_w1, od_cmp_v_w2, od_w_out, mlp_w1, mlp_w2):
    B, S, D = x.shape
    depth = attn_norm.shape[0]
    x2d = x.reshape(B * S, D)
    for layer in range(depth):
        idx = layer // 2
        if layer % 2 == 0:
            mix = even_layer_mix(x2d, B, S, attn_norm[layer], ev_w_in[idx], ev_lam_q1[idx],
                                 ev_lam_k1[idx], ev_lam_q2[idx], ev_lam_k2[idx], ev_subln[idx], layer)
            w_out = even_w_out_rows(ev_w_out[idx])
        else:
            mix = odd_layer_mix(x2d, B, S, attn_norm[layer], od_w_in[idx], od_cmp_pos_k[idx],
                                od_cmp_k_w1[idx], od_cmp_k_w2[idx], od_cmp_pos_v[idx],
                                od_cmp_v_w1[idx], od_cmp_v_w2[idx])
            w_out = od_w_out[idx]
        g_final = final_norm if layer == depth - 1 else None
        x2d = post_block(mix, x2d, w_out.astype(BF16), mlp_norm[layer], mlp_w1[layer].astype(BF16),
                         mlp_w2[layer].astype(BF16), g_final)
    return x2d.reshape(B, S, D)
```

```python
import functools
import math

import jax
import jax.numpy as jnp
import numpy as np
from jax import lax
from jax.experimental import pallas as pl
from jax.experimental.pallas import tpu as pltpu

F32 = jnp.float32
BF16 = jnp.bfloat16

HEAD_DIM = 64
RMS_EPS = 1e-6
NEG_INF = -1e30
FORCE_SCORE = 1e6
NSA_GROUPS = 4
NSA_HPG = 4
CMP_LEN = 32
CMP_STRIDE = 16
SEL_LEN = 64
SEL_TOPK = 16
WINDOW = 512
N_GATES = 3
GATE_PAD = 128
SEL_DROP = 2.0 ** 24
BF16_EXACT_INT = 256
V_ROWS_64 = 80
V_ROWS_128 = 144

LOG2E = math.log2(math.e)
Q_SCALE = HEAD_DIM ** -0.5 * LOG2E
UNDERFLOW_LOG2 = -160.0
SAFE_EXP_LOG2 = 100.0
FAR = 1 << 30
PICKED = -2.0

VMEM_LIMIT = 56 * 1024 * 1024


def _params(n_parallel, n_arbitrary=0):
    return pltpu.CompilerParams(dimension_semantics=("parallel",) * n_parallel + ("arbitrary",) * n_arbitrary,
                                vmem_limit_bytes=VMEM_LIMIT)


def _rms(x, g):
    ms = jnp.mean(x * x, axis=-1, keepdims=True)
    return x * lax.rsqrt(ms + RMS_EPS) * g


def _dot(a, b):
    return jnp.dot(a, b, preferred_element_type=F32)


def _dot_nt(a, b):
    return lax.dot_general(a, b, (((1,), (1,)), ((), ())), preferred_element_type=F32)


def _split_bf16(x):
    hi = x.astype(BF16)
    lo = (x - hi.astype(F32)).astype(BF16)
    return hi, lo


def _slope_pieces(slopes, width):
    def top_bits(x):
        return (x.view(np.uint32) & np.uint32(0xFFFF0000)).view(np.float32)

    s1 = top_bits(slopes)
    r1 = slopes - s1
    s2 = top_bits(r1)
    s3 = top_bits(r1 - s2)
    out = np.zeros((slopes.shape[0], width), np.float32)
    out[:, :6] = np.stack([s1, s2, s3, s1, s2, s3], axis=1)
    return jnp.asarray(out).astype(BF16)


def _key_position_columns(n, tk, width):
    j = jnp.arange(n) % tk
    a = (j // BF16_EXACT_INT) * BF16_EXACT_INT
    b = j % BF16_EXACT_INT
    cols = jnp.stack([a, a, a, b, b, b], axis=1).astype(BF16)
    return jnp.pad(cols, ((0, 0), (0, width - cols.shape[1])))


def _softmax_block(steps, states):
    s_all = [scores() for scores, _, _, _ in steps]
    live = {k: (states[k][0][...], states[k][1][...]) for k in sorted({k for _, _, _, k in steps})}
    for (_, kap, values, k), s_t in zip(steps, s_all):
        m_run, acc = live[k]
        m_new = jnp.maximum(m_run, jnp.max(s_t, axis=0, keepdims=True) + kap)
        p = jnp.exp2(s_t - (m_new - kap)).astype(BF16)
        live[k] = (m_new, jnp.exp2(m_run - m_new) * acc + _dot(values(), p))
    for k, (m_run, acc) in live.items():
        states[k][0][...] = m_run
        states[k][1][...] = acc


def _fixed_frame_block(steps, acc_refs, frames):
    s_all = [scores() for scores, _, _, _ in steps]
    live = {k: acc_refs[k][...] for k in sorted({k for _, _, _, k in steps})}
    for (_, kap, values, k), s_t in zip(steps, s_all):
        live[k] = live[k] + _dot(values(), jnp.exp2(s_t - (frames[k] - kap)).astype(BF16))
    for k, acc in live.items():
        acc_refs[k][...] = acc


def _chunk_loop(n_steps, step, block, groups):
    done = 0
    for group in groups:
        def body(i, carry, group=group, done=done):
            block([step(done + group * i + j) for j in range(group)])
            return carry

        n_groups = (n_steps - done) // group
        lax.fori_loop(0, n_groups, body, 0)
        done = done + group * n_groups


def _score_bound(q, k_norm2):
    q32 = q.astype(F32)
    width = q.shape[1]
    member = (lax.broadcasted_iota(jnp.int32, (width, 128), 0) // HEAD_DIM
              == lax.broadcasted_iota(jnp.int32, (width, 128), 1))
    head_norm2 = _dot((q32 * q32).astype(BF16), jnp.where(member, 1.0, 0.0).astype(BF16))
    return jnp.sqrt(jnp.max(head_norm2) * k_norm2 * 1.05)


PROJ_CHUNK = 512


def _ones_row_tail(rows, width):
    return jnp.where(lax.broadcasted_iota(jnp.int32, (rows, width), 0) == 0, 1.0, 0.0).astype(BF16)


def _even_proj_kernel(x_ref, g_ref, w_ref, o_ref, sbv_ref, dfv_ref):
    tm = x_ref.shape[0]
    xn = _rms(x_ref[...], g_ref[...]).astype(BF16)
    tail = _ones_row_tail(V_ROWS_128 - 2 * HEAD_DIM, tm)
    for c in range(w_ref.shape[1] // PROJ_CHUNK):
        sl = slice(c * PROJ_CHUNK, (c + 1) * PROJ_CHUNK)
        res = _dot(xn, w_ref[:, sl])
        if c in (0, 3):
            res = res * Q_SCALE
        o_ref[:, sl] = res.astype(BF16)
        if c == 2:
            sbv_ref[0] = res.T.astype(BF16)
        if c == 5:
            v_t = res.T.astype(BF16)
            for h in range(dfv_ref.shape[1]):
                dfv_ref[0, h, 0:2 * HEAD_DIM, :] = v_t[h * 2 * HEAD_DIM:(h + 1) * 2 * HEAD_DIM]
                dfv_ref[0, h, 2 * HEAD_DIM:V_ROWS_128, :] = tail


def even_proj(x2d, B, S, g, w_bf16, tm=512):
    T, D = x2d.shape
    n_w = w_bf16.shape[1]
    per_seq = S // tm
    assert n_w == 6 * PROJ_CHUNK
    return pl.pallas_call(
        _even_proj_kernel,
        grid=(T // tm,),
        in_specs=[pl.BlockSpec((tm, D), lambda i: (i, 0)),
                  pl.BlockSpec((1, D), lambda i: (0, 0)),
                  pl.BlockSpec((D, n_w), lambda i: (0, 0))],
        out_specs=[pl.BlockSpec((tm, n_w), lambda i: (i, 0)),
                   pl.BlockSpec((1, PROJ_CHUNK, tm), lambda i: (i // per_seq, 0, i % per_seq)),
                   pl.BlockSpec((1, 4, V_ROWS_128, tm), lambda i: (i // per_seq, 0, 0, i % per_seq))],
        out_shape=[jax.ShapeDtypeStruct((T, n_w), BF16),
                   jax.ShapeDtypeStruct((B, PROJ_CHUNK, S), BF16),
                   jax.ShapeDtypeStruct((B, 4, V_ROWS_128, S), BF16)],
        compiler_params=_params(1),
        name="even_proj",
    )(x2d, g.reshape(1, D), w_bf16)


def _odd_proj_kernel(x_ref, g_ref, w_ref, pos_ref, onehot_ref,
                     q_ref, cin_ref, ks_ref, kw_ref, vst_ref, vwt_ref, gate_ref):
    tm = x_ref.shape[0]
    G, Dh = NSA_GROUPS, HEAD_DIM
    kv = G * Dh
    xn = _rms(x_ref[...], g_ref[...]).astype(BF16)
    tail = _ones_row_tail(V_ROWS_64 - Dh, tm)
    pos = pos_ref[...]
    onehot = onehot_ref[...]
    q_width = q_ref.shape[1]
    for c in range(q_width // PROJ_CHUNK):
        sl = slice(c * PROJ_CHUNK, (c + 1) * PROJ_CHUNK)
        q_ref[:, sl] = (_dot(xn, w_ref[:, sl]) * Q_SCALE).astype(BF16)

    def pair(j):
        return _dot(xn, w_ref[:, q_width + j * 2 * kv:q_width + (j + 1) * 2 * kv])

    res = pair(0)
    for which in range(2):
        for g in range(G):
            cin_ref[which, 0, g] = res[:, which * kv + g * Dh:which * kv + (g + 1) * Dh]

    for j, k_ref, vt_ref, extra in ((1, ks_ref, vst_ref, [pos, onehot]), (2, kw_ref, vwt_ref, [pos])):
        res = pair(j)
        v_t = res[:, kv:2 * kv].T.astype(BF16)
        for g in range(G):
            k_ref[0, g] = jnp.concatenate([res[:, g * Dh:(g + 1) * Dh].astype(BF16)] + extra, axis=1)
            vt_ref[0, g, 0:Dh, :] = v_t[g * Dh:(g + 1) * Dh]
            vt_ref[0, g, Dh:V_ROWS_64, :] = tail

    logits = _dot(xn, w_ref[:, q_width + 6 * kv:q_width + 6 * kv + GATE_PAD])
    gate_ref[0] = jax.nn.sigmoid(logits).T[0:gate_ref.shape[1]]


def odd_proj(x2d, B, S, g, w_bf16, pos_cols, onehot, tm=512):
    T, D = x2d.shape
    G, Dh = NSA_GROUPS, HEAD_DIM
    q_width = G * NSA_HPG * Dh
    n_sel = onehot.shape[1]
    per_seq = S // tm
    seq_tile = lambda i: (i // per_seq, 0, i % per_seq, 0)
    seq_tile_t = lambda i: (i // per_seq, 0, 0, i % per_seq)
    return pl.pallas_call(
        _odd_proj_kernel,
        grid=(T // tm,),
        in_specs=[pl.BlockSpec((tm, D), lambda i: (i, 0)),
                  pl.BlockSpec((1, D), lambda i: (0, 0)),
                  pl.BlockSpec(w_bf16.shape, lambda i: (0, 0)),
                  pl.BlockSpec((tm, Dh), lambda i: (i % per_seq, 0)),
                  pl.BlockSpec((tm, n_sel), lambda i: (i % per_seq, 0))],
        out_specs=[pl.BlockSpec((tm, q_width), lambda i: (i, 0)),
                   pl.BlockSpec((2, 1, G, tm, Dh), lambda i: (0, i // per_seq, 0, i % per_seq, 0)),
                   pl.BlockSpec((1, G, tm, 2 * Dh + n_sel), seq_tile),
                   pl.BlockSpec((1, G, tm, 2 * Dh), seq_tile),
                   pl.BlockSpec((1, G, V_ROWS_64, tm), seq_tile_t),
                   pl.BlockSpec((1, G, V_ROWS_64, tm), seq_tile_t),
                   pl.BlockSpec((1, 4 * NSA_HPG * G, tm), lambda i: (i // per_seq, 0, i % per_seq))],
        out_shape=[jax.ShapeDtypeStruct((T, q_width), BF16),
                   jax.ShapeDtypeStruct((2, B, G, S, Dh), F32),
                   jax.ShapeDtypeStruct((B, G, S, 2 * Dh + n_sel), BF16),
                   jax.ShapeDtypeStruct((B, G, S, 2 * Dh), BF16),
                   jax.ShapeDtypeStruct((B, G, V_ROWS_64, S), BF16),
                   jax.ShapeDtypeStruct((B, G, V_ROWS_64, S), BF16),
                   jax.ShapeDtypeStruct((B, 4 * NSA_HPG * G, S), F32)],
        compiler_params=_params(1),
        name="odd_proj",
    )(x2d, g.reshape(1, D), w_bf16, pos_cols, onehot)


def _head_pair_rows(q, t):
    lane = lax.broadcasted_iota(jnp.int32, (t, 2 * HEAD_DIM), 1)
    zero = jnp.zeros_like(q)
    return jnp.where(lane < HEAD_DIM, q, zero), jnp.where(lane >= HEAD_DIM, q, zero)


def _sb_kernel(q_ref, k_ref, vt_ref, o_ref, acc_ref, carry_ref, *, t):
    qi = pl.program_id(2)
    cols = 2 * t
    q_both = jnp.concatenate(_head_pair_rows(q_ref[0], t), axis=0)
    s_idx = lax.broadcasted_iota(jnp.int32, (t, 2 * t), 0)
    j_idx = lax.broadcasted_iota(jnp.int32, (t, 2 * t), 1) % t
    upper2 = jnp.where(j_idx > s_idx, 1.0, 0.0).astype(BF16)
    key = lax.broadcasted_iota(jnp.int32, (t, cols), 0)
    qry = lax.broadcasted_iota(jnp.int32, (t, cols), 1) % t
    past_diag = key < qry

    acc_ref[...] = jnp.zeros_like(acc_ref)
    carry_ref[...] = jnp.zeros_like(carry_ref)

    def chunks(specs):
        starts = [pl.multiple_of(kc * t, t) for kc, _ in specs]
        z_all = [_dot_nt(k_ref[0, pl.ds(start, t), :], q_both) for start in starts]
        carry = carry_ref[...]
        acc = acc_ref[...]
        for (_, masked), start, z in zip(specs, starts, z_all):
            drop = jnp.maximum(z, 0.0) + jnp.log2(1.0 + jnp.exp2(jnp.abs(z) * -1.0))
            log_beta = z - drop
            if masked:
                drop = jnp.where(past_diag, drop, 0.0)
            hi, lo = _split_bf16(drop)
            tail = _dot(upper2, jnp.concatenate([hi, lo], axis=0))
            w = jnp.exp2(log_beta - tail - carry)
            if masked:
                w = jnp.where(past_diag, w, 0.0)
            carry = carry + tail[0:1] + drop[0:1]
            acc = acc + _dot(vt_ref[0, :, pl.ds(start, t)], w.astype(BF16))
        carry_ref[...] = carry
        acc_ref[...] = acc
        return jnp.min(carry)

    least = lax.cond(qi > 0, lambda: chunks([(qi, True), (qi - 1, False)]), lambda: chunks([(qi, True)]))

    def cond(state):
        j, least_carry = state
        return (j < qi) & (least_carry < -UNDERFLOW_LOG2)

    def body(state):
        j, _ = state
        return j + 1, chunks([(qi - 1 - j, False)])

    lax.while_loop(cond, body, (1, least))
    row = lax.broadcasted_iota(jnp.int32, (2 * HEAD_DIM, t), 0)
    o_t = jnp.where(row < HEAD_DIM, acc_ref[:, :t], acc_ref[:, t:])
    o_ref[0] = o_t.T.astype(o_ref.dtype)


def sb_attention(proj, v_t, t=256):
    B, S, _ = proj.shape
    n_pairs = 4
    return pl.pallas_call(
        functools.partial(_sb_kernel, t=t),
        grid=(B, n_pairs, S // t),
        in_specs=[pl.BlockSpec((1, t, 128), lambda b, p, i: (b, i, p)),
                  pl.BlockSpec((1, S, 128), lambda b, p, i: (b, 0, n_pairs + p)),
                  pl.BlockSpec((1, 128, S), lambda b, p, i: (b, p, 0))],
        out_specs=pl.BlockSpec((1, t, 128), lambda b, p, i: (b, i, p)),
        out_shape=jax.ShapeDtypeStruct((B, S, n_pairs * 128), BF16),
        scratch_shapes=[pltpu.VMEM((128, 2 * t), F32), pltpu.VMEM((1, 2 * t), F32)],
        compiler_params=_params(3),
        name="sb_attention",
    )(proj, proj, v_t)


def _diff_kernel(slopes_ref, lam_ref, q_ref, k_ref, vt_ref, pos_ref, sl_ref, subln_ref, sb_ref, o_ref,
                 m_ref, acc_ref, offset_ref, knorm_ref, *, t, out_scale, lambda_init):
    h = pl.program_id(1)
    qi = pl.program_id(2)
    slope = slopes_ref[h]
    cols = 2 * t
    slope_cols = jnp.broadcast_to(sl_ref[0], (t, 128))
    q_parts = _head_pair_rows(q_ref[0], t)
    q_both = jnp.concatenate([jnp.concatenate([qc, slope_cols], axis=1) for qc in q_parts],
                             axis=0)
    lane = lax.broadcasted_iota(jnp.int32, (t, 2 * HEAD_DIM), 1)

    @pl.when(qi == 0)
    def _():
        def body(c, best):
            k = k_ref[0, pl.ds(pl.multiple_of(c * t, t), t), :].astype(F32)
            k2 = k * k
            return (jnp.maximum(best[0], jnp.max(jnp.sum(jnp.where(lane < HEAD_DIM, k2, 0.0), axis=1))),
                    jnp.maximum(best[1], jnp.max(jnp.sum(jnp.where(lane >= HEAD_DIM, k2, 0.0), axis=1))))

        best = lax.fori_loop(0, k_ref.shape[1] // t, body, (jnp.float32(0.0), jnp.float32(0.0)))
        knorm_ref[0] = best[0]
        knorm_ref[1] = best[1]
        offset_ref[...] = (lax.broadcasted_iota(jnp.int32, (t, cols), 0)
                           - lax.broadcasted_iota(jnp.int32, (t, cols), 1) % t)

    qk_bound = _score_bound(q_ref[0], jnp.maximum(knorm_ref[0], knorm_ref[1]))
    safe = 2.0 * qk_bound <= SAFE_EXP_LOG2
    col = lax.broadcasted_iota(jnp.int32, (1, cols), 1)

    pos = pos_ref[...]

    def step(kc, diagonal=False):
        start = pl.multiple_of(kc * t, t)

        def scores():
            s_t = _dot_nt(jnp.concatenate([k_ref[0, pl.ds(start, t), :], pos], axis=1), q_both)
            return jnp.where(offset_ref[...] <= 0, s_t, NEG_INF) if diagonal else s_t

        return scores, slope * ((kc - qi) * t).astype(F32), lambda: vt_ref[0, 0, :, pl.ds(start, t)], 0

    acc_ref[...] = jnp.zeros_like(acc_ref)

    @pl.when(safe)
    def _():
        frame = qk_bound + slope * (col % t).astype(F32)
        block = lambda steps: _fixed_frame_block(steps, [acc_ref], [frame])
        block([step(qi, True)])
        n_back = jnp.ceil((slope * (t - 1) - UNDERFLOW_LOG2) / (slope * t)).astype(jnp.int32)
        _chunk_loop(jnp.minimum(n_back, qi), lambda i: step(qi - 1 - i), block, (8, 4, 2, 1))

    @pl.when(jnp.logical_not(safe))
    def _():
        m_ref[...] = jnp.full_like(m_ref, NEG_INF)
        block = lambda steps: _softmax_block(steps, [(m_ref, acc_ref)])
        block([step(qi, True)])
        excess = jnp.max(qk_bound - m_ref[...]) + slope * (t - 1)
        n_back = jnp.clip(jnp.ceil((excess - UNDERFLOW_LOG2) / (slope * t)).astype(jnp.int32), 0, qi)
        _chunk_loop(n_back, lambda i: step(qi - 1 - i), block, (4, 2, 1))

    lam_terms = lam_ref[...]
    lam = (jnp.exp(jnp.sum(lam_terms[0:1] * lam_terms[1:2], axis=1, keepdims=True))
           - jnp.exp(jnp.sum(lam_terms[2:3] * lam_terms[3:4], axis=1, keepdims=True))
           + lambda_init)
    o_t = acc_ref[0:128, :] / acc_ref[128:129, :]
    o = (o_t[:, :t] - lam * o_t[:, t:]).T
    o_ref[0, :, 0:128] = sb_ref[0]
    o_ref[0, :, 128:256] = (_rms(o, subln_ref[...]) * out_scale).astype(o_ref.dtype)


def diff_attention(proj, vt_aug, lam_rows, subln, slopes, layer, o_sb, t=512):
    B, S, _ = proj.shape
    n_heads = 4
    lambda_init = 0.8 - 0.6 * math.exp(-0.3 * layer)
    smem = pl.BlockSpec(memory_space=pltpu.SMEM)
    pos = _key_position_columns(t, t, 128)
    slope_cols = _slope_pieces(slopes, 128).reshape(n_heads, 1, 128)
    return pl.pallas_call(
        functools.partial(_diff_kernel, t=t, out_scale=1.0 - lambda_init, lambda_init=lambda_init),
        grid=(B, n_heads, S // t),
        in_specs=[smem,
                  pl.BlockSpec((4, HEAD_DIM), lambda b, h, i: (0, 0)),
                  pl.BlockSpec((1, t, 128), lambda b, h, i: (b, i, 12 + h)),
                  pl.BlockSpec((1, S, 128), lambda b, h, i: (b, 0, 16 + h)),
                  pl.BlockSpec((1, 1, V_ROWS_128, S), lambda b, h, i: (b, h, 0, 0)),
                  pl.BlockSpec((t, 128), lambda b, h, i: (0, 0)),
                  pl.BlockSpec((1, 1, 128), lambda b, h, i: (h, 0, 0)),
                  pl.BlockSpec((1, 128), lambda b, h, i: (0, 0)),
                  pl.BlockSpec((1, t, 128), lambda b, h, i: (b, i, h))],
        out_specs=pl.BlockSpec((1, t, 256), lambda b, h, i: (b, i, h)),
        out_shape=jax.ShapeDtypeStruct((B, S, n_heads * 256), BF16),
        scratch_shapes=[pltpu.VMEM((1, 2 * t), F32), pltpu.VMEM((V_ROWS_128, 2 * t), F32),
                        pltpu.VMEM((t, 2 * t), jnp.int32), pltpu.SMEM((2,), F32)],
        compiler_params=_params(2, 1),
        name="diff_attention",
    )(slopes, lam_rows, proj, proj, vt_aug, pos, slope_cols, subln.reshape(1, 128), o_sb)


def _post_kernel(*refs, ff_chunk, final):
    mix_ref, x_ref, wo_ref, g_ref, w1_ref, w2_ref = refs[:6]
    gf_ref = refs[6] if final else None
    o_ref = refs[-1]
    x = x_ref[...] + _dot(mix_ref[...], wo_ref[...])
    hn = _rms(x, g_ref[...]).astype(BF16)
    acc = x
    for f in range(w1_ref.shape[1] // ff_chunk):
        sl = slice(f * ff_chunk, (f + 1) * ff_chunk)
        hid = jnp.maximum(_dot(hn, w1_ref[:, sl]), 0.0)
        acc = acc + _dot((hid * hid).astype(BF16), w2_ref[sl, :])
    if final:
        acc = _rms(acc, gf_ref[...])
    o_ref[...] = acc


def post_block(mix, x2d, w_out, g_mlp, w1, w2, g_final=None, tm=512, ff_chunk=1024):
    T, D = x2d.shape
    final = g_final is not None
    const = lambda i: (0, 0)
    in_specs = [pl.BlockSpec((tm, mix.shape[1]), lambda i: (i, 0)),
                pl.BlockSpec((tm, D), lambda i: (i, 0)),
                pl.BlockSpec(w_out.shape, const),
                pl.BlockSpec((1, D), const), pl.BlockSpec(w1.shape, const), pl.BlockSpec(w2.shape, const)]
    args = [mix, x2d, w_out, g_mlp.reshape(1, D), w1, w2]
    if final:
        in_specs.append(pl.BlockSpec((1, D), const))
        args.append(g_final.reshape(1, D))
    return pl.pallas_call(
        functools.partial(_post_kernel, ff_chunk=ff_chunk, final=final),
        grid=(T // tm,),
        in_specs=in_specs,
        out_specs=pl.BlockSpec((tm, D), lambda i: (i, 0)),
        out_shape=jax.ShapeDtypeStruct((T, D), F32),
        compiler_params=_params(1),
        name="post_block",
    )(*args)


def _compress_kernel(c_ref, pos_ref, w1_ref, w2_ref, o_ref):
    half = w1_ref.shape[1] // 2
    n_chunks = c_ref.shape[3] // CMP_STRIDE
    first = second = None
    for l in range(CMP_STRIDE):
        tok = c_ref[0, 0, 0, pl.ds(l, n_chunks, stride=CMP_STRIDE), :].astype(BF16)
        a = _dot(tok, w1_ref[0, l * HEAD_DIM:(l + 1) * HEAD_DIM, :])
        b = _dot(tok, w1_ref[0, half + l * HEAD_DIM:half + (l + 1) * HEAD_DIM, :])
        first = a if first is None else first + a
        second = b if second is None else second + b
    pos = jnp.broadcast_to(pos_ref[0], (8, 2 * half)).astype(BF16)
    pre = first + pltpu.roll(second, n_chunks - 1, 0) + _dot(pos, w1_ref[0])[0:1]
    hid = jax.nn.gelu(pre)
    o_ref[0, 0] = _dot(hid.astype(BF16), w2_ref[0]).astype(o_ref.dtype)


def compress_kv(tokens, pos_flat, w1, w2):
    _, B, G, S, Dh = tokens.shape
    n_chunks = S // CMP_STRIDE
    hidden = w1.shape[-1]
    return pl.pallas_call(
        _compress_kernel,
        grid=(2, B * G),
        in_specs=[pl.BlockSpec((1, 1, 1, S, Dh), lambda s, i: (s, i // G, i % G, 0, 0)),
                  pl.BlockSpec((1, 1, CMP_LEN * Dh), lambda s, i: (s, 0, 0)),
                  pl.BlockSpec((1, CMP_LEN * Dh, hidden), lambda s, i: (s, 0, 0)),
                  pl.BlockSpec((1, hidden, Dh), lambda s, i: (s, 0, 0))],
        out_specs=pl.BlockSpec((1, 1, n_chunks, Dh), lambda s, i: (s, i, 0, 0)),
        out_shape=jax.ShapeDtypeStruct((2, B * G, n_chunks, Dh), BF16),
        compiler_params=_params(2),
        name="compress_kv",
    )(tokens, pos_flat, w1, w2)


def _stack_heads(q):
    return jnp.concatenate([q[:, r * HEAD_DIM:(r + 1) * HEAD_DIM] for r in range(NSA_HPG)], axis=0)


def _slope_row(slopes_ref, g, tq):
    col = lax.broadcasted_iota(jnp.int32, (1, NSA_HPG * tq), 1)
    out = jnp.zeros((1, NSA_HPG * tq), F32)
    for r in range(NSA_HPG):
        out = jnp.where(col // tq == r, slopes_ref[g * NSA_HPG + r], out)
    return out


def _cmp_select_kernel(slopes_ref, q_ref, kc_ref, vct_ref, ovt_ref, ocmp_ref, sel_ref, hits_ref,
                       imp_ref, *, tq, n_sel, blocks_per_chunk, hit_tile):
    g = pl.program_id(1)
    t0 = pl.program_id(2) * tq
    cols = NSA_HPG * tq
    n_cmp = kc_ref.shape[2]
    slope_row = _slope_row(slopes_ref, g, tq)
    q_rows = _stack_heads(q_ref[0])

    def group_sum(pc):
        out = pc[:, 0:tq]
        for r in range(1, NSA_HPG):
            out = out + pc[:, r * tq:(r + 1) * tq]
        return out

    def compressed_branch(n_rows):
        tpos = t0 + lax.broadcasted_iota(jnp.int32, (n_rows, cols), 1) % tq
        cmp_end = lax.broadcasted_iota(jnp.int32, (n_rows, cols), 0) * CMP_STRIDE + (CMP_LEN - 1)
        dc = (tpos - cmp_end).astype(F32)
        sc = jnp.where(dc >= 0, _dot_nt(kc_ref[0, 0, 0:n_rows, :], q_rows) - slope_row * dc, NEG_INF)
        e = jnp.exp2(sc - jnp.max(sc, axis=0, keepdims=True))
        any_valid = jnp.where(dc[0:1] >= 0, 1.0, 0.0)
        pc = e * (any_valid / jnp.sum(e, axis=0, keepdims=True))
        o_t = _dot(vct_ref[0, 0, :, 0:n_rows], pc.astype(BF16))
        for r in range(NSA_HPG):
            ocmp_ref[0, 0, r] = o_t[:, r * tq:(r + 1) * tq]
        hi, lo = _split_bf16(group_sum(pc))
        imp_ref[...] = _dot(ovt_ref[:, 0:n_rows], hi) + _dot(ovt_ref[:, 0:n_rows], lo)

    quarter = n_cmp // 4
    needed = jnp.clip(((t0 + tq - CMP_LEN) // CMP_STRIDE) // quarter + 1, 1, 4)
    for count in range(1, 5):
        @pl.when(needed == count)
        def _():
            compressed_branch(count * quarter)

    imp = imp_ref[...]
    blk = lax.broadcasted_iota(jnp.int32, (n_sel, tq), 0)
    cur = (t0 + lax.broadcasted_iota(jnp.int32, (n_sel, tq), 1)) // SEL_LEN
    forced = (blk == 0) | (blk == cur) | (blk == cur - 1)
    imp = jnp.where(blk <= cur, imp, -1.0)
    topk = min(SEL_TOPK, n_sel)

    def pick(imp, count):
        for _ in range(count):
            best = jnp.max(imp, axis=0, keepdims=True)
            first = jnp.min(jnp.where(imp == best, blk, n_sel), axis=0, keepdims=True)
            imp = jnp.where(blk == first, PICKED, imp)
        return imp

    imp = lax.cond(t0 >= 2 * SEL_LEN,
                   lambda: pick(jnp.where(forced, PICKED, imp), topk - 3),
                   lambda: pick(jnp.where(forced, FORCE_SCORE, imp), topk))
    keep = (imp == PICKED) & (blk <= cur)
    sel_ref[0, 0] = jnp.where(keep, 0.0, -SEL_DROP).T.astype(sel_ref.dtype)
    n_chunks = n_sel // blocks_per_chunk
    member = (lax.broadcasted_iota(jnp.int32, (n_chunks, n_sel), 1) // blocks_per_chunk
              == lax.broadcasted_iota(jnp.int32, (n_chunks, n_sel), 0))
    per_query = _dot(jnp.where(member, 1.0, 0.0).astype(BF16),
                     jnp.where(keep, 1.0, 0.0).astype(BF16)).astype(BF16)
    for part in range(tq // hit_tile):
        hits_ref[0, 0, part] = _dot(per_query[:, part * hit_tile:(part + 1) * hit_tile],
                                    jnp.ones((hit_tile, 128), BF16))


def cmp_select(q, kc, vc_t, overlap_t, slopes, tk, hit_tile, tq=256):
    B, S, _ = q.shape
    n_cmp = kc.shape[2]
    n_sel = S // SEL_LEN
    G = NSA_GROUPS
    smem = pl.BlockSpec(memory_space=pltpu.SMEM)
    return pl.pallas_call(
        functools.partial(_cmp_select_kernel, tq=tq, n_sel=n_sel, blocks_per_chunk=tk // SEL_LEN,
                          hit_tile=hit_tile),
        grid=(B, G, S // tq),
        in_specs=[smem,
                  pl.BlockSpec((1, tq, 256), lambda b, g, i: (b, i, g)),
                  pl.BlockSpec((1, 1, n_cmp, HEAD_DIM), lambda b, g, i: (b, g, 0, 0)),
                  pl.BlockSpec((1, 1, HEAD_DIM, n_cmp), lambda b, g, i: (b, g, 0, 0)),
                  pl.BlockSpec((n_sel, n_cmp), lambda b, g, i: (0, 0))],
        out_specs=[pl.BlockSpec((1, 1, NSA_HPG, HEAD_DIM, tq), lambda b, g, i: (b, g, 0, 0, i)),
                   pl.BlockSpec((1, 1, tq, n_sel), lambda b, g, i: (b, g, i, 0)),
                   pl.BlockSpec((1, 1, tq // hit_tile, S // tk, 128), lambda b, g, i: (b, g, i, 0, 0))],
        out_shape=[jax.ShapeDtypeStruct((B, G, NSA_HPG, HEAD_DIM, S), F32),
                   jax.ShapeDtypeStruct((B, G, S, n_sel), BF16),
                   jax.ShapeDtypeStruct((B, G, S // hit_tile, S // tk, 128), F32)],
        scratch_shapes=[pltpu.VMEM((n_sel, tq), F32)],
        compiler_params=_params(3),
        name="cmp_select",
    )(slopes, q, kc, vc_t, overlap_t)


def _sel_win_kernel(slopes_ref, active_ref, q_ref, sl_ref, ks_ref, vst_ref, kw_ref, vwt_ref, sel_ref,
                    ocmp_ref, gate_ref, o_ref, ms_ref, accs_ref, mw_ref, accw_ref, offset_ref, todo_ref,
                    knorm_ref, *, tq, tk):
    g = pl.program_id(1)
    t0 = pl.program_id(2) * tq
    cols = NSA_HPG * tq
    slope_row = _slope_row(slopes_ref, g, tq)
    q = q_ref[0]
    q_aug = jnp.concatenate(
        [jnp.concatenate([q[:, r * HEAD_DIM:(r + 1) * HEAD_DIM],
                          jnp.broadcast_to(sl_ref[0, r:r + 1, :], (tq, HEAD_DIM))], axis=1)
         for r in range(NSA_HPG)], axis=0)
    q_sel = jnp.concatenate([q_aug, jnp.concatenate([sel_ref[0, 0]] * NSA_HPG, axis=0)], axis=1)

    @pl.when(pl.program_id(2) == 0)
    def _():
        offset_ref[...] = (lax.broadcasted_iota(jnp.int32, (tk, cols), 0)
                           - lax.broadcasted_iota(jnp.int32, (tk, cols), 1) % tq)

        def body(c, best):
            rows = pl.ds(pl.multiple_of(c * tk, tk), tk)
            ks = ks_ref[0, 0, rows, :][:, :HEAD_DIM].astype(F32)
            kw = kw_ref[0, 0, rows, :][:, :HEAD_DIM].astype(F32)
            return (jnp.maximum(best[0], jnp.max(jnp.sum(ks * ks, axis=1))),
                    jnp.maximum(best[1], jnp.max(jnp.sum(kw * kw, axis=1))))

        best = lax.fori_loop(0, ks_ref.shape[2] // tk, body, (jnp.float32(0.0), jnp.float32(0.0)))
        knorm_ref[0] = best[0]
        knorm_ref[1] = best[1]

    sel_bound = _score_bound(q, knorm_ref[0])
    win_bound = _score_bound(q, knorm_ref[1])
    safe = 2.0 * jnp.maximum(sel_bound, win_bound) <= SAFE_EXP_LOG2

    def step(c, k_ref, q_rows, vt_ref, keep=None):
        inside = c >= 0
        start = pl.multiple_of(jnp.maximum(c, 0) * tk, tk)
        shift = c * tk - t0

        def scores():
            s_t = _dot_nt(k_ref[0, 0, pl.ds(start, tk), :], q_rows)
            if keep is None:
                return s_t
            lo, hi = keep
            offset = offset_ref[...]
            if hi is not None:
                return jnp.where(offset <= jnp.where(inside, hi - shift, -FAR), s_t, NEG_INF)
            return jnp.where(offset > jnp.where(inside, lo - shift, FAR), s_t, NEG_INF)

        kappa = jnp.where(inside, slope_row * shift.astype(F32), NEG_INF)
        return scores, kappa, lambda: vt_ref[0, 0, :, pl.ds(start, tk)]

    last = t0 // tk
    causal, recent, anything = (None, 0), (-WINDOW, None), (-FAR, None)
    first_steps = [step(last - 2, kw_ref, q_aug, vwt_ref, recent) + (1,),
                   step(last, ks_ref, q_sel, vst_ref, causal) + (0,),
                   step(last - 1, kw_ref, q_aug, vwt_ref, anything) + (1,),
                   step(last, kw_ref, q_aug, vwt_ref, causal) + (1,)]

    def note_active(c, n):
        hit = active_ref[0, 0, 0, 0, c] > 0

        @pl.when(hit)
        def _():
            todo_ref[n] = c

        return n + hit.astype(jnp.int32)

    n_todo = lax.fori_loop(0, last, note_active, 0)
    sel_step = lambda i: step(todo_ref[i], ks_ref, q_sel, vst_ref) + (0,)

    accs_ref[...] = jnp.zeros_like(accs_ref)
    accw_ref[...] = jnp.zeros_like(accw_ref)

    @pl.when(safe)
    def _():
        in_tile = slope_row * (lax.broadcasted_iota(jnp.int32, (1, cols), 1) % tq).astype(F32)
        block = lambda steps: _fixed_frame_block(steps, [accs_ref, accw_ref],
                                                 [sel_bound + in_tile, win_bound + in_tile])
        block(first_steps)
        _chunk_loop(n_todo, sel_step, block, (8, 4, 2, 1))

    @pl.when(jnp.logical_not(safe))
    def _():
        states = [(ms_ref, accs_ref), (mw_ref, accw_ref)]
        for m_ref, _ in states:
            m_ref[...] = jnp.full_like(m_ref, NEG_INF)
        block = lambda steps: _softmax_block(steps, states)
        block(first_steps)
        _chunk_loop(n_todo, sel_step, block, (4, 2, 1))

    o_sel = accs_ref[0:HEAD_DIM, :] / accs_ref[HEAD_DIM:HEAD_DIM + 1, :]
    o_win = accw_ref[0:HEAD_DIM, :] / accw_ref[HEAD_DIM:HEAD_DIM + 1, :]

    gates = gate_ref[0]
    outs = []
    for r in range(NSA_HPG):
        cs = slice(r * tq, (r + 1) * tq)
        outs.append(gates[3 * r:3 * r + 1] * ocmp_ref[0, 0, r] + gates[3 * r + 1:3 * r + 2] * o_sel[:, cs]
                    + gates[3 * r + 2:3 * r + 3] * o_win[:, cs])
    o_ref[0] = jnp.concatenate(outs, axis=0).T.astype(o_ref.dtype)


def sel_win_attention(q, active, slope_cols, ks, vs_t, kw, vw_t, sel_bias, o_cmp, gates_t,
                      slopes, tq=256, tk=256):
    B, S, _ = q.shape
    assert tq == tk and WINDOW == 2 * tk
    G = NSA_GROUPS
    n_sel = S // SEL_LEN
    cols = NSA_HPG * tq
    smem = pl.BlockSpec(memory_space=pltpu.SMEM)
    ks_spec = pl.BlockSpec((1, 1, S, ks.shape[-1]), lambda b, g, i: (b, g, 0, 0))
    kw_spec = pl.BlockSpec((1, 1, S, kw.shape[-1]), lambda b, g, i: (b, g, 0, 0))
    v_spec = pl.BlockSpec((1, 1, V_ROWS_64, S), lambda b, g, i: (b, g, 0, 0))
    return pl.pallas_call(
        functools.partial(_sel_win_kernel, tq=tq, tk=tk),
        grid=(B, G, S // tq),
        in_specs=[smem,
                  pl.BlockSpec((1, 1, 1, 1, S // tk), lambda b, g, i: (b, g, i, 0, 0),
                               memory_space=pltpu.SMEM),
                  pl.BlockSpec((1, tq, 256), lambda b, g, i: (b, i, g)),
                  pl.BlockSpec((1, NSA_HPG, HEAD_DIM), lambda b, g, i: (g, 0, 0)),
                  ks_spec, v_spec, kw_spec, v_spec,
                  pl.BlockSpec((1, 1, tq, n_sel), lambda b, g, i: (b, g, i, 0)),
                  pl.BlockSpec((1, 1, NSA_HPG, HEAD_DIM, tq), lambda b, g, i: (b, g, 0, 0, i)),
                  pl.BlockSpec((1, 4 * NSA_HPG, tq), lambda b, g, i: (b, g, i))],
        out_specs=pl.BlockSpec((1, tq, 256), lambda b, g, i: (b, i, g)),
        out_shape=jax.ShapeDtypeStruct((B, S, G * 256), BF16),
        scratch_shapes=[pltpu.VMEM((1, cols), F32), pltpu.VMEM((V_ROWS_64, cols), F32),
                        pltpu.VMEM((1, cols), F32), pltpu.VMEM((V_ROWS_64, cols), F32),
                        pltpu.VMEM((tk, cols), jnp.int32),
                        pltpu.SMEM((S // tk,), jnp.int32), pltpu.SMEM((2,), F32)],
        compiler_params=_params(2, 1),
        name="sel_win_attention",
    )(slopes, active, q, slope_cols, ks, vs_t, kw, vw_t, sel_bias, o_cmp, gates_t)


def _alibi_slopes_log2(n_heads):
    slopes = np.exp2(-8.0 * (np.arange(n_heads, dtype=np.float32) + 1.0) / n_heads)
    return (slopes.astype(np.float32) * np.float32(LOG2E)).astype(np.float32)


def even_layer_mix(x2d, B, S, norm_g, w_in, lam_q1, lam_k1, lam_q2, lam_k2, subln, layer):
    proj, sb_vt, df_vt = even_proj(x2d, B, S, norm_g, w_in.astype(BF16))
    proj = proj.reshape(B, S, -1)
    lam_rows = jnp.stack([lam_q1, lam_k1, lam_q2, lam_k2]).astype(F32)
    mix = diff_attention(proj, df_vt, lam_rows, subln.astype(F32), _alibi_slopes_log2(4), layer,
                         sb_attention(proj, sb_vt))
    return mix.reshape(B * S, -1)


def even_w_out_rows(w_out):
    d = w_out.shape[1]
    return w_out.reshape(2, 4, 128, d).transpose(1, 0, 2, 3).reshape(-1, d)


def odd_layer_mix(x2d, B, S, norm_g, w_in, pos_k, k_w1, k_w2, pos_v, v_w1, v_w2, tq=256, tk=256):
    G, Dh = NSA_GROUPS, HEAD_DIM
    q_width = G * NSA_HPG * Dh
    n_main = q_width + 6 * G * Dh
    per_group = NSA_HPG * N_GATES
    w_gate = w_in[:, n_main:n_main + G * per_group].reshape(-1, G, per_group)
    w_gate = jnp.pad(w_gate, ((0, 0), (0, 0), (0, 4 * NSA_HPG - per_group))).reshape(-1, 4 * NSA_HPG * G)
    w_gate = jnp.pad(w_gate, ((0, 0), (0, GATE_PAD - w_gate.shape[1])))
    w_all = jnp.concatenate([w_in[:, :n_main], w_gate], axis=1).astype(BF16)

    n_chunks = S // CMP_STRIDE
    n_sel = S // SEL_LEN
    onehot = (jnp.arange(S)[:, None] // SEL_LEN == jnp.arange(n_sel)[None, :]).astype(BF16)
    q, cmp_in, ks, kw, vs_t, vw_t, gates_t = odd_proj(x2d, B, S, norm_g, w_all,
                                                      _key_position_columns(S, tk, Dh), onehot)
    q = q.reshape(B, S, q_width)

    pos_flat = jnp.stack([pos_k, pos_v]).reshape(2, 1, CMP_LEN * Dh).astype(F32)
    w1 = jnp.stack([k_w1, v_w1]).astype(BF16)
    w2 = jnp.stack([k_w2, v_w2]).astype(BF16)
    cmp = compress_kv(cmp_in, pos_flat, w1, w2).reshape(2, B, G, n_chunks, Dh)

    cmp_start = jnp.arange(n_chunks) * CMP_STRIDE
    sel_start = jnp.arange(n_sel) * SEL_LEN
    overlap_t = ((cmp_start[None, :] < sel_start[:, None] + SEL_LEN)
                 & (sel_start[:, None] <= cmp_start[None, :] + CMP_LEN - 1)).astype(BF16)
    slopes = _alibi_slopes_log2(G * NSA_HPG)
    slope_cols = _slope_pieces(slopes, Dh).reshape(G, NSA_HPG, Dh)

    o_cmp, sel_bias, hits = cmp_select(q, cmp[0], cmp[1].transpose(0, 1, 3, 2), overlap_t, slopes, tk, tq)
    active = (hits[..., 0] > 0).astype(jnp.int32)[:, :, :, None, :]
    o = sel_win_attention(q, active, slope_cols, ks, vs_t, kw, vw_t, sel_bias, o_cmp, gates_t, slopes,
                          tq=tq, tk=tk)
    return o.reshape(B * S, q_width)


def kernel(x, attn_norm, mlp_norm, final_norm, ev_w_in, ev_lam_q1, ev_lam_k1, ev_lam_q2, ev_lam_k2,
           ev_subln, ev_w_out, od_w_in, od_cmp_pos_k, od_cmp_k_w1, od_cmp_k_w2, od_cmp_pos_v,
           od_cmp_v_w1, od_cmp_v_w2, od_w_out, mlp_w1, mlp_w2):
    B, S, D = x.shape
    depth = attn_norm.shape[0]
    x2d = x.reshape(B * S, D)
    for layer in range(depth):
        idx = layer // 2
        if layer % 2 == 0:
            mix = even_layer_mix(x2d, B, S, attn_norm[layer], ev_w_in[idx], ev_lam_q1[idx],
                                 ev_lam_k1[idx], ev_lam_q2[idx], ev_lam_k2[idx], ev_subln[idx], layer)
            w_out = even_w_out_rows(ev_w_out[idx])
        else:
            mix = odd_layer_mix(x2d, B, S, attn_norm[layer], od_w_in[idx], od_cmp_pos_k[idx],
                                od_cmp_k_w1[idx], od_cmp_k_w2[idx], od_cmp_pos_v[idx],
                                od_cmp_v_w1[idx], od_cmp_v_w2[idx])
            w_out = od_w_out[idx]
        g_final = final_norm if layer == depth - 1 else None
        x2d = post_block(mix, x2d, w_out.astype(BF16), mlp_norm[layer], mlp_w1[layer].astype(BF16),
                         mlp_w2[layer].astype(BF16), g_final)
    return x2d.reshape(B, S, D)
```

```python
import functools
import math

import jax
import jax.numpy as jnp
import numpy as np
from jax import lax
from jax.experimental import pallas as pl
from jax.experimental.pallas import tpu as pltpu

F32 = jnp.float32
BF16 = jnp.bfloat16

HEAD_DIM = 64
RMS_EPS = 1e-6
NEG_INF = -1e30
FORCE_SCORE = 1e6
NSA_GROUPS = 4
NSA_HPG = 4
CMP_LEN = 32
CMP_STRIDE = 16
SEL_LEN = 64
SEL_TOPK = 16
WINDOW = 512
N_GATES = 3
GATE_PAD = 128
SEL_DROP = 2.0 ** 24
BF16_EXACT_INT = 256
V_ROWS_64 = 80
V_ROWS_128 = 144

LOG2E = math.log2(math.e)
Q_SCALE = HEAD_DIM ** -0.5 * LOG2E
UNDERFLOW_LOG2 = -160.0
SAFE_EXP_LOG2 = 100.0
FAR = 1 << 30
PICKED = -2.0

VMEM_LIMIT = 56 * 1024 * 1024


def _params(n_parallel, n_arbitrary=0):
    return pltpu.CompilerParams(dimension_semantics=("parallel",) * n_parallel + ("arbitrary",) * n_arbitrary,
                                vmem_limit_bytes=VMEM_LIMIT)


def _rms(x, g):
    ms = jnp.mean(x * x, axis=-1, keepdims=True)
    return x * lax.rsqrt(ms + RMS_EPS) * g


def _dot(a, b):
    return jnp.dot(a, b, preferred_element_type=F32)


def _dot_nt(a, b):
    return lax.dot_general(a, b, (((1,), (1,)), ((), ())), preferred_element_type=F32)


def _split_bf16(x):
    hi = x.astype(BF16)
    lo = (x - hi.astype(F32)).astype(BF16)
    return hi, lo


def _slope_pieces(slopes, width):
    def top_bits(x):
        return (x.view(np.uint32) & np.uint32(0xFFFF0000)).view(np.float32)

    s1 = top_bits(slopes)
    r1 = slopes - s1
    s2 = top_bits(r1)
    s3 = top_bits(r1 - s2)
    out = np.zeros((slopes.shape[0], width), np.float32)
    out[:, :6] = np.stack([s1, s2, s3, s1, s2, s3], axis=1)
    return jnp.asarray(out).astype(BF16)


def _key_position_columns(n, tk, width):
    j = jnp.arange(n) % tk
    a = (j // BF16_EXACT_INT) * BF16_EXACT_INT
    b = j % BF16_EXACT_INT
    cols = jnp.stack([a, a, a, b, b, b], axis=1).astype(BF16)
    return jnp.pad(cols, ((0, 0), (0, width - cols.shape[1])))


def _softmax_block(steps, states):
    s_all = [scores() for scores, _, _, _ in steps]
    live = {k: (states[k][0][...], states[k][1][...]) for k in sorted({k for _, _, _, k in steps})}
    for (_, kap, values, k), s_t in zip(steps, s_all):
        m_run, acc = live[k]
        m_new = jnp.maximum(m_run, jnp.max(s_t, axis=0, keepdims=True) + kap)
        p = jnp.exp2(s_t - (m_new - kap)).astype(BF16)
        live[k] = (m_new, jnp.exp2(m_run - m_new) * acc + _dot(values(), p))
    for k, (m_run, acc) in live.items():
        states[k][0][...] = m_run
        states[k][1][...] = acc


def _fixed_frame_block(steps, acc_refs, frames):
    s_all = [scores() for scores, _, _, _ in steps]
    live = {k: acc_refs[k][...] for k in sorted({k for _, _, _, k in steps})}
    for (_, kap, values, k), s_t in zip(steps, s_all):
        live[k] = live[k] + _dot(values(), jnp.exp2(s_t - (frames[k] - kap)).astype(BF16))
    for k, acc in live.items():
        acc_refs[k][...] = acc


def _chunk_loop(n_steps, step, block, groups):
    done = 0
    for group in groups:
        def body(i, carry, group=group, done=done):
            block([step(done + group * i + j) for j in range(group)])
            return carry

        n_groups = (n_steps - done) // group
        lax.fori_loop(0, n_groups, body, 0)
        done = done + group * n_groups


def _score_bound(q, k_norm2):
    q32 = q.astype(F32)
    width = q.shape[1]
    member = (lax.broadcasted_iota(jnp.int32, (width, 128), 0) // HEAD_DIM
              == lax.broadcasted_iota(jnp.int32, (width, 128), 1))
    head_norm2 = _dot((q32 * q32).astype(BF16), jnp.where(member, 1.0, 0.0).astype(BF16))
    return jnp.sqrt(jnp.max(head_norm2) * k_norm2 * 1.05)


PROJ_CHUNK = 512


def _ones_row_tail(rows, width):
    return jnp.where(lax.broadcasted_iota(jnp.int32, (rows, width), 0) == 0, 1.0, 0.0).astype(BF16)


def _even_proj_kernel(x_ref, g_ref, w_ref, o_ref, sbv_ref, dfv_ref):
    tm = x_ref.shape[0]
    xn = _rms(x_ref[...], g_ref[...]).astype(BF16)
    tail = _ones_row_tail(V_ROWS_128 - 2 * HEAD_DIM, tm)
    for c in range(w_ref.shape[1] // PROJ_CHUNK):
        sl = slice(c * PROJ_CHUNK, (c + 1) * PROJ_CHUNK)
        res = _dot(xn, w_ref[:, sl])
        if c in (0, 3):
            res = res * Q_SCALE
        o_ref[:, sl] = res.astype(BF16)
        if c == 2:
            sbv_ref[0] = res.T.astype(BF16)
        if c == 5:
            v_t = res.T.astype(BF16)
            for h in range(dfv_ref.shape[1]):
                dfv_ref[0, h, 0:2 * HEAD_DIM, :] = v_t[h * 2 * HEAD_DIM:(h + 1) * 2 * HEAD_DIM]
                dfv_ref[0, h, 2 * HEAD_DIM:V_ROWS_128, :] = tail


def even_proj(x2d, B, S, g, w_bf16, tm=512):
    T, D = x2d.shape
    n_w = w_bf16.shape[1]
    per_seq = S // tm
    assert n_w == 6 * PROJ_CHUNK
    return pl.pallas_call(
        _even_proj_kernel,
        grid=(T // tm,),
        in_specs=[pl.BlockSpec((tm, D), lambda i: (i, 0)),
                  pl.BlockSpec((1, D), lambda i: (0, 0)),
                  pl.BlockSpec((D, n_w), lambda i: (0, 0))],
        out_specs=[pl.BlockSpec((tm, n_w), lambda i: (i, 0)),
                   pl.BlockSpec((1, PROJ_CHUNK, tm), lambda i: (i // per_seq, 0, i % per_seq)),
                   pl.BlockSpec((1, 4, V_ROWS_128, tm), lambda i: (i // per_seq, 0, 0, i % per_seq))],
        out_shape=[jax.ShapeDtypeStruct((T, n_w), BF16),
                   jax.ShapeDtypeStruct((B, PROJ_CHUNK, S), BF16),
                   jax.ShapeDtypeStruct((B, 4, V_ROWS_128, S), BF16)],
        compiler_params=_params(1),
        name="even_proj",
    )(x2d, g.reshape(1, D), w_bf16)


def _odd_proj_kernel(x_ref, g_ref, w_ref, pos_ref, onehot_ref,
                     q_ref, cin_ref, ks_ref, kw_ref, vst_ref, vwt_ref, gate_ref):
    tm = x_ref.shape[0]
    G, Dh = NSA_GROUPS, HEAD_DIM
    kv = G * Dh
    xn = _rms(x_ref[...], g_ref[...]).astype(BF16)
    tail = _ones_row_tail(V_ROWS_64 - Dh, tm)
    pos = pos_ref[...]
    onehot = onehot_ref[...]
    q_width = q_ref.shape[1]
    for c in range(q_width // PROJ_CHUNK):
        sl = slice(c * PROJ_CHUNK, (c + 1) * PROJ_CHUNK)
        q_ref[:, sl] = (_dot(xn, w_ref[:, sl]) * Q_SCALE).astype(BF16)

    def pair(j):
        return _dot(xn, w_ref[:, q_width + j * 2 * kv:q_width + (j + 1) * 2 * kv])

    res = pair(0)
    for which in range(2):
        for g in range(G):
            cin_ref[which, 0, g] = res[:, which * kv + g * Dh:which * kv + (g + 1) * Dh]

    for j, k_ref, vt_ref, extra in ((1, ks_ref, vst_ref, [pos, onehot]), (2, kw_ref, vwt_ref, [pos])):
        res = pair(j)
        v_t = res[:, kv:2 * kv].T.astype(BF16)
        for g in range(G):
            k_ref[0, g] = jnp.concatenate([res[:, g * Dh:(g + 1) * Dh].astype(BF16)] + extra, axis=1)
            vt_ref[0, g, 0:Dh, :] = v_t[g * Dh:(g + 1) * Dh]
            vt_ref[0, g, Dh:V_ROWS_64, :] = tail

    logits = _dot(xn, w_ref[:, q_width + 6 * kv:q_width + 6 * kv + GATE_PAD])
    gate_ref[0] = jax.nn.sigmoid(logits).T[0:gate_ref.shape[1]]


def odd_proj(x2d, B, S, g, w_bf16, pos_cols, onehot, tm=512):
    T, D = x2d.shape
    G, Dh = NSA_GROUPS, HEAD_DIM
    q_width = G * NSA_HPG * Dh
    n_sel = onehot.shape[1]
    per_seq = S // tm
    seq_tile = lambda i: (i // per_seq, 0, i % per_seq, 0)
    seq_tile_t = lambda i: (i // per_seq, 0, 0, i % per_seq)
    return pl.pallas_call(
        _odd_proj_kernel,
        grid=(T // tm,),
        in_specs=[pl.BlockSpec((tm, D), lambda i: (i, 0)),
                  pl.BlockSpec((1, D), lambda i: (0, 0)),
                  pl.BlockSpec(w_bf16.shape, lambda i: (0, 0)),
                  pl.BlockSpec((tm, Dh), lambda i: (i % per_seq, 0)),
                  pl.BlockSpec((tm, n_sel), lambda i: (i % per_seq, 0))],
        out_specs=[pl.BlockSpec((tm, q_width), lambda i: (i, 0)),
                   pl.BlockSpec((2, 1, G, tm, Dh), lambda i: (0, i // per_seq, 0, i % per_seq, 0)),
                   pl.BlockSpec((1, G, tm, 2 * Dh + n_sel), seq_tile),
                   pl.BlockSpec((1, G, tm, 2 * Dh), seq_tile),
                   pl.BlockSpec((1, G, V_ROWS_64, tm), seq_tile_t),
                   pl.BlockSpec((1, G, V_ROWS_64, tm), seq_tile_t),
                   pl.BlockSpec((1, 4 * NSA_HPG * G, tm), lambda i: (i // per_seq, 0, i % per_seq))],
        out_shape=[jax.ShapeDtypeStruct((T, q_width), BF16),
                   jax.ShapeDtypeStruct((2, B, G, S, Dh), F32),
                   jax.ShapeDtypeStruct((B, G, S, 2 * Dh + n_sel), BF16),
                   jax.ShapeDtypeStruct((B, G, S, 2 * Dh), BF16),
                   jax.ShapeDtypeStruct((B, G, V_ROWS_64, S), BF16),
                   jax.ShapeDtypeStruct((B, G, V_ROWS_64, S), BF16),
                   jax.ShapeDtypeStruct((B, 4 * NSA_HPG * G, S), F32)],
        compiler_params=_params(1),
        name="odd_proj",
    )(x2d, g.reshape(1, D), w_bf16, pos_cols, onehot)


def _head_pair_rows(q, t):
    lane = lax.broadcasted_iota(jnp.int32, (t, 2 * HEAD_DIM), 1)
    zero = jnp.zeros_like(q)
    return jnp.where(lane < HEAD_DIM, q, zero), jnp.where(lane >= HEAD_DIM, q, zero)


def _sb_kernel(q_ref, k_ref, vt_ref, o_ref, acc_ref, carry_ref, *, t):
    qi = pl.program_id(2)
    cols = 2 * t
    q_both = jnp.concatenate(_head_pair_rows(q_ref[0], t), axis=0)
    s_idx = lax.broadcasted_iota(jnp.int32, (t, 2 * t), 0)
    j_idx = lax.broadcasted_iota(jnp.int32, (t, 2 * t), 1) % t
    upper2 = jnp.where(j_idx > s_idx, 1.0, 0.0).astype(BF16)
    key = lax.broadcasted_iota(jnp.int32, (t, cols), 0)
    qry = lax.broadcasted_iota(jnp.int32, (t, cols), 1) % t
    past_diag = key < qry

    acc_ref[...] = jnp.zeros_like(acc_ref)
    carry_ref[...] = jnp.zeros_like(carry_ref)

    def chunks(specs):
        starts = [pl.multiple_of(kc * t, t) for kc, _ in specs]
        z_all = [_dot_nt(k_ref[0, pl.ds(start, t), :], q_both) for start in starts]
        carry = carry_ref[...]
        acc = acc_ref[...]
        for (_, masked), start, z in zip(specs, starts, z_all):
            drop = jnp.maximum(z, 0.0) + jnp.log2(1.0 + jnp.exp2(jnp.abs(z) * -1.0))
            log_beta = z - drop
            if masked:
                drop = jnp.where(past_diag, drop, 0.0)
            hi, lo = _split_bf16(drop)
            tail = _dot(upper2, jnp.concatenate([hi, lo], axis=0))
            w = jnp.exp2(log_beta - tail - carry)
            if masked:
                w = jnp.where(past_diag, w, 0.0)
            carry = carry + tail[0:1] + drop[0:1]
            acc = acc + _dot(vt_ref[0, :, pl.ds(start, t)], w.astype(BF16))
        carry_ref[...] = carry
        acc_ref[...] = acc
        return jnp.min(carry)

    least = lax.cond(qi > 0, lambda: chunks([(qi, True), (qi - 1, False)]), lambda: chunks([(qi, True)]))

    def cond(state):
        j, least_carry = state
        return (j < qi) & (least_carry < -UNDERFLOW_LOG2)

    def body(state):
        j, _ = state
        return j + 1, chunks([(qi - 1 - j, False)])

    lax.while_loop(cond, body, (1, least))
    row = lax.broadcasted_iota(jnp.int32, (2 * HEAD_DIM, t), 0)
    o_t = jnp.where(row < HEAD_DIM, acc_ref[:, :t], acc_ref[:, t:])
    o_ref[0] = o_t.T.astype(o_ref.dtype)


def sb_attention(proj, v_t, t=256):
    B, S, _ = proj.shape
    n_pairs = 4
    return pl.pallas_call(
        functools.partial(_sb_kernel, t=t),
        grid=(B, n_pairs, S // t),
        in_specs=[pl.BlockSpec((1, t, 128), lambda b, p, i: (b, i, p)),
                  pl.BlockSpec((1, S, 128), lambda b, p, i: (b, 0, n_pairs + p)),
                  pl.BlockSpec((1, 128, S), lambda b, p, i: (b, p, 0))],
        out_specs=pl.BlockSpec((1, t, 128), lambda b, p, i: (b, i, p)),
        out_shape=jax.ShapeDtypeStruct((B, S, n_pairs * 128), BF16),
        scratch_shapes=[pltpu.VMEM((128, 2 * t), F32), pltpu.VMEM((1, 2 * t), F32)],
        compiler_params=_params(3),
        name="sb_attention",
    )(proj, proj, v_t)


def _diff_kernel(slopes_ref, lam_ref, q_ref, k_ref, vt_ref, pos_ref, sl_ref, subln_ref, sb_ref, o_ref,
                 m_ref, acc_ref, offset_ref, knorm_ref, *, t, out_scale, lambda_init):
    h = pl.program_id(1)
    qi = pl.program_id(2)
    slope = slopes_ref[h]
    cols = 2 * t
    slope_cols = jnp.broadcast_to(sl_ref[0], (t, 128))
    q_parts = _head_pair_rows(q_ref[0], t)
    q_both = jnp.concatenate([jnp.concatenate([qc, slope_cols], axis=1) for qc in q_parts],
                             axis=0)
    lane = lax.broadcasted_iota(jnp.int32, (t, 2 * HEAD_DIM), 1)

    @pl.when(qi == 0)
    def _():
        def body(c, best):
            k = k_ref[0, pl.ds(pl.multiple_of(c * t, t), t), :].astype(F32)
            k2 = k * k
            return (jnp.maximum(best[0], jnp.max(jnp.sum(jnp.where(lane < HEAD_DIM, k2, 0.0), axis=1))),
                    jnp.maximum(best[1], jnp.max(jnp.sum(jnp.where(lane >= HEAD_DIM, k2, 0.0), axis=1))))

        best = lax.fori_loop(0, k_ref.shape[1] // t, body, (jnp.float32(0.0), jnp.float32(0.0)))
        knorm_ref[0] = best[0]
        knorm_ref[1] = best[1]
        offset_ref[...] = (lax.broadcasted_iota(jnp.int32, (t, cols), 0)
                           - lax.broadcasted_iota(jnp.int32, (t, cols), 1) % t)

    qk_bound = _score_bound(q_ref[0], jnp.maximum(knorm_ref[0], knorm_ref[1]))
    safe = 2.0 * qk_bound <= SAFE_EXP_LOG2
    col = lax.broadcasted_iota(jnp.int32, (1, cols), 1)

    pos = pos_ref[...]

    def step(kc, diagonal=False):
        start = pl.multiple_of(kc * t, t)

        def scores():
            s_t = _dot_nt(jnp.concatenate([k_ref[0, pl.ds(start, t), :], pos], axis=1), q_both)
            return jnp.where(offset_ref[...] <= 0, s_t, NEG_INF) if diagonal else s_t

        return scores, slope * ((kc - qi) * t).astype(F32), lambda: vt_ref[0, 0, :, pl.ds(start, t)], 0

    acc_ref[...] = jnp.zeros_like(acc_ref)

    @pl.when(safe)
    def _():
        frame = qk_bound + slope * (col % t).astype(F32)
        block = lambda steps: _fixed_frame_block(steps, [acc_ref], [frame])
        block([step(qi, True)])
        n_back = jnp.ceil((slope * (t - 1) - UNDERFLOW_LOG2) / (slope * t)).astype(jnp.int32)
        _chunk_loop(jnp.minimum(n_back, qi), lambda i: step(qi - 1 - i), block, (8, 4, 2, 1))

    @pl.when(jnp.logical_not(safe))
    def _():
        m_ref[...] = jnp.full_like(m_ref, NEG_INF)
        block = lambda steps: _softmax_block(steps, [(m_ref, acc_ref)])
        block([step(qi, True)])
        excess = jnp.max(qk_bound - m_ref[...]) + slope * (t - 1)
        n_back = jnp.clip(jnp.ceil((excess - UNDERFLOW_LOG2) / (slope * t)).astype(jnp.int32), 0, qi)
        _chunk_loop(n_back, lambda i: step(qi - 1 - i), block, (4, 2, 1))

    lam_terms = lam_ref[...]
    lam = (jnp.exp(jnp.sum(lam_terms[0:1] * lam_terms[1:2], axis=1, keepdims=True))
           - jnp.exp(jnp.sum(lam_terms[2:3] * lam_terms[3:4], axis=1, keepdims=True))
           + lambda_init)
    o_t = acc_ref[0:128, :] / acc_ref[128:129, :]
    o = (o_t[:, :t] - lam * o_t[:, t:]).T
    o_ref[0, :, 0:128] = sb_ref[0]
    o_ref[0, :, 128:256] = (_rms(o, subln_ref[...]) * out_scale).astype(o_ref.dtype)


def diff_attention(proj, vt_aug, lam_rows, subln, slopes, layer, o_sb, t=512):
    B, S, _ = proj.shape
    n_heads = 4
    lambda_init = 0.8 - 0.6 * math.exp(-0.3 * layer)
    smem = pl.BlockSpec(memory_space=pltpu.SMEM)
    pos = _key_position_columns(t, t, 128)
    slope_cols = _slope_pieces(slopes, 128).reshape(n_heads, 1, 128)
    return pl.pallas_call(
        functools.partial(_diff_kernel, t=t, out_scale=1.0 - lambda_init, lambda_init=lambda_init),
        grid=(B, n_heads, S // t),
        in_specs=[smem,
                  pl.BlockSpec((4, HEAD_DIM), lambda b, h, i: (0, 0)),
                  pl.BlockSpec((1, t, 128), lambda b, h, i: (b, i, 12 + h)),
                  pl.BlockSpec((1, S, 128), lambda b, h, i: (b, 0, 16 + h)),
                  pl.BlockSpec((1, 1, V_ROWS_128, S), lambda b, h, i: (b, h, 0, 0)),
                  pl.BlockSpec((t, 128), lambda b, h, i: (0, 0)),
                  pl.BlockSpec((1, 1, 128), lambda b, h, i: (h, 0, 0)),
                  pl.BlockSpec((1, 128), lambda b, h, i: (0, 0)),
                  pl.BlockSpec((1, t, 128), lambda b, h, i: (b, i, h))],
        out_specs=pl.BlockSpec((1, t, 256), lambda b, h, i: (b, i, h)),
        out_shape=jax.ShapeDtypeStruct((B, S, n_heads * 256), BF16),
        scratch_shapes=[pltpu.VMEM((1, 2 * t), F32), pltpu.VMEM((V_ROWS_128, 2 * t), F32),
                        pltpu.VMEM((t, 2 * t), jnp.int32), pltpu.SMEM((2,), F32)],
        compiler_params=_params(2, 1),
        name="diff_attention",
    )(slopes, lam_rows, proj, proj, vt_aug, pos, slope_cols, subln.reshape(1, 128), o_sb)


def _post_kernel(*refs, ff_chunk, final):
    mix_ref, x_ref, wo_ref, g_ref, w1_ref, w2_ref = refs[:6]
    gf_ref = refs[6] if final else None
    o_ref = refs[-1]
    x = x_ref[...] + _dot(mix_ref[...], wo_ref[...])
    hn = _rms(x, g_ref[...]).astype(BF16)
    acc = x
    for f in range(w1_ref.shape[1] // ff_chunk):
        sl = slice(f * ff_chunk, (f + 1) * ff_chunk)
        hid = jnp.maximum(_dot(hn, w1_ref[:, sl]), 0.0)
        acc = acc + _dot((hid * hid).astype(BF16), w2_ref[sl, :])
    if final:
        acc = _rms(acc, gf_ref[...])
    o_ref[...] = acc


def post_block(mix, x2d, w_out, g_mlp, w1, w2, g_final=None, tm=512, ff_chunk=1024):
    T, D = x2d.shape
    final = g_final is not None
    const = lambda i: (0, 0)
    in_specs = [pl.BlockSpec((tm, mix.shape[1]), lambda i: (i, 0)),
                pl.BlockSpec((tm, D), lambda i: (i, 0)),
                pl.BlockSpec(w_out.shape, const),
                pl.BlockSpec((1, D), const), pl.BlockSpec(w1.shape, const), pl.BlockSpec(w2.shape, const)]
    args = [mix, x2d, w_out, g_mlp.reshape(1, D), w1, w2]
    if final:
        in_specs.append(pl.BlockSpec((1, D), const))
        args.append(g_final.reshape(1, D))
    return pl.pallas_call(
        functools.partial(_post_kernel, ff_chunk=ff_chunk, final=final),
        grid=(T // tm,),
        in_specs=in_specs,
        out_specs=pl.BlockSpec((tm, D), lambda i: (i, 0)),
        out_shape=jax.ShapeDtypeStruct((T, D), F32),
        compiler_params=_params(1),
        name="post_block",
    )(*args)


def _compress_kernel(c_ref, pos_ref, w1_ref, w2_ref, o_ref):
    half = w1_ref.shape[1] // 2
    n_chunks = c_ref.shape[3] // CMP_STRIDE
    first = second = None
    for l in range(CMP_STRIDE):
        tok = c_ref[0, 0, 0, pl.ds(l, n_chunks, stride=CMP_STRIDE), :].astype(BF16)
        a = _dot(tok, w1_ref[0, l * HEAD_DIM:(l + 1) * HEAD_DIM, :])
        b = _dot(tok, w1_ref[0, half + l * HEAD_DIM:half + (l + 1) * HEAD_DIM, :])
        first = a if first is None else first + a
        second = b if second is None else second + b
    pos = jnp.broadcast_to(pos_ref[0], (8, 2 * half)).astype(BF16)
    pre = first + pltpu.roll(second, n_chunks - 1, 0) + _dot(pos, w1_ref[0])[0:1]
    hid = jax.nn.gelu(pre)
    o_ref[0, 0] = _dot(hid.astype(BF16), w2_ref[0]).astype(o_ref.dtype)


def compress_kv(tokens, pos_flat, w1, w2):
    _, B, G, S, Dh = tokens.shape
    n_chunks = S // CMP_STRIDE
    hidden = w1.shape[-1]
    return pl.pallas_call(
        _compress_kernel,
        grid=(2, B * G),
        in_specs=[pl.BlockSpec((1, 1, 1, S, Dh), lambda s, i: (s, i // G, i % G, 0, 0)),
                  pl.BlockSpec((1, 1, CMP_LEN * Dh), lambda s, i: (s, 0, 0)),
                  pl.BlockSpec((1, CMP_LEN * Dh, hidden), lambda s, i: (s, 0, 0)),
                  pl.BlockSpec((1, hidden, Dh), lambda s, i: (s, 0, 0))],
        out_specs=pl.BlockSpec((1, 1, n_chunks, Dh), lambda s, i: (s, i, 0, 0)),
        out_shape=jax.ShapeDtypeStruct((2, B * G, n_chunks, Dh), BF16),
        compiler_params=_params(2),
        name="compress_kv",
    )(tokens, pos_flat, w1, w2)


def _stack_heads(q):
    return jnp.concatenate([q[:, r * HEAD_DIM:(r + 1) * HEAD_DIM] for r in range(NSA_HPG)], axis=0)


def _slope_row(slopes_ref, g, tq):
    col = lax.broadcasted_iota(jnp.int32, (1, NSA_HPG * tq), 1)
    out = jnp.zeros((1, NSA_HPG * tq), F32)
    for r in range(NSA_HPG):
        out = jnp.where(col // tq == r, slopes_ref[g * NSA_HPG + r], out)
    return out


def _cmp_select_kernel(slopes_ref, q_ref, kc_ref, vct_ref, ovt_ref, ocmp_ref, sel_ref, hits_ref,
                       keep_ref, *, tq, n_sel, blocks_per_chunk, hit_tile):
    g = pl.program_id(1)
    t0 = pl.program_id(2) * tq
    cols = NSA_HPG * tq
    n_cmp = kc_ref.shape[2]
    slope_row = _slope_row(slopes_ref, g, tq)
    q_rows = _stack_heads(q_ref[0])

    def group_sum(pc):
        out = pc[:, 0:tq]
        for r in range(1, NSA_HPG):
            out = out + pc[:, r * tq:(r + 1) * tq]
        return out

    def compressed_branch(n_rows, n_blocks):
        tpos = t0 + lax.broadcasted_iota(jnp.int32, (n_rows, cols), 1) % tq
        cmp_end = lax.broadcasted_iota(jnp.int32, (n_rows, cols), 0) * CMP_STRIDE + (CMP_LEN - 1)
        dc = (tpos - cmp_end).astype(F32)
        sc = jnp.where(dc >= 0, _dot_nt(kc_ref[0, 0, 0:n_rows, :], q_rows) - slope_row * dc, NEG_INF)
        e = jnp.exp2(sc - jnp.max(sc, axis=0, keepdims=True))
        any_valid = jnp.where(dc[0:1] >= 0, 1.0, 0.0)
        pc = e * (any_valid / jnp.sum(e, axis=0, keepdims=True))
        o_t = _dot(vct_ref[0, 0, :, 0:n_rows], pc.astype(BF16))
        for r in range(NSA_HPG):
            ocmp_ref[0, 0, r] = o_t[:, r * tq:(r + 1) * tq]
        hi, lo = _split_bf16(group_sum(pc))
        return _dot(ovt_ref[0:n_blocks, 0:n_rows], hi) + _dot(ovt_ref[0:n_blocks, 0:n_rows], lo)

    def select_blocks(imp):
        n_blocks = imp.shape[0]
        blk = lax.broadcasted_iota(jnp.int32, (n_blocks, tq), 0)
        cur = (t0 + lax.broadcasted_iota(jnp.int32, (n_blocks, tq), 1)) // SEL_LEN
        forced = (blk == 0) | (blk == cur) | (blk == cur - 1)
        imp = jnp.where(blk <= cur, imp, -1.0)
        topk = min(SEL_TOPK, n_sel)

        def pick(imp, count):
            for _ in range(count):
                best = jnp.max(imp, axis=0, keepdims=True)
                first = jnp.min(jnp.where(imp == best, blk, n_blocks), axis=0, keepdims=True)
                imp = jnp.where(blk == first, PICKED, imp)
            return imp

        imp = lax.cond(t0 >= 2 * SEL_LEN,
                       lambda: pick(jnp.where(forced, PICKED, imp), topk - 3),
                       lambda: pick(jnp.where(forced, FORCE_SCORE, imp), topk))
        return jnp.where((imp == PICKED) & (blk <= cur), 1.0, 0.0)

    needed = jnp.clip(((t0 + tq - CMP_LEN) // CMP_STRIDE) // (n_cmp // 4) + 1, 1, 4)
    for count in range(1, 5):
        @pl.when(needed == count)
        def _():
            n_blocks = count * (n_sel // 4)
            keep_ref[0:n_blocks, :] = select_blocks(compressed_branch(count * (n_cmp // 4), n_blocks))
            if count < 4:
                keep_ref[n_blocks:n_sel, :] = jnp.zeros((n_sel - n_blocks, tq), F32)

    keep = keep_ref[...] > 0.5
    sel_ref[0, 0] = jnp.where(keep, 0.0, -SEL_DROP).T.astype(sel_ref.dtype)
    n_chunks = n_sel // blocks_per_chunk
    member = (lax.broadcasted_iota(jnp.int32, (n_chunks, n_sel), 1) // blocks_per_chunk
              == lax.broadcasted_iota(jnp.int32, (n_chunks, n_sel), 0))
    per_query = _dot(jnp.where(member, 1.0, 0.0).astype(BF16),
                     jnp.where(keep, 1.0, 0.0).astype(BF16)).astype(BF16)
    for part in range(tq // hit_tile):
        hits_ref[0, 0, part] = _dot(per_query[:, part * hit_tile:(part + 1) * hit_tile],
                                    jnp.ones((hit_tile, 128), BF16))


def cmp_select(q, kc, vc_t, overlap_t, slopes, tk, hit_tile, tq=256):
    B, S, _ = q.shape
    n_cmp = kc.shape[2]
    n_sel = S // SEL_LEN
    G = NSA_GROUPS
    smem = pl.BlockSpec(memory_space=pltpu.SMEM)
    return pl.pallas_call(
        functools.partial(_cmp_select_kernel, tq=tq, n_sel=n_sel, blocks_per_chunk=tk // SEL_LEN,
                          hit_tile=hit_tile),
        grid=(B, G, S // tq),
        in_specs=[smem,
                  pl.BlockSpec((1, tq, 256), lambda b, g, i: (b, i, g)),
                  pl.BlockSpec((1, 1, n_cmp, HEAD_DIM), lambda b, g, i: (b, g, 0, 0)),
                  pl.BlockSpec((1, 1, HEAD_DIM, n_cmp), lambda b, g, i: (b, g, 0, 0)),
                  pl.BlockSpec((n_sel, n_cmp), lambda b, g, i: (0, 0))],
        out_specs=[pl.BlockSpec((1, 1, NSA_HPG, HEAD_DIM, tq), lambda b, g, i: (b, g, 0, 0, i)),
                   pl.BlockSpec((1, 1, tq, n_sel), lambda b, g, i: (b, g, i, 0)),
                   pl.BlockSpec((1, 1, tq // hit_tile, S // tk, 128), lambda b, g, i: (b, g, i, 0, 0))],
        out_shape=[jax.ShapeDtypeStruct((B, G, NSA_HPG, HEAD_DIM, S), F32),
                   jax.ShapeDtypeStruct((B, G, S, n_sel), BF16),
                   jax.ShapeDtypeStruct((B, G, S // hit_tile, S // tk, 128), F32)],
        scratch_shapes=[pltpu.VMEM((n_sel, tq), F32)],
        compiler_params=_params(3),
        name="cmp_select",
    )(slopes, q, kc, vc_t, overlap_t)


def _sel_win_kernel(slopes_ref, active_ref, q_ref, sl_ref, ks_ref, vst_ref, kw_ref, vwt_ref, sel_ref,
                    ocmp_ref, gate_ref, o_ref, ms_ref, accs_ref, mw_ref, accw_ref, offset_ref, todo_ref,
                    knorm_ref, *, tq, tk):
    g = pl.program_id(1)
    t0 = pl.program_id(2) * tq
    cols = NSA_HPG * tq
    slope_row = _slope_row(slopes_ref, g, tq)
    q = q_ref[0]
    q_aug = jnp.concatenate(
        [jnp.concatenate([q[:, r * HEAD_DIM:(r + 1) * HEAD_DIM],
                          jnp.broadcast_to(sl_ref[0, r:r + 1, :], (tq, HEAD_DIM))], axis=1)
         for r in range(NSA_HPG)], axis=0)
    q_sel = jnp.concatenate([q_aug, jnp.concatenate([sel_ref[0, 0]] * NSA_HPG, axis=0)], axis=1)

    @pl.when(pl.program_id(2) == 0)
    def _():
        offset_ref[...] = (lax.broadcasted_iota(jnp.int32, (tk, cols), 0)
                           - lax.broadcasted_iota(jnp.int32, (tk, cols), 1) % tq)

        def body(c, best):
            rows = pl.ds(pl.multiple_of(c * tk, tk), tk)
            ks = ks_ref[0, 0, rows, :][:, :HEAD_DIM].astype(F32)
            kw = kw_ref[0, 0, rows, :][:, :HEAD_DIM].astype(F32)
            return (jnp.maximum(best[0], jnp.max(jnp.sum(ks * ks, axis=1))),
                    jnp.maximum(best[1], jnp.max(jnp.sum(kw * kw, axis=1))))

        best = lax.fori_loop(0, ks_ref.shape[2] // tk, body, (jnp.float32(0.0), jnp.float32(0.0)))
        knorm_ref[0] = best[0]
        knorm_ref[1] = best[1]

    sel_bound = _score_bound(q, knorm_ref[0])
    win_bound = _score_bound(q, knorm_ref[1])
    safe = 2.0 * jnp.maximum(sel_bound, win_bound) <= SAFE_EXP_LOG2

    def step(c, k_ref, q_rows, vt_ref, keep=None):
        inside = c >= 0
        start = pl.multiple_of(jnp.maximum(c, 0) * tk, tk)
        shift = c * tk - t0

        def scores():
            s_t = _dot_nt(k_ref[0, 0, pl.ds(start, tk), :], q_rows)
            if keep is None:
                return s_t
            lo, hi = keep
            offset = offset_ref[...]
            if hi is not None:
                return jnp.where(offset <= jnp.where(inside, hi - shift, -FAR), s_t, NEG_INF)
            return jnp.where(offset > jnp.where(inside, lo - shift, FAR), s_t, NEG_INF)

        kappa = jnp.where(inside, slope_row * shift.astype(F32), NEG_INF)
        return scores, kappa, lambda: vt_ref[0, 0, :, pl.ds(start, tk)]

    last = t0 // tk
    causal, recent, anything = (None, 0), (-WINDOW, None), (-FAR, None)
    first_steps = [step(last - 2, kw_ref, q_aug, vwt_ref, recent) + (1,),
                   step(last, ks_ref, q_sel, vst_ref, causal) + (0,),
                   step(last - 1, kw_ref, q_aug, vwt_ref, anything) + (1,),
                   step(last, kw_ref, q_aug, vwt_ref, causal) + (1,)]

    def note_active(c, n):
        hit = active_ref[0, 0, 0, 0, c] > 0

        @pl.when(hit)
        def _():
            todo_ref[n] = c

        return n + hit.astype(jnp.int32)

    n_todo = lax.fori_loop(0, last, note_active, 0)
    sel_step = lambda i: step(todo_ref[i], ks_ref, q_sel, vst_ref) + (0,)

    accs_ref[...] = jnp.zeros_like(accs_ref)
    accw_ref[...] = jnp.zeros_like(accw_ref)

    @pl.when(safe)
    def _():
        in_tile = slope_row * (lax.broadcasted_iota(jnp.int32, (1, cols), 1) % tq).astype(F32)
        block = lambda steps: _fixed_frame_block(steps, [accs_ref, accw_ref],
                                                 [sel_bound + in_tile, win_bound + in_tile])
        block(first_steps)
        _chunk_loop(n_todo, sel_step, block, (8, 4, 2, 1))

    @pl.when(jnp.logical_not(safe))
    def _():
        states = [(ms_ref, accs_ref), (mw_ref, accw_ref)]
        for m_ref, _ in states:
            m_ref[...] = jnp.full_like(m_ref, NEG_INF)
        block = lambda steps: _softmax_block(steps, states)
        block(first_steps)
        _chunk_loop(n_todo, sel_step, block, (4, 2, 1))

    o_sel = accs_ref[0:HEAD_DIM, :] / accs_ref[HEAD_DIM:HEAD_DIM + 1, :]
    o_win = accw_ref[0:HEAD_DIM, :] / accw_ref[HEAD_DIM:HEAD_DIM + 1, :]

    gates = gate_ref[0]
    outs = []
    for r in range(NSA_HPG):
        cs = slice(r * tq, (r + 1) * tq)
        outs.append(gates[3 * r:3 * r + 1] * ocmp_ref[0, 0, r] + gates[3 * r + 1:3 * r + 2] * o_sel[:, cs]
                    + gates[3 * r + 2:3 * r + 3] * o_win[:, cs])
    o_ref[0] = jnp.concatenate(outs, axis=0).T.astype(o_ref.dtype)


def sel_win_attention(q, active, slope_cols, ks, vs_t, kw, vw_t, sel_bias, o_cmp, gates_t,
                      slopes, tq=256, tk=256):
    B, S, _ = q.shape
    assert tq == tk and WINDOW == 2 * tk
    G = NSA_GROUPS
    n_sel = S // SEL_LEN
    cols = NSA_HPG * tq
    smem = pl.BlockSpec(memory_space=pltpu.SMEM)
    ks_spec = pl.BlockSpec((1, 1, S, ks.shape[-1]), lambda b, g, i: (b, g, 0, 0))
    kw_spec = pl.BlockSpec((1, 1, S, kw.shape[-1]), lambda b, g, i: (b, g, 0, 0))
    v_spec = pl.BlockSpec((1, 1, V_ROWS_64, S), lambda b, g, i: (b, g, 0, 0))
    return pl.pallas_call(
        functools.partial(_sel_win_kernel, tq=tq, tk=tk),
        grid=(B, G, S // tq),
        in_specs=[smem,
                  pl.BlockSpec((1, 1, 1, 1, S // tk), lambda b, g, i: (b, g, i, 0, 0),
                               memory_space=pltpu.SMEM),
                  pl.BlockSpec((1, tq, 256), lambda b, g, i: (b, i, g)),
                  pl.BlockSpec((1, NSA_HPG, HEAD_DIM), lambda b, g, i: (g, 0, 0)),
                  ks_spec, v_spec, kw_spec, v_spec,
                  pl.BlockSpec((1, 1, tq, n_sel), lambda b, g, i: (b, g, i, 0)),
                  pl.BlockSpec((1, 1, NSA_HPG, HEAD_DIM, tq), lambda b, g, i: (b, g, 0, 0, i)),
                  pl.BlockSpec((1, 4 * NSA_HPG, tq), lambda b, g, i: (b, g, i))],
        out_specs=pl.BlockSpec((1, tq, 256), lambda b, g, i: (b, i, g)),
        out_shape=jax.ShapeDtypeStruct((B, S, G * 256), BF16),
        scratch_shapes=[pltpu.VMEM((1, cols), F32), pltpu.VMEM((V_ROWS_64, cols), F32),
                        pltpu.VMEM((1, cols), F32), pltpu.VMEM((V_ROWS_64, cols), F32),
                        pltpu.VMEM((tk, cols), jnp.int32),
                        pltpu.SMEM((S // tk,), jnp.int32), pltpu.SMEM((2,), F32)],
        compiler_params=_params(2, 1),
        name="sel_win_attention",
    )(slopes, active, q, slope_cols, ks, vs_t, kw, vw_t, sel_bias, o_cmp, gates_t)


def _alibi_slopes_log2(n_heads):
    slopes = np.exp2(-8.0 * (np.arange(n_heads, dtype=np.float32) + 1.0) / n_heads)
    return (slopes.astype(np.float32) * np.float32(LOG2E)).astype(np.float32)


def even_layer_mix(x2d, B, S, norm_g, w_in, lam_q1, lam_k1, lam_q2, lam_k2, subln, layer):
    proj, sb_vt, df_vt = even_proj(x2d, B, S, norm_g, w_in.astype(BF16))
    proj = proj.reshape(B, S, -1)
    lam_rows = jnp.stack([lam_q1, lam_k1, lam_q2, lam_k2]).astype(F32)
    mix = diff_attention(proj, df_vt, lam_rows, subln.astype(F32), _alibi_slopes_log2(4), layer,
                         sb_attention(proj, sb_vt))
    return mix.reshape(B * S, -1)


def even_w_out_rows(w_out):
    d = w_out.shape[1]
    return w_out.reshape(2, 4, 128, d).transpose(1, 0, 2, 3).reshape(-1, d)


def odd_layer_mix(x2d, B, S, norm_g, w_in, pos_k, k_w1, k_w2, pos_v, v_w1, v_w2, tq=256, tk=256):
    G, Dh = NSA_GROUPS, HEAD_DIM
    q_width = G * NSA_HPG * Dh
    n_main = q_width + 6 * G * Dh
    per_group = NSA_HPG * N_GATES
    w_gate = w_in[:, n_main:n_main + G * per_group].reshape(-1, G, per_group)
    w_gate = jnp.pad(w_gate, ((0, 0), (0, 0), (0, 4 * NSA_HPG - per_group))).reshape(-1, 4 * NSA_HPG * G)
    w_gate = jnp.pad(w_gate, ((0, 0), (0, GATE_PAD - w_gate.shape[1])))
    w_all = jnp.concatenate([w_in[:, :n_main], w_gate], axis=1).astype(BF16)

    n_chunks = S // CMP_STRIDE
    n_sel = S // SEL_LEN
    onehot = (jnp.arange(S)[:, None] // SEL_LEN == jnp.arange(n_sel)[None, :]).astype(BF16)
    q, cmp_in, ks, kw, vs_t, vw_t, gates_t = odd_proj(x2d, B, S, norm_g, w_all,
                                                      _key_position_columns(S, tk, Dh), onehot)
    q = q.reshape(B, S, q_width)

    pos_flat = jnp.stack([pos_k, pos_v]).reshape(2, 1, CMP_LEN * Dh).astype(F32)
    w1 = jnp.stack([k_w1, v_w1]).astype(BF16)
    w2 = jnp.stack([k_w2, v_w2]).astype(BF16)
    cmp = compress_kv(cmp_in, pos_flat, w1, w2).reshape(2, B, G, n_chunks, Dh)

    cmp_start = jnp.arange(n_chunks) * CMP_STRIDE
    sel_start = jnp.arange(n_sel) * SEL_LEN
    overlap_t = ((cmp_start[None, :] < sel_start[:, None] + SEL_LEN)
                 & (sel_start[:, None] <= cmp_start[None, :] + CMP_LEN - 1)).astype(BF16)
    slopes = _alibi_slopes_log2(G * NSA_HPG)
    slope_cols = _slope_pieces(slopes, Dh).reshape(G, NSA_HPG, Dh)

    o_cmp, sel_bias, hits = cmp_select(q, cmp[0], cmp[1].transpose(0, 1, 3, 2), overlap_t, slopes, tk, tq)
    active = (hits[..., 0] > 0).astype(jnp.int32)[:, :, :, None, :]
    o = sel_win_attention(q, active, slope_cols, ks, vs_t, kw, vw_t, sel_bias, o_cmp, gates_t, slopes,
                          tq=tq, tk=tk)
    return o.reshape(B * S, q_width)


def kernel(x, attn_norm, mlp_norm, final_norm, ev_w_in, ev_lam_q1, ev_lam_k1, ev_lam_q2, ev_lam_k2,
           ev_subln, ev_w_out, od_w_in, od_cmp_pos_k, od_cmp_k_w1, od_cmp_k_w2, od_cmp_pos_v,
           od_cmp_v_w1, od_cmp_v_w2, od_w_out, mlp_w1, mlp_w2):
    B, S, D = x.shape
    depth = attn_norm.shape[0]
    x2d = x.reshape(B * S, D)
    for layer in range(depth):
        idx = layer // 2
        if layer % 2 == 0:
            mix = even_layer_mix(x2d, B, S, attn_norm[layer], ev_w_in[idx], ev_lam_q1[idx],
                                 ev_lam_k1[idx], ev_lam_q2[idx], ev_lam_k2[idx], ev_subln[idx], layer)
            w_out = even_w_out_rows(ev_w_out[idx])
        else:
            mix = odd_layer_mix(x2d, B, S, attn_norm[layer], od_w_in[idx], od_cmp_pos_k[idx],
                                od_cmp_k_w1[idx], od_cmp_k_w2[idx], od_cmp_pos_v[idx],
                                od_cmp_v_w1[idx], od_cmp_v_w2[idx])
            w_out = od_w_out[idx]
        g_final = final_norm if layer == depth - 1 else None
        x2d = post_block(mix, x2d, w_out.astype(BF16), mlp_norm[layer], mlp_w1[layer].astype(BF16),
                         mlp_w2[layer].astype(BF16), g_final)
    return x2d.reshape(B, S, D)
```

```python
import functools
import math

import jax
import jax.numpy as jnp
import numpy as np
from jax import lax
from jax.experimental import pallas as pl
from jax.experimental.pallas import tpu as pltpu

F32 = jnp.float32
BF16 = jnp.bfloat16

HEAD_DIM = 64
RMS_EPS = 1e-6
NEG_INF = -1e30
FORCE_SCORE = 1e6
NSA_GROUPS = 4
NSA_HPG = 4
CMP_LEN = 32
CMP_STRIDE = 16
SEL_LEN = 64
SEL_TOPK = 16
WINDOW = 512
N_GATES = 3
GATE_PAD = 128
SEL_DROP = 2.0 ** 24
BF16_EXACT_INT = 256
V_ROWS_64 = 80
V_ROWS_128 = 144

LOG2E = math.log2(math.e)
Q_SCALE = HEAD_DIM ** -0.5 * LOG2E
UNDERFLOW_LOG2 = -160.0
SAFE_EXP_LOG2 = 100.0
FAR = 1 << 30
PICKED = -2.0

VMEM_LIMIT = 56 * 1024 * 1024


def _params(n_parallel, n_arbitrary=0):
    return pltpu.CompilerParams(dimension_semantics=("parallel",) * n_parallel + ("arbitrary",) * n_arbitrary,
                                vmem_limit_bytes=VMEM_LIMIT)


def _rms(x, g):
    ms = jnp.mean(x * x, axis=-1, keepdims=True)
    return x * lax.rsqrt(ms + RMS_EPS) * g


def _dot(a, b):
    return jnp.dot(a, b, preferred_element_type=F32)


def _dot_nt(a, b):
    return lax.dot_general(a, b, (((1,), (1,)), ((), ())), preferred_element_type=F32)


def _split_bf16(x):
    hi = x.astype(BF16)
    lo = (x - hi.astype(F32)).astype(BF16)
    return hi, lo


def _slope_pieces(slopes, width):
    def top_bits(x):
        return (x.view(np.uint32) & np.uint32(0xFFFF0000)).view(np.float32)

    s1 = top_bits(slopes)
    r1 = slopes - s1
    s2 = top_bits(r1)
    s3 = top_bits(r1 - s2)
    out = np.zeros((slopes.shape[0], width), np.float32)
    out[:, :6] = np.stack([s1, s2, s3, s1, s2, s3], axis=1)
    return jnp.asarray(out).astype(BF16)


def _key_position_columns(n, tk, width):
    j = jnp.arange(n) % tk
    a = (j // BF16_EXACT_INT) * BF16_EXACT_INT
    b = j % BF16_EXACT_INT
    cols = jnp.stack([a, a, a, b, b, b], axis=1).astype(BF16)
    return jnp.pad(cols, ((0, 0), (0, width - cols.shape[1])))


def _softmax_block(steps, states):
    s_all = [scores() for scores, _, _, _ in steps]
    live = {k: (states[k][0][...], states[k][1][...]) for k in sorted({k for _, _, _, k in steps})}
    for (_, kap, values, k), s_t in zip(steps, s_all):
        m_run, acc = live[k]
        m_new = jnp.maximum(m_run, jnp.max(s_t, axis=0, keepdims=True) + kap)
        p = jnp.exp2(s_t - (m_new - kap)).astype(BF16)
        live[k] = (m_new, jnp.exp2(m_run - m_new) * acc + _dot(values(), p))
    for k, (m_run, acc) in live.items():
        states[k][0][...] = m_run
        states[k][1][...] = acc


def _fixed_frame_block(steps, acc_refs, frames):
    s_all = [scores() for scores, _, _, _ in steps]
    live = {k: acc_refs[k][...] for k in sorted({k for _, _, _, k in steps})}
    for (_, kap, values, k), s_t in zip(steps, s_all):
        live[k] = live[k] + _dot(values(), jnp.exp2(s_t - (frames[k] - kap)).astype(BF16))
    for k, acc in live.items():
        acc_refs[k][...] = acc


def _chunk_loop(n_steps, step, block, groups):
    done = 0
    for group in groups:
        def body(i, carry, group=group, done=done):
            block([step(done + group * i + j) for j in range(group)])
            return carry

        n_groups = (n_steps - done) // group
        lax.fori_loop(0, n_groups, body, 0)
        done = done + group * n_groups


def _score_bound(q, k_norm2):
    q32 = q.astype(F32)
    width = q.shape[1]
    member = (lax.broadcasted_iota(jnp.int32, (width, 128), 0) // HEAD_DIM
              == lax.broadcasted_iota(jnp.int32, (width, 128), 1))
    head_norm2 = _dot((q32 * q32).astype(BF16), jnp.where(member, 1.0, 0.0).astype(BF16))
    return jnp.sqrt(jnp.max(head_norm2) * k_norm2 * 1.05)


PROJ_CHUNK = 512
CMP_PARTS = 8


def _ones_row_tail(rows, width):
    return jnp.where(lax.broadcasted_iota(jnp.int32, (rows, width), 0) == 0, 1.0, 0.0).astype(BF16)


def _even_proj_kernel(x_ref, g_ref, w_ref, o_ref, sbv_ref, dfv_ref):
    tm = x_ref.shape[0]
    xn = _rms(x_ref[...], g_ref[...]).astype(BF16)
    tail = _ones_row_tail(V_ROWS_128 - 2 * HEAD_DIM, tm)
    for c in range(w_ref.shape[1] // PROJ_CHUNK):
        sl = slice(c * PROJ_CHUNK, (c + 1) * PROJ_CHUNK)
        res = _dot(xn, w_ref[:, sl])
        if c in (0, 3):
            res = res * Q_SCALE
        o_ref[:, sl] = res.astype(BF16)
        if c == 2:
            sbv_ref[0] = res.T.astype(BF16)
        if c == 5:
            v_t = res.T.astype(BF16)
            for h in range(dfv_ref.shape[1]):
                dfv_ref[0, h, 0:2 * HEAD_DIM, :] = v_t[h * 2 * HEAD_DIM:(h + 1) * 2 * HEAD_DIM]
                dfv_ref[0, h, 2 * HEAD_DIM:V_ROWS_128, :] = tail


def even_proj(x2d, B, S, g, w_bf16, tm=512):
    T, D = x2d.shape
    n_w = w_bf16.shape[1]
    per_seq = S // tm
    assert n_w == 6 * PROJ_CHUNK
    return pl.pallas_call(
        _even_proj_kernel,
        grid=(T // tm,),
        in_specs=[pl.BlockSpec((tm, D), lambda i: (i, 0)),
                  pl.BlockSpec((1, D), lambda i: (0, 0)),
                  pl.BlockSpec((D, n_w), lambda i: (0, 0))],
        out_specs=[pl.BlockSpec((tm, n_w), lambda i: (i, 0)),
                   pl.BlockSpec((1, PROJ_CHUNK, tm), lambda i: (i // per_seq, 0, i % per_seq)),
                   pl.BlockSpec((1, 4, V_ROWS_128, tm), lambda i: (i // per_seq, 0, 0, i % per_seq))],
        out_shape=[jax.ShapeDtypeStruct((T, n_w), BF16),
                   jax.ShapeDtypeStruct((B, PROJ_CHUNK, S), BF16),
                   jax.ShapeDtypeStruct((B, 4, V_ROWS_128, S), BF16)],
        compiler_params=_params(1),
        name="even_proj",
    )(x2d, g.reshape(1, D), w_bf16)


def _odd_proj_kernel(x_ref, g_ref, w_ref, pos_ref, onehot_ref,
                     q_ref, cin_ref, ks_ref, kw_ref, vst_ref, vwt_ref, gate_ref):
    tm = x_ref.shape[0]
    G, Dh = NSA_GROUPS, HEAD_DIM
    kv = G * Dh
    xn = _rms(x_ref[...], g_ref[...]).astype(BF16)
    tail = _ones_row_tail(V_ROWS_64 - Dh, tm)
    pos = pos_ref[...]
    onehot = onehot_ref[...]
    q_width = q_ref.shape[1]
    for c in range(q_width // PROJ_CHUNK):
        sl = slice(c * PROJ_CHUNK, (c + 1) * PROJ_CHUNK)
        q_ref[:, sl] = (_dot(xn, w_ref[:, sl]) * Q_SCALE).astype(BF16)

    def pair(j):
        return _dot(xn, w_ref[:, q_width + j * 2 * kv:q_width + (j + 1) * 2 * kv])

    res = pair(0)
    for which in range(2):
        for g in range(G):
            cin_ref[which, 0, g] = res[:, which * kv + g * Dh:which * kv + (g + 1) * Dh]

    for j, k_ref, vt_ref, extra in ((1, ks_ref, vst_ref, [pos, onehot]), (2, kw_ref, vwt_ref, [pos])):
        res = pair(j)
        v_t = res[:, kv:2 * kv].T.astype(BF16)
        for g in range(G):
            k_ref[0, g] = jnp.concatenate([res[:, g * Dh:(g + 1) * Dh].astype(BF16)] + extra, axis=1)
            vt_ref[0, g, 0:Dh, :] = v_t[g * Dh:(g + 1) * Dh]
            vt_ref[0, g, Dh:V_ROWS_64, :] = tail

    logits = _dot(xn, w_ref[:, q_width + 6 * kv:q_width + 6 * kv + GATE_PAD])
    gate_ref[0] = jax.nn.sigmoid(logits).T[0:gate_ref.shape[1]]


def odd_proj(x2d, B, S, g, w_bf16, pos_cols, onehot, tm=512):
    T, D = x2d.shape
    G, Dh = NSA_GROUPS, HEAD_DIM
    q_width = G * NSA_HPG * Dh
    n_sel = onehot.shape[1]
    per_seq = S // tm
    seq_tile = lambda i: (i // per_seq, 0, i % per_seq, 0)
    seq_tile_t = lambda i: (i // per_seq, 0, 0, i % per_seq)
    return pl.pallas_call(
        _odd_proj_kernel,
        grid=(T // tm,),
        in_specs=[pl.BlockSpec((tm, D), lambda i: (i, 0)),
                  pl.BlockSpec((1, D), lambda i: (0, 0)),
                  pl.BlockSpec(w_bf16.shape, lambda i: (0, 0)),
                  pl.BlockSpec((tm, Dh), lambda i: (i % per_seq, 0)),
                  pl.BlockSpec((tm, n_sel), lambda i: (i % per_seq, 0))],
        out_specs=[pl.BlockSpec((tm, q_width), lambda i: (i, 0)),
                   pl.BlockSpec((2, 1, G, tm, Dh), lambda i: (0, i // per_seq, 0, i % per_seq, 0)),
                   pl.BlockSpec((1, G, tm, 2 * Dh + n_sel), seq_tile),
                   pl.BlockSpec((1, G, tm, 2 * Dh), seq_tile),
                   pl.BlockSpec((1, G, V_ROWS_64, tm), seq_tile_t),
                   pl.BlockSpec((1, G, V_ROWS_64, tm), seq_tile_t),
                   pl.BlockSpec((1, 4 * NSA_HPG * G, tm), lambda i: (i // per_seq, 0, i % per_seq))],
        out_shape=[jax.ShapeDtypeStruct((T, q_width), BF16),
                   jax.ShapeDtypeStruct((2, B, G, S, Dh), F32),
                   jax.ShapeDtypeStruct((B, G, S, 2 * Dh + n_sel), BF16),
                   jax.ShapeDtypeStruct((B, G, S, 2 * Dh), BF16),
                   jax.ShapeDtypeStruct((B, G, V_ROWS_64, S), BF16),
                   jax.ShapeDtypeStruct((B, G, V_ROWS_64, S), BF16),
                   jax.ShapeDtypeStruct((B, 4 * NSA_HPG * G, S), F32)],
        compiler_params=_params(1),
        name="odd_proj",
    )(x2d, g.reshape(1, D), w_bf16, pos_cols, onehot)


def _head_pair_rows(q, t):
    lane = lax.broadcasted_iota(jnp.int32, (t, 2 * HEAD_DIM), 1)
    zero = jnp.zeros_like(q)
    return jnp.where(lane < HEAD_DIM, q, zero), jnp.where(lane >= HEAD_DIM, q, zero)


def _sb_kernel(q_ref, k_ref, vt_ref, o_ref, acc_ref, carry_ref, *, t):
    qi = pl.program_id(2)
    cols = 2 * t
    q_both = jnp.concatenate(_head_pair_rows(q_ref[0], t), axis=0)
    s_idx = lax.broadcasted_iota(jnp.int32, (t, 2 * t), 0)
    j_idx = lax.broadcasted_iota(jnp.int32, (t, 2 * t), 1) % t
    upper2 = jnp.where(j_idx > s_idx, 1.0, 0.0).astype(BF16)
    key = lax.broadcasted_iota(jnp.int32, (t, cols), 0)
    qry = lax.broadcasted_iota(jnp.int32, (t, cols), 1) % t
    past_diag = key < qry

    acc_ref[...] = jnp.zeros_like(acc_ref)
    carry_ref[...] = jnp.zeros_like(carry_ref)

    def chunks(specs):
        starts = [pl.multiple_of(kc * t, t) for kc, _ in specs]
        z_all = [_dot_nt(k_ref[0, pl.ds(start, t), :], q_both) for start in starts]
        carry = carry_ref[...]
        acc = acc_ref[...]
        for (_, masked), start, z in zip(specs, starts, z_all):
            drop = jnp.maximum(z, 0.0) + jnp.log2(1.0 + jnp.exp2(jnp.abs(z) * -1.0))
            log_beta = z - drop
            if masked:
                drop = jnp.where(past_diag, drop, 0.0)
            hi, lo = _split_bf16(drop)
            tail = _dot(upper2, jnp.concatenate([hi, lo], axis=0))
            w = jnp.exp2(log_beta - tail - carry)
            if masked:
                w = jnp.where(past_diag, w, 0.0)
            carry = carry + tail[0:1] + drop[0:1]
            acc = acc + _dot(vt_ref[0, :, pl.ds(start, t)], w.astype(BF16))
        carry_ref[...] = carry
        acc_ref[...] = acc
        return jnp.min(carry)

    least = lax.cond(qi > 0, lambda: chunks([(qi, True), (qi - 1, False)]), lambda: chunks([(qi, True)]))

    def cond(state):
        j, least_carry = state
        return (j < qi) & (least_carry < -UNDERFLOW_LOG2)

    def body(state):
        j, _ = state
        return j + 1, chunks([(qi - 1 - j, False)])

    lax.while_loop(cond, body, (1, least))
    row = lax.broadcasted_iota(jnp.int32, (2 * HEAD_DIM, t), 0)
    o_t = jnp.where(row < HEAD_DIM, acc_ref[:, :t], acc_ref[:, t:])
    o_ref[0] = o_t.T.astype(o_ref.dtype)


def sb_attention(proj, v_t, t=256):
    B, S, _ = proj.shape
    n_pairs = 4
    return pl.pallas_call(
        functools.partial(_sb_kernel, t=t),
        grid=(B, n_pairs, S // t),
        in_specs=[pl.BlockSpec((1, t, 128), lambda b, p, i: (b, i, p)),
                  pl.BlockSpec((1, S, 128), lambda b, p, i: (b, 0, n_pairs + p)),
                  pl.BlockSpec((1, 128, S), lambda b, p, i: (b, p, 0))],
        out_specs=pl.BlockSpec((1, t, 128), lambda b, p, i: (b, i, p)),
        out_shape=jax.ShapeDtypeStruct((B, S, n_pairs * 128), BF16),
        scratch_shapes=[pltpu.VMEM((128, 2 * t), F32), pltpu.VMEM((1, 2 * t), F32)],
        compiler_params=_params(3),
        name="sb_attention",
    )(proj, proj, v_t)


def _diff_kernel(slopes_ref, lam_ref, q_ref, k_ref, vt_ref, pos_ref, sl_ref, subln_ref, sb_ref, o_ref,
                 m_ref, acc_ref, offset_ref, knorm_ref, *, t, out_scale, lambda_init):
    h = pl.program_id(1)
    qi = pl.program_id(2)
    slope = slopes_ref[h]
    cols = 2 * t
    slope_cols = jnp.broadcast_to(sl_ref[0], (t, 128))
    q_parts = _head_pair_rows(q_ref[0], t)
    q_both = jnp.concatenate([jnp.concatenate([qc, slope_cols], axis=1) for qc in q_parts],
                             axis=0)
    lane = lax.broadcasted_iota(jnp.int32, (t, 2 * HEAD_DIM), 1)

    @pl.when(qi == 0)
    def _():
        def body(c, best):
            k = k_ref[0, pl.ds(pl.multiple_of(c * t, t), t), :].astype(F32)
            k2 = k * k
            return (jnp.maximum(best[0], jnp.max(jnp.sum(jnp.where(lane < HEAD_DIM, k2, 0.0), axis=1))),
                    jnp.maximum(best[1], jnp.max(jnp.sum(jnp.where(lane >= HEAD_DIM, k2, 0.0), axis=1))))

        best = lax.fori_loop(0, k_ref.shape[1] // t, body, (jnp.float32(0.0), jnp.float32(0.0)))
        knorm_ref[0] = best[0]
        knorm_ref[1] = best[1]
        offset_ref[...] = (lax.broadcasted_iota(jnp.int32, (t, cols), 0)
                           - lax.broadcasted_iota(jnp.int32, (t, cols), 1) % t)

    qk_bound = _score_bound(q_ref[0], jnp.maximum(knorm_ref[0], knorm_ref[1]))
    safe = 2.0 * qk_bound <= SAFE_EXP_LOG2
    col = lax.broadcasted_iota(jnp.int32, (1, cols), 1)

    pos = pos_ref[...]

    def step(kc, diagonal=False):
        start = pl.multiple_of(kc * t, t)

        def scores():
            s_t = _dot_nt(jnp.concatenate([k_ref[0, pl.ds(start, t), :], pos], axis=1), q_both)
            return jnp.where(offset_ref[...] <= 0, s_t, NEG_INF) if diagonal else s_t

        return scores, slope * ((kc - qi) * t).astype(F32), lambda: vt_ref[0, 0, :, pl.ds(start, t)], 0

    acc_ref[...] = jnp.zeros_like(acc_ref)

    @pl.when(safe)
    def _():
        frame = qk_bound + slope * (col % t).astype(F32)
        block = lambda steps: _fixed_frame_block(steps, [acc_ref], [frame])
        block([step(qi, True)])
        n_back = jnp.ceil((slope * (t - 1) - UNDERFLOW_LOG2) / (slope * t)).astype(jnp.int32)
        _chunk_loop(jnp.minimum(n_back, qi), lambda i: step(qi - 1 - i), block, (8, 4, 2, 1))

    @pl.when(jnp.logical_not(safe))
    def _():
        m_ref[...] = jnp.full_like(m_ref, NEG_INF)
        block = lambda steps: _softmax_block(steps, [(m_ref, acc_ref)])
        block([step(qi, True)])
        excess = jnp.max(qk_bound - m_ref[...]) + slope * (t - 1)
        n_back = jnp.clip(jnp.ceil((excess - UNDERFLOW_LOG2) / (slope * t)).astype(jnp.int32), 0, qi)
        _chunk_loop(n_back, lambda i: step(qi - 1 - i), block, (4, 2, 1))

    lam_terms = lam_ref[...]
    lam = (jnp.exp(jnp.sum(lam_terms[0:1] * lam_terms[1:2], axis=1, keepdims=True))
           - jnp.exp(jnp.sum(lam_terms[2:3] * lam_terms[3:4], axis=1, keepdims=True))
           + lambda_init)
    o_t = acc_ref[0:128, :] / acc_ref[128:129, :]
    o = (o_t[:, :t] - lam * o_t[:, t:]).T
    o_ref[0, :, 0:128] = sb_ref[0]
    o_ref[0, :, 128:256] = (_rms(o, subln_ref[...]) * out_scale).astype(o_ref.dtype)


def diff_attention(proj, vt_aug, lam_rows, subln, slopes, layer, o_sb, t=512):
    B, S, _ = proj.shape
    n_heads = 4
    lambda_init = 0.8 - 0.6 * math.exp(-0.3 * layer)
    smem = pl.BlockSpec(memory_space=pltpu.SMEM)
    pos = _key_position_columns(t, t, 128)
    slope_cols = _slope_pieces(slopes, 128).reshape(n_heads, 1, 128)
    return pl.pallas_call(
        functools.partial(_diff_kernel, t=t, out_scale=1.0 - lambda_init, lambda_init=lambda_init),
        grid=(B, n_heads, S // t),
        in_specs=[smem,
                  pl.BlockSpec((4, HEAD_DIM), lambda b, h, i: (0, 0)),
                  pl.BlockSpec((1, t, 128), lambda b, h, i: (b, i, 12 + h)),
                  pl.BlockSpec((1, S, 128), lambda b, h, i: (b, 0, 16 + h)),
                  pl.BlockSpec((1, 1, V_ROWS_128, S), lambda b, h, i: (b, h, 0, 0)),
                  pl.BlockSpec((t, 128), lambda b, h, i: (0, 0)),
                  pl.BlockSpec((1, 1, 128), lambda b, h, i: (h, 0, 0)),
                  pl.BlockSpec((1, 128), lambda b, h, i: (0, 0)),
                  pl.BlockSpec((1, t, 128), lambda b, h, i: (b, i, h))],
        out_specs=pl.BlockSpec((1, t, 256), lambda b, h, i: (b, i, h)),
        out_shape=jax.ShapeDtypeStruct((B, S, n_heads * 256), BF16),
        scratch_shapes=[pltpu.VMEM((1, 2 * t), F32), pltpu.VMEM((V_ROWS_128, 2 * t), F32),
                        pltpu.VMEM((t, 2 * t), jnp.int32), pltpu.SMEM((2,), F32)],
        compiler_params=_params(2, 1),
        name="diff_attention",
    )(slopes, lam_rows, proj, proj, vt_aug, pos, slope_cols, subln.reshape(1, 128), o_sb)


def _post_kernel(*refs, ff_chunk, final):
    mix_ref, x_ref, wo_ref, g_ref, w1_ref, w2_ref = refs[:6]
    gf_ref = refs[6] if final else None
    o_ref = refs[-1]
    x = x_ref[...] + _dot(mix_ref[...], wo_ref[...])
    hn = _rms(x, g_ref[...]).astype(BF16)
    acc = x
    for f in range(w1_ref.shape[1] // ff_chunk):
        sl = slice(f * ff_chunk, (f + 1) * ff_chunk)
        hid = jnp.maximum(_dot(hn, w1_ref[:, sl]), 0.0)
        acc = acc + _dot((hid * hid).astype(BF16), w2_ref[sl, :])
    if final:
        acc = _rms(acc, gf_ref[...])
    o_ref[...] = acc


def post_block(mix, x2d, w_out, g_mlp, w1, w2, g_final=None, tm=512, ff_chunk=1024):
    T, D = x2d.shape
    final = g_final is not None
    const = lambda i: (0, 0)
    in_specs = [pl.BlockSpec((tm, mix.shape[1]), lambda i: (i, 0)),
                pl.BlockSpec((tm, D), lambda i: (i, 0)),
                pl.BlockSpec(w_out.shape, const),
                pl.BlockSpec((1, D), const), pl.BlockSpec(w1.shape, const), pl.BlockSpec(w2.shape, const)]
    args = [mix, x2d, w_out, g_mlp.reshape(1, D), w1, w2]
    if final:
        in_specs.append(pl.BlockSpec((1, D), const))
        args.append(g_final.reshape(1, D))
    return pl.pallas_call(
        functools.partial(_post_kernel, ff_chunk=ff_chunk, final=final),
        grid=(T // tm,),
        in_specs=in_specs,
        out_specs=pl.BlockSpec((tm, D), lambda i: (i, 0)),
        out_shape=jax.ShapeDtypeStruct((T, D), F32),
        compiler_params=_params(1),
        name="post_block",
    )(*args)


def _compress_kernel(c_ref, pos_ref, w1_ref, w2_ref, o_ref):
    half = w1_ref.shape[1] // 2
    n_chunks = c_ref.shape[3] // CMP_STRIDE
    first = second = None
    for l in range(CMP_STRIDE):
        tok = c_ref[0, 0, 0, pl.ds(l, n_chunks, stride=CMP_STRIDE), :].astype(BF16)
        a = _dot(tok, w1_ref[0, l * HEAD_DIM:(l + 1) * HEAD_DIM, :])
        b = _dot(tok, w1_ref[0, half + l * HEAD_DIM:half + (l + 1) * HEAD_DIM, :])
        first = a if first is None else first + a
        second = b if second is None else second + b
    pos = jnp.broadcast_to(pos_ref[0], (8, 2 * half)).astype(BF16)
    pre = first + pltpu.roll(second, n_chunks - 1, 0) + _dot(pos, w1_ref[0])[0:1]
    hid = jax.nn.gelu(pre)
    o_ref[0, 0] = _dot(hid.astype(BF16), w2_ref[0]).astype(o_ref.dtype)


def compress_kv(tokens, pos_flat, w1, w2):
    _, B, G, S, Dh = tokens.shape
    n_chunks = S // CMP_STRIDE
    hidden = w1.shape[-1]
    return pl.pallas_call(
        _compress_kernel,
        grid=(2, B * G),
        in_specs=[pl.BlockSpec((1, 1, 1, S, Dh), lambda s, i: (s, i // G, i % G, 0, 0)),
                  pl.BlockSpec((1, 1, CMP_LEN * Dh), lambda s, i: (s, 0, 0)),
                  pl.BlockSpec((1, CMP_LEN * Dh, hidden), lambda s, i: (s, 0, 0)),
                  pl.BlockSpec((1, hidden, Dh), lambda s, i: (s, 0, 0))],
        out_specs=pl.BlockSpec((1, 1, n_chunks, Dh), lambda s, i: (s, i, 0, 0)),
        out_shape=jax.ShapeDtypeStruct((2, B * G, n_chunks, Dh), BF16),
        compiler_params=_params(2),
        name="compress_kv",
    )(tokens, pos_flat, w1, w2)


def _stack_heads(q):
    return jnp.concatenate([q[:, r * HEAD_DIM:(r + 1) * HEAD_DIM] for r in range(NSA_HPG)], axis=0)


def _slope_row(slopes_ref, g, tq):
    col = lax.broadcasted_iota(jnp.int32, (1, NSA_HPG * tq), 1)
    out = jnp.zeros((1, NSA_HPG * tq), F32)
    for r in range(NSA_HPG):
        out = jnp.where(col // tq == r, slopes_ref[g * NSA_HPG + r], out)
    return out


def _cmp_select_kernel(slopes_ref, q_ref, kc_ref, vct_ref, ovt_ref, ocmp_ref, sel_ref, hits_ref,
                       keep_ref, *, tq, n_sel, blocks_per_chunk, hit_tile):
    g = pl.program_id(1)
    t0 = pl.program_id(2) * tq
    cols = NSA_HPG * tq
    n_cmp = kc_ref.shape[2]
    slope_row = _slope_row(slopes_ref, g, tq)
    q_rows = _stack_heads(q_ref[0])

    def group_sum(pc):
        out = pc[:, 0:tq]
        for r in range(1, NSA_HPG):
            out = out + pc[:, r * tq:(r + 1) * tq]
        return out

    def compressed_branch(n_rows, n_blocks):
        tpos = t0 + lax.broadcasted_iota(jnp.int32, (n_rows, cols), 1) % tq
        cmp_end = lax.broadcasted_iota(jnp.int32, (n_rows, cols), 0) * CMP_STRIDE + (CMP_LEN - 1)
        dc = (tpos - cmp_end).astype(F32)
        sc = jnp.where(dc >= 0, _dot_nt(kc_ref[0, 0, 0:n_rows, :], q_rows) - slope_row * dc, NEG_INF)
        e = jnp.exp2(sc - jnp.max(sc, axis=0, keepdims=True))
        any_valid = jnp.where(dc[0:1] >= 0, 1.0, 0.0)
        pc = e * (any_valid / jnp.sum(e, axis=0, keepdims=True))
        o_t = _dot(vct_ref[0, 0, :, 0:n_rows], pc.astype(BF16))
        for r in range(NSA_HPG):
            ocmp_ref[0, 0, r] = o_t[:, r * tq:(r + 1) * tq]
        hi, lo = _split_bf16(group_sum(pc))
        return _dot(ovt_ref[0:n_blocks, 0:n_rows], hi) + _dot(ovt_ref[0:n_blocks, 0:n_rows], lo)

    def select_blocks(imp):
        n_blocks = imp.shape[0]
        blk = lax.broadcasted_iota(jnp.int32, (n_blocks, tq), 0)
        cur = (t0 + lax.broadcasted_iota(jnp.int32, (n_blocks, tq), 1)) // SEL_LEN
        forced = (blk == 0) | (blk == cur) | (blk == cur - 1)
        imp = jnp.where(blk <= cur, imp, -1.0)
        topk = min(SEL_TOPK, n_sel)

        def pick(imp, count):
            for _ in range(count):
                best = jnp.max(imp, axis=0, keepdims=True)
                first = jnp.min(jnp.where(imp == best, blk, n_blocks), axis=0, keepdims=True)
                imp = jnp.where(blk == first, PICKED, imp)
            return imp

        imp = lax.cond(t0 >= 2 * SEL_LEN,
                       lambda: pick(jnp.where(forced, PICKED, imp), topk - 3),
                       lambda: pick(jnp.where(forced, FORCE_SCORE, imp), topk))
        return jnp.where((imp == PICKED) & (blk <= cur), 1.0, 0.0)

    needed = jnp.clip(((t0 + tq - CMP_LEN) // CMP_STRIDE) // (n_cmp // CMP_PARTS) + 1, 1, CMP_PARTS)
    for count in range(1, CMP_PARTS + 1):
        @pl.when(needed == count)
        def _():
            n_blocks = count * (n_sel // CMP_PARTS)
            keep_ref[0:n_blocks, :] = select_blocks(
                compressed_branch(count * (n_cmp // CMP_PARTS), n_blocks))
            if count < CMP_PARTS:
                keep_ref[n_blocks:n_sel, :] = jnp.zeros((n_sel - n_blocks, tq), F32)

    keep = keep_ref[...] > 0.5
    sel_ref[0, 0] = jnp.where(keep, 0.0, -SEL_DROP).T.astype(sel_ref.dtype)
    n_chunks = n_sel // blocks_per_chunk
    member = (lax.broadcasted_iota(jnp.int32, (n_chunks, n_sel), 1) // blocks_per_chunk
              == lax.broadcasted_iota(jnp.int32, (n_chunks, n_sel), 0))
    per_query = _dot(jnp.where(member, 1.0, 0.0).astype(BF16),
                     jnp.where(keep, 1.0, 0.0).astype(BF16)).astype(BF16)
    for part in range(tq // hit_tile):
        hits_ref[0, 0, part] = _dot(per_query[:, part * hit_tile:(part + 1) * hit_tile],
                                    jnp.ones((hit_tile, 128), BF16))


def cmp_select(q, kc, vc_t, overlap_t, slopes, tk, hit_tile, tq=256):
    B, S, _ = q.shape
    n_cmp = kc.shape[2]
    n_sel = S // SEL_LEN
    G = NSA_GROUPS
    smem = pl.BlockSpec(memory_space=pltpu.SMEM)
    return pl.pallas_call(
        functools.partial(_cmp_select_kernel, tq=tq, n_sel=n_sel, blocks_per_chunk=tk // SEL_LEN,
                          hit_tile=hit_tile),
        grid=(B, G, S // tq),
        in_specs=[smem,
                  pl.BlockSpec((1, tq, 256), lambda b, g, i: (b, i, g)),
                  pl.BlockSpec((1, 1, n_cmp, HEAD_DIM), lambda b, g, i: (b, g, 0, 0)),
                  pl.BlockSpec((1, 1, HEAD_DIM, n_cmp), lambda b, g, i: (b, g, 0, 0)),
                  pl.BlockSpec((n_sel, n_cmp), lambda b, g, i: (0, 0))],
        out_specs=[pl.BlockSpec((1, 1, NSA_HPG, HEAD_DIM, tq), lambda b, g, i: (b, g, 0, 0, i)),
                   pl.BlockSpec((1, 1, tq, n_sel), lambda b, g, i: (b, g, i, 0)),
                   pl.BlockSpec((1, 1, tq // hit_tile, S // tk, 128), lambda b, g, i: (b, g, i, 0, 0))],
        out_shape=[jax.ShapeDtypeStruct((B, G, NSA_HPG, HEAD_DIM, S), F32),
                   jax.ShapeDtypeStruct((B, G, S, n_sel), BF16),
                   jax.ShapeDtypeStruct((B, G, S // hit_tile, S // tk, 128), F32)],
        scratch_shapes=[pltpu.VMEM((n_sel, tq), F32)],
        compiler_params=_params(3),
        name="cmp_select",
    )(slopes, q, kc, vc_t, overlap_t)


def _sel_win_kernel(slopes_ref, active_ref, q_ref, sl_ref, ks_ref, vst_ref, kw_ref, vwt_ref, sel_ref,
                    ocmp_ref, gate_ref, o_ref, ms_ref, accs_ref, mw_ref, accw_ref, offset_ref, todo_ref,
                    knorm_ref, *, tq, tk):
    g = pl.program_id(1)
    t0 = pl.program_id(2) * tq
    cols = NSA_HPG * tq
    slope_row = _slope_row(slopes_ref, g, tq)
    q = q_ref[0]
    q_aug = jnp.concatenate(
        [jnp.concatenate([q[:, r * HEAD_DIM:(r + 1) * HEAD_DIM],
                          jnp.broadcast_to(sl_ref[0, r:r + 1, :], (tq, HEAD_DIM))], axis=1)
         for r in range(NSA_HPG)], axis=0)
    q_sel = jnp.concatenate([q_aug, jnp.concatenate([sel_ref[0, 0]] * NSA_HPG, axis=0)], axis=1)

    @pl.when(pl.program_id(2) == 0)
    def _():
        offset_ref[...] = (lax.broadcasted_iota(jnp.int32, (tk, cols), 0)
                           - lax.broadcasted_iota(jnp.int32, (tk, cols), 1) % tq)

        def body(c, best):
            rows = pl.ds(pl.multiple_of(c * tk, tk), tk)
            ks = ks_ref[0, 0, rows, :][:, :HEAD_DIM].astype(F32)
            kw = kw_ref[0, 0, rows, :][:, :HEAD_DIM].astype(F32)
            return (jnp.maximum(best[0], jnp.max(jnp.sum(ks * ks, axis=1))),
                    jnp.maximum(best[1], jnp.max(jnp.sum(kw * kw, axis=1))))

        best = lax.fori_loop(0, ks_ref.shape[2] // tk, body, (jnp.float32(0.0), jnp.float32(0.0)))
        knorm_ref[0] = best[0]
        knorm_ref[1] = best[1]

    sel_bound = _score_bound(q, knorm_ref[0])
    win_bound = _score_bound(q, knorm_ref[1])
    safe = 2.0 * jnp.maximum(sel_bound, win_bound) <= SAFE_EXP_LOG2

    def step(c, k_ref, q_rows, vt_ref, keep=None):
        inside = c >= 0
        start = pl.multiple_of(jnp.maximum(c, 0) * tk, tk)
        shift = c * tk - t0

        def scores():
            s_t = _dot_nt(k_ref[0, 0, pl.ds(start, tk), :], q_rows)
            if keep is None:
                return s_t
            lo, hi = keep
            offset = offset_ref[...]
            if hi is not None:
                return jnp.where(offset <= jnp.where(inside, hi - shift, -FAR), s_t, NEG_INF)
            return jnp.where(offset > jnp.where(inside, lo - shift, FAR), s_t, NEG_INF)

        kappa = jnp.where(inside, slope_row * shift.astype(F32), NEG_INF)
        return scores, kappa, lambda: vt_ref[0, 0, :, pl.ds(start, tk)]

    last = t0 // tk
    causal, recent, anything = (None, 0), (-WINDOW, None), (-FAR, None)
    first_steps = [step(last - 2, kw_ref, q_aug, vwt_ref, recent) + (1,),
                   step(last, ks_ref, q_sel, vst_ref, causal) + (0,),
                   step(last - 1, kw_ref, q_aug, vwt_ref, anything) + (1,),
                   step(last, kw_ref, q_aug, vwt_ref, causal) + (1,)]

    def note_active(c, n):
        hit = active_ref[0, 0, 0, 0, c] > 0

        @pl.when(hit)
        def _():
            todo_ref[n] = c

        return n + hit.astype(jnp.int32)

    n_todo = lax.fori_loop(0, last, note_active, 0)
    sel_step = lambda i: step(todo_ref[i], ks_ref, q_sel, vst_ref) + (0,)

    accs_ref[...] = jnp.zeros_like(accs_ref)
    accw_ref[...] = jnp.zeros_like(accw_ref)

    @pl.when(safe)
    def _():
        in_tile = slope_row * (lax.broadcasted_iota(jnp.int32, (1, cols), 1) % tq).astype(F32)
        block = lambda steps: _fixed_frame_block(steps, [accs_ref, accw_ref],
                                                 [sel_bound + in_tile, win_bound + in_tile])
        block(first_steps)
        _chunk_loop(n_todo, sel_step, block, (8, 4, 2, 1))

    @pl.when(jnp.logical_not(safe))
    def _():
        states = [(ms_ref, accs_ref), (mw_ref, accw_ref)]
        for m_ref, _ in states:
            m_ref[...] = jnp.full_like(m_ref, NEG_INF)
        block = lambda steps: _softmax_block(steps, states)
        block(first_steps)
        _chunk_loop(n_todo, sel_step, block, (4, 2, 1))

    o_sel = accs_ref[0:HEAD_DIM, :] / accs_ref[HEAD_DIM:HEAD_DIM + 1, :]
    o_win = accw_ref[0:HEAD_DIM, :] / accw_ref[HEAD_DIM:HEAD_DIM + 1, :]

    gates = gate_ref[0]
    outs = []
    for r in range(NSA_HPG):
        cs = slice(r * tq, (r + 1) * tq)
        outs.append(gates[3 * r:3 * r + 1] * ocmp_ref[0, 0, r] + gates[3 * r + 1:3 * r + 2] * o_sel[:, cs]
                    + gates[3 * r + 2:3 * r + 3] * o_win[:, cs])
    o_ref[0] = jnp.concatenate(outs, axis=0).T.astype(o_ref.dtype)


def sel_win_attention(q, active, slope_cols, ks, vs_t, kw, vw_t, sel_bias, o_cmp, gates_t,
                      slopes, tq=256, tk=256):
    B, S, _ = q.shape
    assert tq == tk and WINDOW == 2 * tk
    G = NSA_GROUPS
    n_sel = S // SEL_LEN
    cols = NSA_HPG * tq
    smem = pl.BlockSpec(memory_space=pltpu.SMEM)
    ks_spec = pl.BlockSpec((1, 1, S, ks.shape[-1]), lambda b, g, i: (b, g, 0, 0))
    kw_spec = pl.BlockSpec((1, 1, S, kw.shape[-1]), lambda b, g, i: (b, g, 0, 0))
    v_spec = pl.BlockSpec((1, 1, V_ROWS_64, S), lambda b, g, i: (b, g, 0, 0))
    return pl.pallas_call(
        functools.partial(_sel_win_kernel, tq=tq, tk=tk),
        grid=(B, G, S // tq),
        in_specs=[smem,
                  pl.BlockSpec((1, 1, 1, 1, S // tk), lambda b, g, i: (b, g, i, 0, 0),
                               memory_space=pltpu.SMEM),
                  pl.BlockSpec((1, tq, 256), lambda b, g, i: (b, i, g)),
                  pl.BlockSpec((1, NSA_HPG, HEAD_DIM), lambda b, g, i: (g, 0, 0)),
                  ks_spec, v_spec, kw_spec, v_spec,
                  pl.BlockSpec((1, 1, tq, n_sel), lambda b, g, i: (b, g, i, 0)),
                  pl.BlockSpec((1, 1, NSA_HPG, HEAD_DIM, tq), lambda b, g, i: (b, g, 0, 0, i)),
                  pl.BlockSpec((1, 4 * NSA_HPG, tq), lambda b, g, i: (b, g, i))],
        out_specs=pl.BlockSpec((1, tq, 256), lambda b, g, i: (b, i, g)),
        out_shape=jax.ShapeDtypeStruct((B, S, G * 256), BF16),
        scratch_shapes=[pltpu.VMEM((1, cols), F32), pltpu.VMEM((V_ROWS_64, cols), F32),
                        pltpu.VMEM((1, cols), F32), pltpu.VMEM((V_ROWS_64, cols), F32),
                        pltpu.VMEM((tk, cols), jnp.int32),
                        pltpu.SMEM((S // tk,), jnp.int32), pltpu.SMEM((2,), F32)],
        compiler_params=_params(2, 1),
        name="sel_win_attention",
    )(slopes, active, q, slope_cols, ks, vs_t, kw, vw_t, sel_bias, o_cmp, gates_t)


def _alibi_slopes_log2(n_heads):
    slopes = np.exp2(-8.0 * (np.arange(n_heads, dtype=np.float32) + 1.0) / n_heads)
    return (slopes.astype(np.float32) * np.float32(LOG2E)).astype(np.float32)


def even_layer_mix(x2d, B, S, norm_g, w_in, lam_q1, lam_k1, lam_q2, lam_k2, subln, layer):
    proj, sb_vt, df_vt = even_proj(x2d, B, S, norm_g, w_in.astype(BF16))
    proj = proj.reshape(B, S, -1)
    lam_rows = jnp.stack([lam_q1, lam_k1, lam_q2, lam_k2]).astype(F32)
    mix = diff_attention(proj, df_vt, lam_rows, subln.astype(F32), _alibi_slopes_log2(4), layer,
                         sb_attention(proj, sb_vt))
    return mix.reshape(B * S, -1)


def even_w_out_rows(w_out):
    d = w_out.shape[1]
    return w_out.reshape(2, 4, 128, d).transpose(1, 0, 2, 3).reshape(-1, d)


def odd_layer_mix(x2d, B, S, norm_g, w_in, pos_k, k_w1, k_w2, pos_v, v_w1, v_w2, tq=256, tk=256):
    G, Dh = NSA_GROUPS, HEAD_DIM
    q_width = G * NSA_HPG * Dh
    n_main = q_width + 6 * G * Dh
    per_group = NSA_HPG * N_GATES
    w_gate = w_in[:, n_main:n_main + G * per_group].reshape(-1, G, per_group)
    w_gate = jnp.pad(w_gate, ((0, 0), (0, 0), (0, 4 * NSA_HPG - per_group))).reshape(-1, 4 * NSA_HPG * G)
    w_gate = jnp.pad(w_gate, ((0, 0), (0, GATE_PAD - w_gate.shape[1])))
    w_all = jnp.concatenate([w_in[:, :n_main], w_gate], axis=1).astype(BF16)

    n_chunks = S // CMP_STRIDE
    n_sel = S // SEL_LEN
    onehot = (jnp.arange(S)[:, None] // SEL_LEN == jnp.arange(n_sel)[None, :]).astype(BF16)
    q, cmp_in, ks, kw, vs_t, vw_t, gates_t = odd_proj(x2d, B, S, norm_g, w_all,
                                                      _key_position_columns(S, tk, Dh), onehot)
    q = q.reshape(B, S, q_width)

    pos_flat = jnp.stack([pos_k, pos_v]).reshape(2, 1, CMP_LEN * Dh).astype(F32)
    w1 = jnp.stack([k_w1, v_w1]).astype(BF16)
    w2 = jnp.stack([k_w2, v_w2]).astype(BF16)
    cmp = compress_kv(cmp_in, pos_flat, w1, w2).reshape(2, B, G, n_chunks, Dh)

    cmp_start = jnp.arange(n_chunks) * CMP_STRIDE
    sel_start = jnp.arange(n_sel) * SEL_LEN
    overlap_t = ((cmp_start[None, :] < sel_start[:, None] + SEL_LEN)
                 & (sel_start[:, None] <= cmp_start[None, :] + CMP_LEN - 1)).astype(BF16)
    slopes = _alibi_slopes_log2(G * NSA_HPG)
    slope_cols = _slope_pieces(slopes, Dh).reshape(G, NSA_HPG, Dh)

    o_cmp, sel_bias, hits = cmp_select(q, cmp[0], cmp[1].transpose(0, 1, 3, 2), overlap_t, slopes, tk, tq)
    active = (hits[..., 0] > 0).astype(jnp.int32)[:, :, :, None, :]
    o = sel_win_attention(q, active, slope_cols, ks, vs_t, kw, vw_t, sel_bias, o_cmp, gates_t, slopes,
                          tq=tq, tk=tk)
    return o.reshape(B * S, q_width)


def kernel(x, attn_norm, mlp_norm, final_norm, ev_w_in, ev_lam_q1, ev_lam_k1, ev_lam_q2, ev_lam_k2,
           ev_subln, ev_w_out, od_w_in, od_cmp_pos_k, od_cmp_k_w1, od_cmp_k_w2, od_cmp_pos_v,
           od_cmp_v_w1, od_cmp_v_w2, od_w_out, mlp_w1, mlp_w2):
    B, S, D = x.shape
    depth = attn_norm.shape[0]
    x2d = x.reshape(B * S, D)
    for layer in range(depth):
        idx = layer // 2
        if layer % 2 == 0:
            mix = even_layer_mix(x2d, B, S, attn_norm[layer], ev_w_in[idx], ev_lam_q1[idx],
                                 ev_lam_k1[idx], ev_lam_q2[idx], ev_lam_k2[idx], ev_subln[idx], layer)
            w_out = even_w_out_rows(ev_w_out[idx])
        else:
            mix = odd_layer_mix(x2d, B, S, attn_norm[layer], od_w_in[idx], od_cmp_pos_k[idx],
                                od_cmp_k_w1[idx], od_cmp_k_w2[idx], od_cmp_pos_v[idx],
                                od_cmp_v_w1[idx], od_cmp_v_w2[idx])
            w_out = od_w_out[idx]
        g_final = final_norm if layer == depth - 1 else None
        x2d = post_block(mix, x2d, w_out.astype(BF16), mlp_norm[layer], mlp_w1[layer].astype(BF16),
                         mlp_w2[layer].astype(BF16), g_final)
    return x2d.reshape(B, S, D)
```

```python
import functools
import math

import jax
import jax.numpy as jnp
import numpy as np
from jax import lax
from jax.experimental import pallas as pl
from jax.experimental.pallas import tpu as pltpu

F32 = jnp.float32
BF16 = jnp.bfloat16

HEAD_DIM = 64
RMS_EPS = 1e-6
NEG_INF = -1e30
FORCE_SCORE = 1e6
NSA_GROUPS = 4
NSA_HPG = 4
CMP_LEN = 32
CMP_STRIDE = 16
SEL_LEN = 64
SEL_TOPK = 16
WINDOW = 512
N_GATES = 3
GATE_PAD = 128
SEL_DROP = 2.0 ** 24
BF16_EXACT_INT = 256
V_ROWS_64 = 80
V_ROWS_128 = 144

LOG2E = math.log2(math.e)
Q_SCALE = HEAD_DIM ** -0.5 * LOG2E
UNDERFLOW_LOG2 = -160.0
SAFE_EXP_LOG2 = 100.0
FAR = 1 << 30
PICKED = -2.0

VMEM_LIMIT = 56 * 1024 * 1024


def _params(n_parallel, n_arbitrary=0):
    return pltpu.CompilerParams(dimension_semantics=("parallel",) * n_parallel + ("arbitrary",) * n_arbitrary,
                                vmem_limit_bytes=VMEM_LIMIT)


def _rms(x, g):
    ms = jnp.mean(x * x, axis=-1, keepdims=True)
    return x * lax.rsqrt(ms + RMS_EPS) * g


def _dot(a, b):
    return jnp.dot(a, b, preferred_element_type=F32)


def _dot_nt(a, b):
    return lax.dot_general(a, b, (((1,), (1,)), ((), ())), preferred_element_type=F32)


def _split_bf16(x):
    hi = x.astype(BF16)
    lo = (x - hi.astype(F32)).astype(BF16)
    return hi, lo


def _slope_pieces(slopes, width):
    def top_bits(x):
        return (x.view(np.uint32) & np.uint32(0xFFFF0000)).view(np.float32)

    s1 = top_bits(slopes)
    r1 = slopes - s1
    s2 = top_bits(r1)
    s3 = top_bits(r1 - s2)
    out = np.zeros((slopes.shape[0], width), np.float32)
    out[:, :6] = np.stack([s1, s2, s3, s1, s2, s3], axis=1)
    return jnp.asarray(out).astype(BF16)


def _key_position_columns(n, tk, width):
    j = jnp.arange(n) % tk
    a = (j // BF16_EXACT_INT) * BF16_EXACT_INT
    b = j % BF16_EXACT_INT
    cols = jnp.stack([a, a, a, b, b, b], axis=1).astype(BF16)
    return jnp.pad(cols, ((0, 0), (0, width - cols.shape[1])))


def _softmax_block(steps, states):
    s_all = [scores() for scores, _, _, _ in steps]
    live = {k: (states[k][0][...], states[k][1][...]) for k in sorted({k for _, _, _, k in steps})}
    for (_, kap, values, k), s_t in zip(steps, s_all):
        m_run, acc = live[k]
        m_new = jnp.maximum(m_run, jnp.max(s_t, axis=0, keepdims=True) + kap)
        p = jnp.exp2(s_t - (m_new - kap)).astype(BF16)
        live[k] = (m_new, jnp.exp2(m_run - m_new) * acc + _dot(values(), p))
    for k, (m_run, acc) in live.items():
        states[k][0][...] = m_run
        states[k][1][...] = acc


def _fixed_frame_block(steps, acc_refs, frames):
    s_all = [scores() for scores, _, _, _ in steps]
    live = {k: acc_refs[k][...] for k in sorted({k for _, _, _, k in steps})}
    for (_, kap, values, k), s_t in zip(steps, s_all):
        live[k] = live[k] + _dot(values(), jnp.exp2(s_t - (frames[k] - kap)).astype(BF16))
    for k, acc in live.items():
        acc_refs[k][...] = acc


def _chunk_loop(n_steps, step, block, groups):
    done = 0
    for group in groups:
        def body(i, carry, group=group, done=done):
            block([step(done + group * i + j) for j in range(group)])
            return carry

        n_groups = (n_steps - done) // group
        lax.fori_loop(0, n_groups, body, 0)
        done = done + group * n_groups


def _score_bound(q, k_norm2):
    q32 = q.astype(F32)
    width = q.shape[1]
    member = (lax.broadcasted_iota(jnp.int32, (width, 128), 0) // HEAD_DIM
              == lax.broadcasted_iota(jnp.int32, (width, 128), 1))
    head_norm2 = _dot((q32 * q32).astype(BF16), jnp.where(member, 1.0, 0.0).astype(BF16))
    return jnp.sqrt(jnp.max(head_norm2) * k_norm2 * 1.05)


PROJ_CHUNK = 512
CMP_PARTS = 8


def _ones_row_tail(rows, width):
    return jnp.where(lax.broadcasted_iota(jnp.int32, (rows, width), 0) == 0, 1.0, 0.0).astype(BF16)


def _even_proj_kernel(x_ref, g_ref, w_ref, o_ref, sbv_ref, dfv_ref):
    tm = x_ref.shape[0]
    xn = _rms(x_ref[...], g_ref[...]).astype(BF16)
    tail = _ones_row_tail(V_ROWS_128 - 2 * HEAD_DIM, tm)
    for c in range(w_ref.shape[1] // PROJ_CHUNK):
        sl = slice(c * PROJ_CHUNK, (c + 1) * PROJ_CHUNK)
        res = _dot(xn, w_ref[:, sl])
        if c in (0, 3):
            res = res * Q_SCALE
        o_ref[:, sl] = res.astype(BF16)
        if c == 2:
            sbv_ref[0] = res.T.astype(BF16)
        if c == 5:
            v_t = res.T.astype(BF16)
            for h in range(dfv_ref.shape[1]):
                dfv_ref[0, h, 0:2 * HEAD_DIM, :] = v_t[h * 2 * HEAD_DIM:(h + 1) * 2 * HEAD_DIM]
                dfv_ref[0, h, 2 * HEAD_DIM:V_ROWS_128, :] = tail


def even_proj(x2d, B, S, g, w_bf16, tm=512):
    T, D = x2d.shape
    n_w = w_bf16.shape[1]
    per_seq = S // tm
    assert n_w == 6 * PROJ_CHUNK
    return pl.pallas_call(
        _even_proj_kernel,
        grid=(T // tm,),
        in_specs=[pl.BlockSpec((tm, D), lambda i: (i, 0)),
                  pl.BlockSpec((1, D), lambda i: (0, 0)),
                  pl.BlockSpec((D, n_w), lambda i: (0, 0))],
        out_specs=[pl.BlockSpec((tm, n_w), lambda i: (i, 0)),
                   pl.BlockSpec((1, PROJ_CHUNK, tm), lambda i: (i // per_seq, 0, i % per_seq)),
                   pl.BlockSpec((1, 4, V_ROWS_128, tm), lambda i: (i // per_seq, 0, 0, i % per_seq))],
        out_shape=[jax.ShapeDtypeStruct((T, n_w), BF16),
                   jax.ShapeDtypeStruct((B, PROJ_CHUNK, S), BF16),
                   jax.ShapeDtypeStruct((B, 4, V_ROWS_128, S), BF16)],
        compiler_params=_params(1),
        name="even_proj",
    )(x2d, g.reshape(1, D), w_bf16)


def _odd_proj_kernel(x_ref, g_ref, w_ref, pos_ref, onehot_ref,
                     q_ref, cin_ref, ks_ref, kw_ref, vst_ref, vwt_ref, gate_ref):
    tm = x_ref.shape[0]
    G, Dh = NSA_GROUPS, HEAD_DIM
    kv = G * Dh
    xn = _rms(x_ref[...], g_ref[...]).astype(BF16)
    tail = _ones_row_tail(V_ROWS_64 - Dh, tm)
    pos = pos_ref[...]
    onehot = onehot_ref[...]
    q_width = q_ref.shape[1]
    for c in range(q_width // PROJ_CHUNK):
        sl = slice(c * PROJ_CHUNK, (c + 1) * PROJ_CHUNK)
        q_ref[:, sl] = (_dot(xn, w_ref[:, sl]) * Q_SCALE).astype(BF16)

    def pair(j):
        return _dot(xn, w_ref[:, q_width + j * 2 * kv:q_width + (j + 1) * 2 * kv])

    res = pair(0)
    for which in range(2):
        for g in range(G):
            cin_ref[which, 0, g] = res[:, which * kv + g * Dh:which * kv + (g + 1) * Dh]

    for j, k_ref, vt_ref, extra in ((1, ks_ref, vst_ref, [pos, onehot]), (2, kw_ref, vwt_ref, [pos])):
        res = pair(j)
        v_t = res[:, kv:2 * kv].T.astype(BF16)
        for g in range(G):
            k_ref[0, g] = jnp.concatenate([res[:, g * Dh:(g + 1) * Dh].astype(BF16)] + extra, axis=1)
            vt_ref[0, g, 0:Dh, :] = v_t[g * Dh:(g + 1) * Dh]
            vt_ref[0, g, Dh:V_ROWS_64, :] = tail

    logits = _dot(xn, w_ref[:, q_width + 6 * kv:q_width + 6 * kv + GATE_PAD])
    gate_ref[0] = jax.nn.sigmoid(logits).T[0:gate_ref.shape[1]]


def odd_proj(x2d, B, S, g, w_bf16, pos_cols, onehot, tm=512):
    T, D = x2d.shape
    G, Dh = NSA_GROUPS, HEAD_DIM
    q_width = G * NSA_HPG * Dh
    n_sel = onehot.shape[1]
    per_seq = S // tm
    seq_tile = lambda i: (i // per_seq, 0, i % per_seq, 0)
    seq_tile_t = lambda i: (i // per_seq, 0, 0, i % per_seq)
    return pl.pallas_call(
        _odd_proj_kernel,
        grid=(T // tm,),
        in_specs=[pl.BlockSpec((tm, D), lambda i: (i, 0)),
                  pl.BlockSpec((1, D), lambda i: (0, 0)),
                  pl.BlockSpec(w_bf16.shape, lambda i: (0, 0)),
                  pl.BlockSpec((tm, Dh), lambda i: (i % per_seq, 0)),
                  pl.BlockSpec((tm, n_sel), lambda i: (i % per_seq, 0))],
        out_specs=[pl.BlockSpec((tm, q_width), lambda i: (i, 0)),
                   pl.BlockSpec((2, 1, G, tm, Dh), lambda i: (0, i // per_seq, 0, i % per_seq, 0)),
                   pl.BlockSpec((1, G, tm, 2 * Dh + n_sel), seq_tile),
                   pl.BlockSpec((1, G, tm, 2 * Dh), seq_tile),
                   pl.BlockSpec((1, G, V_ROWS_64, tm), seq_tile_t),
                   pl.BlockSpec((1, G, V_ROWS_64, tm), seq_tile_t),
                   pl.BlockSpec((1, 4 * NSA_HPG * G, tm), lambda i: (i // per_seq, 0, i % per_seq))],
        out_shape=[jax.ShapeDtypeStruct((T, q_width), BF16),
                   jax.ShapeDtypeStruct((2, B, G, S, Dh), F32),
                   jax.ShapeDtypeStruct((B, G, S, 2 * Dh + n_sel), BF16),
                   jax.ShapeDtypeStruct((B, G, S, 2 * Dh), BF16),
                   jax.ShapeDtypeStruct((B, G, V_ROWS_64, S), BF16),
                   jax.ShapeDtypeStruct((B, G, V_ROWS_64, S), BF16),
                   jax.ShapeDtypeStruct((B, 4 * NSA_HPG * G, S), F32)],
        compiler_params=_params(1),
        name="odd_proj",
    )(x2d, g.reshape(1, D), w_bf16, pos_cols, onehot)


def _head_pair_rows(q, t):
    lane = lax.broadcasted_iota(jnp.int32, (t, 2 * HEAD_DIM), 1)
    zero = jnp.zeros_like(q)
    return jnp.where(lane < HEAD_DIM, q, zero), jnp.where(lane >= HEAD_DIM, q, zero)


def _sb_kernel(q_ref, k_ref, vt_ref, o_ref, acc_ref, carry_ref, *, t):
    qi = pl.program_id(2)
    cols = 2 * t
    q_both = jnp.concatenate(_head_pair_rows(q_ref[0], t), axis=0)
    s_idx = lax.broadcasted_iota(jnp.int32, (t, 2 * t), 0)
    j_idx = lax.broadcasted_iota(jnp.int32, (t, 2 * t), 1) % t
    upper2 = jnp.where(j_idx > s_idx, 1.0, 0.0).astype(BF16)
    key = lax.broadcasted_iota(jnp.int32, (t, cols), 0)
    qry = lax.broadcasted_iota(jnp.int32, (t, cols), 1) % t
    past_diag = key < qry

    acc_ref[...] = jnp.zeros_like(acc_ref)
    carry_ref[...] = jnp.zeros_like(carry_ref)

    def chunks(specs):
        starts = [pl.multiple_of(kc * t, t) for kc, _ in specs]
        z_all = [_dot_nt(k_ref[0, pl.ds(start, t), :], q_both) for start in starts]
        carry = carry_ref[...]
        acc = acc_ref[...]
        for (_, masked), start, z in zip(specs, starts, z_all):
            drop = jnp.maximum(z, 0.0) + jnp.log2(1.0 + jnp.exp2(jnp.abs(z) * -1.0))
            log_beta = z - drop
            if masked:
                drop = jnp.where(past_diag, drop, 0.0)
            hi, lo = _split_bf16(drop)
            tail = _dot(upper2, jnp.concatenate([hi, lo], axis=0))
            w = jnp.exp2(log_beta - tail - carry)
            if masked:
                w = jnp.where(past_diag, w, 0.0)
            carry = carry + tail[0:1] + drop[0:1]
            acc = acc + _dot(vt_ref[0, :, pl.ds(start, t)], w.astype(BF16))
        carry_ref[...] = carry
        acc_ref[...] = acc
        return jnp.min(carry)

    least = lax.cond(qi > 0, lambda: chunks([(qi, True), (qi - 1, False)]), lambda: chunks([(qi, True)]))

    def cond(state):
        j, least_carry = state
        return (j < qi) & (least_carry < -UNDERFLOW_LOG2)

    def body(state):
        j, _ = state
        return j + 1, chunks([(qi - 1 - j, False)])

    lax.while_loop(cond, body, (1, least))
    row = lax.broadcasted_iota(jnp.int32, (2 * HEAD_DIM, t), 0)
    o_t = jnp.where(row < HEAD_DIM, acc_ref[:, :t], acc_ref[:, t:])
    o_ref[0] = o_t.T.astype(o_ref.dtype)


def sb_attention(proj, v_t, t=256):
    B, S, _ = proj.shape
    n_pairs = 4
    return pl.pallas_call(
        functools.partial(_sb_kernel, t=t),
        grid=(B, n_pairs, S // t),
        in_specs=[pl.BlockSpec((1, t, 128), lambda b, p, i: (b, i, p)),
                  pl.BlockSpec((1, S, 128), lambda b, p, i: (b, 0, n_pairs + p)),
                  pl.BlockSpec((1, 128, S), lambda b, p, i: (b, p, 0))],
        out_specs=pl.BlockSpec((1, t, 128), lambda b, p, i: (b, i, p)),
        out_shape=jax.ShapeDtypeStruct((B, S, n_pairs * 128), BF16),
        scratch_shapes=[pltpu.VMEM((128, 2 * t), F32), pltpu.VMEM((1, 2 * t), F32)],
        compiler_params=_params(3),
        name="sb_attention",
    )(proj, proj, v_t)


def _diff_kernel(slopes_ref, lam_ref, q_ref, k_ref, vt_ref, pos_ref, sl_ref, subln_ref, sb_ref, o_ref,
                 m_ref, acc_ref, offset_ref, knorm_ref, *, t, out_scale, lambda_init):
    h = pl.program_id(1)
    qi = pl.program_id(2)
    slope = slopes_ref[h]
    cols = 2 * t
    slope_cols = jnp.broadcast_to(sl_ref[0], (t, 128))
    q_parts = _head_pair_rows(q_ref[0], t)
    q_both = jnp.concatenate([jnp.concatenate([qc, slope_cols], axis=1) for qc in q_parts],
                             axis=0)
    lane = lax.broadcasted_iota(jnp.int32, (t, 2 * HEAD_DIM), 1)

    @pl.when(qi == 0)
    def _():
        def body(c, best):
            k = k_ref[0, pl.ds(pl.multiple_of(c * t, t), t), :].astype(F32)
            k2 = k * k
            return (jnp.maximum(best[0], jnp.max(jnp.sum(jnp.where(lane < HEAD_DIM, k2, 0.0), axis=1))),
                    jnp.maximum(best[1], jnp.max(jnp.sum(jnp.where(lane >= HEAD_DIM, k2, 0.0), axis=1))))

        best = lax.fori_loop(0, k_ref.shape[1] // t, body, (jnp.float32(0.0), jnp.float32(0.0)))
        knorm_ref[0] = best[0]
        knorm_ref[1] = best[1]
        offset_ref[...] = (lax.broadcasted_iota(jnp.int32, (t, cols), 0)
                           - lax.broadcasted_iota(jnp.int32, (t, cols), 1) % t)

    qk_bound = _score_bound(q_ref[0], jnp.maximum(knorm_ref[0], knorm_ref[1]))
    safe = 2.0 * qk_bound <= SAFE_EXP_LOG2
    col = lax.broadcasted_iota(jnp.int32, (1, cols), 1)

    pos = pos_ref[...]

    def step(kc, diagonal=False):
        start = pl.multiple_of(kc * t, t)

        def scores():
            s_t = _dot_nt(jnp.concatenate([k_ref[0, pl.ds(start, t), :], pos], axis=1), q_both)
            return jnp.where(offset_ref[...] <= 0, s_t, NEG_INF) if diagonal else s_t

        return scores, slope * ((kc - qi) * t).astype(F32), lambda: vt_ref[0, 0, :, pl.ds(start, t)], 0

    acc_ref[...] = jnp.zeros_like(acc_ref)

    @pl.when(safe)
    def _():
        frame = qk_bound + slope * (col % t).astype(F32)
        block = lambda steps: _fixed_frame_block(steps, [acc_ref], [frame])
        block([step(qi, True)])
        n_back = jnp.ceil((slope * (t - 1) - UNDERFLOW_LOG2) / (slope * t)).astype(jnp.int32)
        _chunk_loop(jnp.minimum(n_back, qi), lambda i: step(qi - 1 - i), block, (8, 4, 2, 1))

    @pl.when(jnp.logical_not(safe))
    def _():
        m_ref[...] = jnp.full_like(m_ref, NEG_INF)
        block = lambda steps: _softmax_block(steps, [(m_ref, acc_ref)])
        block([step(qi, True)])
        excess = jnp.max(qk_bound - m_ref[...]) + slope * (t - 1)
        n_back = jnp.clip(jnp.ceil((excess - UNDERFLOW_LOG2) / (slope * t)).astype(jnp.int32), 0, qi)
        _chunk_loop(n_back, lambda i: step(qi - 1 - i), block, (4, 2, 1))

    lam_terms = lam_ref[...]
    lam = (jnp.exp(jnp.sum(lam_terms[0:1] * lam_terms[1:2], axis=1, keepdims=True))
           - jnp.exp(jnp.sum(lam_terms[2:3] * lam_terms[3:4], axis=1, keepdims=True))
           + lambda_init)
    o_t = acc_ref[0:128, :] / acc_ref[128:129, :]
    o = (o_t[:, :t] - lam * o_t[:, t:]).T
    o_ref[0, :, 0:128] = sb_ref[0]
    o_ref[0, :, 128:256] = (_rms(o, subln_ref[...]) * out_scale).astype(o_ref.dtype)


def diff_attention(proj, vt_aug, lam_rows, subln, slopes, layer, o_sb, t=512):
    B, S, _ = proj.shape
    n_heads = 4
    lambda_init = 0.8 - 0.6 * math.exp(-0.3 * layer)
    smem = pl.BlockSpec(memory_space=pltpu.SMEM)
    pos = _key_position_columns(t, t, 128)
    slope_cols = _slope_pieces(slopes, 128).reshape(n_heads, 1, 128)
    return pl.pallas_call(
        functools.partial(_diff_kernel, t=t, out_scale=1.0 - lambda_init, lambda_init=lambda_init),
        grid=(B, n_heads, S // t),
        in_specs=[smem,
                  pl.BlockSpec((4, HEAD_DIM), lambda b, h, i: (0, 0)),
                  pl.BlockSpec((1, t, 128), lambda b, h, i: (b, i, 12 + h)),
                  pl.BlockSpec((1, S, 128), lambda b, h, i: (b, 0, 16 + h)),
                  pl.BlockSpec((1, 1, V_ROWS_128, S), lambda b, h, i: (b, h, 0, 0)),
                  pl.BlockSpec((t, 128), lambda b, h, i: (0, 0)),
                  pl.BlockSpec((1, 1, 128), lambda b, h, i: (h, 0, 0)),
                  pl.BlockSpec((1, 128), lambda b, h, i: (0, 0)),
                  pl.BlockSpec((1, t, 128), lambda b, h, i: (b, i, h))],
        out_specs=pl.BlockSpec((1, t, 256), lambda b, h, i: (b, i, h)),
        out_shape=jax.ShapeDtypeStruct((B, S, n_heads * 256), BF16),
        scratch_shapes=[pltpu.VMEM((1, 2 * t), F32), pltpu.VMEM((V_ROWS_128, 2 * t), F32),
                        pltpu.VMEM((t, 2 * t), jnp.int32), pltpu.SMEM((2,), F32)],
        compiler_params=_params(2, 1),
        name="diff_attention",
    )(slopes, lam_rows, proj, proj, vt_aug, pos, slope_cols, subln.reshape(1, 128), o_sb)


def _post_kernel(*refs, ff_chunk, final):
    mix_ref, x_ref, wo_ref, g_ref, w1_ref, w2_ref = refs[:6]
    gf_ref = refs[6] if final else None
    o_ref = refs[-1]
    x = x_ref[...] + _dot(mix_ref[...], wo_ref[...])
    hn = _rms(x, g_ref[...]).astype(BF16)
    acc = x
    for f in range(w1_ref.shape[1] // ff_chunk):
        sl = slice(f * ff_chunk, (f + 1) * ff_chunk)
        hid = jnp.maximum(_dot(hn, w1_ref[:, sl]), 0.0)
        acc = acc + _dot((hid * hid).astype(BF16), w2_ref[sl, :])
    if final:
        acc = _rms(acc, gf_ref[...])
    o_ref[...] = acc


def post_block(mix, x2d, w_out, g_mlp, w1, w2, g_final=None, tm=512, ff_chunk=1024):
    T, D = x2d.shape
    final = g_final is not None
    const = lambda i: (0, 0)
    in_specs = [pl.BlockSpec((tm, mix.shape[1]), lambda i: (i, 0)),
                pl.BlockSpec((tm, D), lambda i: (i, 0)),
                pl.BlockSpec(w_out.shape, const),
                pl.BlockSpec((1, D), const), pl.BlockSpec(w1.shape, const), pl.BlockSpec(w2.shape, const)]
    args = [mix, x2d, w_out, g_mlp.reshape(1, D), w1, w2]
    if final:
        in_specs.append(pl.BlockSpec((1, D), const))
        args.append(g_final.reshape(1, D))
    return pl.pallas_call(
        functools.partial(_post_kernel, ff_chunk=ff_chunk, final=final),
        grid=(T // tm,),
        in_specs=in_specs,
        out_specs=pl.BlockSpec((tm, D), lambda i: (i, 0)),
        out_shape=jax.ShapeDtypeStruct((T, D), F32),
        compiler_params=_params(1),
        name="post_block",
    )(*args)


def _compress_kernel(c_ref, pos_ref, w1_ref, w2_ref, o_ref):
    half = w1_ref.shape[1] // 2
    n_chunks = c_ref.shape[3] // CMP_STRIDE
    first = second = None
    for l in range(CMP_STRIDE):
        tok = c_ref[0, 0, 0, pl.ds(l, n_chunks, stride=CMP_STRIDE), :].astype(BF16)
        a = _dot(tok, w1_ref[0, l * HEAD_DIM:(l + 1) * HEAD_DIM, :])
        b = _dot(tok, w1_ref[0, half + l * HEAD_DIM:half + (l + 1) * HEAD_DIM, :])
        first = a if first is None else first + a
        second = b if second is None else second + b
    pos = jnp.broadcast_to(pos_ref[0], (8, 2 * half)).astype(BF16)
    pre = first + pltpu.roll(second, n_chunks - 1, 0) + _dot(pos, w1_ref[0])[0:1]
    hid = jax.nn.gelu(pre)
    o_ref[0, 0] = _dot(hid.astype(BF16), w2_ref[0]).astype(o_ref.dtype)


def compress_kv(tokens, pos_flat, w1, w2):
    _, B, G, S, Dh = tokens.shape
    n_chunks = S // CMP_STRIDE
    hidden = w1.shape[-1]
    return pl.pallas_call(
        _compress_kernel,
        grid=(2, B * G),
        in_specs=[pl.BlockSpec((1, 1, 1, S, Dh), lambda s, i: (s, i // G, i % G, 0, 0)),
                  pl.BlockSpec((1, 1, CMP_LEN * Dh), lambda s, i: (s, 0, 0)),
                  pl.BlockSpec((1, CMP_LEN * Dh, hidden), lambda s, i: (s, 0, 0)),
                  pl.BlockSpec((1, hidden, Dh), lambda s, i: (s, 0, 0))],
        out_specs=pl.BlockSpec((1, 1, n_chunks, Dh), lambda s, i: (s, i, 0, 0)),
        out_shape=jax.ShapeDtypeStruct((2, B * G, n_chunks, Dh), BF16),
        compiler_params=_params(2),
        name="compress_kv",
    )(tokens, pos_flat, w1, w2)


def _stack_heads(q):
    return jnp.concatenate([q[:, r * HEAD_DIM:(r + 1) * HEAD_DIM] for r in range(NSA_HPG)], axis=0)


def _slope_row(slopes_ref, g, tq):
    col = lax.broadcasted_iota(jnp.int32, (1, NSA_HPG * tq), 1)
    out = jnp.zeros((1, NSA_HPG * tq), F32)
    for r in range(NSA_HPG):
        out = jnp.where(col // tq == r, slopes_ref[g * NSA_HPG + r], out)
    return out


def _cmp_select_kernel(slopes_ref, q_ref, kc_ref, vct_ref, ovt_ref, ocmp_ref, sel_ref, hits_ref,
                       keep_ref, *, tq, n_sel, blocks_per_chunk, hit_tile):
    g = pl.program_id(1)
    t0 = pl.program_id(2) * tq
    cols = NSA_HPG * tq
    n_cmp = kc_ref.shape[2]
    slope_row = _slope_row(slopes_ref, g, tq)
    q_rows = _stack_heads(q_ref[0])

    def group_sum(pc):
        out = pc[:, 0:tq]
        for r in range(1, NSA_HPG):
            out = out + pc[:, r * tq:(r + 1) * tq]
        return out

    def compressed_branch(n_rows, n_blocks):
        tpos = t0 + lax.broadcasted_iota(jnp.int32, (n_rows, cols), 1) % tq
        cmp_end = lax.broadcasted_iota(jnp.int32, (n_rows, cols), 0) * CMP_STRIDE + (CMP_LEN - 1)
        dc = (tpos - cmp_end).astype(F32)
        sc = jnp.where(dc >= 0, _dot_nt(kc_ref[0, 0, 0:n_rows, :], q_rows) - slope_row * dc, NEG_INF)
        e = jnp.exp2(sc - jnp.max(sc, axis=0, keepdims=True))
        any_valid = jnp.where(dc[0:1] >= 0, 1.0, 0.0)
        pc = e * (any_valid / jnp.sum(e, axis=0, keepdims=True))
        o_t = _dot(vct_ref[0, 0, :, 0:n_rows], pc.astype(BF16))
        for r in range(NSA_HPG):
            ocmp_ref[0, 0, r] = o_t[:, r * tq:(r + 1) * tq]
        hi, lo = _split_bf16(group_sum(pc))
        return _dot(ovt_ref[0:n_blocks, 0:n_rows], hi) + _dot(ovt_ref[0:n_blocks, 0:n_rows], lo)

    def select_blocks(imp):
        n_blocks = imp.shape[0]
        blk = lax.broadcasted_iota(jnp.int32, (n_blocks, tq), 0)
        cur = (t0 + lax.broadcasted_iota(jnp.int32, (n_blocks, tq), 1)) // SEL_LEN
        forced = (blk == 0) | (blk == cur) | (blk == cur - 1)
        imp = jnp.where(blk <= cur, imp, -1.0)
        topk = min(SEL_TOPK, n_sel)

        def pick(imp, count):
            for _ in range(count):
                best = jnp.max(imp, axis=0, keepdims=True)
                first = jnp.min(jnp.where(imp == best, blk, n_blocks), axis=0, keepdims=True)
                imp = jnp.where(blk == first, PICKED, imp)
            return imp

        imp = lax.cond(t0 >= 2 * SEL_LEN,
                       lambda: pick(jnp.where(forced, PICKED, imp), topk - 3),
                       lambda: pick(jnp.where(forced, FORCE_SCORE, imp), topk))
        return jnp.where((imp == PICKED) & (blk <= cur), 1.0, 0.0)

    needed = jnp.clip(((t0 + tq - CMP_LEN) // CMP_STRIDE) // (n_cmp // CMP_PARTS) + 1, 1, CMP_PARTS)
    for count in range(1, CMP_PARTS + 1):
        @pl.when(needed == count)
        def _():
            n_blocks = count * (n_sel // CMP_PARTS)
            keep_ref[0:n_blocks, :] = select_blocks(
                compressed_branch(count * (n_cmp // CMP_PARTS), n_blocks))
            if count < CMP_PARTS:
                keep_ref[n_blocks:n_sel, :] = jnp.zeros((n_sel - n_blocks, tq), F32)

    keep = keep_ref[...] > 0.5
    sel_ref[0, 0] = jnp.where(keep, 0.0, -SEL_DROP).T.astype(sel_ref.dtype)
    n_chunks = n_sel // blocks_per_chunk
    member = (lax.broadcasted_iota(jnp.int32, (n_chunks, n_sel), 1) // blocks_per_chunk
              == lax.broadcasted_iota(jnp.int32, (n_chunks, n_sel), 0))
    per_query = _dot(jnp.where(member, 1.0, 0.0).astype(BF16),
                     jnp.where(keep, 1.0, 0.0).astype(BF16)).astype(BF16)
    for part in range(tq // hit_tile):
        hits_ref[0, 0, part] = _dot(per_query[:, part * hit_tile:(part + 1) * hit_tile],
                                    jnp.ones((hit_tile, 128), BF16))


def cmp_select(q, kc, vc_t, overlap_t, slopes, tk, hit_tile, tq=512):
    B, S, _ = q.shape
    n_cmp = kc.shape[2]
    n_sel = S // SEL_LEN
    G = NSA_GROUPS
    smem = pl.BlockSpec(memory_space=pltpu.SMEM)
    return pl.pallas_call(
        functools.partial(_cmp_select_kernel, tq=tq, n_sel=n_sel, blocks_per_chunk=tk // SEL_LEN,
                          hit_tile=hit_tile),
        grid=(B, G, S // tq),
        in_specs=[smem,
                  pl.BlockSpec((1, tq, 256), lambda b, g, i: (b, i, g)),
                  pl.BlockSpec((1, 1, n_cmp, HEAD_DIM), lambda b, g, i: (b, g, 0, 0)),
                  pl.BlockSpec((1, 1, HEAD_DIM, n_cmp), lambda b, g, i: (b, g, 0, 0)),
                  pl.BlockSpec((n_sel, n_cmp), lambda b, g, i: (0, 0))],
        out_specs=[pl.BlockSpec((1, 1, NSA_HPG, HEAD_DIM, tq), lambda b, g, i: (b, g, 0, 0, i)),
                   pl.BlockSpec((1, 1, tq, n_sel), lambda b, g, i: (b, g, i, 0)),
                   pl.BlockSpec((1, 1, tq // hit_tile, S // tk, 128), lambda b, g, i: (b, g, i, 0, 0))],
        out_shape=[jax.ShapeDtypeStruct((B, G, NSA_HPG, HEAD_DIM, S), F32),
                   jax.ShapeDtypeStruct((B, G, S, n_sel), BF16),
                   jax.ShapeDtypeStruct((B, G, S // hit_tile, S // tk, 128), F32)],
        scratch_shapes=[pltpu.VMEM((n_sel, tq), F32)],
        compiler_params=_params(3),
        name="cmp_select",
    )(slopes, q, kc, vc_t, overlap_t)


def _sel_win_kernel(slopes_ref, active_ref, q_ref, sl_ref, ks_ref, vst_ref, kw_ref, vwt_ref, sel_ref,
                    ocmp_ref, gate_ref, o_ref, ms_ref, accs_ref, mw_ref, accw_ref, offset_ref, todo_ref,
                    knorm_ref, *, tq, tk):
    g = pl.program_id(1)
    t0 = pl.program_id(2) * tq
    cols = NSA_HPG * tq
    slope_row = _slope_row(slopes_ref, g, tq)
    q = q_ref[0]
    q_aug = jnp.concatenate(
        [jnp.concatenate([q[:, r * HEAD_DIM:(r + 1) * HEAD_DIM],
                          jnp.broadcast_to(sl_ref[0, r:r + 1, :], (tq, HEAD_DIM))], axis=1)
         for r in range(NSA_HPG)], axis=0)
    q_sel = jnp.concatenate([q_aug, jnp.concatenate([sel_ref[0, 0]] * NSA_HPG, axis=0)], axis=1)

    @pl.when(pl.program_id(2) == 0)
    def _():
        offset_ref[...] = (lax.broadcasted_iota(jnp.int32, (tk, cols), 0)
                           - lax.broadcasted_iota(jnp.int32, (tk, cols), 1) % tq)

        def body(c, best):
            rows = pl.ds(pl.multiple_of(c * tk, tk), tk)
            ks = ks_ref[0, 0, rows, :][:, :HEAD_DIM].astype(F32)
            kw = kw_ref[0, 0, rows, :][:, :HEAD_DIM].astype(F32)
            return (jnp.maximum(best[0], jnp.max(jnp.sum(ks * ks, axis=1))),
                    jnp.maximum(best[1], jnp.max(jnp.sum(kw * kw, axis=1))))

        best = lax.fori_loop(0, ks_ref.shape[2] // tk, body, (jnp.float32(0.0), jnp.float32(0.0)))
        knorm_ref[0] = best[0]
        knorm_ref[1] = best[1]

    sel_bound = _score_bound(q, knorm_ref[0])
    win_bound = _score_bound(q, knorm_ref[1])
    safe = 2.0 * jnp.maximum(sel_bound, win_bound) <= SAFE_EXP_LOG2

    def step(c, k_ref, q_rows, vt_ref, keep=None):
        inside = c >= 0
        start = pl.multiple_of(jnp.maximum(c, 0) * tk, tk)
        shift = c * tk - t0

        def scores():
            s_t = _dot_nt(k_ref[0, 0, pl.ds(start, tk), :], q_rows)
            if keep is None:
                return s_t
            lo, hi = keep
            offset = offset_ref[...]
            if hi is not None:
                return jnp.where(offset <= jnp.where(inside, hi - shift, -FAR), s_t, NEG_INF)
            return jnp.where(offset > jnp.where(inside, lo - shift, FAR), s_t, NEG_INF)

        kappa = jnp.where(inside, slope_row * shift.astype(F32), NEG_INF)
        return scores, kappa, lambda: vt_ref[0, 0, :, pl.ds(start, tk)]

    last = t0 // tk
    causal, recent, anything = (None, 0), (-WINDOW, None), (-FAR, None)
    first_steps = [step(last - 2, kw_ref, q_aug, vwt_ref, recent) + (1,),
                   step(last, ks_ref, q_sel, vst_ref, causal) + (0,),
                   step(last - 1, kw_ref, q_aug, vwt_ref, anything) + (1,),
                   step(last, kw_ref, q_aug, vwt_ref, causal) + (1,)]

    def note_active(c, n):
        hit = active_ref[0, 0, 0, 0, c] > 0

        @pl.when(hit)
        def _():
            todo_ref[n] = c

        return n + hit.astype(jnp.int32)

    n_todo = lax.fori_loop(0, last, note_active, 0)
    sel_step = lambda i: step(todo_ref[i], ks_ref, q_sel, vst_ref) + (0,)

    accs_ref[...] = jnp.zeros_like(accs_ref)
    accw_ref[...] = jnp.zeros_like(accw_ref)

    @pl.when(safe)
    def _():
        in_tile = slope_row * (lax.broadcasted_iota(jnp.int32, (1, cols), 1) % tq).astype(F32)
        block = lambda steps: _fixed_frame_block(steps, [accs_ref, accw_ref],
                                                 [sel_bound + in_tile, win_bound + in_tile])
        block(first_steps)
        _chunk_loop(n_todo, sel_step, block, (8, 4, 2, 1))

    @pl.when(jnp.logical_not(safe))
    def _():
        states = [(ms_ref, accs_ref), (mw_ref, accw_ref)]
        for m_ref, _ in states:
            m_ref[...] = jnp.full_like(m_ref, NEG_INF)
        block = lambda steps: _softmax_block(steps, states)
        block(first_steps)
        _chunk_loop(n_todo, sel_step, block, (4, 2, 1))

    o_sel = accs_ref[0:HEAD_DIM, :] / accs_ref[HEAD_DIM:HEAD_DIM + 1, :]
    o_win = accw_ref[0:HEAD_DIM, :] / accw_ref[HEAD_DIM:HEAD_DIM + 1, :]

    gates = gate_ref[0]
    outs = []
    for r in range(NSA_HPG):
        cs = slice(r * tq, (r + 1) * tq)
        outs.append(gates[3 * r:3 * r + 1] * ocmp_ref[0, 0, r] + gates[3 * r + 1:3 * r + 2] * o_sel[:, cs]
                    + gates[3 * r + 2:3 * r + 3] * o_win[:, cs])
    o_ref[0] = jnp.concatenate(outs, axis=0).T.astype(o_ref.dtype)


def sel_win_attention(q, active, slope_cols, ks, vs_t, kw, vw_t, sel_bias, o_cmp, gates_t,
                      slopes, tq=256, tk=256):
    B, S, _ = q.shape
    assert tq == tk and WINDOW == 2 * tk
    G = NSA_GROUPS
    n_sel = S // SEL_LEN
    cols = NSA_HPG * tq
    smem = pl.BlockSpec(memory_space=pltpu.SMEM)
    ks_spec = pl.BlockSpec((1, 1, S, ks.shape[-1]), lambda b, g, i: (b, g, 0, 0))
    kw_spec = pl.BlockSpec((1, 1, S, kw.shape[-1]), lambda b, g, i: (b, g, 0, 0))
    v_spec = pl.BlockSpec((1, 1, V_ROWS_64, S), lambda b, g, i: (b, g, 0, 0))
    return pl.pallas_call(
        functools.partial(_sel_win_kernel, tq=tq, tk=tk),
        grid=(B, G, S // tq),
        in_specs=[smem,
                  pl.BlockSpec((1, 1, 1, 1, S // tk), lambda b, g, i: (b, g, i, 0, 0),
                               memory_space=pltpu.SMEM),
                  pl.BlockSpec((1, tq, 256), lambda b, g, i: (b, i, g)),
                  pl.BlockSpec((1, NSA_HPG, HEAD_DIM), lambda b, g, i: (g, 0, 0)),
                  ks_spec, v_spec, kw_spec, v_spec,
                  pl.BlockSpec((1, 1, tq, n_sel), lambda b, g, i: (b, g, i, 0)),
                  pl.BlockSpec((1, 1, NSA_HPG, HEAD_DIM, tq), lambda b, g, i: (b, g, 0, 0, i)),
                  pl.BlockSpec((1, 4 * NSA_HPG, tq), lambda b, g, i: (b, g, i))],
        out_specs=pl.BlockSpec((1, tq, 256), lambda b, g, i: (b, i, g)),
        out_shape=jax.ShapeDtypeStruct((B, S, G * 256), BF16),
        scratch_shapes=[pltpu.VMEM((1, cols), F32), pltpu.VMEM((V_ROWS_64, cols), F32),
                        pltpu.VMEM((1, cols), F32), pltpu.VMEM((V_ROWS_64, cols), F32),
                        pltpu.VMEM((tk, cols), jnp.int32),
                        pltpu.SMEM((S // tk,), jnp.int32), pltpu.SMEM((2,), F32)],
        compiler_params=_params(2, 1),
        name="sel_win_attention",
    )(slopes, active, q, slope_cols, ks, vs_t, kw, vw_t, sel_bias, o_cmp, gates_t)


def _alibi_slopes_log2(n_heads):
    slopes = np.exp2(-8.0 * (np.arange(n_heads, dtype=np.float32) + 1.0) / n_heads)
    return (slopes.astype(np.float32) * np.float32(LOG2E)).astype(np.float32)


def even_layer_mix(x2d, B, S, norm_g, w_in, lam_q1, lam_k1, lam_q2, lam_k2, subln, layer):
    proj, sb_vt, df_vt = even_proj(x2d, B, S, norm_g, w_in.astype(BF16))
    proj = proj.reshape(B, S, -1)
    lam_rows = jnp.stack([lam_q1, lam_k1, lam_q2, lam_k2]).astype(F32)
    mix = diff_attention(proj, df_vt, lam_rows, subln.astype(F32), _alibi_slopes_log2(4), layer,
                         sb_attention(proj, sb_vt))
    return mix.reshape(B * S, -1)


def even_w_out_rows(w_out):
    d = w_out.shape[1]
    return w_out.reshape(2, 4, 128, d).transpose(1, 0, 2, 3).reshape(-1, d)


def odd_layer_mix(x2d, B, S, norm_g, w_in, pos_k, k_w1, k_w2, pos_v, v_w1, v_w2, tq=256, tk=256):
    G, Dh = NSA_GROUPS, HEAD_DIM
    q_width = G * NSA_HPG * Dh
    n_main = q_width + 6 * G * Dh
    per_group = NSA_HPG * N_GATES
    w_gate = w_in[:, n_main:n_main + G * per_group].reshape(-1, G, per_group)
    w_gate = jnp.pad(w_gate, ((0, 0), (0, 0), (0, 4 * NSA_HPG - per_group))).reshape(-1, 4 * NSA_HPG * G)
    w_gate = jnp.pad(w_gate, ((0, 0), (0, GATE_PAD - w_gate.shape[1])))
    w_all = jnp.concatenate([w_in[:, :n_main], w_gate], axis=1).astype(BF16)

    n_chunks = S // CMP_STRIDE
    n_sel = S // SEL_LEN
    onehot = (jnp.arange(S)[:, None] // SEL_LEN == jnp.arange(n_sel)[None, :]).astype(BF16)
    q, cmp_in, ks, kw, vs_t, vw_t, gates_t = odd_proj(x2d, B, S, norm_g, w_all,
                                                      _key_position_columns(S, tk, Dh), onehot)
    q = q.reshape(B, S, q_width)

    pos_flat = jnp.stack([pos_k, pos_v]).reshape(2, 1, CMP_LEN * Dh).astype(F32)
    w1 = jnp.stack([k_w1, v_w1]).astype(BF16)
    w2 = jnp.stack([k_w2, v_w2]).astype(BF16)
    cmp = compress_kv(cmp_in, pos_flat, w1, w2).reshape(2, B, G, n_chunks, Dh)

    cmp_start = jnp.arange(n_chunks) * CMP_STRIDE
    sel_start = jnp.arange(n_sel) * SEL_LEN
    overlap_t = ((cmp_start[None, :] < sel_start[:, None] + SEL_LEN)
                 & (sel_start[:, None] <= cmp_start[None, :] + CMP_LEN - 1)).astype(BF16)
    slopes = _alibi_slopes_log2(G * NSA_HPG)
    slope_cols = _slope_pieces(slopes, Dh).reshape(G, NSA_HPG, Dh)

    o_cmp, sel_bias, hits = cmp_select(q, cmp[0], cmp[1].transpose(0, 1, 3, 2), overlap_t, slopes, tk, tq)
    active = (hits[..., 0] > 0).astype(jnp.int32)[:, :, :, None, :]
    o = sel_win_attention(q, active, slope_cols, ks, vs_t, kw, vw_t, sel_bias, o_cmp, gates_t, slopes,
                          tq=tq, tk=tk)
    return o.reshape(B * S, q_width)


def kernel(x, attn_norm, mlp_norm, final_norm, ev_w_in, ev_lam_q1, ev_lam_k1, ev_lam_q2, ev_lam_k2,
           ev_subln, ev_w_out, od_w_in, od_cmp_pos_k, od_cmp_k_w1, od_cmp_k_w2, od_cmp_pos_v,
           od_cmp_v_w1, od_cmp_v_w2, od_w_out, mlp_w1, mlp_w2):
    B, S, D = x.shape
    depth = attn_norm.shape[0]
    x2d = x.reshape(B * S, D)
    for layer in range(depth):
        idx = layer // 2
        if layer % 2 == 0:
            mix = even_layer_mix(x2d, B, S, attn_norm[layer], ev_w_in[idx], ev_lam_q1[idx],
                                 ev_lam_k1[idx], ev_lam_q2[idx], ev_lam_k2[idx], ev_subln[idx], layer)
            w_out = even_w_out_rows(ev_w_out[idx])
        else:
            mix = odd_layer_mix(x2d, B, S, attn_norm[layer], od_w_in[idx], od_cmp_pos_k[idx],
                                od_cmp_k_w1[idx], od_cmp_k_w2[idx], od_cmp_pos_v[idx],
                                od_cmp_v_w1[idx], od_cmp_v_w2[idx])
            w_out = od_w_out[idx]
        g_final = final_norm if layer == depth - 1 else None
        x2d = post_block(mix, x2d, w_out.astype(BF16), mlp_norm[layer], mlp_w1[layer].astype(BF16),
                         mlp_w2[layer].astype(BF16), g_final)
    return x2d.reshape(B, S, D)
```

```python
import functools
import math

import jax
import jax.numpy as jnp
import numpy as np
from jax import lax
from jax.experimental import pallas as pl
from jax.experimental.pallas import tpu as pltpu

F32 = jnp.float32
BF16 = jnp.bfloat16

HEAD_DIM = 64
RMS_EPS = 1e-6
NEG_INF = -1e30
FORCE_SCORE = 1e6
NSA_GROUPS = 4
NSA_HPG = 4
CMP_LEN = 32
CMP_STRIDE = 16
SEL_LEN = 64
SEL_TOPK = 16
WINDOW = 512
N_GATES = 3
GATE_PAD = 128
SEL_DROP = 2.0 ** 24
BF16_EXACT_INT = 256
V_ROWS_64 = 80
V_ROWS_128 = 144

LOG2E = math.log2(math.e)
Q_SCALE = HEAD_DIM ** -0.5 * LOG2E
UNDERFLOW_LOG2 = -160.0
SAFE_EXP_LOG2 = 100.0
FAR = 1 << 30
PICKED = -2.0

VMEM_LIMIT = 56 * 1024 * 1024


def _params(n_parallel, n_arbitrary=0):
    return pltpu.CompilerParams(dimension_semantics=("parallel",) * n_parallel + ("arbitrary",) * n_arbitrary,
                                vmem_limit_bytes=VMEM_LIMIT)


def _rms(x, g):
    ms = jnp.mean(x * x, axis=-1, keepdims=True)
    return x * lax.rsqrt(ms + RMS_EPS) * g


def _dot(a, b):
    return jnp.dot(a, b, preferred_element_type=F32)


def _dot_nt(a, b):
    return lax.dot_general(a, b, (((1,), (1,)), ((), ())), preferred_element_type=F32)


def _split_bf16(x):
    hi = x.astype(BF16)
    lo = (x - hi.astype(F32)).astype(BF16)
    return hi, lo


def _slope_pieces(slopes, width):
    def top_bits(x):
        return (x.view(np.uint32) & np.uint32(0xFFFF0000)).view(np.float32)

    s1 = top_bits(slopes)
    r1 = slopes - s1
    s2 = top_bits(r1)
    s3 = top_bits(r1 - s2)
    out = np.zeros((slopes.shape[0], width), np.float32)
    out[:, :6] = np.stack([s1, s2, s3, s1, s2, s3], axis=1)
    return jnp.asarray(out).astype(BF16)


def _key_position_columns(n, tk, width):
    j = jnp.arange(n) % tk
    a = (j // BF16_EXACT_INT) * BF16_EXACT_INT
    b = j % BF16_EXACT_INT
    cols = jnp.stack([a, a, a, b, b, b], axis=1).astype(BF16)
    return jnp.pad(cols, ((0, 0), (0, width - cols.shape[1])))


def _softmax_block(steps, states):
    s_all = [scores() for scores, _, _, _ in steps]
    live = {k: (states[k][0][...], states[k][1][...]) for k in sorted({k for _, _, _, k in steps})}
    for (_, kap, values, k), s_t in zip(steps, s_all):
        m_run, acc = live[k]
        m_new = jnp.maximum(m_run, jnp.max(s_t, axis=0, keepdims=True) + kap)
        p = jnp.exp2(s_t - (m_new - kap)).astype(BF16)
        live[k] = (m_new, jnp.exp2(m_run - m_new) * acc + _dot(values(), p))
    for k, (m_run, acc) in live.items():
        states[k][0][...] = m_run
        states[k][1][...] = acc


def _fixed_frame_block(steps, acc_refs, frames):
    s_all = [scores() for scores, _, _, _ in steps]
    live = {k: acc_refs[k][...] for k in sorted({k for _, _, _, k in steps})}
    for (_, kap, values, k), s_t in zip(steps, s_all):
        live[k] = live[k] + _dot(values(), jnp.exp2(s_t - (frames[k] - kap)).astype(BF16))
    for k, acc in live.items():
        acc_refs[k][...] = acc


def _chunk_loop(n_steps, step, block, groups):
    done = 0
    for group in groups:
        def body(i, carry, group=group, done=done):
            block([step(done + group * i + j) for j in range(group)])
            return carry

        n_groups = (n_steps - done) // group
        lax.fori_loop(0, n_groups, body, 0)
        done = done + group * n_groups


def _score_bound(q, k_norm2):
    q32 = q.astype(F32)
    width = q.shape[1]
    member = (lax.broadcasted_iota(jnp.int32, (width, 128), 0) // HEAD_DIM
              == lax.broadcasted_iota(jnp.int32, (width, 128), 1))
    head_norm2 = _dot((q32 * q32).astype(BF16), jnp.where(member, 1.0, 0.0).astype(BF16))
    return jnp.sqrt(jnp.max(head_norm2) * k_norm2 * 1.05)


PROJ_CHUNK = 512
CMP_PARTS = 8


def _ones_row_tail(rows, width):
    return jnp.where(lax.broadcasted_iota(jnp.int32, (rows, width), 0) == 0, 1.0, 0.0).astype(BF16)


def _even_proj_kernel(x_ref, g_ref, w_ref, o_ref, sbv_ref, dfv_ref):
    tm = x_ref.shape[0]
    xn = _rms(x_ref[...], g_ref[...]).astype(BF16)
    tail = _ones_row_tail(V_ROWS_128 - 2 * HEAD_DIM, tm)
    for c in range(w_ref.shape[1] // PROJ_CHUNK):
        sl = slice(c * PROJ_CHUNK, (c + 1) * PROJ_CHUNK)
        res = _dot(xn, w_ref[:, sl])
        if c in (0, 3):
            res = res * Q_SCALE
        o_ref[:, sl] = res.astype(BF16)
        if c == 2:
            sbv_ref[0] = res.T.astype(BF16)
        if c == 5:
            v_t = res.T.astype(BF16)
            for h in range(dfv_ref.shape[1]):
                dfv_ref[0, h, 0:2 * HEAD_DIM, :] = v_t[h * 2 * HEAD_DIM:(h + 1) * 2 * HEAD_DIM]
                dfv_ref[0, h, 2 * HEAD_DIM:V_ROWS_128, :] = tail


def even_proj(x2d, B, S, g, w_bf16, tm=512):
    T, D = x2d.shape
    n_w = w_bf16.shape[1]
    per_seq = S // tm
    assert n_w == 6 * PROJ_CHUNK
    return pl.pallas_call(
        _even_proj_kernel,
        grid=(T // tm,),
        in_specs=[pl.BlockSpec((tm, D), lambda i: (i, 0)),
                  pl.BlockSpec((1, D), lambda i: (0, 0)),
                  pl.BlockSpec((D, n_w), lambda i: (0, 0))],
        out_specs=[pl.BlockSpec((tm, n_w), lambda i: (i, 0)),
                   pl.BlockSpec((1, PROJ_CHUNK, tm), lambda i: (i // per_seq, 0, i % per_seq)),
                   pl.BlockSpec((1, 4, V_ROWS_128, tm), lambda i: (i // per_seq, 0, 0, i % per_seq))],
        out_shape=[jax.ShapeDtypeStruct((T, n_w), BF16),
                   jax.ShapeDtypeStruct((B, PROJ_CHUNK, S), BF16),
                   jax.ShapeDtypeStruct((B, 4, V_ROWS_128, S), BF16)],
        compiler_params=_params(1),
        name="even_proj",
    )(x2d, g.reshape(1, D), w_bf16)


def _odd_proj_kernel(x_ref, g_ref, w_ref, pos_ref, onehot_ref,
                     q_ref, cin_ref, ks_ref, kw_ref, vst_ref, vwt_ref, gate_ref):
    tm = x_ref.shape[0]
    G, Dh = NSA_GROUPS, HEAD_DIM
    kv = G * Dh
    xn = _rms(x_ref[...], g_ref[...]).astype(BF16)
    tail = _ones_row_tail(V_ROWS_64 - Dh, tm)
    pos = pos_ref[...]
    onehot = onehot_ref[...]
    q_width = q_ref.shape[1]
    for c in range(q_width // PROJ_CHUNK):
        sl = slice(c * PROJ_CHUNK, (c + 1) * PROJ_CHUNK)
        q_ref[:, sl] = (_dot(xn, w_ref[:, sl]) * Q_SCALE).astype(BF16)

    def pair(j):
        return _dot(xn, w_ref[:, q_width + j * 2 * kv:q_width + (j + 1) * 2 * kv])

    res = pair(0)
    for which in range(2):
        for g in range(G):
            cin_ref[which, 0, g] = res[:, which * kv + g * Dh:which * kv + (g + 1) * Dh]

    for j, k_ref, vt_ref, extra in ((1, ks_ref, vst_ref, [pos, onehot]), (2, kw_ref, vwt_ref, [pos])):
        res = pair(j)
        v_t = res[:, kv:2 * kv].T.astype(BF16)
        for g in range(G):
            k_ref[0, g] = jnp.concatenate([res[:, g * Dh:(g + 1) * Dh].astype(BF16)] + extra, axis=1)
            vt_ref[0, g, 0:Dh, :] = v_t[g * Dh:(g + 1) * Dh]
            vt_ref[0, g, Dh:V_ROWS_64, :] = tail

    logits = _dot(xn, w_ref[:, q_width + 6 * kv:q_width + 6 * kv + GATE_PAD])
    gate_ref[0] = jax.nn.sigmoid(logits).T[0:gate_ref.shape[1]]


def odd_proj(x2d, B, S, g, w_bf16, pos_cols, onehot, tm=512):
    T, D = x2d.shape
    G, Dh = NSA_GROUPS, HEAD_DIM
    q_width = G * NSA_HPG * Dh
    n_sel = onehot.shape[1]
    per_seq = S // tm
    seq_tile = lambda i: (i // per_seq, 0, i % per_seq, 0)
    seq_tile_t = lambda i: (i // per_seq, 0, 0, i % per_seq)
    return pl.pallas_call(
        _odd_proj_kernel,
        grid=(T // tm,),
        in_specs=[pl.BlockSpec((tm, D), lambda i: (i, 0)),
                  pl.BlockSpec((1, D), lambda i: (0, 0)),
                  pl.BlockSpec(w_bf16.shape, lambda i: (0, 0)),
                  pl.BlockSpec((tm, Dh), lambda i: (i % per_seq, 0)),
                  pl.BlockSpec((tm, n_sel), lambda i: (i % per_seq, 0))],
        out_specs=[pl.BlockSpec((tm, q_width), lambda i: (i, 0)),
                   pl.BlockSpec((2, 1, G, tm, Dh), lambda i: (0, i // per_seq, 0, i % per_seq, 0)),
                   pl.BlockSpec((1, G, tm, 2 * Dh + n_sel), seq_tile),
                   pl.BlockSpec((1, G, tm, 2 * Dh), seq_tile),
                   pl.BlockSpec((1, G, V_ROWS_64, tm), seq_tile_t),
                   pl.BlockSpec((1, G, V_ROWS_64, tm), seq_tile_t),
                   pl.BlockSpec((1, 4 * NSA_HPG * G, tm), lambda i: (i // per_seq, 0, i % per_seq))],
        out_shape=[jax.ShapeDtypeStruct((T, q_width), BF16),
                   jax.ShapeDtypeStruct((2, B, G, S, Dh), F32),
                   jax.ShapeDtypeStruct((B, G, S, 2 * Dh + n_sel), BF16),
                   jax.ShapeDtypeStruct((B, G, S, 2 * Dh), BF16),
                   jax.ShapeDtypeStruct((B, G, V_ROWS_64, S), BF16),
                   jax.ShapeDtypeStruct((B, G, V_ROWS_64, S), BF16),
                   jax.ShapeDtypeStruct((B, 4 * NSA_HPG * G, S), F32)],
        compiler_params=_params(1),
        name="odd_proj",
    )(x2d, g.reshape(1, D), w_bf16, pos_cols, onehot)


def _head_pair_rows(q, t):
    lane = lax.broadcasted_iota(jnp.int32, (t, 2 * HEAD_DIM), 1)
    zero = jnp.zeros_like(q)
    return jnp.where(lane < HEAD_DIM, q, zero), jnp.where(lane >= HEAD_DIM, q, zero)


def _sb_kernel(q_ref, k_ref, vt_ref, o_ref, acc_ref, carry_ref, *, t):
    qi = pl.program_id(2)
    cols = 2 * t
    q_both = jnp.concatenate(_head_pair_rows(q_ref[0], t), axis=0)
    s_idx = lax.broadcasted_iota(jnp.int32, (t, 2 * t), 0)
    j_idx = lax.broadcasted_iota(jnp.int32, (t, 2 * t), 1) % t
    upper2 = jnp.where(j_idx > s_idx, 1.0, 0.0).astype(BF16)
    key = lax.broadcasted_iota(jnp.int32, (t, cols), 0)
    qry = lax.broadcasted_iota(jnp.int32, (t, cols), 1) % t
    past_diag = key < qry

    acc_ref[...] = jnp.zeros_like(acc_ref)
    carry_ref[...] = jnp.zeros_like(carry_ref)

    def chunks(specs):
        starts = [pl.multiple_of(kc * t, t) for kc, _ in specs]
        z_all = [_dot_nt(k_ref[0, pl.ds(start, t), :], q_both) for start in starts]
        carry = carry_ref[...]
        acc = acc_ref[...]
        for (_, masked), start, z in zip(specs, starts, z_all):
            drop = jnp.maximum(z, 0.0) + jnp.log2(1.0 + jnp.exp2(jnp.abs(z) * -1.0))
            log_beta = z - drop
            if masked:
                drop = jnp.where(past_diag, drop, 0.0)
            hi, lo = _split_bf16(drop)
            tail = _dot(upper2, jnp.concatenate([hi, lo], axis=0))
            w = jnp.exp2(log_beta - tail - carry)
            if masked:
                w = jnp.where(past_diag, w, 0.0)
            carry = carry + tail[0:1] + drop[0:1]
            acc = acc + _dot(vt_ref[0, :, pl.ds(start, t)], w.astype(BF16))
        carry_ref[...] = carry
        acc_ref[...] = acc
        return jnp.min(carry)

    least = lax.cond(qi > 0, lambda: chunks([(qi, True), (qi - 1, False)]), lambda: chunks([(qi, True)]))

    def cond(state):
        j, least_carry = state
        return (j < qi) & (least_carry < -UNDERFLOW_LOG2)

    def body(state):
        j, _ = state
        return j + 1, chunks([(qi - 1 - j, False)])

    lax.while_loop(cond, body, (1, least))
    row = lax.broadcasted_iota(jnp.int32, (2 * HEAD_DIM, t), 0)
    o_t = jnp.where(row < HEAD_DIM, acc_ref[:, :t], acc_ref[:, t:])
    o_ref[0] = o_t.T.astype(o_ref.dtype)


def sb_attention(proj, v_t, t=256):
    B, S, _ = proj.shape
    n_pairs = 4
    return pl.pallas_call(
        functools.partial(_sb_kernel, t=t),
        grid=(B, n_pairs, S // t),
        in_specs=[pl.BlockSpec((1, t, 128), lambda b, p, i: (b, i, p)),
                  pl.BlockSpec((1, S, 128), lambda b, p, i: (b, 0, n_pairs + p)),
                  pl.BlockSpec((1, 128, S), lambda b, p, i: (b, p, 0))],
        out_specs=pl.BlockSpec((1, t, 128), lambda b, p, i: (b, i, p)),
        out_shape=jax.ShapeDtypeStruct((B, S, n_pairs * 128), BF16),
        scratch_shapes=[pltpu.VMEM((128, 2 * t), F32), pltpu.VMEM((1, 2 * t), F32)],
        compiler_params=_params(3),
        name="sb_attention",
    )(proj, proj, v_t)


def _diff_kernel(slopes_ref, lam_ref, q_ref, k_ref, vt_ref, pos_ref, sl_ref, subln_ref, sb_ref, o_ref,
                 m_ref, acc_ref, offset_ref, knorm_ref, *, t, out_scale, lambda_init):
    h = pl.program_id(1)
    qi = pl.program_id(2)
    slope = slopes_ref[h]
    cols = 2 * t
    slope_cols = jnp.broadcast_to(sl_ref[0], (t, 128))
    q_parts = _head_pair_rows(q_ref[0], t)
    q_both = jnp.concatenate([jnp.concatenate([qc, slope_cols], axis=1) for qc in q_parts],
                             axis=0)
    lane = lax.broadcasted_iota(jnp.int32, (t, 2 * HEAD_DIM), 1)

    @pl.when(qi == 0)
    def _():
        def body(c, best):
            k = k_ref[0, pl.ds(pl.multiple_of(c * t, t), t), :].astype(F32)
            k2 = k * k
            return (jnp.maximum(best[0], jnp.max(jnp.sum(jnp.where(lane < HEAD_DIM, k2, 0.0), axis=1))),
                    jnp.maximum(best[1], jnp.max(jnp.sum(jnp.where(lane >= HEAD_DIM, k2, 0.0), axis=1))))

        best = lax.fori_loop(0, k_ref.shape[1] // t, body, (jnp.float32(0.0), jnp.float32(0.0)))
        knorm_ref[0] = best[0]
        knorm_ref[1] = best[1]
        offset_ref[...] = (lax.broadcasted_iota(jnp.int32, (t, cols), 0)
                           - lax.broadcasted_iota(jnp.int32, (t, cols), 1) % t)

    qk_bound = _score_bound(q_ref[0], jnp.maximum(knorm_ref[0], knorm_ref[1]))
    safe = 2.0 * qk_bound <= SAFE_EXP_LOG2
    col = lax.broadcasted_iota(jnp.int32, (1, cols), 1)

    pos = pos_ref[...]

    def step(kc, diagonal=False):
        start = pl.multiple_of(kc * t, t)

        def scores():
            s_t = _dot_nt(jnp.concatenate([k_ref[0, pl.ds(start, t), :], pos], axis=1), q_both)
            return jnp.where(offset_ref[...] <= 0, s_t, NEG_INF) if diagonal else s_t

        return scores, slope * ((kc - qi) * t).astype(F32), lambda: vt_ref[0, 0, :, pl.ds(start, t)], 0

    acc_ref[...] = jnp.zeros_like(acc_ref)

    @pl.when(safe)
    def _():
        frame = qk_bound + slope * (col % t).astype(F32)
        block = lambda steps: _fixed_frame_block(steps, [acc_ref], [frame])
        block([step(qi, True)])
        n_back = jnp.ceil((slope * (t - 1) - UNDERFLOW_LOG2) / (slope * t)).astype(jnp.int32)
        _chunk_loop(jnp.minimum(n_back, qi), lambda i: step(qi - 1 - i), block, (8, 4, 2, 1))

    @pl.when(jnp.logical_not(safe))
    def _():
        m_ref[...] = jnp.full_like(m_ref, NEG_INF)
        block = lambda steps: _softmax_block(steps, [(m_ref, acc_ref)])
        block([step(qi, True)])
        excess = jnp.max(qk_bound - m_ref[...]) + slope * (t - 1)
        n_back = jnp.clip(jnp.ceil((excess - UNDERFLOW_LOG2) / (slope * t)).astype(jnp.int32), 0, qi)
        _chunk_loop(n_back, lambda i: step(qi - 1 - i), block, (4, 2, 1))

    lam_terms = lam_ref[...]
    lam = (jnp.exp(jnp.sum(lam_terms[0:1] * lam_terms[1:2], axis=1, keepdims=True))
           - jnp.exp(jnp.sum(lam_terms[2:3] * lam_terms[3:4], axis=1, keepdims=True))
           + lambda_init)
    o_t = acc_ref[0:128, :] / acc_ref[128:129, :]
    o = (o_t[:, :t] - lam * o_t[:, t:]).T
    o_ref[0, :, 0:128] = sb_ref[0]
    o_ref[0, :, 128:256] = (_rms(o, subln_ref[...]) * out_scale).astype(o_ref.dtype)


def diff_attention(proj, vt_aug, lam_rows, subln, slopes, layer, o_sb, t=512):
    B, S, _ = proj.shape
    n_heads = 4
    lambda_init = 0.8 - 0.6 * math.exp(-0.3 * layer)
    smem = pl.BlockSpec(memory_space=pltpu.SMEM)
    pos = _key_position_columns(t, t, 128)
    slope_cols = _slope_pieces(slopes, 128).reshape(n_heads, 1, 128)
    return pl.pallas_call(
        functools.partial(_diff_kernel, t=t, out_scale=1.0 - lambda_init, lambda_init=lambda_init),
        grid=(B, n_heads, S // t),
        in_specs=[smem,
                  pl.BlockSpec((4, HEAD_DIM), lambda b, h, i: (0, 0)),
                  pl.BlockSpec((1, t, 128), lambda b, h, i: (b, i, 12 + h)),
                  pl.BlockSpec((1, S, 128), lambda b, h, i: (b, 0, 16 + h)),
                  pl.BlockSpec((1, 1, V_ROWS_128, S), lambda b, h, i: (b, h, 0, 0)),
                  pl.BlockSpec((t, 128), lambda b, h, i: (0, 0)),
                  pl.BlockSpec((1, 1, 128), lambda b, h, i: (h, 0, 0)),
                  pl.BlockSpec((1, 128), lambda b, h, i: (0, 0)),
                  pl.BlockSpec((1, t, 128), lambda b, h, i: (b, i, h))],
        out_specs=pl.BlockSpec((1, t, 256), lambda b, h, i: (b, i, h)),
        out_shape=jax.ShapeDtypeStruct((B, S, n_heads * 256), BF16),
        scratch_shapes=[pltpu.VMEM((1, 2 * t), F32), pltpu.VMEM((V_ROWS_128, 2 * t), F32),
                        pltpu.VMEM((t, 2 * t), jnp.int32), pltpu.SMEM((2,), F32)],
        compiler_params=_params(2, 1),
        name="diff_attention",
    )(slopes, lam_rows, proj, proj, vt_aug, pos, slope_cols, subln.reshape(1, 128), o_sb)


def _post_kernel(*refs, ff_chunk, final):
    mix_ref, x_ref, wo_ref, g_ref, w1_ref, w2_ref = refs[:6]
    gf_ref = refs[6] if final else None
    o_ref = refs[-1]
    x = x_ref[...] + _dot(mix_ref[...], wo_ref[...])
    hn = _rms(x, g_ref[...]).astype(BF16)
    acc = x
    for f in range(w1_ref.shape[1] // ff_chunk):
        sl = slice(f * ff_chunk, (f + 1) * ff_chunk)
        hid = jnp.maximum(_dot(hn, w1_ref[:, sl]), 0.0)
        acc = acc + _dot((hid * hid).astype(BF16), w2_ref[sl, :])
    if final:
        acc = _rms(acc, gf_ref[...])
    o_ref[...] = acc


def post_block(mix, x2d, w_out, g_mlp, w1, w2, g_final=None, tm=512, ff_chunk=1024):
    T, D = x2d.shape
    final = g_final is not None
    const = lambda i: (0, 0)
    in_specs = [pl.BlockSpec((tm, mix.shape[1]), lambda i: (i, 0)),
                pl.BlockSpec((tm, D), lambda i: (i, 0)),
                pl.BlockSpec(w_out.shape, const),
                pl.BlockSpec((1, D), const), pl.BlockSpec(w1.shape, const), pl.BlockSpec(w2.shape, const)]
    args = [mix, x2d, w_out, g_mlp.reshape(1, D), w1, w2]
    if final:
        in_specs.append(pl.BlockSpec((1, D), const))
        args.append(g_final.reshape(1, D))
    return pl.pallas_call(
        functools.partial(_post_kernel, ff_chunk=ff_chunk, final=final),
        grid=(T // tm,),
        in_specs=in_specs,
        out_specs=pl.BlockSpec((tm, D), lambda i: (i, 0)),
        out_shape=jax.ShapeDtypeStruct((T, D), F32),
        compiler_params=_params(1),
        name="post_block",
    )(*args)


def _compress_kernel(c_ref, pos_ref, w1_ref, w2_ref, o_ref):
    half = w1_ref.shape[1] // 2
    n_chunks = c_ref.shape[3] // CMP_STRIDE
    first = second = None
    for l in range(CMP_STRIDE):
        tok = c_ref[0, 0, 0, pl.ds(l, n_chunks, stride=CMP_STRIDE), :].astype(BF16)
        a = _dot(tok, w1_ref[0, l * HEAD_DIM:(l + 1) * HEAD_DIM, :])
        b = _dot(tok, w1_ref[0, half + l * HEAD_DIM:half + (l + 1) * HEAD_DIM, :])
        first = a if first is None else first + a
        second = b if second is None else second + b
    pos = jnp.broadcast_to(pos_ref[0], (8, 2 * half)).astype(BF16)
    pre = first + pltpu.roll(second, n_chunks - 1, 0) + _dot(pos, w1_ref[0])[0:1]
    hid = jax.nn.gelu(pre)
    o_ref[0, 0] = _dot(hid.astype(BF16), w2_ref[0]).astype(o_ref.dtype)


def compress_kv(tokens, pos_flat, w1, w2):
    _, B, G, S, Dh = tokens.shape
    n_chunks = S // CMP_STRIDE
    hidden = w1.shape[-1]
    return pl.pallas_call(
        _compress_kernel,
        grid=(2, B * G),
        in_specs=[pl.BlockSpec((1, 1, 1, S, Dh), lambda s, i: (s, i // G, i % G, 0, 0)),
                  pl.BlockSpec((1, 1, CMP_LEN * Dh), lambda s, i: (s, 0, 0)),
                  pl.BlockSpec((1, CMP_LEN * Dh, hidden), lambda s, i: (s, 0, 0)),
                  pl.BlockSpec((1, hidden, Dh), lambda s, i: (s, 0, 0))],
        out_specs=pl.BlockSpec((1, 1, n_chunks, Dh), lambda s, i: (s, i, 0, 0)),
        out_shape=jax.ShapeDtypeStruct((2, B * G, n_chunks, Dh), BF16),
        compiler_params=_params(2),
        name="compress_kv",
    )(tokens, pos_flat, w1, w2)


def _stack_heads(q):
    return jnp.concatenate([q[:, r * HEAD_DIM:(r + 1) * HEAD_DIM] for r in range(NSA_HPG)], axis=0)


def _slope_row(slopes_ref, g, tq):
    col = lax.broadcasted_iota(jnp.int32, (1, NSA_HPG * tq), 1)
    out = jnp.zeros((1, NSA_HPG * tq), F32)
    for r in range(NSA_HPG):
        out = jnp.where(col // tq == r, slopes_ref[g * NSA_HPG + r], out)
    return out


def _cmp_select_kernel(slopes_ref, q_ref, kc_ref, vct_ref, ovt_ref, ocmp_ref, sel_ref, hits_ref,
                       keep_ref, *, tq, n_sel, blocks_per_chunk, hit_tile):
    g = pl.program_id(1)
    t0 = pl.program_id(2) * tq
    cols = NSA_HPG * tq
    n_cmp = kc_ref.shape[2]
    slope_row = _slope_row(slopes_ref, g, tq)
    q_rows = _stack_heads(q_ref[0])

    def group_sum(pc):
        out = pc[:, 0:tq]
        for r in range(1, NSA_HPG):
            out = out + pc[:, r * tq:(r + 1) * tq]
        return out

    def compressed_branch(n_rows, n_blocks):
        tpos = t0 + lax.broadcasted_iota(jnp.int32, (n_rows, cols), 1) % tq
        cmp_end = lax.broadcasted_iota(jnp.int32, (n_rows, cols), 0) * CMP_STRIDE + (CMP_LEN - 1)
        dc = (tpos - cmp_end).astype(F32)
        sc = jnp.where(dc >= 0, _dot_nt(kc_ref[0, 0, 0:n_rows, :], q_rows) - slope_row * dc, NEG_INF)
        e = jnp.exp2(sc - jnp.max(sc, axis=0, keepdims=True))
        any_valid = jnp.where(dc[0:1] >= 0, 1.0, 0.0)
        pc = e * (any_valid / jnp.sum(e, axis=0, keepdims=True))
        o_t = _dot(vct_ref[0, 0, :, 0:n_rows], pc.astype(BF16))
        for r in range(NSA_HPG):
            ocmp_ref[0, 0, r] = o_t[:, r * tq:(r + 1) * tq]
        hi, lo = _split_bf16(group_sum(pc))
        return _dot(ovt_ref[0:n_blocks, 0:n_rows], hi) + _dot(ovt_ref[0:n_blocks, 0:n_rows], lo)

    def select_blocks(imp):
        n_blocks = imp.shape[0]
        blk = lax.broadcasted_iota(jnp.int32, (n_blocks, tq), 0)
        cur = (t0 + lax.broadcasted_iota(jnp.int32, (n_blocks, tq), 1)) // SEL_LEN
        forced = (blk == 0) | (blk == cur) | (blk == cur - 1)
        imp = jnp.where(blk <= cur, imp, -1.0)
        topk = min(SEL_TOPK, n_sel)

        def pick(imp, count):
            for _ in range(count):
                best = jnp.max(imp, axis=0, keepdims=True)
                first = jnp.min(jnp.where(imp == best, blk, n_blocks), axis=0, keepdims=True)
                imp = jnp.where(blk == first, PICKED, imp)
            return imp

        imp = lax.cond(t0 >= 2 * SEL_LEN,
                       lambda: pick(jnp.where(forced, PICKED, imp), topk - 3),
                       lambda: pick(jnp.where(forced, FORCE_SCORE, imp), topk))
        return jnp.where((imp == PICKED) & (blk <= cur), 1.0, 0.0)

    needed = jnp.clip(((t0 + tq - CMP_LEN) // CMP_STRIDE) // (n_cmp // CMP_PARTS) + 1, 1, CMP_PARTS)
    for count in range(1, CMP_PARTS + 1):
        @pl.when(needed == count)
        def _():
            n_blocks = count * (n_sel // CMP_PARTS)
            keep_ref[0:n_blocks, :] = select_blocks(
                compressed_branch(count * (n_cmp // CMP_PARTS), n_blocks))
            if count < CMP_PARTS:
                keep_ref[n_blocks:n_sel, :] = jnp.zeros((n_sel - n_blocks, tq), F32)

    keep = keep_ref[...] > 0.5
    sel_ref[0, 0] = jnp.where(keep, 0.0, -SEL_DROP).T.astype(sel_ref.dtype)
    n_chunks = n_sel // blocks_per_chunk
    member = (lax.broadcasted_iota(jnp.int32, (n_chunks, n_sel), 1) // blocks_per_chunk
              == lax.broadcasted_iota(jnp.int32, (n_chunks, n_sel), 0))
    per_query = _dot(jnp.where(member, 1.0, 0.0).astype(BF16),
                     jnp.where(keep, 1.0, 0.0).astype(BF16)).astype(BF16)
    for part in range(tq // hit_tile):
        hits_ref[0, 0, part] = _dot(per_query[:, part * hit_tile:(part + 1) * hit_tile],
                                    jnp.ones((hit_tile, 128), BF16))


def cmp_select(q, kc, vc_t, overlap_t, slopes, tk, hit_tile, tq=1024):
    B, S, _ = q.shape
    n_cmp = kc.shape[2]
    n_sel = S // SEL_LEN
    G = NSA_GROUPS
    smem = pl.BlockSpec(memory_space=pltpu.SMEM)
    return pl.pallas_call(
        functools.partial(_cmp_select_kernel, tq=tq, n_sel=n_sel, blocks_per_chunk=tk // SEL_LEN,
                          hit_tile=hit_tile),
        grid=(B, G, S // tq),
        in_specs=[smem,
                  pl.BlockSpec((1, tq, 256), lambda b, g, i: (b, i, g)),
                  pl.BlockSpec((1, 1, n_cmp, HEAD_DIM), lambda b, g, i: (b, g, 0, 0)),
                  pl.BlockSpec((1, 1, HEAD_DIM, n_cmp), lambda b, g, i: (b, g, 0, 0)),
                  pl.BlockSpec((n_sel, n_cmp), lambda b, g, i: (0, 0))],
        out_specs=[pl.BlockSpec((1, 1, NSA_HPG, HEAD_DIM, tq), lambda b, g, i: (b, g, 0, 0, i)),
                   pl.BlockSpec((1, 1, tq, n_sel), lambda b, g, i: (b, g, i, 0)),
                   pl.BlockSpec((1, 1, tq // hit_tile, S // tk, 128), lambda b, g, i: (b, g, i, 0, 0))],
        out_shape=[jax.ShapeDtypeStruct((B, G, NSA_HPG, HEAD_DIM, S), F32),
                   jax.ShapeDtypeStruct((B, G, S, n_sel), BF16),
                   jax.ShapeDtypeStruct((B, G, S // hit_tile, S // tk, 128), F32)],
        scratch_shapes=[pltpu.VMEM((n_sel, tq), F32)],
        compiler_params=_params(3),
        name="cmp_select",
    )(slopes, q, kc, vc_t, overlap_t)


def _sel_win_kernel(slopes_ref, active_ref, q_ref, sl_ref, ks_ref, vst_ref, kw_ref, vwt_ref, sel_ref,
                    ocmp_ref, gate_ref, o_ref, ms_ref, accs_ref, mw_ref, accw_ref, offset_ref, todo_ref,
                    knorm_ref, *, tq, tk):
    g = pl.program_id(1)
    t0 = pl.program_id(2) * tq
    cols = NSA_HPG * tq
    slope_row = _slope_row(slopes_ref, g, tq)
    q = q_ref[0]
    q_aug = jnp.concatenate(
        [jnp.concatenate([q[:, r * HEAD_DIM:(r + 1) * HEAD_DIM],
                          jnp.broadcast_to(sl_ref[0, r:r + 1, :], (tq, HEAD_DIM))], axis=1)
         for r in range(NSA_HPG)], axis=0)
    q_sel = jnp.concatenate([q_aug, jnp.concatenate([sel_ref[0, 0]] * NSA_HPG, axis=0)], axis=1)

    @pl.when(pl.program_id(2) == 0)
    def _():
        offset_ref[...] = (lax.broadcasted_iota(jnp.int32, (tk, cols), 0)
                           - lax.broadcasted_iota(jnp.int32, (tk, cols), 1) % tq)

        def body(c, best):
            rows = pl.ds(pl.multiple_of(c * tk, tk), tk)
            ks = ks_ref[0, 0, rows, :][:, :HEAD_DIM].astype(F32)
            kw = kw_ref[0, 0, rows, :][:, :HEAD_DIM].astype(F32)
            return (jnp.maximum(best[0], jnp.max(jnp.sum(ks * ks, axis=1))),
                    jnp.maximum(best[1], jnp.max(jnp.sum(kw * kw, axis=1))))

        best = lax.fori_loop(0, ks_ref.shape[2] // tk, body, (jnp.float32(0.0), jnp.float32(0.0)))
        knorm_ref[0] = best[0]
        knorm_ref[1] = best[1]

    sel_bound = _score_bound(q, knorm_ref[0])
    win_bound = _score_bound(q, knorm_ref[1])
    safe = 2.0 * jnp.maximum(sel_bound, win_bound) <= SAFE_EXP_LOG2

    def step(c, k_ref, q_rows, vt_ref, keep=None):
        inside = c >= 0
        start = pl.multiple_of(jnp.maximum(c, 0) * tk, tk)
        shift = c * tk - t0

        def scores():
            s_t = _dot_nt(k_ref[0, 0, pl.ds(start, tk), :], q_rows)
            if keep is None:
                return s_t
            lo, hi = keep
            offset = offset_ref[...]
            if hi is not None:
                return jnp.where(offset <= jnp.where(inside, hi - shift, -FAR), s_t, NEG_INF)
            return jnp.where(offset > jnp.where(inside, lo - shift, FAR), s_t, NEG_INF)

        kappa = jnp.where(inside, slope_row * shift.astype(F32), NEG_INF)
        return scores, kappa, lambda: vt_ref[0, 0, :, pl.ds(start, tk)]

    last = t0 // tk
    causal, recent, anything = (None, 0), (-WINDOW, None), (-FAR, None)
    first_steps = [step(last - 2, kw_ref, q_aug, vwt_ref, recent) + (1,),
                   step(last, ks_ref, q_sel, vst_ref, causal) + (0,),
                   step(last - 1, kw_ref, q_aug, vwt_ref, anything) + (1,),
                   step(last, kw_ref, q_aug, vwt_ref, causal) + (1,)]

    def note_active(c, n):
        hit = active_ref[0, 0, 0, 0, c] > 0

        @pl.when(hit)
        def _():
            todo_ref[n] = c

        return n + hit.astype(jnp.int32)

    n_todo = lax.fori_loop(0, last, note_active, 0)
    sel_step = lambda i: step(todo_ref[i], ks_ref, q_sel, vst_ref) + (0,)

    accs_ref[...] = jnp.zeros_like(accs_ref)
    accw_ref[...] = jnp.zeros_like(accw_ref)

    @pl.when(safe)
    def _():
        in_tile = slope_row * (lax.broadcasted_iota(jnp.int32, (1, cols), 1) % tq).astype(F32)
        block = lambda steps: _fixed_frame_block(steps, [accs_ref, accw_ref],
                                                 [sel_bound + in_tile, win_bound + in_tile])
        block(first_steps)
        _chunk_loop(n_todo, sel_step, block, (8, 4, 2, 1))

    @pl.when(jnp.logical_not(safe))
    def _():
        states = [(ms_ref, accs_ref), (mw_ref, accw_ref)]
        for m_ref, _ in states:
            m_ref[...] = jnp.full_like(m_ref, NEG_INF)
        block = lambda steps: _softmax_block(steps, states)
        block(first_steps)
        _chunk_loop(n_todo, sel_step, block, (4, 2, 1))

    o_sel = accs_ref[0:HEAD_DIM, :] / accs_ref[HEAD_DIM:HEAD_DIM + 1, :]
    o_win = accw_ref[0:HEAD_DIM, :] / accw_ref[HEAD_DIM:HEAD_DIM + 1, :]

    gates = gate_ref[0]
    outs = []
    for r in range(NSA_HPG):
        cs = slice(r * tq, (r + 1) * tq)
        outs.append(gates[3 * r:3 * r + 1] * ocmp_ref[0, 0, r] + gates[3 * r + 1:3 * r + 2] * o_sel[:, cs]
                    + gates[3 * r + 2:3 * r + 3] * o_win[:, cs])
    o_ref[0] = jnp.concatenate(outs, axis=0).T.astype(o_ref.dtype)


def sel_win_attention(q, active, slope_cols, ks, vs_t, kw, vw_t, sel_bias, o_cmp, gates_t,
                      slopes, tq=256, tk=256):
    B, S, _ = q.shape
    assert tq == tk and WINDOW == 2 * tk
    G = NSA_GROUPS
    n_sel = S // SEL_LEN
    cols = NSA_HPG * tq
    smem = pl.BlockSpec(memory_space=pltpu.SMEM)
    ks_spec = pl.BlockSpec((1, 1, S, ks.shape[-1]), lambda b, g, i: (b, g, 0, 0))
    kw_spec = pl.BlockSpec((1, 1, S, kw.shape[-1]), lambda b, g, i: (b, g, 0, 0))
    v_spec = pl.BlockSpec((1, 1, V_ROWS_64, S), lambda b, g, i: (b, g, 0, 0))
    return pl.pallas_call(
        functools.partial(_sel_win_kernel, tq=tq, tk=tk),
        grid=(B, G, S // tq),
        in_specs=[smem,
                  pl.BlockSpec((1, 1, 1, 1, S // tk), lambda b, g, i: (b, g, i, 0, 0),
                               memory_space=pltpu.SMEM),
                  pl.BlockSpec((1, tq, 256), lambda b, g, i: (b, i, g)),
                  pl.BlockSpec((1, NSA_HPG, HEAD_DIM), lambda b, g, i: (g, 0, 0)),
                  ks_spec, v_spec, kw_spec, v_spec,
                  pl.BlockSpec((1, 1, tq, n_sel), lambda b, g, i: (b, g, i, 0)),
                  pl.BlockSpec((1, 1, NSA_HPG, HEAD_DIM, tq), lambda b, g, i: (b, g, 0, 0, i)),
                  pl.BlockSpec((1, 4 * NSA_HPG, tq), lambda b, g, i: (b, g, i))],
        out_specs=pl.BlockSpec((1, tq, 256), lambda b, g, i: (b, i, g)),
        out_shape=jax.ShapeDtypeStruct((B, S, G * 256), BF16),
        scratch_shapes=[pltpu.VMEM((1, cols), F32), pltpu.VMEM((V_ROWS_64, cols), F32),
                        pltpu.VMEM((1, cols), F32), pltpu.VMEM((V_ROWS_64, cols), F32),
                        pltpu.VMEM((tk, cols), jnp.int32),
                        pltpu.SMEM((S // tk,), jnp.int32), pltpu.SMEM((2,), F32)],
        compiler_params=_params(2, 1),
        name="sel_win_attention",
    )(slopes, active, q, slope_cols, ks, vs_t, kw, vw_t, sel_bias, o_cmp, gates_t)


def _alibi_slopes_log2(n_heads):
    slopes = np.exp2(-8.0 * (np.arange(n_heads, dtype=np.float32) + 1.0) / n_heads)
    return (slopes.astype(np.float32) * np.float32(LOG2E)).astype(np.float32)


def even_layer_mix(x2d, B, S, norm_g, w_in, lam_q1, lam_k1, lam_q2, lam_k2, subln, layer):
    proj, sb_vt, df_vt = even_proj(x2d, B, S, norm_g, w_in.astype(BF16))
    proj = proj.reshape(B, S, -1)
    lam_rows = jnp.stack([lam_q1, lam_k1, lam_q2, lam_k2]).astype(F32)
    mix = diff_attention(proj, df_vt, lam_rows, subln.astype(F32), _alibi_slopes_log2(4), layer,
                         sb_attention(proj, sb_vt))
    return mix.reshape(B * S, -1)


def even_w_out_rows(w_out):
    d = w_out.shape[1]
    return w_out.reshape(2, 4, 128, d).transpose(1, 0, 2, 3).reshape(-1, d)


def odd_layer_mix(x2d, B, S, norm_g, w_in, pos_k, k_w1, k_w2, pos_v, v_w1, v_w2, tq=256, tk=256):
    G, Dh = NSA_GROUPS, HEAD_DIM
    q_width = G * NSA_HPG * Dh
    n_main = q_width + 6 * G * Dh
    per_group = NSA_HPG * N_GATES
    w_gate = w_in[:, n_main:n_main + G * per_group].reshape(-1, G, per_group)
    w_gate = jnp.pad(w_gate, ((0, 0), (0, 0), (0, 4 * NSA_HPG - per_group))).reshape(-1, 4 * NSA_HPG * G)
    w_gate = jnp.pad(w_gate, ((0, 0), (0, GATE_PAD - w_gate.shape[1])))
    w_all = jnp.concatenate([w_in[:, :n_main], w_gate], axis=1).astype(BF16)

    n_chunks = S // CMP_STRIDE
    n_sel = S // SEL_LEN
    onehot = (jnp.arange(S)[:, None] // SEL_LEN == jnp.arange(n_sel)[None, :]).astype(BF16)
    q, cmp_in, ks, kw, vs_t, vw_t, gates_t = odd_proj(x2d, B, S, norm_g, w_all,
                                                      _key_position_columns(S, tk, Dh), onehot)
    q = q.reshape(B, S, q_width)

    pos_flat = jnp.stack([pos_k, pos_v]).reshape(2, 1, CMP_LEN * Dh).astype(F32)
    w1 = jnp.stack([k_w1, v_w1]).astype(BF16)
    w2 = jnp.stack([k_w2, v_w2]).astype(BF16)
    cmp = compress_kv(cmp_in, pos_flat, w1, w2).reshape(2, B, G, n_chunks, Dh)

    cmp_start = jnp.arange(n_chunks) * CMP_STRIDE
    sel_start = jnp.arange(n_sel) * SEL_LEN
    overlap_t = ((cmp_start[None, :] < sel_start[:, None] + SEL_LEN)
                 & (sel_start[:, None] <= cmp_start[None, :] + CMP_LEN - 1)).astype(BF16)
    slopes = _alibi_slopes_log2(G * NSA_HPG)
    slope_cols = _slope_pieces(slopes, Dh).reshape(G, NSA_HPG, Dh)

    o_cmp, sel_bias, hits = cmp_select(q, cmp[0], cmp[1].transpose(0, 1, 3, 2), overlap_t, slopes, tk, tq)
    active = (hits[..., 0] > 0).astype(jnp.int32)[:, :, :, None, :]
    o = sel_win_attention(q, active, slope_cols, ks, vs_t, kw, vw_t, sel_bias, o_cmp, gates_t, slopes,
                          tq=tq, tk=tk)
    return o.reshape(B * S, q_width)


def kernel(x, attn_norm, mlp_norm, final_norm, ev_w_in, ev_lam_q1, ev_lam_k1, ev_lam_q2, ev_lam_k2,
           ev_subln, ev_w_out, od_w_in, od_cmp_pos_k, od_cmp_k_w1, od_cmp_k_w2, od_cmp_pos_v,
           od_cmp_v_w1, od_cmp_v_w2, od_w_out, mlp_w1, mlp_w2):
    B, S, D = x.shape
    depth = attn_norm.shape[0]
    x2d = x.reshape(B * S, D)
    for layer in range(depth):
        idx = layer // 2
        if layer % 2 == 0:
            mix = even_layer_mix(x2d, B, S, attn_norm[layer], ev_w_in[idx], ev_lam_q1[idx],
                                 ev_lam_k1[idx], ev_lam_q2[idx], ev_lam_k2[idx], ev_subln[idx], layer)
            w_out = even_w_out_rows(ev_w_out[idx])
        else:
            mix = odd_layer_mix(x2d, B, S, attn_norm[layer], od_w_in[idx], od_cmp_pos_k[idx],
                                od_cmp_k_w1[idx], od_cmp_k_w2[idx], od_cmp_pos_v[idx],
                                od_cmp_v_w1[idx], od_cmp_v_w2[idx])
            w_out = od_w_out[idx]
        g_final = final_norm if layer == depth - 1 else None
        x2d = post_block(mix, x2d, w_out.astype(BF16), mlp_norm[layer], mlp_w1[layer].astype(BF16),
                         mlp_w2[layer].astype(BF16), g_final)
    return x2d.reshape(B, S, D)
```
